```python
import math
import jax
import jax.numpy as jnp
from jax import lax
import numpy as np

D_MODEL = 1024
BATCH = 4
SEQ = 8192
DEPTH = 2

D_MIX = D_MODEL
DIFF_HEADS = 4
DIFF_HD = 32
DIL_HEADS = 4
DIL_HD = 64
DIL_PATTERNS = ((128, 1), (512, 4), (2048, 16))
MLA_HEADS = 4
MLA_Q_LORA = 384
MLA_KV_LORA = 128
MLA_NOPE = 64
MLA_ROPE = 32
MLA_V = 64
ROPE_THETA = 10000.0
SWA_HEADS = 4
SWA_KV_HEADS = 2
SWA_HD = 64
SWA_WINDOW = 128
Q_BLOCK = 128
BAND_BLOCK = 128
D_FF = 2816
N_EXPERTS = 8
TOP_K = 2
D_FF_EXPERT = 3584
MOE_BLOCK = 256
RMS_EPS = 1e-6
NEG_INF = -1e30
N_ALIBI = DIFF_HEADS + DIL_HEADS + SWA_HEADS

IN_WIDTHS = (
    DIFF_HEADS * 2 * DIFF_HD, DIFF_HEADS * 2 * DIFF_HD, DIFF_HEADS * 2 * DIFF_HD,
    DIL_HEADS * DIL_HD, DIL_HEADS * DIL_HD, DIL_HEADS * DIL_HD,
    MLA_Q_LORA, MLA_KV_LORA, MLA_ROPE,
    SWA_HEADS * SWA_HD, SWA_KV_HEADS * SWA_HD, SWA_KV_HEADS * SWA_HD,
)
IN_COLS = 2592

kernel_name = 'hybrid_parallel_heads_diff_dilated_mla_swa_moe'


def _rmsnorm(x, g):
    xf = x.astype(jnp.float32)
    y = xf * lax.rsqrt(jnp.mean(xf * xf, axis=-1, keepdims=True) + RMS_EPS)
    return (y * g.astype(jnp.float32)).astype(x.dtype)


def _alibi_slopes():
    s = 2.0 ** (-8.0 * (np.arange(N_ALIBI) + 1) / N_ALIBI)
    f = lambda a: jnp.asarray(a, dtype=jnp.float32)
    return f(s[0::3]), f(s[1::3]), f(s[2::3])


def _split_heads(t, n, dh):
    b, s = t.shape[:2]
    return t.reshape(b, s, n, dh).transpose(0, 2, 1, 3)


def _merge_heads(t):
    b, h, s, dh = t.shape
    return t.transpose(0, 2, 1, 3).reshape(b, s, h * dh)


def _to_query_blocks(t):
    b, h, s, dh = t.shape
    return t.reshape(b, h, s // Q_BLOCK, Q_BLOCK, dh).transpose(2, 0, 1, 3, 4)


def _from_query_blocks(t):
    nb, b, h, qb, dh = t.shape
    return t.transpose(1, 2, 0, 3, 4).reshape(b, h, nb * qb, dh)


def _causal_block_probs(q_blk, k, q_start, scale, slopes):
    s = jnp.einsum('bhqd,bhkd->bhqk', q_blk, k).astype(jnp.float32) * scale
    dist = (q_start + jnp.arange(q_blk.shape[2]))[:, None] - jnp.arange(k.shape[2])[None, :]
    if slopes is not None:
        s = s - slopes[:, None, None] * dist.astype(jnp.float32)
    s = jnp.where(dist >= 0, s, NEG_INF)
    return jax.nn.softmax(s, axis=-1)


def _diff_attention(q1, q2, k1, k2, v, lam, slopes):
    scale = DIFF_HD ** -0.5
    nb = q1.shape[2] // Q_BLOCK

    def body(args):
        qa, qb, i = args
        start = i * Q_BLOCK
        p = (_causal_block_probs(qa, k1, start, scale, slopes)
             - lam * _causal_block_probs(qb, k2, start, scale, slopes))
        return jnp.einsum('bhqk,bhkd->bhqd', p.astype(v.dtype), v)

    o = lax.map(body, (_to_query_blocks(q1), _to_query_blocks(q2), jnp.arange(nb)))
    return _from_query_blocks(o)


def _dense_causal_attention(q, k, v, scale):
    nb = q.shape[2] // Q_BLOCK

    def body(args):
        qb, i = args
        p = _causal_block_probs(qb, k, i * Q_BLOCK, scale, None)
        return jnp.einsum('bhqk,bhkd->bhqd', p.astype(v.dtype), v)

    o = lax.map(body, (_to_query_blocks(q), jnp.arange(nb)))
    return _from_query_blocks(o)


def _banded_attention(q, k, v, max_dist, dist_unit, slopes, sinks):
    bn, g, r, n, dh = q.shape
    L = BAND_BLOCK
    pad = (-n) % L
    if pad:
        q = jnp.pad(q, ((0, 0), (0, 0), (0, 0), (0, pad), (0, 0)))
        k = jnp.pad(k, ((0, 0), (0, 0), (0, pad), (0, 0)))
        v = jnp.pad(v, ((0, 0), (0, 0), (0, pad), (0, 0)))
    npad = n + pad
    nb = npad // L
    qb = q.reshape(bn, g, r, nb, L, dh)

    def band(t):
        tb = t.reshape(bn, g, nb, L, dh)
        prev = jnp.pad(tb, ((0, 0), (0, 0), (1, 0), (0, 0), (0, 0)))[:, :, :nb]
        return jnp.concatenate([prev, tb], axis=3)

    kk, vv = band(k), band(v)
    s = jnp.einsum('bgrnqd,bgnkd->bgrnqk', qb, kk).astype(jnp.float32) * (dh ** -0.5)
    dist = jnp.arange(L)[:, None] + L - jnp.arange(2 * L)[None, :]
    k_pos = jnp.arange(nb)[:, None, None] * L + jnp.arange(2 * L)[None, None, :] - L
    valid = (dist >= 0) & (dist <= max_dist) & (k_pos >= 0)
    if slopes is not None:
        s = s - slopes[None, :, :, None, None, None] * (dist * dist_unit).astype(jnp.float32)
    s = jnp.where(valid, s, NEG_INF)
    m = jnp.max(s, axis=-1)
    if sinks is not None:
        sk = sinks.astype(jnp.float32)[None, :, :, None, None]
        m = jnp.maximum(m, sk)
    p = jnp.exp(s - m[..., None])
    denom = jnp.sum(p, axis=-1)
    if sinks is not None:
        denom = denom + jnp.exp(sk - m)
    o = jnp.einsum('bgrnqk,bgnkd->bgrnqd', p.astype(vv.dtype), vv)
    o = (o.astype(jnp.float32) / denom[..., None]).astype(v.dtype)
    lse = m + jnp.log(denom)
    o = o.reshape(bn, g, r, npad, dh)[:, :, :, :n]
    lse = lse.reshape(bn, g, r, npad)[..., :n]
    return o, lse


def _to_strided(t, d):
    b, h, s, dh = t.shape
    return t.reshape(b, h, s // d, d, dh).transpose(0, 3, 1, 2, 4).reshape(b * d, h, s // d, dh)


def _from_strided(t, d, b):
    _, h, n, dh = t.shape
    return t.reshape(b, d, h, n, dh).transpose(0, 2, 3, 1, 4).reshape(b, h, n * d, dh)


def _dilated_attention(q, k, v, slopes):
    bn = q.shape[0]
    outs, lses = [], []
    for window, d in DIL_PATTERNS:
        qs, ks, vs = _to_strided(q, d), _to_strided(k, d), _to_strided(v, d)
        o, lse = _banded_attention(qs[:, :, None], ks, vs, window // d, d, slopes[:, None], None)
        outs.append(_from_strided(o[:, :, 0], d, bn))
        lses.append(_from_strided(lse[:, :, 0][..., None], d, bn)[..., 0])
    w = jax.nn.softmax(jnp.stack(lses), axis=0)
    return jnp.einsum('gbhs,gbhsd->bhsd', w.astype(q.dtype), jnp.stack(outs))


def _rope_tables(s, dtype):
    inv = ROPE_THETA ** (-jnp.arange(0, MLA_ROPE, 2, dtype=jnp.float32) / MLA_ROPE)
    ang = jnp.arange(s, dtype=jnp.float32)[:, None] * inv[None, :]
    return jnp.cos(ang).astype(dtype), jnp.sin(ang).astype(dtype)


def _rope(t, cos, sin):
    half = MLA_ROPE // 2
    t1, t2 = t[..., :half], t[..., half:]
    return jnp.concatenate([t1 * cos - t2 * sin, t2 * cos + t1 * sin], axis=-1)


def _diff_mixer(qa, ka, va, lam_params, subln, layer_idx, slopes):
    bn, s, _ = qa.shape
    q12 = qa.reshape(bn, s, DIFF_HEADS, 2, DIFF_HD).transpose(3, 0, 2, 1, 4)
    k12 = ka.reshape(bn, s, DIFF_HEADS, 2, DIFF_HD).transpose(3, 0, 2, 1, 4)
    v = _split_heads(va, DIFF_HEADS, 2 * DIFF_HD)
    lam_init = 0.8 - 0.6 * math.exp(-0.3 * layer_idx)
    lp = lam_params.astype(jnp.float32)
    lam = jnp.exp(jnp.sum(lp[0] * lp[1])) - jnp.exp(jnp.sum(lp[2] * lp[3])) + lam_init
    o = _diff_attention(q12[0], q12[1], k12[0], k12[1], v, lam, slopes)
    o = _rmsnorm(o, subln) * (1.0 - lam_init)
    return _merge_heads(o)


def _dilated_mixer(qb, kb, vb, slopes):
    q = _split_heads(qb, DIL_HEADS, DIL_HD)
    k = _split_heads(kb, DIL_HEADS, DIL_HD)
    v = _split_heads(vb, DIL_HEADS, DIL_HD)
    return _merge_heads(_dilated_attention(q, k, v, slopes))


def _mla_mixer(cq, ckv, kpe, q_norm, w_uq, kv_norm, w_ukv):
    bn, s, _ = cq.shape
    q = _split_heads(_rmsnorm(cq, q_norm) @ w_uq, MLA_HEADS, MLA_NOPE + MLA_ROPE)
    kv = _split_heads(_rmsnorm(ckv, kv_norm) @ w_ukv, MLA_HEADS, MLA_NOPE + MLA_V)
    cos, sin = _rope_tables(s, cq.dtype)
    q = jnp.concatenate([q[..., :MLA_NOPE], _rope(q[..., MLA_NOPE:], cos, sin)], axis=-1)
    k_pe = jnp.broadcast_to(_rope(kpe[:, None], cos, sin), (bn, MLA_HEADS, s, MLA_ROPE))
    k = jnp.concatenate([kv[..., :MLA_NOPE], k_pe], axis=-1)
    v = kv[..., MLA_NOPE:]
    return _merge_heads(_dense_causal_attention(q, k, v, (MLA_NOPE + MLA_ROPE) ** -0.5))


def _swa_mixer(qd, kd, vd, sinks, slopes):
    bn, s, _ = qd.shape
    rep = SWA_HEADS // SWA_KV_HEADS
    q = qd.reshape(bn, s, SWA_KV_HEADS, rep, SWA_HD).transpose(0, 2, 3, 1, 4)
    k = _split_heads(kd, SWA_KV_HEADS, SWA_HD)
    v = _split_heads(vd, SWA_KV_HEADS, SWA_HD)
    o, _ = _banded_attention(q, k, v, SWA_WINDOW - 1, 1,
                             slopes.reshape(SWA_KV_HEADS, rep), sinks.reshape(SWA_KV_HEADS, rep))
    return o.transpose(0, 3, 1, 2, 4).reshape(bn, s, SWA_HEADS * SWA_HD)


def _swiglu(h, w1, w3, w2):
    return (jax.nn.silu(h @ w1) * (h @ w3)) @ w2


def _moe_swiglu(h, router, w1, w3, w2):
    bn, s, d = h.shape
    t = bn * s
    xt = h.reshape(t, d)
    logits = (xt @ router).astype(jnp.float32)
    top_v, top_i = lax.top_k(logits, TOP_K)
    gates = jax.nn.softmax(top_v, axis=-1)
    a = t * TOP_K
    e_flat = top_i.reshape(a)
    tok_flat = jnp.repeat(jnp.arange(t, dtype=jnp.int32), TOP_K)
    g_flat = gates.reshape(a)
    order = jnp.argsort(e_flat)
    e_s, tok_s, g_s = e_flat[order], tok_flat[order], g_flat[order]
    counts = jnp.bincount(e_flat, length=N_EXPERTS)
    starts = jnp.cumsum(counts) - counts
    padded = (counts + MOE_BLOCK - 1) // MOE_BLOCK * MOE_BLOCK
    pends = jnp.cumsum(padded)
    pstarts = pends - padded
    dest = pstarts[e_s] + jnp.arange(a) - starts[e_s]
    nb = -(-a // MOE_BLOCK) + N_EXPERTS
    rows = nb * MOE_BLOCK
    row_tok = jnp.zeros((rows,), jnp.int32).at[dest].set(tok_s)
    row_gate = jnp.zeros((rows,), jnp.float32).at[dest].set(g_s)
    blk_expert = jnp.clip(jnp.searchsorted(pends, jnp.arange(nb) * MOE_BLOCK, side='right'), 0, N_EXPERTS - 1)

    def body(args):
        tok, g, e = args
        xb = xt[tok]
        y = (jax.nn.silu(xb @ w1[e]) * (xb @ w3[e])) @ w2[e]
        return y * g[:, None].astype(y.dtype)

    ys = lax.map(body, (row_tok.reshape(nb, MOE_BLOCK), row_gate.reshape(nb, MOE_BLOCK), blk_expert))
    out = jnp.zeros((t, d), h.dtype).at[row_tok].add(ys.reshape(rows, d).astype(h.dtype))
    return out.reshape(bn, s, d)


def setup_inputs(seed: int = 0) -> dict:
    key = jax.random.key(seed)
    ks = jax.random.split(key, 24)
    n_dense = (DEPTH + 1) // 2
    n_moe = DEPTH // 2
    nrm = lambda k, shape, sc: jax.random.normal(k, shape, jnp.float32) * sc
    gain = lambda k, shape: 1.0 + 0.02 * jax.random.normal(k, shape, jnp.float32)
    return {
        'x': nrm(ks[0], (BATCH, SEQ, D_MODEL), 1.0),
        'attn_norm': gain(ks[1], (DEPTH, D_MODEL)),
        'w_in': nrm(ks[2], (DEPTH, D_MODEL, IN_COLS), D_MODEL ** -0.5),
        'w_out': nrm(ks[3], (DEPTH, D_MIX, D_MODEL), D_MIX ** -0.5),
        'diff_lambda': nrm(ks[4], (DEPTH, 4, DIFF_HD), 0.1),
        'diff_subln': gain(ks[5], (DEPTH, 2 * DIFF_HD)),
        'mla_q_norm': gain(ks[6], (DEPTH, MLA_Q_LORA)),
        'mla_w_uq': nrm(ks[7], (DEPTH, MLA_Q_LORA, MLA_HEADS * (MLA_NOPE + MLA_ROPE)), MLA_Q_LORA ** -0.5),
        'mla_kv_norm': gain(ks[8], (DEPTH, MLA_KV_LORA)),
        'mla_w_ukv': nrm(ks[9], (DEPTH, MLA_KV_LORA, MLA_HEADS * (MLA_NOPE + MLA_V)), MLA_KV_LORA ** -0.5),
        'swa_sinks': nrm(ks[10], (DEPTH, SWA_HEADS), 0.5),
        'ffn_norm': gain(ks[11], (DEPTH, D_MODEL)),
        'ffn_w1': nrm(ks[12], (n_dense, D_MODEL, D_FF), D_MODEL ** -0.5),
        'ffn_w3': nrm(ks[13], (n_dense, D_MODEL, D_FF), D_MODEL ** -0.5),
        'ffn_w2': nrm(ks[14], (n_dense, D_FF, D_MODEL), D_FF ** -0.5),
        'moe_router': nrm(ks[15], (n_moe, D_MODEL, N_EXPERTS), D_MODEL ** -0.5),
        'moe_w1': nrm(ks[16], (n_moe, N_EXPERTS, D_MODEL, D_FF_EXPERT), D_MODEL ** -0.5),
        'moe_w3': nrm(ks[17], (n_moe, N_EXPERTS, D_MODEL, D_FF_EXPERT), D_MODEL ** -0.5),
        'moe_w2': nrm(ks[18], (n_moe, N_EXPERTS, D_FF_EXPERT, D_MODEL), D_FF_EXPERT ** -0.5),
        'final_norm': gain(ks[19], (D_MODEL,)),
    }


def reference(x, attn_norm, w_in, w_out, diff_lambda, diff_subln, mla_q_norm, mla_w_uq, mla_kv_norm,
              mla_w_ukv, swa_sinks, ffn_norm, ffn_w1, ffn_w3, ffn_w2, moe_router, moe_w1, moe_w3, moe_w2,
              final_norm):
    slopes_a, slopes_b, slopes_d = _alibi_slopes()
    split_points = [int(c) for c in np.cumsum(IN_WIDTHS)[:-1]]
    for l in range(DEPTH):
        h = _rmsnorm(x, attn_norm[l])
        (qa, ka, va, qb, kb, vb, cq, ckv, kpe, qd, kd, vd) = jnp.split(h @ w_in[l], split_points, axis=-1)
        o_a = _diff_mixer(qa, ka, va, diff_lambda[l], diff_subln[l], l, slopes_a)
        o_b = _dilated_mixer(qb, kb, vb, slopes_b)
        o_c = _mla_mixer(cq, ckv, kpe, mla_q_norm[l], mla_w_uq[l], mla_kv_norm[l], mla_w_ukv[l])
        o_d = _swa_mixer(qd, kd, vd, swa_sinks[l], slopes_d)
        x = x + jnp.concatenate([o_a, o_b, o_c, o_d], axis=-1) @ w_out[l]
        h2 = _rmsnorm(x, ffn_norm[l])
        if l % 2 == 0:
            j = l // 2
            x = x + _swiglu(h2, ffn_w1[j], ffn_w3[j], ffn_w2[j])
        else:
            j = l // 2
            x = x + _moe_swiglu(h2, moe_router[j], moe_w1[j], moe_w3[j], moe_w2[j])
    return _rmsnorm(x, final_norm)
```

```python
import functools
import math

import numpy as np
import jax
import jax.numpy as jnp
from jax import lax
from jax.experimental import pallas as pl
from jax.experimental.pallas import tpu as pltpu

D_MODEL = 1024
DIFF_HEADS, DIFF_HD = 4, 32
DIL_HEADS, DIL_HD = 4, 64
DIL_PATTERNS = ((128, 1), (512, 4), (2048, 16))
MLA_HEADS, MLA_Q_LORA, MLA_KV_LORA, MLA_NOPE, MLA_ROPE, MLA_V = 4, 384, 128, 64, 32, 64
ROPE_THETA = 10000.0
SWA_HEADS, SWA_KV_HEADS, SWA_HD, SWA_WINDOW = 4, 2, 64, 128
D_FF = 2816
N_EXPERTS, TOP_K, D_FF_EXPERT = 8, 2, 3584
RMS_EPS = 1e-6
NEG_INF = -1e30
N_ALIBI = DIFF_HEADS + DIL_HEADS + SWA_HEADS

LOG2E = 1.4426950408889634
LANES = 128
HEAD_LANES = 64
VMEM_LIMIT = 56 * 1024 * 1024

F32 = jnp.float32
BF16 = jnp.bfloat16

_NT = (((1,), (1,)), ((), ()))


def _params(sem, vmem=VMEM_LIMIT):
    return pltpu.CompilerParams(dimension_semantics=sem, vmem_limit_bytes=vmem)


def _alibi_slopes():
    s = 2.0 ** (-8.0 * (np.arange(N_ALIBI) + 1) / N_ALIBI)
    return s[0::3], s[1::3], s[2::3]


def _pick(n, pref):
    t = min(pref, n)
    while n % t:
        t //= 2
    return t


N_SCALED = 2048
C_CQ, C_CKV, C_KPE, C_KPE_SW, N_MAIN = 2048, 2432, 2560, 2688, 2816


def _proj_body(x_ref, g_ref, w_ref, cs_ref, qn_ref, wq_ref, kn_ref, wkv_ref,
               cosq_ref, sinq_ref, cosk_ref, sink_ref,
               a_ref, b_ref, d_ref, cq_ref, ck_ref, cv_ref):
    x = x_ref[...]
    ms = jnp.mean(x * x, axis=-1, keepdims=True)
    h = (x * lax.rsqrt(ms + RMS_EPS) * g_ref[...]).astype(BF16)
    acc = jnp.dot(h, w_ref[...], preferred_element_type=F32)
    sc = acc[:, :N_SCALED] * cs_ref[...]
    a_ref[...] = sc[:, 0:768].astype(BF16)
    b_ref[...] = sc[:, 768:1536].astype(BF16)
    d_ref[...] = sc[:, 1536:2048].astype(BF16)

    cq = acc[:, C_CQ:C_CKV]
    hq = (cq * lax.rsqrt(jnp.mean(cq * cq, axis=-1, keepdims=True) + RMS_EPS) * qn_ref[...]).astype(BF16)
    yq = jnp.dot(hq, wq_ref[...], preferred_element_type=F32)
    cosq, sinq = cosq_ref[...], sinq_ref[...]
    c_mla = (MLA_NOPE + MLA_ROPE) ** -0.5 * LOG2E
    for hd in range(MLA_HEADS):
        main = yq[:, hd * 256:(hd + 1) * 256]
        sw = yq[:, 1024 + hd * LANES:1024 + (hd + 1) * LANES]
        cq_ref[:, hd * 256:hd * 256 + LANES] = (main[:, :LANES] * cosq + sw * sinq).astype(BF16)
        cq_ref[:, hd * 256 + LANES:(hd + 1) * 256] = (main[:, LANES:] * c_mla).astype(BF16)

    ckv = acc[:, C_CKV:C_KPE]
    hk = (ckv * lax.rsqrt(jnp.mean(ckv * ckv, axis=-1, keepdims=True) + RMS_EPS) * kn_ref[...]).astype(BF16)
    kv = jnp.dot(hk, wkv_ref[...], preferred_element_type=F32)
    kr = (acc[:, C_KPE:C_KPE_SW] * cosk_ref[...] + acc[:, C_KPE_SW:N_MAIN] * sink_ref[...]).astype(BF16)
    for grp in range(2):
        ck_ref[:, grp * 256:grp * 256 + LANES] = kr
        ck_ref[:, grp * 256 + LANES:(grp + 1) * 256] = kv[:, grp * LANES:(grp + 1) * LANES].astype(BF16)
    cv_ref[...] = kv[:, 256:512].astype(BF16)


def _proj(x2d, g, w_main, colscale, qn, wq_big, kn, wkv, tabs, seq):
    t = x2d.shape[0]
    tm = _pick(seq, 512)
    nsb = seq // tm
    full = lambda shape: pl.BlockSpec(shape, lambda i: (0,) * len(shape))
    tab = pl.BlockSpec((tm, LANES), lambda i: (i % nsb, 0))
    row = lambda w: pl.BlockSpec((tm, w), lambda i: (i, 0))
    outs = [(768, BF16), (768, BF16), (512, BF16), (1024, BF16), (512, BF16), (256, BF16)]
    return pl.pallas_call(
        _proj_body,
        grid=(t // tm,),
        in_specs=[row(D_MODEL), full((1, D_MODEL)), full((D_MODEL, N_MAIN)), full((1, N_SCALED)),
                  full((1, MLA_Q_LORA)), full((MLA_Q_LORA, 1536)), full((1, MLA_KV_LORA)),
                  full((MLA_KV_LORA, 512)), tab, tab, tab, tab],
        out_specs=[row(w) for w, _ in outs],
        out_shape=[jax.ShapeDtypeStruct((t, w), dt) for w, dt in outs],
        compiler_params=_params(("parallel",)),
        name="proj",
    )(x2d, g, w_main, colscale, qn, wq_big, kn, wkv, *tabs)


def _softmax_update(i, s, v, m_ref, l_ref, acc_ref):
    m_prev = m_ref[i]
    m_new = jnp.maximum(m_prev, jnp.max(s, axis=-1, keepdims=True))
    p = jnp.exp2(s - m_new)
    alpha = jnp.exp2(m_prev - m_new)
    l_ref[i] = alpha * l_ref[i] + jnp.sum(p, axis=-1, keepdims=True)
    acc_ref[i] = alpha * acc_ref[i] + jnp.dot(p.astype(BF16), v, preferred_element_type=F32)
    m_ref[i] = m_new


def _lane_mask(shape, lo, hi):
    lane = lax.broadcasted_iota(jnp.int32, shape, len(shape) - 1)
    return jnp.logical_and(lane >= lo, lane < hi)


def _diff_body(sl_ref, q_ref, k_ref, v_ref, mask_ref, lam_ref, sub_ref, o_ref,
               qs_ref, m_ref, l_ref, acc_ref, *, tq, lam_init):
    pair, qi = pl.program_id(1), pl.program_id(2)
    tk = tq
    q = q_ref[...].astype(F32)
    for hd in range(2):
        for mp in range(2):
            lo = hd * HEAD_LANES + mp * DIFF_HD
            qs_ref[2 * hd + mp] = jnp.where(_lane_mask(q.shape, lo, lo + DIFF_HD), q, 0.0).astype(BF16)
    m_ref[...] = jnp.full(m_ref.shape, NEG_INF, F32)
    l_ref[...] = jnp.zeros(l_ref.shape, F32)
    acc_ref[...] = jnp.zeros(acc_ref.shape, F32)
    col = lax.broadcasted_iota(jnp.int32, (1, tk), 1).astype(F32)

    def step(j, masked):
        off = pl.multiple_of(j * tk, tk)
        k = k_ref[pl.ds(off, tk), :]
        v = v_ref[pl.ds(off, tk), :]
        rel = ((j - qi) * tk).astype(F32)
        for hd in range(2):
            cv = sl_ref[2 * pair + hd] * (col + rel)
            for mp in range(2):
                i = 2 * hd + mp
                s = lax.dot_general(qs_ref[i], k, _NT, preferred_element_type=F32) + cv
                if masked:
                    s = s + mask_ref[...]
                _softmax_update(i, s, v, m_ref, l_ref, acc_ref)

    def loop_body(j, carry):
        step(j, False)
        return carry

    lax.fori_loop(0, qi, loop_body, 0)
    step(qi, True)

    lp = lam_ref[...]
    lam = (jnp.exp(jnp.sum(lp[0:1, :] * lp[1:2, :], axis=-1, keepdims=True))
           - jnp.exp(jnp.sum(lp[2:3, :] * lp[3:4, :], axis=-1, keepdims=True)) + lam_init)
    in_h0 = _lane_mask((tq, LANES), 0, HEAD_LANES)
    o_h = [acc_ref[2 * hd] / l_ref[2 * hd] - lam * (acc_ref[2 * hd + 1] / l_ref[2 * hd + 1]) for hd in range(2)]
    o = jnp.where(in_h0, o_h[0], o_h[1])
    sq = o * o
    ms0 = jnp.sum(jnp.where(in_h0, sq, 0.0), axis=-1, keepdims=True)
    ms1 = jnp.sum(jnp.where(in_h0, 0.0, sq), axis=-1, keepdims=True)
    ms = jnp.where(in_h0, ms0, ms1) * (1.0 / HEAD_LANES)
    y = o * lax.rsqrt(ms + RMS_EPS) * sub_ref[...]
    o_ref[...] = (y * (1.0 - lam_init)).astype(o_ref.dtype)


def _diff_attention(a3, slopes_l2, mask, lam_p, sub128, layer_idx, tq):
    b, s, _ = a3.shape
    nq = s // tq
    lam_init = 0.8 - 0.6 * math.exp(-0.3 * layer_idx)
    body = functools.partial(_diff_body, tq=tq, lam_init=lam_init)
    grid_spec = pltpu.PrefetchScalarGridSpec(
        num_scalar_prefetch=1,
        grid=(b, 2, nq),
        in_specs=[
            pl.BlockSpec((None, tq, LANES), lambda bi, p, qi, sl: (bi, qi, p)),
            pl.BlockSpec((None, s, LANES), lambda bi, p, qi, sl: (bi, 0, 2 + p)),
            pl.BlockSpec((None, s, LANES), lambda bi, p, qi, sl: (bi, 0, 4 + p)),
            pl.BlockSpec((tq, tq), lambda bi, p, qi, sl: (0, 0)),
            pl.BlockSpec((4, DIFF_HD), lambda bi, p, qi, sl: (0, 0)),
            pl.BlockSpec((1, LANES), lambda bi, p, qi, sl: (0, 0)),
        ],
        out_specs=pl.BlockSpec((None, tq, LANES), lambda bi, p, qi, sl: (bi, qi, p)),
        scratch_shapes=[pltpu.VMEM((4, tq, LANES), BF16), pltpu.VMEM((4, tq, 1), F32),
                        pltpu.VMEM((4, tq, 1), F32), pltpu.VMEM((4, tq, LANES), F32)],
    )
    return pl.pallas_call(
        body, grid_spec=grid_spec,
        out_shape=jax.ShapeDtypeStruct((b, s, 256), BF16),
        compiler_params=_params(("parallel", "parallel", "arbitrary")),
        name="diff_attn",
    )(slopes_l2, a3, a3, a3, mask, lam_p, sub128)


def _mla_body(q0_ref, q1_ref, k_ref, v_ref, mask_ref, o_ref, m_ref, l_ref, acc_ref, *, tq):
    qi = pl.program_id(2)
    tk = tq
    m_ref[...] = jnp.full(m_ref.shape, NEG_INF, F32)
    l_ref[...] = jnp.zeros(l_ref.shape, F32)
    acc_ref[...] = jnp.zeros(acc_ref.shape, F32)
    q_refs = (q0_ref, q1_ref)

    def step(j, masked):
        off = pl.multiple_of(j * tk, tk)
        k = k_ref[pl.ds(off, tk), :]
        v = v_ref[pl.ds(off, tk), :]
        for i in range(2):
            s = lax.dot_general(q_refs[i][...], k, _NT, preferred_element_type=F32)
            if masked:
                s = s + mask_ref[...]
            _softmax_update(i, s, v, m_ref, l_ref, acc_ref)

    def loop_body(j, carry):
        step(j, False)
        return carry

    lax.fori_loop(0, qi, loop_body, 0)
    step(qi, True)
    in_h0 = _lane_mask((tq, LANES), 0, HEAD_LANES)
    o_ref[...] = jnp.where(in_h0, acc_ref[0] / l_ref[0], acc_ref[1] / l_ref[1]).astype(o_ref.dtype)


def _mla_attention(cq3, ck3, cv3, mask, tq):
    b, s, _ = cq3.shape
    nq = s // tq
    return pl.pallas_call(
        functools.partial(_mla_body, tq=tq),
        grid=(b, 2, nq),
        in_specs=[
            pl.BlockSpec((None, tq, 256), lambda bi, g, qi: (bi, qi, 2 * g)),
            pl.BlockSpec((None, tq, 256), lambda bi, g, qi: (bi, qi, 2 * g + 1)),
            pl.BlockSpec((None, s, 256), lambda bi, g, qi: (bi, 0, g)),
            pl.BlockSpec((None, s, LANES), lambda bi, g, qi: (bi, 0, g)),
            pl.BlockSpec((tq, tq), lambda bi, g, qi: (0, 0)),
        ],
        out_specs=pl.BlockSpec((None, tq, LANES), lambda bi, g, qi: (bi, qi, g)),
        out_shape=jax.ShapeDtypeStruct((b, s, 256), BF16),
        scratch_shapes=[pltpu.VMEM((2, tq, 1), F32), pltpu.VMEM((2, tq, 1), F32),
                        pltpu.VMEM((2, tq, LANES), F32)],
        compiler_params=_params(("parallel", "parallel", "arbitrary")),
        name="mla_attn",
    )(cq3, cq3, ck3, cv3, mask)


def _band_body(m0_ref, q_ref, k_ref, v_ref, bias_ref, o_ref, qs_ref, m_ref, l_ref, acc_ref,
               *, tq, n_off, l_init):
    slab, qi = pl.program_id(1), pl.program_id(2)
    tk = tq
    q = q_ref[...].astype(F32)
    for i in range(2):
        qs_ref[i] = jnp.where(_lane_mask(q.shape, i * HEAD_LANES, (i + 1) * HEAD_LANES), q, 0.0).astype(BF16)
        m_ref[i] = jnp.full((tq, 1), m0_ref[2 * slab + i], F32)
    l_ref[...] = jnp.full(l_ref.shape, l_init, F32)
    acc_ref[...] = jnp.zeros(acc_ref.shape, F32)

    def loop_body(o, carry):
        off = pl.multiple_of((qi - o) * tk, tk)
        k = k_ref[pl.ds(off, tk), :]
        v = v_ref[pl.ds(off, tk), :]
        for i in range(2):
            s = lax.dot_general(qs_ref[i], k, _NT, preferred_element_type=F32) + bias_ref[i, o]
            _softmax_update(i, s, v, m_ref, l_ref, acc_ref)
        return carry

    lax.fori_loop(0, jnp.minimum(qi, n_off - 1) + 1, loop_body, 0)
    in_h0 = _lane_mask((tq, LANES), 0, HEAD_LANES)
    o_ref[...] = jnp.where(in_h0, acc_ref[0] / l_ref[0], acc_ref[1] / l_ref[1]).astype(o_ref.dtype)


def _band_attention(x3, q_blk0, k_blk_of, v_blk_of, bias, m0, l_init, tq, name):
    b, s, _ = x3.shape
    nq = s // tq
    n_off = bias.shape[2]
    grid_spec = pltpu.PrefetchScalarGridSpec(
        num_scalar_prefetch=1,
        grid=(b, 2, nq),
        in_specs=[
            pl.BlockSpec((None, tq, LANES), lambda bi, p, qi, m0r: (bi, qi, q_blk0 + p)),
            pl.BlockSpec((None, s, LANES), lambda bi, p, qi, m0r: (bi, 0, k_blk_of(p))),
            pl.BlockSpec((None, s, LANES), lambda bi, p, qi, m0r: (bi, 0, v_blk_of(p))),
            pl.BlockSpec((None, 2, n_off, tq, tq), lambda bi, p, qi, m0r: (p, 0, 0, 0, 0)),
        ],
        out_specs=pl.BlockSpec((None, tq, LANES), lambda bi, p, qi, m0r: (bi, qi, p)),
        scratch_shapes=[pltpu.VMEM((2, tq, LANES), BF16), pltpu.VMEM((2, tq, 1), F32),
                        pltpu.VMEM((2, tq, 1), F32), pltpu.VMEM((2, tq, LANES), F32)],
    )
    return pl.pallas_call(
        functools.partial(_band_body, tq=tq, n_off=n_off, l_init=l_init),
        grid_spec=grid_spec,
        out_shape=jax.ShapeDtypeStruct((b, s, 256), BF16),
        compiler_params=_params(("parallel", "parallel", "arbitrary")),
        name=name,
    )(m0, x3, x3, x3, bias)


def _band_bias(slopes, tq, n_off, mult_fn):
    di = jnp.arange(tq, dtype=jnp.int32)[:, None]
    dj = jnp.arange(tq, dtype=jnp.int32)[None, :]
    off = jnp.arange(n_off, dtype=jnp.int32)[:, None, None]
    delta = off * tq + di - dj
    mult = mult_fn(delta)
    sl = jnp.asarray(slopes, F32)[:, None, None, None] * LOG2E
    val = -sl * delta.astype(F32)[None] + jnp.log2(jnp.maximum(mult, 1).astype(F32))[None]
    return jnp.where((mult > 0)[None], val, NEG_INF)


def _dil_mult(delta):
    m = jnp.zeros(delta.shape, jnp.int32)
    for window, d in DIL_PATTERNS:
        m = m + ((delta >= 0) & (delta <= window) & (delta % d == 0)).astype(jnp.int32)
    return m


def _swa_mult(delta):
    return ((delta >= 0) & (delta <= SWA_WINDOW - 1)).astype(jnp.int32)


def _outproj_body(*refs, with_router):
    if with_router:
        (oa, ob, oc, od, w_ref, x_ref, g_ref, rh_ref, rl_ref, x1_ref, h2_ref, lg_ref) = refs
    else:
        (oa, ob, oc, od, w_ref, x_ref, g_ref, x1_ref, h2_ref) = refs
    acc = x_ref[...]
    for i, o in enumerate((oa, ob, oc, od)):
        acc = acc + jnp.dot(o[...], w_ref[i], preferred_element_type=F32)
    x1_ref[...] = acc
    h2 = acc * lax.rsqrt(jnp.mean(acc * acc, axis=-1, keepdims=True) + RMS_EPS) * g_ref[...]
    hi = h2.astype(BF16)
    h2_ref[...] = hi
    if with_router:
        lo = (h2 - hi.astype(F32)).astype(BF16)
        lg_ref[...] = (jnp.dot(hi, rh_ref[...], preferred_element_type=F32)
                       + (jnp.dot(hi, rl_ref[...], preferred_element_type=F32)
                          + jnp.dot(lo, rh_ref[...], preferred_element_type=F32)))


def _outproj(oa, ob, oc, od, w4, x2d, g, router_hl=None):
    t = x2d.shape[0]
    tm = _pick(t, 512)
    with_router = router_hl is not None
    full = lambda shape: pl.BlockSpec(shape, lambda i: (0,) * len(shape))
    row = lambda w: pl.BlockSpec((tm, w), lambda i: (i, 0))
    in_specs = [row(256)] * 4 + [full((4, 256, D_MODEL)), row(D_MODEL), full((1, D_MODEL))]
    args = [oa, ob, oc, od, w4, x2d, g]
    out_specs = [row(D_MODEL), row(D_MODEL)]
    out_shape = [jax.ShapeDtypeStruct((t, D_MODEL), F32), jax.ShapeDtypeStruct((t, D_MODEL), BF16)]
    if with_router:
        in_specs += [full((D_MODEL, LANES))] * 2
        args += list(router_hl)
        out_specs.append(row(LANES))
        out_shape.append(jax.ShapeDtypeStruct((t, LANES), F32))
    return pl.pallas_call(
        functools.partial(_outproj_body, with_router=with_router),
        grid=(t // tm,), in_specs=in_specs, out_specs=out_specs, out_shape=out_shape,
        compiler_params=_params(("parallel",)),
        name="outproj",
    )(*args)


def _ffn_body(h_ref, w1_ref, w3_ref, w2_ref, x_ref, o_ref, acc_ref):
    f = pl.program_id(1)
    h = h_ref[...]
    a = jnp.dot(h, w1_ref[...], preferred_element_type=F32)
    b = jnp.dot(h, w3_ref[...], preferred_element_type=F32)
    mid = (a * jax.nn.sigmoid(a) * b).astype(BF16)
    contrib = jnp.dot(mid, w2_ref[...], preferred_element_type=F32)

    @pl.when(f == 0)
    def _():
        acc_ref[...] = contrib

    @pl.when(f > 0)
    def _():
        acc_ref[...] += contrib

    @pl.when(f == pl.num_programs(1) - 1)
    def _():
        o_ref[...] = x_ref[...] + acc_ref[...]


def _ffn(h2, w1, w3, w2, x1):
    t = h2.shape[0]
    tm = _pick(t, 512)
    tf = D_FF // 2
    return pl.pallas_call(
        _ffn_body,
        grid=(t // tm, D_FF // tf),
        in_specs=[pl.BlockSpec((tm, D_MODEL), lambda i, f: (i, 0)),
                  pl.BlockSpec((D_MODEL, tf), lambda i, f: (0, f)),
                  pl.BlockSpec((D_MODEL, tf), lambda i, f: (0, f)),
                  pl.BlockSpec((tf, D_MODEL), lambda i, f: (f, 0)),
                  pl.BlockSpec((tm, D_MODEL), lambda i, f: (i, 0))],
        out_specs=pl.BlockSpec((tm, D_MODEL), lambda i, f: (i, 0)),
        out_shape=jax.ShapeDtypeStruct((t, D_MODEL), F32),
        scratch_shapes=[pltpu.VMEM((tm, D_MODEL), F32)],
        compiler_params=_params(("parallel", "arbitrary")),
        name="ffn",
    )(h2, w1, w3, w2, x1)


def _route_body(lg_ref, tri_ref, info_ref, cnt_ref, carry_ref):
    i = pl.program_id(0)

    @pl.when(i == 0)
    def _():
        carry_ref[...] = jnp.zeros(carry_ref.shape, F32)

    lg = lg_ref[...]
    lane = lax.broadcasted_iota(jnp.int32, lg.shape, 1)
    lg = jnp.where(lane < N_EXPERTS, lg, -jnp.inf)
    m1 = jnp.max(lg, axis=-1, keepdims=True)
    i1 = jnp.min(jnp.where(lg == m1, lane, LANES), axis=-1, keepdims=True)
    oh1 = lane == i1
    lg2 = jnp.where(oh1, -jnp.inf, lg)
    m2 = jnp.max(lg2, axis=-1, keepdims=True)
    i2 = jnp.min(jnp.where(lg2 == m2, lane, LANES), axis=-1, keepdims=True)
    oh2 = lane == i2
    e2 = jnp.exp(m2 - m1)
    g1 = 1.0 / (1.0 + e2)
    g2 = e2 / (1.0 + e2)
    oh = jnp.where(jnp.logical_or(oh1, oh2), 1.0, 0.0)
    rank = jnp.dot(tri_ref[...], oh.astype(BF16), preferred_element_type=F32) + carry_ref[...]
    r1 = jnp.sum(jnp.where(oh1, rank, 0.0), axis=-1, keepdims=True)
    r2 = jnp.sum(jnp.where(oh2, rank, 0.0), axis=-1, keepdims=True)
    carry_ref[...] += jnp.sum(oh, axis=0, keepdims=True)
    cnt_ref[...] = carry_ref[...]
    info = jnp.zeros(lg.shape, F32)
    for c, val in enumerate((i1.astype(F32), i2.astype(F32), r1, r2, g1, g2)):
        info = jnp.where(lane == c, val, info)
    info_ref[...] = info


def _route(logits):
    t = logits.shape[0]
    tm = _pick(t, 512)
    tri = (jnp.arange(tm)[:, None] > jnp.arange(tm)[None, :]).astype(BF16)
    return pl.pallas_call(
        _route_body,
        grid=(t // tm,),
        in_specs=[pl.BlockSpec((tm, LANES), lambda i: (i, 0)), pl.BlockSpec((tm, tm), lambda i: (0, 0))],
        out_specs=[pl.BlockSpec((tm, LANES), lambda i: (i, 0)), pl.BlockSpec((1, LANES), lambda i: (0, 0))],
        out_shape=[jax.ShapeDtypeStruct((t, LANES), F32), jax.ShapeDtypeStruct((1, LANES), F32)],
        scratch_shapes=[pltpu.VMEM((1, LANES), F32)],
        compiler_params=_params(("arbitrary",)),
        name="route",
    )(logits, tri)


def _moe_ffn_body(be_ref, x_ref, w1_ref, w3_ref, w2_ref, o_ref, acc_ref):
    f = pl.program_id(1)
    x = x_ref[...]
    a = jnp.dot(x, w1_ref[...], preferred_element_type=F32)
    b = jnp.dot(x, w3_ref[...], preferred_element_type=F32)
    mid = (a * jax.nn.sigmoid(a) * b).astype(BF16)
    contrib = jnp.dot(mid, w2_ref[...], preferred_element_type=F32)

    @pl.when(f == 0)
    def _():
        acc_ref[...] = contrib

    @pl.when(f > 0)
    def _():
        acc_ref[...] += contrib

    @pl.when(f == pl.num_programs(1) - 1)
    def _():
        o_ref[...] = acc_ref[...].astype(o_ref.dtype)


def _moe_ffn(xs, blk_expert, w1, w3, w2, bm):
    rows = xs.shape[0]
    tf = D_FF_EXPERT // 4
    grid_spec = pltpu.PrefetchScalarGridSpec(
        num_scalar_prefetch=1,
        grid=(rows // bm, D_FF_EXPERT // tf),
        in_specs=[pl.BlockSpec((bm, D_MODEL), lambda i, f, be: (i, 0)),
                  pl.BlockSpec((None, D_MODEL, tf), lambda i, f, be: (be[i], 0, f)),
                  pl.BlockSpec((None, D_MODEL, tf), lambda i, f, be: (be[i], 0, f)),
                  pl.BlockSpec((None, tf, D_MODEL), lambda i, f, be: (be[i], f, 0))],
        out_specs=pl.BlockSpec((bm, D_MODEL), lambda i, f, be: (i, 0)),
        scratch_shapes=[pltpu.VMEM((bm, D_MODEL), F32)],
    )
    return pl.pallas_call(
        _moe_ffn_body, grid_spec=grid_spec,
        out_shape=jax.ShapeDtypeStruct((rows, D_MODEL), F32),
        compiler_params=_params(("parallel", "arbitrary")),
        name="moe_ffn",
    )(blk_expert, xs, w1, w3, w2)


def _final_body(*refs, with_moe):
    if with_moe:
        x_ref, y0_ref, y1_ref, info_ref, g_ref, o_ref = refs
        info = info_ref[...]
        x = x_ref[...] + (info[:, 4:5] * y0_ref[...] + info[:, 5:6] * y1_ref[...])
    else:
        x_ref, g_ref, o_ref = refs
        x = x_ref[...]
    o_ref[...] = x * lax.rsqrt(jnp.mean(x * x, axis=-1, keepdims=True) + RMS_EPS) * g_ref[...]


def _final(x2d, g, moe=None):
    t = x2d.shape[0]
    tm = _pick(t, 512)
    row = lambda w: pl.BlockSpec((tm, w), lambda i: (i, 0))
    in_specs, args = [row(D_MODEL)], [x2d]
    if moe is not None:
        in_specs += [row(D_MODEL), row(D_MODEL), row(LANES)]
        args += list(moe)
    in_specs.append(pl.BlockSpec((1, D_MODEL), lambda i: (0, 0)))
    args.append(g)
    return pl.pallas_call(
        functools.partial(_final_body, with_moe=moe is not None),
        grid=(t // tm,), in_specs=in_specs, out_specs=row(D_MODEL),
        out_shape=jax.ShapeDtypeStruct((t, D_MODEL), F32),
        compiler_params=_params(("parallel",)),
        name="final",
    )(*args)


def _prep_attn_weights(w_in, w_out, w_uq, w_ukv):
    widths = (256, 256, 256, 256, 256, 256, MLA_Q_LORA, MLA_KV_LORA, MLA_ROPE, 256, 128, 128)
    offs = np.concatenate([[0], np.cumsum(widths)])
    qa, ka, va, qb, kb, vb, cq, ckv, kpe, qd, kd, vd = [w_in[:, offs[i]:offs[i + 1]] for i in range(12)]
    hperm = np.array([0, 2, 1, 3])
    qd = qd.reshape(D_MODEL, 4, SWA_HD)[:, hperm].reshape(D_MODEL, 256)
    pad = jnp.zeros((D_MODEL, LANES - MLA_ROPE), w_in.dtype)
    half = MLA_ROPE // 2
    kpe_sw = jnp.concatenate([kpe[:, half:], kpe[:, :half]], axis=1)
    w_main = jnp.concatenate([qa, ka, va, qb, kb, vb, qd, kd, vd, cq, ckv, kpe, pad, kpe_sw, pad], axis=1)

    uq = w_uq.reshape(MLA_Q_LORA, MLA_HEADS, MLA_NOPE + MLA_ROPE)
    z = lambda n: jnp.zeros((MLA_Q_LORA, n), w_uq.dtype)
    mains, sws = [], []
    for hd in range(MLA_HEADS):
        nope, rope = uq[:, hd, :MLA_NOPE], uq[:, hd, MLA_NOPE:]
        nope128 = jnp.concatenate([nope, z(64)] if hd % 2 == 0 else [z(64), nope], axis=1)
        mains.append(jnp.concatenate([rope, z(LANES - MLA_ROPE), nope128], axis=1))
        sws.append(jnp.concatenate([rope[:, half:], rope[:, :half], z(LANES - MLA_ROPE)], axis=1))
    wq_big = jnp.concatenate(mains + sws, axis=1)

    ukv = w_ukv.reshape(MLA_KV_LORA, MLA_HEADS, MLA_NOPE + MLA_V)
    wkv = jnp.concatenate([ukv[:, :, :MLA_NOPE].reshape(MLA_KV_LORA, 256),
                           ukv[:, :, MLA_NOPE:].reshape(MLA_KV_LORA, 256)], axis=1)

    wo = w_out.reshape(4, 256, D_MODEL)
    wo_d = wo[3].reshape(4, SWA_HD, D_MODEL)[hperm].reshape(256, D_MODEL)
    w4 = jnp.stack([wo[0], wo[1], wo[2], wo_d])
    return w_main.astype(BF16), wq_big.astype(BF16), wkv.astype(BF16), w4.astype(BF16)


def _rope_tables(s):
    inv = ROPE_THETA ** (-jnp.arange(0, MLA_ROPE, 2, dtype=F32) / MLA_ROPE)
    ang = jnp.arange(s, dtype=F32)[:, None] * inv[None, :]
    cos, sin = jnp.cos(ang), jnp.sin(ang)
    z = jnp.zeros((s, LANES - MLA_ROPE), F32)
    cos128 = jnp.concatenate([cos, cos, z], axis=1)
    sin128 = jnp.concatenate([-sin, sin, z], axis=1)
    c_mla = (MLA_NOPE + MLA_ROPE) ** -0.5 * LOG2E
    return cos128 * c_mla, sin128 * c_mla, cos128, sin128


def _col_scale():
    ca = DIFF_HD ** -0.5 * LOG2E
    cb = DIL_HD ** -0.5 * LOG2E
    cd = SWA_HD ** -0.5 * LOG2E
    v = np.ones((1, N_SCALED), np.float32)
    v[0, 0:256] = ca
    v[0, 768:1024] = cb
    v[0, 1536:1792] = cd
    return jnp.asarray(v)


def _moe_layer(x1, h2, logits, w1, w3, w2, bm):
    t = h2.shape[0]
    info, counts = _route(logits)
    cnt = counts[0, :N_EXPERTS].astype(jnp.int32)
    padded = (cnt + bm - 1) // bm * bm
    pends = jnp.cumsum(padded)
    pstarts = pends - padded
    e12 = info[:, 0:2].astype(jnp.int32)
    pos = pstarts[e12] + info[:, 2:4].astype(jnp.int32)
    nb = (t * TOP_K) // bm + N_EXPERTS
    rows = nb * bm
    blk_expert = jnp.clip(jnp.searchsorted(pends, jnp.arange(nb) * bm, side='right'), 0, N_EXPERTS - 1)
    row_tok = jnp.zeros((rows,), jnp.int32).at[pos.reshape(-1)].set(
        jnp.repeat(jnp.arange(t, dtype=jnp.int32), TOP_K))
    xs = h2[row_tok]
    ys = _moe_ffn(xs, blk_expert.astype(jnp.int32), w1, w3, w2, bm)
    return ys[pos[:, 0]], ys[pos[:, 1]], info


def kernel(x, attn_norm, w_in, w_out, diff_lambda, diff_subln, mla_q_norm, mla_w_uq, mla_kv_norm, mla_w_ukv,
           swa_sinks, ffn_norm, ffn_w1, ffn_w3, ffn_w2, moe_router, moe_w1, moe_w3, moe_w2, final_norm):
    bsz, seq, _ = x.shape
    t = bsz * seq
    depth = w_in.shape[0]
    slopes_a, slopes_b, slopes_d = _alibi_slopes()
    tq_dense = _pick(seq, 512)
    tq_band = _pick(seq, 256)

    tabs = _rope_tables(seq)
    colscale = _col_scale()
    di = jnp.arange(tq_dense)
    causal_mask = jnp.where(di[None, :] <= di[:, None], 0.0, NEG_INF).astype(F32)
    n_off_dil = min(max(w for w, _ in DIL_PATTERNS) // tq_band + 1, seq // tq_band)
    n_off_swa = min((SWA_WINDOW - 1 + tq_band - 1) // tq_band + 1, seq // tq_band)
    dil_bias = _band_bias(slopes_b, tq_band, n_off_dil, _dil_mult).reshape(2, 2, n_off_dil, tq_band, tq_band)
    swa_bias = _band_bias(slopes_d[np.array([0, 2, 1, 3])], tq_band, n_off_swa, _swa_mult
                          ).reshape(2, 2, n_off_swa, tq_band, tq_band)
    slopes_a_l2 = jnp.asarray(slopes_a * LOG2E, F32)
    dil_m0 = jnp.full((4,), NEG_INF, F32)

    x2d = x.reshape(t, D_MODEL)
    for l in range(depth):
        w_main, wq_big, wkv, w4 = _prep_attn_weights(w_in[l], w_out[l], mla_w_uq[l], mla_w_ukv[l])
        a, b, d, cq, ck, cv = _proj(x2d, attn_norm[l][None], w_main, colscale, mla_q_norm[l][None], wq_big,
                                    mla_kv_norm[l][None], wkv, tabs, seq)
        a3, b3, d3 = (v.reshape(bsz, seq, -1) for v in (a, b, d))
        sub128 = jnp.concatenate([diff_subln[l], diff_subln[l]])[None]
        o_a = _diff_attention(a3, slopes_a_l2, causal_mask, diff_lambda[l], sub128, l, tq_dense)
        o_b = _band_attention(b3, 0, lambda p: 2 + p, lambda p: 4 + p, dil_bias, dil_m0, 0.0, tq_band, "dil_attn")
        o_c = _mla_attention(cq.reshape(bsz, seq, -1), ck.reshape(bsz, seq, -1), cv.reshape(bsz, seq, -1),
                             causal_mask, tq_dense)
        swa_m0 = (swa_sinks[l].astype(F32) * LOG2E)[np.array([0, 2, 1, 3])]
        o_d = _band_attention(d3, 0, lambda p: 2, lambda p: 3, swa_bias, swa_m0, 1.0, tq_band, "swa_attn")
        outs = [v.reshape(t, 256) for v in (o_a, o_b, o_c, o_d)]
        j = l // 2
        if l % 2 == 0:
            x1, h2 = _outproj(*outs, w4, x2d, ffn_norm[l][None])
            x2d = _ffn(h2, ffn_w1[j].astype(BF16), ffn_w3[j].astype(BF16), ffn_w2[j].astype(BF16), x1)
            moe = None
        else:
            r = jnp.pad(moe_router[j], ((0, 0), (0, LANES - N_EXPERTS)))
            r_hi = r.astype(BF16)
            r_lo = (r - r_hi.astype(F32)).astype(BF16)
            x1, h2, logits = _outproj(*outs, w4, x2d, ffn_norm[l][None], (r_hi, r_lo))
            bm = _pick(t, 512)
            moe = _moe_layer(x1, h2, logits, moe_w1[j].astype(BF16), moe_w3[j].astype(BF16),
                             moe_w2[j].astype(BF16), bm)
            x2d = x1
        if l < depth - 1 and moe is not None:
            raise NotImplementedError("MoE combine is fused into the final norm; MoE layer must be last")
    return _final(x2d, final_norm[None], moe).reshape(bsz, seq, D_MODEL)
```

```python
import functools
import math

import numpy as np
import jax
import jax.numpy as jnp
from jax import lax
from jax.experimental import pallas as pl
from jax.experimental.pallas import tpu as pltpu

D_MODEL = 1024
DIFF_HEADS, DIFF_HD = 4, 32
DIL_HEADS, DIL_HD = 4, 64
DIL_PATTERNS = ((128, 1), (512, 4), (2048, 16))
MLA_HEADS, MLA_Q_LORA, MLA_KV_LORA, MLA_NOPE, MLA_ROPE, MLA_V = 4, 384, 128, 64, 32, 64
ROPE_THETA = 10000.0
SWA_HEADS, SWA_KV_HEADS, SWA_HD, SWA_WINDOW = 4, 2, 64, 128
D_FF = 2816
N_EXPERTS, TOP_K, D_FF_EXPERT = 8, 2, 3584
RMS_EPS = 1e-6
NEG_INF = -1e30
N_ALIBI = DIFF_HEADS + DIL_HEADS + SWA_HEADS

LOG2E = 1.4426950408889634
LANES = 128
HEAD_LANES = 64
VMEM_LIMIT = 56 * 1024 * 1024

F32 = jnp.float32
BF16 = jnp.bfloat16

_NT = (((1,), (1,)), ((), ()))


def _params(sem, vmem=VMEM_LIMIT):
    return pltpu.CompilerParams(dimension_semantics=sem, vmem_limit_bytes=vmem)


def _alibi_slopes():
    s = 2.0 ** (-8.0 * (np.arange(N_ALIBI) + 1) / N_ALIBI)
    return s[0::3], s[1::3], s[2::3]


def _pick(n, pref):
    t = min(pref, n)
    while n % t:
        t //= 2
    return t


N_SCALED = 2048
C_CQ, C_CKV, C_KPE, C_KPE_SW, N_MAIN = 2048, 2432, 2560, 2688, 2816


def _proj_body(x_ref, g_ref, w_ref, cs_ref, qn_ref, wq_ref, kn_ref, wkv_ref,
               cosq_ref, sinq_ref, cosk_ref, sink_ref,
               a_ref, avt_ref, b_ref, bvt_ref, d_ref, dvt_ref, cq_ref, ck_ref, cvt_ref):
    x = x_ref[...]
    ms = jnp.mean(x * x, axis=-1, keepdims=True)
    h = (x * lax.rsqrt(ms + RMS_EPS) * g_ref[...]).astype(BF16)
    acc = jnp.dot(h, w_ref[...], preferred_element_type=F32)
    sc = acc[:, :N_SCALED] * cs_ref[...]
    a_ref[...] = sc[:, 0:512].astype(BF16)
    avt_ref[...] = sc[:, 512:768].T.astype(BF16)
    b_ref[...] = sc[:, 768:1280].astype(BF16)
    bvt_ref[...] = sc[:, 1280:1536].T.astype(BF16)
    d_ref[...] = sc[:, 1536:1920].astype(BF16)
    dvt_ref[...] = sc[:, 1920:2048].T.astype(BF16)

    cq = acc[:, C_CQ:C_CKV]
    hq = (cq * lax.rsqrt(jnp.mean(cq * cq, axis=-1, keepdims=True) + RMS_EPS) * qn_ref[...]).astype(BF16)
    yq = jnp.dot(hq, wq_ref[...], preferred_element_type=F32)
    cosq, sinq = cosq_ref[...], sinq_ref[...]
    c_mla = (MLA_NOPE + MLA_ROPE) ** -0.5 * LOG2E
    for hd in range(MLA_HEADS):
        main = yq[:, hd * 256:(hd + 1) * 256]
        sw = yq[:, 1024 + hd * LANES:1024 + (hd + 1) * LANES]
        cq_ref[:, hd * 256:hd * 256 + LANES] = (main[:, :LANES] * cosq + sw * sinq).astype(BF16)
        cq_ref[:, hd * 256 + LANES:(hd + 1) * 256] = (main[:, LANES:] * c_mla).astype(BF16)

    ckv = acc[:, C_CKV:C_KPE]
    hk = (ckv * lax.rsqrt(jnp.mean(ckv * ckv, axis=-1, keepdims=True) + RMS_EPS) * kn_ref[...]).astype(BF16)
    kv = jnp.dot(hk, wkv_ref[...], preferred_element_type=F32)
    kr = (acc[:, C_KPE:C_KPE_SW] * cosk_ref[...] + acc[:, C_KPE_SW:N_MAIN] * sink_ref[...]).astype(BF16)
    for grp in range(2):
        ck_ref[:, grp * 256:grp * 256 + LANES] = kr
        ck_ref[:, grp * 256 + LANES:(grp + 1) * 256] = kv[:, grp * LANES:(grp + 1) * LANES].astype(BF16)
    cvt_ref[...] = kv[:, 256:512].T.astype(BF16)


def _proj(x2d, g, w_main, colscale, qn, wq_big, kn, wkv, tabs, seq):
    t = x2d.shape[0]
    tm = _pick(seq, 512)
    nsb = seq // tm
    full = lambda shape: pl.BlockSpec(shape, lambda i: (0,) * len(shape))
    tab = pl.BlockSpec((tm, LANES), lambda i: (i % nsb, 0))
    row = lambda w: pl.BlockSpec((tm, w), lambda i: (i, 0))
    vt = lambda w: pl.BlockSpec((None, w, tm), lambda i: (i // nsb, 0, i % nsb))
    row_sds = lambda w: jax.ShapeDtypeStruct((t, w), BF16)
    vt_sds = lambda w: jax.ShapeDtypeStruct((t // seq, w, seq), BF16)
    return pl.pallas_call(
        _proj_body,
        grid=(t // tm,),
        in_specs=[row(D_MODEL), full((1, D_MODEL)), full((D_MODEL, N_MAIN)), full((1, N_SCALED)),
                  full((1, MLA_Q_LORA)), full((MLA_Q_LORA, 1536)), full((1, MLA_KV_LORA)),
                  full((MLA_KV_LORA, 512)), tab, tab, tab, tab],
        out_specs=[row(512), vt(256), row(512), vt(256), row(384), vt(128), row(1024), row(512), vt(256)],
        out_shape=[row_sds(512), vt_sds(256), row_sds(512), vt_sds(256), row_sds(384), vt_sds(128),
                   row_sds(1024), row_sds(512), vt_sds(256)],
        compiler_params=_params(("parallel",)),
        name="proj",
    )(x2d, g, w_main, colscale, qn, wq_big, kn, wkv, *tabs)


def _softmax_update(i, s, vt, m_ref, l_ref, acc_ref):
    m_prev = m_ref[i]
    m_new = jnp.maximum(m_prev, jnp.max(s, axis=0, keepdims=True))
    p = jnp.exp2(s - m_new)
    alpha = jnp.exp2(m_prev - m_new)
    l_ref[i] = alpha * l_ref[i] + jnp.sum(p, axis=0, keepdims=True)
    acc_ref[i] = alpha * acc_ref[i] + jnp.dot(vt, p.astype(BF16), preferred_element_type=F32)
    m_ref[i] = m_new


def _row_mask(shape, lo, hi):
    row = lax.broadcasted_iota(jnp.int32, shape, 0)
    return jnp.logical_and(row >= lo, row < hi)


def _lane_mask(shape, lo, hi):
    lane = lax.broadcasted_iota(jnp.int32, shape, len(shape) - 1)
    return jnp.logical_and(lane >= lo, lane < hi)


def _diff_body(sl_ref, q_ref, k_ref, vt_ref, mask_ref, krow_ref, lam_ref, sub_ref, o_ref,
               qs_ref, m_ref, l_ref, acc_ref, *, tq, lam_init):
    pair, qi = pl.program_id(1), pl.program_id(2)
    tk = tq
    q = q_ref[...].astype(F32)
    for hd in range(2):
        for mp in range(2):
            lo = hd * HEAD_LANES + mp * DIFF_HD
            qs_ref[2 * hd + mp] = jnp.where(_lane_mask(q.shape, lo, lo + DIFF_HD), q, 0.0).astype(BF16)
    m_ref[...] = jnp.full(m_ref.shape, NEG_INF, F32)
    l_ref[...] = jnp.zeros(l_ref.shape, F32)
    acc_ref[...] = jnp.zeros(acc_ref.shape, F32)

    def step(j, masked):
        off = pl.multiple_of(j * tk, tk)
        k = k_ref[pl.ds(off, tk), :]
        vt = vt_ref[:, pl.ds(off, tk)]
        rel = ((j - qi) * tk).astype(F32)
        for hd in range(2):
            cv = jnp.tile(sl_ref[2 * pair + hd] * (krow_ref[...] + rel), (1, tq // LANES))
            for mp in range(2):
                i = 2 * hd + mp
                s = lax.dot_general(k, qs_ref[i], _NT, preferred_element_type=F32) + cv
                if masked:
                    s = s + mask_ref[...]
                _softmax_update(i, s, vt, m_ref, l_ref, acc_ref)

    def loop_body(j, carry):
        step(j, False)
        return carry

    lax.fori_loop(0, qi, loop_body, 0)
    step(qi, True)

    lp = lam_ref[...]
    lam = (jnp.exp(jnp.sum(lp[0:1, :] * lp[1:2, :], axis=-1, keepdims=True))
           - jnp.exp(jnp.sum(lp[2:3, :] * lp[3:4, :], axis=-1, keepdims=True)) + lam_init)
    o_h = [acc_ref[2 * hd] / l_ref[2 * hd] - lam * (acc_ref[2 * hd + 1] / l_ref[2 * hd + 1]) for hd in range(2)]
    o = jnp.where(_row_mask((LANES, tq), 0, HEAD_LANES), o_h[0], o_h[1]).T
    in_h0 = _lane_mask((tq, LANES), 0, HEAD_LANES)
    sq = o * o
    ms0 = jnp.sum(jnp.where(in_h0, sq, 0.0), axis=-1, keepdims=True)
    ms1 = jnp.sum(jnp.where(in_h0, 0.0, sq), axis=-1, keepdims=True)
    ms = jnp.where(in_h0, ms0, ms1) * (1.0 / HEAD_LANES)
    y = o * lax.rsqrt(ms + RMS_EPS) * sub_ref[...]
    o_ref[...] = (y * (1.0 - lam_init)).astype(o_ref.dtype)


def _diff_attention(a3, avt, slopes_l2, mask_t, krow, lam_p, sub128, layer_idx, tq):
    b, s, _ = a3.shape
    nq = s // tq
    lam_init = 0.8 - 0.6 * math.exp(-0.3 * layer_idx)
    body = functools.partial(_diff_body, tq=tq, lam_init=lam_init)
    grid_spec = pltpu.PrefetchScalarGridSpec(
        num_scalar_prefetch=1,
        grid=(b, 2, nq),
        in_specs=[
            pl.BlockSpec((None, tq, LANES), lambda bi, p, qi, sl: (bi, qi, p)),
            pl.BlockSpec((None, s, LANES), lambda bi, p, qi, sl: (bi, 0, 2 + p)),
            pl.BlockSpec((None, LANES, s), lambda bi, p, qi, sl: (bi, p, 0)),
            pl.BlockSpec((tq, tq), lambda bi, p, qi, sl: (0, 0)),
            pl.BlockSpec((tq, LANES), lambda bi, p, qi, sl: (0, 0)),
            pl.BlockSpec((4, DIFF_HD), lambda bi, p, qi, sl: (0, 0)),
            pl.BlockSpec((1, LANES), lambda bi, p, qi, sl: (0, 0)),
        ],
        out_specs=pl.BlockSpec((None, tq, LANES), lambda bi, p, qi, sl: (bi, qi, p)),
        scratch_shapes=[pltpu.VMEM((4, tq, LANES), BF16), pltpu.VMEM((4, 1, tq), F32),
                        pltpu.VMEM((4, 1, tq), F32), pltpu.VMEM((4, LANES, tq), F32)],
    )
    return pl.pallas_call(
        body, grid_spec=grid_spec,
        out_shape=jax.ShapeDtypeStruct((b, s, 256), BF16),
        compiler_params=_params(("parallel", "parallel", "arbitrary")),
        name="diff_attn",
    )(slopes_l2, a3, a3, avt, mask_t, krow, lam_p, sub128)


def _mla_body(q0_ref, q1_ref, k_ref, vt_ref, mask_ref, o_ref, m_ref, l_ref, acc_ref, *, tq):
    qi = pl.program_id(2)
    tk = tq
    m_ref[...] = jnp.full(m_ref.shape, NEG_INF, F32)
    l_ref[...] = jnp.zeros(l_ref.shape, F32)
    acc_ref[...] = jnp.zeros(acc_ref.shape, F32)
    q_refs = (q0_ref, q1_ref)

    def step(j, masked):
        off = pl.multiple_of(j * tk, tk)
        k = k_ref[pl.ds(off, tk), :]
        vt = vt_ref[:, pl.ds(off, tk)]
        for i in range(2):
            s = lax.dot_general(k, q_refs[i][...], _NT, preferred_element_type=F32)
            if masked:
                s = s + mask_ref[...]
            _softmax_update(i, s, vt, m_ref, l_ref, acc_ref)

    def loop_body(j, carry):
        step(j, False)
        return carry

    lax.fori_loop(0, qi, loop_body, 0)
    step(qi, True)
    o_t = jnp.where(_row_mask((LANES, tq), 0, HEAD_LANES), acc_ref[0] / l_ref[0], acc_ref[1] / l_ref[1])
    o_ref[...] = o_t.T.astype(o_ref.dtype)


def _mla_attention(cq3, ck3, cvt, mask_t, tq):
    b, s, _ = cq3.shape
    nq = s // tq
    return pl.pallas_call(
        functools.partial(_mla_body, tq=tq),
        grid=(b, 2, nq),
        in_specs=[
            pl.BlockSpec((None, tq, 256), lambda bi, g, qi: (bi, qi, 2 * g)),
            pl.BlockSpec((None, tq, 256), lambda bi, g, qi: (bi, qi, 2 * g + 1)),
            pl.BlockSpec((None, s, 256), lambda bi, g, qi: (bi, 0, g)),
            pl.BlockSpec((None, LANES, s), lambda bi, g, qi: (bi, g, 0)),
            pl.BlockSpec((tq, tq), lambda bi, g, qi: (0, 0)),
        ],
        out_specs=pl.BlockSpec((None, tq, LANES), lambda bi, g, qi: (bi, qi, g)),
        out_shape=jax.ShapeDtypeStruct((b, s, 256), BF16),
        scratch_shapes=[pltpu.VMEM((2, 1, tq), F32), pltpu.VMEM((2, 1, tq), F32),
                        pltpu.VMEM((2, LANES, tq), F32)],
        compiler_params=_params(("parallel", "parallel", "arbitrary")),
        name="mla_attn",
    )(cq3, cq3, ck3, cvt, mask_t)


def _band_body(m0_ref, q_ref, k_ref, vt_ref, bias_ref, o_ref, qs_ref, m_ref, l_ref, acc_ref,
               *, tq, n_off, l_init):
    slab, qi = pl.program_id(1), pl.program_id(2)
    tk = tq
    q = q_ref[...].astype(F32)
    for i in range(2):
        qs_ref[i] = jnp.where(_lane_mask(q.shape, i * HEAD_LANES, (i + 1) * HEAD_LANES), q, 0.0).astype(BF16)
        m_ref[i] = jnp.full((1, tq), m0_ref[2 * slab + i], F32)
    l_ref[...] = jnp.full(l_ref.shape, l_init, F32)
    acc_ref[...] = jnp.zeros(acc_ref.shape, F32)

    def loop_body(o, carry):
        off = pl.multiple_of((qi - o) * tk, tk)
        k = k_ref[pl.ds(off, tk), :]
        vt = vt_ref[:, pl.ds(off, tk)]
        for i in range(2):
            s = lax.dot_general(k, qs_ref[i], _NT, preferred_element_type=F32) + bias_ref[i, o]
            _softmax_update(i, s, vt, m_ref, l_ref, acc_ref)
        return carry

    lax.fori_loop(0, jnp.minimum(qi, n_off - 1) + 1, loop_body, 0)
    o_t = jnp.where(_row_mask((LANES, tq), 0, HEAD_LANES), acc_ref[0] / l_ref[0], acc_ref[1] / l_ref[1])
    o_ref[...] = o_t.T.astype(o_ref.dtype)


def _band_attention(x3, xvt, k_blk_of, vt_blk_of, bias, m0, l_init, tq, name):
    b, s, _ = x3.shape
    nq = s // tq
    n_off = bias.shape[2]
    grid_spec = pltpu.PrefetchScalarGridSpec(
        num_scalar_prefetch=1,
        grid=(b, 2, nq),
        in_specs=[
            pl.BlockSpec((None, tq, LANES), lambda bi, p, qi, m0r: (bi, qi, p)),
            pl.BlockSpec((None, s, LANES), lambda bi, p, qi, m0r: (bi, 0, k_blk_of(p))),
            pl.BlockSpec((None, LANES, s), lambda bi, p, qi, m0r: (bi, vt_blk_of(p), 0)),
            pl.BlockSpec((None, 2, n_off, tq, tq), lambda bi, p, qi, m0r: (p, 0, 0, 0, 0)),
        ],
        out_specs=pl.BlockSpec((None, tq, LANES), lambda bi, p, qi, m0r: (bi, qi, p)),
        scratch_shapes=[pltpu.VMEM((2, tq, LANES), BF16), pltpu.VMEM((2, 1, tq), F32),
                        pltpu.VMEM((2, 1, tq), F32), pltpu.VMEM((2, LANES, tq), F32)],
    )
    return pl.pallas_call(
        functools.partial(_band_body, tq=tq, n_off=n_off, l_init=l_init),
        grid_spec=grid_spec,
        out_shape=jax.ShapeDtypeStruct((b, s, 256), BF16),
        compiler_params=_params(("parallel", "parallel", "arbitrary")),
        name=name,
    )(m0, x3, x3, xvt, bias)


def _band_bias(slopes, tq, n_off, mult_fn):
    di = jnp.arange(tq, dtype=jnp.int32)[None, :]
    dj = jnp.arange(tq, dtype=jnp.int32)[:, None]
    off = jnp.arange(n_off, dtype=jnp.int32)[:, None, None]
    delta = off * tq + di - dj
    mult = mult_fn(delta)
    sl = jnp.asarray(slopes, F32)[:, None, None, None] * LOG2E
    val = -sl * delta.astype(F32)[None] + jnp.log2(jnp.maximum(mult, 1).astype(F32))[None]
    return jnp.where((mult > 0)[None], val, NEG_INF)


def _dil_mult(delta):
    m = jnp.zeros(delta.shape, jnp.int32)
    for window, d in DIL_PATTERNS:
        m = m + ((delta >= 0) & (delta <= window) & (delta % d == 0)).astype(jnp.int32)
    return m


def _swa_mult(delta):
    return ((delta >= 0) & (delta <= SWA_WINDOW - 1)).astype(jnp.int32)


def _outproj_body(*refs, with_router):
    if with_router:
        (oa, ob, oc, od, w_ref, x_ref, g_ref, rh_ref, rl_ref, x1_ref, h2_ref, lg_ref) = refs
    else:
        (oa, ob, oc, od, w_ref, x_ref, g_ref, x1_ref, h2_ref) = refs
    acc = x_ref[...]
    for i, o in enumerate((oa, ob, oc, od)):
        acc = acc + jnp.dot(o[...], w_ref[i], preferred_element_type=F32)
    x1_ref[...] = acc
    h2 = acc * lax.rsqrt(jnp.mean(acc * acc, axis=-1, keepdims=True) + RMS_EPS) * g_ref[...]
    hi = h2.astype(BF16)
    h2_ref[...] = hi
    if with_router:
        lo = (h2 - hi.astype(F32)).astype(BF16)
        lg_ref[...] = (jnp.dot(hi, rh_ref[...], preferred_element_type=F32)
                       + (jnp.dot(hi, rl_ref[...], preferred_element_type=F32)
                          + jnp.dot(lo, rh_ref[...], preferred_element_type=F32)))


def _outproj(oa, ob, oc, od, w4, x2d, g, router_hl=None):
    t = x2d.shape[0]
    tm = _pick(t, 512)
    with_router = router_hl is not None
    full = lambda shape: pl.BlockSpec(shape, lambda i: (0,) * len(shape))
    row = lambda w: pl.BlockSpec((tm, w), lambda i: (i, 0))
    in_specs = [row(256)] * 4 + [full((4, 256, D_MODEL)), row(D_MODEL), full((1, D_MODEL))]
    args = [oa, ob, oc, od, w4, x2d, g]
    out_specs = [row(D_MODEL), row(D_MODEL)]
    out_shape = [jax.ShapeDtypeStruct((t, D_MODEL), F32), jax.ShapeDtypeStruct((t, D_MODEL), BF16)]
    if with_router:
        in_specs += [full((D_MODEL, LANES))] * 2
        args += list(router_hl)
        out_specs.append(row(LANES))
        out_shape.append(jax.ShapeDtypeStruct((t, LANES), F32))
    return pl.pallas_call(
        functools.partial(_outproj_body, with_router=with_router),
        grid=(t // tm,), in_specs=in_specs, out_specs=out_specs, out_shape=out_shape,
        compiler_params=_params(("parallel",)),
        name="outproj",
    )(*args)


def _ffn_body(h_ref, w1_ref, w3_ref, w2_ref, x_ref, o_ref, acc_ref):
    f = pl.program_id(1)
    h = h_ref[...]
    a = jnp.dot(h, w1_ref[...], preferred_element_type=F32)
    b = jnp.dot(h, w3_ref[...], preferred_element_type=F32)
    mid = (a * jax.nn.sigmoid(a) * b).astype(BF16)
    contrib = jnp.dot(mid, w2_ref[...], preferred_element_type=F32)

    @pl.when(f == 0)
    def _():
        acc_ref[...] = contrib

    @pl.when(f > 0)
    def _():
        acc_ref[...] += contrib

    @pl.when(f == pl.num_programs(1) - 1)
    def _():
        o_ref[...] = x_ref[...] + acc_ref[...]


def _ffn(h2, w1, w3, w2, x1):
    t = h2.shape[0]
    tm = _pick(t, 512)
    tf = D_FF // 2
    return pl.pallas_call(
        _ffn_body,
        grid=(t // tm, D_FF // tf),
        in_specs=[pl.BlockSpec((tm, D_MODEL), lambda i, f: (i, 0)),
                  pl.BlockSpec((D_MODEL, tf), lambda i, f: (0, f)),
                  pl.BlockSpec((D_MODEL, tf), lambda i, f: (0, f)),
                  pl.BlockSpec((tf, D_MODEL), lambda i, f: (f, 0)),
                  pl.BlockSpec((tm, D_MODEL), lambda i, f: (i, 0))],
        out_specs=pl.BlockSpec((tm, D_MODEL), lambda i, f: (i, 0)),
        out_shape=jax.ShapeDtypeStruct((t, D_MODEL), F32),
        scratch_shapes=[pltpu.VMEM((tm, D_MODEL), F32)],
        compiler_params=_params(("parallel", "arbitrary")),
        name="ffn",
    )(h2, w1, w3, w2, x1)


def _route_body(lg_ref, tri_ref, info_ref, cnt_ref, carry_ref):
    i = pl.program_id(0)

    @pl.when(i == 0)
    def _():
        carry_ref[...] = jnp.zeros(carry_ref.shape, F32)

    lg = lg_ref[...]
    lane = lax.broadcasted_iota(jnp.int32, lg.shape, 1)
    lg = jnp.where(lane < N_EXPERTS, lg, -jnp.inf)
    m1 = jnp.max(lg, axis=-1, keepdims=True)
    i1 = jnp.min(jnp.where(lg == m1, lane, LANES), axis=-1, keepdims=True)
    oh1 = lane == i1
    lg2 = jnp.where(oh1, -jnp.inf, lg)
    m2 = jnp.max(lg2, axis=-1, keepdims=True)
    i2 = jnp.min(jnp.where(lg2 == m2, lane, LANES), axis=-1, keepdims=True)
    oh2 = lane == i2
    e2 = jnp.exp(m2 - m1)
    g1 = 1.0 / (1.0 + e2)
    g2 = e2 / (1.0 + e2)
    oh = jnp.where(jnp.logical_or(oh1, oh2), 1.0, 0.0)
    rank = jnp.dot(tri_ref[...], oh.astype(BF16), preferred_element_type=F32) + carry_ref[...]
    r1 = jnp.sum(jnp.where(oh1, rank, 0.0), axis=-1, keepdims=True)
    r2 = jnp.sum(jnp.where(oh2, rank, 0.0), axis=-1, keepdims=True)
    carry_ref[...] += jnp.sum(oh, axis=0, keepdims=True)
    cnt_ref[...] = carry_ref[...]
    info = jnp.zeros(lg.shape, F32)
    for c, val in enumerate((i1.astype(F32), i2.astype(F32), r1, r2, g1, g2)):
        info = jnp.where(lane == c, val, info)
    info_ref[...] = info


def _route(logits):
    t = logits.shape[0]
    tm = _pick(t, 512)
    tri = (jnp.arange(tm)[:, None] > jnp.arange(tm)[None, :]).astype(BF16)
    return pl.pallas_call(
        _route_body,
        grid=(t // tm,),
        in_specs=[pl.BlockSpec((tm, LANES), lambda i: (i, 0)), pl.BlockSpec((tm, tm), lambda i: (0, 0))],
        out_specs=[pl.BlockSpec((tm, LANES), lambda i: (i, 0)), pl.BlockSpec((1, LANES), lambda i: (0, 0))],
        out_shape=[jax.ShapeDtypeStruct((t, LANES), F32), jax.ShapeDtypeStruct((1, LANES), F32)],
        scratch_shapes=[pltpu.VMEM((1, LANES), F32)],
        compiler_params=_params(("arbitrary",)),
        name="route",
    )(logits, tri)


def _moe_ffn_body(be_ref, x_ref, w1_ref, w3_ref, w2_ref, o_ref, acc_ref):
    f = pl.program_id(1)
    x = x_ref[...]
    a = jnp.dot(x, w1_ref[...], preferred_element_type=F32)
    b = jnp.dot(x, w3_ref[...], preferred_element_type=F32)
    mid = (a * jax.nn.sigmoid(a) * b).astype(BF16)
    contrib = jnp.dot(mid, w2_ref[...], preferred_element_type=F32)

    @pl.when(f == 0)
    def _():
        acc_ref[...] = contrib

    @pl.when(f > 0)
    def _():
        acc_ref[...] += contrib

    @pl.when(f == pl.num_programs(1) - 1)
    def _():
        o_ref[...] = acc_ref[...].astype(o_ref.dtype)


def _moe_ffn(xs, blk_expert, w1, w3, w2, bm):
    rows = xs.shape[0]
    tf = D_FF_EXPERT // 4
    grid_spec = pltpu.PrefetchScalarGridSpec(
        num_scalar_prefetch=1,
        grid=(rows // bm, D_FF_EXPERT // tf),
        in_specs=[pl.BlockSpec((bm, D_MODEL), lambda i, f, be: (i, 0)),
                  pl.BlockSpec((None, D_MODEL, tf), lambda i, f, be: (be[i], 0, f)),
                  pl.BlockSpec((None, D_MODEL, tf), lambda i, f, be: (be[i], 0, f)),
                  pl.BlockSpec((None, tf, D_MODEL), lambda i, f, be: (be[i], f, 0))],
        out_specs=pl.BlockSpec((bm, D_MODEL), lambda i, f, be: (i, 0)),
        scratch_shapes=[pltpu.VMEM((bm, D_MODEL), F32)],
    )
    return pl.pallas_call(
        _moe_ffn_body, grid_spec=grid_spec,
        out_shape=jax.ShapeDtypeStruct((rows, D_MODEL), F32),
        compiler_params=_params(("parallel", "arbitrary")),
        name="moe_ffn",
    )(blk_expert, xs, w1, w3, w2)


def _final_body(*refs, with_moe):
    if with_moe:
        x_ref, y0_ref, y1_ref, info_ref, g_ref, o_ref = refs
        info = info_ref[...]
        x = x_ref[...] + (info[:, 4:5] * y0_ref[...] + info[:, 5:6] * y1_ref[...])
    else:
        x_ref, g_ref, o_ref = refs
        x = x_ref[...]
    o_ref[...] = x * lax.rsqrt(jnp.mean(x * x, axis=-1, keepdims=True) + RMS_EPS) * g_ref[...]


def _final(x2d, g, moe=None):
    t = x2d.shape[0]
    tm = _pick(t, 512)
    row = lambda w: pl.BlockSpec((tm, w), lambda i: (i, 0))
    in_specs, args = [row(D_MODEL)], [x2d]
    if moe is not None:
        in_specs += [row(D_MODEL), row(D_MODEL), row(LANES)]
        args += list(moe)
    in_specs.append(pl.BlockSpec((1, D_MODEL), lambda i: (0, 0)))
    args.append(g)
    return pl.pallas_call(
        functools.partial(_final_body, with_moe=moe is not None),
        grid=(t // tm,), in_specs=in_specs, out_specs=row(D_MODEL),
        out_shape=jax.ShapeDtypeStruct((t, D_MODEL), F32),
        compiler_params=_params(("parallel",)),
        name="final",
    )(*args)


def _prep_attn_weights(w_in, w_out, w_uq, w_ukv):
    widths = (256, 256, 256, 256, 256, 256, MLA_Q_LORA, MLA_KV_LORA, MLA_ROPE, 256, 128, 128)
    offs = np.concatenate([[0], np.cumsum(widths)])
    qa, ka, va, qb, kb, vb, cq, ckv, kpe, qd, kd, vd = [w_in[:, offs[i]:offs[i + 1]] for i in range(12)]
    hperm = np.array([0, 2, 1, 3])
    qd = qd.reshape(D_MODEL, 4, SWA_HD)[:, hperm].reshape(D_MODEL, 256)
    pad = jnp.zeros((D_MODEL, LANES - MLA_ROPE), w_in.dtype)
    half = MLA_ROPE // 2
    kpe_sw = jnp.concatenate([kpe[:, half:], kpe[:, :half]], axis=1)
    w_main = jnp.concatenate([qa, ka, va, qb, kb, vb, qd, kd, vd, cq, ckv, kpe, pad, kpe_sw, pad], axis=1)

    uq = w_uq.reshape(MLA_Q_LORA, MLA_HEADS, MLA_NOPE + MLA_ROPE)
    z = lambda n: jnp.zeros((MLA_Q_LORA, n), w_uq.dtype)
    mains, sws = [], []
    for hd in range(MLA_HEADS):
        nope, rope = uq[:, hd, :MLA_NOPE], uq[:, hd, MLA_NOPE:]
        nope128 = jnp.concatenate([nope, z(64)] if hd % 2 == 0 else [z(64), nope], axis=1)
        mains.append(jnp.concatenate([rope, z(LANES - MLA_ROPE), nope128], axis=1))
        sws.append(jnp.concatenate([rope[:, half:], rope[:, :half], z(LANES - MLA_ROPE)], axis=1))
    wq_big = jnp.concatenate(mains + sws, axis=1)

    ukv = w_ukv.reshape(MLA_KV_LORA, MLA_HEADS, MLA_NOPE + MLA_V)
    wkv = jnp.concatenate([ukv[:, :, :MLA_NOPE].reshape(MLA_KV_LORA, 256),
                           ukv[:, :, MLA_NOPE:].reshape(MLA_KV_LORA, 256)], axis=1)

    wo = w_out.reshape(4, 256, D_MODEL)
    wo_d = wo[3].reshape(4, SWA_HD, D_MODEL)[hperm].reshape(256, D_MODEL)
    w4 = jnp.stack([wo[0], wo[1], wo[2], wo_d])
    return w_main.astype(BF16), wq_big.astype(BF16), wkv.astype(BF16), w4.astype(BF16)


def _rope_tables(s):
    inv = ROPE_THETA ** (-jnp.arange(0, MLA_ROPE, 2, dtype=F32) / MLA_ROPE)
    ang = jnp.arange(s, dtype=F32)[:, None] * inv[None, :]
    cos, sin = jnp.cos(ang), jnp.sin(ang)
    z = jnp.zeros((s, LANES - MLA_ROPE), F32)
    cos128 = jnp.concatenate([cos, cos, z], axis=1)
    sin128 = jnp.concatenate([-sin, sin, z], axis=1)
    c_mla = (MLA_NOPE + MLA_ROPE) ** -0.5 * LOG2E
    return cos128 * c_mla, sin128 * c_mla, cos128, sin128


def _col_scale():
    ca = DIFF_HD ** -0.5 * LOG2E
    cb = DIL_HD ** -0.5 * LOG2E
    cd = SWA_HD ** -0.5 * LOG2E
    v = np.ones((1, N_SCALED), np.float32)
    v[0, 0:256] = ca
    v[0, 768:1024] = cb
    v[0, 1536:1792] = cd
    return jnp.asarray(v)


def _moe_layer(x1, h2, logits, w1, w3, w2, bm):
    t = h2.shape[0]
    info, counts = _route(logits)
    cnt = counts[0, :N_EXPERTS].astype(jnp.int32)
    padded = (cnt + bm - 1) // bm * bm
    pends = jnp.cumsum(padded)
    pstarts = pends - padded
    e12 = info[:, 0:2].astype(jnp.int32)
    pos = pstarts[e12] + info[:, 2:4].astype(jnp.int32)
    nb = (t * TOP_K) // bm + N_EXPERTS
    rows = nb * bm
    blk_start = jnp.arange(nb, dtype=jnp.int32) * bm
    blk_expert = jnp.minimum(jnp.sum((pends[None, :] <= blk_start[:, None]).astype(jnp.int32), axis=1),
                             N_EXPERTS - 1)
    row_tok = jnp.zeros((rows,), jnp.int32).at[pos.reshape(-1)].set(
        jnp.repeat(jnp.arange(t, dtype=jnp.int32), TOP_K))
    xs = h2[row_tok]
    ys = _moe_ffn(xs, blk_expert.astype(jnp.int32), w1, w3, w2, bm)
    return ys[pos[:, 0]], ys[pos[:, 1]], info


def kernel(x, attn_norm, w_in, w_out, diff_lambda, diff_subln, mla_q_norm, mla_w_uq, mla_kv_norm, mla_w_ukv,
           swa_sinks, ffn_norm, ffn_w1, ffn_w3, ffn_w2, moe_router, moe_w1, moe_w3, moe_w2, final_norm):
    bsz, seq, _ = x.shape
    t = bsz * seq
    depth = w_in.shape[0]
    slopes_a, slopes_b, slopes_d = _alibi_slopes()
    tq_dense = _pick(seq, 512)
    tq_band = _pick(seq, 256)

    tabs = _rope_tables(seq)
    colscale = _col_scale()
    di = jnp.arange(tq_dense)
    causal_mask_t = jnp.where(di[:, None] <= di[None, :], 0.0, NEG_INF).astype(F32)
    krow = jnp.broadcast_to(di.astype(F32)[:, None], (tq_dense, LANES))
    n_off_dil = min(max(w for w, _ in DIL_PATTERNS) // tq_band + 1, seq // tq_band)
    n_off_swa = min((SWA_WINDOW - 1 + tq_band - 1) // tq_band + 1, seq // tq_band)
    dil_bias = _band_bias(slopes_b, tq_band, n_off_dil, _dil_mult).reshape(2, 2, n_off_dil, tq_band, tq_band)
    swa_bias = _band_bias(slopes_d[np.array([0, 2, 1, 3])], tq_band, n_off_swa, _swa_mult
                          ).reshape(2, 2, n_off_swa, tq_band, tq_band)
    slopes_a_l2 = jnp.asarray(slopes_a * LOG2E, F32)
    dil_m0 = jnp.full((4,), NEG_INF, F32)

    x2d = x.reshape(t, D_MODEL)
    for l in range(depth):
        w_main, wq_big, wkv, w4 = _prep_attn_weights(w_in[l], w_out[l], mla_w_uq[l], mla_w_ukv[l])
        a, avt, b, bvt, d, dvt, cq, ck, cvt = _proj(x2d, attn_norm[l][None], w_main, colscale, mla_q_norm[l][None],
                                                    wq_big, mla_kv_norm[l][None], wkv, tabs, seq)
        a3, b3, d3 = (v.reshape(bsz, seq, -1) for v in (a, b, d))
        sub128 = jnp.concatenate([diff_subln[l], diff_subln[l]])[None]
        o_a = _diff_attention(a3, avt, slopes_a_l2, causal_mask_t, krow, diff_lambda[l], sub128, l, tq_dense)
        o_b = _band_attention(b3, bvt, lambda p: 2 + p, lambda p: p, dil_bias, dil_m0, 0.0, tq_band, "dil_attn")
        o_c = _mla_attention(cq.reshape(bsz, seq, -1), ck.reshape(bsz, seq, -1), cvt, causal_mask_t, tq_dense)
        swa_m0 = (swa_sinks[l].astype(F32) * LOG2E)[np.array([0, 2, 1, 3])]
        o_d = _band_attention(d3, dvt, lambda p: 2, lambda p: 0, swa_bias, swa_m0, 1.0, tq_band, "swa_attn")
        outs = [v.reshape(t, 256) for v in (o_a, o_b, o_c, o_d)]
        j = l // 2
        if l % 2 == 0:
            x1, h2 = _outproj(*outs, w4, x2d, ffn_norm[l][None])
            x2d = _ffn(h2, ffn_w1[j].astype(BF16), ffn_w3[j].astype(BF16), ffn_w2[j].astype(BF16), x1)
            moe = None
        else:
            r = jnp.pad(moe_router[j], ((0, 0), (0, LANES - N_EXPERTS)))
            r_hi = r.astype(BF16)
            r_lo = (r - r_hi.astype(F32)).astype(BF16)
            x1, h2, logits = _outproj(*outs, w4, x2d, ffn_norm[l][None], (r_hi, r_lo))
            bm = _pick(t, 512)
            moe = _moe_layer(x1, h2, logits, moe_w1[j].astype(BF16), moe_w3[j].astype(BF16),
                             moe_w2[j].astype(BF16), bm)
            x2d = x1
        if l < depth - 1 and moe is not None:
            raise NotImplementedError("MoE combine is fused into the final norm; MoE layer must be last")
    return _final(x2d, final_norm[None], moe).reshape(bsz, seq, D_MODEL)
```

```python
import functools
import math

import numpy as np
import jax
import jax.numpy as jnp
from jax import lax
from jax.experimental import pallas as pl
from jax.experimental.pallas import tpu as pltpu

D_MODEL = 1024
DIFF_HEADS, DIFF_HD = 4, 32
DIL_HEADS, DIL_HD = 4, 64
DIL_PATTERNS = ((128, 1), (512, 4), (2048, 16))
MLA_HEADS, MLA_Q_LORA, MLA_KV_LORA, MLA_NOPE, MLA_ROPE, MLA_V = 4, 384, 128, 64, 32, 64
ROPE_THETA = 10000.0
SWA_HEADS, SWA_KV_HEADS, SWA_HD, SWA_WINDOW = 4, 2, 64, 128
D_FF = 2816
N_EXPERTS, TOP_K, D_FF_EXPERT = 8, 2, 3584
RMS_EPS = 1e-6
NEG_INF = -1e30
N_ALIBI = DIFF_HEADS + DIL_HEADS + SWA_HEADS

LOG2E = 1.4426950408889634
LANES = 128
HEAD_LANES = 64
VMEM_LIMIT = 56 * 1024 * 1024

F32 = jnp.float32
BF16 = jnp.bfloat16

_NT = (((1,), (1,)), ((), ()))


def _params(sem, vmem=VMEM_LIMIT):
    return pltpu.CompilerParams(dimension_semantics=sem, vmem_limit_bytes=vmem)


def _alibi_slopes():
    s = 2.0 ** (-8.0 * (np.arange(N_ALIBI) + 1) / N_ALIBI)
    return s[0::3], s[1::3], s[2::3]


def _pick(n, pref):
    t = min(pref, n)
    while n % t:
        t //= 2
    return t


N_SCALED = 2048
C_CQ, C_CKV, C_KPE, C_KPE_SW, N_MAIN = 2048, 2432, 2560, 2688, 2816


def _proj_body(x_ref, g_ref, w_ref, cs_ref, qn_ref, wq_ref, kn_ref, wkv_ref,
               cosq_ref, sinq_ref, cosk_ref, sink_ref,
               a_ref, avt_ref, b_ref, bvt_ref, d_ref, dvt_ref, cq_ref, ck_ref, cvt_ref):
    x = x_ref[...]
    ms = jnp.mean(x * x, axis=-1, keepdims=True)
    h = (x * lax.rsqrt(ms + RMS_EPS) * g_ref[...]).astype(BF16)
    acc = jnp.dot(h, w_ref[...], preferred_element_type=F32)
    sc = acc[:, :N_SCALED] * cs_ref[...]
    a_ref[...] = sc[:, 0:512].astype(BF16)
    avt_ref[...] = sc[:, 512:768].T.astype(BF16)
    b_ref[...] = sc[:, 768:1280].astype(BF16)
    bvt_ref[...] = sc[:, 1280:1536].T.astype(BF16)
    d_ref[...] = sc[:, 1536:1920].astype(BF16)
    dvt_ref[...] = sc[:, 1920:2048].T.astype(BF16)

    cq = acc[:, C_CQ:C_CKV]
    hq = (cq * lax.rsqrt(jnp.mean(cq * cq, axis=-1, keepdims=True) + RMS_EPS) * qn_ref[...]).astype(BF16)
    yq = jnp.dot(hq, wq_ref[...], preferred_element_type=F32)
    cosq, sinq = cosq_ref[...], sinq_ref[...]
    c_mla = (MLA_NOPE + MLA_ROPE) ** -0.5 * LOG2E
    for hd in range(MLA_HEADS):
        main = yq[:, hd * 256:(hd + 1) * 256]
        sw = yq[:, 1024 + hd * LANES:1024 + (hd + 1) * LANES]
        cq_ref[:, hd * 256:hd * 256 + LANES] = (main[:, :LANES] * cosq + sw * sinq).astype(BF16)
        cq_ref[:, hd * 256 + LANES:(hd + 1) * 256] = (main[:, LANES:] * c_mla).astype(BF16)

    ckv = acc[:, C_CKV:C_KPE]
    hk = (ckv * lax.rsqrt(jnp.mean(ckv * ckv, axis=-1, keepdims=True) + RMS_EPS) * kn_ref[...]).astype(BF16)
    kv = jnp.dot(hk, wkv_ref[...], preferred_element_type=F32)
    kr = (acc[:, C_KPE:C_KPE_SW] * cosk_ref[...] + acc[:, C_KPE_SW:N_MAIN] * sink_ref[...]).astype(BF16)
    for grp in range(2):
        ck_ref[:, grp * 256:grp * 256 + LANES] = kr
        ck_ref[:, grp * 256 + LANES:(grp + 1) * 256] = kv[:, grp * LANES:(grp + 1) * LANES].astype(BF16)
    cvt_ref[...] = kv[:, 256:512].T.astype(BF16)


def _proj(x2d, g, w_main, colscale, qn, wq_big, kn, wkv, tabs, seq):
    t = x2d.shape[0]
    tm = _pick(seq, 512)
    nsb = seq // tm
    full = lambda shape: pl.BlockSpec(shape, lambda i: (0,) * len(shape))
    tab = pl.BlockSpec((tm, LANES), lambda i: (i % nsb, 0))
    row = lambda w: pl.BlockSpec((tm, w), lambda i: (i, 0))
    vt = lambda w: pl.BlockSpec((None, w, tm), lambda i: (i // nsb, 0, i % nsb))
    row_sds = lambda w: jax.ShapeDtypeStruct((t, w), BF16)
    vt_sds = lambda w: jax.ShapeDtypeStruct((t // seq, w, seq), BF16)
    return pl.pallas_call(
        _proj_body,
        grid=(t // tm,),
        in_specs=[row(D_MODEL), full((1, D_MODEL)), full((D_MODEL, N_MAIN)), full((1, N_SCALED)),
                  full((1, MLA_Q_LORA)), full((MLA_Q_LORA, 1536)), full((1, MLA_KV_LORA)),
                  full((MLA_KV_LORA, 512)), tab, tab, tab, tab],
        out_specs=[row(512), vt(256), row(512), vt(256), row(384), vt(128), row(1024), row(512), vt(256)],
        out_shape=[row_sds(512), vt_sds(256), row_sds(512), vt_sds(256), row_sds(384), vt_sds(128),
                   row_sds(1024), row_sds(512), vt_sds(256)],
        compiler_params=_params(("parallel",)),
        name="proj",
    )(x2d, g, w_main, colscale, qn, wq_big, kn, wkv, *tabs)


def _attn_pipeline(n_soft, n_steps, scores, load_vt, m0, s_ref, l_ref, acc_ref):
    def stage_a(ss, dst_ref, m_cur):
        m_next, alpha = [], []
        for i in range(n_soft):
            m_new = jnp.maximum(m_cur[i], jnp.max(ss[i], axis=0, keepdims=True))
            dst_ref[i] = ss[i]
            m_next.append(m_new)
            alpha.append(jnp.exp2(m_cur[i] - m_new))
        return tuple(m_next), tuple(alpha)

    def stage_b(t, src_ref, carry):
        m_cur, alpha = carry
        vt = load_vt(t)
        for i in range(n_soft):
            p = jnp.exp2(src_ref[i] - m_cur[i])
            l_ref[i] = alpha[i] * l_ref[i] + jnp.sum(p, axis=0, keepdims=True)
            acc_ref[i] = alpha[i] * acc_ref[i] + jnp.dot(vt, p.astype(BF16), preferred_element_type=F32)

    def half(t, src_ref, dst_ref, carry):
        ss = [scores(i, t + 1, False) for i in range(n_soft)]
        stage_b(t, src_ref, carry)
        return stage_a(ss, dst_ref, carry[0])

    def body(u, carry):
        carry = half(2 * u, s_ref[0], s_ref[1], carry)
        return half(2 * u + 1, s_ref[1], s_ref[0], carry)

    carry = stage_a([scores(i, 0, True) for i in range(n_soft)], s_ref[0], m0)
    carry = lax.fori_loop(0, n_steps // 2, body, carry)

    @pl.when(n_steps % 2 == 1)
    def _():
        stage_b(n_steps, s_ref[1], half(n_steps - 1, s_ref[0], s_ref[1], carry))

    @pl.when(n_steps % 2 == 0)
    def _():
        stage_b(n_steps, s_ref[0], carry)


def _row_mask(shape, lo, hi):
    row = lax.broadcasted_iota(jnp.int32, shape, 0)
    return jnp.logical_and(row >= lo, row < hi)


def _lane_mask(shape, lo, hi):
    lane = lax.broadcasted_iota(jnp.int32, shape, len(shape) - 1)
    return jnp.logical_and(lane >= lo, lane < hi)


def _diff_body(sl_ref, q_ref, k_ref, vt_ref, mask_ref, krow_ref, lam_ref, sub_ref, o_ref,
               qs_ref, s0_ref, s1_ref, l_ref, acc_ref, *, tq, lam_init):
    pair, qi = pl.program_id(1), pl.program_id(2)
    tk = tq
    q = q_ref[...].astype(F32)
    for hd in range(2):
        for mp in range(2):
            lo = hd * HEAD_LANES + mp * DIFF_HD
            qs_ref[2 * hd + mp] = jnp.where(_lane_mask(q.shape, lo, lo + DIFF_HD), q, 0.0).astype(BF16)
    l_ref[...] = jnp.zeros(l_ref.shape, F32)
    acc_ref[...] = jnp.zeros(acc_ref.shape, F32)

    def scores(i, t, first):
        off = pl.multiple_of((qi - t) * tk, tk)
        k = k_ref[pl.ds(off, tk), :]
        rel = (-t * tk).astype(F32) if not first else 0.0
        cv = jnp.tile(sl_ref[2 * pair + i // 2] * (krow_ref[...] + rel), (1, tq // LANES))
        s = lax.dot_general(k, qs_ref[i], _NT, preferred_element_type=F32) + cv
        return s + mask_ref[...] if first else s

    def load_vt(t):
        return vt_ref[:, pl.ds(pl.multiple_of((qi - t) * tk, tk), tk)]

    m0 = tuple(jnp.full((1, tq), NEG_INF, F32) for _ in range(4))
    _attn_pipeline(4, qi, scores, load_vt, m0, (s0_ref, s1_ref), l_ref, acc_ref)

    lp = lam_ref[...]
    lam = (jnp.exp(jnp.sum(lp[0:1, :] * lp[1:2, :], axis=-1, keepdims=True))
           - jnp.exp(jnp.sum(lp[2:3, :] * lp[3:4, :], axis=-1, keepdims=True)) + lam_init)
    o_h = [acc_ref[2 * hd] / l_ref[2 * hd] - lam * (acc_ref[2 * hd + 1] / l_ref[2 * hd + 1]) for hd in range(2)]
    o = jnp.where(_row_mask((LANES, tq), 0, HEAD_LANES), o_h[0], o_h[1]).T
    in_h0 = _lane_mask((tq, LANES), 0, HEAD_LANES)
    sq = o * o
    ms0 = jnp.sum(jnp.where(in_h0, sq, 0.0), axis=-1, keepdims=True)
    ms1 = jnp.sum(jnp.where(in_h0, 0.0, sq), axis=-1, keepdims=True)
    ms = jnp.where(in_h0, ms0, ms1) * (1.0 / HEAD_LANES)
    y = o * lax.rsqrt(ms + RMS_EPS) * sub_ref[...]
    o_ref[...] = (y * (1.0 - lam_init)).astype(o_ref.dtype)


def _diff_attention(a3, avt, slopes_l2, mask_t, krow, lam_p, sub128, layer_idx, tq):
    b, s, _ = a3.shape
    nq = s // tq
    lam_init = 0.8 - 0.6 * math.exp(-0.3 * layer_idx)
    body = functools.partial(_diff_body, tq=tq, lam_init=lam_init)
    grid_spec = pltpu.PrefetchScalarGridSpec(
        num_scalar_prefetch=1,
        grid=(b, 2, nq),
        in_specs=[
            pl.BlockSpec((None, tq, LANES), lambda bi, p, qi, sl: (bi, qi, p)),
            pl.BlockSpec((None, s, LANES), lambda bi, p, qi, sl: (bi, 0, 2 + p)),
            pl.BlockSpec((None, LANES, s), lambda bi, p, qi, sl: (bi, p, 0)),
            pl.BlockSpec((tq, tq), lambda bi, p, qi, sl: (0, 0)),
            pl.BlockSpec((tq, LANES), lambda bi, p, qi, sl: (0, 0)),
            pl.BlockSpec((4, DIFF_HD), lambda bi, p, qi, sl: (0, 0)),
            pl.BlockSpec((1, LANES), lambda bi, p, qi, sl: (0, 0)),
        ],
        out_specs=pl.BlockSpec((None, tq, LANES), lambda bi, p, qi, sl: (bi, qi, p)),
        scratch_shapes=[pltpu.VMEM((4, tq, LANES), BF16), pltpu.VMEM((4, tq, tq), F32),
                        pltpu.VMEM((4, tq, tq), F32), pltpu.VMEM((4, 1, tq), F32),
                        pltpu.VMEM((4, LANES, tq), F32)],
    )
    return pl.pallas_call(
        body, grid_spec=grid_spec,
        out_shape=jax.ShapeDtypeStruct((b, s, 256), BF16),
        compiler_params=_params(("parallel", "parallel", "arbitrary")),
        name="diff_attn",
    )(slopes_l2, a3, a3, avt, mask_t, krow, lam_p, sub128)


def _mla_body(q0_ref, q1_ref, k_ref, vt_ref, mask_ref, o_ref, s0_ref, s1_ref, l_ref, acc_ref, *, tq):
    qi = pl.program_id(2)
    tk = tq
    l_ref[...] = jnp.zeros(l_ref.shape, F32)
    acc_ref[...] = jnp.zeros(acc_ref.shape, F32)
    q_refs = (q0_ref, q1_ref)

    def scores(i, t, first):
        k = k_ref[pl.ds(pl.multiple_of((qi - t) * tk, tk), tk), :]
        s = lax.dot_general(k, q_refs[i][...], _NT, preferred_element_type=F32)
        return s + mask_ref[...] if first else s

    def load_vt(t):
        return vt_ref[:, pl.ds(pl.multiple_of((qi - t) * tk, tk), tk)]

    m0 = tuple(jnp.full((1, tq), NEG_INF, F32) for _ in range(2))
    _attn_pipeline(2, qi, scores, load_vt, m0, (s0_ref, s1_ref), l_ref, acc_ref)
    o_t = jnp.where(_row_mask((LANES, tq), 0, HEAD_LANES), acc_ref[0] / l_ref[0], acc_ref[1] / l_ref[1])
    o_ref[...] = o_t.T.astype(o_ref.dtype)


def _mla_attention(cq3, ck3, cvt, mask_t, tq):
    b, s, _ = cq3.shape
    nq = s // tq
    return pl.pallas_call(
        functools.partial(_mla_body, tq=tq),
        grid=(b, 2, nq),
        in_specs=[
            pl.BlockSpec((None, tq, 256), lambda bi, g, qi: (bi, qi, 2 * g)),
            pl.BlockSpec((None, tq, 256), lambda bi, g, qi: (bi, qi, 2 * g + 1)),
            pl.BlockSpec((None, s, 256), lambda bi, g, qi: (bi, 0, g)),
            pl.BlockSpec((None, LANES, s), lambda bi, g, qi: (bi, g, 0)),
            pl.BlockSpec((tq, tq), lambda bi, g, qi: (0, 0)),
        ],
        out_specs=pl.BlockSpec((None, tq, LANES), lambda bi, g, qi: (bi, qi, g)),
        out_shape=jax.ShapeDtypeStruct((b, s, 256), BF16),
        scratch_shapes=[pltpu.VMEM((2, tq, tq), F32), pltpu.VMEM((2, tq, tq), F32),
                        pltpu.VMEM((2, 1, tq), F32), pltpu.VMEM((2, LANES, tq), F32)],
        compiler_params=_params(("parallel", "parallel", "arbitrary")),
        name="mla_attn",
    )(cq3, cq3, ck3, cvt, mask_t)


def _band_body(m0_ref, q_ref, k_ref, vt_ref, bias_ref, o_ref, qs_ref, s0_ref, s1_ref, l_ref, acc_ref,
               *, tq, n_off, l_init):
    slab, qi = pl.program_id(1), pl.program_id(2)
    tk = tq
    q = q_ref[...].astype(F32)
    for i in range(2):
        qs_ref[i] = jnp.where(_lane_mask(q.shape, i * HEAD_LANES, (i + 1) * HEAD_LANES), q, 0.0).astype(BF16)
    l_ref[...] = jnp.full(l_ref.shape, l_init, F32)
    acc_ref[...] = jnp.zeros(acc_ref.shape, F32)

    def scores(i, t, first):
        k = k_ref[pl.ds(pl.multiple_of((qi - t) * tk, tk), tk), :]
        return lax.dot_general(k, qs_ref[i], _NT, preferred_element_type=F32) + bias_ref[i, t]

    def load_vt(t):
        return vt_ref[:, pl.ds(pl.multiple_of((qi - t) * tk, tk), tk)]

    m0 = tuple(jnp.full((1, tq), m0_ref[2 * slab + i], F32) for i in range(2))
    _attn_pipeline(2, jnp.minimum(qi, n_off - 1), scores, load_vt, m0, (s0_ref, s1_ref), l_ref, acc_ref)
    o_t = jnp.where(_row_mask((LANES, tq), 0, HEAD_LANES), acc_ref[0] / l_ref[0], acc_ref[1] / l_ref[1])
    o_ref[...] = o_t.T.astype(o_ref.dtype)


def _band_attention(x3, xvt, k_blk_of, vt_blk_of, bias, m0, l_init, tq, name):
    b, s, _ = x3.shape
    nq = s // tq
    n_off = bias.shape[2]
    grid_spec = pltpu.PrefetchScalarGridSpec(
        num_scalar_prefetch=1,
        grid=(b, 2, nq),
        in_specs=[
            pl.BlockSpec((None, tq, LANES), lambda bi, p, qi, m0r: (bi, qi, p)),
            pl.BlockSpec((None, s, LANES), lambda bi, p, qi, m0r: (bi, 0, k_blk_of(p))),
            pl.BlockSpec((None, LANES, s), lambda bi, p, qi, m0r: (bi, vt_blk_of(p), 0)),
            pl.BlockSpec((None, 2, n_off, tq, tq), lambda bi, p, qi, m0r: (p, 0, 0, 0, 0)),
        ],
        out_specs=pl.BlockSpec((None, tq, LANES), lambda bi, p, qi, m0r: (bi, qi, p)),
        scratch_shapes=[pltpu.VMEM((2, tq, LANES), BF16), pltpu.VMEM((2, tq, tq), F32),
                        pltpu.VMEM((2, tq, tq), F32), pltpu.VMEM((2, 1, tq), F32),
                        pltpu.VMEM((2, LANES, tq), F32)],
    )
    return pl.pallas_call(
        functools.partial(_band_body, tq=tq, n_off=n_off, l_init=l_init),
        grid_spec=grid_spec,
        out_shape=jax.ShapeDtypeStruct((b, s, 256), BF16),
        compiler_params=_params(("parallel", "parallel", "arbitrary")),
        name=name,
    )(m0, x3, x3, xvt, bias)


def _band_bias(slopes, tq, n_off, mult_fn):
    di = jnp.arange(tq, dtype=jnp.int32)[None, :]
    dj = jnp.arange(tq, dtype=jnp.int32)[:, None]
    off = jnp.arange(n_off, dtype=jnp.int32)[:, None, None]
    delta = off * tq + di - dj
    mult = mult_fn(delta)
    sl = jnp.asarray(slopes, F32)[:, None, None, None] * LOG2E
    val = -sl * delta.astype(F32)[None] + jnp.log2(jnp.maximum(mult, 1).astype(F32))[None]
    return jnp.where((mult > 0)[None], val, NEG_INF)


def _dil_mult(delta):
    m = jnp.zeros(delta.shape, jnp.int32)
    for window, d in DIL_PATTERNS:
        m = m + ((delta >= 0) & (delta <= window) & (delta % d == 0)).astype(jnp.int32)
    return m


def _swa_mult(delta):
    return ((delta >= 0) & (delta <= SWA_WINDOW - 1)).astype(jnp.int32)


def _outproj_body(*refs, with_router):
    if with_router:
        (oa, ob, oc, od, w_ref, x_ref, g_ref, rh_ref, rl_ref, x1_ref, h2_ref, lg_ref) = refs
    else:
        (oa, ob, oc, od, w_ref, x_ref, g_ref, x1_ref, h2_ref) = refs
    acc = x_ref[...]
    for i, o in enumerate((oa, ob, oc, od)):
        acc = acc + jnp.dot(o[...], w_ref[i], preferred_element_type=F32)
    x1_ref[...] = acc
    h2 = acc * lax.rsqrt(jnp.mean(acc * acc, axis=-1, keepdims=True) + RMS_EPS) * g_ref[...]
    hi = h2.astype(BF16)
    h2_ref[...] = hi
    if with_router:
        lo = (h2 - hi.astype(F32)).astype(BF16)
        lg_ref[...] = (jnp.dot(hi, rh_ref[...], preferred_element_type=F32)
                       + (jnp.dot(hi, rl_ref[...], preferred_element_type=F32)
                          + jnp.dot(lo, rh_ref[...], preferred_element_type=F32)))


def _outproj(oa, ob, oc, od, w4, x2d, g, router_hl=None):
    t = x2d.shape[0]
    tm = _pick(t, 512)
    with_router = router_hl is not None
    full = lambda shape: pl.BlockSpec(shape, lambda i: (0,) * len(shape))
    row = lambda w: pl.BlockSpec((tm, w), lambda i: (i, 0))
    in_specs = [row(256)] * 4 + [full((4, 256, D_MODEL)), row(D_MODEL), full((1, D_MODEL))]
    args = [oa, ob, oc, od, w4, x2d, g]
    out_specs = [row(D_MODEL), row(D_MODEL)]
    out_shape = [jax.ShapeDtypeStruct((t, D_MODEL), F32), jax.ShapeDtypeStruct((t, D_MODEL), BF16)]
    if with_router:
        in_specs += [full((D_MODEL, LANES))] * 2
        args += list(router_hl)
        out_specs.append(row(LANES))
        out_shape.append(jax.ShapeDtypeStruct((t, LANES), F32))
    return pl.pallas_call(
        functools.partial(_outproj_body, with_router=with_router),
        grid=(t // tm,), in_specs=in_specs, out_specs=out_specs, out_shape=out_shape,
        compiler_params=_params(("parallel",)),
        name="outproj",
    )(*args)


def _ffn_body(h_ref, w1_ref, w3_ref, w2_ref, x_ref, o_ref, acc_ref):
    f = pl.program_id(1)
    h = h_ref[...]
    a = jnp.dot(h, w1_ref[...], preferred_element_type=F32)
    b = jnp.dot(h, w3_ref[...], preferred_element_type=F32)
    mid = (a * jax.nn.sigmoid(a) * b).astype(BF16)
    contrib = jnp.dot(mid, w2_ref[...], preferred_element_type=F32)

    @pl.when(f == 0)
    def _():
        acc_ref[...] = contrib

    @pl.when(f > 0)
    def _():
        acc_ref[...] += contrib

    @pl.when(f == pl.num_programs(1) - 1)
    def _():
        o_ref[...] = x_ref[...] + acc_ref[...]


def _ffn(h2, w1, w3, w2, x1):
    t = h2.shape[0]
    tm = _pick(t, 512)
    tf = D_FF // 2
    return pl.pallas_call(
        _ffn_body,
        grid=(t // tm, D_FF // tf),
        in_specs=[pl.BlockSpec((tm, D_MODEL), lambda i, f: (i, 0)),
                  pl.BlockSpec((D_MODEL, tf), lambda i, f: (0, f)),
                  pl.BlockSpec((D_MODEL, tf), lambda i, f: (0, f)),
                  pl.BlockSpec((tf, D_MODEL), lambda i, f: (f, 0)),
                  pl.BlockSpec((tm, D_MODEL), lambda i, f: (i, 0))],
        out_specs=pl.BlockSpec((tm, D_MODEL), lambda i, f: (i, 0)),
        out_shape=jax.ShapeDtypeStruct((t, D_MODEL), F32),
        scratch_shapes=[pltpu.VMEM((tm, D_MODEL), F32)],
        compiler_params=_params(("parallel", "arbitrary")),
        name="ffn",
    )(h2, w1, w3, w2, x1)


def _route_body(lg_ref, tri_ref, info_ref, cnt_ref, carry_ref):
    i = pl.program_id(0)

    @pl.when(i == 0)
    def _():
        carry_ref[...] = jnp.zeros(carry_ref.shape, F32)

    lg = lg_ref[...]
    lane = lax.broadcasted_iota(jnp.int32, lg.shape, 1)
    lg = jnp.where(lane < N_EXPERTS, lg, -jnp.inf)
    m1 = jnp.max(lg, axis=-1, keepdims=True)
    i1 = jnp.min(jnp.where(lg == m1, lane, LANES), axis=-1, keepdims=True)
    oh1 = lane == i1
    lg2 = jnp.where(oh1, -jnp.inf, lg)
    m2 = jnp.max(lg2, axis=-1, keepdims=True)
    i2 = jnp.min(jnp.where(lg2 == m2, lane, LANES), axis=-1, keepdims=True)
    oh2 = lane == i2
    e2 = jnp.exp(m2 - m1)
    g1 = 1.0 / (1.0 + e2)
    g2 = e2 / (1.0 + e2)
    oh = jnp.where(jnp.logical_or(oh1, oh2), 1.0, 0.0)
    rank = jnp.dot(tri_ref[...], oh.astype(BF16), preferred_element_type=F32) + carry_ref[...]
    r1 = jnp.sum(jnp.where(oh1, rank, 0.0), axis=-1, keepdims=True)
    r2 = jnp.sum(jnp.where(oh2, rank, 0.0), axis=-1, keepdims=True)
    carry_ref[...] += jnp.sum(oh, axis=0, keepdims=True)
    cnt_ref[...] = carry_ref[...]
    info = jnp.zeros(lg.shape, F32)
    for c, val in enumerate((i1.astype(F32), i2.astype(F32), r1, r2, g1, g2)):
        info = jnp.where(lane == c, val, info)
    info_ref[...] = info


def _route(logits):
    t = logits.shape[0]
    tm = _pick(t, 512)
    tri = (jnp.arange(tm)[:, None] > jnp.arange(tm)[None, :]).astype(BF16)
    return pl.pallas_call(
        _route_body,
        grid=(t // tm,),
        in_specs=[pl.BlockSpec((tm, LANES), lambda i: (i, 0)), pl.BlockSpec((tm, tm), lambda i: (0, 0))],
        out_specs=[pl.BlockSpec((tm, LANES), lambda i: (i, 0)), pl.BlockSpec((1, LANES), lambda i: (0, 0))],
        out_shape=[jax.ShapeDtypeStruct((t, LANES), F32), jax.ShapeDtypeStruct((1, LANES), F32)],
        scratch_shapes=[pltpu.VMEM((1, LANES), F32)],
        compiler_params=_params(("arbitrary",)),
        name="route",
    )(logits, tri)


def _moe_ffn_body(be_ref, x_ref, w1_ref, w3_ref, w2_ref, o_ref, acc_ref):
    f = pl.program_id(1)
    x = x_ref[...]
    a = jnp.dot(x, w1_ref[...], preferred_element_type=F32)
    b = jnp.dot(x, w3_ref[...], preferred_element_type=F32)
    mid = (a * jax.nn.sigmoid(a) * b).astype(BF16)
    contrib = jnp.dot(mid, w2_ref[...], preferred_element_type=F32)

    @pl.when(f == 0)
    def _():
        acc_ref[...] = contrib

    @pl.when(f > 0)
    def _():
        acc_ref[...] += contrib

    @pl.when(f == pl.num_programs(1) - 1)
    def _():
        o_ref[...] = acc_ref[...].astype(o_ref.dtype)


def _moe_ffn(xs, blk_expert, w1, w3, w2, bm):
    rows = xs.shape[0]
    tf = D_FF_EXPERT // 4
    grid_spec = pltpu.PrefetchScalarGridSpec(
        num_scalar_prefetch=1,
        grid=(rows // bm, D_FF_EXPERT // tf),
        in_specs=[pl.BlockSpec((bm, D_MODEL), lambda i, f, be: (i, 0)),
                  pl.BlockSpec((None, D_MODEL, tf), lambda i, f, be: (be[i], 0, f)),
                  pl.BlockSpec((None, D_MODEL, tf), lambda i, f, be: (be[i], 0, f)),
                  pl.BlockSpec((None, tf, D_MODEL), lambda i, f, be: (be[i], f, 0))],
        out_specs=pl.BlockSpec((bm, D_MODEL), lambda i, f, be: (i, 0)),
        scratch_shapes=[pltpu.VMEM((bm, D_MODEL), F32)],
    )
    return pl.pallas_call(
        _moe_ffn_body, grid_spec=grid_spec,
        out_shape=jax.ShapeDtypeStruct((rows, D_MODEL), F32),
        compiler_params=_params(("parallel", "arbitrary")),
        name="moe_ffn",
    )(blk_expert, xs, w1, w3, w2)


def _final_body(*refs, with_moe):
    if with_moe:
        x_ref, y0_ref, y1_ref, info_ref, g_ref, o_ref = refs
        info = info_ref[...]
        x = x_ref[...] + (info[:, 4:5] * y0_ref[...] + info[:, 5:6] * y1_ref[...])
    else:
        x_ref, g_ref, o_ref = refs
        x = x_ref[...]
    o_ref[...] = x * lax.rsqrt(jnp.mean(x * x, axis=-1, keepdims=True) + RMS_EPS) * g_ref[...]


def _final(x2d, g, moe=None):
    t = x2d.shape[0]
    tm = _pick(t, 512)
    row = lambda w: pl.BlockSpec((tm, w), lambda i: (i, 0))
    in_specs, args = [row(D_MODEL)], [x2d]
    if moe is not None:
        in_specs += [row(D_MODEL), row(D_MODEL), row(LANES)]
        args += list(moe)
    in_specs.append(pl.BlockSpec((1, D_MODEL), lambda i: (0, 0)))
    args.append(g)
    return pl.pallas_call(
        functools.partial(_final_body, with_moe=moe is not None),
        grid=(t // tm,), in_specs=in_specs, out_specs=row(D_MODEL),
        out_shape=jax.ShapeDtypeStruct((t, D_MODEL), F32),
        compiler_params=_params(("parallel",)),
        name="final",
    )(*args)


def _prep_attn_weights(w_in, w_out, w_uq, w_ukv):
    widths = (256, 256, 256, 256, 256, 256, MLA_Q_LORA, MLA_KV_LORA, MLA_ROPE, 256, 128, 128)
    offs = np.concatenate([[0], np.cumsum(widths)])
    qa, ka, va, qb, kb, vb, cq, ckv, kpe, qd, kd, vd = [w_in[:, offs[i]:offs[i + 1]] for i in range(12)]
    hperm = np.array([0, 2, 1, 3])
    qd = qd.reshape(D_MODEL, 4, SWA_HD)[:, hperm].reshape(D_MODEL, 256)
    pad = jnp.zeros((D_MODEL, LANES - MLA_ROPE), w_in.dtype)
    half = MLA_ROPE // 2
    kpe_sw = jnp.concatenate([kpe[:, half:], kpe[:, :half]], axis=1)
    w_main = jnp.concatenate([qa, ka, va, qb, kb, vb, qd, kd, vd, cq, ckv, kpe, pad, kpe_sw, pad], axis=1)

    uq = w_uq.reshape(MLA_Q_LORA, MLA_HEADS, MLA_NOPE + MLA_ROPE)
    z = lambda n: jnp.zeros((MLA_Q_LORA, n), w_uq.dtype)
    mains, sws = [], []
    for hd in range(MLA_HEADS):
        nope, rope = uq[:, hd, :MLA_NOPE], uq[:, hd, MLA_NOPE:]
        nope128 = jnp.concatenate([nope, z(64)] if hd % 2 == 0 else [z(64), nope], axis=1)
        mains.append(jnp.concatenate([rope, z(LANES - MLA_ROPE), nope128], axis=1))
        sws.append(jnp.concatenate([rope[:, half:], rope[:, :half], z(LANES - MLA_ROPE)], axis=1))
    wq_big = jnp.concatenate(mains + sws, axis=1)

    ukv = w_ukv.reshape(MLA_KV_LORA, MLA_HEADS, MLA_NOPE + MLA_V)
    wkv = jnp.concatenate([ukv[:, :, :MLA_NOPE].reshape(MLA_KV_LORA, 256),
                           ukv[:, :, MLA_NOPE:].reshape(MLA_KV_LORA, 256)], axis=1)

    wo = w_out.reshape(4, 256, D_MODEL)
    wo_d = wo[3].reshape(4, SWA_HD, D_MODEL)[hperm].reshape(256, D_MODEL)
    w4 = jnp.stack([wo[0], wo[1], wo[2], wo_d])
    return w_main.astype(BF16), wq_big.astype(BF16), wkv.astype(BF16), w4.astype(BF16)


def _rope_tables(s):
    inv = ROPE_THETA ** (-jnp.arange(0, MLA_ROPE, 2, dtype=F32) / MLA_ROPE)
    ang = jnp.arange(s, dtype=F32)[:, None] * inv[None, :]
    cos, sin = jnp.cos(ang), jnp.sin(ang)
    z = jnp.zeros((s, LANES - MLA_ROPE), F32)
    cos128 = jnp.concatenate([cos, cos, z], axis=1)
    sin128 = jnp.concatenate([-sin, sin, z], axis=1)
    c_mla = (MLA_NOPE + MLA_ROPE) ** -0.5 * LOG2E
    return cos128 * c_mla, sin128 * c_mla, cos128, sin128


def _col_scale():
    ca = DIFF_HD ** -0.5 * LOG2E
    cb = DIL_HD ** -0.5 * LOG2E
    cd = SWA_HD ** -0.5 * LOG2E
    v = np.ones((1, N_SCALED), np.float32)
    v[0, 0:256] = ca
    v[0, 768:1024] = cb
    v[0, 1536:1792] = cd
    return jnp.asarray(v)


def _moe_layer(x1, h2, logits, w1, w3, w2, bm):
    t = h2.shape[0]
    info, counts = _route(logits)
    cnt = counts[0, :N_EXPERTS].astype(jnp.int32)
    padded = (cnt + bm - 1) // bm * bm
    pends = jnp.cumsum(padded)
    pstarts = pends - padded
    e12 = info[:, 0:2].astype(jnp.int32)
    pos = pstarts[e12] + info[:, 2:4].astype(jnp.int32)
    nb = (t * TOP_K) // bm + N_EXPERTS
    rows = nb * bm
    blk_start = jnp.arange(nb, dtype=jnp.int32) * bm
    blk_expert = jnp.minimum(jnp.sum((pends[None, :] <= blk_start[:, None]).astype(jnp.int32), axis=1),
                             N_EXPERTS - 1)
    row_tok = jnp.zeros((rows,), jnp.int32).at[pos.reshape(-1)].set(
        jnp.repeat(jnp.arange(t, dtype=jnp.int32), TOP_K))
    xs = h2[row_tok]
    ys = _moe_ffn(xs, blk_expert.astype(jnp.int32), w1, w3, w2, bm)
    return ys[pos[:, 0]], ys[pos[:, 1]], info


def kernel(x, attn_norm, w_in, w_out, diff_lambda, diff_subln, mla_q_norm, mla_w_uq, mla_kv_norm, mla_w_ukv,
           swa_sinks, ffn_norm, ffn_w1, ffn_w3, ffn_w2, moe_router, moe_w1, moe_w3, moe_w2, final_norm):
    bsz, seq, _ = x.shape
    t = bsz * seq
    depth = w_in.shape[0]
    slopes_a, slopes_b, slopes_d = _alibi_slopes()
    tq_dense = _pick(seq, 512)
    tq_band = _pick(seq, 256)

    tabs = _rope_tables(seq)
    colscale = _col_scale()
    di = jnp.arange(tq_dense)
    causal_mask_t = jnp.where(di[:, None] <= di[None, :], 0.0, NEG_INF).astype(F32)
    krow = jnp.broadcast_to(di.astype(F32)[:, None], (tq_dense, LANES))
    n_off_dil = min(max(w for w, _ in DIL_PATTERNS) // tq_band + 1, seq // tq_band)
    n_off_swa = min((SWA_WINDOW - 1 + tq_band - 1) // tq_band + 1, seq // tq_band)
    dil_bias = _band_bias(slopes_b, tq_band, n_off_dil, _dil_mult).reshape(2, 2, n_off_dil, tq_band, tq_band)
    swa_bias = _band_bias(slopes_d[np.array([0, 2, 1, 3])], tq_band, n_off_swa, _swa_mult
                          ).reshape(2, 2, n_off_swa, tq_band, tq_band)
    slopes_a_l2 = jnp.asarray(slopes_a * LOG2E, F32)
    dil_m0 = jnp.full((4,), NEG_INF, F32)

    x2d = x.reshape(t, D_MODEL)
    for l in range(depth):
        w_main, wq_big, wkv, w4 = _prep_attn_weights(w_in[l], w_out[l], mla_w_uq[l], mla_w_ukv[l])
        a, avt, b, bvt, d, dvt, cq, ck, cvt = _proj(x2d, attn_norm[l][None], w_main, colscale, mla_q_norm[l][None],
                                                    wq_big, mla_kv_norm[l][None], wkv, tabs, seq)
        a3, b3, d3 = (v.reshape(bsz, seq, -1) for v in (a, b, d))
        sub128 = jnp.concatenate([diff_subln[l], diff_subln[l]])[None]
        o_a = _diff_attention(a3, avt, slopes_a_l2, causal_mask_t, krow, diff_lambda[l], sub128, l, tq_dense)
        o_b = _band_attention(b3, bvt, lambda p: 2 + p, lambda p: p, dil_bias, dil_m0, 0.0, tq_band, "dil_attn")
        o_c = _mla_attention(cq.reshape(bsz, seq, -1), ck.reshape(bsz, seq, -1), cvt, causal_mask_t, tq_dense)
        swa_m0 = (swa_sinks[l].astype(F32) * LOG2E)[np.array([0, 2, 1, 3])]
        o_d = _band_attention(d3, dvt, lambda p: 2, lambda p: 0, swa_bias, swa_m0, 1.0, tq_band, "swa_attn")
        outs = [v.reshape(t, 256) for v in (o_a, o_b, o_c, o_d)]
        j = l // 2
        if l % 2 == 0:
            x1, h2 = _outproj(*outs, w4, x2d, ffn_norm[l][None])
            x2d = _ffn(h2, ffn_w1[j].astype(BF16), ffn_w3[j].astype(BF16), ffn_w2[j].astype(BF16), x1)
            moe = None
        else:
            r = jnp.pad(moe_router[j], ((0, 0), (0, LANES - N_EXPERTS)))
            r_hi = r.astype(BF16)
            r_lo = (r - r_hi.astype(F32)).astype(BF16)
            x1, h2, logits = _outproj(*outs, w4, x2d, ffn_norm[l][None], (r_hi, r_lo))
            bm = _pick(t, 512)
            moe = _moe_layer(x1, h2, logits, moe_w1[j].astype(BF16), moe_w3[j].astype(BF16),
                             moe_w2[j].astype(BF16), bm)
            x2d = x1
        if l < depth - 1 and moe is not None:
            raise NotImplementedError("MoE combine is fused into the final norm; MoE layer must be last")
    return _final(x2d, final_norm[None], moe).reshape(bsz, seq, D_MODEL)
```

```python
import functools
import math

import numpy as np
import jax
import jax.numpy as jnp
from jax import lax
from jax.experimental import pallas as pl
from jax.experimental.pallas import tpu as pltpu
from jax.experimental.pallas import tpu_sc as plsc

D_MODEL = 1024
DIFF_HEADS, DIFF_HD = 4, 32
DIL_HEADS, DIL_HD = 4, 64
DIL_PATTERNS = ((128, 1), (512, 4), (2048, 16))
MLA_HEADS, MLA_Q_LORA, MLA_KV_LORA, MLA_NOPE, MLA_ROPE, MLA_V = 4, 384, 128, 64, 32, 64
ROPE_THETA = 10000.0
SWA_HEADS, SWA_KV_HEADS, SWA_HD, SWA_WINDOW = 4, 2, 64, 128
D_FF = 2816
N_EXPERTS, TOP_K, D_FF_EXPERT = 8, 2, 3584
RMS_EPS = 1e-6
NEG_INF = -1e30
N_ALIBI = DIFF_HEADS + DIL_HEADS + SWA_HEADS

LOG2E = 1.4426950408889634
LANES = 128
HEAD_LANES = 64
VMEM_LIMIT = 56 * 1024 * 1024

F32 = jnp.float32
BF16 = jnp.bfloat16

_NT = (((1,), (1,)), ((), ()))


def _params(sem, vmem=VMEM_LIMIT):
    return pltpu.CompilerParams(dimension_semantics=sem, vmem_limit_bytes=vmem)


def _alibi_slopes():
    s = 2.0 ** (-8.0 * (np.arange(N_ALIBI) + 1) / N_ALIBI)
    return s[0::3], s[1::3], s[2::3]


def _pick(n, pref):
    t = min(pref, n)
    while n % t:
        t //= 2
    return t


N_SCALED = 2048
C_CQ, C_CKV, C_KPE, C_KPE_SW, N_MAIN = 2048, 2432, 2560, 2688, 2816


def _proj_body(x_ref, g_ref, w_ref, cs_ref, qn_ref, wq_ref, kn_ref, wkv_ref,
               cosq_ref, sinq_ref, cosk_ref, sink_ref,
               a_ref, avt_ref, b_ref, bvt_ref, d_ref, dvt_ref, cq_ref, ck_ref, cvt_ref):
    x = x_ref[...]
    ms = jnp.mean(x * x, axis=-1, keepdims=True)
    h = (x * lax.rsqrt(ms + RMS_EPS) * g_ref[...]).astype(BF16)
    acc = jnp.dot(h, w_ref[...], preferred_element_type=F32)
    sc = acc[:, :N_SCALED] * cs_ref[...]
    a_ref[...] = sc[:, 0:512].astype(BF16)
    avt_ref[...] = sc[:, 512:768].T.astype(BF16)
    b_ref[...] = sc[:, 768:1280].astype(BF16)
    bvt_ref[...] = sc[:, 1280:1536].T.astype(BF16)
    d_ref[...] = sc[:, 1536:1920].astype(BF16)
    dvt_ref[...] = sc[:, 1920:2048].T.astype(BF16)

    cq = acc[:, C_CQ:C_CKV]
    hq = (cq * lax.rsqrt(jnp.mean(cq * cq, axis=-1, keepdims=True) + RMS_EPS) * qn_ref[...]).astype(BF16)
    yq = jnp.dot(hq, wq_ref[...], preferred_element_type=F32)
    cosq, sinq = cosq_ref[...], sinq_ref[...]
    c_mla = (MLA_NOPE + MLA_ROPE) ** -0.5 * LOG2E
    for hd in range(MLA_HEADS):
        main = yq[:, hd * 256:(hd + 1) * 256]
        sw = yq[:, 1024 + hd * LANES:1024 + (hd + 1) * LANES]
        cq_ref[:, hd * 256:hd * 256 + LANES] = (main[:, :LANES] * cosq + sw * sinq).astype(BF16)
        cq_ref[:, hd * 256 + LANES:(hd + 1) * 256] = (main[:, LANES:] * c_mla).astype(BF16)

    ckv = acc[:, C_CKV:C_KPE]
    hk = (ckv * lax.rsqrt(jnp.mean(ckv * ckv, axis=-1, keepdims=True) + RMS_EPS) * kn_ref[...]).astype(BF16)
    kv = jnp.dot(hk, wkv_ref[...], preferred_element_type=F32)
    kr = (acc[:, C_KPE:C_KPE_SW] * cosk_ref[...] + acc[:, C_KPE_SW:N_MAIN] * sink_ref[...]).astype(BF16)
    for grp in range(2):
        ck_ref[:, grp * 256:grp * 256 + LANES] = kr
        ck_ref[:, grp * 256 + LANES:(grp + 1) * 256] = kv[:, grp * LANES:(grp + 1) * LANES].astype(BF16)
    cvt_ref[...] = kv[:, 256:512].T.astype(BF16)


def _proj(x2d, g, w_main, colscale, qn, wq_big, kn, wkv, tabs, seq):
    t = x2d.shape[0]
    tm = _pick(seq, 512)
    nsb = seq // tm
    full = lambda shape: pl.BlockSpec(shape, lambda i: (0,) * len(shape))
    tab = pl.BlockSpec((tm, LANES), lambda i: (i % nsb, 0))
    row = lambda w: pl.BlockSpec((tm, w), lambda i: (i, 0))
    vt = lambda w: pl.BlockSpec((None, w, tm), lambda i: (i // nsb, 0, i % nsb))
    row_sds = lambda w: jax.ShapeDtypeStruct((t, w), BF16)
    vt_sds = lambda w: jax.ShapeDtypeStruct((t // seq, w, seq), BF16)
    return pl.pallas_call(
        _proj_body,
        grid=(t // tm,),
        in_specs=[row(D_MODEL), full((1, D_MODEL)), full((D_MODEL, N_MAIN)), full((1, N_SCALED)),
                  full((1, MLA_Q_LORA)), full((MLA_Q_LORA, 1536)), full((1, MLA_KV_LORA)),
                  full((MLA_KV_LORA, 512)), tab, tab, tab, tab],
        out_specs=[row(512), vt(256), row(512), vt(256), row(384), vt(128), row(1024), row(512), vt(256)],
        out_shape=[row_sds(512), vt_sds(256), row_sds(512), vt_sds(256), row_sds(384), vt_sds(128),
                   row_sds(1024), row_sds(512), vt_sds(256)],
        compiler_params=_params(("parallel",)),
        name="proj",
    )(x2d, g, w_main, colscale, qn, wq_big, kn, wkv, *tabs)


def _attn_pipeline(n_soft, n_steps, scores, load_vt, m0, s_ref, l_ref, acc_ref):
    def stage_a(ss, dst_ref, m_cur):
        m_next, alpha = [], []
        for i in range(n_soft):
            m_new = jnp.maximum(m_cur[i], jnp.max(ss[i], axis=0, keepdims=True))
            dst_ref[i] = ss[i]
            m_next.append(m_new)
            alpha.append(jnp.exp2(m_cur[i] - m_new))
        return tuple(m_next), tuple(alpha)

    def stage_b(t, src_ref, carry):
        m_cur, alpha = carry
        vt = load_vt(t)
        for i in range(n_soft):
            p = jnp.exp2(src_ref[i] - m_cur[i])
            l_ref[i] = alpha[i] * l_ref[i] + jnp.sum(p, axis=0, keepdims=True)
            acc_ref[i] = alpha[i] * acc_ref[i] + jnp.dot(vt, p.astype(BF16), preferred_element_type=F32)

    def half(t, src_ref, dst_ref, carry):
        ss = [scores(i, t + 1, False) for i in range(n_soft)]
        stage_b(t, src_ref, carry)
        return stage_a(ss, dst_ref, carry[0])

    def body(u, carry):
        carry = half(2 * u, s_ref[0], s_ref[1], carry)
        return half(2 * u + 1, s_ref[1], s_ref[0], carry)

    carry = stage_a([scores(i, 0, True) for i in range(n_soft)], s_ref[0], m0)
    carry = lax.fori_loop(0, n_steps // 2, body, carry)

    @pl.when(n_steps % 2 == 1)
    def _():
        stage_b(n_steps, s_ref[1], half(n_steps - 1, s_ref[0], s_ref[1], carry))

    @pl.when(n_steps % 2 == 0)
    def _():
        stage_b(n_steps, s_ref[0], carry)


def _row_mask(shape, lo, hi):
    row = lax.broadcasted_iota(jnp.int32, shape, 0)
    return jnp.logical_and(row >= lo, row < hi)


def _lane_mask(shape, lo, hi):
    lane = lax.broadcasted_iota(jnp.int32, shape, len(shape) - 1)
    return jnp.logical_and(lane >= lo, lane < hi)


def _diff_body(sl_ref, q_ref, k_ref, vt_ref, mask_ref, krow_ref, lam_ref, sub_ref, o_ref,
               qs_ref, s0_ref, s1_ref, l_ref, acc_ref, *, tq, lam_init):
    pair, qi = pl.program_id(1), pl.program_id(2)
    tk = tq
    q = q_ref[...].astype(F32)
    for hd in range(2):
        for mp in range(2):
            lo = hd * HEAD_LANES + mp * DIFF_HD
            qs_ref[2 * hd + mp] = jnp.where(_lane_mask(q.shape, lo, lo + DIFF_HD), q, 0.0).astype(BF16)
    l_ref[...] = jnp.zeros(l_ref.shape, F32)
    acc_ref[...] = jnp.zeros(acc_ref.shape, F32)

    def scores(i, t, first):
        off = pl.multiple_of((qi - t) * tk, tk)
        k = k_ref[pl.ds(off, tk), :]
        rel = (-t * tk).astype(F32) if not first else 0.0
        cv = jnp.tile(sl_ref[2 * pair + i // 2] * (krow_ref[...] + rel), (1, tq // LANES))
        s = lax.dot_general(k, qs_ref[i], _NT, preferred_element_type=F32) + cv
        return s + mask_ref[...] if first else s

    def load_vt(t):
        return vt_ref[:, pl.ds(pl.multiple_of((qi - t) * tk, tk), tk)]

    m0 = tuple(jnp.full((1, tq), NEG_INF, F32) for _ in range(4))
    _attn_pipeline(4, qi, scores, load_vt, m0, (s0_ref, s1_ref), l_ref, acc_ref)

    lp = lam_ref[...]
    lam = (jnp.exp(jnp.sum(lp[0:1, :] * lp[1:2, :], axis=-1, keepdims=True))
           - jnp.exp(jnp.sum(lp[2:3, :] * lp[3:4, :], axis=-1, keepdims=True)) + lam_init)
    o_h = [acc_ref[2 * hd] / l_ref[2 * hd] - lam * (acc_ref[2 * hd + 1] / l_ref[2 * hd + 1]) for hd in range(2)]
    o = jnp.where(_row_mask((LANES, tq), 0, HEAD_LANES), o_h[0], o_h[1]).T
    in_h0 = _lane_mask((tq, LANES), 0, HEAD_LANES)
    sq = o * o
    ms0 = jnp.sum(jnp.where(in_h0, sq, 0.0), axis=-1, keepdims=True)
    ms1 = jnp.sum(jnp.where(in_h0, 0.0, sq), axis=-1, keepdims=True)
    ms = jnp.where(in_h0, ms0, ms1) * (1.0 / HEAD_LANES)
    y = o * lax.rsqrt(ms + RMS_EPS) * sub_ref[...]
    o_ref[...] = (y * (1.0 - lam_init)).astype(o_ref.dtype)


def _diff_attention(a3, avt, slopes_l2, mask_t, krow, lam_p, sub128, layer_idx, tq):
    b, s, _ = a3.shape
    nq = s // tq
    lam_init = 0.8 - 0.6 * math.exp(-0.3 * layer_idx)
    body = functools.partial(_diff_body, tq=tq, lam_init=lam_init)
    grid_spec = pltpu.PrefetchScalarGridSpec(
        num_scalar_prefetch=1,
        grid=(b, 2, nq),
        in_specs=[
            pl.BlockSpec((None, tq, LANES), lambda bi, p, qi, sl: (bi, qi, p)),
            pl.BlockSpec((None, s, LANES), lambda bi, p, qi, sl: (bi, 0, 2 + p)),
            pl.BlockSpec((None, LANES, s), lambda bi, p, qi, sl: (bi, p, 0)),
            pl.BlockSpec((tq, tq), lambda bi, p, qi, sl: (0, 0)),
            pl.BlockSpec((tq, LANES), lambda bi, p, qi, sl: (0, 0)),
            pl.BlockSpec((4, DIFF_HD), lambda bi, p, qi, sl: (0, 0)),
            pl.BlockSpec((1, LANES), lambda bi, p, qi, sl: (0, 0)),
        ],
        out_specs=pl.BlockSpec((None, tq, LANES), lambda bi, p, qi, sl: (bi, qi, p)),
        scratch_shapes=[pltpu.VMEM((4, tq, LANES), BF16), pltpu.VMEM((4, tq, tq), F32),
                        pltpu.VMEM((4, tq, tq), F32), pltpu.VMEM((4, 1, tq), F32),
                        pltpu.VMEM((4, LANES, tq), F32)],
    )
    return pl.pallas_call(
        body, grid_spec=grid_spec,
        out_shape=jax.ShapeDtypeStruct((b, s, 256), BF16),
        compiler_params=_params(("parallel", "parallel", "arbitrary")),
        name="diff_attn",
    )(slopes_l2, a3, a3, avt, mask_t, krow, lam_p, sub128)


def _mla_body(q0_ref, q1_ref, k_ref, vt_ref, mask_ref, o_ref, s0_ref, s1_ref, l_ref, acc_ref, *, tq):
    qi = pl.program_id(2)
    tk = tq
    l_ref[...] = jnp.zeros(l_ref.shape, F32)
    acc_ref[...] = jnp.zeros(acc_ref.shape, F32)
    q_refs = (q0_ref, q1_ref)

    def scores(i, t, first):
        k = k_ref[pl.ds(pl.multiple_of((qi - t) * tk, tk), tk), :]
        s = lax.dot_general(k, q_refs[i][...], _NT, preferred_element_type=F32)
        return s + mask_ref[...] if first else s

    def load_vt(t):
        return vt_ref[:, pl.ds(pl.multiple_of((qi - t) * tk, tk), tk)]

    m0 = tuple(jnp.full((1, tq), NEG_INF, F32) for _ in range(2))
    _attn_pipeline(2, qi, scores, load_vt, m0, (s0_ref, s1_ref), l_ref, acc_ref)
    o_t = jnp.where(_row_mask((LANES, tq), 0, HEAD_LANES), acc_ref[0] / l_ref[0], acc_ref[1] / l_ref[1])
    o_ref[...] = o_t.T.astype(o_ref.dtype)


def _mla_attention(cq3, ck3, cvt, mask_t, tq):
    b, s, _ = cq3.shape
    nq = s // tq
    return pl.pallas_call(
        functools.partial(_mla_body, tq=tq),
        grid=(b, 2, nq),
        in_specs=[
            pl.BlockSpec((None, tq, 256), lambda bi, g, qi: (bi, qi, 2 * g)),
            pl.BlockSpec((None, tq, 256), lambda bi, g, qi: (bi, qi, 2 * g + 1)),
            pl.BlockSpec((None, s, 256), lambda bi, g, qi: (bi, 0, g)),
            pl.BlockSpec((None, LANES, s), lambda bi, g, qi: (bi, g, 0)),
            pl.BlockSpec((tq, tq), lambda bi, g, qi: (0, 0)),
        ],
        out_specs=pl.BlockSpec((None, tq, LANES), lambda bi, g, qi: (bi, qi, g)),
        out_shape=jax.ShapeDtypeStruct((b, s, 256), BF16),
        scratch_shapes=[pltpu.VMEM((2, tq, tq), F32), pltpu.VMEM((2, tq, tq), F32),
                        pltpu.VMEM((2, 1, tq), F32), pltpu.VMEM((2, LANES, tq), F32)],
        compiler_params=_params(("parallel", "parallel", "arbitrary")),
        name="mla_attn",
    )(cq3, cq3, ck3, cvt, mask_t)


def _band_body(m0_ref, q_ref, k_ref, vt_ref, bias_ref, o_ref, qs_ref, s0_ref, s1_ref, l_ref, acc_ref,
               *, tq, n_off, l_init):
    slab, qi = pl.program_id(1), pl.program_id(2)
    tk = tq
    q = q_ref[...].astype(F32)
    for i in range(2):
        qs_ref[i] = jnp.where(_lane_mask(q.shape, i * HEAD_LANES, (i + 1) * HEAD_LANES), q, 0.0).astype(BF16)
    l_ref[...] = jnp.full(l_ref.shape, l_init, F32)
    acc_ref[...] = jnp.zeros(acc_ref.shape, F32)

    def scores(i, t, first):
        k = k_ref[pl.ds(pl.multiple_of((qi - t) * tk, tk), tk), :]
        return lax.dot_general(k, qs_ref[i], _NT, preferred_element_type=F32) + bias_ref[i, t]

    def load_vt(t):
        return vt_ref[:, pl.ds(pl.multiple_of((qi - t) * tk, tk), tk)]

    m0 = tuple(jnp.full((1, tq), m0_ref[2 * slab + i], F32) for i in range(2))
    _attn_pipeline(2, jnp.minimum(qi, n_off - 1), scores, load_vt, m0, (s0_ref, s1_ref), l_ref, acc_ref)
    o_t = jnp.where(_row_mask((LANES, tq), 0, HEAD_LANES), acc_ref[0] / l_ref[0], acc_ref[1] / l_ref[1])
    o_ref[...] = o_t.T.astype(o_ref.dtype)


def _band_attention(x3, xvt, k_blk_of, vt_blk_of, bias, m0, l_init, tq, name):
    b, s, _ = x3.shape
    nq = s // tq
    n_off = bias.shape[2]
    grid_spec = pltpu.PrefetchScalarGridSpec(
        num_scalar_prefetch=1,
        grid=(b, 2, nq),
        in_specs=[
            pl.BlockSpec((None, tq, LANES), lambda bi, p, qi, m0r: (bi, qi, p)),
            pl.BlockSpec((None, s, LANES), lambda bi, p, qi, m0r: (bi, 0, k_blk_of(p))),
            pl.BlockSpec((None, LANES, s), lambda bi, p, qi, m0r: (bi, vt_blk_of(p), 0)),
            pl.BlockSpec((None, 2, n_off, tq, tq), lambda bi, p, qi, m0r: (p, 0, 0, 0, 0)),
        ],
        out_specs=pl.BlockSpec((None, tq, LANES), lambda bi, p, qi, m0r: (bi, qi, p)),
        scratch_shapes=[pltpu.VMEM((2, tq, LANES), BF16), pltpu.VMEM((2, tq, tq), F32),
                        pltpu.VMEM((2, tq, tq), F32), pltpu.VMEM((2, 1, tq), F32),
                        pltpu.VMEM((2, LANES, tq), F32)],
    )
    return pl.pallas_call(
        functools.partial(_band_body, tq=tq, n_off=n_off, l_init=l_init),
        grid_spec=grid_spec,
        out_shape=jax.ShapeDtypeStruct((b, s, 256), BF16),
        compiler_params=_params(("parallel", "parallel", "arbitrary")),
        name=name,
    )(m0, x3, x3, xvt, bias)


def _band_bias(slopes, tq, n_off, mult_fn):
    di = jnp.arange(tq, dtype=jnp.int32)[None, :]
    dj = jnp.arange(tq, dtype=jnp.int32)[:, None]
    off = jnp.arange(n_off, dtype=jnp.int32)[:, None, None]
    delta = off * tq + di - dj
    mult = mult_fn(delta)
    sl = jnp.asarray(slopes, F32)[:, None, None, None] * LOG2E
    val = -sl * delta.astype(F32)[None] + jnp.log2(jnp.maximum(mult, 1).astype(F32))[None]
    return jnp.where((mult > 0)[None], val, NEG_INF)


def _dil_mult(delta):
    m = jnp.zeros(delta.shape, jnp.int32)
    for window, d in DIL_PATTERNS:
        m = m + ((delta >= 0) & (delta <= window) & (delta % d == 0)).astype(jnp.int32)
    return m


def _swa_mult(delta):
    return ((delta >= 0) & (delta <= SWA_WINDOW - 1)).astype(jnp.int32)


def _outproj_body(*refs, with_router):
    if with_router:
        (oa, ob, oc, od, w_ref, x_ref, g_ref, rh_ref, rl_ref, x1_ref, h2_ref, lg_ref) = refs
    else:
        (oa, ob, oc, od, w_ref, x_ref, g_ref, x1_ref, h2_ref) = refs
    acc = x_ref[...]
    for i, o in enumerate((oa, ob, oc, od)):
        acc = acc + jnp.dot(o[...], w_ref[i], preferred_element_type=F32)
    x1_ref[...] = acc
    h2 = acc * lax.rsqrt(jnp.mean(acc * acc, axis=-1, keepdims=True) + RMS_EPS) * g_ref[...]
    hi = h2.astype(BF16)
    if with_router:
        for c in range(SC_CHUNKS):
            h2_ref[c] = h2[:, c * SC_COLS:(c + 1) * SC_COLS]
        lo = (h2 - hi.astype(F32)).astype(BF16)
        lg_ref[...] = (jnp.dot(hi, rh_ref[...], preferred_element_type=F32)
                       + (jnp.dot(hi, rl_ref[...], preferred_element_type=F32)
                          + jnp.dot(lo, rh_ref[...], preferred_element_type=F32)))
    else:
        h2_ref[...] = hi


def _outproj(oa, ob, oc, od, w4, x2d, g, router_hl=None):
    t = x2d.shape[0]
    tm = _pick(t, 512)
    with_router = router_hl is not None
    full = lambda shape: pl.BlockSpec(shape, lambda i: (0,) * len(shape))
    row = lambda w: pl.BlockSpec((tm, w), lambda i: (i, 0))
    in_specs = [row(256)] * 4 + [full((4, 256, D_MODEL)), row(D_MODEL), full((1, D_MODEL))]
    args = [oa, ob, oc, od, w4, x2d, g]
    out_specs = [row(D_MODEL), row(D_MODEL)]
    out_shape = [jax.ShapeDtypeStruct((t, D_MODEL), F32), jax.ShapeDtypeStruct((t, D_MODEL), BF16)]
    if with_router:
        out_specs[1] = pl.BlockSpec((SC_CHUNKS, tm, SC_COLS), lambda i: (0, i, 0))
        out_shape[1] = jax.ShapeDtypeStruct((SC_CHUNKS, t, SC_COLS), F32)
        in_specs += [full((D_MODEL, LANES))] * 2
        args += list(router_hl)
        out_specs.append(row(LANES))
        out_shape.append(jax.ShapeDtypeStruct((t, LANES), F32))
    return pl.pallas_call(
        functools.partial(_outproj_body, with_router=with_router),
        grid=(t // tm,), in_specs=in_specs, out_specs=out_specs, out_shape=out_shape,
        compiler_params=_params(("parallel",)),
        name="outproj",
    )(*args)


def _ffn_body(h_ref, w1_ref, w3_ref, w2_ref, x_ref, o_ref, acc_ref):
    f = pl.program_id(1)
    h = h_ref[...]
    a = jnp.dot(h, w1_ref[...], preferred_element_type=F32)
    b = jnp.dot(h, w3_ref[...], preferred_element_type=F32)
    mid = (a * jax.nn.sigmoid(a) * b).astype(BF16)
    contrib = jnp.dot(mid, w2_ref[...], preferred_element_type=F32)

    @pl.when(f == 0)
    def _():
        acc_ref[...] = contrib

    @pl.when(f > 0)
    def _():
        acc_ref[...] += contrib

    @pl.when(f == pl.num_programs(1) - 1)
    def _():
        o_ref[...] = x_ref[...] + acc_ref[...]


def _ffn(h2, w1, w3, w2, x1):
    t = h2.shape[0]
    tm = _pick(t, 512)
    tf = D_FF // 2
    return pl.pallas_call(
        _ffn_body,
        grid=(t // tm, D_FF // tf),
        in_specs=[pl.BlockSpec((tm, D_MODEL), lambda i, f: (i, 0)),
                  pl.BlockSpec((D_MODEL, tf), lambda i, f: (0, f)),
                  pl.BlockSpec((D_MODEL, tf), lambda i, f: (0, f)),
                  pl.BlockSpec((tf, D_MODEL), lambda i, f: (f, 0)),
                  pl.BlockSpec((tm, D_MODEL), lambda i, f: (i, 0))],
        out_specs=pl.BlockSpec((tm, D_MODEL), lambda i, f: (i, 0)),
        out_shape=jax.ShapeDtypeStruct((t, D_MODEL), F32),
        scratch_shapes=[pltpu.VMEM((tm, D_MODEL), F32)],
        compiler_params=_params(("parallel", "arbitrary")),
        name="ffn",
    )(h2, w1, w3, w2, x1)


def _route_body(lg_ref, tri_ref, info_ref, cnt_ref, carry_ref):
    i = pl.program_id(0)

    @pl.when(i == 0)
    def _():
        carry_ref[...] = jnp.zeros(carry_ref.shape, F32)

    lg = lg_ref[...]
    lane = lax.broadcasted_iota(jnp.int32, lg.shape, 1)
    lg = jnp.where(lane < N_EXPERTS, lg, -jnp.inf)
    m1 = jnp.max(lg, axis=-1, keepdims=True)
    i1 = jnp.min(jnp.where(lg == m1, lane, LANES), axis=-1, keepdims=True)
    oh1 = lane == i1
    lg2 = jnp.where(oh1, -jnp.inf, lg)
    m2 = jnp.max(lg2, axis=-1, keepdims=True)
    i2 = jnp.min(jnp.where(lg2 == m2, lane, LANES), axis=-1, keepdims=True)
    oh2 = lane == i2
    e2 = jnp.exp(m2 - m1)
    g1 = 1.0 / (1.0 + e2)
    g2 = e2 / (1.0 + e2)
    oh = jnp.where(jnp.logical_or(oh1, oh2), 1.0, 0.0)
    rank = jnp.dot(tri_ref[...], oh.astype(BF16), preferred_element_type=F32) + carry_ref[...]
    r1 = jnp.sum(jnp.where(oh1, rank, 0.0), axis=-1, keepdims=True)
    r2 = jnp.sum(jnp.where(oh2, rank, 0.0), axis=-1, keepdims=True)
    carry_ref[...] += jnp.sum(oh, axis=0, keepdims=True)
    cnt_ref[...] = carry_ref[...]
    info = jnp.zeros(lg.shape, F32)
    for c, val in enumerate((i1.astype(F32), i2.astype(F32), r1, r2, g1, g2)):
        info = jnp.where(lane == c, val, info)
    info_ref[...] = info


def _route(logits):
    t = logits.shape[0]
    tm = _pick(t, 512)
    tri = (jnp.arange(tm)[:, None] > jnp.arange(tm)[None, :]).astype(BF16)
    return pl.pallas_call(
        _route_body,
        grid=(t // tm,),
        in_specs=[pl.BlockSpec((tm, LANES), lambda i: (i, 0)), pl.BlockSpec((tm, tm), lambda i: (0, 0))],
        out_specs=[pl.BlockSpec((tm, LANES), lambda i: (i, 0)), pl.BlockSpec((1, LANES), lambda i: (0, 0))],
        out_shape=[jax.ShapeDtypeStruct((t, LANES), F32), jax.ShapeDtypeStruct((1, LANES), F32)],
        scratch_shapes=[pltpu.VMEM((1, LANES), F32)],
        compiler_params=_params(("arbitrary",)),
        name="route",
    )(logits, tri)


def _moe_ffn_body(be_ref, x_ref, w1_ref, w3_ref, w2_ref, o_ref, acc_ref):
    f = pl.program_id(1)
    x = jnp.concatenate([x_ref[c] for c in range(SC_CHUNKS)], axis=1).astype(BF16)
    a = jnp.dot(x, w1_ref[...], preferred_element_type=F32)
    b = jnp.dot(x, w3_ref[...], preferred_element_type=F32)
    mid = (a * jax.nn.sigmoid(a) * b).astype(BF16)
    contrib = jnp.dot(mid, w2_ref[...], preferred_element_type=F32)

    @pl.when(f == 0)
    def _():
        acc_ref[...] = contrib

    @pl.when(f > 0)
    def _():
        acc_ref[...] += contrib

    @pl.when(f == pl.num_programs(1) - 1)
    def _():
        for c in range(SC_CHUNKS):
            o_ref[c] = acc_ref[:, c * SC_COLS:(c + 1) * SC_COLS]


def _moe_ffn(xs, blk_expert, w1, w3, w2, bm):
    rows = xs.shape[1]
    tf = D_FF_EXPERT // 4
    chunked = pl.BlockSpec((SC_CHUNKS, bm, SC_COLS), lambda i, f, be: (0, i, 0))
    grid_spec = pltpu.PrefetchScalarGridSpec(
        num_scalar_prefetch=1,
        grid=(rows // bm, D_FF_EXPERT // tf),
        in_specs=[chunked,
                  pl.BlockSpec((None, D_MODEL, tf), lambda i, f, be: (be[i], 0, f)),
                  pl.BlockSpec((None, D_MODEL, tf), lambda i, f, be: (be[i], 0, f)),
                  pl.BlockSpec((None, tf, D_MODEL), lambda i, f, be: (be[i], f, 0))],
        out_specs=chunked,
        scratch_shapes=[pltpu.VMEM((bm, D_MODEL), F32)],
    )
    return pl.pallas_call(
        _moe_ffn_body, grid_spec=grid_spec,
        out_shape=jax.ShapeDtypeStruct(xs.shape, F32),
        compiler_params=_params(("parallel", "arbitrary")),
        name="moe_ffn",
    )(blk_expert, xs, w1, w3, w2)


def _final_body(*refs, with_moe):
    if with_moe:
        x_ref, y0_ref, y1_ref, info_ref, g_ref, o_ref = refs
        info = info_ref[...]
        y0 = jnp.concatenate([y0_ref[c] for c in range(SC_CHUNKS)], axis=1)
        y1 = jnp.concatenate([y1_ref[c] for c in range(SC_CHUNKS)], axis=1)
        x = x_ref[...] + (info[:, 4:5] * y0 + info[:, 5:6] * y1)
    else:
        x_ref, g_ref, o_ref = refs
        x = x_ref[...]
    o_ref[...] = x * lax.rsqrt(jnp.mean(x * x, axis=-1, keepdims=True) + RMS_EPS) * g_ref[...]


def _final(x2d, g, moe=None):
    t = x2d.shape[0]
    tm = _pick(t, 512)
    row = lambda w: pl.BlockSpec((tm, w), lambda i: (i, 0))
    in_specs, args = [row(D_MODEL)], [x2d]
    if moe is not None:
        yg, info = moe
        slot = lambda k: pl.BlockSpec((None, SC_CHUNKS, tm, SC_COLS), lambda i: (k, 0, i, 0))
        in_specs += [slot(0), slot(1), row(LANES)]
        args += [yg, yg, info]
    in_specs.append(pl.BlockSpec((1, D_MODEL), lambda i: (0, 0)))
    args.append(g)
    return pl.pallas_call(
        functools.partial(_final_body, with_moe=moe is not None),
        grid=(t // tm,), in_specs=in_specs, out_specs=row(D_MODEL),
        out_shape=jax.ShapeDtypeStruct((t, D_MODEL), F32),
        compiler_params=_params(("parallel",)),
        name="final",
    )(*args)


def _prep_attn_weights(w_in, w_out, w_uq, w_ukv):
    widths = (256, 256, 256, 256, 256, 256, MLA_Q_LORA, MLA_KV_LORA, MLA_ROPE, 256, 128, 128)
    offs = np.concatenate([[0], np.cumsum(widths)])
    qa, ka, va, qb, kb, vb, cq, ckv, kpe, qd, kd, vd = [w_in[:, offs[i]:offs[i + 1]] for i in range(12)]
    hperm = np.array([0, 2, 1, 3])
    qd = qd.reshape(D_MODEL, 4, SWA_HD)[:, hperm].reshape(D_MODEL, 256)
    pad = jnp.zeros((D_MODEL, LANES - MLA_ROPE), w_in.dtype)
    half = MLA_ROPE // 2
    kpe_sw = jnp.concatenate([kpe[:, half:], kpe[:, :half]], axis=1)
    w_main = jnp.concatenate([qa, ka, va, qb, kb, vb, qd, kd, vd, cq, ckv, kpe, pad, kpe_sw, pad], axis=1)

    uq = w_uq.reshape(MLA_Q_LORA, MLA_HEADS, MLA_NOPE + MLA_ROPE)
    z = lambda n: jnp.zeros((MLA_Q_LORA, n), w_uq.dtype)
    mains, sws = [], []
    for hd in range(MLA_HEADS):
        nope, rope = uq[:, hd, :MLA_NOPE], uq[:, hd, MLA_NOPE:]
        nope128 = jnp.concatenate([nope, z(64)] if hd % 2 == 0 else [z(64), nope], axis=1)
        mains.append(jnp.concatenate([rope, z(LANES - MLA_ROPE), nope128], axis=1))
        sws.append(jnp.concatenate([rope[:, half:], rope[:, :half], z(LANES - MLA_ROPE)], axis=1))
    wq_big = jnp.concatenate(mains + sws, axis=1)

    ukv = w_ukv.reshape(MLA_KV_LORA, MLA_HEADS, MLA_NOPE + MLA_V)
    wkv = jnp.concatenate([ukv[:, :, :MLA_NOPE].reshape(MLA_KV_LORA, 256),
                           ukv[:, :, MLA_NOPE:].reshape(MLA_KV_LORA, 256)], axis=1)

    wo = w_out.reshape(4, 256, D_MODEL)
    wo_d = wo[3].reshape(4, SWA_HD, D_MODEL)[hperm].reshape(256, D_MODEL)
    w4 = jnp.stack([wo[0], wo[1], wo[2], wo_d])
    return w_main.astype(BF16), wq_big.astype(BF16), wkv.astype(BF16), w4.astype(BF16)


def _rope_tables(s):
    inv = ROPE_THETA ** (-jnp.arange(0, MLA_ROPE, 2, dtype=F32) / MLA_ROPE)
    ang = jnp.arange(s, dtype=F32)[:, None] * inv[None, :]
    cos, sin = jnp.cos(ang), jnp.sin(ang)
    z = jnp.zeros((s, LANES - MLA_ROPE), F32)
    cos128 = jnp.concatenate([cos, cos, z], axis=1)
    sin128 = jnp.concatenate([-sin, sin, z], axis=1)
    c_mla = (MLA_NOPE + MLA_ROPE) ** -0.5 * LOG2E
    return cos128 * c_mla, sin128 * c_mla, cos128, sin128


def _col_scale():
    ca = DIFF_HD ** -0.5 * LOG2E
    cb = DIL_HD ** -0.5 * LOG2E
    cd = SWA_HD ** -0.5 * LOG2E
    v = np.ones((1, N_SCALED), np.float32)
    v[0, 0:256] = ca
    v[0, 768:1024] = cb
    v[0, 1536:1792] = cd
    return jnp.asarray(v)


def _moe_layer(h2, logits, w1, w3, w2, bm):
    t = logits.shape[0]
    info, counts = _route(logits)
    cnt = counts[0, :N_EXPERTS].astype(jnp.int32)
    padded = (cnt + bm - 1) // bm * bm
    pends = jnp.cumsum(padded)
    pstarts = pends - padded
    e12 = info[:, 0:2].astype(jnp.int32)
    pos = pstarts[e12] + info[:, 2:4].astype(jnp.int32)
    nb = (t * TOP_K) // bm + N_EXPERTS
    rows = nb * bm
    blk_start = jnp.arange(nb, dtype=jnp.int32) * bm
    blk_expert = jnp.minimum(jnp.sum((pends[None, :] <= blk_start[:, None]).astype(jnp.int32), axis=1),
                             N_EXPERTS - 1)
    sub = (jnp.arange(SC_CHUNKS, dtype=jnp.int32) * rows)[None, :, None] + pos.T[:, None, :]
    xs = _sc_scatter_rows(h2.reshape(SC_CHUNKS * t, SC_COLS),
                          [sub[k].reshape(1, SC_CHUNKS * t) for k in range(TOP_K)], SC_CHUNKS * rows)
    ys = _moe_ffn(xs.reshape(SC_CHUNKS, rows, SC_COLS), blk_expert.astype(jnp.int32), w1, w3, w2, bm)
    yg = _sc_gather_rows(ys.reshape(SC_CHUNKS * rows, SC_COLS), sub.reshape(1, TOP_K * SC_CHUNKS * t))
    return yg.reshape(TOP_K, SC_CHUNKS, t, SC_COLS), info


SC_WINDOW = 128
SC_COLS = 256
SC_CHUNKS = D_MODEL // SC_COLS


def _sc_mesh():
    return plsc.VectorSubcoreMesh(core_axis_name="c", subcore_axis_name="s")


def _sc_scatter_rows(x, idx_list, rows):
    t, d = x.shape
    n = len(idx_list)

    @functools.partial(pl.kernel, out_type=jax.ShapeDtypeStruct((rows, d), x.dtype), mesh=_sc_mesh(),
                       scratch_types=[])
    def scatter_kernel(x_hbm, *rest):
        idx_hbms, o_hbm = rest[:n], rest[n]

        def body(x_vmem, *idx_vmems):
            for iv in idx_vmems:
                pltpu.sync_copy(x_vmem, o_hbm.at[iv.at[0]])

        pltpu.emit_pipeline(
            body, grid=(t // SC_WINDOW,),
            in_specs=[pl.BlockSpec((SC_WINDOW, d), lambda i: (i, 0))]
            + [pl.BlockSpec((1, SC_WINDOW), lambda i: (0, i))] * n,
            out_specs=[], core_axis_name=("c", "s"), dimension_semantics=(pltpu.PARALLEL,),
        )(x_hbm, *idx_hbms)

    return scatter_kernel(x, *idx_list)


def _sc_gather_rows(y, idx):
    m = idx.shape[1]
    d = y.shape[1]

    @functools.partial(pl.kernel, out_type=jax.ShapeDtypeStruct((m, d), y.dtype), mesh=_sc_mesh(),
                       scratch_types=[])
    def gather_kernel(y_hbm, idx_hbm, o_hbm):
        def body(idx_vmem, o_vmem):
            pltpu.sync_copy(y_hbm.at[idx_vmem.at[0]], o_vmem)

        pltpu.emit_pipeline(
            body, grid=(m // SC_WINDOW,),
            in_specs=[pl.BlockSpec((1, SC_WINDOW), lambda i: (0, i))],
            out_specs=[pl.BlockSpec((SC_WINDOW, d), lambda i: (i, 0))],
            core_axis_name=("c", "s"), dimension_semantics=(pltpu.PARALLEL,),
        )(idx_hbm, o_hbm)

    return gather_kernel(y, idx)


def kernel(x, attn_norm, w_in, w_out, diff_lambda, diff_subln, mla_q_norm, mla_w_uq, mla_kv_norm, mla_w_ukv,
           swa_sinks, ffn_norm, ffn_w1, ffn_w3, ffn_w2, moe_router, moe_w1, moe_w3, moe_w2, final_norm):
    bsz, seq, _ = x.shape
    t = bsz * seq
    depth = w_in.shape[0]
    slopes_a, slopes_b, slopes_d = _alibi_slopes()
    tq_dense = _pick(seq, 512)
    tq_band = _pick(seq, 256)

    tabs = _rope_tables(seq)
    colscale = _col_scale()
    di = jnp.arange(tq_dense)
    causal_mask_t = jnp.where(di[:, None] <= di[None, :], 0.0, NEG_INF).astype(F32)
    krow = jnp.broadcast_to(di.astype(F32)[:, None], (tq_dense, LANES))
    n_off_dil = min(max(w for w, _ in DIL_PATTERNS) // tq_band + 1, seq // tq_band)
    n_off_swa = min((SWA_WINDOW - 1 + tq_band - 1) // tq_band + 1, seq // tq_band)
    dil_bias = _band_bias(slopes_b, tq_band, n_off_dil, _dil_mult).reshape(2, 2, n_off_dil, tq_band, tq_band)
    swa_bias = _band_bias(slopes_d[np.array([0, 2, 1, 3])], tq_band, n_off_swa, _swa_mult
                          ).reshape(2, 2, n_off_swa, tq_band, tq_band)
    slopes_a_l2 = jnp.asarray(slopes_a * LOG2E, F32)
    dil_m0 = jnp.full((4,), NEG_INF, F32)

    x2d = x.reshape(t, D_MODEL)
    for l in range(depth):
        w_main, wq_big, wkv, w4 = _prep_attn_weights(w_in[l], w_out[l], mla_w_uq[l], mla_w_ukv[l])
        a, avt, b, bvt, d, dvt, cq, ck, cvt = _proj(x2d, attn_norm[l][None], w_main, colscale, mla_q_norm[l][None],
                                                    wq_big, mla_kv_norm[l][None], wkv, tabs, seq)
        a3, b3, d3 = (v.reshape(bsz, seq, -1) for v in (a, b, d))
        sub128 = jnp.concatenate([diff_subln[l], diff_subln[l]])[None]
        o_a = _diff_attention(a3, avt, slopes_a_l2, causal_mask_t, krow, diff_lambda[l], sub128, l, tq_dense)
        o_b = _band_attention(b3, bvt, lambda p: 2 + p, lambda p: p, dil_bias, dil_m0, 0.0, tq_band, "dil_attn")
        o_c = _mla_attention(cq.reshape(bsz, seq, -1), ck.reshape(bsz, seq, -1), cvt, causal_mask_t, tq_dense)
        swa_m0 = (swa_sinks[l].astype(F32) * LOG2E)[np.array([0, 2, 1, 3])]
        o_d = _band_attention(d3, dvt, lambda p: 2, lambda p: 0, swa_bias, swa_m0, 1.0, tq_band, "swa_attn")
        outs = [v.reshape(t, 256) for v in (o_a, o_b, o_c, o_d)]
        j = l // 2
        if l % 2 == 0:
            x1, h2 = _outproj(*outs, w4, x2d, ffn_norm[l][None])
            x2d = _ffn(h2, ffn_w1[j].astype(BF16), ffn_w3[j].astype(BF16), ffn_w2[j].astype(BF16), x1)
            moe = None
        else:
            r = jnp.pad(moe_router[j], ((0, 0), (0, LANES - N_EXPERTS)))
            r_hi = r.astype(BF16)
            r_lo = (r - r_hi.astype(F32)).astype(BF16)
            x1, h2, logits = _outproj(*outs, w4, x2d, ffn_norm[l][None], (r_hi, r_lo))
            bm = _pick(t, 512)
            moe = _moe_layer(h2, logits, moe_w1[j].astype(BF16), moe_w3[j].astype(BF16),
                             moe_w2[j].astype(BF16), bm)
            x2d = x1
        if l < depth - 1 and moe is not None:
            raise NotImplementedError("MoE combine is fused into the final norm; MoE layer must be last")
    return _final(x2d, final_norm[None], moe).reshape(bsz, seq, D_MODEL)
```

```python
import functools
import math

import numpy as np
import jax
import jax.numpy as jnp
from jax import lax
from jax.experimental import pallas as pl
from jax.experimental.pallas import tpu as pltpu
from jax.experimental.pallas import tpu_sc as plsc

D_MODEL = 1024
DIFF_HEADS, DIFF_HD = 4, 32
DIL_HEADS, DIL_HD = 4, 64
DIL_PATTERNS = ((128, 1), (512, 4), (2048, 16))
MLA_HEADS, MLA_Q_LORA, MLA_KV_LORA, MLA_NOPE, MLA_ROPE, MLA_V = 4, 384, 128, 64, 32, 64
ROPE_THETA = 10000.0
SWA_HEADS, SWA_KV_HEADS, SWA_HD, SWA_WINDOW = 4, 2, 64, 128
D_FF = 2816
N_EXPERTS, TOP_K, D_FF_EXPERT = 8, 2, 3584
RMS_EPS = 1e-6
NEG_INF = -1e30
N_ALIBI = DIFF_HEADS + DIL_HEADS + SWA_HEADS

LOG2E = 1.4426950408889634
LANES = 128
HEAD_LANES = 64
VMEM_LIMIT = 56 * 1024 * 1024

F32 = jnp.float32
BF16 = jnp.bfloat16

_NT = (((1,), (1,)), ((), ()))


def _params(sem, vmem=VMEM_LIMIT):
    return pltpu.CompilerParams(dimension_semantics=sem, vmem_limit_bytes=vmem)


def _alibi_slopes():
    s = 2.0 ** (-8.0 * (np.arange(N_ALIBI) + 1) / N_ALIBI)
    return s[0::3], s[1::3], s[2::3]


def _pick(n, pref):
    t = min(pref, n)
    while n % t:
        t //= 2
    return t


N_SCALED = 2048
C_CQ, C_CKV, C_KPE, C_KPE_SW, N_MAIN = 2048, 2432, 2560, 2688, 2816


def _vt_with_ones(v):
    vt = v.T
    ones = jnp.ones((HEAD_LANES, v.shape[0]), F32)
    parts = []
    for pair in range(v.shape[1] // LANES):
        h0 = vt[pair * LANES:pair * LANES + HEAD_LANES]
        h1 = vt[pair * LANES + HEAD_LANES:(pair + 1) * LANES]
        parts += [h0, ones, ones, h1]
    return jnp.concatenate(parts, axis=0).astype(BF16)


def _proj_body(x_ref, g_ref, w_ref, cs_ref, qn_ref, wq_ref, kn_ref, wkv_ref,
               cosq_ref, sinq_ref, cosk_ref, sink_ref,
               a_ref, avt_ref, b_ref, bvt_ref, d_ref, dvt_ref, cq_ref, ck_ref, cvt_ref):
    x = x_ref[...]
    ms = jnp.mean(x * x, axis=-1, keepdims=True)
    h = (x * lax.rsqrt(ms + RMS_EPS) * g_ref[...]).astype(BF16)
    acc = jnp.dot(h, w_ref[...], preferred_element_type=F32)
    sc = acc[:, :N_SCALED] * cs_ref[...]
    a_ref[...] = sc[:, 0:512].astype(BF16)
    avt_ref[...] = _vt_with_ones(sc[:, 512:768])
    b_ref[...] = sc[:, 768:1280].astype(BF16)
    bvt_ref[...] = _vt_with_ones(sc[:, 1280:1536])
    d_ref[...] = sc[:, 1536:1920].astype(BF16)
    dvt_ref[...] = _vt_with_ones(sc[:, 1920:2048])

    cq = acc[:, C_CQ:C_CKV]
    hq = (cq * lax.rsqrt(jnp.mean(cq * cq, axis=-1, keepdims=True) + RMS_EPS) * qn_ref[...]).astype(BF16)
    yq = jnp.dot(hq, wq_ref[...], preferred_element_type=F32)
    cosq, sinq = cosq_ref[...], sinq_ref[...]
    c_mla = (MLA_NOPE + MLA_ROPE) ** -0.5 * LOG2E
    for hd in range(MLA_HEADS):
        main = yq[:, hd * 256:(hd + 1) * 256]
        sw = yq[:, 1024 + hd * LANES:1024 + (hd + 1) * LANES]
        cq_ref[:, hd * 256:hd * 256 + LANES] = (main[:, :LANES] * cosq + sw * sinq).astype(BF16)
        cq_ref[:, hd * 256 + LANES:(hd + 1) * 256] = (main[:, LANES:] * c_mla).astype(BF16)

    ckv = acc[:, C_CKV:C_KPE]
    hk = (ckv * lax.rsqrt(jnp.mean(ckv * ckv, axis=-1, keepdims=True) + RMS_EPS) * kn_ref[...]).astype(BF16)
    kv = jnp.dot(hk, wkv_ref[...], preferred_element_type=F32)
    kr = (acc[:, C_KPE:C_KPE_SW] * cosk_ref[...] + acc[:, C_KPE_SW:N_MAIN] * sink_ref[...]).astype(BF16)
    for grp in range(2):
        ck_ref[:, grp * 256:grp * 256 + LANES] = kr
        ck_ref[:, grp * 256 + LANES:(grp + 1) * 256] = kv[:, grp * LANES:(grp + 1) * LANES].astype(BF16)
    cvt_ref[...] = _vt_with_ones(kv[:, 256:512])


def _proj(x2d, g, w_main, colscale, qn, wq_big, kn, wkv, tabs, seq):
    t = x2d.shape[0]
    tm = _pick(seq, 512)
    nsb = seq // tm
    full = lambda shape: pl.BlockSpec(shape, lambda i: (0,) * len(shape))
    tab = pl.BlockSpec((tm, LANES), lambda i: (i % nsb, 0))
    row = lambda w: pl.BlockSpec((tm, w), lambda i: (i, 0))
    vt = lambda w: pl.BlockSpec((None, w, tm), lambda i: (i // nsb, 0, i % nsb))
    row_sds = lambda w: jax.ShapeDtypeStruct((t, w), BF16)
    vt_sds = lambda w: jax.ShapeDtypeStruct((t // seq, w, seq), BF16)
    return pl.pallas_call(
        _proj_body,
        grid=(t // tm,),
        in_specs=[row(D_MODEL), full((1, D_MODEL)), full((D_MODEL, N_MAIN)), full((1, N_SCALED)),
                  full((1, MLA_Q_LORA)), full((MLA_Q_LORA, 1536)), full((1, MLA_KV_LORA)),
                  full((MLA_KV_LORA, 512)), tab, tab, tab, tab],
        out_specs=[row(512), vt(512), row(512), vt(512), row(384), vt(256), row(1024), row(512), vt(512)],
        out_shape=[row_sds(512), vt_sds(512), row_sds(512), vt_sds(512), row_sds(384), vt_sds(256),
                   row_sds(1024), row_sds(512), vt_sds(512)],
        compiler_params=_params(("parallel",)),
        name="proj",
    )(x2d, g, w_main, colscale, qn, wq_big, kn, wkv, *tabs)


def _attn_pipeline(n_soft, n_steps, scores, shift, load_vt, m0, s_ref, acc_ref):
    def stage_a(t, ss, dst_ref, m_cur):
        m_next, alpha = [], []
        for i in range(n_soft):
            m_new = jnp.maximum(m_cur[i], jnp.max(ss[i], axis=0, keepdims=True) + shift(i, t))
            dst_ref[i] = ss[i]
            m_next.append(m_new)
            alpha.append(jnp.exp2(m_cur[i] - m_new))
        return tuple(m_next), tuple(alpha)

    def stage_b(t, src_ref, carry):
        m_cur, alpha = carry
        for i in range(n_soft):
            p = jnp.exp2(src_ref[i] - (m_cur[i] - shift(i, t)))
            acc_ref[i] = alpha[i] * acc_ref[i] + jnp.dot(load_vt(t, i), p.astype(BF16),
                                                         preferred_element_type=F32)

    def half(t, src_ref, dst_ref, carry):
        ss = [scores(i, t + 1, False) for i in range(n_soft)]
        stage_b(t, src_ref, carry)
        return stage_a(t + 1, ss, dst_ref, carry[0])

    def body(u, carry):
        carry = half(2 * u, s_ref[0], s_ref[1], carry)
        return half(2 * u + 1, s_ref[1], s_ref[0], carry)

    carry = stage_a(0, [scores(i, 0, True) for i in range(n_soft)], s_ref[0], m0)
    carry = lax.fori_loop(0, n_steps // 2, body, carry)

    @pl.when(n_steps % 2 == 1)
    def _():
        stage_b(n_steps, s_ref[1], half(n_steps - 1, s_ref[0], s_ref[1], carry))

    @pl.when(n_steps % 2 == 0)
    def _():
        stage_b(n_steps, s_ref[0], carry)


def _row_mask(shape, lo, hi):
    row = lax.broadcasted_iota(jnp.int32, shape, 0)
    return jnp.logical_and(row >= lo, row < hi)


def _lane_mask(shape, lo, hi):
    lane = lax.broadcasted_iota(jnp.int32, shape, len(shape) - 1)
    return jnp.logical_and(lane >= lo, lane < hi)


def _denominator(acc_ref, i, head):
    row = HEAD_LANES if head == 0 else 0
    return acc_ref[i, row:row + 1, :]


def _diff_body(sl_ref, q_ref, k_ref, vt_ref, mask_ref, kb_ref, lam_ref, sub_ref, o_ref,
               qs_ref, s0_ref, s1_ref, acc_ref, *, tq, lam_init):
    pair, qi = pl.program_id(1), pl.program_id(2)
    tk = tq
    q = q_ref[...].astype(F32)
    for hd in range(2):
        sel = jnp.where(_lane_mask(q.shape, N_BIAS_SPLIT * hd, N_BIAS_SPLIT * (hd + 1)), 1.0, 0.0).astype(BF16)
        for mp in range(2):
            lo = hd * HEAD_LANES + mp * DIFF_HD
            qs_ref[2 * hd + mp, :, 0:LANES] = jnp.where(_lane_mask(q.shape, lo, lo + DIFF_HD), q, 0.0).astype(BF16)
            qs_ref[2 * hd + mp, :, LANES:2 * LANES] = sel
    acc_ref[...] = jnp.zeros(acc_ref.shape, F32)

    def scores(i, t, first):
        off = pl.multiple_of((qi - t) * tk, tk)
        k = jnp.concatenate([k_ref[pl.ds(off, tk), :], kb_ref[...]], axis=1)
        s = lax.dot_general(k, qs_ref[i], _NT, preferred_element_type=F32)
        return s + mask_ref[...] if first else s

    def shift(i, t):
        return sl_ref[2 * pair + i // 2] * (-(t * tk)).astype(F32) if not isinstance(t, int) else 0.0

    def load_vt(t, i):
        return vt_ref[pl.ds((i // 2) * LANES, LANES), pl.ds(pl.multiple_of((qi - t) * tk, tk), tk)]

    m0 = tuple(jnp.full((1, tq), NEG_INF, F32) for _ in range(4))
    _attn_pipeline(4, qi, scores, shift, load_vt, m0, (s0_ref, s1_ref), acc_ref)

    lp = lam_ref[...]
    lam = (jnp.exp(jnp.sum(lp[0:1, :] * lp[1:2, :], axis=-1, keepdims=True))
           - jnp.exp(jnp.sum(lp[2:3, :] * lp[3:4, :], axis=-1, keepdims=True)) + lam_init)
    o_h = [acc_ref[2 * hd] / _denominator(acc_ref, 2 * hd, hd)
           - lam * (acc_ref[2 * hd + 1] / _denominator(acc_ref, 2 * hd + 1, hd)) for hd in range(2)]
    o = jnp.where(_row_mask((LANES, tq), 0, HEAD_LANES), o_h[0], o_h[1]).T
    in_h0 = _lane_mask((tq, LANES), 0, HEAD_LANES)
    sq = o * o
    ms0 = jnp.sum(jnp.where(in_h0, sq, 0.0), axis=-1, keepdims=True)
    ms1 = jnp.sum(jnp.where(in_h0, 0.0, sq), axis=-1, keepdims=True)
    ms = jnp.where(in_h0, ms0, ms1) * (1.0 / HEAD_LANES)
    y = o * lax.rsqrt(ms + RMS_EPS) * sub_ref[...]
    o_ref[...] = (y * (1.0 - lam_init)).astype(o_ref.dtype)


N_BIAS_SPLIT = 3


def _alibi_key_columns(slopes_l2, tk):
    dj = jnp.arange(tk, dtype=F32)[None, :, None]
    val = jnp.asarray(slopes_l2, F32).reshape(-1, 1, 2) * dj
    cols = jnp.stack(_split_bf16(val, N_BIAS_SPLIT), axis=-1).reshape(val.shape[0], tk, 2 * N_BIAS_SPLIT)
    return jnp.pad(cols, ((0, 0), (0, 0), (0, LANES - 2 * N_BIAS_SPLIT)))


def _split_bf16(x, n):
    pieces = []
    for _ in range(n):
        bits = lax.bitcast_convert_type(x, jnp.uint32) & jnp.uint32(0xFFFF0000)
        head = lax.bitcast_convert_type(bits, F32)
        pieces.append(head.astype(BF16))
        x = x - head
    return pieces


def _diff_attention(a3, avt, slopes_l2, mask_t, kb, lam_p, sub128, layer_idx, tq):
    b, s, _ = a3.shape
    nq = s // tq
    lam_init = 0.8 - 0.6 * math.exp(-0.3 * layer_idx)
    body = functools.partial(_diff_body, tq=tq, lam_init=lam_init)
    grid_spec = pltpu.PrefetchScalarGridSpec(
        num_scalar_prefetch=1,
        grid=(b, 2, nq),
        in_specs=[
            pl.BlockSpec((None, tq, LANES), lambda bi, p, qi, sl: (bi, qi, p)),
            pl.BlockSpec((None, s, LANES), lambda bi, p, qi, sl: (bi, 0, 2 + p)),
            pl.BlockSpec((None, 2 * LANES, s), lambda bi, p, qi, sl: (bi, p, 0)),
            pl.BlockSpec((tq, tq), lambda bi, p, qi, sl: (0, 0)),
            pl.BlockSpec((None, tq, LANES), lambda bi, p, qi, sl: (p, 0, 0)),
            pl.BlockSpec((4, DIFF_HD), lambda bi, p, qi, sl: (0, 0)),
            pl.BlockSpec((1, LANES), lambda bi, p, qi, sl: (0, 0)),
        ],
        out_specs=pl.BlockSpec((None, tq, LANES), lambda bi, p, qi, sl: (bi, qi, p)),
        scratch_shapes=[pltpu.VMEM((4, tq, 2 * LANES), BF16), pltpu.VMEM((4, tq, tq), F32),
                        pltpu.VMEM((4, tq, tq), F32), pltpu.VMEM((4, LANES, tq), F32)],
    )
    return pl.pallas_call(
        body, grid_spec=grid_spec,
        out_shape=jax.ShapeDtypeStruct((b, s, 256), BF16),
        compiler_params=_params(("parallel", "parallel", "arbitrary")),
        name="diff_attn",
    )(slopes_l2, a3, a3, avt, mask_t, kb, lam_p, sub128)


def _mla_body(q0_ref, q1_ref, k_ref, vt_ref, mask_ref, o_ref, s0_ref, s1_ref, acc_ref, *, tq):
    qi = pl.program_id(2)
    tk = tq
    acc_ref[...] = jnp.zeros(acc_ref.shape, F32)
    q_refs = (q0_ref, q1_ref)

    def scores(i, t, first):
        k = k_ref[pl.ds(pl.multiple_of((qi - t) * tk, tk), tk), :]
        s = lax.dot_general(k, q_refs[i][...], _NT, preferred_element_type=F32)
        return s + mask_ref[...] if first else s

    def load_vt(t, i):
        return vt_ref[pl.ds(i * LANES, LANES), pl.ds(pl.multiple_of((qi - t) * tk, tk), tk)]

    m0 = tuple(jnp.full((1, tq), NEG_INF, F32) for _ in range(2))
    _attn_pipeline(2, qi, scores, lambda i, t: 0.0, load_vt, m0, (s0_ref, s1_ref), acc_ref)
    o_t = jnp.where(_row_mask((LANES, tq), 0, HEAD_LANES), acc_ref[0] / _denominator(acc_ref, 0, 0),
                    acc_ref[1] / _denominator(acc_ref, 1, 1))
    o_ref[...] = o_t.T.astype(o_ref.dtype)


def _mla_attention(cq3, ck3, cvt, mask_t, tq):
    b, s, _ = cq3.shape
    nq = s // tq
    return pl.pallas_call(
        functools.partial(_mla_body, tq=tq),
        grid=(b, 2, nq),
        in_specs=[
            pl.BlockSpec((None, tq, 256), lambda bi, g, qi: (bi, qi, 2 * g)),
            pl.BlockSpec((None, tq, 256), lambda bi, g, qi: (bi, qi, 2 * g + 1)),
            pl.BlockSpec((None, s, 256), lambda bi, g, qi: (bi, 0, g)),
            pl.BlockSpec((None, 2 * LANES, s), lambda bi, g, qi: (bi, g, 0)),
            pl.BlockSpec((tq, tq), lambda bi, g, qi: (0, 0)),
        ],
        out_specs=pl.BlockSpec((None, tq, LANES), lambda bi, g, qi: (bi, qi, g)),
        out_shape=jax.ShapeDtypeStruct((b, s, 256), BF16),
        scratch_shapes=[pltpu.VMEM((2, tq, tq), F32), pltpu.VMEM((2, tq, tq), F32),
                        pltpu.VMEM((2, LANES, tq), F32)],
        compiler_params=_params(("parallel", "parallel", "arbitrary")),
        name="mla_attn",
    )(cq3, cq3, ck3, cvt, mask_t)


def _band_body(m0_ref, q_ref, k_ref, vt_ref, bias_ref, o_ref, qs_ref, s0_ref, s1_ref, acc_ref,
               *, tq, n_off, l_init):
    slab, qi = pl.program_id(1), pl.program_id(2)
    tk = tq
    q = q_ref[...].astype(F32)
    in_h0 = _row_mask((LANES, tq), 0, HEAD_LANES)
    for i in range(2):
        qs_ref[i] = jnp.where(_lane_mask(q.shape, i * HEAD_LANES, (i + 1) * HEAD_LANES), q, 0.0).astype(BF16)
        acc_ref[i] = jnp.where(in_h0 if i == 1 else jnp.logical_not(in_h0), l_init, 0.0).astype(F32)

    def scores(i, t, first):
        k = k_ref[pl.ds(pl.multiple_of((qi - t) * tk, tk), tk), :]
        return lax.dot_general(k, qs_ref[i], _NT, preferred_element_type=F32) + bias_ref[i, t]

    def load_vt(t, i):
        return vt_ref[pl.ds(i * LANES, LANES), pl.ds(pl.multiple_of((qi - t) * tk, tk), tk)]

    m0 = tuple(jnp.full((1, tq), m0_ref[2 * slab + i], F32) for i in range(2))
    _attn_pipeline(2, jnp.minimum(qi, n_off - 1), scores, lambda i, t: 0.0, load_vt, m0, (s0_ref, s1_ref), acc_ref)
    o_t = jnp.where(in_h0, acc_ref[0] / _denominator(acc_ref, 0, 0), acc_ref[1] / _denominator(acc_ref, 1, 1))
    o_ref[...] = o_t.T.astype(o_ref.dtype)


def _band_attention(x3, xvt, k_blk_of, vt_blk_of, bias, m0, l_init, tq, name):
    b, s, _ = x3.shape
    nq = s // tq
    n_off = bias.shape[2]
    grid_spec = pltpu.PrefetchScalarGridSpec(
        num_scalar_prefetch=1,
        grid=(b, 2, nq),
        in_specs=[
            pl.BlockSpec((None, tq, LANES), lambda bi, p, qi, m0r: (bi, qi, p)),
            pl.BlockSpec((None, s, LANES), lambda bi, p, qi, m0r: (bi, 0, k_blk_of(p))),
            pl.BlockSpec((None, 2 * LANES, s), lambda bi, p, qi, m0r: (bi, vt_blk_of(p), 0)),
            pl.BlockSpec((None, 2, n_off, tq, tq), lambda bi, p, qi, m0r: (p, 0, 0, 0, 0)),
        ],
        out_specs=pl.BlockSpec((None, tq, LANES), lambda bi, p, qi, m0r: (bi, qi, p)),
        scratch_shapes=[pltpu.VMEM((2, tq, LANES), BF16), pltpu.VMEM((2, tq, tq), F32),
                        pltpu.VMEM((2, tq, tq), F32), pltpu.VMEM((2, LANES, tq), F32)],
    )
    return pl.pallas_call(
        functools.partial(_band_body, tq=tq, n_off=n_off, l_init=l_init),
        grid_spec=grid_spec,
        out_shape=jax.ShapeDtypeStruct((b, s, 256), BF16),
        compiler_params=_params(("parallel", "parallel", "arbitrary")),
        name=name,
    )(m0, x3, x3, xvt, bias)


def _band_bias(slopes, tq, n_off, mult_fn):
    di = jnp.arange(tq, dtype=jnp.int32)[None, :]
    dj = jnp.arange(tq, dtype=jnp.int32)[:, None]
    off = jnp.arange(n_off, dtype=jnp.int32)[:, None, None]
    delta = off * tq + di - dj
    mult = mult_fn(delta)
    sl = jnp.asarray(slopes, F32)[:, None, None, None] * LOG2E
    val = -sl * delta.astype(F32)[None] + jnp.log2(jnp.maximum(mult, 1).astype(F32))[None]
    return jnp.where((mult > 0)[None], val, NEG_INF)


def _dil_mult(delta):
    m = jnp.zeros(delta.shape, jnp.int32)
    for window, d in DIL_PATTERNS:
        m = m + ((delta >= 0) & (delta <= window) & (delta % d == 0)).astype(jnp.int32)
    return m


def _swa_mult(delta):
    return ((delta >= 0) & (delta <= SWA_WINDOW - 1)).astype(jnp.int32)


def _outproj_body(*refs, with_router):
    if with_router:
        (oa, ob, oc, od, w_ref, x_ref, g_ref, rh_ref, rl_ref, x1_ref, h2_ref, lg_ref) = refs
    else:
        (oa, ob, oc, od, w_ref, x_ref, g_ref, x1_ref, h2_ref) = refs
    acc = x_ref[...]
    for i, o in enumerate((oa, ob, oc, od)):
        acc = acc + jnp.dot(o[...], w_ref[i], preferred_element_type=F32)
    x1_ref[...] = acc
    h2 = acc * lax.rsqrt(jnp.mean(acc * acc, axis=-1, keepdims=True) + RMS_EPS) * g_ref[...]
    hi = h2.astype(BF16)
    if with_router:
        for c in range(SC_CHUNKS):
            h2_ref[c] = h2[:, c * SC_COLS:(c + 1) * SC_COLS]
        lo = (h2 - hi.astype(F32)).astype(BF16)
        lg_ref[...] = (jnp.dot(hi, rh_ref[...], preferred_element_type=F32)
                       + (jnp.dot(hi, rl_ref[...], preferred_element_type=F32)
                          + jnp.dot(lo, rh_ref[...], preferred_element_type=F32)))
    else:
        h2_ref[...] = hi


def _outproj(oa, ob, oc, od, w4, x2d, g, router_hl=None):
    t = x2d.shape[0]
    tm = _pick(t, 512)
    with_router = router_hl is not None
    full = lambda shape: pl.BlockSpec(shape, lambda i: (0,) * len(shape))
    row = lambda w: pl.BlockSpec((tm, w), lambda i: (i, 0))
    in_specs = [row(256)] * 4 + [full((4, 256, D_MODEL)), row(D_MODEL), full((1, D_MODEL))]
    args = [oa, ob, oc, od, w4, x2d, g]
    out_specs = [row(D_MODEL), row(D_MODEL)]
    out_shape = [jax.ShapeDtypeStruct((t, D_MODEL), F32), jax.ShapeDtypeStruct((t, D_MODEL), BF16)]
    if with_router:
        out_specs[1] = pl.BlockSpec((SC_CHUNKS, tm, SC_COLS), lambda i: (0, i, 0))
        out_shape[1] = jax.ShapeDtypeStruct((SC_CHUNKS, t, SC_COLS), F32)
        in_specs += [full((D_MODEL, LANES))] * 2
        args += list(router_hl)
        out_specs.append(row(LANES))
        out_shape.append(jax.ShapeDtypeStruct((t, LANES), F32))
    return pl.pallas_call(
        functools.partial(_outproj_body, with_router=with_router),
        grid=(t // tm,), in_specs=in_specs, out_specs=out_specs, out_shape=out_shape,
        compiler_params=_params(("parallel",)),
        name="outproj",
    )(*args)


def _ffn_body(h_ref, w1_ref, w3_ref, w2_ref, x_ref, o_ref, acc_ref):
    f = pl.program_id(1)
    h = h_ref[...]
    a = jnp.dot(h, w1_ref[...], preferred_element_type=F32)
    b = jnp.dot(h, w3_ref[...], preferred_element_type=F32)
    mid = (a * jax.nn.sigmoid(a) * b).astype(BF16)
    contrib = jnp.dot(mid, w2_ref[...], preferred_element_type=F32)

    @pl.when(f == 0)
    def _():
        acc_ref[...] = contrib

    @pl.when(f > 0)
    def _():
        acc_ref[...] += contrib

    @pl.when(f == pl.num_programs(1) - 1)
    def _():
        o_ref[...] = x_ref[...] + acc_ref[...]


def _ffn(h2, w1, w3, w2, x1):
    t = h2.shape[0]
    tm = _pick(t, 512)
    tf = D_FF // 2
    return pl.pallas_call(
        _ffn_body,
        grid=(t // tm, D_FF // tf),
        in_specs=[pl.BlockSpec((tm, D_MODEL), lambda i, f: (i, 0)),
                  pl.BlockSpec((D_MODEL, tf), lambda i, f: (0, f)),
                  pl.BlockSpec((D_MODEL, tf), lambda i, f: (0, f)),
                  pl.BlockSpec((tf, D_MODEL), lambda i, f: (f, 0)),
                  pl.BlockSpec((tm, D_MODEL), lambda i, f: (i, 0))],
        out_specs=pl.BlockSpec((tm, D_MODEL), lambda i, f: (i, 0)),
        out_shape=jax.ShapeDtypeStruct((t, D_MODEL), F32),
        scratch_shapes=[pltpu.VMEM((tm, D_MODEL), F32)],
        compiler_params=_params(("parallel", "arbitrary")),
        name="ffn",
    )(h2, w1, w3, w2, x1)


def _route_body(lg_ref, tri_ref, info_ref, cnt_ref, carry_ref):
    i = pl.program_id(0)

    @pl.when(i == 0)
    def _():
        carry_ref[...] = jnp.zeros(carry_ref.shape, F32)

    lg = lg_ref[...]
    lane = lax.broadcasted_iota(jnp.int32, lg.shape, 1)
    lg = jnp.where(lane < N_EXPERTS, lg, -jnp.inf)
    m1 = jnp.max(lg, axis=-1, keepdims=True)
    i1 = jnp.min(jnp.where(lg == m1, lane, LANES), axis=-1, keepdims=True)
    oh1 = lane == i1
    lg2 = jnp.where(oh1, -jnp.inf, lg)
    m2 = jnp.max(lg2, axis=-1, keepdims=True)
    i2 = jnp.min(jnp.where(lg2 == m2, lane, LANES), axis=-1, keepdims=True)
    oh2 = lane == i2
    e2 = jnp.exp(m2 - m1)
    g1 = 1.0 / (1.0 + e2)
    g2 = e2 / (1.0 + e2)
    oh = jnp.where(jnp.logical_or(oh1, oh2), 1.0, 0.0)
    rank = jnp.dot(tri_ref[...], oh.astype(BF16), preferred_element_type=F32) + carry_ref[...]
    r1 = jnp.sum(jnp.where(oh1, rank, 0.0), axis=-1, keepdims=True)
    r2 = jnp.sum(jnp.where(oh2, rank, 0.0), axis=-1, keepdims=True)
    carry_ref[...] += jnp.sum(oh, axis=0, keepdims=True)
    cnt_ref[...] = carry_ref[...]
    info = jnp.zeros(lg.shape, F32)
    for c, val in enumerate((i1.astype(F32), i2.astype(F32), r1, r2, g1, g2)):
        info = jnp.where(lane == c, val, info)
    info_ref[...] = info


def _route(logits):
    t = logits.shape[0]
    tm = _pick(t, 512)
    tri = (jnp.arange(tm)[:, None] > jnp.arange(tm)[None, :]).astype(BF16)
    return pl.pallas_call(
        _route_body,
        grid=(t // tm,),
        in_specs=[pl.BlockSpec((tm, LANES), lambda i: (i, 0)), pl.BlockSpec((tm, tm), lambda i: (0, 0))],
        out_specs=[pl.BlockSpec((tm, LANES), lambda i: (i, 0)), pl.BlockSpec((1, LANES), lambda i: (0, 0))],
        out_shape=[jax.ShapeDtypeStruct((t, LANES), F32), jax.ShapeDtypeStruct((1, LANES), F32)],
        scratch_shapes=[pltpu.VMEM((1, LANES), F32)],
        compiler_params=_params(("arbitrary",)),
        name="route",
    )(logits, tri)


def _moe_ffn_body(be_ref, x_ref, w1_ref, w3_ref, w2_ref, o_ref, acc_ref):
    f = pl.program_id(1)
    x = jnp.concatenate([x_ref[c] for c in range(SC_CHUNKS)], axis=1).astype(BF16)
    a = jnp.dot(x, w1_ref[...], preferred_element_type=F32)
    b = jnp.dot(x, w3_ref[...], preferred_element_type=F32)
    mid = (a * jax.nn.sigmoid(a) * b).astype(BF16)
    contrib = jnp.dot(mid, w2_ref[...], preferred_element_type=F32)

    @pl.when(f == 0)
    def _():
        acc_ref[...] = contrib

    @pl.when(f > 0)
    def _():
        acc_ref[...] += contrib

    @pl.when(f == pl.num_programs(1) - 1)
    def _():
        for c in range(SC_CHUNKS):
            o_ref[c] = acc_ref[:, c * SC_COLS:(c + 1) * SC_COLS]


def _moe_ffn(xs, blk_expert, w1, w3, w2, bm):
    rows = xs.shape[1]
    tf = D_FF_EXPERT // 4
    chunked = pl.BlockSpec((SC_CHUNKS, bm, SC_COLS), lambda i, f, be: (0, i, 0))
    grid_spec = pltpu.PrefetchScalarGridSpec(
        num_scalar_prefetch=1,
        grid=(rows // bm, D_FF_EXPERT // tf),
        in_specs=[chunked,
                  pl.BlockSpec((None, D_MODEL, tf), lambda i, f, be: (be[i], 0, f)),
                  pl.BlockSpec((None, D_MODEL, tf), lambda i, f, be: (be[i], 0, f)),
                  pl.BlockSpec((None, tf, D_MODEL), lambda i, f, be: (be[i], f, 0))],
        out_specs=chunked,
        scratch_shapes=[pltpu.VMEM((bm, D_MODEL), F32)],
    )
    return pl.pallas_call(
        _moe_ffn_body, grid_spec=grid_spec,
        out_shape=jax.ShapeDtypeStruct(xs.shape, F32),
        compiler_params=_params(("parallel", "arbitrary")),
        name="moe_ffn",
    )(blk_expert, xs, w1, w3, w2)


def _final_body(*refs, with_moe):
    if with_moe:
        x_ref, y0_ref, y1_ref, info_ref, g_ref, o_ref = refs
        info = info_ref[...]
        y0 = jnp.concatenate([y0_ref[c] for c in range(SC_CHUNKS)], axis=1)
        y1 = jnp.concatenate([y1_ref[c] for c in range(SC_CHUNKS)], axis=1)
        x = x_ref[...] + (info[:, 4:5] * y0 + info[:, 5:6] * y1)
    else:
        x_ref, g_ref, o_ref = refs
        x = x_ref[...]
    o_ref[...] = x * lax.rsqrt(jnp.mean(x * x, axis=-1, keepdims=True) + RMS_EPS) * g_ref[...]


def _final(x2d, g, moe=None):
    t = x2d.shape[0]
    tm = _pick(t, 512)
    row = lambda w: pl.BlockSpec((tm, w), lambda i: (i, 0))
    in_specs, args = [row(D_MODEL)], [x2d]
    if moe is not None:
        yg, info = moe
        slot = lambda k: pl.BlockSpec((None, SC_CHUNKS, tm, SC_COLS), lambda i: (k, 0, i, 0))
        in_specs += [slot(0), slot(1), row(LANES)]
        args += [yg, yg, info]
    in_specs.append(pl.BlockSpec((1, D_MODEL), lambda i: (0, 0)))
    args.append(g)
    return pl.pallas_call(
        functools.partial(_final_body, with_moe=moe is not None),
        grid=(t // tm,), in_specs=in_specs, out_specs=row(D_MODEL),
        out_shape=jax.ShapeDtypeStruct((t, D_MODEL), F32),
        compiler_params=_params(("parallel",)),
        name="final",
    )(*args)


def _prep_attn_weights(w_in, w_out, w_uq, w_ukv):
    widths = (256, 256, 256, 256, 256, 256, MLA_Q_LORA, MLA_KV_LORA, MLA_ROPE, 256, 128, 128)
    offs = np.concatenate([[0], np.cumsum(widths)])
    qa, ka, va, qb, kb, vb, cq, ckv, kpe, qd, kd, vd = [w_in[:, offs[i]:offs[i + 1]] for i in range(12)]
    hperm = np.array([0, 2, 1, 3])
    qd = qd.reshape(D_MODEL, 4, SWA_HD)[:, hperm].reshape(D_MODEL, 256)
    pad = jnp.zeros((D_MODEL, LANES - MLA_ROPE), w_in.dtype)
    half = MLA_ROPE // 2
    kpe_sw = jnp.concatenate([kpe[:, half:], kpe[:, :half]], axis=1)
    w_main = jnp.concatenate([qa, ka, va, qb, kb, vb, qd, kd, vd, cq, ckv, kpe, pad, kpe_sw, pad], axis=1)

    uq = w_uq.reshape(MLA_Q_LORA, MLA_HEADS, MLA_NOPE + MLA_ROPE)
    z = lambda n: jnp.zeros((MLA_Q_LORA, n), w_uq.dtype)
    mains, sws = [], []
    for hd in range(MLA_HEADS):
        nope, rope = uq[:, hd, :MLA_NOPE], uq[:, hd, MLA_NOPE:]
        nope128 = jnp.concatenate([nope, z(64)] if hd % 2 == 0 else [z(64), nope], axis=1)
        mains.append(jnp.concatenate([rope, z(LANES - MLA_ROPE), nope128], axis=1))
        sws.append(jnp.concatenate([rope[:, half:], rope[:, :half], z(LANES - MLA_ROPE)], axis=1))
    wq_big = jnp.concatenate(mains + sws, axis=1)

    ukv = w_ukv.reshape(MLA_KV_LORA, MLA_HEADS, MLA_NOPE + MLA_V)
    wkv = jnp.concatenate([ukv[:, :, :MLA_NOPE].reshape(MLA_KV_LORA, 256),
                           ukv[:, :, MLA_NOPE:].reshape(MLA_KV_LORA, 256)], axis=1)

    wo = w_out.reshape(4, 256, D_MODEL)
    wo_d = wo[3].reshape(4, SWA_HD, D_MODEL)[hperm].reshape(256, D_MODEL)
    w4 = jnp.stack([wo[0], wo[1], wo[2], wo_d])
    return w_main.astype(BF16), wq_big.astype(BF16), wkv.astype(BF16), w4.astype(BF16)


def _rope_tables(s):
    inv = ROPE_THETA ** (-jnp.arange(0, MLA_ROPE, 2, dtype=F32) / MLA_ROPE)
    ang = jnp.arange(s, dtype=F32)[:, None] * inv[None, :]
    cos, sin = jnp.cos(ang), jnp.sin(ang)
    z = jnp.zeros((s, LANES - MLA_ROPE), F32)
    cos128 = jnp.concatenate([cos, cos, z], axis=1)
    sin128 = jnp.concatenate([-sin, sin, z], axis=1)
    c_mla = (MLA_NOPE + MLA_ROPE) ** -0.5 * LOG2E
    return cos128 * c_mla, sin128 * c_mla, cos128, sin128


def _col_scale():
    ca = DIFF_HD ** -0.5 * LOG2E
    cb = DIL_HD ** -0.5 * LOG2E
    cd = SWA_HD ** -0.5 * LOG2E
    v = np.ones((1, N_SCALED), np.float32)
    v[0, 0:256] = ca
    v[0, 768:1024] = cb
    v[0, 1536:1792] = cd
    return jnp.asarray(v)


def _moe_layer(h2, logits, w1, w3, w2, bm):
    t = logits.shape[0]
    info, counts = _route(logits)
    cnt = counts[0, :N_EXPERTS].astype(jnp.int32)
    padded = (cnt + bm - 1) // bm * bm
    pends = jnp.cumsum(padded)
    pstarts = pends - padded
    e12 = info[:, 0:2].astype(jnp.int32)
    pos = pstarts[e12] + info[:, 2:4].astype(jnp.int32)
    nb = (t * TOP_K) // bm + N_EXPERTS
    rows = nb * bm
    blk_start = jnp.arange(nb, dtype=jnp.int32) * bm
    blk_expert = jnp.minimum(jnp.sum((pends[None, :] <= blk_start[:, None]).astype(jnp.int32), axis=1),
                             N_EXPERTS - 1)
    sub = (jnp.arange(SC_CHUNKS, dtype=jnp.int32) * rows)[None, :, None] + pos.T[:, None, :]
    xs = _sc_scatter_rows(h2.reshape(SC_CHUNKS * t, SC_COLS),
                          [sub[k].reshape(1, SC_CHUNKS * t) for k in range(TOP_K)], SC_CHUNKS * rows)
    ys = _moe_ffn(xs.reshape(SC_CHUNKS, rows, SC_COLS), blk_expert.astype(jnp.int32), w1, w3, w2, bm)
    yg = _sc_gather_rows(ys.reshape(SC_CHUNKS * rows, SC_COLS), sub.reshape(1, TOP_K * SC_CHUNKS * t))
    return yg.reshape(TOP_K, SC_CHUNKS, t, SC_COLS), info


SC_WINDOW = 128
SC_COLS = 256
SC_CHUNKS = D_MODEL // SC_COLS


def _sc_mesh():
    return plsc.VectorSubcoreMesh(core_axis_name="c", subcore_axis_name="s")


def _sc_scatter_rows(x, idx_list, rows):
    t, d = x.shape
    n = len(idx_list)

    @functools.partial(pl.kernel, out_type=jax.ShapeDtypeStruct((rows, d), x.dtype), mesh=_sc_mesh(),
                       scratch_types=[])
    def scatter_kernel(x_hbm, *rest):
        idx_hbms, o_hbm = rest[:n], rest[n]

        def body(x_vmem, *idx_vmems):
            for iv in idx_vmems:
                pltpu.sync_copy(x_vmem, o_hbm.at[iv.at[0]])

        pltpu.emit_pipeline(
            body, grid=(t // SC_WINDOW,),
            in_specs=[pl.BlockSpec((SC_WINDOW, d), lambda i: (i, 0))]
            + [pl.BlockSpec((1, SC_WINDOW), lambda i: (0, i))] * n,
            out_specs=[], core_axis_name=("c", "s"), dimension_semantics=(pltpu.PARALLEL,),
        )(x_hbm, *idx_hbms)

    return scatter_kernel(x, *idx_list)


def _sc_gather_rows(y, idx):
    m = idx.shape[1]
    d = y.shape[1]

    @functools.partial(pl.kernel, out_type=jax.ShapeDtypeStruct((m, d), y.dtype), mesh=_sc_mesh(),
                       scratch_types=[])
    def gather_kernel(y_hbm, idx_hbm, o_hbm):
        def body(idx_vmem, o_vmem):
            pltpu.sync_copy(y_hbm.at[idx_vmem.at[0]], o_vmem)

        pltpu.emit_pipeline(
            body, grid=(m // SC_WINDOW,),
            in_specs=[pl.BlockSpec((1, SC_WINDOW), lambda i: (0, i))],
            out_specs=[pl.BlockSpec((SC_WINDOW, d), lambda i: (i, 0))],
            core_axis_name=("c", "s"), dimension_semantics=(pltpu.PARALLEL,),
        )(idx_hbm, o_hbm)

    return gather_kernel(y, idx)


def kernel(x, attn_norm, w_in, w_out, diff_lambda, diff_subln, mla_q_norm, mla_w_uq, mla_kv_norm, mla_w_ukv,
           swa_sinks, ffn_norm, ffn_w1, ffn_w3, ffn_w2, moe_router, moe_w1, moe_w3, moe_w2, final_norm):
    bsz, seq, _ = x.shape
    t = bsz * seq
    depth = w_in.shape[0]
    slopes_a, slopes_b, slopes_d = _alibi_slopes()
    tq_dense = _pick(seq, 512)
    tq_band = _pick(seq, 256)

    tabs = _rope_tables(seq)
    colscale = _col_scale()
    di = jnp.arange(tq_dense)
    causal_mask_t = jnp.where(di[:, None] <= di[None, :], 0.0, NEG_INF).astype(F32)
    alibi_kb = _alibi_key_columns(slopes_a * LOG2E, tq_dense)
    n_off_dil = min(max(w for w, _ in DIL_PATTERNS) // tq_band + 1, seq // tq_band)
    n_off_swa = min((SWA_WINDOW - 1 + tq_band - 1) // tq_band + 1, seq // tq_band)
    dil_bias = _band_bias(slopes_b, tq_band, n_off_dil, _dil_mult).reshape(2, 2, n_off_dil, tq_band, tq_band)
    swa_bias = _band_bias(slopes_d[np.array([0, 2, 1, 3])], tq_band, n_off_swa, _swa_mult
                          ).reshape(2, 2, n_off_swa, tq_band, tq_band)
    slopes_a_l2 = jnp.asarray(slopes_a * LOG2E, F32)
    dil_m0 = jnp.full((4,), NEG_INF, F32)

    x2d = x.reshape(t, D_MODEL)
    for l in range(depth):
        w_main, wq_big, wkv, w4 = _prep_attn_weights(w_in[l], w_out[l], mla_w_uq[l], mla_w_ukv[l])
        a, avt, b, bvt, d, dvt, cq, ck, cvt = _proj(x2d, attn_norm[l][None], w_main, colscale, mla_q_norm[l][None],
                                                    wq_big, mla_kv_norm[l][None], wkv, tabs, seq)
        a3, b3, d3 = (v.reshape(bsz, seq, -1) for v in (a, b, d))
        sub128 = jnp.concatenate([diff_subln[l], diff_subln[l]])[None]
        o_a = _diff_attention(a3, avt, slopes_a_l2, causal_mask_t, alibi_kb, diff_lambda[l], sub128, l, tq_dense)
        o_b = _band_attention(b3, bvt, lambda p: 2 + p, lambda p: p, dil_bias, dil_m0, 0.0, tq_band, "dil_attn")
        o_c = _mla_attention(cq.reshape(bsz, seq, -1), ck.reshape(bsz, seq, -1), cvt, causal_mask_t, tq_dense)
        swa_m0 = (swa_sinks[l].astype(F32) * LOG2E)[np.array([0, 2, 1, 3])]
        o_d = _band_attention(d3, dvt, lambda p: 2, lambda p: 0, swa_bias, swa_m0, 1.0, tq_band, "swa_attn")
        outs = [v.reshape(t, 256) for v in (o_a, o_b, o_c, o_d)]
        j = l // 2
        if l % 2 == 0:
            x1, h2 = _outproj(*outs, w4, x2d, ffn_norm[l][None])
            x2d = _ffn(h2, ffn_w1[j].astype(BF16), ffn_w3[j].astype(BF16), ffn_w2[j].astype(BF16), x1)
            moe = None
        else:
            r = jnp.pad(moe_router[j], ((0, 0), (0, LANES - N_EXPERTS)))
            r_hi, r_lo = _split_bf16(r, 2)
            x1, h2, logits = _outproj(*outs, w4, x2d, ffn_norm[l][None], (r_hi, r_lo))
            bm = _pick(t, 512)
            moe = _moe_layer(h2, logits, moe_w1[j].astype(BF16), moe_w3[j].astype(BF16),
                             moe_w2[j].astype(BF16), bm)
            x2d = x1
        if l < depth - 1 and moe is not None:
            raise NotImplementedError("MoE combine is fused into the final norm; MoE layer must be last")
    return _final(x2d, final_norm[None], moe).reshape(bsz, seq, D_MODEL)
```

```python
import functools
import math

import numpy as np
import jax
import jax.numpy as jnp
from jax import lax
from jax.experimental import pallas as pl
from jax.experimental.pallas import tpu as pltpu
from jax.experimental.pallas import tpu_sc as plsc

D_MODEL = 1024
DIFF_HEADS, DIFF_HD = 4, 32
DIL_HEADS, DIL_HD = 4, 64
DIL_PATTERNS = ((128, 1), (512, 4), (2048, 16))
MLA_HEADS, MLA_Q_LORA, MLA_KV_LORA, MLA_NOPE, MLA_ROPE, MLA_V = 4, 384, 128, 64, 32, 64
ROPE_THETA = 10000.0
SWA_HEADS, SWA_KV_HEADS, SWA_HD, SWA_WINDOW = 4, 2, 64, 128
D_FF = 2816
N_EXPERTS, TOP_K, D_FF_EXPERT = 8, 2, 3584
RMS_EPS = 1e-6
NEG_INF = -1e30
N_ALIBI = DIFF_HEADS + DIL_HEADS + SWA_HEADS

LOG2E = 1.4426950408889634
LANES = 128
HEAD_LANES = 64
VMEM_LIMIT = 56 * 1024 * 1024

F32 = jnp.float32
BF16 = jnp.bfloat16

_NT = (((1,), (1,)), ((), ()))


def _params(sem, vmem=VMEM_LIMIT):
    return pltpu.CompilerParams(dimension_semantics=sem, vmem_limit_bytes=vmem)


def _alibi_slopes():
    s = 2.0 ** (-8.0 * (np.arange(N_ALIBI) + 1) / N_ALIBI)
    return s[0::3], s[1::3], s[2::3]


def _pick(n, pref):
    t = min(pref, n)
    while n % t:
        t //= 2
    return t


N_SCALED = 2048
C_CQ, C_CKV, C_KPE, C_KPE_SW, N_MAIN = 2048, 2432, 2560, 2688, 2816


def _vt_with_ones(v):
    vt = v.T
    ones = jnp.ones((HEAD_LANES, v.shape[0]), F32)
    parts = []
    for pair in range(v.shape[1] // LANES):
        h0 = vt[pair * LANES:pair * LANES + HEAD_LANES]
        h1 = vt[pair * LANES + HEAD_LANES:(pair + 1) * LANES]
        parts += [h0, ones, ones, h1]
    return jnp.concatenate(parts, axis=0).astype(BF16)


def _proj_body(x_ref, g_ref, w_ref, cs_ref, qn_ref, wq_ref, kn_ref, wkv_ref,
               cosq_ref, sinq_ref, cosk_ref, sink_ref,
               a_ref, avt_ref, b_ref, bvt_ref, d_ref, dvt_ref, cq_ref, ck_ref, cvt_ref):
    x = x_ref[...]
    ms = jnp.mean(x * x, axis=-1, keepdims=True)
    h = (x * lax.rsqrt(ms + RMS_EPS) * g_ref[...]).astype(BF16)
    acc = jnp.dot(h, w_ref[...], preferred_element_type=F32)
    sc = acc[:, :N_SCALED] * cs_ref[...]
    a_ref[...] = sc[:, 0:512].astype(BF16)
    avt_ref[...] = _vt_with_ones(sc[:, 512:768])
    b_ref[...] = sc[:, 768:1280].astype(BF16)
    bvt_ref[...] = _vt_with_ones(sc[:, 1280:1536])
    d_ref[...] = sc[:, 1536:1920].astype(BF16)
    dvt_ref[...] = _vt_with_ones(sc[:, 1920:2048])

    cq = acc[:, C_CQ:C_CKV]
    hq = (cq * lax.rsqrt(jnp.mean(cq * cq, axis=-1, keepdims=True) + RMS_EPS) * qn_ref[...]).astype(BF16)
    yq = jnp.dot(hq, wq_ref[...], preferred_element_type=F32)
    cosq, sinq = cosq_ref[...], sinq_ref[...]
    c_mla = (MLA_NOPE + MLA_ROPE) ** -0.5 * LOG2E
    for hd in range(MLA_HEADS):
        main = yq[:, hd * 256:(hd + 1) * 256]
        sw = yq[:, 1024 + hd * LANES:1024 + (hd + 1) * LANES]
        cq_ref[:, hd * 256:hd * 256 + LANES] = (main[:, :LANES] * cosq + sw * sinq).astype(BF16)
        cq_ref[:, hd * 256 + LANES:(hd + 1) * 256] = (main[:, LANES:] * c_mla).astype(BF16)

    ckv = acc[:, C_CKV:C_KPE]
    hk = (ckv * lax.rsqrt(jnp.mean(ckv * ckv, axis=-1, keepdims=True) + RMS_EPS) * kn_ref[...]).astype(BF16)
    kv = jnp.dot(hk, wkv_ref[...], preferred_element_type=F32)
    kr = (acc[:, C_KPE:C_KPE_SW] * cosk_ref[...] + acc[:, C_KPE_SW:N_MAIN] * sink_ref[...]).astype(BF16)
    for grp in range(2):
        ck_ref[:, grp * 256:grp * 256 + LANES] = kr
        ck_ref[:, grp * 256 + LANES:(grp + 1) * 256] = kv[:, grp * LANES:(grp + 1) * LANES].astype(BF16)
    cvt_ref[...] = _vt_with_ones(kv[:, 256:512])


def _proj(x2d, g, w_main, colscale, qn, wq_big, kn, wkv, tabs, seq):
    t = x2d.shape[0]
    tm = _pick(seq, 512)
    nsb = seq // tm
    full = lambda shape: pl.BlockSpec(shape, lambda i: (0,) * len(shape))
    tab = pl.BlockSpec((tm, LANES), lambda i: (i % nsb, 0))
    row = lambda w: pl.BlockSpec((tm, w), lambda i: (i, 0))
    vt = lambda w: pl.BlockSpec((None, w, tm), lambda i: (i // nsb, 0, i % nsb))
    row_sds = lambda w: jax.ShapeDtypeStruct((t, w), BF16)
    vt_sds = lambda w: jax.ShapeDtypeStruct((t // seq, w, seq), BF16)
    return pl.pallas_call(
        _proj_body,
        grid=(t // tm,),
        in_specs=[row(D_MODEL), full((1, D_MODEL)), full((D_MODEL, N_MAIN)), full((1, N_SCALED)),
                  full((1, MLA_Q_LORA)), full((MLA_Q_LORA, 1536)), full((1, MLA_KV_LORA)),
                  full((MLA_KV_LORA, 512)), tab, tab, tab, tab],
        out_specs=[row(512), vt(512), row(512), vt(512), row(384), vt(256), row(1024), row(512), vt(512)],
        out_shape=[row_sds(512), vt_sds(512), row_sds(512), vt_sds(512), row_sds(384), vt_sds(256),
                   row_sds(1024), row_sds(512), vt_sds(512)],
        compiler_params=_params(("parallel",)),
        name="proj",
    )(x2d, g, w_main, colscale, qn, wq_big, kn, wkv, *tabs)


def _attn_pipeline(n_soft, n_steps, scores, shift, load_vt, m0, s_ref, acc_ref):
    def stage_a(t, ss, dst_ref, m_cur):
        m_next, alpha = [], []
        for i in range(n_soft):
            m_new = jnp.maximum(m_cur[i], jnp.max(ss[i], axis=0, keepdims=True) + shift(i, t))
            dst_ref[i] = ss[i]
            m_next.append(m_new)
            alpha.append(jnp.exp2(m_cur[i] - m_new))
        return tuple(m_next), tuple(alpha)

    def stage_b(t, src_ref, carry):
        m_cur, alpha = carry
        for i in range(n_soft):
            p = jnp.exp2(src_ref[i] - (m_cur[i] - shift(i, t)))
            acc_ref[i] = alpha[i] * acc_ref[i] + jnp.dot(load_vt(t, i), p.astype(BF16),
                                                         preferred_element_type=F32)

    def half(t, src_ref, dst_ref, carry):
        ss = [scores(i, t + 1, False) for i in range(n_soft)]
        stage_b(t, src_ref, carry)
        return stage_a(t + 1, ss, dst_ref, carry[0])

    def body(u, carry):
        carry = half(2 * u, s_ref[0], s_ref[1], carry)
        return half(2 * u + 1, s_ref[1], s_ref[0], carry)

    carry = stage_a(0, [scores(i, 0, True) for i in range(n_soft)], s_ref[0], m0)
    carry = lax.fori_loop(0, n_steps // 2, body, carry)

    @pl.when(n_steps % 2 == 1)
    def _():
        stage_b(n_steps, s_ref[1], half(n_steps - 1, s_ref[0], s_ref[1], carry))

    @pl.when(n_steps % 2 == 0)
    def _():
        stage_b(n_steps, s_ref[0], carry)


def _row_mask(shape, lo, hi):
    row = lax.broadcasted_iota(jnp.int32, shape, 0)
    return jnp.logical_and(row >= lo, row < hi)


def _lane_mask(shape, lo, hi):
    lane = lax.broadcasted_iota(jnp.int32, shape, len(shape) - 1)
    return jnp.logical_and(lane >= lo, lane < hi)


def _denominator(acc_ref, i, head):
    row = HEAD_LANES if head == 0 else 0
    return acc_ref[i, row:row + 1, :]


def _diff_body(sl_ref, q_ref, k_ref, vt_ref, mask_ref, kb_ref, lam_ref, sub_ref, o_ref,
               qs_ref, s0_ref, s1_ref, acc_ref, *, tq, lam_init):
    pair, qi = pl.program_id(1), pl.program_id(2)
    tk = tq
    q = q_ref[...].astype(F32)
    for hd in range(2):
        sel = jnp.where(_lane_mask(q.shape, N_BIAS_SPLIT * hd, N_BIAS_SPLIT * (hd + 1)), 1.0, 0.0).astype(BF16)
        for mp in range(2):
            lo = hd * HEAD_LANES + mp * DIFF_HD
            qs_ref[2 * hd + mp, :, 0:LANES] = jnp.where(_lane_mask(q.shape, lo, lo + DIFF_HD), q, 0.0).astype(BF16)
            qs_ref[2 * hd + mp, :, LANES:2 * LANES] = sel
    acc_ref[...] = jnp.zeros(acc_ref.shape, F32)

    def scores(i, t, first):
        off = pl.multiple_of((qi - t) * tk, tk)
        k = jnp.concatenate([k_ref[pl.ds(off, tk), :], kb_ref[...]], axis=1)
        s = lax.dot_general(k, qs_ref[i], _NT, preferred_element_type=F32)
        return s + mask_ref[...] if first else s

    def shift(i, t):
        return sl_ref[2 * pair + i // 2] * (-(t * tk)).astype(F32) if not isinstance(t, int) else 0.0

    def load_vt(t, i):
        return vt_ref[pl.ds((i // 2) * LANES, LANES), pl.ds(pl.multiple_of((qi - t) * tk, tk), tk)]

    m0 = tuple(jnp.full((1, tq), NEG_INF, F32) for _ in range(4))
    _attn_pipeline(4, qi, scores, shift, load_vt, m0, (s0_ref, s1_ref), acc_ref)

    lp = lam_ref[...]
    lam = (jnp.exp(jnp.sum(lp[0:1, :] * lp[1:2, :], axis=-1, keepdims=True))
           - jnp.exp(jnp.sum(lp[2:3, :] * lp[3:4, :], axis=-1, keepdims=True)) + lam_init)
    o_h = [acc_ref[2 * hd] / _denominator(acc_ref, 2 * hd, hd)
           - lam * (acc_ref[2 * hd + 1] / _denominator(acc_ref, 2 * hd + 1, hd)) for hd in range(2)]
    o = jnp.where(_row_mask((LANES, tq), 0, HEAD_LANES), o_h[0], o_h[1]).T
    in_h0 = _lane_mask((tq, LANES), 0, HEAD_LANES)
    sq = o * o
    ms0 = jnp.sum(jnp.where(in_h0, sq, 0.0), axis=-1, keepdims=True)
    ms1 = jnp.sum(jnp.where(in_h0, 0.0, sq), axis=-1, keepdims=True)
    ms = jnp.where(in_h0, ms0, ms1) * (1.0 / HEAD_LANES)
    y = o * lax.rsqrt(ms + RMS_EPS) * sub_ref[...]
    o_ref[...] = (y * (1.0 - lam_init)).astype(o_ref.dtype)


N_BIAS_SPLIT = 3


def _alibi_key_columns(slopes_l2, tk):
    dj = jnp.arange(tk, dtype=F32)[None, :, None]
    val = jnp.asarray(slopes_l2, F32).reshape(-1, 1, 2) * dj
    cols = jnp.stack(_split_bf16(val, N_BIAS_SPLIT), axis=-1).reshape(val.shape[0], tk, 2 * N_BIAS_SPLIT)
    return jnp.pad(cols, ((0, 0), (0, 0), (0, LANES - 2 * N_BIAS_SPLIT)))


def _split_bf16(x, n):
    pieces = []
    for _ in range(n):
        bits = lax.bitcast_convert_type(x, jnp.uint32) & jnp.uint32(0xFFFF0000)
        head = lax.bitcast_convert_type(bits, F32)
        pieces.append(head.astype(BF16))
        x = x - head
    return pieces


def _diff_attention(a3, avt, slopes_l2, mask_t, kb, lam_p, sub128, layer_idx, tq):
    b, s, _ = a3.shape
    nq = s // tq
    lam_init = 0.8 - 0.6 * math.exp(-0.3 * layer_idx)
    body = functools.partial(_diff_body, tq=tq, lam_init=lam_init)
    grid_spec = pltpu.PrefetchScalarGridSpec(
        num_scalar_prefetch=1,
        grid=(b, 2, nq),
        in_specs=[
            pl.BlockSpec((None, tq, LANES), lambda bi, p, qi, sl: (bi, qi, p)),
            pl.BlockSpec((None, s, LANES), lambda bi, p, qi, sl: (bi, 0, 2 + p)),
            pl.BlockSpec((None, 2 * LANES, s), lambda bi, p, qi, sl: (bi, p, 0)),
            pl.BlockSpec((tq, tq), lambda bi, p, qi, sl: (0, 0)),
            pl.BlockSpec((None, tq, LANES), lambda bi, p, qi, sl: (p, 0, 0)),
            pl.BlockSpec((4, DIFF_HD), lambda bi, p, qi, sl: (0, 0)),
            pl.BlockSpec((1, LANES), lambda bi, p, qi, sl: (0, 0)),
        ],
        out_specs=pl.BlockSpec((None, tq, LANES), lambda bi, p, qi, sl: (bi, qi, p)),
        scratch_shapes=[pltpu.VMEM((4, tq, 2 * LANES), BF16), pltpu.VMEM((4, tq, tq), F32),
                        pltpu.VMEM((4, tq, tq), F32), pltpu.VMEM((4, LANES, tq), F32)],
    )
    return pl.pallas_call(
        body, grid_spec=grid_spec,
        out_shape=jax.ShapeDtypeStruct((b, s, 256), BF16),
        compiler_params=_params(("parallel", "parallel", "arbitrary")),
        name="diff_attn",
    )(slopes_l2, a3, a3, avt, mask_t, kb, lam_p, sub128)


def _mla_body(q0_ref, q1_ref, k_ref, vt_ref, mask_ref, o_ref, s0_ref, s1_ref, acc_ref, *, tq):
    qi = pl.program_id(2)
    tk = tq
    acc_ref[...] = jnp.zeros(acc_ref.shape, F32)
    q_refs = (q0_ref, q1_ref)

    def scores(i, t, first):
        k = k_ref[pl.ds(pl.multiple_of((qi - t) * tk, tk), tk), :]
        s = lax.dot_general(k, q_refs[i][...], _NT, preferred_element_type=F32)
        return s + mask_ref[...] if first else s

    def load_vt(t, i):
        return vt_ref[pl.ds(i * LANES, LANES), pl.ds(pl.multiple_of((qi - t) * tk, tk), tk)]

    m0 = tuple(jnp.full((1, tq), NEG_INF, F32) for _ in range(2))
    _attn_pipeline(2, qi, scores, lambda i, t: 0.0, load_vt, m0, (s0_ref, s1_ref), acc_ref)
    o_t = jnp.where(_row_mask((LANES, tq), 0, HEAD_LANES), acc_ref[0] / _denominator(acc_ref, 0, 0),
                    acc_ref[1] / _denominator(acc_ref, 1, 1))
    o_ref[...] = o_t.T.astype(o_ref.dtype)


def _mla_attention(cq3, ck3, cvt, mask_t, tq):
    b, s, _ = cq3.shape
    nq = s // tq
    return pl.pallas_call(
        functools.partial(_mla_body, tq=tq),
        grid=(b, 2, nq),
        in_specs=[
            pl.BlockSpec((None, tq, 256), lambda bi, g, qi: (bi, qi, 2 * g)),
            pl.BlockSpec((None, tq, 256), lambda bi, g, qi: (bi, qi, 2 * g + 1)),
            pl.BlockSpec((None, s, 256), lambda bi, g, qi: (bi, 0, g)),
            pl.BlockSpec((None, 2 * LANES, s), lambda bi, g, qi: (bi, g, 0)),
            pl.BlockSpec((tq, tq), lambda bi, g, qi: (0, 0)),
        ],
        out_specs=pl.BlockSpec((None, tq, LANES), lambda bi, g, qi: (bi, qi, g)),
        out_shape=jax.ShapeDtypeStruct((b, s, 256), BF16),
        scratch_shapes=[pltpu.VMEM((2, tq, tq), F32), pltpu.VMEM((2, tq, tq), F32),
                        pltpu.VMEM((2, LANES, tq), F32)],
        compiler_params=_params(("parallel", "parallel", "arbitrary")),
        name="mla_attn",
    )(cq3, cq3, ck3, cvt, mask_t)


def _band_body(m0_ref, q_ref, k_ref, vt_ref, bias_ref, o_ref, qs_ref, s0_ref, s1_ref, acc_ref,
               *, tq, n_off, l_init):
    slab, qi = pl.program_id(1), pl.program_id(2)
    tk = tq
    q = q_ref[...].astype(F32)
    in_h0 = _row_mask((LANES, tq), 0, HEAD_LANES)
    for i in range(2):
        qs_ref[i] = jnp.where(_lane_mask(q.shape, i * HEAD_LANES, (i + 1) * HEAD_LANES), q, 0.0).astype(BF16)
        acc_ref[i] = jnp.where(in_h0 if i == 1 else jnp.logical_not(in_h0), l_init, 0.0).astype(F32)

    def scores(i, t, first):
        k = k_ref[pl.ds(pl.multiple_of((qi - t) * tk, tk), tk), :]
        return lax.dot_general(k, qs_ref[i], _NT, preferred_element_type=F32) + bias_ref[i, t]

    def load_vt(t, i):
        return vt_ref[pl.ds(i * LANES, LANES), pl.ds(pl.multiple_of((qi - t) * tk, tk), tk)]

    m0 = tuple(jnp.full((1, tq), m0_ref[2 * slab + i], F32) for i in range(2))
    _attn_pipeline(2, jnp.minimum(qi, n_off - 1), scores, lambda i, t: 0.0, load_vt, m0, (s0_ref, s1_ref), acc_ref)
    o_t = jnp.where(in_h0, acc_ref[0] / _denominator(acc_ref, 0, 0), acc_ref[1] / _denominator(acc_ref, 1, 1))
    o_ref[...] = o_t.T.astype(o_ref.dtype)


def _band_attention(x3, xvt, k_blk_of, vt_blk_of, bias, m0, l_init, tq, name):
    b, s, _ = x3.shape
    nq = s // tq
    n_off = bias.shape[2]
    grid_spec = pltpu.PrefetchScalarGridSpec(
        num_scalar_prefetch=1,
        grid=(b, 2, nq),
        in_specs=[
            pl.BlockSpec((None, tq, LANES), lambda bi, p, qi, m0r: (bi, qi, p)),
            pl.BlockSpec((None, s, LANES), lambda bi, p, qi, m0r: (bi, 0, k_blk_of(p))),
            pl.BlockSpec((None, 2 * LANES, s), lambda bi, p, qi, m0r: (bi, vt_blk_of(p), 0)),
            pl.BlockSpec((None, 2, n_off, tq, tq), lambda bi, p, qi, m0r: (p, 0, 0, 0, 0)),
        ],
        out_specs=pl.BlockSpec((None, tq, LANES), lambda bi, p, qi, m0r: (bi, qi, p)),
        scratch_shapes=[pltpu.VMEM((2, tq, LANES), BF16), pltpu.VMEM((2, tq, tq), F32),
                        pltpu.VMEM((2, tq, tq), F32), pltpu.VMEM((2, LANES, tq), F32)],
    )
    return pl.pallas_call(
        functools.partial(_band_body, tq=tq, n_off=n_off, l_init=l_init),
        grid_spec=grid_spec,
        out_shape=jax.ShapeDtypeStruct((b, s, 256), BF16),
        compiler_params=_params(("parallel", "parallel", "arbitrary")),
        name=name,
    )(m0, x3, x3, xvt, bias)


def _band_bias(slopes, tq, n_off, mult_fn):
    di = jnp.arange(tq, dtype=jnp.int32)[None, :]
    dj = jnp.arange(tq, dtype=jnp.int32)[:, None]
    off = jnp.arange(n_off, dtype=jnp.int32)[:, None, None]
    delta = off * tq + di - dj
    mult = mult_fn(delta)
    sl = jnp.asarray(slopes, F32)[:, None, None, None] * LOG2E
    val = -sl * delta.astype(F32)[None] + jnp.log2(jnp.maximum(mult, 1).astype(F32))[None]
    return jnp.where((mult > 0)[None], val, NEG_INF)


def _dil_mult(delta):
    m = jnp.zeros(delta.shape, jnp.int32)
    for window, d in DIL_PATTERNS:
        m = m + ((delta >= 0) & (delta <= window) & (delta % d == 0)).astype(jnp.int32)
    return m


def _swa_mult(delta):
    return ((delta >= 0) & (delta <= SWA_WINDOW - 1)).astype(jnp.int32)


def _outproj_body(*refs, with_router):
    if with_router:
        (oa, ob, oc, od, w_ref, x_ref, g_ref, rh_ref, rl_ref, x1_ref, h2_ref, lg_ref) = refs
    else:
        (oa, ob, oc, od, w_ref, x_ref, g_ref, x1_ref, h2_ref) = refs
    acc = x_ref[...]
    for i, o in enumerate((oa, ob, oc, od)):
        acc = acc + jnp.dot(o[...], w_ref[i], preferred_element_type=F32)
    x1_ref[...] = acc
    h2 = acc * lax.rsqrt(jnp.mean(acc * acc, axis=-1, keepdims=True) + RMS_EPS) * g_ref[...]
    hi = h2.astype(BF16)
    if with_router:
        for c in range(SC_CHUNKS):
            h2_ref[c] = h2[:, c * SC_COLS:(c + 1) * SC_COLS]
        lo = (h2 - hi.astype(F32)).astype(BF16)
        lg_ref[...] = (jnp.dot(hi, rh_ref[...], preferred_element_type=F32)
                       + (jnp.dot(hi, rl_ref[...], preferred_element_type=F32)
                          + jnp.dot(lo, rh_ref[...], preferred_element_type=F32)))
    else:
        h2_ref[...] = hi


def _outproj(oa, ob, oc, od, w4, x2d, g, router_hl=None):
    t = x2d.shape[0]
    tm = _pick(t, 512)
    with_router = router_hl is not None
    full = lambda shape: pl.BlockSpec(shape, lambda i: (0,) * len(shape))
    row = lambda w: pl.BlockSpec((tm, w), lambda i: (i, 0))
    in_specs = [row(256)] * 4 + [full((4, 256, D_MODEL)), row(D_MODEL), full((1, D_MODEL))]
    args = [oa, ob, oc, od, w4, x2d, g]
    out_specs = [row(D_MODEL), row(D_MODEL)]
    out_shape = [jax.ShapeDtypeStruct((t, D_MODEL), F32), jax.ShapeDtypeStruct((t, D_MODEL), BF16)]
    if with_router:
        out_specs[1] = pl.BlockSpec((SC_CHUNKS, tm, SC_COLS), lambda i: (0, i, 0))
        out_shape[1] = jax.ShapeDtypeStruct((SC_CHUNKS, t, SC_COLS), F32)
        in_specs += [full((D_MODEL, LANES))] * 2
        args += list(router_hl)
        out_specs.append(row(LANES))
        out_shape.append(jax.ShapeDtypeStruct((t, LANES), F32))
    return pl.pallas_call(
        functools.partial(_outproj_body, with_router=with_router),
        grid=(t // tm,), in_specs=in_specs, out_specs=out_specs, out_shape=out_shape,
        compiler_params=_params(("parallel",)),
        name="outproj",
    )(*args)


def _ffn_body(h_ref, w1_ref, w3_ref, w2_ref, x_ref, o_ref, acc_ref):
    f = pl.program_id(1)
    h = h_ref[...]
    a = jnp.dot(h, w1_ref[...], preferred_element_type=F32)
    b = jnp.dot(h, w3_ref[...], preferred_element_type=F32)
    mid = (a * jax.nn.sigmoid(a) * b).astype(BF16)
    contrib = jnp.dot(mid, w2_ref[...], preferred_element_type=F32)

    @pl.when(f == 0)
    def _():
        acc_ref[...] = contrib

    @pl.when(f > 0)
    def _():
        acc_ref[...] += contrib

    @pl.when(f == pl.num_programs(1) - 1)
    def _():
        o_ref[...] = x_ref[...] + acc_ref[...]


def _ffn(h2, w1, w3, w2, x1):
    t = h2.shape[0]
    tm = _pick(t, 512)
    tf = D_FF // 2
    return pl.pallas_call(
        _ffn_body,
        grid=(t // tm, D_FF // tf),
        in_specs=[pl.BlockSpec((tm, D_MODEL), lambda i, f: (i, 0)),
                  pl.BlockSpec((D_MODEL, tf), lambda i, f: (0, f)),
                  pl.BlockSpec((D_MODEL, tf), lambda i, f: (0, f)),
                  pl.BlockSpec((tf, D_MODEL), lambda i, f: (f, 0)),
                  pl.BlockSpec((tm, D_MODEL), lambda i, f: (i, 0))],
        out_specs=pl.BlockSpec((tm, D_MODEL), lambda i, f: (i, 0)),
        out_shape=jax.ShapeDtypeStruct((t, D_MODEL), F32),
        scratch_shapes=[pltpu.VMEM((tm, D_MODEL), F32)],
        compiler_params=_params(("parallel", "arbitrary")),
        name="ffn",
    )(h2, w1, w3, w2, x1)


def _route_body(lg_ref, tri_ref, info_ref, cnt_ref, carry_ref):
    i = pl.program_id(0)

    @pl.when(i == 0)
    def _():
        carry_ref[...] = jnp.zeros(carry_ref.shape, F32)

    lg = lg_ref[...]
    lane = lax.broadcasted_iota(jnp.int32, lg.shape, 1)
    lg = jnp.where(lane < N_EXPERTS, lg, -jnp.inf)
    m1 = jnp.max(lg, axis=-1, keepdims=True)
    i1 = jnp.min(jnp.where(lg == m1, lane, LANES), axis=-1, keepdims=True)
    oh1 = lane == i1
    lg2 = jnp.where(oh1, -jnp.inf, lg)
    m2 = jnp.max(lg2, axis=-1, keepdims=True)
    i2 = jnp.min(jnp.where(lg2 == m2, lane, LANES), axis=-1, keepdims=True)
    oh2 = lane == i2
    e2 = jnp.exp(m2 - m1)
    g1 = 1.0 / (1.0 + e2)
    g2 = e2 / (1.0 + e2)
    oh = jnp.where(jnp.logical_or(oh1, oh2), 1.0, 0.0)
    rank = jnp.dot(tri_ref[...], oh.astype(BF16), preferred_element_type=F32) + carry_ref[...]
    r1 = jnp.sum(jnp.where(oh1, rank, 0.0), axis=-1, keepdims=True)
    r2 = jnp.sum(jnp.where(oh2, rank, 0.0), axis=-1, keepdims=True)
    carry_ref[...] += jnp.sum(oh, axis=0, keepdims=True)
    cnt_ref[...] = carry_ref[...]
    info = jnp.zeros(lg.shape, F32)
    for c, val in enumerate((i1.astype(F32), i2.astype(F32), r1, r2, g1, g2)):
        info = jnp.where(lane == c, val, info)
    info_ref[...] = info


def _route(logits):
    t = logits.shape[0]
    tm = _pick(t, 512)
    tri = (jnp.arange(tm)[:, None] > jnp.arange(tm)[None, :]).astype(BF16)
    return pl.pallas_call(
        _route_body,
        grid=(t // tm,),
        in_specs=[pl.BlockSpec((tm, LANES), lambda i: (i, 0)), pl.BlockSpec((tm, tm), lambda i: (0, 0))],
        out_specs=[pl.BlockSpec((tm, LANES), lambda i: (i, 0)), pl.BlockSpec((1, LANES), lambda i: (0, 0))],
        out_shape=[jax.ShapeDtypeStruct((t, LANES), F32), jax.ShapeDtypeStruct((1, LANES), F32)],
        scratch_shapes=[pltpu.VMEM((1, LANES), F32)],
        compiler_params=_params(("arbitrary",)),
        name="route",
    )(logits, tri)


def _moe_ffn_body(be_ref, bv_ref, x_ref, w1_ref, w3_ref, w2_ref, o_ref, acc_ref):
    i, f = pl.program_id(0), pl.program_id(1)
    last = pl.num_programs(1) - 1
    valid = bv_ref[i]

    @pl.when(valid > 0)
    def _():
        x = jnp.concatenate([x_ref[c] for c in range(SC_CHUNKS)], axis=1)
        x = jnp.where(lax.broadcasted_iota(jnp.int32, x.shape, 0) < valid, x, 0.0).astype(BF16)
        a = jnp.dot(x, w1_ref[...], preferred_element_type=F32)
        b = jnp.dot(x, w3_ref[...], preferred_element_type=F32)
        mid = (a * jax.nn.sigmoid(a) * b).astype(BF16)
        contrib = jnp.dot(mid, w2_ref[...], preferred_element_type=F32)

        @pl.when(f == 0)
        def _():
            acc_ref[...] = contrib

        @pl.when(f > 0)
        def _():
            acc_ref[...] += contrib

        @pl.when(f == last)
        def _():
            for c in range(SC_CHUNKS):
                o_ref[c] = acc_ref[:, c * SC_COLS:(c + 1) * SC_COLS]

    @pl.when(jnp.logical_and(valid == 0, f == last))
    def _():
        o_ref[...] = jnp.zeros(o_ref.shape, o_ref.dtype)


def _moe_ffn(xs, blk_expert, blk_valid, w1, w3, w2, bm):
    rows = xs.shape[1]
    tf = D_FF_EXPERT // 4
    chunked = pl.BlockSpec((SC_CHUNKS, bm, SC_COLS), lambda i, f, be, bv: (0, i, 0))
    grid_spec = pltpu.PrefetchScalarGridSpec(
        num_scalar_prefetch=2,
        grid=(rows // bm, D_FF_EXPERT // tf),
        in_specs=[chunked,
                  pl.BlockSpec((None, D_MODEL, tf), lambda i, f, be, bv: (be[i], 0, f)),
                  pl.BlockSpec((None, D_MODEL, tf), lambda i, f, be, bv: (be[i], 0, f)),
                  pl.BlockSpec((None, tf, D_MODEL), lambda i, f, be, bv: (be[i], f, 0))],
        out_specs=chunked,
        scratch_shapes=[pltpu.VMEM((bm, D_MODEL), F32)],
    )
    return pl.pallas_call(
        _moe_ffn_body, grid_spec=grid_spec,
        out_shape=jax.ShapeDtypeStruct(xs.shape, F32),
        compiler_params=_params(("parallel", "arbitrary")),
        name="moe_ffn",
    )(blk_expert, blk_valid, xs, w1, w3, w2)


def _final_body(*refs, with_moe):
    if with_moe:
        x_ref, y0_ref, y1_ref, info_ref, g_ref, o_ref = refs
        info = info_ref[...]
        y0 = jnp.concatenate([y0_ref[c] for c in range(SC_CHUNKS)], axis=1)
        y1 = jnp.concatenate([y1_ref[c] for c in range(SC_CHUNKS)], axis=1)
        x = x_ref[...] + (info[:, 4:5] * y0 + info[:, 5:6] * y1)
    else:
        x_ref, g_ref, o_ref = refs
        x = x_ref[...]
    o_ref[...] = x * lax.rsqrt(jnp.mean(x * x, axis=-1, keepdims=True) + RMS_EPS) * g_ref[...]


def _final(x2d, g, moe=None):
    t = x2d.shape[0]
    tm = _pick(t, 512)
    row = lambda w: pl.BlockSpec((tm, w), lambda i: (i, 0))
    in_specs, args = [row(D_MODEL)], [x2d]
    if moe is not None:
        yg, info = moe
        slot = lambda k: pl.BlockSpec((None, SC_CHUNKS, tm, SC_COLS), lambda i: (k, 0, i, 0))
        in_specs += [slot(0), slot(1), row(LANES)]
        args += [yg, yg, info]
    in_specs.append(pl.BlockSpec((1, D_MODEL), lambda i: (0, 0)))
    args.append(g)
    return pl.pallas_call(
        functools.partial(_final_body, with_moe=moe is not None),
        grid=(t // tm,), in_specs=in_specs, out_specs=row(D_MODEL),
        out_shape=jax.ShapeDtypeStruct((t, D_MODEL), F32),
        compiler_params=_params(("parallel",)),
        name="final",
    )(*args)


def _prep_attn_weights(w_in, w_out, w_uq, w_ukv):
    widths = (256, 256, 256, 256, 256, 256, MLA_Q_LORA, MLA_KV_LORA, MLA_ROPE, 256, 128, 128)
    offs = np.concatenate([[0], np.cumsum(widths)])
    qa, ka, va, qb, kb, vb, cq, ckv, kpe, qd, kd, vd = [w_in[:, offs[i]:offs[i + 1]] for i in range(12)]
    hperm = np.array([0, 2, 1, 3])
    qd = qd.reshape(D_MODEL, 4, SWA_HD)[:, hperm].reshape(D_MODEL, 256)
    pad = jnp.zeros((D_MODEL, LANES - MLA_ROPE), w_in.dtype)
    half = MLA_ROPE // 2
    kpe_sw = jnp.concatenate([kpe[:, half:], kpe[:, :half]], axis=1)
    w_main = jnp.concatenate([qa, ka, va, qb, kb, vb, qd, kd, vd, cq, ckv, kpe, pad, kpe_sw, pad], axis=1)

    uq = w_uq.reshape(MLA_Q_LORA, MLA_HEADS, MLA_NOPE + MLA_ROPE)
    z = lambda n: jnp.zeros((MLA_Q_LORA, n), w_uq.dtype)
    mains, sws = [], []
    for hd in range(MLA_HEADS):
        nope, rope = uq[:, hd, :MLA_NOPE], uq[:, hd, MLA_NOPE:]
        nope128 = jnp.concatenate([nope, z(64)] if hd % 2 == 0 else [z(64), nope], axis=1)
        mains.append(jnp.concatenate([rope, z(LANES - MLA_ROPE), nope128], axis=1))
        sws.append(jnp.concatenate([rope[:, half:], rope[:, :half], z(LANES - MLA_ROPE)], axis=1))
    wq_big = jnp.concatenate(mains + sws, axis=1)

    ukv = w_ukv.reshape(MLA_KV_LORA, MLA_HEADS, MLA_NOPE + MLA_V)
    wkv = jnp.concatenate([ukv[:, :, :MLA_NOPE].reshape(MLA_KV_LORA, 256),
                           ukv[:, :, MLA_NOPE:].reshape(MLA_KV_LORA, 256)], axis=1)

    wo = w_out.reshape(4, 256, D_MODEL)
    wo_d = wo[3].reshape(4, SWA_HD, D_MODEL)[hperm].reshape(256, D_MODEL)
    w4 = jnp.stack([wo[0], wo[1], wo[2], wo_d])
    return w_main.astype(BF16), wq_big.astype(BF16), wkv.astype(BF16), w4.astype(BF16)


def _rope_tables(s):
    inv = ROPE_THETA ** (-jnp.arange(0, MLA_ROPE, 2, dtype=F32) / MLA_ROPE)
    ang = jnp.arange(s, dtype=F32)[:, None] * inv[None, :]
    cos, sin = jnp.cos(ang), jnp.sin(ang)
    z = jnp.zeros((s, LANES - MLA_ROPE), F32)
    cos128 = jnp.concatenate([cos, cos, z], axis=1)
    sin128 = jnp.concatenate([-sin, sin, z], axis=1)
    c_mla = (MLA_NOPE + MLA_ROPE) ** -0.5 * LOG2E
    return cos128 * c_mla, sin128 * c_mla, cos128, sin128


def _col_scale():
    ca = DIFF_HD ** -0.5 * LOG2E
    cb = DIL_HD ** -0.5 * LOG2E
    cd = SWA_HD ** -0.5 * LOG2E
    v = np.ones((1, N_SCALED), np.float32)
    v[0, 0:256] = ca
    v[0, 768:1024] = cb
    v[0, 1536:1792] = cd
    return jnp.asarray(v)


def _moe_layer(h2, logits, w1, w3, w2, bm):
    t = logits.shape[0]
    info, counts = _route(logits)
    cnt = counts[0, :N_EXPERTS].astype(jnp.int32)
    padded = (cnt + bm - 1) // bm * bm
    pends = jnp.cumsum(padded)
    pstarts = pends - padded
    e12 = info[:, 0:2].astype(jnp.int32)
    pos = pstarts[e12] + info[:, 2:4].astype(jnp.int32)
    nb = (t * TOP_K) // bm + N_EXPERTS
    rows = nb * bm
    blk_start = jnp.arange(nb, dtype=jnp.int32) * bm
    blk_expert = jnp.minimum(jnp.sum((pends[None, :] <= blk_start[:, None]).astype(jnp.int32), axis=1),
                             N_EXPERTS - 1)
    blk_valid = jnp.clip(pstarts[blk_expert] + cnt[blk_expert] - blk_start, 0, bm).astype(jnp.int32)
    sub = (jnp.arange(SC_CHUNKS, dtype=jnp.int32) * rows)[None, :, None] + pos.T[:, None, :]
    xs = _sc_scatter_rows(h2.reshape(SC_CHUNKS * t, SC_COLS),
                          [sub[k].reshape(1, SC_CHUNKS * t) for k in range(TOP_K)], SC_CHUNKS * rows)
    ys = _moe_ffn(xs.reshape(SC_CHUNKS, rows, SC_COLS), blk_expert.astype(jnp.int32), blk_valid, w1, w3, w2, bm)
    yg = _sc_gather_rows(ys.reshape(SC_CHUNKS * rows, SC_COLS), sub.reshape(1, TOP_K * SC_CHUNKS * t))
    return yg.reshape(TOP_K, SC_CHUNKS, t, SC_COLS), info


SC_WINDOW = 128
SC_COLS = 256
SC_CHUNKS = D_MODEL // SC_COLS


def _sc_mesh():
    return plsc.VectorSubcoreMesh(core_axis_name="c", subcore_axis_name="s")


def _sc_scatter_rows(x, idx_list, rows):
    t, d = x.shape
    n = len(idx_list)

    @functools.partial(pl.kernel, out_type=jax.ShapeDtypeStruct((rows, d), x.dtype), mesh=_sc_mesh(),
                       scratch_types=[])
    def scatter_kernel(x_hbm, *rest):
        idx_hbms, o_hbm = rest[:n], rest[n]

        def body(x_vmem, *idx_vmems):
            for iv in idx_vmems:
                pltpu.sync_copy(x_vmem, o_hbm.at[iv.at[0]])

        pltpu.emit_pipeline(
            body, grid=(t // SC_WINDOW,),
            in_specs=[pl.BlockSpec((SC_WINDOW, d), lambda i: (i, 0))]
            + [pl.BlockSpec((1, SC_WINDOW), lambda i: (0, i))] * n,
            out_specs=[], core_axis_name=("c", "s"), dimension_semantics=(pltpu.PARALLEL,),
        )(x_hbm, *idx_hbms)

    return scatter_kernel(x, *idx_list)


def _sc_gather_rows(y, idx):
    m = idx.shape[1]
    d = y.shape[1]

    @functools.partial(pl.kernel, out_type=jax.ShapeDtypeStruct((m, d), y.dtype), mesh=_sc_mesh(),
                       scratch_types=[])
    def gather_kernel(y_hbm, idx_hbm, o_hbm):
        def body(idx_vmem, o_vmem):
            pltpu.sync_copy(y_hbm.at[idx_vmem.at[0]], o_vmem)

        pltpu.emit_pipeline(
            body, grid=(m // SC_WINDOW,),
            in_specs=[pl.BlockSpec((1, SC_WINDOW), lambda i: (0, i))],
            out_specs=[pl.BlockSpec((SC_WINDOW, d), lambda i: (i, 0))],
            core_axis_name=("c", "s"), dimension_semantics=(pltpu.PARALLEL,),
        )(idx_hbm, o_hbm)

    return gather_kernel(y, idx)


def kernel(x, attn_norm, w_in, w_out, diff_lambda, diff_subln, mla_q_norm, mla_w_uq, mla_kv_norm, mla_w_ukv,
           swa_sinks, ffn_norm, ffn_w1, ffn_w3, ffn_w2, moe_router, moe_w1, moe_w3, moe_w2, final_norm):
    bsz, seq, _ = x.shape
    t = bsz * seq
    depth = w_in.shape[0]
    slopes_a, slopes_b, slopes_d = _alibi_slopes()
    tq_dense = _pick(seq, 512)
    tq_band = _pick(seq, 256)

    tabs = _rope_tables(seq)
    colscale = _col_scale()
    di = jnp.arange(tq_dense)
    causal_mask_t = jnp.where(di[:, None] <= di[None, :], 0.0, NEG_INF).astype(F32)
    alibi_kb = _alibi_key_columns(slopes_a * LOG2E, tq_dense)
    n_off_dil = min(max(w for w, _ in DIL_PATTERNS) // tq_band + 1, seq // tq_band)
    n_off_swa = min((SWA_WINDOW - 1 + tq_band - 1) // tq_band + 1, seq // tq_band)
    dil_bias = _band_bias(slopes_b, tq_band, n_off_dil, _dil_mult).reshape(2, 2, n_off_dil, tq_band, tq_band)
    swa_bias = _band_bias(slopes_d[np.array([0, 2, 1, 3])], tq_band, n_off_swa, _swa_mult
                          ).reshape(2, 2, n_off_swa, tq_band, tq_band)
    slopes_a_l2 = jnp.asarray(slopes_a * LOG2E, F32)
    dil_m0 = jnp.full((4,), NEG_INF, F32)

    x2d = x.reshape(t, D_MODEL)
    for l in range(depth):
        w_main, wq_big, wkv, w4 = _prep_attn_weights(w_in[l], w_out[l], mla_w_uq[l], mla_w_ukv[l])
        a, avt, b, bvt, d, dvt, cq, ck, cvt = _proj(x2d, attn_norm[l][None], w_main, colscale, mla_q_norm[l][None],
                                                    wq_big, mla_kv_norm[l][None], wkv, tabs, seq)
        a3, b3, d3 = (v.reshape(bsz, seq, -1) for v in (a, b, d))
        sub128 = jnp.concatenate([diff_subln[l], diff_subln[l]])[None]
        o_a = _diff_attention(a3, avt, slopes_a_l2, causal_mask_t, alibi_kb, diff_lambda[l], sub128, l, tq_dense)
        o_b = _band_attention(b3, bvt, lambda p: 2 + p, lambda p: p, dil_bias, dil_m0, 0.0, tq_band, "dil_attn")
        o_c = _mla_attention(cq.reshape(bsz, seq, -1), ck.reshape(bsz, seq, -1), cvt, causal_mask_t, tq_dense)
        swa_m0 = (swa_sinks[l].astype(F32) * LOG2E)[np.array([0, 2, 1, 3])]
        o_d = _band_attention(d3, dvt, lambda p: 2, lambda p: 0, swa_bias, swa_m0, 1.0, tq_band, "swa_attn")
        outs = [v.reshape(t, 256) for v in (o_a, o_b, o_c, o_d)]
        j = l // 2
        if l % 2 == 0:
            x1, h2 = _outproj(*outs, w4, x2d, ffn_norm[l][None])
            x2d = _ffn(h2, ffn_w1[j].astype(BF16), ffn_w3[j].astype(BF16), ffn_w2[j].astype(BF16), x1)
            moe = None
        else:
            r = jnp.pad(moe_router[j], ((0, 0), (0, LANES - N_EXPERTS)))
            r_hi, r_lo = _split_bf16(r, 2)
            x1, h2, logits = _outproj(*outs, w4, x2d, ffn_norm[l][None], (r_hi, r_lo))
            bm = _pick(t, 1024)
            moe = _moe_layer(h2, logits, moe_w1[j].astype(BF16), moe_w3[j].astype(BF16),
                             moe_w2[j].astype(BF16), bm)
            x2d = x1
        if l < depth - 1 and moe is not None:
            raise NotImplementedError("MoE combine is fused into the final norm; MoE layer must be last")
    return _final(x2d, final_norm[None], moe).reshape(bsz, seq, D_MODEL)
```

```python
import functools
import math

import numpy as np
import jax
import jax.numpy as jnp
from jax import lax
from jax.experimental import pallas as pl
from jax.experimental.pallas import tpu as pltpu
from jax.experimental.pallas import tpu_sc as plsc

D_MODEL = 1024
DIFF_HEADS, DIFF_HD = 4, 32
DIL_HEADS, DIL_HD = 4, 64
DIL_PATTERNS = ((128, 1), (512, 4), (2048, 16))
MLA_HEADS, MLA_Q_LORA, MLA_KV_LORA, MLA_NOPE, MLA_ROPE, MLA_V = 4, 384, 128, 64, 32, 64
ROPE_THETA = 10000.0
SWA_HEADS, SWA_KV_HEADS, SWA_HD, SWA_WINDOW = 4, 2, 64, 128
D_FF = 2816
N_EXPERTS, TOP_K, D_FF_EXPERT = 8, 2, 3584
RMS_EPS = 1e-6
NEG_INF = -1e30
N_ALIBI = DIFF_HEADS + DIL_HEADS + SWA_HEADS

LOG2E = 1.4426950408889634
LANES = 128
HEAD_LANES = 64
VMEM_LIMIT = 56 * 1024 * 1024

F32 = jnp.float32
BF16 = jnp.bfloat16

_NT = (((1,), (1,)), ((), ()))


def _params(sem, vmem=VMEM_LIMIT):
    return pltpu.CompilerParams(dimension_semantics=sem, vmem_limit_bytes=vmem)


def _alibi_slopes():
    s = 2.0 ** (-8.0 * (np.arange(N_ALIBI) + 1) / N_ALIBI)
    return s[0::3], s[1::3], s[2::3]


def _pick(n, pref):
    t = min(pref, n)
    while n % t:
        t //= 2
    return t


N_SCALED = 2048
C_CQ, C_CKV, C_KPE, C_KPE_SW, N_MAIN = 2048, 2432, 2560, 2688, 2816


def _vt_with_ones(v):
    vt = v.T
    ones = jnp.ones((HEAD_LANES, v.shape[0]), F32)
    parts = []
    for pair in range(v.shape[1] // LANES):
        h0 = vt[pair * LANES:pair * LANES + HEAD_LANES]
        h1 = vt[pair * LANES + HEAD_LANES:(pair + 1) * LANES]
        parts += [h0, ones, ones, h1]
    return jnp.concatenate(parts, axis=0).astype(BF16)


def _proj_body(x_ref, g_ref, w_ref, cs_ref, qn_ref, wq_ref, kn_ref, wkv_ref,
               cosq_ref, sinq_ref, cosk_ref, sink_ref,
               a_ref, avt_ref, b_ref, bvt_ref, d_ref, dvt_ref, cq_ref, ck_ref, cvt_ref):
    x = x_ref[...]
    ms = jnp.mean(x * x, axis=-1, keepdims=True)
    h = (x * lax.rsqrt(ms + RMS_EPS) * g_ref[...]).astype(BF16)
    acc = jnp.dot(h, w_ref[...], preferred_element_type=F32)
    sc = acc[:, :N_SCALED] * cs_ref[...]
    a_ref[...] = sc[:, 0:512].astype(BF16)
    avt_ref[...] = _vt_with_ones(sc[:, 512:768])
    b_ref[...] = sc[:, 768:1280].astype(BF16)
    bvt_ref[...] = _vt_with_ones(sc[:, 1280:1536])
    d_ref[...] = sc[:, 1536:1920].astype(BF16)
    dvt_ref[...] = _vt_with_ones(sc[:, 1920:2048])

    cq = acc[:, C_CQ:C_CKV]
    hq = (cq * lax.rsqrt(jnp.mean(cq * cq, axis=-1, keepdims=True) + RMS_EPS) * qn_ref[...]).astype(BF16)
    yq = jnp.dot(hq, wq_ref[...], preferred_element_type=F32)
    cosq, sinq = cosq_ref[...], sinq_ref[...]
    c_mla = (MLA_NOPE + MLA_ROPE) ** -0.5 * LOG2E
    for hd in range(MLA_HEADS):
        main = yq[:, hd * 256:(hd + 1) * 256]
        sw = yq[:, 1024 + hd * LANES:1024 + (hd + 1) * LANES]
        cq_ref[:, hd * 256:hd * 256 + LANES] = (main[:, :LANES] * cosq + sw * sinq).astype(BF16)
        cq_ref[:, hd * 256 + LANES:(hd + 1) * 256] = (main[:, LANES:] * c_mla).astype(BF16)

    ckv = acc[:, C_CKV:C_KPE]
    hk = (ckv * lax.rsqrt(jnp.mean(ckv * ckv, axis=-1, keepdims=True) + RMS_EPS) * kn_ref[...]).astype(BF16)
    kv = jnp.dot(hk, wkv_ref[...], preferred_element_type=F32)
    kr = (acc[:, C_KPE:C_KPE_SW] * cosk_ref[...] + acc[:, C_KPE_SW:N_MAIN] * sink_ref[...]).astype(BF16)
    for grp in range(2):
        ck_ref[:, grp * 256:grp * 256 + LANES] = kr
        ck_ref[:, grp * 256 + LANES:(grp + 1) * 256] = kv[:, grp * LANES:(grp + 1) * LANES].astype(BF16)
    cvt_ref[...] = _vt_with_ones(kv[:, 256:512])


def _proj(x2d, g, w_main, colscale, qn, wq_big, kn, wkv, tabs, seq):
    t = x2d.shape[0]
    tm = _pick(seq, 512)
    nsb = seq // tm
    full = lambda shape: pl.BlockSpec(shape, lambda i: (0,) * len(shape))
    tab = pl.BlockSpec((tm, LANES), lambda i: (i % nsb, 0))
    row = lambda w: pl.BlockSpec((tm, w), lambda i: (i, 0))
    vt = lambda w: pl.BlockSpec((None, w, tm), lambda i: (i // nsb, 0, i % nsb))
    row_sds = lambda w: jax.ShapeDtypeStruct((t, w), BF16)
    vt_sds = lambda w: jax.ShapeDtypeStruct((t // seq, w, seq), BF16)
    return pl.pallas_call(
        _proj_body,
        grid=(t // tm,),
        in_specs=[row(D_MODEL), full((1, D_MODEL)), full((D_MODEL, N_MAIN)), full((1, N_SCALED)),
                  full((1, MLA_Q_LORA)), full((MLA_Q_LORA, 1536)), full((1, MLA_KV_LORA)),
                  full((MLA_KV_LORA, 512)), tab, tab, tab, tab],
        out_specs=[row(512), vt(512), row(512), vt(512), row(384), vt(256), row(1024), row(512), vt(512)],
        out_shape=[row_sds(512), vt_sds(512), row_sds(512), vt_sds(512), row_sds(384), vt_sds(256),
                   row_sds(1024), row_sds(512), vt_sds(512)],
        compiler_params=_params(("parallel",)),
        name="proj",
    )(x2d, g, w_main, colscale, qn, wq_big, kn, wkv, *tabs)


def _attn_pipeline(n_soft, n_steps, scores, shift, load_vt, m0, s_ref, acc_ref):
    def stage_a(t, ss, dst_ref, m_cur):
        m_next, alpha = [], []
        for i in range(n_soft):
            m_new = jnp.maximum(m_cur[i], jnp.max(ss[i], axis=0, keepdims=True) + shift(i, t))
            dst_ref[i] = ss[i]
            m_next.append(m_new)
            alpha.append(jnp.exp2(m_cur[i] - m_new))
        return tuple(m_next), tuple(alpha)

    def stage_b(t, src_ref, carry):
        m_cur, alpha = carry
        for i in range(n_soft):
            p = jnp.exp2(src_ref[i] - (m_cur[i] - shift(i, t)))
            acc_ref[i] = alpha[i] * acc_ref[i] + jnp.dot(load_vt(t, i), p.astype(BF16),
                                                         preferred_element_type=F32)

    def half(t, src_ref, dst_ref, carry):
        ss = [scores(i, t + 1, False) for i in range(n_soft)]
        stage_b(t, src_ref, carry)
        return stage_a(t + 1, ss, dst_ref, carry[0])

    def body(u, carry):
        carry = half(2 * u, s_ref[0], s_ref[1], carry)
        return half(2 * u + 1, s_ref[1], s_ref[0], carry)

    carry = stage_a(0, [scores(i, 0, True) for i in range(n_soft)], s_ref[0], m0)
    carry = lax.fori_loop(0, n_steps // 2, body, carry)

    @pl.when(n_steps % 2 == 1)
    def _():
        stage_b(n_steps, s_ref[1], half(n_steps - 1, s_ref[0], s_ref[1], carry))

    @pl.when(n_steps % 2 == 0)
    def _():
        stage_b(n_steps, s_ref[0], carry)


def _row_mask(shape, lo, hi):
    row = lax.broadcasted_iota(jnp.int32, shape, 0)
    return jnp.logical_and(row >= lo, row < hi)


def _lane_mask(shape, lo, hi):
    lane = lax.broadcasted_iota(jnp.int32, shape, len(shape) - 1)
    return jnp.logical_and(lane >= lo, lane < hi)


def _denominator(acc_ref, i, head):
    row = HEAD_LANES if head == 0 else 0
    return acc_ref[i, row:row + 1, :]


def _diff_body(sl_ref, q_ref, k_ref, vt_ref, mask_ref, kb_ref, lam_ref, sub_ref, o_ref,
               qs_ref, s0_ref, s1_ref, acc_ref, *, tq, lam_init):
    pair, qi = pl.program_id(1), pl.program_id(2)
    tk = tq
    q = q_ref[...].astype(F32)
    for hd in range(2):
        sel = jnp.where(_lane_mask(q.shape, N_BIAS_SPLIT * hd, N_BIAS_SPLIT * (hd + 1)), 1.0, 0.0).astype(BF16)
        for mp in range(2):
            lo = hd * HEAD_LANES + mp * DIFF_HD
            qs_ref[2 * hd + mp, :, 0:LANES] = jnp.where(_lane_mask(q.shape, lo, lo + DIFF_HD), q, 0.0).astype(BF16)
            qs_ref[2 * hd + mp, :, LANES:2 * LANES] = sel
    acc_ref[...] = jnp.zeros(acc_ref.shape, F32)

    def scores(i, t, first):
        off = pl.multiple_of((qi - t) * tk, tk)
        k = jnp.concatenate([k_ref[pl.ds(off, tk), :], kb_ref[...]], axis=1)
        s = lax.dot_general(k, qs_ref[i], _NT, preferred_element_type=F32)
        return s + mask_ref[...] if first else s

    def shift(i, t):
        return sl_ref[2 * pair + i // 2] * (-(t * tk)).astype(F32) if not isinstance(t, int) else 0.0

    def load_vt(t, i):
        return vt_ref[pl.ds((i // 2) * LANES, LANES), pl.ds(pl.multiple_of((qi - t) * tk, tk), tk)]

    m0 = tuple(jnp.full((1, tq), NEG_INF, F32) for _ in range(4))
    _attn_pipeline(4, qi, scores, shift, load_vt, m0, (s0_ref, s1_ref), acc_ref)

    lp = lam_ref[...]
    lam = (jnp.exp(jnp.sum(lp[0:1, :] * lp[1:2, :], axis=-1, keepdims=True))
           - jnp.exp(jnp.sum(lp[2:3, :] * lp[3:4, :], axis=-1, keepdims=True)) + lam_init)
    o_h = [acc_ref[2 * hd] / _denominator(acc_ref, 2 * hd, hd)
           - lam * (acc_ref[2 * hd + 1] / _denominator(acc_ref, 2 * hd + 1, hd)) for hd in range(2)]
    o = jnp.where(_row_mask((LANES, tq), 0, HEAD_LANES), o_h[0], o_h[1]).T
    in_h0 = _lane_mask((tq, LANES), 0, HEAD_LANES)
    sq = o * o
    ms0 = jnp.sum(jnp.where(in_h0, sq, 0.0), axis=-1, keepdims=True)
    ms1 = jnp.sum(jnp.where(in_h0, 0.0, sq), axis=-1, keepdims=True)
    ms = jnp.where(in_h0, ms0, ms1) * (1.0 / HEAD_LANES)
    y = o * lax.rsqrt(ms + RMS_EPS) * sub_ref[...]
    o_ref[...] = (y * (1.0 - lam_init)).astype(o_ref.dtype)


N_BIAS_SPLIT = 3


def _alibi_key_columns(slopes_l2, tk):
    dj = jnp.arange(tk, dtype=F32)[None, :, None]
    val = jnp.asarray(slopes_l2, F32).reshape(-1, 1, 2) * dj
    cols = jnp.stack(_split_bf16(val, N_BIAS_SPLIT), axis=-1).reshape(val.shape[0], tk, 2 * N_BIAS_SPLIT)
    return jnp.pad(cols, ((0, 0), (0, 0), (0, LANES - 2 * N_BIAS_SPLIT)))


def _split_bf16(x, n):
    pieces = []
    for _ in range(n):
        bits = lax.bitcast_convert_type(x, jnp.uint32) & jnp.uint32(0xFFFF0000)
        head = lax.bitcast_convert_type(bits, F32)
        pieces.append(head.astype(BF16))
        x = x - head
    return pieces


def _diff_attention(a3, avt, slopes_l2, mask_t, kb, lam_p, sub128, layer_idx, tq):
    b, s, _ = a3.shape
    nq = s // tq
    lam_init = 0.8 - 0.6 * math.exp(-0.3 * layer_idx)
    body = functools.partial(_diff_body, tq=tq, lam_init=lam_init)
    grid_spec = pltpu.PrefetchScalarGridSpec(
        num_scalar_prefetch=1,
        grid=(b, 2, nq),
        in_specs=[
            pl.BlockSpec((None, tq, LANES), lambda bi, p, qi, sl: (bi, qi, p)),
            pl.BlockSpec((None, s, LANES), lambda bi, p, qi, sl: (bi, 0, 2 + p)),
            pl.BlockSpec((None, 2 * LANES, s), lambda bi, p, qi, sl: (bi, p, 0)),
            pl.BlockSpec((tq, tq), lambda bi, p, qi, sl: (0, 0)),
            pl.BlockSpec((None, tq, LANES), lambda bi, p, qi, sl: (p, 0, 0)),
            pl.BlockSpec((4, DIFF_HD), lambda bi, p, qi, sl: (0, 0)),
            pl.BlockSpec((1, LANES), lambda bi, p, qi, sl: (0, 0)),
        ],
        out_specs=pl.BlockSpec((None, tq, LANES), lambda bi, p, qi, sl: (bi, qi, p)),
        scratch_shapes=[pltpu.VMEM((4, tq, 2 * LANES), BF16), pltpu.VMEM((4, tq, tq), F32),
                        pltpu.VMEM((4, tq, tq), F32), pltpu.VMEM((4, LANES, tq), F32)],
    )
    return pl.pallas_call(
        body, grid_spec=grid_spec,
        out_shape=jax.ShapeDtypeStruct((b, s, 256), BF16),
        compiler_params=_params(("parallel", "parallel", "arbitrary")),
        name="diff_attn",
    )(slopes_l2, a3, a3, avt, mask_t, kb, lam_p, sub128)


def _mla_body(q0_ref, q1_ref, k_ref, vt_ref, mask_ref, o_ref, s0_ref, s1_ref, acc_ref, *, tq):
    qi = pl.program_id(2)
    tk = tq
    acc_ref[...] = jnp.zeros(acc_ref.shape, F32)
    q_refs = (q0_ref, q1_ref)

    def scores(i, t, first):
        k = k_ref[pl.ds(pl.multiple_of((qi - t) * tk, tk), tk), :]
        s = lax.dot_general(k, q_refs[i][...], _NT, preferred_element_type=F32)
        return s + mask_ref[...] if first else s

    def load_vt(t, i):
        return vt_ref[pl.ds(i * LANES, LANES), pl.ds(pl.multiple_of((qi - t) * tk, tk), tk)]

    m0 = tuple(jnp.full((1, tq), NEG_INF, F32) for _ in range(2))
    _attn_pipeline(2, qi, scores, lambda i, t: 0.0, load_vt, m0, (s0_ref, s1_ref), acc_ref)
    o_t = jnp.where(_row_mask((LANES, tq), 0, HEAD_LANES), acc_ref[0] / _denominator(acc_ref, 0, 0),
                    acc_ref[1] / _denominator(acc_ref, 1, 1))
    o_ref[...] = o_t.T.astype(o_ref.dtype)


def _mla_attention(cq3, ck3, cvt, mask_t, tq):
    b, s, _ = cq3.shape
    nq = s // tq
    return pl.pallas_call(
        functools.partial(_mla_body, tq=tq),
        grid=(b, 2, nq),
        in_specs=[
            pl.BlockSpec((None, tq, 256), lambda bi, g, qi: (bi, qi, 2 * g)),
            pl.BlockSpec((None, tq, 256), lambda bi, g, qi: (bi, qi, 2 * g + 1)),
            pl.BlockSpec((None, s, 256), lambda bi, g, qi: (bi, 0, g)),
            pl.BlockSpec((None, 2 * LANES, s), lambda bi, g, qi: (bi, g, 0)),
            pl.BlockSpec((tq, tq), lambda bi, g, qi: (0, 0)),
        ],
        out_specs=pl.BlockSpec((None, tq, LANES), lambda bi, g, qi: (bi, qi, g)),
        out_shape=jax.ShapeDtypeStruct((b, s, 256), BF16),
        scratch_shapes=[pltpu.VMEM((2, tq, tq), F32), pltpu.VMEM((2, tq, tq), F32),
                        pltpu.VMEM((2, LANES, tq), F32)],
        compiler_params=_params(("parallel", "parallel", "arbitrary")),
        name="mla_attn",
    )(cq3, cq3, ck3, cvt, mask_t)


def _band_body(m0_ref, q_ref, k_ref, vt_ref, bias_ref, o_ref, qs_ref, s0_ref, s1_ref, acc_ref,
               *, tq, n_off, l_init):
    slab, qi = pl.program_id(1), pl.program_id(2)
    tk = tq
    q = q_ref[...].astype(F32)
    in_h0 = _row_mask((LANES, tq), 0, HEAD_LANES)
    for i in range(2):
        qs_ref[i] = jnp.where(_lane_mask(q.shape, i * HEAD_LANES, (i + 1) * HEAD_LANES), q, 0.0).astype(BF16)
        acc_ref[i] = jnp.where(in_h0 if i == 1 else jnp.logical_not(in_h0), l_init, 0.0).astype(F32)

    def scores(i, t, first):
        k = k_ref[pl.ds(pl.multiple_of((qi - t) * tk, tk), tk), :]
        return lax.dot_general(k, qs_ref[i], _NT, preferred_element_type=F32) + bias_ref[i, t]

    def load_vt(t, i):
        return vt_ref[pl.ds(i * LANES, LANES), pl.ds(pl.multiple_of((qi - t) * tk, tk), tk)]

    m0 = tuple(jnp.full((1, tq), m0_ref[2 * slab + i], F32) for i in range(2))
    _attn_pipeline(2, jnp.minimum(qi, n_off - 1), scores, lambda i, t: 0.0, load_vt, m0, (s0_ref, s1_ref), acc_ref)
    o_t = jnp.where(in_h0, acc_ref[0] / _denominator(acc_ref, 0, 0), acc_ref[1] / _denominator(acc_ref, 1, 1))
    o_ref[...] = o_t.T.astype(o_ref.dtype)


def _band_attention(x3, xvt, k_blk_of, vt_blk_of, bias, m0, l_init, tq, name):
    b, s, _ = x3.shape
    nq = s // tq
    n_off = bias.shape[2]
    grid_spec = pltpu.PrefetchScalarGridSpec(
        num_scalar_prefetch=1,
        grid=(b, 2, nq),
        in_specs=[
            pl.BlockSpec((None, tq, LANES), lambda bi, p, qi, m0r: (bi, qi, p)),
            pl.BlockSpec((None, s, LANES), lambda bi, p, qi, m0r: (bi, 0, k_blk_of(p))),
            pl.BlockSpec((None, 2 * LANES, s), lambda bi, p, qi, m0r: (bi, vt_blk_of(p), 0)),
            pl.BlockSpec((None, 2, n_off, tq, tq), lambda bi, p, qi, m0r: (p, 0, 0, 0, 0)),
        ],
        out_specs=pl.BlockSpec((None, tq, LANES), lambda bi, p, qi, m0r: (bi, qi, p)),
        scratch_shapes=[pltpu.VMEM((2, tq, LANES), BF16), pltpu.VMEM((2, tq, tq), F32),
                        pltpu.VMEM((2, tq, tq), F32), pltpu.VMEM((2, LANES, tq), F32)],
    )
    return pl.pallas_call(
        functools.partial(_band_body, tq=tq, n_off=n_off, l_init=l_init),
        grid_spec=grid_spec,
        out_shape=jax.ShapeDtypeStruct((b, s, 256), BF16),
        compiler_params=_params(("parallel", "parallel", "arbitrary")),
        name=name,
    )(m0, x3, x3, xvt, bias)


def _band_bias(slopes, tq, n_off, mult_fn):
    di = jnp.arange(tq, dtype=jnp.int32)[None, :]
    dj = jnp.arange(tq, dtype=jnp.int32)[:, None]
    off = jnp.arange(n_off, dtype=jnp.int32)[:, None, None]
    delta = off * tq + di - dj
    mult = mult_fn(delta)
    sl = jnp.asarray(slopes, F32)[:, None, None, None] * LOG2E
    val = -sl * delta.astype(F32)[None] + jnp.log2(jnp.maximum(mult, 1).astype(F32))[None]
    return jnp.where((mult > 0)[None], val, NEG_INF)


def _dil_mult(delta):
    m = jnp.zeros(delta.shape, jnp.int32)
    for window, d in DIL_PATTERNS:
        m = m + ((delta >= 0) & (delta <= window) & (delta % d == 0)).astype(jnp.int32)
    return m


def _swa_mult(delta):
    return ((delta >= 0) & (delta <= SWA_WINDOW - 1)).astype(jnp.int32)


def _outproj_body(*refs, with_router):
    if with_router:
        (oa, ob, oc, od, w_ref, x_ref, g_ref, rh_ref, rl_ref, x1_ref, h2_ref, lg_ref) = refs
    else:
        (oa, ob, oc, od, w_ref, x_ref, g_ref, x1_ref, h2_ref) = refs
    acc = x_ref[...]
    for i, o in enumerate((oa, ob, oc, od)):
        acc = acc + jnp.dot(o[...], w_ref[i], preferred_element_type=F32)
    x1_ref[...] = acc
    h2 = acc * lax.rsqrt(jnp.mean(acc * acc, axis=-1, keepdims=True) + RMS_EPS) * g_ref[...]
    hi = h2.astype(BF16)
    if with_router:
        for c in range(SC_CHUNKS):
            h2_ref[c] = h2[:, c * SC_COLS:(c + 1) * SC_COLS]
        lo = (h2 - hi.astype(F32)).astype(BF16)
        lg_ref[...] = (jnp.dot(hi, rh_ref[...], preferred_element_type=F32)
                       + (jnp.dot(hi, rl_ref[...], preferred_element_type=F32)
                          + jnp.dot(lo, rh_ref[...], preferred_element_type=F32)))
    else:
        h2_ref[...] = hi


def _outproj(oa, ob, oc, od, w4, x2d, g, router_hl=None):
    t = x2d.shape[0]
    tm = _pick(t, 512)
    with_router = router_hl is not None
    full = lambda shape: pl.BlockSpec(shape, lambda i: (0,) * len(shape))
    row = lambda w: pl.BlockSpec((tm, w), lambda i: (i, 0))
    in_specs = [row(256)] * 4 + [full((4, 256, D_MODEL)), row(D_MODEL), full((1, D_MODEL))]
    args = [oa, ob, oc, od, w4, x2d, g]
    out_specs = [row(D_MODEL), row(D_MODEL)]
    out_shape = [jax.ShapeDtypeStruct((t, D_MODEL), F32), jax.ShapeDtypeStruct((t, D_MODEL), BF16)]
    if with_router:
        out_specs[1] = pl.BlockSpec((SC_CHUNKS, tm, SC_COLS), lambda i: (0, i, 0))
        out_shape[1] = jax.ShapeDtypeStruct((SC_CHUNKS, t, SC_COLS), F32)
        in_specs += [full((D_MODEL, LANES))] * 2
        args += list(router_hl)
        out_specs.append(row(LANES))
        out_shape.append(jax.ShapeDtypeStruct((t, LANES), F32))
    return pl.pallas_call(
        functools.partial(_outproj_body, with_router=with_router),
        grid=(t // tm,), in_specs=in_specs, out_specs=out_specs, out_shape=out_shape,
        compiler_params=_params(("parallel",)),
        name="outproj",
    )(*args)


def _ffn_body(h_ref, w1_ref, w3_ref, w2_ref, x_ref, o_ref, acc_ref):
    f = pl.program_id(1)
    h = h_ref[...]
    a = jnp.dot(h, w1_ref[...], preferred_element_type=F32)
    b = jnp.dot(h, w3_ref[...], preferred_element_type=F32)
    mid = (a * jax.nn.sigmoid(a) * b).astype(BF16)
    contrib = jnp.dot(mid, w2_ref[...], preferred_element_type=F32)

    @pl.when(f == 0)
    def _():
        acc_ref[...] = contrib

    @pl.when(f > 0)
    def _():
        acc_ref[...] += contrib

    @pl.when(f == pl.num_programs(1) - 1)
    def _():
        o_ref[...] = x_ref[...] + acc_ref[...]


def _ffn(h2, w1, w3, w2, x1):
    t = h2.shape[0]
    tm = _pick(t, 512)
    tf = D_FF // 2
    return pl.pallas_call(
        _ffn_body,
        grid=(t // tm, D_FF // tf),
        in_specs=[pl.BlockSpec((tm, D_MODEL), lambda i, f: (i, 0)),
                  pl.BlockSpec((D_MODEL, tf), lambda i, f: (0, f)),
                  pl.BlockSpec((D_MODEL, tf), lambda i, f: (0, f)),
                  pl.BlockSpec((tf, D_MODEL), lambda i, f: (f, 0)),
                  pl.BlockSpec((tm, D_MODEL), lambda i, f: (i, 0))],
        out_specs=pl.BlockSpec((tm, D_MODEL), lambda i, f: (i, 0)),
        out_shape=jax.ShapeDtypeStruct((t, D_MODEL), F32),
        scratch_shapes=[pltpu.VMEM((tm, D_MODEL), F32)],
        compiler_params=_params(("parallel", "arbitrary")),
        name="ffn",
    )(h2, w1, w3, w2, x1)


def _route_body(lg_ref, tri_ref, info_ref, cnt_ref, carry_ref):
    i = pl.program_id(0)

    @pl.when(i == 0)
    def _():
        carry_ref[...] = jnp.zeros(carry_ref.shape, F32)

    lg = lg_ref[...]
    lane = lax.broadcasted_iota(jnp.int32, lg.shape, 1)
    lg = jnp.where(lane < N_EXPERTS, lg, -jnp.inf)
    m1 = jnp.max(lg, axis=-1, keepdims=True)
    i1 = jnp.min(jnp.where(lg == m1, lane, LANES), axis=-1, keepdims=True)
    oh1 = lane == i1
    lg2 = jnp.where(oh1, -jnp.inf, lg)
    m2 = jnp.max(lg2, axis=-1, keepdims=True)
    i2 = jnp.min(jnp.where(lg2 == m2, lane, LANES), axis=-1, keepdims=True)
    oh2 = lane == i2
    e2 = jnp.exp(m2 - m1)
    g1 = 1.0 / (1.0 + e2)
    g2 = e2 / (1.0 + e2)
    oh = jnp.where(jnp.logical_or(oh1, oh2), 1.0, 0.0)
    rank = jnp.dot(tri_ref[...], oh.astype(BF16), preferred_element_type=F32) + carry_ref[...]
    r1 = jnp.sum(jnp.where(oh1, rank, 0.0), axis=-1, keepdims=True)
    r2 = jnp.sum(jnp.where(oh2, rank, 0.0), axis=-1, keepdims=True)
    carry_ref[...] += jnp.sum(oh, axis=0, keepdims=True)
    cnt_ref[...] = carry_ref[...]
    info = jnp.zeros(lg.shape, F32)
    for c, val in enumerate((i1.astype(F32), i2.astype(F32), r1, r2, g1, g2)):
        info = jnp.where(lane == c, val, info)
    info_ref[...] = info


def _route(logits):
    t = logits.shape[0]
    tm = _pick(t, 512)
    tri = (jnp.arange(tm)[:, None] > jnp.arange(tm)[None, :]).astype(BF16)
    return pl.pallas_call(
        _route_body,
        grid=(t // tm,),
        in_specs=[pl.BlockSpec((tm, LANES), lambda i: (i, 0)), pl.BlockSpec((tm, tm), lambda i: (0, 0))],
        out_specs=[pl.BlockSpec((tm, LANES), lambda i: (i, 0)), pl.BlockSpec((1, LANES), lambda i: (0, 0))],
        out_shape=[jax.ShapeDtypeStruct((t, LANES), F32), jax.ShapeDtypeStruct((1, LANES), F32)],
        scratch_shapes=[pltpu.VMEM((1, LANES), F32)],
        compiler_params=_params(("arbitrary",)),
        name="route",
    )(logits, tri)


def _moe_ffn_body(be_ref, bv_ref, x_ref, w1_ref, w3_ref, w2_ref, o_ref, acc_ref):
    i, f = pl.program_id(0), pl.program_id(1)
    last = pl.num_programs(1) - 1
    valid = bv_ref[i]

    @pl.when(valid > 0)
    def _():
        x = jnp.concatenate([x_ref[c] for c in range(SC_CHUNKS)], axis=1)
        x = jnp.where(lax.broadcasted_iota(jnp.int32, x.shape, 0) < valid, x, 0.0).astype(BF16)
        a = jnp.dot(x, w1_ref[...], preferred_element_type=F32)
        b = jnp.dot(x, w3_ref[...], preferred_element_type=F32)
        mid = (a * jax.nn.sigmoid(a) * b).astype(BF16)
        contrib = jnp.dot(mid, w2_ref[...], preferred_element_type=F32)

        @pl.when(f == 0)
        def _():
            acc_ref[...] = contrib

        @pl.when(f > 0)
        def _():
            acc_ref[...] += contrib

        @pl.when(f == last)
        def _():
            for c in range(SC_CHUNKS):
                o_ref[c] = acc_ref[:, c * SC_COLS:(c + 1) * SC_COLS]

    @pl.when(jnp.logical_and(valid == 0, f == last))
    def _():
        o_ref[...] = jnp.zeros(o_ref.shape, o_ref.dtype)


def _moe_ffn(xs, blk_expert, blk_valid, w1, w3, w2, bm):
    rows = xs.shape[1]
    tf = D_FF_EXPERT // 4
    chunked = pl.BlockSpec((SC_CHUNKS, bm, SC_COLS), lambda i, f, be, bv: (0, i, 0))
    grid_spec = pltpu.PrefetchScalarGridSpec(
        num_scalar_prefetch=2,
        grid=(rows // bm, D_FF_EXPERT // tf),
        in_specs=[chunked,
                  pl.BlockSpec((None, D_MODEL, tf), lambda i, f, be, bv: (be[i], 0, f)),
                  pl.BlockSpec((None, D_MODEL, tf), lambda i, f, be, bv: (be[i], 0, f)),
                  pl.BlockSpec((None, tf, D_MODEL), lambda i, f, be, bv: (be[i], f, 0))],
        out_specs=chunked,
        scratch_shapes=[pltpu.VMEM((bm, D_MODEL), F32)],
    )
    return pl.pallas_call(
        _moe_ffn_body, grid_spec=grid_spec,
        out_shape=jax.ShapeDtypeStruct(xs.shape, F32),
        compiler_params=_params(("parallel", "arbitrary")),
        name="moe_ffn",
    )(blk_expert, blk_valid, xs, w1, w3, w2)


def _final_body(*refs, with_moe):
    if with_moe:
        x_ref, y0_ref, y1_ref, info_ref, g_ref, o_ref = refs
        info = info_ref[...]
        y0 = jnp.concatenate([y0_ref[c] for c in range(SC_CHUNKS)], axis=1)
        y1 = jnp.concatenate([y1_ref[c] for c in range(SC_CHUNKS)], axis=1)
        x = x_ref[...] + (info[:, 4:5] * y0 + info[:, 5:6] * y1)
    else:
        x_ref, g_ref, o_ref = refs
        x = x_ref[...]
    o_ref[...] = x * lax.rsqrt(jnp.mean(x * x, axis=-1, keepdims=True) + RMS_EPS) * g_ref[...]


def _final(x2d, g, moe=None):
    t = x2d.shape[0]
    tm = _pick(t, 512)
    row = lambda w: pl.BlockSpec((tm, w), lambda i: (i, 0))
    in_specs, args = [row(D_MODEL)], [x2d]
    if moe is not None:
        yg, info = moe
        slot = lambda k: pl.BlockSpec((None, SC_CHUNKS, tm, SC_COLS), lambda i: (k, 0, i, 0))
        in_specs += [slot(0), slot(1), row(LANES)]
        args += [yg, yg, info]
    in_specs.append(pl.BlockSpec((1, D_MODEL), lambda i: (0, 0)))
    args.append(g)
    return pl.pallas_call(
        functools.partial(_final_body, with_moe=moe is not None),
        grid=(t // tm,), in_specs=in_specs, out_specs=row(D_MODEL),
        out_shape=jax.ShapeDtypeStruct((t, D_MODEL), F32),
        compiler_params=_params(("parallel",)),
        name="final",
    )(*args)


def _prep_attn_weights(w_in, w_out, w_uq, w_ukv):
    widths = (256, 256, 256, 256, 256, 256, MLA_Q_LORA, MLA_KV_LORA, MLA_ROPE, 256, 128, 128)
    offs = np.concatenate([[0], np.cumsum(widths)])
    qa, ka, va, qb, kb, vb, cq, ckv, kpe, qd, kd, vd = [w_in[:, offs[i]:offs[i + 1]] for i in range(12)]
    hperm = np.array([0, 2, 1, 3])
    qd = qd.reshape(D_MODEL, 4, SWA_HD)[:, hperm].reshape(D_MODEL, 256)
    pad = jnp.zeros((D_MODEL, LANES - MLA_ROPE), w_in.dtype)
    half = MLA_ROPE // 2
    kpe_sw = jnp.concatenate([kpe[:, half:], kpe[:, :half]], axis=1)
    w_main = jnp.concatenate([qa, ka, va, qb, kb, vb, qd, kd, vd, cq, ckv, kpe, pad, kpe_sw, pad], axis=1)

    uq = w_uq.reshape(MLA_Q_LORA, MLA_HEADS, MLA_NOPE + MLA_ROPE)
    z = lambda n: jnp.zeros((MLA_Q_LORA, n), w_uq.dtype)
    mains, sws = [], []
    for hd in range(MLA_HEADS):
        nope, rope = uq[:, hd, :MLA_NOPE], uq[:, hd, MLA_NOPE:]
        nope128 = jnp.concatenate([nope, z(64)] if hd % 2 == 0 else [z(64), nope], axis=1)
        mains.append(jnp.concatenate([rope, z(LANES - MLA_ROPE), nope128], axis=1))
        sws.append(jnp.concatenate([rope[:, half:], rope[:, :half], z(LANES - MLA_ROPE)], axis=1))
    wq_big = jnp.concatenate(mains + sws, axis=1)

    ukv = w_ukv.reshape(MLA_KV_LORA, MLA_HEADS, MLA_NOPE + MLA_V)
    wkv = jnp.concatenate([ukv[:, :, :MLA_NOPE].reshape(MLA_KV_LORA, 256),
                           ukv[:, :, MLA_NOPE:].reshape(MLA_KV_LORA, 256)], axis=1)

    wo = w_out.reshape(4, 256, D_MODEL)
    wo_d = wo[3].reshape(4, SWA_HD, D_MODEL)[hperm].reshape(256, D_MODEL)
    w4 = jnp.stack([wo[0], wo[1], wo[2], wo_d])
    return w_main.astype(BF16), wq_big.astype(BF16), wkv.astype(BF16), w4.astype(BF16)


def _rope_tables(s):
    inv = ROPE_THETA ** (-jnp.arange(0, MLA_ROPE, 2, dtype=F32) / MLA_ROPE)
    ang = jnp.arange(s, dtype=F32)[:, None] * inv[None, :]
    cos, sin = jnp.cos(ang), jnp.sin(ang)
    z = jnp.zeros((s, LANES - MLA_ROPE), F32)
    cos128 = jnp.concatenate([cos, cos, z], axis=1)
    sin128 = jnp.concatenate([-sin, sin, z], axis=1)
    c_mla = (MLA_NOPE + MLA_ROPE) ** -0.5 * LOG2E
    return cos128 * c_mla, sin128 * c_mla, cos128, sin128


def _col_scale():
    ca = DIFF_HD ** -0.5 * LOG2E
    cb = DIL_HD ** -0.5 * LOG2E
    cd = SWA_HD ** -0.5 * LOG2E
    v = np.ones((1, N_SCALED), np.float32)
    v[0, 0:256] = ca
    v[0, 768:1024] = cb
    v[0, 1536:1792] = cd
    return jnp.asarray(v)


def _moe_layer(h2, logits, w1, w3, w2, bm):
    t = logits.shape[0]
    info, counts = _route(logits)
    cnt = counts[0, :N_EXPERTS].astype(jnp.int32)
    padded = (cnt + bm - 1) // bm * bm
    pends = jnp.cumsum(padded)
    pstarts = pends - padded
    e12 = info[:, 0:2].astype(jnp.int32)
    pos = pstarts[e12] + info[:, 2:4].astype(jnp.int32)
    nb = (t * TOP_K) // bm + N_EXPERTS
    rows = nb * bm
    blk_start = jnp.arange(nb, dtype=jnp.int32) * bm
    blk_expert = jnp.minimum(jnp.sum((pends[None, :] <= blk_start[:, None]).astype(jnp.int32), axis=1),
                             N_EXPERTS - 1)
    blk_valid = jnp.clip(pstarts[blk_expert] + cnt[blk_expert] - blk_start, 0, bm).astype(jnp.int32)
    sub = (jnp.arange(SC_CHUNKS, dtype=jnp.int32) * rows)[None, :, None] + pos.T[:, None, :]
    xs = _sc_scatter_rows(h2.reshape(SC_CHUNKS * t, SC_COLS),
                          [sub[k].reshape(1, SC_CHUNKS * t) for k in range(TOP_K)], SC_CHUNKS * rows)
    ys = _moe_ffn(xs.reshape(SC_CHUNKS, rows, SC_COLS), blk_expert.astype(jnp.int32), blk_valid, w1, w3, w2, bm)
    yg = _sc_gather_rows(ys.reshape(SC_CHUNKS * rows, SC_COLS), sub.reshape(1, TOP_K * SC_CHUNKS * t))
    return yg.reshape(TOP_K, SC_CHUNKS, t, SC_COLS), info


SC_WINDOW = 128
SC_COLS = 256
SC_CHUNKS = D_MODEL // SC_COLS


def _sc_mesh():
    return plsc.VectorSubcoreMesh(core_axis_name="c", subcore_axis_name="s")


def _sc_scatter_rows(x, idx_list, rows):
    t, d = x.shape
    n = len(idx_list)

    @functools.partial(pl.kernel, out_type=jax.ShapeDtypeStruct((rows, d), x.dtype), mesh=_sc_mesh(),
                       scratch_types=[])
    def scatter_kernel(x_hbm, *rest):
        idx_hbms, o_hbm = rest[:n], rest[n]

        def body(x_vmem, *idx_vmems):
            for iv in idx_vmems:
                pltpu.sync_copy(x_vmem, o_hbm.at[iv.at[0]])

        pltpu.emit_pipeline(
            body, grid=(t // SC_WINDOW,),
            in_specs=[pl.BlockSpec((SC_WINDOW, d), lambda i: (i, 0))]
            + [pl.BlockSpec((1, SC_WINDOW), lambda i: (0, i))] * n,
            out_specs=[], core_axis_name=("c", "s"), dimension_semantics=(pltpu.PARALLEL,),
        )(x_hbm, *idx_hbms)

    return scatter_kernel(x, *idx_list)


def _sc_gather_rows(y, idx):
    m = idx.shape[1]
    d = y.shape[1]

    @functools.partial(pl.kernel, out_type=jax.ShapeDtypeStruct((m, d), y.dtype), mesh=_sc_mesh(),
                       scratch_types=[])
    def gather_kernel(y_hbm, idx_hbm, o_hbm):
        def body(idx_vmem, o_vmem):
            pltpu.sync_copy(y_hbm.at[idx_vmem.at[0]], o_vmem)

        pltpu.emit_pipeline(
            body, grid=(m // SC_WINDOW,),
            in_specs=[pl.BlockSpec((1, SC_WINDOW), lambda i: (0, i))],
            out_specs=[pl.BlockSpec((SC_WINDOW, d), lambda i: (i, 0))],
            core_axis_name=("c", "s"), dimension_semantics=(pltpu.PARALLEL,),
        )(idx_hbm, o_hbm)

    return gather_kernel(y, idx)


def kernel(x, attn_norm, w_in, w_out, diff_lambda, diff_subln, mla_q_norm, mla_w_uq, mla_kv_norm, mla_w_ukv,
           swa_sinks, ffn_norm, ffn_w1, ffn_w3, ffn_w2, moe_router, moe_w1, moe_w3, moe_w2, final_norm):
    bsz, seq, _ = x.shape
    t = bsz * seq
    depth = w_in.shape[0]
    slopes_a, slopes_b, slopes_d = _alibi_slopes()
    tq_dense = _pick(seq, 512)
    tq_band = _pick(seq, 256)

    tabs = _rope_tables(seq)
    colscale = _col_scale()
    di = jnp.arange(tq_dense)
    causal_mask_t = jnp.where(di[:, None] <= di[None, :], 0.0, NEG_INF).astype(F32)
    alibi_kb = _alibi_key_columns(slopes_a * LOG2E, tq_dense)
    tq_dil = _pick(seq, 512)
    n_off_dil = min((max(w for w, _ in DIL_PATTERNS) + tq_dil - 1) // tq_dil + 1, seq // tq_dil)
    n_off_swa = min((SWA_WINDOW - 1 + tq_band - 1) // tq_band + 1, seq // tq_band)
    dil_bias = _band_bias(slopes_b, tq_dil, n_off_dil, _dil_mult).reshape(2, 2, n_off_dil, tq_dil, tq_dil)
    swa_bias = _band_bias(slopes_d[np.array([0, 2, 1, 3])], tq_band, n_off_swa, _swa_mult
                          ).reshape(2, 2, n_off_swa, tq_band, tq_band)
    slopes_a_l2 = jnp.asarray(slopes_a * LOG2E, F32)
    dil_m0 = jnp.full((4,), NEG_INF, F32)

    x2d = x.reshape(t, D_MODEL)
    for l in range(depth):
        w_main, wq_big, wkv, w4 = _prep_attn_weights(w_in[l], w_out[l], mla_w_uq[l], mla_w_ukv[l])
        a, avt, b, bvt, d, dvt, cq, ck, cvt = _proj(x2d, attn_norm[l][None], w_main, colscale, mla_q_norm[l][None],
                                                    wq_big, mla_kv_norm[l][None], wkv, tabs, seq)
        a3, b3, d3 = (v.reshape(bsz, seq, -1) for v in (a, b, d))
        sub128 = jnp.concatenate([diff_subln[l], diff_subln[l]])[None]
        o_a = _diff_attention(a3, avt, slopes_a_l2, causal_mask_t, alibi_kb, diff_lambda[l], sub128, l, tq_dense)
        o_b = _band_attention(b3, bvt, lambda p: 2 + p, lambda p: p, dil_bias, dil_m0, 0.0, tq_dil, "dil_attn")
        o_c = _mla_attention(cq.reshape(bsz, seq, -1), ck.reshape(bsz, seq, -1), cvt, causal_mask_t, tq_dense)
        swa_m0 = (swa_sinks[l].astype(F32) * LOG2E)[np.array([0, 2, 1, 3])]
        o_d = _band_attention(d3, dvt, lambda p: 2, lambda p: 0, swa_bias, swa_m0, 1.0, tq_band, "swa_attn")
        outs = [v.reshape(t, 256) for v in (o_a, o_b, o_c, o_d)]
        j = l // 2
        if l % 2 == 0:
            x1, h2 = _outproj(*outs, w4, x2d, ffn_norm[l][None])
            x2d = _ffn(h2, ffn_w1[j].astype(BF16), ffn_w3[j].astype(BF16), ffn_w2[j].astype(BF16), x1)
            moe = None
        else:
            r = jnp.pad(moe_router[j], ((0, 0), (0, LANES - N_EXPERTS)))
            r_hi, r_lo = _split_bf16(r, 2)
            x1, h2, logits = _outproj(*outs, w4, x2d, ffn_norm[l][None], (r_hi, r_lo))
            bm = _pick(t, 1024)
            moe = _moe_layer(h2, logits, moe_w1[j].astype(BF16), moe_w3[j].astype(BF16),
                             moe_w2[j].astype(BF16), bm)
            x2d = x1
        if l < depth - 1 and moe is not None:
            raise NotImplementedError("MoE combine is fused into the final norm; MoE layer must be last")
    return _final(x2d, final_norm[None], moe).reshape(bsz, seq, D_MODEL)
```

```python
import functools
import math

import numpy as np
import jax
import jax.numpy as jnp
from jax import lax
from jax.experimental import pallas as pl
from jax.experimental.pallas import tpu as pltpu
from jax.experimental.pallas import tpu_sc as plsc

D_MODEL = 1024
DIFF_HEADS, DIFF_HD = 4, 32
DIL_HEADS, DIL_HD = 4, 64
DIL_PATTERNS = ((128, 1), (512, 4), (2048, 16))
MLA_HEADS, MLA_Q_LORA, MLA_KV_LORA, MLA_NOPE, MLA_ROPE, MLA_V = 4, 384, 128, 64, 32, 64
ROPE_THETA = 10000.0
SWA_HEADS, SWA_KV_HEADS, SWA_HD, SWA_WINDOW = 4, 2, 64, 128
D_FF = 2816
N_EXPERTS, TOP_K, D_FF_EXPERT = 8, 2, 3584
RMS_EPS = 1e-6
NEG_INF = -1e30
N_ALIBI = DIFF_HEADS + DIL_HEADS + SWA_HEADS

LOG2E = 1.4426950408889634
LANES = 128
HEAD_LANES = 64
VMEM_LIMIT = 56 * 1024 * 1024

F32 = jnp.float32
BF16 = jnp.bfloat16

_NT = (((1,), (1,)), ((), ()))


def _params(sem, vmem=VMEM_LIMIT):
    return pltpu.CompilerParams(dimension_semantics=sem, vmem_limit_bytes=vmem)


def _alibi_slopes():
    s = 2.0 ** (-8.0 * (np.arange(N_ALIBI) + 1) / N_ALIBI)
    return s[0::3], s[1::3], s[2::3]


def _pick(n, pref):
    t = min(pref, n)
    while n % t:
        t //= 2
    return t


N_SCALED = 2048
C_CQ, C_CKV, C_KPE, C_KPE_SW, N_MAIN = 2048, 2432, 2560, 2688, 2816


def _vt_with_ones(v):
    vt = v.T
    ones = jnp.ones((HEAD_LANES, v.shape[0]), F32)
    parts = []
    for pair in range(v.shape[1] // LANES):
        h0 = vt[pair * LANES:pair * LANES + HEAD_LANES]
        h1 = vt[pair * LANES + HEAD_LANES:(pair + 1) * LANES]
        parts += [h0, ones, ones, h1]
    return jnp.concatenate(parts, axis=0).astype(BF16)


def _proj_body(x_ref, g_ref, w_ref, cs_ref, qn_ref, wq_ref, kn_ref, wkv_ref,
               cosq_ref, sinq_ref, cosk_ref, sink_ref,
               a_ref, avt_ref, b_ref, bvt_ref, d_ref, dvt_ref, cq_ref, ck_ref, cvt_ref):
    x = x_ref[...]
    ms = jnp.mean(x * x, axis=-1, keepdims=True)
    h = (x * lax.rsqrt(ms + RMS_EPS) * g_ref[...]).astype(BF16)
    acc = jnp.dot(h, w_ref[...], preferred_element_type=F32)
    sc = acc[:, :N_SCALED] * cs_ref[...]
    a_ref[...] = sc[:, 0:512].astype(BF16)
    avt_ref[...] = _vt_with_ones(sc[:, 512:768])
    b_ref[...] = sc[:, 768:1280].astype(BF16)
    bvt_ref[...] = _vt_with_ones(sc[:, 1280:1536])
    d_ref[...] = sc[:, 1536:1920].astype(BF16)
    dvt_ref[...] = _vt_with_ones(sc[:, 1920:2048])

    cq = acc[:, C_CQ:C_CKV]
    hq = (cq * lax.rsqrt(jnp.mean(cq * cq, axis=-1, keepdims=True) + RMS_EPS) * qn_ref[...]).astype(BF16)
    yq = jnp.dot(hq, wq_ref[...], preferred_element_type=F32)
    cosq, sinq = cosq_ref[...], sinq_ref[...]
    c_mla = (MLA_NOPE + MLA_ROPE) ** -0.5 * LOG2E
    for hd in range(MLA_HEADS):
        main = yq[:, hd * 256:(hd + 1) * 256]
        sw = yq[:, 1024 + hd * LANES:1024 + (hd + 1) * LANES]
        cq_ref[:, hd * 256:hd * 256 + LANES] = (main[:, :LANES] * cosq + sw * sinq).astype(BF16)
        cq_ref[:, hd * 256 + LANES:(hd + 1) * 256] = (main[:, LANES:] * c_mla).astype(BF16)

    ckv = acc[:, C_CKV:C_KPE]
    hk = (ckv * lax.rsqrt(jnp.mean(ckv * ckv, axis=-1, keepdims=True) + RMS_EPS) * kn_ref[...]).astype(BF16)
    kv = jnp.dot(hk, wkv_ref[...], preferred_element_type=F32)
    kr = (acc[:, C_KPE:C_KPE_SW] * cosk_ref[...] + acc[:, C_KPE_SW:N_MAIN] * sink_ref[...]).astype(BF16)
    for grp in range(2):
        ck_ref[:, grp * 256:grp * 256 + LANES] = kr
        ck_ref[:, grp * 256 + LANES:(grp + 1) * 256] = kv[:, grp * LANES:(grp + 1) * LANES].astype(BF16)
    cvt_ref[...] = _vt_with_ones(kv[:, 256:512])


def _proj(x2d, g, w_main, colscale, qn, wq_big, kn, wkv, tabs, seq):
    t = x2d.shape[0]
    tm = _pick(seq, 512)
    nsb = seq // tm
    full = lambda shape: pl.BlockSpec(shape, lambda i: (0,) * len(shape))
    tab = pl.BlockSpec((tm, LANES), lambda i: (i % nsb, 0))
    row = lambda w: pl.BlockSpec((tm, w), lambda i: (i, 0))
    vt = lambda w: pl.BlockSpec((None, w, tm), lambda i: (i // nsb, 0, i % nsb))
    row_sds = lambda w: jax.ShapeDtypeStruct((t, w), BF16)
    vt_sds = lambda w: jax.ShapeDtypeStruct((t // seq, w, seq), BF16)
    return pl.pallas_call(
        _proj_body,
        grid=(t // tm,),
        in_specs=[row(D_MODEL), full((1, D_MODEL)), full((D_MODEL, N_MAIN)), full((1, N_SCALED)),
                  full((1, MLA_Q_LORA)), full((MLA_Q_LORA, 1536)), full((1, MLA_KV_LORA)),
                  full((MLA_KV_LORA, 512)), tab, tab, tab, tab],
        out_specs=[row(512), vt(512), row(512), vt(512), row(384), vt(256), row(1024), row(512), vt(512)],
        out_shape=[row_sds(512), vt_sds(512), row_sds(512), vt_sds(512), row_sds(384), vt_sds(256),
                   row_sds(1024), row_sds(512), vt_sds(512)],
        compiler_params=_params(("parallel",)),
        name="proj",
    )(x2d, g, w_main, colscale, qn, wq_big, kn, wkv, *tabs)


def _attn_pipeline(n_soft, n_steps, scores, shift, load_vt, m0, s_ref, acc_ref):
    def stage_a(t, ss, dst_ref, m_cur):
        m_next, alpha = [], []
        for i in range(n_soft):
            m_new = jnp.maximum(m_cur[i], jnp.max(ss[i], axis=0, keepdims=True) + shift(i, t))
            dst_ref[i] = ss[i]
            m_next.append(m_new)
            alpha.append(jnp.exp2(m_cur[i] - m_new))
        return tuple(m_next), tuple(alpha)

    def stage_b(t, src_ref, carry):
        m_cur, alpha = carry
        for i in range(n_soft):
            p = jnp.exp2(src_ref[i] - (m_cur[i] - shift(i, t)))
            acc_ref[i] = alpha[i] * acc_ref[i] + jnp.dot(load_vt(t, i), p.astype(BF16),
                                                         preferred_element_type=F32)

    def half(t, src_ref, dst_ref, carry):
        ss = [scores(i, t + 1, False) for i in range(n_soft)]
        stage_b(t, src_ref, carry)
        return stage_a(t + 1, ss, dst_ref, carry[0])

    def body(u, carry):
        carry = half(2 * u, s_ref[0], s_ref[1], carry)
        return half(2 * u + 1, s_ref[1], s_ref[0], carry)

    carry = stage_a(0, [scores(i, 0, True) for i in range(n_soft)], s_ref[0], m0)
    carry = lax.fori_loop(0, n_steps // 2, body, carry)

    @pl.when(n_steps % 2 == 1)
    def _():
        stage_b(n_steps, s_ref[1], half(n_steps - 1, s_ref[0], s_ref[1], carry))

    @pl.when(n_steps % 2 == 0)
    def _():
        stage_b(n_steps, s_ref[0], carry)


def _row_mask(shape, lo, hi):
    row = lax.broadcasted_iota(jnp.int32, shape, 0)
    return jnp.logical_and(row >= lo, row < hi)


def _lane_mask(shape, lo, hi):
    lane = lax.broadcasted_iota(jnp.int32, shape, len(shape) - 1)
    return jnp.logical_and(lane >= lo, lane < hi)


def _denominator(acc_ref, i, head):
    row = HEAD_LANES if head == 0 else 0
    return acc_ref[i, row:row + 1, :]


def _diff_body(sl_ref, q_ref, k_ref, vt_ref, mask_ref, kb_ref, lam_ref, sub_ref, o_ref,
               qs_ref, s0_ref, s1_ref, acc_ref, *, tq, lam_init):
    pair, qi = pl.program_id(1), pl.program_id(2)
    tk = tq
    q = q_ref[...].astype(F32)
    for hd in range(2):
        sel = jnp.where(_lane_mask(q.shape, N_BIAS_SPLIT * hd, N_BIAS_SPLIT * (hd + 1)), 1.0, 0.0).astype(BF16)
        for mp in range(2):
            lo = hd * HEAD_LANES + mp * DIFF_HD
            qs_ref[2 * hd + mp, :, 0:LANES] = jnp.where(_lane_mask(q.shape, lo, lo + DIFF_HD), q, 0.0).astype(BF16)
            qs_ref[2 * hd + mp, :, LANES:2 * LANES] = sel
    acc_ref[...] = jnp.zeros(acc_ref.shape, F32)

    def scores(i, t, first):
        off = pl.multiple_of((qi - t) * tk, tk)
        k = jnp.concatenate([k_ref[pl.ds(off, tk), :], kb_ref[...]], axis=1)
        s = lax.dot_general(k, qs_ref[i], _NT, preferred_element_type=F32)
        return s + mask_ref[...] if first else s

    def shift(i, t):
        return sl_ref[2 * pair + i // 2] * (-(t * tk)).astype(F32) if not isinstance(t, int) else 0.0

    def load_vt(t, i):
        return vt_ref[pl.ds((i // 2) * LANES, LANES), pl.ds(pl.multiple_of((qi - t) * tk, tk), tk)]

    m0 = tuple(jnp.full((1, tq), NEG_INF, F32) for _ in range(4))
    _attn_pipeline(4, qi, scores, shift, load_vt, m0, (s0_ref, s1_ref), acc_ref)

    lp = lam_ref[...]
    lam = (jnp.exp(jnp.sum(lp[0:1, :] * lp[1:2, :], axis=-1, keepdims=True))
           - jnp.exp(jnp.sum(lp[2:3, :] * lp[3:4, :], axis=-1, keepdims=True)) + lam_init)
    o_h = [acc_ref[2 * hd] / _denominator(acc_ref, 2 * hd, hd)
           - lam * (acc_ref[2 * hd + 1] / _denominator(acc_ref, 2 * hd + 1, hd)) for hd in range(2)]
    o = jnp.where(_row_mask((LANES, tq), 0, HEAD_LANES), o_h[0], o_h[1]).T
    in_h0 = _lane_mask((tq, LANES), 0, HEAD_LANES)
    sq = o * o
    ms0 = jnp.sum(jnp.where(in_h0, sq, 0.0), axis=-1, keepdims=True)
    ms1 = jnp.sum(jnp.where(in_h0, 0.0, sq), axis=-1, keepdims=True)
    ms = jnp.where(in_h0, ms0, ms1) * (1.0 / HEAD_LANES)
    y = o * lax.rsqrt(ms + RMS_EPS) * sub_ref[...]
    o_ref[...] = (y * (1.0 - lam_init)).astype(o_ref.dtype)


N_BIAS_SPLIT = 3


def _alibi_key_columns(slopes_l2, tk):
    dj = jnp.arange(tk, dtype=F32)[None, :, None]
    val = jnp.asarray(slopes_l2, F32).reshape(-1, 1, 2) * dj
    cols = jnp.stack(_split_bf16(val, N_BIAS_SPLIT), axis=-1).reshape(val.shape[0], tk, 2 * N_BIAS_SPLIT)
    return jnp.pad(cols, ((0, 0), (0, 0), (0, LANES - 2 * N_BIAS_SPLIT)))


def _split_bf16(x, n):
    pieces = []
    for _ in range(n):
        bits = lax.bitcast_convert_type(x, jnp.uint32) & jnp.uint32(0xFFFF0000)
        head = lax.bitcast_convert_type(bits, F32)
        pieces.append(head.astype(BF16))
        x = x - head
    return pieces


def _diff_attention(a3, avt, slopes_l2, mask_t, kb, lam_p, sub128, layer_idx, tq):
    b, s, _ = a3.shape
    nq = s // tq
    lam_init = 0.8 - 0.6 * math.exp(-0.3 * layer_idx)
    body = functools.partial(_diff_body, tq=tq, lam_init=lam_init)
    grid_spec = pltpu.PrefetchScalarGridSpec(
        num_scalar_prefetch=1,
        grid=(b, 2, nq),
        in_specs=[
            pl.BlockSpec((None, tq, LANES), lambda bi, p, qi, sl: (bi, qi, p)),
            pl.BlockSpec((None, s, LANES), lambda bi, p, qi, sl: (bi, 0, 2 + p)),
            pl.BlockSpec((None, 2 * LANES, s), lambda bi, p, qi, sl: (bi, p, 0)),
            pl.BlockSpec((tq, tq), lambda bi, p, qi, sl: (0, 0)),
            pl.BlockSpec((None, tq, LANES), lambda bi, p, qi, sl: (p, 0, 0)),
            pl.BlockSpec((4, DIFF_HD), lambda bi, p, qi, sl: (0, 0)),
            pl.BlockSpec((1, LANES), lambda bi, p, qi, sl: (0, 0)),
        ],
        out_specs=pl.BlockSpec((None, tq, LANES), lambda bi, p, qi, sl: (bi, qi, p)),
        scratch_shapes=[pltpu.VMEM((4, tq, 2 * LANES), BF16), pltpu.VMEM((4, tq, tq), F32),
                        pltpu.VMEM((4, tq, tq), F32), pltpu.VMEM((4, LANES, tq), F32)],
    )
    return pl.pallas_call(
        body, grid_spec=grid_spec,
        out_shape=jax.ShapeDtypeStruct((b, s, 256), BF16),
        compiler_params=_params(("parallel", "parallel", "arbitrary")),
        name="diff_attn",
    )(slopes_l2, a3, a3, avt, mask_t, kb, lam_p, sub128)


def _mla_body(q0_ref, q1_ref, k_ref, vt_ref, mask_ref, o_ref, s0_ref, s1_ref, acc_ref, *, tq):
    qi = pl.program_id(2)
    tk = tq
    acc_ref[...] = jnp.zeros(acc_ref.shape, F32)
    q_refs = (q0_ref, q1_ref)

    def scores(i, t, first):
        k = k_ref[pl.ds(pl.multiple_of((qi - t) * tk, tk), tk), :]
        s = lax.dot_general(k, q_refs[i][...], _NT, preferred_element_type=F32)
        return s + mask_ref[...] if first else s

    def load_vt(t, i):
        return vt_ref[pl.ds(i * LANES, LANES), pl.ds(pl.multiple_of((qi - t) * tk, tk), tk)]

    m0 = tuple(jnp.full((1, tq), NEG_INF, F32) for _ in range(2))
    _attn_pipeline(2, qi, scores, lambda i, t: 0.0, load_vt, m0, (s0_ref, s1_ref), acc_ref)
    o_t = jnp.where(_row_mask((LANES, tq), 0, HEAD_LANES), acc_ref[0] / _denominator(acc_ref, 0, 0),
                    acc_ref[1] / _denominator(acc_ref, 1, 1))
    o_ref[...] = o_t.T.astype(o_ref.dtype)


def _mla_attention(cq3, ck3, cvt, mask_t, tq):
    b, s, _ = cq3.shape
    nq = s // tq
    return pl.pallas_call(
        functools.partial(_mla_body, tq=tq),
        grid=(b, 2, nq),
        in_specs=[
            pl.BlockSpec((None, tq, 256), lambda bi, g, qi: (bi, qi, 2 * g)),
            pl.BlockSpec((None, tq, 256), lambda bi, g, qi: (bi, qi, 2 * g + 1)),
            pl.BlockSpec((None, s, 256), lambda bi, g, qi: (bi, 0, g)),
            pl.BlockSpec((None, 2 * LANES, s), lambda bi, g, qi: (bi, g, 0)),
            pl.BlockSpec((tq, tq), lambda bi, g, qi: (0, 0)),
        ],
        out_specs=pl.BlockSpec((None, tq, LANES), lambda bi, g, qi: (bi, qi, g)),
        out_shape=jax.ShapeDtypeStruct((b, s, 256), BF16),
        scratch_shapes=[pltpu.VMEM((2, tq, tq), F32), pltpu.VMEM((2, tq, tq), F32),
                        pltpu.VMEM((2, LANES, tq), F32)],
        compiler_params=_params(("parallel", "parallel", "arbitrary")),
        name="mla_attn",
    )(cq3, cq3, ck3, cvt, mask_t)


def _band_body(m0_ref, q_ref, k_ref, vt_ref, bias_ref, o_ref, qs_ref, s0_ref, s1_ref, acc_ref,
               *, tq, n_off, l_init):
    slab, qi = pl.program_id(1), pl.program_id(2)
    tk = tq
    q = q_ref[...].astype(F32)
    in_h0 = _row_mask((LANES, tq), 0, HEAD_LANES)
    for i in range(2):
        qs_ref[i] = jnp.where(_lane_mask(q.shape, i * HEAD_LANES, (i + 1) * HEAD_LANES), q, 0.0).astype(BF16)
        acc_ref[i] = jnp.where(in_h0 if i == 1 else jnp.logical_not(in_h0), l_init, 0.0).astype(F32)

    def scores(i, t, first):
        k = k_ref[pl.ds(pl.multiple_of((qi - t) * tk, tk), tk), :]
        return lax.dot_general(k, qs_ref[i], _NT, preferred_element_type=F32) + bias_ref[i, t]

    def load_vt(t, i):
        return vt_ref[pl.ds(i * LANES, LANES), pl.ds(pl.multiple_of((qi - t) * tk, tk), tk)]

    m0 = tuple(jnp.full((1, tq), m0_ref[2 * slab + i], F32) for i in range(2))
    _attn_pipeline(2, jnp.minimum(qi, n_off - 1), scores, lambda i, t: 0.0, load_vt, m0, (s0_ref, s1_ref), acc_ref)
    o_t = jnp.where(in_h0, acc_ref[0] / _denominator(acc_ref, 0, 0), acc_ref[1] / _denominator(acc_ref, 1, 1))
    o_ref[...] = o_t.T.astype(o_ref.dtype)


def _band_attention(x3, xvt, k_blk_of, vt_blk_of, bias, m0, l_init, tq, name):
    b, s, _ = x3.shape
    nq = s // tq
    n_off = bias.shape[2]
    grid_spec = pltpu.PrefetchScalarGridSpec(
        num_scalar_prefetch=1,
        grid=(b, 2, nq),
        in_specs=[
            pl.BlockSpec((None, tq, LANES), lambda bi, p, qi, m0r: (bi, qi, p)),
            pl.BlockSpec((None, s, LANES), lambda bi, p, qi, m0r: (bi, 0, k_blk_of(p))),
            pl.BlockSpec((None, 2 * LANES, s), lambda bi, p, qi, m0r: (bi, vt_blk_of(p), 0)),
            pl.BlockSpec((None, 2, n_off, tq, tq), lambda bi, p, qi, m0r: (p, 0, 0, 0, 0)),
        ],
        out_specs=pl.BlockSpec((None, tq, LANES), lambda bi, p, qi, m0r: (bi, qi, p)),
        scratch_shapes=[pltpu.VMEM((2, tq, LANES), BF16), pltpu.VMEM((2, tq, tq), F32),
                        pltpu.VMEM((2, tq, tq), F32), pltpu.VMEM((2, LANES, tq), F32)],
    )
    return pl.pallas_call(
        functools.partial(_band_body, tq=tq, n_off=n_off, l_init=l_init),
        grid_spec=grid_spec,
        out_shape=jax.ShapeDtypeStruct((b, s, 256), BF16),
        compiler_params=_params(("parallel", "parallel", "arbitrary")),
        name=name,
    )(m0, x3, x3, xvt, bias)


def _band_bias(slopes, tq, n_off, mult_fn):
    di = jnp.arange(tq, dtype=jnp.int32)[None, :]
    dj = jnp.arange(tq, dtype=jnp.int32)[:, None]
    off = jnp.arange(n_off, dtype=jnp.int32)[:, None, None]
    delta = off * tq + di - dj
    mult = mult_fn(delta)
    sl = jnp.asarray(slopes, F32)[:, None, None, None] * LOG2E
    val = -sl * delta.astype(F32)[None] + jnp.log2(jnp.maximum(mult, 1).astype(F32))[None]
    return jnp.where((mult > 0)[None], val, NEG_INF)


def _dil_mult(delta):
    m = jnp.zeros(delta.shape, jnp.int32)
    for window, d in DIL_PATTERNS:
        m = m + ((delta >= 0) & (delta <= window) & (delta % d == 0)).astype(jnp.int32)
    return m


def _swa_mult(delta):
    return ((delta >= 0) & (delta <= SWA_WINDOW - 1)).astype(jnp.int32)


def _outproj_body(*refs, with_router):
    if with_router:
        (oa, ob, oc, od, w_ref, x_ref, g_ref, rh_ref, rl_ref, x1_ref, h2_ref, lg_ref) = refs
    else:
        (oa, ob, oc, od, w_ref, x_ref, g_ref, x1_ref, h2_ref) = refs
    acc = x_ref[...]
    for i, o in enumerate((oa, ob, oc, od)):
        acc = acc + jnp.dot(o[...], w_ref[i], preferred_element_type=F32)
    x1_ref[...] = acc
    h2 = acc * lax.rsqrt(jnp.mean(acc * acc, axis=-1, keepdims=True) + RMS_EPS) * g_ref[...]
    hi = h2.astype(BF16)
    if with_router:
        for c in range(SC_CHUNKS):
            h2_ref[c] = h2[:, c * SC_COLS:(c + 1) * SC_COLS]
        lo = (h2 - hi.astype(F32)).astype(BF16)
        lg_ref[...] = (jnp.dot(hi, rh_ref[...], preferred_element_type=F32)
                       + (jnp.dot(hi, rl_ref[...], preferred_element_type=F32)
                          + jnp.dot(lo, rh_ref[...], preferred_element_type=F32)))
    else:
        h2_ref[...] = hi


def _outproj(oa, ob, oc, od, w4, x2d, g, router_hl=None):
    t = x2d.shape[0]
    tm = _pick(t, 512)
    with_router = router_hl is not None
    full = lambda shape: pl.BlockSpec(shape, lambda i: (0,) * len(shape))
    row = lambda w: pl.BlockSpec((tm, w), lambda i: (i, 0))
    in_specs = [row(256)] * 4 + [full((4, 256, D_MODEL)), row(D_MODEL), full((1, D_MODEL))]
    args = [oa, ob, oc, od, w4, x2d, g]
    out_specs = [row(D_MODEL), row(D_MODEL)]
    out_shape = [jax.ShapeDtypeStruct((t, D_MODEL), F32), jax.ShapeDtypeStruct((t, D_MODEL), BF16)]
    if with_router:
        out_specs[1] = pl.BlockSpec((SC_CHUNKS, tm, SC_COLS), lambda i: (0, i, 0))
        out_shape[1] = jax.ShapeDtypeStruct((SC_CHUNKS, t, SC_COLS), F32)
        in_specs += [full((D_MODEL, LANES))] * 2
        args += list(router_hl)
        out_specs.append(row(LANES))
        out_shape.append(jax.ShapeDtypeStruct((t, LANES), F32))
    return pl.pallas_call(
        functools.partial(_outproj_body, with_router=with_router),
        grid=(t // tm,), in_specs=in_specs, out_specs=out_specs, out_shape=out_shape,
        compiler_params=_params(("parallel",)),
        name="outproj",
    )(*args)


FF_CHUNK = 512


def _swiglu(x, w1_ref, w3_ref, w2_ref):
    tf = w1_ref.shape[1]
    total = None
    for lo in range(0, tf, FF_CHUNK):
        hi = min(lo + FF_CHUNK, tf)
        a = jnp.dot(x, w1_ref[:, lo:hi], preferred_element_type=F32)
        b = jnp.dot(x, w3_ref[:, lo:hi], preferred_element_type=F32)
        mid = (a * jax.nn.sigmoid(a) * b).astype(BF16)
        part = jnp.dot(mid, w2_ref[lo:hi, :], preferred_element_type=F32)
        total = part if total is None else total + part
    return total


def _ffn_body(h_ref, w1_ref, w3_ref, w2_ref, x_ref, o_ref):
    o_ref[...] = x_ref[...] + _swiglu(h_ref[...], w1_ref, w3_ref, w2_ref)


def _ffn(h2, w1, w3, w2, x1):
    t = h2.shape[0]
    tm = _pick(t, 512)
    resident = lambda shape: pl.BlockSpec(shape, lambda i: (0, 0), pipeline_mode=pl.Buffered(1))
    return pl.pallas_call(
        _ffn_body,
        grid=(t // tm,),
        in_specs=[pl.BlockSpec((tm, D_MODEL), lambda i: (i, 0)),
                  resident((D_MODEL, D_FF)), resident((D_MODEL, D_FF)), resident((D_FF, D_MODEL)),
                  pl.BlockSpec((tm, D_MODEL), lambda i: (i, 0))],
        out_specs=pl.BlockSpec((tm, D_MODEL), lambda i: (i, 0)),
        out_shape=jax.ShapeDtypeStruct((t, D_MODEL), F32),
        compiler_params=_params(("parallel",)),
        name="ffn",
    )(h2, w1, w3, w2, x1)


def _route_body(lg_ref, tri_ref, info_ref, cnt_ref, carry_ref):
    i = pl.program_id(0)

    @pl.when(i == 0)
    def _():
        carry_ref[...] = jnp.zeros(carry_ref.shape, F32)

    lg = lg_ref[...]
    lane = lax.broadcasted_iota(jnp.int32, lg.shape, 1)
    lg = jnp.where(lane < N_EXPERTS, lg, -jnp.inf)
    m1 = jnp.max(lg, axis=-1, keepdims=True)
    i1 = jnp.min(jnp.where(lg == m1, lane, LANES), axis=-1, keepdims=True)
    oh1 = lane == i1
    lg2 = jnp.where(oh1, -jnp.inf, lg)
    m2 = jnp.max(lg2, axis=-1, keepdims=True)
    i2 = jnp.min(jnp.where(lg2 == m2, lane, LANES), axis=-1, keepdims=True)
    oh2 = lane == i2
    e2 = jnp.exp(m2 - m1)
    g1 = 1.0 / (1.0 + e2)
    g2 = e2 / (1.0 + e2)
    oh = jnp.where(jnp.logical_or(oh1, oh2), 1.0, 0.0)
    rank = jnp.dot(tri_ref[...], oh.astype(BF16), preferred_element_type=F32) + carry_ref[...]
    r1 = jnp.sum(jnp.where(oh1, rank, 0.0), axis=-1, keepdims=True)
    r2 = jnp.sum(jnp.where(oh2, rank, 0.0), axis=-1, keepdims=True)
    carry_ref[...] += jnp.sum(oh, axis=0, keepdims=True)
    cnt_ref[...] = carry_ref[...]
    info = jnp.zeros(lg.shape, F32)
    for c, val in enumerate((i1.astype(F32), i2.astype(F32), r1, r2, g1, g2)):
        info = jnp.where(lane == c, val, info)
    info_ref[...] = info


def _route(logits):
    t = logits.shape[0]
    tm = _pick(t, 512)
    tri = (jnp.arange(tm)[:, None] > jnp.arange(tm)[None, :]).astype(BF16)
    return pl.pallas_call(
        _route_body,
        grid=(t // tm,),
        in_specs=[pl.BlockSpec((tm, LANES), lambda i: (i, 0)), pl.BlockSpec((tm, tm), lambda i: (0, 0))],
        out_specs=[pl.BlockSpec((tm, LANES), lambda i: (i, 0)), pl.BlockSpec((1, LANES), lambda i: (0, 0))],
        out_shape=[jax.ShapeDtypeStruct((t, LANES), F32), jax.ShapeDtypeStruct((1, LANES), F32)],
        scratch_shapes=[pltpu.VMEM((1, LANES), F32)],
        compiler_params=_params(("arbitrary",)),
        name="route",
    )(logits, tri)


def _moe_ffn_body(be_ref, bv_ref, x_ref, w1_ref, w3_ref, w2_ref, o_ref, acc_ref):
    i, f = pl.program_id(0), pl.program_id(1)
    last = pl.num_programs(1) - 1
    valid = bv_ref[i]

    @pl.when(valid > 0)
    def _():
        x = jnp.concatenate([x_ref[c] for c in range(SC_CHUNKS)], axis=1)
        x = jnp.where(lax.broadcasted_iota(jnp.int32, x.shape, 0) < valid, x, 0.0).astype(BF16)
        contrib = _swiglu(x, w1_ref, w3_ref, w2_ref)

        @pl.when(f == 0)
        def _():
            acc_ref[...] = contrib

        @pl.when(f > 0)
        def _():
            acc_ref[...] += contrib

        @pl.when(f == last)
        def _():
            for c in range(SC_CHUNKS):
                o_ref[c] = acc_ref[:, c * SC_COLS:(c + 1) * SC_COLS]

    @pl.when(jnp.logical_and(valid == 0, f == last))
    def _():
        o_ref[...] = jnp.zeros(o_ref.shape, o_ref.dtype)


def _moe_ffn(xs, blk_expert, blk_valid, w1, w3, w2, bm):
    rows = xs.shape[1]
    tf = D_FF_EXPERT // 2
    chunked = pl.BlockSpec((SC_CHUNKS, bm, SC_COLS), lambda i, f, be, bv: (0, i, 0))
    grid_spec = pltpu.PrefetchScalarGridSpec(
        num_scalar_prefetch=2,
        grid=(rows // bm, D_FF_EXPERT // tf),
        in_specs=[chunked,
                  pl.BlockSpec((None, D_MODEL, tf), lambda i, f, be, bv: (be[i], 0, f)),
                  pl.BlockSpec((None, D_MODEL, tf), lambda i, f, be, bv: (be[i], 0, f)),
                  pl.BlockSpec((None, tf, D_MODEL), lambda i, f, be, bv: (be[i], f, 0))],
        out_specs=chunked,
        scratch_shapes=[pltpu.VMEM((bm, D_MODEL), F32)],
    )
    return pl.pallas_call(
        _moe_ffn_body, grid_spec=grid_spec,
        out_shape=jax.ShapeDtypeStruct(xs.shape, F32),
        compiler_params=_params(("parallel", "arbitrary")),
        name="moe_ffn",
    )(blk_expert, blk_valid, xs, w1, w3, w2)


def _final_body(*refs, with_moe):
    if with_moe:
        x_ref, y0_ref, y1_ref, info_ref, g_ref, o_ref = refs
        info = info_ref[...]
        y0 = jnp.concatenate([y0_ref[c] for c in range(SC_CHUNKS)], axis=1)
        y1 = jnp.concatenate([y1_ref[c] for c in range(SC_CHUNKS)], axis=1)
        x = x_ref[...] + (info[:, 4:5] * y0 + info[:, 5:6] * y1)
    else:
        x_ref, g_ref, o_ref = refs
        x = x_ref[...]
    o_ref[...] = x * lax.rsqrt(jnp.mean(x * x, axis=-1, keepdims=True) + RMS_EPS) * g_ref[...]


def _final(x2d, g, moe=None):
    t = x2d.shape[0]
    tm = _pick(t, 512)
    row = lambda w: pl.BlockSpec((tm, w), lambda i: (i, 0))
    in_specs, args = [row(D_MODEL)], [x2d]
    if moe is not None:
        yg, info = moe
        slot = lambda k: pl.BlockSpec((None, SC_CHUNKS, tm, SC_COLS), lambda i: (k, 0, i, 0))
        in_specs += [slot(0), slot(1), row(LANES)]
        args += [yg, yg, info]
    in_specs.append(pl.BlockSpec((1, D_MODEL), lambda i: (0, 0)))
    args.append(g)
    return pl.pallas_call(
        functools.partial(_final_body, with_moe=moe is not None),
        grid=(t // tm,), in_specs=in_specs, out_specs=row(D_MODEL),
        out_shape=jax.ShapeDtypeStruct((t, D_MODEL), F32),
        compiler_params=_params(("parallel",)),
        name="final",
    )(*args)


def _prep_attn_weights(w_in, w_out, w_uq, w_ukv):
    widths = (256, 256, 256, 256, 256, 256, MLA_Q_LORA, MLA_KV_LORA, MLA_ROPE, 256, 128, 128)
    offs = np.concatenate([[0], np.cumsum(widths)])
    qa, ka, va, qb, kb, vb, cq, ckv, kpe, qd, kd, vd = [w_in[:, offs[i]:offs[i + 1]] for i in range(12)]
    hperm = np.array([0, 2, 1, 3])
    qd = qd.reshape(D_MODEL, 4, SWA_HD)[:, hperm].reshape(D_MODEL, 256)
    pad = jnp.zeros((D_MODEL, LANES - MLA_ROPE), w_in.dtype)
    half = MLA_ROPE // 2
    kpe_sw = jnp.concatenate([kpe[:, half:], kpe[:, :half]], axis=1)
    w_main = jnp.concatenate([qa, ka, va, qb, kb, vb, qd, kd, vd, cq, ckv, kpe, pad, kpe_sw, pad], axis=1)

    uq = w_uq.reshape(MLA_Q_LORA, MLA_HEADS, MLA_NOPE + MLA_ROPE)
    z = lambda n: jnp.zeros((MLA_Q_LORA, n), w_uq.dtype)
    mains, sws = [], []
    for hd in range(MLA_HEADS):
        nope, rope = uq[:, hd, :MLA_NOPE], uq[:, hd, MLA_NOPE:]
        nope128 = jnp.concatenate([nope, z(64)] if hd % 2 == 0 else [z(64), nope], axis=1)
        mains.append(jnp.concatenate([rope, z(LANES - MLA_ROPE), nope128], axis=1))
        sws.append(jnp.concatenate([rope[:, half:], rope[:, :half], z(LANES - MLA_ROPE)], axis=1))
    wq_big = jnp.concatenate(mains + sws, axis=1)

    ukv = w_ukv.reshape(MLA_KV_LORA, MLA_HEADS, MLA_NOPE + MLA_V)
    wkv = jnp.concatenate([ukv[:, :, :MLA_NOPE].reshape(MLA_KV_LORA, 256),
                           ukv[:, :, MLA_NOPE:].reshape(MLA_KV_LORA, 256)], axis=1)

    wo = w_out.reshape(4, 256, D_MODEL)
    wo_d = wo[3].reshape(4, SWA_HD, D_MODEL)[hperm].reshape(256, D_MODEL)
    w4 = jnp.stack([wo[0], wo[1], wo[2], wo_d])
    return w_main.astype(BF16), wq_big.astype(BF16), wkv.astype(BF16), w4.astype(BF16)


def _rope_tables(s):
    inv = ROPE_THETA ** (-jnp.arange(0, MLA_ROPE, 2, dtype=F32) / MLA_ROPE)
    ang = jnp.arange(s, dtype=F32)[:, None] * inv[None, :]
    cos, sin = jnp.cos(ang), jnp.sin(ang)
    z = jnp.zeros((s, LANES - MLA_ROPE), F32)
    cos128 = jnp.concatenate([cos, cos, z], axis=1)
    sin128 = jnp.concatenate([-sin, sin, z], axis=1)
    c_mla = (MLA_NOPE + MLA_ROPE) ** -0.5 * LOG2E
    return cos128 * c_mla, sin128 * c_mla, cos128, sin128


def _col_scale():
    ca = DIFF_HD ** -0.5 * LOG2E
    cb = DIL_HD ** -0.5 * LOG2E
    cd = SWA_HD ** -0.5 * LOG2E
    v = np.ones((1, N_SCALED), np.float32)
    v[0, 0:256] = ca
    v[0, 768:1024] = cb
    v[0, 1536:1792] = cd
    return jnp.asarray(v)


def _moe_layer(h2, logits, w1, w3, w2, bm):
    t = logits.shape[0]
    info, counts = _route(logits)
    cnt = counts[0, :N_EXPERTS].astype(jnp.int32)
    padded = (cnt + bm - 1) // bm * bm
    pends = jnp.cumsum(padded)
    pstarts = pends - padded
    e12 = info[:, 0:2].astype(jnp.int32)
    pos = pstarts[e12] + info[:, 2:4].astype(jnp.int32)
    nb = (t * TOP_K) // bm + N_EXPERTS
    rows = nb * bm
    blk_start = jnp.arange(nb, dtype=jnp.int32) * bm
    blk_expert = jnp.minimum(jnp.sum((pends[None, :] <= blk_start[:, None]).astype(jnp.int32), axis=1),
                             N_EXPERTS - 1)
    blk_valid = jnp.clip(pstarts[blk_expert] + cnt[blk_expert] - blk_start, 0, bm).astype(jnp.int32)
    sub = (jnp.arange(SC_CHUNKS, dtype=jnp.int32) * rows)[None, :, None] + pos.T[:, None, :]
    xs = _sc_scatter_rows(h2.reshape(SC_CHUNKS * t, SC_COLS),
                          [sub[k].reshape(1, SC_CHUNKS * t) for k in range(TOP_K)], SC_CHUNKS * rows)
    ys = _moe_ffn(xs.reshape(SC_CHUNKS, rows, SC_COLS), blk_expert.astype(jnp.int32), blk_valid, w1, w3, w2, bm)
    yg = _sc_gather_rows(ys.reshape(SC_CHUNKS * rows, SC_COLS), sub.reshape(1, TOP_K * SC_CHUNKS * t))
    return yg.reshape(TOP_K, SC_CHUNKS, t, SC_COLS), info


SC_WINDOW = 128
SC_COLS = 256
SC_CHUNKS = D_MODEL // SC_COLS


def _sc_mesh():
    return plsc.VectorSubcoreMesh(core_axis_name="c", subcore_axis_name="s")


def _sc_scatter_rows(x, idx_list, rows):
    t, d = x.shape
    n = len(idx_list)

    @functools.partial(pl.kernel, out_type=jax.ShapeDtypeStruct((rows, d), x.dtype), mesh=_sc_mesh(),
                       scratch_types=[])
    def scatter_kernel(x_hbm, *rest):
        idx_hbms, o_hbm = rest[:n], rest[n]

        def body(x_vmem, *idx_vmems):
            for iv in idx_vmems:
                pltpu.sync_copy(x_vmem, o_hbm.at[iv.at[0]])

        pltpu.emit_pipeline(
            body, grid=(t // SC_WINDOW,),
            in_specs=[pl.BlockSpec((SC_WINDOW, d), lambda i: (i, 0))]
            + [pl.BlockSpec((1, SC_WINDOW), lambda i: (0, i))] * n,
            out_specs=[], core_axis_name=("c", "s"), dimension_semantics=(pltpu.PARALLEL,),
        )(x_hbm, *idx_hbms)

    return scatter_kernel(x, *idx_list)


def _sc_gather_rows(y, idx):
    m = idx.shape[1]
    d = y.shape[1]

    @functools.partial(pl.kernel, out_type=jax.ShapeDtypeStruct((m, d), y.dtype), mesh=_sc_mesh(),
                       scratch_types=[])
    def gather_kernel(y_hbm, idx_hbm, o_hbm):
        def body(idx_vmem, o_vmem):
            pltpu.sync_copy(y_hbm.at[idx_vmem.at[0]], o_vmem)

        pltpu.emit_pipeline(
            body, grid=(m // SC_WINDOW,),
            in_specs=[pl.BlockSpec((1, SC_WINDOW), lambda i: (0, i))],
            out_specs=[pl.BlockSpec((SC_WINDOW, d), lambda i: (i, 0))],
            core_axis_name=("c", "s"), dimension_semantics=(pltpu.PARALLEL,),
        )(idx_hbm, o_hbm)

    return gather_kernel(y, idx)


def kernel(x, attn_norm, w_in, w_out, diff_lambda, diff_subln, mla_q_norm, mla_w_uq, mla_kv_norm, mla_w_ukv,
           swa_sinks, ffn_norm, ffn_w1, ffn_w3, ffn_w2, moe_router, moe_w1, moe_w3, moe_w2, final_norm):
    bsz, seq, _ = x.shape
    t = bsz * seq
    depth = w_in.shape[0]
    slopes_a, slopes_b, slopes_d = _alibi_slopes()
    tq_dense = _pick(seq, 512)
    tq_band = _pick(seq, 256)

    tabs = _rope_tables(seq)
    colscale = _col_scale()
    di = jnp.arange(tq_dense)
    causal_mask_t = jnp.where(di[:, None] <= di[None, :], 0.0, NEG_INF).astype(F32)
    alibi_kb = _alibi_key_columns(slopes_a * LOG2E, tq_dense)
    tq_dil = _pick(seq, 512)
    n_off_dil = min((max(w for w, _ in DIL_PATTERNS) + tq_dil - 1) // tq_dil + 1, seq // tq_dil)
    n_off_swa = min((SWA_WINDOW - 1 + tq_band - 1) // tq_band + 1, seq // tq_band)
    dil_bias = _band_bias(slopes_b, tq_dil, n_off_dil, _dil_mult).reshape(2, 2, n_off_dil, tq_dil, tq_dil)
    swa_bias = _band_bias(slopes_d[np.array([0, 2, 1, 3])], tq_band, n_off_swa, _swa_mult
                          ).reshape(2, 2, n_off_swa, tq_band, tq_band)
    slopes_a_l2 = jnp.asarray(slopes_a * LOG2E, F32)
    dil_m0 = jnp.full((4,), NEG_INF, F32)

    x2d = x.reshape(t, D_MODEL)
    for l in range(depth):
        w_main, wq_big, wkv, w4 = _prep_attn_weights(w_in[l], w_out[l], mla_w_uq[l], mla_w_ukv[l])
        a, avt, b, bvt, d, dvt, cq, ck, cvt = _proj(x2d, attn_norm[l][None], w_main, colscale, mla_q_norm[l][None],
                                                    wq_big, mla_kv_norm[l][None], wkv, tabs, seq)
        a3, b3, d3 = (v.reshape(bsz, seq, -1) for v in (a, b, d))
        sub128 = jnp.concatenate([diff_subln[l], diff_subln[l]])[None]
        o_a = _diff_attention(a3, avt, slopes_a_l2, causal_mask_t, alibi_kb, diff_lambda[l], sub128, l, tq_dense)
        o_b = _band_attention(b3, bvt, lambda p: 2 + p, lambda p: p, dil_bias, dil_m0, 0.0, tq_dil, "dil_attn")
        o_c = _mla_attention(cq.reshape(bsz, seq, -1), ck.reshape(bsz, seq, -1), cvt, causal_mask_t, tq_dense)
        swa_m0 = (swa_sinks[l].astype(F32) * LOG2E)[np.array([0, 2, 1, 3])]
        o_d = _band_attention(d3, dvt, lambda p: 2, lambda p: 0, swa_bias, swa_m0, 1.0, tq_band, "swa_attn")
        outs = [v.reshape(t, 256) for v in (o_a, o_b, o_c, o_d)]
        j = l // 2
        if l % 2 == 0:
            x1, h2 = _outproj(*outs, w4, x2d, ffn_norm[l][None])
            x2d = _ffn(h2, ffn_w1[j].astype(BF16), ffn_w3[j].astype(BF16), ffn_w2[j].astype(BF16), x1)
            moe = None
        else:
            r = jnp.pad(moe_router[j], ((0, 0), (0, LANES - N_EXPERTS)))
            r_hi, r_lo = _split_bf16(r, 2)
            x1, h2, logits = _outproj(*outs, w4, x2d, ffn_norm[l][None], (r_hi, r_lo))
            bm = _pick(t, 512)
            moe = _moe_layer(h2, logits, moe_w1[j].astype(BF16), moe_w3[j].astype(BF16),
                             moe_w2[j].astype(BF16), bm)
            x2d = x1
        if l < depth - 1 and moe is not None:
            raise NotImplementedError("MoE combine is fused into the final norm; MoE layer must be last")
    return _final(x2d, final_norm[None], moe).reshape(bsz, seq, D_MODEL)
```

```python
import functools
import math

import numpy as np
import jax
import jax.numpy as jnp
from jax import lax
from jax.experimental import pallas as pl
from jax.experimental.pallas import tpu as pltpu
from jax.experimental.pallas import tpu_sc as plsc

D_MODEL = 1024
DIFF_HEADS, DIFF_HD = 4, 32
DIL_HEADS, DIL_HD = 4, 64
DIL_PATTERNS = ((128, 1), (512, 4), (2048, 16))
MLA_HEADS, MLA_Q_LORA, MLA_KV_LORA, MLA_NOPE, MLA_ROPE, MLA_V = 4, 384, 128, 64, 32, 64
ROPE_THETA = 10000.0
SWA_HEADS, SWA_KV_HEADS, SWA_HD, SWA_WINDOW = 4, 2, 64, 128
D_FF = 2816
N_EXPERTS, TOP_K, D_FF_EXPERT = 8, 2, 3584
RMS_EPS = 1e-6
NEG_INF = -1e30
N_ALIBI = DIFF_HEADS + DIL_HEADS + SWA_HEADS

LOG2E = 1.4426950408889634
LANES = 128
HEAD_LANES = 64
VMEM_LIMIT = 56 * 1024 * 1024

F32 = jnp.float32
BF16 = jnp.bfloat16

_NT = (((1,), (1,)), ((), ()))


def _params(sem, vmem=VMEM_LIMIT):
    return pltpu.CompilerParams(dimension_semantics=sem, vmem_limit_bytes=vmem)


def _alibi_slopes():
    s = 2.0 ** (-8.0 * (np.arange(N_ALIBI) + 1) / N_ALIBI)
    return s[0::3], s[1::3], s[2::3]


def _pick(n, pref):
    t = min(pref, n)
    while n % t:
        t //= 2
    return t


N_SCALED = 2048
C_CQ, C_CKV, C_KPE, C_KPE_SW, N_MAIN = 2048, 2432, 2560, 2688, 2816


ONES_ROWS = 16
VT_ROWS = HEAD_LANES + ONES_ROWS


def _vt_with_ones(v):
    vt = v.T
    ones = jnp.ones((ONES_ROWS, v.shape[0]), F32)
    parts = []
    for hd in range(v.shape[1] // HEAD_LANES):
        parts += [vt[hd * HEAD_LANES:(hd + 1) * HEAD_LANES], ones]
    return jnp.concatenate(parts, axis=0).astype(BF16)


def _proj_body(x_ref, g_ref, w_ref, cs_ref, qn_ref, wq_ref, kn_ref, wkv_ref,
               cosq_ref, sinq_ref, cosk_ref, sink_ref,
               a_ref, avt_ref, b_ref, bvt_ref, d_ref, dvt_ref, cq_ref, ck_ref, cvt_ref):
    x = x_ref[...]
    ms = jnp.mean(x * x, axis=-1, keepdims=True)
    h = (x * lax.rsqrt(ms + RMS_EPS) * g_ref[...]).astype(BF16)
    acc = jnp.dot(h, w_ref[...], preferred_element_type=F32)
    sc = acc[:, :N_SCALED] * cs_ref[...]
    a_ref[...] = sc[:, 0:512].astype(BF16)
    avt_ref[...] = _vt_with_ones(sc[:, 512:768])
    b_ref[...] = sc[:, 768:1280].astype(BF16)
    bvt_ref[...] = _vt_with_ones(sc[:, 1280:1536])
    d_ref[...] = sc[:, 1536:1920].astype(BF16)
    dvt_ref[...] = _vt_with_ones(sc[:, 1920:2048])

    cq = acc[:, C_CQ:C_CKV]
    hq = (cq * lax.rsqrt(jnp.mean(cq * cq, axis=-1, keepdims=True) + RMS_EPS) * qn_ref[...]).astype(BF16)
    yq = jnp.dot(hq, wq_ref[...], preferred_element_type=F32)
    cosq, sinq = cosq_ref[...], sinq_ref[...]
    c_mla = (MLA_NOPE + MLA_ROPE) ** -0.5 * LOG2E
    for hd in range(MLA_HEADS):
        main = yq[:, hd * 256:(hd + 1) * 256]
        sw = yq[:, 1024 + hd * LANES:1024 + (hd + 1) * LANES]
        cq_ref[:, hd * 256:hd * 256 + LANES] = (main[:, :LANES] * cosq + sw * sinq).astype(BF16)
        cq_ref[:, hd * 256 + LANES:(hd + 1) * 256] = (main[:, LANES:] * c_mla).astype(BF16)

    ckv = acc[:, C_CKV:C_KPE]
    hk = (ckv * lax.rsqrt(jnp.mean(ckv * ckv, axis=-1, keepdims=True) + RMS_EPS) * kn_ref[...]).astype(BF16)
    kv = jnp.dot(hk, wkv_ref[...], preferred_element_type=F32)
    kr = (acc[:, C_KPE:C_KPE_SW] * cosk_ref[...] + acc[:, C_KPE_SW:N_MAIN] * sink_ref[...]).astype(BF16)
    for grp in range(2):
        ck_ref[:, grp * 256:grp * 256 + LANES] = kr
        ck_ref[:, grp * 256 + LANES:(grp + 1) * 256] = kv[:, grp * LANES:(grp + 1) * LANES].astype(BF16)
    cvt_ref[...] = _vt_with_ones(kv[:, 256:512])


def _proj(x2d, g, w_main, colscale, qn, wq_big, kn, wkv, tabs, seq):
    t = x2d.shape[0]
    tm = _pick(seq, 512)
    nsb = seq // tm
    full = lambda shape: pl.BlockSpec(shape, lambda i: (0,) * len(shape))
    tab = pl.BlockSpec((tm, LANES), lambda i: (i % nsb, 0))
    row = lambda w: pl.BlockSpec((tm, w), lambda i: (i, 0))
    vt = lambda w: pl.BlockSpec((None, w, tm), lambda i: (i // nsb, 0, i % nsb))
    row_sds = lambda w: jax.ShapeDtypeStruct((t, w), BF16)
    vt_sds = lambda w: jax.ShapeDtypeStruct((t // seq, w, seq), BF16)
    return pl.pallas_call(
        _proj_body,
        grid=(t // tm,),
        in_specs=[row(D_MODEL), full((1, D_MODEL)), full((D_MODEL, N_MAIN)), full((1, N_SCALED)),
                  full((1, MLA_Q_LORA)), full((MLA_Q_LORA, 1536)), full((1, MLA_KV_LORA)),
                  full((MLA_KV_LORA, 512)), tab, tab, tab, tab],
        out_specs=[row(512), vt(4 * VT_ROWS), row(512), vt(4 * VT_ROWS), row(384), vt(2 * VT_ROWS),
                   row(1024), row(512), vt(4 * VT_ROWS)],
        out_shape=[row_sds(512), vt_sds(4 * VT_ROWS), row_sds(512), vt_sds(4 * VT_ROWS), row_sds(384),
                   vt_sds(2 * VT_ROWS), row_sds(1024), row_sds(512), vt_sds(4 * VT_ROWS)],
        compiler_params=_params(("parallel",)),
        name="proj",
    )(x2d, g, w_main, colscale, qn, wq_big, kn, wkv, *tabs)


def _attn_pipeline(n_soft, n_steps, scores, shift, load_vt, m0, s_ref, acc_ref):
    def stage_a(t, ss, dst_ref, m_cur):
        m_next, alpha = [], []
        for i in range(n_soft):
            m_new = jnp.maximum(m_cur[i], jnp.max(ss[i], axis=0, keepdims=True) + shift(i, t))
            dst_ref[i] = ss[i]
            m_next.append(m_new)
            alpha.append(jnp.exp2(m_cur[i] - m_new))
        return tuple(m_next), tuple(alpha)

    def stage_b(t, src_ref, carry):
        m_cur, alpha = carry
        for i in range(n_soft):
            p = jnp.exp2(src_ref[i] - (m_cur[i] - shift(i, t)))
            acc_ref[i] = alpha[i] * acc_ref[i] + jnp.dot(load_vt(t, i), p.astype(BF16),
                                                         preferred_element_type=F32)

    def half(t, src_ref, dst_ref, carry):
        ss = [scores(i, t + 1, False) for i in range(n_soft)]
        stage_b(t, src_ref, carry)
        return stage_a(t + 1, ss, dst_ref, carry[0])

    def body(u, carry):
        carry = half(2 * u, s_ref[0], s_ref[1], carry)
        return half(2 * u + 1, s_ref[1], s_ref[0], carry)

    carry = stage_a(0, [scores(i, 0, True) for i in range(n_soft)], s_ref[0], m0)
    carry = lax.fori_loop(0, n_steps // 2, body, carry)

    @pl.when(n_steps % 2 == 1)
    def _():
        stage_b(n_steps, s_ref[1], half(n_steps - 1, s_ref[0], s_ref[1], carry))

    @pl.when(n_steps % 2 == 0)
    def _():
        stage_b(n_steps, s_ref[0], carry)


def _row_mask(shape, lo, hi):
    row = lax.broadcasted_iota(jnp.int32, shape, 0)
    return jnp.logical_and(row >= lo, row < hi)


def _lane_mask(shape, lo, hi):
    lane = lax.broadcasted_iota(jnp.int32, shape, len(shape) - 1)
    return jnp.logical_and(lane >= lo, lane < hi)


def _normalized(acc_ref, i):
    return acc_ref[i, 0:HEAD_LANES, :] / acc_ref[i, HEAD_LANES:HEAD_LANES + 1, :]


def _diff_body(sl_ref, q_ref, k_ref, vt_ref, mask_ref, kb_ref, lam_ref, sub_ref, o_ref,
               qs_ref, s0_ref, s1_ref, acc_ref, *, tq, lam_init):
    pair, qi = pl.program_id(1), pl.program_id(2)
    tk = tq
    q = q_ref[...].astype(F32)
    for hd in range(2):
        sel = jnp.where(_lane_mask(q.shape, N_BIAS_SPLIT * hd, N_BIAS_SPLIT * (hd + 1)), 1.0, 0.0).astype(BF16)
        for mp in range(2):
            lo = hd * HEAD_LANES + mp * DIFF_HD
            qs_ref[2 * hd + mp, :, 0:LANES] = jnp.where(_lane_mask(q.shape, lo, lo + DIFF_HD), q, 0.0).astype(BF16)
            qs_ref[2 * hd + mp, :, LANES:2 * LANES] = sel
    acc_ref[...] = jnp.zeros(acc_ref.shape, F32)

    def scores(i, t, first):
        off = pl.multiple_of((qi - t) * tk, tk)
        k = jnp.concatenate([k_ref[pl.ds(off, tk), :], kb_ref[...]], axis=1)
        s = lax.dot_general(k, qs_ref[i], _NT, preferred_element_type=F32)
        return s + mask_ref[...] if first else s

    def shift(i, t):
        return sl_ref[2 * pair + i // 2] * (-(t * tk)).astype(F32) if not isinstance(t, int) else 0.0

    def load_vt(t, i):
        return vt_ref[pl.ds((i // 2) * VT_ROWS, VT_ROWS), pl.ds(pl.multiple_of((qi - t) * tk, tk), tk)]

    m0 = tuple(jnp.full((1, tq), NEG_INF, F32) for _ in range(4))
    _attn_pipeline(4, qi, scores, shift, load_vt, m0, (s0_ref, s1_ref), acc_ref)

    lp = lam_ref[...]
    lam = (jnp.exp(jnp.sum(lp[0:1, :] * lp[1:2, :], axis=-1, keepdims=True))
           - jnp.exp(jnp.sum(lp[2:3, :] * lp[3:4, :], axis=-1, keepdims=True)) + lam_init)
    o_h = [_normalized(acc_ref, 2 * hd) - lam * _normalized(acc_ref, 2 * hd + 1) for hd in range(2)]
    o = jnp.concatenate(o_h, axis=0).T
    in_h0 = _lane_mask((tq, LANES), 0, HEAD_LANES)
    sq = o * o
    ms0 = jnp.sum(jnp.where(in_h0, sq, 0.0), axis=-1, keepdims=True)
    ms1 = jnp.sum(jnp.where(in_h0, 0.0, sq), axis=-1, keepdims=True)
    ms = jnp.where(in_h0, ms0, ms1) * (1.0 / HEAD_LANES)
    y = o * lax.rsqrt(ms + RMS_EPS) * sub_ref[...]
    o_ref[...] = (y * (1.0 - lam_init)).astype(o_ref.dtype)


N_BIAS_SPLIT = 3


def _alibi_key_columns(slopes_l2, tk):
    dj = jnp.arange(tk, dtype=F32)[None, :, None]
    val = jnp.asarray(slopes_l2, F32).reshape(-1, 1, 2) * dj
    cols = jnp.stack(_split_bf16(val, N_BIAS_SPLIT), axis=-1).reshape(val.shape[0], tk, 2 * N_BIAS_SPLIT)
    return jnp.pad(cols, ((0, 0), (0, 0), (0, LANES - 2 * N_BIAS_SPLIT)))


def _split_bf16(x, n):
    pieces = []
    for _ in range(n):
        bits = lax.bitcast_convert_type(x, jnp.uint32) & jnp.uint32(0xFFFF0000)
        head = lax.bitcast_convert_type(bits, F32)
        pieces.append(head.astype(BF16))
        x = x - head
    return pieces


def _diff_attention(a3, avt, slopes_l2, mask_t, kb, lam_p, sub128, layer_idx, tq):
    b, s, _ = a3.shape
    nq = s // tq
    lam_init = 0.8 - 0.6 * math.exp(-0.3 * layer_idx)
    body = functools.partial(_diff_body, tq=tq, lam_init=lam_init)
    grid_spec = pltpu.PrefetchScalarGridSpec(
        num_scalar_prefetch=1,
        grid=(b, 2, nq),
        in_specs=[
            pl.BlockSpec((None, tq, LANES), lambda bi, p, qi, sl: (bi, qi, p)),
            pl.BlockSpec((None, s, LANES), lambda bi, p, qi, sl: (bi, 0, 2 + p)),
            pl.BlockSpec((None, 2 * VT_ROWS, s), lambda bi, p, qi, sl: (bi, p, 0)),
            pl.BlockSpec((tq, tq), lambda bi, p, qi, sl: (0, 0)),
            pl.BlockSpec((None, tq, LANES), lambda bi, p, qi, sl: (p, 0, 0)),
            pl.BlockSpec((4, DIFF_HD), lambda bi, p, qi, sl: (0, 0)),
            pl.BlockSpec((1, LANES), lambda bi, p, qi, sl: (0, 0)),
        ],
        out_specs=pl.BlockSpec((None, tq, LANES), lambda bi, p, qi, sl: (bi, qi, p)),
        scratch_shapes=[pltpu.VMEM((4, tq, 2 * LANES), BF16), pltpu.VMEM((4, tq, tq), F32),
                        pltpu.VMEM((4, tq, tq), F32), pltpu.VMEM((4, VT_ROWS, tq), F32)],
    )
    return pl.pallas_call(
        body, grid_spec=grid_spec,
        out_shape=jax.ShapeDtypeStruct((b, s, 256), BF16),
        compiler_params=_params(("parallel", "parallel", "arbitrary")),
        name="diff_attn",
    )(slopes_l2, a3, a3, avt, mask_t, kb, lam_p, sub128)


def _mla_body(q0_ref, q1_ref, k_ref, vt_ref, mask_ref, o_ref, s0_ref, s1_ref, acc_ref, *, tq):
    qi = pl.program_id(2)
    tk = tq
    acc_ref[...] = jnp.zeros(acc_ref.shape, F32)
    q_refs = (q0_ref, q1_ref)

    def scores(i, t, first):
        k = k_ref[pl.ds(pl.multiple_of((qi - t) * tk, tk), tk), :]
        s = lax.dot_general(k, q_refs[i][...], _NT, preferred_element_type=F32)
        return s + mask_ref[...] if first else s

    def load_vt(t, i):
        return vt_ref[pl.ds(i * VT_ROWS, VT_ROWS), pl.ds(pl.multiple_of((qi - t) * tk, tk), tk)]

    m0 = tuple(jnp.full((1, tq), NEG_INF, F32) for _ in range(2))
    _attn_pipeline(2, qi, scores, lambda i, t: 0.0, load_vt, m0, (s0_ref, s1_ref), acc_ref)
    o_t = jnp.concatenate([_normalized(acc_ref, 0), _normalized(acc_ref, 1)], axis=0)
    o_ref[...] = o_t.T.astype(o_ref.dtype)


def _mla_attention(cq3, ck3, cvt, mask_t, tq):
    b, s, _ = cq3.shape
    nq = s // tq
    return pl.pallas_call(
        functools.partial(_mla_body, tq=tq),
        grid=(b, 2, nq),
        in_specs=[
            pl.BlockSpec((None, tq, 256), lambda bi, g, qi: (bi, qi, 2 * g)),
            pl.BlockSpec((None, tq, 256), lambda bi, g, qi: (bi, qi, 2 * g + 1)),
            pl.BlockSpec((None, s, 256), lambda bi, g, qi: (bi, 0, g)),
            pl.BlockSpec((None, 2 * VT_ROWS, s), lambda bi, g, qi: (bi, g, 0)),
            pl.BlockSpec((tq, tq), lambda bi, g, qi: (0, 0)),
        ],
        out_specs=pl.BlockSpec((None, tq, LANES), lambda bi, g, qi: (bi, qi, g)),
        out_shape=jax.ShapeDtypeStruct((b, s, 256), BF16),
        scratch_shapes=[pltpu.VMEM((2, tq, tq), F32), pltpu.VMEM((2, tq, tq), F32),
                        pltpu.VMEM((2, VT_ROWS, tq), F32)],
        compiler_params=_params(("parallel", "parallel", "arbitrary")),
        name="mla_attn",
    )(cq3, cq3, ck3, cvt, mask_t)


def _band_body(m0_ref, q_ref, k_ref, vt_ref, bias_ref, o_ref, qs_ref, s0_ref, s1_ref, acc_ref,
               *, tq, n_off, l_init):
    slab, qi = pl.program_id(1), pl.program_id(2)
    tk = tq
    q = q_ref[...].astype(F32)
    for i in range(2):
        qs_ref[i] = jnp.where(_lane_mask(q.shape, i * HEAD_LANES, (i + 1) * HEAD_LANES), q, 0.0).astype(BF16)
        acc_ref[i] = jnp.where(_row_mask((VT_ROWS, tq), 0, HEAD_LANES), 0.0, l_init).astype(F32)

    def scores(i, t, first):
        k = k_ref[pl.ds(pl.multiple_of((qi - t) * tk, tk), tk), :]
        return lax.dot_general(k, qs_ref[i], _NT, preferred_element_type=F32) + bias_ref[i, t]

    def load_vt(t, i):
        return vt_ref[pl.ds(i * VT_ROWS, VT_ROWS), pl.ds(pl.multiple_of((qi - t) * tk, tk), tk)]

    m0 = tuple(jnp.full((1, tq), m0_ref[2 * slab + i], F32) for i in range(2))
    _attn_pipeline(2, jnp.minimum(qi, n_off - 1), scores, lambda i, t: 0.0, load_vt, m0, (s0_ref, s1_ref), acc_ref)
    o_t = jnp.concatenate([_normalized(acc_ref, 0), _normalized(acc_ref, 1)], axis=0)
    o_ref[...] = o_t.T.astype(o_ref.dtype)


def _band_attention(x3, xvt, k_blk_of, vt_blk_of, bias, m0, l_init, tq, name):
    b, s, _ = x3.shape
    nq = s // tq
    n_off = bias.shape[2]
    grid_spec = pltpu.PrefetchScalarGridSpec(
        num_scalar_prefetch=1,
        grid=(b, 2, nq),
        in_specs=[
            pl.BlockSpec((None, tq, LANES), lambda bi, p, qi, m0r: (bi, qi, p)),
            pl.BlockSpec((None, s, LANES), lambda bi, p, qi, m0r: (bi, 0, k_blk_of(p))),
            pl.BlockSpec((None, 2 * VT_ROWS, s), lambda bi, p, qi, m0r: (bi, vt_blk_of(p), 0)),
            pl.BlockSpec((None, 2, n_off, tq, tq), lambda bi, p, qi, m0r: (p, 0, 0, 0, 0)),
        ],
        out_specs=pl.BlockSpec((None, tq, LANES), lambda bi, p, qi, m0r: (bi, qi, p)),
        scratch_shapes=[pltpu.VMEM((2, tq, LANES), BF16), pltpu.VMEM((2, tq, tq), F32),
                        pltpu.VMEM((2, tq, tq), F32), pltpu.VMEM((2, VT_ROWS, tq), F32)],
    )
    return pl.pallas_call(
        functools.partial(_band_body, tq=tq, n_off=n_off, l_init=l_init),
        grid_spec=grid_spec,
        out_shape=jax.ShapeDtypeStruct((b, s, 256), BF16),
        compiler_params=_params(("parallel", "parallel", "arbitrary")),
        name=name,
    )(m0, x3, x3, xvt, bias)


def _band_bias(slopes, tq, n_off, mult_fn):
    di = jnp.arange(tq, dtype=jnp.int32)[None, :]
    dj = jnp.arange(tq, dtype=jnp.int32)[:, None]
    off = jnp.arange(n_off, dtype=jnp.int32)[:, None, None]
    delta = off * tq + di - dj
    mult = mult_fn(delta)
    sl = jnp.asarray(slopes, F32)[:, None, None, None] * LOG2E
    val = -sl * delta.astype(F32)[None] + jnp.log2(jnp.maximum(mult, 1).astype(F32))[None]
    return jnp.where((mult > 0)[None], val, NEG_INF)


def _dil_mult(delta):
    m = jnp.zeros(delta.shape, jnp.int32)
    for window, d in DIL_PATTERNS:
        m = m + ((delta >= 0) & (delta <= window) & (delta % d == 0)).astype(jnp.int32)
    return m


def _swa_mult(delta):
    return ((delta >= 0) & (delta <= SWA_WINDOW - 1)).astype(jnp.int32)


def _outproj_body(*refs, with_router):
    if with_router:
        (oa, ob, oc, od, w_ref, x_ref, g_ref, rh_ref, rl_ref, x1_ref, h2_ref, lg_ref) = refs
    else:
        (oa, ob, oc, od, w_ref, x_ref, g_ref, x1_ref, h2_ref) = refs
    acc = x_ref[...]
    for i, o in enumerate((oa, ob, oc, od)):
        acc = acc + jnp.dot(o[...], w_ref[i], preferred_element_type=F32)
    x1_ref[...] = acc
    h2 = acc * lax.rsqrt(jnp.mean(acc * acc, axis=-1, keepdims=True) + RMS_EPS) * g_ref[...]
    hi = h2.astype(BF16)
    if with_router:
        for c in range(SC_CHUNKS):
            h2_ref[c] = h2[:, c * SC_COLS:(c + 1) * SC_COLS]
        lo = (h2 - hi.astype(F32)).astype(BF16)
        lg_ref[...] = (jnp.dot(hi, rh_ref[...], preferred_element_type=F32)
                       + (jnp.dot(hi, rl_ref[...], preferred_element_type=F32)
                          + jnp.dot(lo, rh_ref[...], preferred_element_type=F32)))
    else:
        h2_ref[...] = hi


def _outproj(oa, ob, oc, od, w4, x2d, g, router_hl=None):
    t = x2d.shape[0]
    tm = _pick(t, 512)
    with_router = router_hl is not None
    full = lambda shape: pl.BlockSpec(shape, lambda i: (0,) * len(shape))
    row = lambda w: pl.BlockSpec((tm, w), lambda i: (i, 0))
    in_specs = [row(256)] * 4 + [full((4, 256, D_MODEL)), row(D_MODEL), full((1, D_MODEL))]
    args = [oa, ob, oc, od, w4, x2d, g]
    out_specs = [row(D_MODEL), row(D_MODEL)]
    out_shape = [jax.ShapeDtypeStruct((t, D_MODEL), F32), jax.ShapeDtypeStruct((t, D_MODEL), BF16)]
    if with_router:
        out_specs[1] = pl.BlockSpec((SC_CHUNKS, tm, SC_COLS), lambda i: (0, i, 0))
        out_shape[1] = jax.ShapeDtypeStruct((SC_CHUNKS, t, SC_COLS), F32)
        in_specs += [full((D_MODEL, LANES))] * 2
        args += list(router_hl)
        out_specs.append(row(LANES))
        out_shape.append(jax.ShapeDtypeStruct((t, LANES), F32))
    return pl.pallas_call(
        functools.partial(_outproj_body, with_router=with_router),
        grid=(t // tm,), in_specs=in_specs, out_specs=out_specs, out_shape=out_shape,
        compiler_params=_params(("parallel",)),
        name="outproj",
    )(*args)


FF_CHUNK = 512


def _swiglu(x, w1_ref, w3_ref, w2_ref):
    tf = w1_ref.shape[1]
    total = None
    for lo in range(0, tf, FF_CHUNK):
        hi = min(lo + FF_CHUNK, tf)
        a = jnp.dot(x, w1_ref[:, lo:hi], preferred_element_type=F32)
        b = jnp.dot(x, w3_ref[:, lo:hi], preferred_element_type=F32)
        mid = (a * jax.nn.sigmoid(a) * b).astype(BF16)
        part = jnp.dot(mid, w2_ref[lo:hi, :], preferred_element_type=F32)
        total = part if total is None else total + part
    return total


def _ffn_body(h_ref, w1_ref, w3_ref, w2_ref, x_ref, o_ref):
    o_ref[...] = x_ref[...] + _swiglu(h_ref[...], w1_ref, w3_ref, w2_ref)


def _ffn(h2, w1, w3, w2, x1):
    t = h2.shape[0]
    tm = _pick(t, 512)
    resident = lambda shape: pl.BlockSpec(shape, lambda i: (0, 0), pipeline_mode=pl.Buffered(1))
    return pl.pallas_call(
        _ffn_body,
        grid=(t // tm,),
        in_specs=[pl.BlockSpec((tm, D_MODEL), lambda i: (i, 0)),
                  resident((D_MODEL, D_FF)), resident((D_MODEL, D_FF)), resident((D_FF, D_MODEL)),
                  pl.BlockSpec((tm, D_MODEL), lambda i: (i, 0))],
        out_specs=pl.BlockSpec((tm, D_MODEL), lambda i: (i, 0)),
        out_shape=jax.ShapeDtypeStruct((t, D_MODEL), F32),
        compiler_params=_params(("parallel",)),
        name="ffn",
    )(h2, w1, w3, w2, x1)


def _route_body(lg_ref, tri_ref, info_ref, cnt_ref, carry_ref):
    i = pl.program_id(0)

    @pl.when(i == 0)
    def _():
        carry_ref[...] = jnp.zeros(carry_ref.shape, F32)

    lg = lg_ref[...]
    lane = lax.broadcasted_iota(jnp.int32, lg.shape, 1)
    lg = jnp.where(lane < N_EXPERTS, lg, -jnp.inf)
    m1 = jnp.max(lg, axis=-1, keepdims=True)
    i1 = jnp.min(jnp.where(lg == m1, lane, LANES), axis=-1, keepdims=True)
    oh1 = lane == i1
    lg2 = jnp.where(oh1, -jnp.inf, lg)
    m2 = jnp.max(lg2, axis=-1, keepdims=True)
    i2 = jnp.min(jnp.where(lg2 == m2, lane, LANES), axis=-1, keepdims=True)
    oh2 = lane == i2
    e2 = jnp.exp(m2 - m1)
    g1 = 1.0 / (1.0 + e2)
    g2 = e2 / (1.0 + e2)
    oh = jnp.where(jnp.logical_or(oh1, oh2), 1.0, 0.0)
    rank = jnp.dot(tri_ref[...], oh.astype(BF16), preferred_element_type=F32) + carry_ref[...]
    r1 = jnp.sum(jnp.where(oh1, rank, 0.0), axis=-1, keepdims=True)
    r2 = jnp.sum(jnp.where(oh2, rank, 0.0), axis=-1, keepdims=True)
    carry_ref[...] += jnp.sum(oh, axis=0, keepdims=True)
    cnt_ref[...] = carry_ref[...]
    info = jnp.zeros(lg.shape, F32)
    for c, val in enumerate((i1.astype(F32), i2.astype(F32), r1, r2, g1, g2)):
        info = jnp.where(lane == c, val, info)
    info_ref[...] = info


def _route(logits):
    t = logits.shape[0]
    tm = _pick(t, 512)
    tri = (jnp.arange(tm)[:, None] > jnp.arange(tm)[None, :]).astype(BF16)
    return pl.pallas_call(
        _route_body,
        grid=(t // tm,),
        in_specs=[pl.BlockSpec((tm, LANES), lambda i: (i, 0)), pl.BlockSpec((tm, tm), lambda i: (0, 0))],
        out_specs=[pl.BlockSpec((tm, LANES), lambda i: (i, 0)), pl.BlockSpec((1, LANES), lambda i: (0, 0))],
        out_shape=[jax.ShapeDtypeStruct((t, LANES), F32), jax.ShapeDtypeStruct((1, LANES), F32)],
        scratch_shapes=[pltpu.VMEM((1, LANES), F32)],
        compiler_params=_params(("arbitrary",)),
        name="route",
    )(logits, tri)


def _moe_ffn_body(be_ref, bv_ref, x_ref, w1_ref, w3_ref, w2_ref, o_ref, acc_ref):
    i, f = pl.program_id(0), pl.program_id(1)
    last = pl.num_programs(1) - 1
    valid = bv_ref[i]

    @pl.when(valid > 0)
    def _():
        x = jnp.concatenate([x_ref[c] for c in range(SC_CHUNKS)], axis=1)
        x = jnp.where(lax.broadcasted_iota(jnp.int32, x.shape, 0) < valid, x, 0.0).astype(BF16)
        contrib = _swiglu(x, w1_ref, w3_ref, w2_ref)

        @pl.when(f == 0)
        def _():
            acc_ref[...] = contrib

        @pl.when(f > 0)
        def _():
            acc_ref[...] += contrib

        @pl.when(f == last)
        def _():
            for c in range(SC_CHUNKS):
                o_ref[c] = acc_ref[:, c * SC_COLS:(c + 1) * SC_COLS]

    @pl.when(jnp.logical_and(valid == 0, f == last))
    def _():
        o_ref[...] = jnp.zeros(o_ref.shape, o_ref.dtype)


def _moe_ffn(xs, blk_expert, blk_valid, w1, w3, w2, bm):
    rows = xs.shape[1]
    tf = D_FF_EXPERT // 2
    chunked = pl.BlockSpec((SC_CHUNKS, bm, SC_COLS), lambda i, f, be, bv: (0, i, 0))
    grid_spec = pltpu.PrefetchScalarGridSpec(
        num_scalar_prefetch=2,
        grid=(rows // bm, D_FF_EXPERT // tf),
        in_specs=[chunked,
                  pl.BlockSpec((None, D_MODEL, tf), lambda i, f, be, bv: (be[i], 0, f)),
                  pl.BlockSpec((None, D_MODEL, tf), lambda i, f, be, bv: (be[i], 0, f)),
                  pl.BlockSpec((None, tf, D_MODEL), lambda i, f, be, bv: (be[i], f, 0))],
        out_specs=chunked,
        scratch_shapes=[pltpu.VMEM((bm, D_MODEL), F32)],
    )
    return pl.pallas_call(
        _moe_ffn_body, grid_spec=grid_spec,
        out_shape=jax.ShapeDtypeStruct(xs.shape, F32),
        compiler_params=_params(("parallel", "arbitrary")),
        name="moe_ffn",
    )(blk_expert, blk_valid, xs, w1, w3, w2)


def _final_body(*refs, with_moe):
    if with_moe:
        x_ref, y0_ref, y1_ref, info_ref, g_ref, o_ref = refs
        info = info_ref[...]
        y0 = jnp.concatenate([y0_ref[c] for c in range(SC_CHUNKS)], axis=1)
        y1 = jnp.concatenate([y1_ref[c] for c in range(SC_CHUNKS)], axis=1)
        x = x_ref[...] + (info[:, 4:5] * y0 + info[:, 5:6] * y1)
    else:
        x_ref, g_ref, o_ref = refs
        x = x_ref[...]
    o_ref[...] = x * lax.rsqrt(jnp.mean(x * x, axis=-1, keepdims=True) + RMS_EPS) * g_ref[...]


def _final(x2d, g, moe=None):
    t = x2d.shape[0]
    tm = _pick(t, 512)
    row = lambda w: pl.BlockSpec((tm, w), lambda i: (i, 0))
    in_specs, args = [row(D_MODEL)], [x2d]
    if moe is not None:
        yg, info = moe
        slot = lambda k: pl.BlockSpec((None, SC_CHUNKS, tm, SC_COLS), lambda i: (k, 0, i, 0))
        in_specs += [slot(0), slot(1), row(LANES)]
        args += [yg, yg, info]
    in_specs.append(pl.BlockSpec((1, D_MODEL), lambda i: (0, 0)))
    args.append(g)
    return pl.pallas_call(
        functools.partial(_final_body, with_moe=moe is not None),
        grid=(t // tm,), in_specs=in_specs, out_specs=row(D_MODEL),
        out_shape=jax.ShapeDtypeStruct((t, D_MODEL), F32),
        compiler_params=_params(("parallel",)),
        name="final",
    )(*args)


def _prep_attn_weights(w_in, w_out, w_uq, w_ukv):
    widths = (256, 256, 256, 256, 256, 256, MLA_Q_LORA, MLA_KV_LORA, MLA_ROPE, 256, 128, 128)
    offs = np.concatenate([[0], np.cumsum(widths)])
    qa, ka, va, qb, kb, vb, cq, ckv, kpe, qd, kd, vd = [w_in[:, offs[i]:offs[i + 1]] for i in range(12)]
    hperm = np.array([0, 2, 1, 3])
    qd = qd.reshape(D_MODEL, 4, SWA_HD)[:, hperm].reshape(D_MODEL, 256)
    pad = jnp.zeros((D_MODEL, LANES - MLA_ROPE), w_in.dtype)
    half = MLA_ROPE // 2
    kpe_sw = jnp.concatenate([kpe[:, half:], kpe[:, :half]], axis=1)
    w_main = jnp.concatenate([qa, ka, va, qb, kb, vb, qd, kd, vd, cq, ckv, kpe, pad, kpe_sw, pad], axis=1)

    uq = w_uq.reshape(MLA_Q_LORA, MLA_HEADS, MLA_NOPE + MLA_ROPE)
    z = lambda n: jnp.zeros((MLA_Q_LORA, n), w_uq.dtype)
    mains, sws = [], []
    for hd in range(MLA_HEADS):
        nope, rope = uq[:, hd, :MLA_NOPE], uq[:, hd, MLA_NOPE:]
        nope128 = jnp.concatenate([nope, z(64)] if hd % 2 == 0 else [z(64), nope], axis=1)
        mains.append(jnp.concatenate([rope, z(LANES - MLA_ROPE), nope128], axis=1))
        sws.append(jnp.concatenate([rope[:, half:], rope[:, :half], z(LANES - MLA_ROPE)], axis=1))
    wq_big = jnp.concatenate(mains + sws, axis=1)

    ukv = w_ukv.reshape(MLA_KV_LORA, MLA_HEADS, MLA_NOPE + MLA_V)
    wkv = jnp.concatenate([ukv[:, :, :MLA_NOPE].reshape(MLA_KV_LORA, 256),
                           ukv[:, :, MLA_NOPE:].reshape(MLA_KV_LORA, 256)], axis=1)

    wo = w_out.reshape(4, 256, D_MODEL)
    wo_d = wo[3].reshape(4, SWA_HD, D_MODEL)[hperm].reshape(256, D_MODEL)
    w4 = jnp.stack([wo[0], wo[1], wo[2], wo_d])
    return w_main.astype(BF16), wq_big.astype(BF16), wkv.astype(BF16), w4.astype(BF16)


def _rope_tables(s):
    inv = ROPE_THETA ** (-jnp.arange(0, MLA_ROPE, 2, dtype=F32) / MLA_ROPE)
    ang = jnp.arange(s, dtype=F32)[:, None] * inv[None, :]
    cos, sin = jnp.cos(ang), jnp.sin(ang)
    z = jnp.zeros((s, LANES - MLA_ROPE), F32)
    cos128 = jnp.concatenate([cos, cos, z], axis=1)
    sin128 = jnp.concatenate([-sin, sin, z], axis=1)
    c_mla = (MLA_NOPE + MLA_ROPE) ** -0.5 * LOG2E
    return cos128 * c_mla, sin128 * c_mla, cos128, sin128


def _col_scale():
    ca = DIFF_HD ** -0.5 * LOG2E
    cb = DIL_HD ** -0.5 * LOG2E
    cd = SWA_HD ** -0.5 * LOG2E
    v = np.ones((1, N_SCALED), np.float32)
    v[0, 0:256] = ca
    v[0, 768:1024] = cb
    v[0, 1536:1792] = cd
    return jnp.asarray(v)


def _moe_layer(h2, logits, w1, w3, w2, bm):
    t = logits.shape[0]
    info, counts = _route(logits)
    cnt = counts[0, :N_EXPERTS].astype(jnp.int32)
    padded = (cnt + bm - 1) // bm * bm
    pends = jnp.cumsum(padded)
    pstarts = pends - padded
    e12 = info[:, 0:2].astype(jnp.int32)
    pos = pstarts[e12] + info[:, 2:4].astype(jnp.int32)
    nb = (t * TOP_K) // bm + N_EXPERTS
    rows = nb * bm
    blk_start = jnp.arange(nb, dtype=jnp.int32) * bm
    blk_expert = jnp.minimum(jnp.sum((pends[None, :] <= blk_start[:, None]).astype(jnp.int32), axis=1),
                             N_EXPERTS - 1)
    blk_valid = jnp.clip(pstarts[blk_expert] + cnt[blk_expert] - blk_start, 0, bm).astype(jnp.int32)
    sub = (jnp.arange(SC_CHUNKS, dtype=jnp.int32) * rows)[None, :, None] + pos.T[:, None, :]
    xs = _sc_scatter_rows(h2.reshape(SC_CHUNKS * t, SC_COLS),
                          [sub[k].reshape(1, SC_CHUNKS * t) for k in range(TOP_K)], SC_CHUNKS * rows)
    ys = _moe_ffn(xs.reshape(SC_CHUNKS, rows, SC_COLS), blk_expert.astype(jnp.int32), blk_valid, w1, w3, w2, bm)
    yg = _sc_gather_rows(ys.reshape(SC_CHUNKS * rows, SC_COLS), sub.reshape(1, TOP_K * SC_CHUNKS * t))
    return yg.reshape(TOP_K, SC_CHUNKS, t, SC_COLS), info


SC_WINDOW = 128
SC_COLS = 256
SC_CHUNKS = D_MODEL // SC_COLS


def _sc_mesh():
    return plsc.VectorSubcoreMesh(core_axis_name="c", subcore_axis_name="s")


def _sc_scatter_rows(x, idx_list, rows):
    t, d = x.shape
    n = len(idx_list)

    @functools.partial(pl.kernel, out_type=jax.ShapeDtypeStruct((rows, d), x.dtype), mesh=_sc_mesh(),
                       scratch_types=[])
    def scatter_kernel(x_hbm, *rest):
        idx_hbms, o_hbm = rest[:n], rest[n]

        def body(x_vmem, *idx_vmems):
            for iv in idx_vmems:
                pltpu.sync_copy(x_vmem, o_hbm.at[iv.at[0]])

        pltpu.emit_pipeline(
            body, grid=(t // SC_WINDOW,),
            in_specs=[pl.BlockSpec((SC_WINDOW, d), lambda i: (i, 0))]
            + [pl.BlockSpec((1, SC_WINDOW), lambda i: (0, i))] * n,
            out_specs=[], core_axis_name=("c", "s"), dimension_semantics=(pltpu.PARALLEL,),
        )(x_hbm, *idx_hbms)

    return scatter_kernel(x, *idx_list)


def _sc_gather_rows(y, idx):
    m = idx.shape[1]
    d = y.shape[1]

    @functools.partial(pl.kernel, out_type=jax.ShapeDtypeStruct((m, d), y.dtype), mesh=_sc_mesh(),
                       scratch_types=[])
    def gather_kernel(y_hbm, idx_hbm, o_hbm):
        def body(idx_vmem, o_vmem):
            pltpu.sync_copy(y_hbm.at[idx_vmem.at[0]], o_vmem)

        pltpu.emit_pipeline(
            body, grid=(m // SC_WINDOW,),
            in_specs=[pl.BlockSpec((1, SC_WINDOW), lambda i: (0, i))],
            out_specs=[pl.BlockSpec((SC_WINDOW, d), lambda i: (i, 0))],
            core_axis_name=("c", "s"), dimension_semantics=(pltpu.PARALLEL,),
        )(idx_hbm, o_hbm)

    return gather_kernel(y, idx)


def kernel(x, attn_norm, w_in, w_out, diff_lambda, diff_subln, mla_q_norm, mla_w_uq, mla_kv_norm, mla_w_ukv,
           swa_sinks, ffn_norm, ffn_w1, ffn_w3, ffn_w2, moe_router, moe_w1, moe_w3, moe_w2, final_norm):
    bsz, seq, _ = x.shape
    t = bsz * seq
    depth = w_in.shape[0]
    slopes_a, slopes_b, slopes_d = _alibi_slopes()
    tq_dense = _pick(seq, 512)
    tq_band = _pick(seq, 256)

    tabs = _rope_tables(seq)
    colscale = _col_scale()
    di = jnp.arange(tq_dense)
    causal_mask_t = jnp.where(di[:, None] <= di[None, :], 0.0, NEG_INF).astype(F32)
    alibi_kb = _alibi_key_columns(slopes_a * LOG2E, tq_dense)
    tq_dil = _pick(seq, 512)
    n_off_dil = min((max(w for w, _ in DIL_PATTERNS) + tq_dil - 1) // tq_dil + 1, seq // tq_dil)
    n_off_swa = min((SWA_WINDOW - 1 + tq_band - 1) // tq_band + 1, seq // tq_band)
    dil_bias = _band_bias(slopes_b, tq_dil, n_off_dil, _dil_mult).reshape(2, 2, n_off_dil, tq_dil, tq_dil)
    swa_bias = _band_bias(slopes_d[np.array([0, 2, 1, 3])], tq_band, n_off_swa, _swa_mult
                          ).reshape(2, 2, n_off_swa, tq_band, tq_band)
    slopes_a_l2 = jnp.asarray(slopes_a * LOG2E, F32)
    dil_m0 = jnp.full((4,), NEG_INF, F32)

    x2d = x.reshape(t, D_MODEL)
    for l in range(depth):
        w_main, wq_big, wkv, w4 = _prep_attn_weights(w_in[l], w_out[l], mla_w_uq[l], mla_w_ukv[l])
        a, avt, b, bvt, d, dvt, cq, ck, cvt = _proj(x2d, attn_norm[l][None], w_main, colscale, mla_q_norm[l][None],
                                                    wq_big, mla_kv_norm[l][None], wkv, tabs, seq)
        a3, b3, d3 = (v.reshape(bsz, seq, -1) for v in (a, b, d))
        sub128 = jnp.concatenate([diff_subln[l], diff_subln[l]])[None]
        o_a = _diff_attention(a3, avt, slopes_a_l2, causal_mask_t, alibi_kb, diff_lambda[l], sub128, l, tq_dense)
        o_b = _band_attention(b3, bvt, lambda p: 2 + p, lambda p: p, dil_bias, dil_m0, 0.0, tq_dil, "dil_attn")
        o_c = _mla_attention(cq.reshape(bsz, seq, -1), ck.reshape(bsz, seq, -1), cvt, causal_mask_t, tq_dense)
        swa_m0 = (swa_sinks[l].astype(F32) * LOG2E)[np.array([0, 2, 1, 3])]
        o_d = _band_attention(d3, dvt, lambda p: 2, lambda p: 0, swa_bias, swa_m0, 1.0, tq_band, "swa_attn")
        outs = [v.reshape(t, 256) for v in (o_a, o_b, o_c, o_d)]
        j = l // 2
        if l % 2 == 0:
            x1, h2 = _outproj(*outs, w4, x2d, ffn_norm[l][None])
            x2d = _ffn(h2, ffn_w1[j].astype(BF16), ffn_w3[j].astype(BF16), ffn_w2[j].astype(BF16), x1)
            moe = None
        else:
            r = jnp.pad(moe_router[j], ((0, 0), (0, LANES - N_EXPERTS)))
            r_hi, r_lo = _split_bf16(r, 2)
            x1, h2, logits = _outproj(*outs, w4, x2d, ffn_norm[l][None], (r_hi, r_lo))
            bm = _pick(t, 512)
            moe = _moe_layer(h2, logits, moe_w1[j].astype(BF16), moe_w3[j].astype(BF16),
                             moe_w2[j].astype(BF16), bm)
            x2d = x1
        if l < depth - 1 and moe is not None:
            raise NotImplementedError("MoE combine is fused into the final norm; MoE layer must be last")
    return _final(x2d, final_norm[None], moe).reshape(bsz, seq, D_MODEL)
```

```python
import functools
import math

import numpy as np
import jax
import jax.numpy as jnp
from jax import lax
from jax.experimental import pallas as pl
from jax.experimental.pallas import tpu as pltpu
from jax.experimental.pallas import tpu_sc as plsc

D_MODEL = 1024
DIFF_HEADS, DIFF_HD = 4, 32
DIL_HEADS, DIL_HD = 4, 64
DIL_PATTERNS = ((128, 1), (512, 4), (2048, 16))
MLA_HEADS, MLA_Q_LORA, MLA_KV_LORA, MLA_NOPE, MLA_ROPE, MLA_V = 4, 384, 128, 64, 32, 64
ROPE_THETA = 10000.0
SWA_HEADS, SWA_KV_HEADS, SWA_HD, SWA_WINDOW = 4, 2, 64, 128
D_FF = 2816
N_EXPERTS, TOP_K, D_FF_EXPERT = 8, 2, 3584
RMS_EPS = 1e-6
NEG_INF = -1e30
N_ALIBI = DIFF_HEADS + DIL_HEADS + SWA_HEADS

LOG2E = 1.4426950408889634
LANES = 128
HEAD_LANES = 64
VMEM_LIMIT = 56 * 1024 * 1024

F32 = jnp.float32
BF16 = jnp.bfloat16

_NT = (((1,), (1,)), ((), ()))


def _params(sem, vmem=VMEM_LIMIT):
    return pltpu.CompilerParams(dimension_semantics=sem, vmem_limit_bytes=vmem)


def _alibi_slopes():
    s = 2.0 ** (-8.0 * (np.arange(N_ALIBI) + 1) / N_ALIBI)
    return s[0::3], s[1::3], s[2::3]


def _pick(n, pref):
    t = min(pref, n)
    while n % t:
        t //= 2
    return t


N_SCALED = 2048
C_CQ, C_CKV, C_KPE, C_KPE_SW, N_MAIN = 2048, 2432, 2560, 2688, 2816


ONES_ROWS = 16
VT_ROWS = HEAD_LANES + ONES_ROWS


def _vt_with_ones(v):
    vt = v.T
    ones = jnp.ones((ONES_ROWS, v.shape[0]), F32)
    parts = []
    for hd in range(v.shape[1] // HEAD_LANES):
        parts += [vt[hd * HEAD_LANES:(hd + 1) * HEAD_LANES], ones]
    return jnp.concatenate(parts, axis=0).astype(BF16)


def _proj_body(x_ref, g_ref, w_ref, cs_ref, qn_ref, wq_ref, kn_ref, wkv_ref,
               cosq_ref, sinq_ref, cosk_ref, sink_ref,
               a_ref, avt_ref, b_ref, bvt_ref, d_ref, dvt_ref, cq_ref, ck_ref, cvt_ref):
    x = x_ref[...]
    ms = jnp.mean(x * x, axis=-1, keepdims=True)
    h = (x * lax.rsqrt(ms + RMS_EPS) * g_ref[...]).astype(BF16)
    acc = jnp.dot(h, w_ref[...], preferred_element_type=F32)
    sc = acc[:, :N_SCALED] * cs_ref[...]
    a_ref[...] = sc[:, 0:512].astype(BF16)
    avt_ref[...] = _vt_with_ones(sc[:, 512:768])
    b_ref[...] = sc[:, 768:1280].astype(BF16)
    bvt_ref[...] = _vt_with_ones(sc[:, 1280:1536])
    d_ref[...] = sc[:, 1536:1920].astype(BF16)
    dvt_ref[...] = _vt_with_ones(sc[:, 1920:2048])

    cq = acc[:, C_CQ:C_CKV]
    hq = (cq * lax.rsqrt(jnp.mean(cq * cq, axis=-1, keepdims=True) + RMS_EPS) * qn_ref[...]).astype(BF16)
    yq = jnp.dot(hq, wq_ref[...], preferred_element_type=F32)
    cosq, sinq = cosq_ref[...], sinq_ref[...]
    c_mla = (MLA_NOPE + MLA_ROPE) ** -0.5 * LOG2E
    for hd in range(MLA_HEADS):
        main = yq[:, hd * 256:(hd + 1) * 256]
        sw = yq[:, 1024 + hd * LANES:1024 + (hd + 1) * LANES]
        cq_ref[:, hd * 256:hd * 256 + LANES] = (main[:, :LANES] * cosq + sw * sinq).astype(BF16)
        cq_ref[:, hd * 256 + LANES:(hd + 1) * 256] = (main[:, LANES:] * c_mla).astype(BF16)

    ckv = acc[:, C_CKV:C_KPE]
    hk = (ckv * lax.rsqrt(jnp.mean(ckv * ckv, axis=-1, keepdims=True) + RMS_EPS) * kn_ref[...]).astype(BF16)
    kv = jnp.dot(hk, wkv_ref[...], preferred_element_type=F32)
    kr = (acc[:, C_KPE:C_KPE_SW] * cosk_ref[...] + acc[:, C_KPE_SW:N_MAIN] * sink_ref[...]).astype(BF16)
    for grp in range(2):
        ck_ref[:, grp * 256:grp * 256 + LANES] = kr
        ck_ref[:, grp * 256 + LANES:(grp + 1) * 256] = kv[:, grp * LANES:(grp + 1) * LANES].astype(BF16)
    cvt_ref[...] = _vt_with_ones(kv[:, 256:512])


def _proj(x2d, g, w_main, colscale, qn, wq_big, kn, wkv, tabs, seq):
    t = x2d.shape[0]
    tm = _pick(seq, 512)
    nsb = seq // tm
    full = lambda shape: pl.BlockSpec(shape, lambda i: (0,) * len(shape))
    tab = pl.BlockSpec((tm, LANES), lambda i: (i % nsb, 0))
    row = lambda w: pl.BlockSpec((tm, w), lambda i: (i, 0))
    vt = lambda w: pl.BlockSpec((None, w, tm), lambda i: (i // nsb, 0, i % nsb))
    row_sds = lambda w: jax.ShapeDtypeStruct((t, w), BF16)
    vt_sds = lambda w: jax.ShapeDtypeStruct((t // seq, w, seq), BF16)
    return pl.pallas_call(
        _proj_body,
        grid=(t // tm,),
        in_specs=[row(D_MODEL), full((1, D_MODEL)), full((D_MODEL, N_MAIN)), full((1, N_SCALED)),
                  full((1, MLA_Q_LORA)), full((MLA_Q_LORA, 1536)), full((1, MLA_KV_LORA)),
                  full((MLA_KV_LORA, 512)), tab, tab, tab, tab],
        out_specs=[row(512), vt(4 * VT_ROWS), row(512), vt(4 * VT_ROWS), row(384), vt(2 * VT_ROWS),
                   row(1024), row(512), vt(4 * VT_ROWS)],
        out_shape=[row_sds(512), vt_sds(4 * VT_ROWS), row_sds(512), vt_sds(4 * VT_ROWS), row_sds(384),
                   vt_sds(2 * VT_ROWS), row_sds(1024), row_sds(512), vt_sds(4 * VT_ROWS)],
        compiler_params=_params(("parallel",)),
        name="proj",
    )(x2d, g, w_main, colscale, qn, wq_big, kn, wkv, *tabs)


def _attn_pipeline(n_soft, n_steps, scores, shift, load_vt, m0, s_ref, acc_ref):
    def stage_a(t, ss, dst_ref, m_cur):
        m_next, alpha = [], []
        for i in range(n_soft):
            m_new = jnp.maximum(m_cur[i], jnp.max(ss[i], axis=0, keepdims=True) + shift(i, t))
            dst_ref[i] = ss[i]
            m_next.append(m_new)
            alpha.append(jnp.exp2(m_cur[i] - m_new))
        return tuple(m_next), tuple(alpha)

    def stage_b(t, src_ref, carry):
        m_cur, alpha = carry
        for i in range(n_soft):
            p = jnp.exp2(src_ref[i] - (m_cur[i] - shift(i, t)))
            acc_ref[i] = alpha[i] * acc_ref[i] + jnp.dot(load_vt(t, i), p.astype(BF16),
                                                         preferred_element_type=F32)

    def half(t, src_ref, dst_ref, carry):
        ss = [scores(i, t + 1, False) for i in range(n_soft)]
        stage_b(t, src_ref, carry)
        return stage_a(t + 1, ss, dst_ref, carry[0])

    def body(u, carry):
        carry = half(2 * u, s_ref[0], s_ref[1], carry)
        return half(2 * u + 1, s_ref[1], s_ref[0], carry)

    carry = stage_a(0, [scores(i, 0, True) for i in range(n_soft)], s_ref[0], m0)
    carry = lax.fori_loop(0, n_steps // 2, body, carry)

    @pl.when(n_steps % 2 == 1)
    def _():
        stage_b(n_steps, s_ref[1], half(n_steps - 1, s_ref[0], s_ref[1], carry))

    @pl.when(n_steps % 2 == 0)
    def _():
        stage_b(n_steps, s_ref[0], carry)


def _row_mask(shape, lo, hi):
    row = lax.broadcasted_iota(jnp.int32, shape, 0)
    return jnp.logical_and(row >= lo, row < hi)


def _lane_mask(shape, lo, hi):
    lane = lax.broadcasted_iota(jnp.int32, shape, len(shape) - 1)
    return jnp.logical_and(lane >= lo, lane < hi)


def _normalized(acc_ref, i):
    return acc_ref[i, 0:HEAD_LANES, :] / acc_ref[i, HEAD_LANES:HEAD_LANES + 1, :]


def _diff_body(sl_ref, q_ref, k_ref, vt_ref, mask_ref, kb_ref, lam_ref, sub_ref, o_ref,
               qs_ref, s0_ref, s1_ref, acc_ref, *, tq, lam_init):
    pair, qi = pl.program_id(1), pl.program_id(2)
    tk = tq
    q = q_ref[...].astype(F32)
    for hd in range(2):
        sel = jnp.where(_lane_mask(q.shape, N_BIAS_SPLIT * hd, N_BIAS_SPLIT * (hd + 1)), 1.0, 0.0).astype(BF16)
        for mp in range(2):
            lo = hd * HEAD_LANES + mp * DIFF_HD
            qs_ref[2 * hd + mp, :, 0:LANES] = jnp.where(_lane_mask(q.shape, lo, lo + DIFF_HD), q, 0.0).astype(BF16)
            qs_ref[2 * hd + mp, :, LANES:2 * LANES] = sel
    acc_ref[...] = jnp.zeros(acc_ref.shape, F32)

    def scores(i, t, first):
        off = pl.multiple_of((qi - t) * tk, tk)
        k = jnp.concatenate([k_ref[pl.ds(off, tk), :], kb_ref[...]], axis=1)
        s = lax.dot_general(k, qs_ref[i], _NT, preferred_element_type=F32)
        return s + mask_ref[...] if first else s

    def shift(i, t):
        return sl_ref[2 * pair + i // 2] * (-(t * tk)).astype(F32) if not isinstance(t, int) else 0.0

    def load_vt(t, i):
        return vt_ref[pl.ds((i // 2) * VT_ROWS, VT_ROWS), pl.ds(pl.multiple_of((qi - t) * tk, tk), tk)]

    m0 = tuple(jnp.full((1, tq), NEG_INF, F32) for _ in range(4))
    _attn_pipeline(4, qi, scores, shift, load_vt, m0, (s0_ref, s1_ref), acc_ref)

    lp = lam_ref[...]
    lam = (jnp.exp(jnp.sum(lp[0:1, :] * lp[1:2, :], axis=-1, keepdims=True))
           - jnp.exp(jnp.sum(lp[2:3, :] * lp[3:4, :], axis=-1, keepdims=True)) + lam_init)
    o_h = [_normalized(acc_ref, 2 * hd) - lam * _normalized(acc_ref, 2 * hd + 1) for hd in range(2)]
    o = jnp.concatenate(o_h, axis=0).T
    in_h0 = _lane_mask((tq, LANES), 0, HEAD_LANES)
    sq = o * o
    ms0 = jnp.sum(jnp.where(in_h0, sq, 0.0), axis=-1, keepdims=True)
    ms1 = jnp.sum(jnp.where(in_h0, 0.0, sq), axis=-1, keepdims=True)
    ms = jnp.where(in_h0, ms0, ms1) * (1.0 / HEAD_LANES)
    y = o * lax.rsqrt(ms + RMS_EPS) * sub_ref[...]
    o_ref[...] = (y * (1.0 - lam_init)).astype(o_ref.dtype)


N_BIAS_SPLIT = 3


def _alibi_key_columns(slopes_l2, tk):
    dj = jnp.arange(tk, dtype=F32)[None, :, None]
    val = jnp.asarray(slopes_l2, F32).reshape(-1, 1, 2) * dj
    cols = jnp.stack(_split_bf16(val, N_BIAS_SPLIT), axis=-1).reshape(val.shape[0], tk, 2 * N_BIAS_SPLIT)
    return jnp.pad(cols, ((0, 0), (0, 0), (0, LANES - 2 * N_BIAS_SPLIT)))


def _split_bf16(x, n):
    pieces = []
    for _ in range(n):
        bits = lax.bitcast_convert_type(x, jnp.uint32) & jnp.uint32(0xFFFF0000)
        head = lax.bitcast_convert_type(bits, F32)
        pieces.append(head.astype(BF16))
        x = x - head
    return pieces


def _diff_attention(a3, avt, slopes_l2, mask_t, kb, lam_p, sub128, layer_idx, tq):
    b, s, _ = a3.shape
    nq = s // tq
    lam_init = 0.8 - 0.6 * math.exp(-0.3 * layer_idx)
    body = functools.partial(_diff_body, tq=tq, lam_init=lam_init)
    grid_spec = pltpu.PrefetchScalarGridSpec(
        num_scalar_prefetch=1,
        grid=(b, 2, nq),
        in_specs=[
            pl.BlockSpec((None, tq, LANES), lambda bi, p, qi, sl: (bi, qi, p)),
            pl.BlockSpec((None, s, LANES), lambda bi, p, qi, sl: (bi, 0, 2 + p)),
            pl.BlockSpec((None, 2 * VT_ROWS, s), lambda bi, p, qi, sl: (bi, p, 0)),
            pl.BlockSpec((tq, tq), lambda bi, p, qi, sl: (0, 0)),
            pl.BlockSpec((None, tq, LANES), lambda bi, p, qi, sl: (p, 0, 0)),
            pl.BlockSpec((4, DIFF_HD), lambda bi, p, qi, sl: (0, 0)),
            pl.BlockSpec((1, LANES), lambda bi, p, qi, sl: (0, 0)),
        ],
        out_specs=pl.BlockSpec((None, tq, LANES), lambda bi, p, qi, sl: (bi, qi, p)),
        scratch_shapes=[pltpu.VMEM((4, tq, 2 * LANES), BF16), pltpu.VMEM((4, tq, tq), F32),
                        pltpu.VMEM((4, tq, tq), F32), pltpu.VMEM((4, VT_ROWS, tq), F32)],
    )
    return pl.pallas_call(
        body, grid_spec=grid_spec,
        out_shape=jax.ShapeDtypeStruct((b, s, 256), BF16),
        compiler_params=_params(("parallel", "parallel", "arbitrary")),
        name="diff_attn",
    )(slopes_l2, a3, a3, avt, mask_t, kb, lam_p, sub128)


def _mla_body(q0_ref, q1_ref, k_ref, vt_ref, mask_ref, o_ref, s0_ref, s1_ref, acc_ref, *, tq):
    qi = pl.program_id(2)
    tk = tq
    acc_ref[...] = jnp.zeros(acc_ref.shape, F32)
    q_refs = (q0_ref, q1_ref)

    def scores(i, t, first):
        k = k_ref[pl.ds(pl.multiple_of((qi - t) * tk, tk), tk), :]
        s = lax.dot_general(k, q_refs[i][...], _NT, preferred_element_type=F32)
        return s + mask_ref[...] if first else s

    def load_vt(t, i):
        return vt_ref[pl.ds(i * VT_ROWS, VT_ROWS), pl.ds(pl.multiple_of((qi - t) * tk, tk), tk)]

    m0 = tuple(jnp.full((1, tq), NEG_INF, F32) for _ in range(2))
    _attn_pipeline(2, qi, scores, lambda i, t: 0.0, load_vt, m0, (s0_ref, s1_ref), acc_ref)
    o_t = jnp.concatenate([_normalized(acc_ref, 0), _normalized(acc_ref, 1)], axis=0)
    o_ref[...] = o_t.T.astype(o_ref.dtype)


def _mla_attention(cq3, ck3, cvt, mask_t, tq):
    b, s, _ = cq3.shape
    nq = s // tq
    return pl.pallas_call(
        functools.partial(_mla_body, tq=tq),
        grid=(b, 2, nq),
        in_specs=[
            pl.BlockSpec((None, tq, 256), lambda bi, g, qi: (bi, qi, 2 * g)),
            pl.BlockSpec((None, tq, 256), lambda bi, g, qi: (bi, qi, 2 * g + 1)),
            pl.BlockSpec((None, s, 256), lambda bi, g, qi: (bi, 0, g)),
            pl.BlockSpec((None, 2 * VT_ROWS, s), lambda bi, g, qi: (bi, g, 0)),
            pl.BlockSpec((tq, tq), lambda bi, g, qi: (0, 0)),
        ],
        out_specs=pl.BlockSpec((None, tq, LANES), lambda bi, g, qi: (bi, qi, g)),
        out_shape=jax.ShapeDtypeStruct((b, s, 256), BF16),
        scratch_shapes=[pltpu.VMEM((2, tq, tq), F32), pltpu.VMEM((2, tq, tq), F32),
                        pltpu.VMEM((2, VT_ROWS, tq), F32)],
        compiler_params=_params(("parallel", "parallel", "arbitrary")),
        name="mla_attn",
    )(cq3, cq3, ck3, cvt, mask_t)


def _band_body(m0_ref, q_ref, k_ref, vt_ref, bias_ref, o_ref, qs_ref, s0_ref, s1_ref, acc_ref,
               *, tq, n_off, l_init):
    slab, qi = pl.program_id(1), pl.program_id(2)
    tk = tq
    q = q_ref[...].astype(F32)
    for i in range(2):
        qs_ref[i] = jnp.where(_lane_mask(q.shape, i * HEAD_LANES, (i + 1) * HEAD_LANES), q, 0.0).astype(BF16)
        acc_ref[i] = jnp.where(_row_mask((VT_ROWS, tq), 0, HEAD_LANES), 0.0, l_init).astype(F32)

    def scores(i, t, first):
        k = k_ref[pl.ds(pl.multiple_of((qi - t) * tk, tk), tk), :]
        return lax.dot_general(k, qs_ref[i], _NT, preferred_element_type=F32) + bias_ref[i, t]

    def load_vt(t, i):
        return vt_ref[pl.ds(i * VT_ROWS, VT_ROWS), pl.ds(pl.multiple_of((qi - t) * tk, tk), tk)]

    m0 = tuple(jnp.full((1, tq), m0_ref[2 * slab + i], F32) for i in range(2))
    _attn_pipeline(2, jnp.minimum(qi, n_off - 1), scores, lambda i, t: 0.0, load_vt, m0, (s0_ref, s1_ref), acc_ref)
    o_t = jnp.concatenate([_normalized(acc_ref, 0), _normalized(acc_ref, 1)], axis=0)
    o_ref[...] = o_t.T.astype(o_ref.dtype)


def _band_attention(x3, xvt, k_blk_of, vt_blk_of, bias, m0, l_init, tq, name):
    b, s, _ = x3.shape
    nq = s // tq
    n_off = bias.shape[2]
    grid_spec = pltpu.PrefetchScalarGridSpec(
        num_scalar_prefetch=1,
        grid=(b, 2, nq),
        in_specs=[
            pl.BlockSpec((None, tq, LANES), lambda bi, p, qi, m0r: (bi, qi, p)),
            pl.BlockSpec((None, s, LANES), lambda bi, p, qi, m0r: (bi, 0, k_blk_of(p))),
            pl.BlockSpec((None, 2 * VT_ROWS, s), lambda bi, p, qi, m0r: (bi, vt_blk_of(p), 0)),
            pl.BlockSpec((None, 2, n_off, tq, tq), lambda bi, p, qi, m0r: (p, 0, 0, 0, 0)),
        ],
        out_specs=pl.BlockSpec((None, tq, LANES), lambda bi, p, qi, m0r: (bi, qi, p)),
        scratch_shapes=[pltpu.VMEM((2, tq, LANES), BF16), pltpu.VMEM((2, tq, tq), F32),
                        pltpu.VMEM((2, tq, tq), F32), pltpu.VMEM((2, VT_ROWS, tq), F32)],
    )
    return pl.pallas_call(
        functools.partial(_band_body, tq=tq, n_off=n_off, l_init=l_init),
        grid_spec=grid_spec,
        out_shape=jax.ShapeDtypeStruct((b, s, 256), BF16),
        compiler_params=_params(("parallel", "parallel", "arbitrary")),
        name=name,
    )(m0, x3, x3, xvt, bias)


def _band_bias(slopes, tq, n_off, mult_fn):
    di = jnp.arange(tq, dtype=jnp.int32)[None, :]
    dj = jnp.arange(tq, dtype=jnp.int32)[:, None]
    off = jnp.arange(n_off, dtype=jnp.int32)[:, None, None]
    delta = off * tq + di - dj
    mult = mult_fn(delta)
    sl = jnp.asarray(slopes, F32)[:, None, None, None] * LOG2E
    val = -sl * delta.astype(F32)[None] + jnp.log2(jnp.maximum(mult, 1).astype(F32))[None]
    return jnp.where((mult > 0)[None], val, NEG_INF)


def _dil_mult(delta):
    m = jnp.zeros(delta.shape, jnp.int32)
    for window, d in DIL_PATTERNS:
        m = m + ((delta >= 0) & (delta <= window) & (delta % d == 0)).astype(jnp.int32)
    return m


def _swa_mult(delta):
    return ((delta >= 0) & (delta <= SWA_WINDOW - 1)).astype(jnp.int32)


def _outproj_body(*refs, with_router):
    if with_router:
        (oa, ob, oc, od, w_ref, x_ref, g_ref, rh_ref, rl_ref, x1_ref, h2_ref, lg_ref) = refs
    else:
        (oa, ob, oc, od, w_ref, x_ref, g_ref, x1_ref, h2_ref) = refs
    acc = x_ref[...]
    for i, o in enumerate((oa, ob, oc, od)):
        acc = acc + jnp.dot(o[...], w_ref[i], preferred_element_type=F32)
    x1_ref[...] = acc
    h2 = acc * lax.rsqrt(jnp.mean(acc * acc, axis=-1, keepdims=True) + RMS_EPS) * g_ref[...]
    hi = h2.astype(BF16)
    if with_router:
        for c in range(SC_CHUNKS):
            h2_ref[c] = h2[:, c * SC_COLS:(c + 1) * SC_COLS]
        lo = (h2 - hi.astype(F32)).astype(BF16)
        lg_ref[...] = (jnp.dot(hi, rh_ref[...], preferred_element_type=F32)
                       + (jnp.dot(hi, rl_ref[...], preferred_element_type=F32)
                          + jnp.dot(lo, rh_ref[...], preferred_element_type=F32)))
    else:
        h2_ref[...] = hi


def _outproj(oa, ob, oc, od, w4, x2d, g, router_hl=None):
    t = x2d.shape[0]
    tm = _pick(t, 512)
    with_router = router_hl is not None
    full = lambda shape: pl.BlockSpec(shape, lambda i: (0,) * len(shape))
    row = lambda w: pl.BlockSpec((tm, w), lambda i: (i, 0))
    in_specs = [row(256)] * 4 + [full((4, 256, D_MODEL)), row(D_MODEL), full((1, D_MODEL))]
    args = [oa, ob, oc, od, w4, x2d, g]
    out_specs = [row(D_MODEL), row(D_MODEL)]
    out_shape = [jax.ShapeDtypeStruct((t, D_MODEL), F32), jax.ShapeDtypeStruct((t, D_MODEL), BF16)]
    if with_router:
        out_specs[1] = pl.BlockSpec((SC_CHUNKS, tm, SC_COLS), lambda i: (0, i, 0))
        out_shape[1] = jax.ShapeDtypeStruct((SC_CHUNKS, t, SC_COLS), F32)
        in_specs += [full((D_MODEL, LANES))] * 2
        args += list(router_hl)
        out_specs.append(row(LANES))
        out_shape.append(jax.ShapeDtypeStruct((t, LANES), F32))
    return pl.pallas_call(
        functools.partial(_outproj_body, with_router=with_router),
        grid=(t // tm,), in_specs=in_specs, out_specs=out_specs, out_shape=out_shape,
        compiler_params=_params(("parallel",)),
        name="outproj",
    )(*args)


FF_CHUNK = 512


def _swiglu(x, w1_ref, w3_ref, w2_ref):
    tf = w1_ref.shape[1]
    total = None
    for lo in range(0, tf, FF_CHUNK):
        hi = min(lo + FF_CHUNK, tf)
        a = jnp.dot(x, w1_ref[:, lo:hi], preferred_element_type=F32)
        b = jnp.dot(x, w3_ref[:, lo:hi], preferred_element_type=F32)
        mid = (a * jax.nn.sigmoid(a) * b).astype(BF16)
        part = jnp.dot(mid, w2_ref[lo:hi, :], preferred_element_type=F32)
        total = part if total is None else total + part
    return total


def _ffn_body(h_ref, w1_ref, w3_ref, w2_ref, x_ref, o_ref):
    o_ref[...] = x_ref[...] + _swiglu(h_ref[...], w1_ref, w3_ref, w2_ref)


def _ffn(h2, w1, w3, w2, x1):
    t = h2.shape[0]
    tm = _pick(t, 512)
    resident = lambda shape: pl.BlockSpec(shape, lambda i: (0, 0), pipeline_mode=pl.Buffered(1))
    return pl.pallas_call(
        _ffn_body,
        grid=(t // tm,),
        in_specs=[pl.BlockSpec((tm, D_MODEL), lambda i: (i, 0)),
                  resident((D_MODEL, D_FF)), resident((D_MODEL, D_FF)), resident((D_FF, D_MODEL)),
                  pl.BlockSpec((tm, D_MODEL), lambda i: (i, 0))],
        out_specs=pl.BlockSpec((tm, D_MODEL), lambda i: (i, 0)),
        out_shape=jax.ShapeDtypeStruct((t, D_MODEL), F32),
        compiler_params=_params(("parallel",)),
        name="ffn",
    )(h2, w1, w3, w2, x1)


def _route_body(lg_ref, tri_ref, info_ref, cnt_ref, carry_ref):
    i = pl.program_id(0)

    @pl.when(i == 0)
    def _():
        carry_ref[...] = jnp.zeros(carry_ref.shape, F32)

    lg = lg_ref[...]
    lane = lax.broadcasted_iota(jnp.int32, lg.shape, 1)
    lg = jnp.where(lane < N_EXPERTS, lg, -jnp.inf)
    m1 = jnp.max(lg, axis=-1, keepdims=True)
    i1 = jnp.min(jnp.where(lg == m1, lane, LANES), axis=-1, keepdims=True)
    oh1 = lane == i1
    lg2 = jnp.where(oh1, -jnp.inf, lg)
    m2 = jnp.max(lg2, axis=-1, keepdims=True)
    i2 = jnp.min(jnp.where(lg2 == m2, lane, LANES), axis=-1, keepdims=True)
    oh2 = lane == i2
    e2 = jnp.exp(m2 - m1)
    g1 = 1.0 / (1.0 + e2)
    g2 = e2 / (1.0 + e2)
    oh = jnp.where(jnp.logical_or(oh1, oh2), 1.0, 0.0)
    rank = jnp.dot(tri_ref[...], oh.astype(BF16), preferred_element_type=F32) + carry_ref[...]
    r1 = jnp.sum(jnp.where(oh1, rank, 0.0), axis=-1, keepdims=True)
    r2 = jnp.sum(jnp.where(oh2, rank, 0.0), axis=-1, keepdims=True)
    carry_ref[...] += jnp.sum(oh, axis=0, keepdims=True)
    cnt_ref[...] = carry_ref[...]
    info = jnp.zeros(lg.shape, F32)
    for c, val in enumerate((i1.astype(F32), i2.astype(F32), r1, r2, g1, g2)):
        info = jnp.where(lane == c, val, info)
    info_ref[...] = info


def _route(logits):
    t = logits.shape[0]
    tm = _pick(t, 512)
    tri = (jnp.arange(tm)[:, None] > jnp.arange(tm)[None, :]).astype(BF16)
    return pl.pallas_call(
        _route_body,
        grid=(t // tm,),
        in_specs=[pl.BlockSpec((tm, LANES), lambda i: (i, 0)), pl.BlockSpec((tm, tm), lambda i: (0, 0))],
        out_specs=[pl.BlockSpec((tm, LANES), lambda i: (i, 0)), pl.BlockSpec((1, LANES), lambda i: (0, 0))],
        out_shape=[jax.ShapeDtypeStruct((t, LANES), F32), jax.ShapeDtypeStruct((1, LANES), F32)],
        scratch_shapes=[pltpu.VMEM((1, LANES), F32)],
        compiler_params=_params(("arbitrary",)),
        name="route",
    )(logits, tri)


def _moe_ffn_body(be_ref, bv_ref, x_ref, w1_ref, w3_ref, w2_ref, o_ref, acc_ref):
    i, f = pl.program_id(0), pl.program_id(1)
    last = pl.num_programs(1) - 1
    valid = bv_ref[i]

    @pl.when(valid > 0)
    def _():
        x = jnp.concatenate([x_ref[c] for c in range(SC_CHUNKS)], axis=1)
        x = jnp.where(lax.broadcasted_iota(jnp.int32, x.shape, 0) < valid, x, 0.0).astype(BF16)
        contrib = _swiglu(x, w1_ref, w3_ref, w2_ref)

        @pl.when(f == 0)
        def _():
            acc_ref[...] = contrib

        @pl.when(f > 0)
        def _():
            acc_ref[...] += contrib

        @pl.when(f == last)
        def _():
            for c in range(SC_CHUNKS):
                o_ref[c] = acc_ref[:, c * SC_COLS:(c + 1) * SC_COLS]

    @pl.when(jnp.logical_and(valid == 0, f == last))
    def _():
        o_ref[...] = jnp.zeros(o_ref.shape, o_ref.dtype)


def _moe_ffn(xs, blk_expert, blk_valid, w1, w3, w2, bm):
    rows = xs.shape[1]
    tf = D_FF_EXPERT // 2
    chunked = pl.BlockSpec((SC_CHUNKS, bm, SC_COLS), lambda i, f, be, bv: (0, i, 0))
    grid_spec = pltpu.PrefetchScalarGridSpec(
        num_scalar_prefetch=2,
        grid=(rows // bm, D_FF_EXPERT // tf),
        in_specs=[chunked,
                  pl.BlockSpec((None, D_MODEL, tf), lambda i, f, be, bv: (be[i], 0, f)),
                  pl.BlockSpec((None, D_MODEL, tf), lambda i, f, be, bv: (be[i], 0, f)),
                  pl.BlockSpec((None, tf, D_MODEL), lambda i, f, be, bv: (be[i], f, 0))],
        out_specs=chunked,
        scratch_shapes=[pltpu.VMEM((bm, D_MODEL), F32)],
    )
    return pl.pallas_call(
        _moe_ffn_body, grid_spec=grid_spec,
        out_shape=jax.ShapeDtypeStruct(xs.shape, F32),
        compiler_params=_params(("parallel", "arbitrary")),
        name="moe_ffn",
    )(blk_expert, blk_valid, xs, w1, w3, w2)


def _final_body(*refs, with_moe):
    if with_moe:
        x_ref, y0_ref, y1_ref, info_ref, g_ref, o_ref = refs
        info = info_ref[...]
        y0 = jnp.concatenate([y0_ref[c] for c in range(SC_CHUNKS)], axis=1)
        y1 = jnp.concatenate([y1_ref[c] for c in range(SC_CHUNKS)], axis=1)
        x = x_ref[...] + (info[:, 4:5] * y0 + info[:, 5:6] * y1)
    else:
        x_ref, g_ref, o_ref = refs
        x = x_ref[...]
    o_ref[...] = x * lax.rsqrt(jnp.mean(x * x, axis=-1, keepdims=True) + RMS_EPS) * g_ref[...]


def _final(x2d, g, moe=None):
    t = x2d.shape[0]
    tm = _pick(t, 512)
    row = lambda w: pl.BlockSpec((tm, w), lambda i: (i, 0))
    in_specs, args = [row(D_MODEL)], [x2d]
    if moe is not None:
        yg, info = moe
        slot = lambda k: pl.BlockSpec((None, SC_CHUNKS, tm, SC_COLS), lambda i: (k, 0, i, 0))
        in_specs += [slot(0), slot(1), row(LANES)]
        args += [yg, yg, info]
    in_specs.append(pl.BlockSpec((1, D_MODEL), lambda i: (0, 0)))
    args.append(g)
    return pl.pallas_call(
        functools.partial(_final_body, with_moe=moe is not None),
        grid=(t // tm,), in_specs=in_specs, out_specs=row(D_MODEL),
        out_shape=jax.ShapeDtypeStruct((t, D_MODEL), F32),
        compiler_params=_params(("parallel",)),
        name="final",
    )(*args)


def _prep_attn_weights(w_in, w_out, w_uq, w_ukv):
    widths = (256, 256, 256, 256, 256, 256, MLA_Q_LORA, MLA_KV_LORA, MLA_ROPE, 256, 128, 128)
    offs = np.concatenate([[0], np.cumsum(widths)])
    qa, ka, va, qb, kb, vb, cq, ckv, kpe, qd, kd, vd = [w_in[:, offs[i]:offs[i + 1]] for i in range(12)]
    hperm = np.array([0, 2, 1, 3])
    qd = qd.reshape(D_MODEL, 4, SWA_HD)[:, hperm].reshape(D_MODEL, 256)
    pad = jnp.zeros((D_MODEL, LANES - MLA_ROPE), w_in.dtype)
    half = MLA_ROPE // 2
    kpe_sw = jnp.concatenate([kpe[:, half:], kpe[:, :half]], axis=1)
    w_main = jnp.concatenate([qa, ka, va, qb, kb, vb, qd, kd, vd, cq, ckv, kpe, pad, kpe_sw, pad], axis=1)

    uq = w_uq.reshape(MLA_Q_LORA, MLA_HEADS, MLA_NOPE + MLA_ROPE)
    z = lambda n: jnp.zeros((MLA_Q_LORA, n), w_uq.dtype)
    mains, sws = [], []
    for hd in range(MLA_HEADS):
        nope, rope = uq[:, hd, :MLA_NOPE], uq[:, hd, MLA_NOPE:]
        nope128 = jnp.concatenate([nope, z(64)] if hd % 2 == 0 else [z(64), nope], axis=1)
        mains.append(jnp.concatenate([rope, z(LANES - MLA_ROPE), nope128], axis=1))
        sws.append(jnp.concatenate([rope[:, half:], rope[:, :half], z(LANES - MLA_ROPE)], axis=1))
    wq_big = jnp.concatenate(mains + sws, axis=1)

    ukv = w_ukv.reshape(MLA_KV_LORA, MLA_HEADS, MLA_NOPE + MLA_V)
    wkv = jnp.concatenate([ukv[:, :, :MLA_NOPE].reshape(MLA_KV_LORA, 256),
                           ukv[:, :, MLA_NOPE:].reshape(MLA_KV_LORA, 256)], axis=1)

    wo = w_out.reshape(4, 256, D_MODEL)
    wo_d = wo[3].reshape(4, SWA_HD, D_MODEL)[hperm].reshape(256, D_MODEL)
    w4 = jnp.stack([wo[0], wo[1], wo[2], wo_d])
    return w_main.astype(BF16), wq_big.astype(BF16), wkv.astype(BF16), w4.astype(BF16)


def _rope_tables(s):
    inv = ROPE_THETA ** (-jnp.arange(0, MLA_ROPE, 2, dtype=F32) / MLA_ROPE)
    ang = jnp.arange(s, dtype=F32)[:, None] * inv[None, :]
    cos, sin = jnp.cos(ang), jnp.sin(ang)
    z = jnp.zeros((s, LANES - MLA_ROPE), F32)
    cos128 = jnp.concatenate([cos, cos, z], axis=1)
    sin128 = jnp.concatenate([-sin, sin, z], axis=1)
    c_mla = (MLA_NOPE + MLA_ROPE) ** -0.5 * LOG2E
    return cos128 * c_mla, sin128 * c_mla, cos128, sin128


def _col_scale():
    ca = DIFF_HD ** -0.5 * LOG2E
    cb = DIL_HD ** -0.5 * LOG2E
    cd = SWA_HD ** -0.5 * LOG2E
    v = np.ones((1, N_SCALED), np.float32)
    v[0, 0:256] = ca
    v[0, 768:1024] = cb
    v[0, 1536:1792] = cd
    return jnp.asarray(v)


def _moe_layer(h2, logits, w1, w3, w2, bm):
    t = logits.shape[0]
    info, counts = _route(logits)
    cnt = counts[0, :N_EXPERTS].astype(jnp.int32)
    padded = (cnt + bm - 1) // bm * bm
    pends = jnp.cumsum(padded)
    pstarts = pends - padded
    e12 = info[:, 0:2].astype(jnp.int32)
    pos = pstarts[e12] + info[:, 2:4].astype(jnp.int32)
    nb = (t * TOP_K) // bm + N_EXPERTS
    rows = nb * bm
    blk_start = jnp.arange(nb, dtype=jnp.int32) * bm
    blk_expert = jnp.minimum(jnp.sum((pends[None, :] <= blk_start[:, None]).astype(jnp.int32), axis=1),
                             N_EXPERTS - 1)
    blk_valid = jnp.clip(pstarts[blk_expert] + cnt[blk_expert] - blk_start, 0, bm).astype(jnp.int32)
    sub = (jnp.arange(SC_CHUNKS, dtype=jnp.int32) * rows)[None, :, None] + pos.T[:, None, :]
    xs = _sc_scatter_rows(h2.reshape(SC_CHUNKS * t, SC_COLS),
                          [sub[k].reshape(1, SC_CHUNKS * t) for k in range(TOP_K)], SC_CHUNKS * rows)
    ys = _moe_ffn(xs.reshape(SC_CHUNKS, rows, SC_COLS), blk_expert.astype(jnp.int32), blk_valid, w1, w3, w2, bm)
    yg = _sc_gather_rows(ys.reshape(SC_CHUNKS * rows, SC_COLS), sub.reshape(1, TOP_K * SC_CHUNKS * t))
    return yg.reshape(TOP_K, SC_CHUNKS, t, SC_COLS), info


SC_WINDOW = 128
SC_COLS = 256
SC_CHUNKS = D_MODEL // SC_COLS


def _sc_mesh():
    return plsc.VectorSubcoreMesh(core_axis_name="c", subcore_axis_name="s")


def _sc_scatter_rows(x, idx_list, rows):
    t, d = x.shape
    n = len(idx_list)

    @functools.partial(pl.kernel, out_type=jax.ShapeDtypeStruct((rows, d), x.dtype), mesh=_sc_mesh(),
                       scratch_types=[])
    def scatter_kernel(x_hbm, *rest):
        idx_hbms, o_hbm = rest[:n], rest[n]

        def body(x_vmem, *idx_vmems):
            for iv in idx_vmems:
                pltpu.sync_copy(x_vmem, o_hbm.at[iv.at[0]])

        pltpu.emit_pipeline(
            body, grid=(t // SC_WINDOW,),
            in_specs=[pl.BlockSpec((SC_WINDOW, d), lambda i: (i, 0))]
            + [pl.BlockSpec((1, SC_WINDOW), lambda i: (0, i))] * n,
            out_specs=[], core_axis_name=("c", "s"), dimension_semantics=(pltpu.PARALLEL,),
        )(x_hbm, *idx_hbms)

    return scatter_kernel(x, *idx_list)


def _sc_gather_rows(y, idx):
    m = idx.shape[1]
    d = y.shape[1]

    @functools.partial(pl.kernel, out_type=jax.ShapeDtypeStruct((m, d), y.dtype), mesh=_sc_mesh(),
                       scratch_types=[])
    def gather_kernel(y_hbm, idx_hbm, o_hbm):
        def body(idx_vmem, o_vmem):
            pltpu.sync_copy(y_hbm.at[idx_vmem.at[0]], o_vmem)

        pltpu.emit_pipeline(
            body, grid=(m // SC_WINDOW,),
            in_specs=[pl.BlockSpec((1, SC_WINDOW), lambda i: (0, i))],
            out_specs=[pl.BlockSpec((SC_WINDOW, d), lambda i: (i, 0))],
            core_axis_name=("c", "s"), dimension_semantics=(pltpu.PARALLEL,),
        )(idx_hbm, o_hbm)

    return gather_kernel(y, idx)


def kernel(x, attn_norm, w_in, w_out, diff_lambda, diff_subln, mla_q_norm, mla_w_uq, mla_kv_norm, mla_w_ukv,
           swa_sinks, ffn_norm, ffn_w1, ffn_w3, ffn_w2, moe_router, moe_w1, moe_w3, moe_w2, final_norm):
    bsz, seq, _ = x.shape
    t = bsz * seq
    depth = w_in.shape[0]
    slopes_a, slopes_b, slopes_d = _alibi_slopes()
    tq_dense = _pick(seq, 512)
    tq_band = _pick(seq, 256)

    tabs = _rope_tables(seq)
    colscale = _col_scale()
    di = jnp.arange(tq_dense)
    causal_mask_t = jnp.where(di[:, None] <= di[None, :], 0.0, NEG_INF).astype(F32)
    tq_diff = tq_dense
    alibi_kb = _alibi_key_columns(slopes_a * LOG2E, tq_diff)
    tq_dil = _pick(seq, 512)
    n_off_dil = min((max(w for w, _ in DIL_PATTERNS) + tq_dil - 1) // tq_dil + 1, seq // tq_dil)
    n_off_swa = min((SWA_WINDOW - 1 + tq_band - 1) // tq_band + 1, seq // tq_band)
    dil_bias = _band_bias(slopes_b, tq_dil, n_off_dil, _dil_mult).reshape(2, 2, n_off_dil, tq_dil, tq_dil)
    swa_bias = _band_bias(slopes_d[np.array([0, 2, 1, 3])], tq_band, n_off_swa, _swa_mult
                          ).reshape(2, 2, n_off_swa, tq_band, tq_band)
    slopes_a_l2 = jnp.asarray(slopes_a * LOG2E, F32)
    dil_m0 = jnp.full((4,), NEG_INF, F32)

    x2d = x.reshape(t, D_MODEL)
    for l in range(depth):
        w_main, wq_big, wkv, w4 = _prep_attn_weights(w_in[l], w_out[l], mla_w_uq[l], mla_w_ukv[l])
        a, avt, b, bvt, d, dvt, cq, ck, cvt = _proj(x2d, attn_norm[l][None], w_main, colscale, mla_q_norm[l][None],
                                                    wq_big, mla_kv_norm[l][None], wkv, tabs, seq)
        a3, b3, d3 = (v.reshape(bsz, seq, -1) for v in (a, b, d))
        sub128 = jnp.concatenate([diff_subln[l], diff_subln[l]])[None]
        o_a = _diff_attention(a3, avt, slopes_a_l2, causal_mask_t[:tq_diff, :tq_diff], alibi_kb, diff_lambda[l],
                              sub128, l, tq_diff)
        o_b = _band_attention(b3, bvt, lambda p: 2 + p, lambda p: p, dil_bias, dil_m0, 0.0, tq_dil, "dil_attn")
        o_c = _mla_attention(cq.reshape(bsz, seq, -1), ck.reshape(bsz, seq, -1), cvt, causal_mask_t, tq_dense)
        swa_m0 = (swa_sinks[l].astype(F32) * LOG2E)[np.array([0, 2, 1, 3])]
        o_d = _band_attention(d3, dvt, lambda p: 2, lambda p: 0, swa_bias, swa_m0, 1.0, tq_band, "swa_attn")
        outs = [v.reshape(t, 256) for v in (o_a, o_b, o_c, o_d)]
        j = l // 2
        if l % 2 == 0:
            x1, h2 = _outproj(*outs, w4, x2d, ffn_norm[l][None])
            x2d = _ffn(h2, ffn_w1[j].astype(BF16), ffn_w3[j].astype(BF16), ffn_w2[j].astype(BF16), x1)
            moe = None
        else:
            r = jnp.pad(moe_router[j], ((0, 0), (0, LANES - N_EXPERTS)))
            r_hi, r_lo = _split_bf16(r, 2)
            x1, h2, logits = _outproj(*outs, w4, x2d, ffn_norm[l][None], (r_hi, r_lo))
            bm = _pick(t, 1024)
            moe = _moe_layer(h2, logits, moe_w1[j].astype(BF16), moe_w3[j].astype(BF16),
                             moe_w2[j].astype(BF16), bm)
            x2d = x1
        if l < depth - 1 and moe is not None:
            raise NotImplementedError("MoE combine is fused into the final norm; MoE layer must be last")
    return _final(x2d, final_norm[None], moe).reshape(bsz, seq, D_MODEL)
```

```python
import functools
import math

import numpy as np
import jax
import jax.numpy as jnp
from jax import lax
from jax.experimental import pallas as pl
from jax.experimental.pallas import tpu as pltpu
from jax.experimental.pallas import tpu_sc as plsc

D_MODEL = 1024
DIFF_HEADS, DIFF_HD = 4, 32
DIL_HEADS, DIL_HD = 4, 64
DIL_PATTERNS = ((128, 1), (512, 4), (2048, 16))
MLA_HEADS, MLA_Q_LORA, MLA_KV_LORA, MLA_NOPE, MLA_ROPE, MLA_V = 4, 384, 128, 64, 32, 64
ROPE_THETA = 10000.0
SWA_HEADS, SWA_KV_HEADS, SWA_HD, SWA_WINDOW = 4, 2, 64, 128
D_FF = 2816
N_EXPERTS, TOP_K, D_FF_EXPERT = 8, 2, 3584
RMS_EPS = 1e-6
NEG_INF = -1e30
N_ALIBI = DIFF_HEADS + DIL_HEADS + SWA_HEADS

LOG2E = 1.4426950408889634
LANES = 128
HEAD_LANES = 64
VMEM_LIMIT = 56 * 1024 * 1024

F32 = jnp.float32
BF16 = jnp.bfloat16

_NT = (((1,), (1,)), ((), ()))


def _params(sem, vmem=VMEM_LIMIT):
    return pltpu.CompilerParams(dimension_semantics=sem, vmem_limit_bytes=vmem)


def _alibi_slopes():
    s = 2.0 ** (-8.0 * (np.arange(N_ALIBI) + 1) / N_ALIBI)
    return s[0::3], s[1::3], s[2::3]


def _pick(n, pref):
    t = min(pref, n)
    while n % t:
        t //= 2
    return t


N_SCALED = 2048
C_CQ, C_CKV, C_KPE, C_KPE_SW, N_MAIN = 2048, 2432, 2560, 2688, 2816


ONES_ROWS = 16
VT_ROWS = HEAD_LANES + ONES_ROWS


def _vt_with_ones(v):
    vt = v.T
    ones = jnp.ones((ONES_ROWS, v.shape[0]), F32)
    parts = []
    for hd in range(v.shape[1] // HEAD_LANES):
        parts += [vt[hd * HEAD_LANES:(hd + 1) * HEAD_LANES], ones]
    return jnp.concatenate(parts, axis=0).astype(BF16)


def _proj_body(x_ref, g_ref, w_ref, cs_ref, qn_ref, wq_ref, kn_ref, wkv_ref,
               cosq_ref, sinq_ref, cosk_ref, sink_ref,
               a_ref, avt_ref, b_ref, bvt_ref, d_ref, dvt_ref, cq_ref, ck_ref, cvt_ref):
    x = x_ref[...]
    ms = jnp.mean(x * x, axis=-1, keepdims=True)
    h = (x * lax.rsqrt(ms + RMS_EPS) * g_ref[...]).astype(BF16)
    acc = jnp.dot(h, w_ref[...], preferred_element_type=F32)
    sc = acc[:, :N_SCALED] * cs_ref[...]
    a_ref[...] = sc[:, 0:512].astype(BF16)
    avt_ref[...] = _vt_with_ones(sc[:, 512:768])
    b_ref[...] = sc[:, 768:1280].astype(BF16)
    bvt_ref[...] = _vt_with_ones(sc[:, 1280:1536])
    d_ref[...] = sc[:, 1536:1920].astype(BF16)
    dvt_ref[...] = _vt_with_ones(sc[:, 1920:2048])

    cq = acc[:, C_CQ:C_CKV]
    hq = (cq * lax.rsqrt(jnp.mean(cq * cq, axis=-1, keepdims=True) + RMS_EPS) * qn_ref[...]).astype(BF16)
    yq = jnp.dot(hq, wq_ref[...], preferred_element_type=F32)
    cosq, sinq = cosq_ref[...], sinq_ref[...]
    c_mla = (MLA_NOPE + MLA_ROPE) ** -0.5 * LOG2E
    for hd in range(MLA_HEADS):
        main = yq[:, hd * 256:(hd + 1) * 256]
        sw = yq[:, 1024 + hd * LANES:1024 + (hd + 1) * LANES]
        cq_ref[:, hd * 256:hd * 256 + LANES] = (main[:, :LANES] * cosq + sw * sinq).astype(BF16)
        cq_ref[:, hd * 256 + LANES:(hd + 1) * 256] = (main[:, LANES:] * c_mla).astype(BF16)

    ckv = acc[:, C_CKV:C_KPE]
    hk = (ckv * lax.rsqrt(jnp.mean(ckv * ckv, axis=-1, keepdims=True) + RMS_EPS) * kn_ref[...]).astype(BF16)
    kv = jnp.dot(hk, wkv_ref[...], preferred_element_type=F32)
    kr = (acc[:, C_KPE:C_KPE_SW] * cosk_ref[...] + acc[:, C_KPE_SW:N_MAIN] * sink_ref[...]).astype(BF16)
    for grp in range(2):
        ck_ref[:, grp * 256:grp * 256 + LANES] = kr
        ck_ref[:, grp * 256 + LANES:(grp + 1) * 256] = kv[:, grp * LANES:(grp + 1) * LANES].astype(BF16)
    cvt_ref[...] = _vt_with_ones(kv[:, 256:512])


def _proj(x2d, g, w_main, colscale, qn, wq_big, kn, wkv, tabs, seq):
    t = x2d.shape[0]
    tm = _pick(seq, 512)
    nsb = seq // tm
    full = lambda shape: pl.BlockSpec(shape, lambda i: (0,) * len(shape))
    tab = pl.BlockSpec((tm, LANES), lambda i: (i % nsb, 0))
    row = lambda w: pl.BlockSpec((tm, w), lambda i: (i, 0))
    vt = lambda w: pl.BlockSpec((None, w, tm), lambda i: (i // nsb, 0, i % nsb))
    row_sds = lambda w: jax.ShapeDtypeStruct((t, w), BF16)
    vt_sds = lambda w: jax.ShapeDtypeStruct((t // seq, w, seq), BF16)
    return pl.pallas_call(
        _proj_body,
        grid=(t // tm,),
        in_specs=[row(D_MODEL), full((1, D_MODEL)), full((D_MODEL, N_MAIN)), full((1, N_SCALED)),
                  full((1, MLA_Q_LORA)), full((MLA_Q_LORA, 1536)), full((1, MLA_KV_LORA)),
                  full((MLA_KV_LORA, 512)), tab, tab, tab, tab],
        out_specs=[row(512), vt(4 * VT_ROWS), row(512), vt(4 * VT_ROWS), row(384), vt(2 * VT_ROWS),
                   row(1024), row(512), vt(4 * VT_ROWS)],
        out_shape=[row_sds(512), vt_sds(4 * VT_ROWS), row_sds(512), vt_sds(4 * VT_ROWS), row_sds(384),
                   vt_sds(2 * VT_ROWS), row_sds(1024), row_sds(512), vt_sds(4 * VT_ROWS)],
        compiler_params=_params(("parallel",)),
        name="proj",
    )(x2d, g, w_main, colscale, qn, wq_big, kn, wkv, *tabs)


def _attn_pipeline(n_soft, n_steps, scores, shift, load_vt, m0, s_ref, acc_ref):
    def stage_a(t, ss, dst_ref, m_cur):
        m_next, alpha = [], []
        for i in range(n_soft):
            m_new = jnp.maximum(m_cur[i], jnp.max(ss[i], axis=0, keepdims=True) + shift(i, t))
            dst_ref[i] = ss[i]
            m_next.append(m_new)
            alpha.append(jnp.exp2(m_cur[i] - m_new))
        return tuple(m_next), tuple(alpha)

    def stage_b(t, src_ref, carry):
        m_cur, alpha = carry
        for i in range(n_soft):
            p = jnp.exp2(src_ref[i] - (m_cur[i] - shift(i, t)))
            acc_ref[i] = alpha[i] * acc_ref[i] + jnp.dot(load_vt(t, i), p.astype(BF16),
                                                         preferred_element_type=F32)

    def half(t, src_ref, dst_ref, carry):
        ss = [scores(i, t + 1, False) for i in range(n_soft)]
        stage_b(t, src_ref, carry)
        return stage_a(t + 1, ss, dst_ref, carry[0])

    def body(u, carry):
        carry = half(2 * u, s_ref[0], s_ref[1], carry)
        return half(2 * u + 1, s_ref[1], s_ref[0], carry)

    carry = stage_a(0, [scores(i, 0, True) for i in range(n_soft)], s_ref[0], m0)
    carry = lax.fori_loop(0, n_steps // 2, body, carry)

    @pl.when(n_steps % 2 == 1)
    def _():
        stage_b(n_steps, s_ref[1], half(n_steps - 1, s_ref[0], s_ref[1], carry))

    @pl.when(n_steps % 2 == 0)
    def _():
        stage_b(n_steps, s_ref[0], carry)


def _row_mask(shape, lo, hi):
    row = lax.broadcasted_iota(jnp.int32, shape, 0)
    return jnp.logical_and(row >= lo, row < hi)


def _lane_mask(shape, lo, hi):
    lane = lax.broadcasted_iota(jnp.int32, shape, len(shape) - 1)
    return jnp.logical_and(lane >= lo, lane < hi)


def _normalized(acc_ref, i):
    return acc_ref[i, 0:HEAD_LANES, :] / acc_ref[i, HEAD_LANES:HEAD_LANES + 1, :]


def _diff_body(sl_ref, q_ref, k_ref, vt_ref, mask_ref, kb_ref, lam_ref, sub_ref, o_ref,
               qs_ref, s0_ref, s1_ref, acc_ref, *, tq, lam_init):
    pair, qi = pl.program_id(1), pl.program_id(2)
    tk = tq
    q = q_ref[...].astype(F32)
    for hd in range(2):
        sel = jnp.where(_lane_mask(q.shape, N_BIAS_SPLIT * hd, N_BIAS_SPLIT * (hd + 1)), 1.0, 0.0).astype(BF16)
        for mp in range(2):
            lo = hd * HEAD_LANES + mp * DIFF_HD
            qs_ref[2 * hd + mp, :, 0:LANES] = jnp.where(_lane_mask(q.shape, lo, lo + DIFF_HD), q, 0.0).astype(BF16)
            qs_ref[2 * hd + mp, :, LANES:2 * LANES] = sel
    acc_ref[...] = jnp.zeros(acc_ref.shape, F32)

    def scores(i, t, first):
        off = pl.multiple_of((qi - t) * tk, tk)
        k = jnp.concatenate([k_ref[pl.ds(off, tk), :], kb_ref[...]], axis=1)
        s = lax.dot_general(k, qs_ref[i], _NT, preferred_element_type=F32)
        return s + mask_ref[...] if first else s

    def shift(i, t):
        return sl_ref[2 * pair + i // 2] * (-(t * tk)).astype(F32) if not isinstance(t, int) else 0.0

    def load_vt(t, i):
        return vt_ref[pl.ds((i // 2) * VT_ROWS, VT_ROWS), pl.ds(pl.multiple_of((qi - t) * tk, tk), tk)]

    m0 = tuple(jnp.full((1, tq), NEG_INF, F32) for _ in range(4))
    _attn_pipeline(4, qi, scores, shift, load_vt, m0, (s0_ref, s1_ref), acc_ref)

    lp = lam_ref[...]
    lam = (jnp.exp(jnp.sum(lp[0:1, :] * lp[1:2, :], axis=-1, keepdims=True))
           - jnp.exp(jnp.sum(lp[2:3, :] * lp[3:4, :], axis=-1, keepdims=True)) + lam_init)
    o_h = [_normalized(acc_ref, 2 * hd) - lam * _normalized(acc_ref, 2 * hd + 1) for hd in range(2)]
    o = jnp.concatenate(o_h, axis=0).T
    in_h0 = _lane_mask((tq, LANES), 0, HEAD_LANES)
    sq = o * o
    ms0 = jnp.sum(jnp.where(in_h0, sq, 0.0), axis=-1, keepdims=True)
    ms1 = jnp.sum(jnp.where(in_h0, 0.0, sq), axis=-1, keepdims=True)
    ms = jnp.where(in_h0, ms0, ms1) * (1.0 / HEAD_LANES)
    y = o * lax.rsqrt(ms + RMS_EPS) * sub_ref[...]
    o_ref[...] = (y * (1.0 - lam_init)).astype(o_ref.dtype)


N_BIAS_SPLIT = 3


def _alibi_key_columns(slopes_l2, tk):
    dj = jnp.arange(tk, dtype=F32)[None, :, None]
    val = jnp.asarray(slopes_l2, F32).reshape(-1, 1, 2) * dj
    cols = jnp.stack(_split_bf16(val, N_BIAS_SPLIT), axis=-1).reshape(val.shape[0], tk, 2 * N_BIAS_SPLIT)
    return jnp.pad(cols, ((0, 0), (0, 0), (0, LANES - 2 * N_BIAS_SPLIT)))


def _split_bf16(x, n):
    pieces = []
    for _ in range(n):
        bits = lax.bitcast_convert_type(x, jnp.uint32) & jnp.uint32(0xFFFF0000)
        head = lax.bitcast_convert_type(bits, F32)
        pieces.append(head.astype(BF16))
        x = x - head
    return pieces


def _diff_attention(a3, avt, slopes_l2, mask_t, kb, lam_p, sub128, layer_idx, tq):
    b, s, _ = a3.shape
    nq = s // tq
    lam_init = 0.8 - 0.6 * math.exp(-0.3 * layer_idx)
    body = functools.partial(_diff_body, tq=tq, lam_init=lam_init)
    grid_spec = pltpu.PrefetchScalarGridSpec(
        num_scalar_prefetch=1,
        grid=(b, 2, nq),
        in_specs=[
            pl.BlockSpec((None, tq, LANES), lambda bi, p, qi, sl: (bi, qi, p)),
            pl.BlockSpec((None, s, LANES), lambda bi, p, qi, sl: (bi, 0, 2 + p)),
            pl.BlockSpec((None, 2 * VT_ROWS, s), lambda bi, p, qi, sl: (bi, p, 0)),
            pl.BlockSpec((tq, tq), lambda bi, p, qi, sl: (0, 0)),
            pl.BlockSpec((None, tq, LANES), lambda bi, p, qi, sl: (p, 0, 0)),
            pl.BlockSpec((4, DIFF_HD), lambda bi, p, qi, sl: (0, 0)),
            pl.BlockSpec((1, LANES), lambda bi, p, qi, sl: (0, 0)),
        ],
        out_specs=pl.BlockSpec((None, tq, LANES), lambda bi, p, qi, sl: (bi, qi, p)),
        scratch_shapes=[pltpu.VMEM((4, tq, 2 * LANES), BF16), pltpu.VMEM((4, tq, tq), F32),
                        pltpu.VMEM((4, tq, tq), F32), pltpu.VMEM((4, VT_ROWS, tq), F32)],
    )
    return pl.pallas_call(
        body, grid_spec=grid_spec,
        out_shape=jax.ShapeDtypeStruct((b, s, 256), BF16),
        compiler_params=_params(("parallel", "parallel", "arbitrary")),
        name="diff_attn",
    )(slopes_l2, a3, a3, avt, mask_t, kb, lam_p, sub128)


def _mla_body(q0_ref, q1_ref, k_ref, vt_ref, mask_ref, o_ref, s0_ref, s1_ref, acc_ref, *, tq):
    qi = pl.program_id(2)
    tk = tq
    acc_ref[...] = jnp.zeros(acc_ref.shape, F32)
    q_refs = (q0_ref, q1_ref)

    def scores(i, t, first):
        k = k_ref[pl.ds(pl.multiple_of((qi - t) * tk, tk), tk), :]
        s = lax.dot_general(k, q_refs[i][...], _NT, preferred_element_type=F32)
        return s + mask_ref[...] if first else s

    def load_vt(t, i):
        return vt_ref[pl.ds(i * VT_ROWS, VT_ROWS), pl.ds(pl.multiple_of((qi - t) * tk, tk), tk)]

    m0 = tuple(jnp.full((1, tq), NEG_INF, F32) for _ in range(2))
    _attn_pipeline(2, qi, scores, lambda i, t: 0.0, load_vt, m0, (s0_ref, s1_ref), acc_ref)
    o_t = jnp.concatenate([_normalized(acc_ref, 0), _normalized(acc_ref, 1)], axis=0)
    o_ref[...] = o_t.T.astype(o_ref.dtype)


def _mla_attention(cq3, ck3, cvt, mask_t, tq):
    b, s, _ = cq3.shape
    nq = s // tq
    return pl.pallas_call(
        functools.partial(_mla_body, tq=tq),
        grid=(b, 2, nq),
        in_specs=[
            pl.BlockSpec((None, tq, 256), lambda bi, g, qi: (bi, qi, 2 * g)),
            pl.BlockSpec((None, tq, 256), lambda bi, g, qi: (bi, qi, 2 * g + 1)),
            pl.BlockSpec((None, s, 256), lambda bi, g, qi: (bi, 0, g)),
            pl.BlockSpec((None, 2 * VT_ROWS, s), lambda bi, g, qi: (bi, g, 0)),
            pl.BlockSpec((tq, tq), lambda bi, g, qi: (0, 0)),
        ],
        out_specs=pl.BlockSpec((None, tq, LANES), lambda bi, g, qi: (bi, qi, g)),
        out_shape=jax.ShapeDtypeStruct((b, s, 256), BF16),
        scratch_shapes=[pltpu.VMEM((2, tq, tq), F32), pltpu.VMEM((2, tq, tq), F32),
                        pltpu.VMEM((2, VT_ROWS, tq), F32)],
        compiler_params=_params(("parallel", "parallel", "arbitrary")),
        name="mla_attn",
    )(cq3, cq3, ck3, cvt, mask_t)


def _band_body(m0_ref, q_ref, k_ref, vt_ref, bias_ref, o_ref, qs_ref, s0_ref, s1_ref, acc_ref,
               *, tq, n_off, l_init, n_slabs):
    slab0, qi = pl.program_id(1) * n_slabs, pl.program_id(2)
    tk = tq
    n_soft = 2 * n_slabs
    for sl in range(n_slabs):
        q = q_ref[:, sl * LANES:(sl + 1) * LANES].astype(F32)
        for g in range(2):
            qs_ref[2 * sl + g] = jnp.where(_lane_mask(q.shape, g * HEAD_LANES, (g + 1) * HEAD_LANES), q, 0.0
                                           ).astype(BF16)
    for i in range(n_soft):
        acc_ref[i] = jnp.where(_row_mask((VT_ROWS, tq), 0, HEAD_LANES), 0.0, l_init).astype(F32)

    def scores(i, t, first):
        k = k_ref[pl.ds(pl.multiple_of((qi - t) * tk, tk), tk), :]
        return lax.dot_general(k, qs_ref[i], _NT, preferred_element_type=F32) + bias_ref[i // 2, i % 2, t]

    def load_vt(t, i):
        return vt_ref[pl.ds((i % 2) * VT_ROWS, VT_ROWS), pl.ds(pl.multiple_of((qi - t) * tk, tk), tk)]

    m0 = tuple(jnp.full((1, tq), m0_ref[2 * slab0 + i], F32) for i in range(n_soft))
    _attn_pipeline(n_soft, jnp.minimum(qi, n_off - 1), scores, lambda i, t: 0.0, load_vt, m0,
                   (s0_ref, s1_ref), acc_ref)
    o_t = jnp.concatenate([_normalized(acc_ref, i) for i in range(n_soft)], axis=0)
    o_ref[...] = o_t.T.astype(o_ref.dtype)


def _band_attention(x3, xvt, k_blk_of, vt_blk_of, bias, m0, l_init, tq, n_slabs, name):
    b, s, _ = x3.shape
    nq = s // tq
    n_off = bias.shape[2]
    n_soft = 2 * n_slabs
    grid_spec = pltpu.PrefetchScalarGridSpec(
        num_scalar_prefetch=1,
        grid=(b, 2 // n_slabs, nq),
        in_specs=[
            pl.BlockSpec((None, tq, n_slabs * LANES), lambda bi, p, qi, m0r: (bi, qi, p)),
            pl.BlockSpec((None, s, LANES), lambda bi, p, qi, m0r: (bi, 0, k_blk_of(p))),
            pl.BlockSpec((None, 2 * VT_ROWS, s), lambda bi, p, qi, m0r: (bi, vt_blk_of(p), 0)),
            pl.BlockSpec((n_slabs, 2, n_off, tq, tq), lambda bi, p, qi, m0r: (p, 0, 0, 0, 0)),
        ],
        out_specs=pl.BlockSpec((None, tq, n_slabs * LANES), lambda bi, p, qi, m0r: (bi, qi, p)),
        scratch_shapes=[pltpu.VMEM((n_soft, tq, LANES), BF16), pltpu.VMEM((n_soft, tq, tq), F32),
                        pltpu.VMEM((n_soft, tq, tq), F32), pltpu.VMEM((n_soft, VT_ROWS, tq), F32)],
    )
    return pl.pallas_call(
        functools.partial(_band_body, tq=tq, n_off=n_off, l_init=l_init, n_slabs=n_slabs),
        grid_spec=grid_spec,
        out_shape=jax.ShapeDtypeStruct((b, s, 256), BF16),
        compiler_params=_params(("parallel", "parallel", "arbitrary")),
        name=name,
    )(m0, x3, x3, xvt, bias)


def _band_bias(slopes, tq, n_off, mult_fn):
    di = jnp.arange(tq, dtype=jnp.int32)[None, :]
    dj = jnp.arange(tq, dtype=jnp.int32)[:, None]
    off = jnp.arange(n_off, dtype=jnp.int32)[:, None, None]
    delta = off * tq + di - dj
    mult = mult_fn(delta)
    sl = jnp.asarray(slopes, F32)[:, None, None, None] * LOG2E
    val = -sl * delta.astype(F32)[None] + jnp.log2(jnp.maximum(mult, 1).astype(F32))[None]
    return jnp.where((mult > 0)[None], val, NEG_INF)


def _dil_mult(delta):
    m = jnp.zeros(delta.shape, jnp.int32)
    for window, d in DIL_PATTERNS:
        m = m + ((delta >= 0) & (delta <= window) & (delta % d == 0)).astype(jnp.int32)
    return m


def _swa_mult(delta):
    return ((delta >= 0) & (delta <= SWA_WINDOW - 1)).astype(jnp.int32)


def _outproj_body(*refs, with_router):
    if with_router:
        (oa, ob, oc, od, w_ref, x_ref, g_ref, rh_ref, rl_ref, x1_ref, h2_ref, lg_ref) = refs
    else:
        (oa, ob, oc, od, w_ref, x_ref, g_ref, x1_ref, h2_ref) = refs
    acc = x_ref[...]
    for i, o in enumerate((oa, ob, oc, od)):
        acc = acc + jnp.dot(o[...], w_ref[i], preferred_element_type=F32)
    x1_ref[...] = acc
    h2 = acc * lax.rsqrt(jnp.mean(acc * acc, axis=-1, keepdims=True) + RMS_EPS) * g_ref[...]
    hi = h2.astype(BF16)
    if with_router:
        for c in range(SC_CHUNKS):
            h2_ref[c] = h2[:, c * SC_COLS:(c + 1) * SC_COLS]
        lo = (h2 - hi.astype(F32)).astype(BF16)
        lg_ref[...] = (jnp.dot(hi, rh_ref[...], preferred_element_type=F32)
                       + (jnp.dot(hi, rl_ref[...], preferred_element_type=F32)
                          + jnp.dot(lo, rh_ref[...], preferred_element_type=F32)))
    else:
        h2_ref[...] = hi


def _outproj(oa, ob, oc, od, w4, x2d, g, router_hl=None):
    t = x2d.shape[0]
    tm = _pick(t, 512)
    with_router = router_hl is not None
    full = lambda shape: pl.BlockSpec(shape, lambda i: (0,) * len(shape))
    row = lambda w: pl.BlockSpec((tm, w), lambda i: (i, 0))
    in_specs = [row(256)] * 4 + [full((4, 256, D_MODEL)), row(D_MODEL), full((1, D_MODEL))]
    args = [oa, ob, oc, od, w4, x2d, g]
    out_specs = [row(D_MODEL), row(D_MODEL)]
    out_shape = [jax.ShapeDtypeStruct((t, D_MODEL), F32), jax.ShapeDtypeStruct((t, D_MODEL), BF16)]
    if with_router:
        out_specs[1] = pl.BlockSpec((SC_CHUNKS, tm, SC_COLS), lambda i: (0, i, 0))
        out_shape[1] = jax.ShapeDtypeStruct((SC_CHUNKS, t, SC_COLS), F32)
        in_specs += [full((D_MODEL, LANES))] * 2
        args += list(router_hl)
        out_specs.append(row(LANES))
        out_shape.append(jax.ShapeDtypeStruct((t, LANES), F32))
    return pl.pallas_call(
        functools.partial(_outproj_body, with_router=with_router),
        grid=(t // tm,), in_specs=in_specs, out_specs=out_specs, out_shape=out_shape,
        compiler_params=_params(("parallel",)),
        name="outproj",
    )(*args)


FF_CHUNK = 512


def _swiglu(x, w1_ref, w3_ref, w2_ref):
    tf = w1_ref.shape[1]
    total = None
    for lo in range(0, tf, FF_CHUNK):
        hi = min(lo + FF_CHUNK, tf)
        a = jnp.dot(x, w1_ref[:, lo:hi], preferred_element_type=F32)
        b = jnp.dot(x, w3_ref[:, lo:hi], preferred_element_type=F32)
        mid = (a * jax.nn.sigmoid(a) * b).astype(BF16)
        part = jnp.dot(mid, w2_ref[lo:hi, :], preferred_element_type=F32)
        total = part if total is None else total + part
    return total


def _ffn_body(h_ref, w1_ref, w3_ref, w2_ref, x_ref, o_ref):
    o_ref[...] = x_ref[...] + _swiglu(h_ref[...], w1_ref, w3_ref, w2_ref)


def _ffn(h2, w1, w3, w2, x1):
    t = h2.shape[0]
    tm = _pick(t, 512)
    resident = lambda shape: pl.BlockSpec(shape, lambda i: (0, 0), pipeline_mode=pl.Buffered(1))
    return pl.pallas_call(
        _ffn_body,
        grid=(t // tm,),
        in_specs=[pl.BlockSpec((tm, D_MODEL), lambda i: (i, 0)),
                  resident((D_MODEL, D_FF)), resident((D_MODEL, D_FF)), resident((D_FF, D_MODEL)),
                  pl.BlockSpec((tm, D_MODEL), lambda i: (i, 0))],
        out_specs=pl.BlockSpec((tm, D_MODEL), lambda i: (i, 0)),
        out_shape=jax.ShapeDtypeStruct((t, D_MODEL), F32),
        compiler_params=_params(("parallel",)),
        name="ffn",
    )(h2, w1, w3, w2, x1)


def _route_body(lg_ref, tri_ref, info_ref, cnt_ref, carry_ref):
    i = pl.program_id(0)

    @pl.when(i == 0)
    def _():
        carry_ref[...] = jnp.zeros(carry_ref.shape, F32)

    lg = lg_ref[...]
    lane = lax.broadcasted_iota(jnp.int32, lg.shape, 1)
    lg = jnp.where(lane < N_EXPERTS, lg, -jnp.inf)
    m1 = jnp.max(lg, axis=-1, keepdims=True)
    i1 = jnp.min(jnp.where(lg == m1, lane, LANES), axis=-1, keepdims=True)
    oh1 = lane == i1
    lg2 = jnp.where(oh1, -jnp.inf, lg)
    m2 = jnp.max(lg2, axis=-1, keepdims=True)
    i2 = jnp.min(jnp.where(lg2 == m2, lane, LANES), axis=-1, keepdims=True)
    oh2 = lane == i2
    e2 = jnp.exp(m2 - m1)
    g1 = 1.0 / (1.0 + e2)
    g2 = e2 / (1.0 + e2)
    oh = jnp.where(jnp.logical_or(oh1, oh2), 1.0, 0.0)
    rank = jnp.dot(tri_ref[...], oh.astype(BF16), preferred_element_type=F32) + carry_ref[...]
    r1 = jnp.sum(jnp.where(oh1, rank, 0.0), axis=-1, keepdims=True)
    r2 = jnp.sum(jnp.where(oh2, rank, 0.0), axis=-1, keepdims=True)
    carry_ref[...] += jnp.sum(oh, axis=0, keepdims=True)
    cnt_ref[...] = carry_ref[...]
    info = jnp.zeros(lg.shape, F32)
    for c, val in enumerate((i1.astype(F32), i2.astype(F32), r1, r2, g1, g2)):
        info = jnp.where(lane == c, val, info)
    info_ref[...] = info


def _route(logits):
    t = logits.shape[0]
    tm = _pick(t, 512)
    tri = (jnp.arange(tm)[:, None] > jnp.arange(tm)[None, :]).astype(BF16)
    return pl.pallas_call(
        _route_body,
        grid=(t // tm,),
        in_specs=[pl.BlockSpec((tm, LANES), lambda i: (i, 0)), pl.BlockSpec((tm, tm), lambda i: (0, 0))],
        out_specs=[pl.BlockSpec((tm, LANES), lambda i: (i, 0)), pl.BlockSpec((1, LANES), lambda i: (0, 0))],
        out_shape=[jax.ShapeDtypeStruct((t, LANES), F32), jax.ShapeDtypeStruct((1, LANES), F32)],
        scratch_shapes=[pltpu.VMEM((1, LANES), F32)],
        compiler_params=_params(("arbitrary",)),
        name="route",
    )(logits, tri)


def _moe_ffn_body(be_ref, bv_ref, x_ref, w1_ref, w3_ref, w2_ref, o_ref, acc_ref):
    i, f = pl.program_id(0), pl.program_id(1)
    last = pl.num_programs(1) - 1
    valid = bv_ref[i]

    @pl.when(valid > 0)
    def _():
        x = jnp.concatenate([x_ref[c] for c in range(SC_CHUNKS)], axis=1)
        x = jnp.where(lax.broadcasted_iota(jnp.int32, x.shape, 0) < valid, x, 0.0).astype(BF16)
        contrib = _swiglu(x, w1_ref, w3_ref, w2_ref)

        @pl.when(f == 0)
        def _():
            acc_ref[...] = contrib

        @pl.when(f > 0)
        def _():
            acc_ref[...] += contrib

        @pl.when(f == last)
        def _():
            for c in range(SC_CHUNKS):
                o_ref[c] = acc_ref[:, c * SC_COLS:(c + 1) * SC_COLS]

    @pl.when(jnp.logical_and(valid == 0, f == last))
    def _():
        o_ref[...] = jnp.zeros(o_ref.shape, o_ref.dtype)


def _moe_ffn(xs, blk_expert, blk_valid, w1, w3, w2, bm):
    rows = xs.shape[1]
    tf = D_FF_EXPERT // 2
    chunked = pl.BlockSpec((SC_CHUNKS, bm, SC_COLS), lambda i, f, be, bv: (0, i, 0))
    grid_spec = pltpu.PrefetchScalarGridSpec(
        num_scalar_prefetch=2,
        grid=(rows // bm, D_FF_EXPERT // tf),
        in_specs=[chunked,
                  pl.BlockSpec((None, D_MODEL, tf), lambda i, f, be, bv: (be[i], 0, f)),
                  pl.BlockSpec((None, D_MODEL, tf), lambda i, f, be, bv: (be[i], 0, f)),
                  pl.BlockSpec((None, tf, D_MODEL), lambda i, f, be, bv: (be[i], f, 0))],
        out_specs=chunked,
        scratch_shapes=[pltpu.VMEM((bm, D_MODEL), F32)],
    )
    return pl.pallas_call(
        _moe_ffn_body, grid_spec=grid_spec,
        out_shape=jax.ShapeDtypeStruct(xs.shape, F32),
        compiler_params=_params(("parallel", "arbitrary")),
        name="moe_ffn",
    )(blk_expert, blk_valid, xs, w1, w3, w2)


def _final_body(*refs, with_moe):
    if with_moe:
        x_ref, y0_ref, y1_ref, info_ref, g_ref, o_ref = refs
        info = info_ref[...]
        y0 = jnp.concatenate([y0_ref[c] for c in range(SC_CHUNKS)], axis=1)
        y1 = jnp.concatenate([y1_ref[c] for c in range(SC_CHUNKS)], axis=1)
        x = x_ref[...] + (info[:, 4:5] * y0 + info[:, 5:6] * y1)
    else:
        x_ref, g_ref, o_ref = refs
        x = x_ref[...]
    o_ref[...] = x * lax.rsqrt(jnp.mean(x * x, axis=-1, keepdims=True) + RMS_EPS) * g_ref[...]


def _final(x2d, g, moe=None):
    t = x2d.shape[0]
    tm = _pick(t, 512)
    row = lambda w: pl.BlockSpec((tm, w), lambda i: (i, 0))
    in_specs, args = [row(D_MODEL)], [x2d]
    if moe is not None:
        yg, info = moe
        slot = lambda k: pl.BlockSpec((None, SC_CHUNKS, tm, SC_COLS), lambda i: (k, 0, i, 0))
        in_specs += [slot(0), slot(1), row(LANES)]
        args += [yg, yg, info]
    in_specs.append(pl.BlockSpec((1, D_MODEL), lambda i: (0, 0)))
    args.append(g)
    return pl.pallas_call(
        functools.partial(_final_body, with_moe=moe is not None),
        grid=(t // tm,), in_specs=in_specs, out_specs=row(D_MODEL),
        out_shape=jax.ShapeDtypeStruct((t, D_MODEL), F32),
        compiler_params=_params(("parallel",)),
        name="final",
    )(*args)


def _prep_attn_weights(w_in, w_out, w_uq, w_ukv):
    widths = (256, 256, 256, 256, 256, 256, MLA_Q_LORA, MLA_KV_LORA, MLA_ROPE, 256, 128, 128)
    offs = np.concatenate([[0], np.cumsum(widths)])
    qa, ka, va, qb, kb, vb, cq, ckv, kpe, qd, kd, vd = [w_in[:, offs[i]:offs[i + 1]] for i in range(12)]
    hperm = np.array([0, 2, 1, 3])
    qd = qd.reshape(D_MODEL, 4, SWA_HD)[:, hperm].reshape(D_MODEL, 256)
    pad = jnp.zeros((D_MODEL, LANES - MLA_ROPE), w_in.dtype)
    half = MLA_ROPE // 2
    kpe_sw = jnp.concatenate([kpe[:, half:], kpe[:, :half]], axis=1)
    w_main = jnp.concatenate([qa, ka, va, qb, kb, vb, qd, kd, vd, cq, ckv, kpe, pad, kpe_sw, pad], axis=1)

    uq = w_uq.reshape(MLA_Q_LORA, MLA_HEADS, MLA_NOPE + MLA_ROPE)
    z = lambda n: jnp.zeros((MLA_Q_LORA, n), w_uq.dtype)
    mains, sws = [], []
    for hd in range(MLA_HEADS):
        nope, rope = uq[:, hd, :MLA_NOPE], uq[:, hd, MLA_NOPE:]
        nope128 = jnp.concatenate([nope, z(64)] if hd % 2 == 0 else [z(64), nope], axis=1)
        mains.append(jnp.concatenate([rope, z(LANES - MLA_ROPE), nope128], axis=1))
        sws.append(jnp.concatenate([rope[:, half:], rope[:, :half], z(LANES - MLA_ROPE)], axis=1))
    wq_big = jnp.concatenate(mains + sws, axis=1)

    ukv = w_ukv.reshape(MLA_KV_LORA, MLA_HEADS, MLA_NOPE + MLA_V)
    wkv = jnp.concatenate([ukv[:, :, :MLA_NOPE].reshape(MLA_KV_LORA, 256),
                           ukv[:, :, MLA_NOPE:].reshape(MLA_KV_LORA, 256)], axis=1)

    wo = w_out.reshape(4, 256, D_MODEL)
    wo_d = wo[3].reshape(4, SWA_HD, D_MODEL)[hperm].reshape(256, D_MODEL)
    w4 = jnp.stack([wo[0], wo[1], wo[2], wo_d])
    return w_main.astype(BF16), wq_big.astype(BF16), wkv.astype(BF16), w4.astype(BF16)


def _rope_tables(s):
    inv = ROPE_THETA ** (-jnp.arange(0, MLA_ROPE, 2, dtype=F32) / MLA_ROPE)
    ang = jnp.arange(s, dtype=F32)[:, None] * inv[None, :]
    cos, sin = jnp.cos(ang), jnp.sin(ang)
    z = jnp.zeros((s, LANES - MLA_ROPE), F32)
    cos128 = jnp.concatenate([cos, cos, z], axis=1)
    sin128 = jnp.concatenate([-sin, sin, z], axis=1)
    c_mla = (MLA_NOPE + MLA_ROPE) ** -0.5 * LOG2E
    return cos128 * c_mla, sin128 * c_mla, cos128, sin128


def _col_scale():
    ca = DIFF_HD ** -0.5 * LOG2E
    cb = DIL_HD ** -0.5 * LOG2E
    cd = SWA_HD ** -0.5 * LOG2E
    v = np.ones((1, N_SCALED), np.float32)
    v[0, 0:256] = ca
    v[0, 768:1024] = cb
    v[0, 1536:1792] = cd
    return jnp.asarray(v)


def _moe_layer(h2, logits, w1, w3, w2, bm):
    t = logits.shape[0]
    info, counts = _route(logits)
    cnt = counts[0, :N_EXPERTS].astype(jnp.int32)
    padded = (cnt + bm - 1) // bm * bm
    pends = jnp.cumsum(padded)
    pstarts = pends - padded
    e12 = info[:, 0:2].astype(jnp.int32)
    pos = pstarts[e12] + info[:, 2:4].astype(jnp.int32)
    nb = (t * TOP_K) // bm + N_EXPERTS
    rows = nb * bm
    blk_start = jnp.arange(nb, dtype=jnp.int32) * bm
    blk_expert = jnp.minimum(jnp.sum((pends[None, :] <= blk_start[:, None]).astype(jnp.int32), axis=1),
                             N_EXPERTS - 1)
    blk_valid = jnp.clip(pstarts[blk_expert] + cnt[blk_expert] - blk_start, 0, bm).astype(jnp.int32)
    sub = (jnp.arange(SC_CHUNKS, dtype=jnp.int32) * rows)[None, :, None] + pos.T[:, None, :]
    xs = _sc_scatter_rows(h2.reshape(SC_CHUNKS * t, SC_COLS),
                          [sub[k].reshape(1, SC_CHUNKS * t) for k in range(TOP_K)], SC_CHUNKS * rows)
    ys = _moe_ffn(xs.reshape(SC_CHUNKS, rows, SC_COLS), blk_expert.astype(jnp.int32), blk_valid, w1, w3, w2, bm)
    yg = _sc_gather_rows(ys.reshape(SC_CHUNKS * rows, SC_COLS), sub.reshape(1, TOP_K * SC_CHUNKS * t))
    return yg.reshape(TOP_K, SC_CHUNKS, t, SC_COLS), info


SC_WINDOW = 128
SC_COLS = 256
SC_CHUNKS = D_MODEL // SC_COLS


def _sc_mesh():
    return plsc.VectorSubcoreMesh(core_axis_name="c", subcore_axis_name="s")


def _sc_scatter_rows(x, idx_list, rows):
    t, d = x.shape
    n = len(idx_list)

    @functools.partial(pl.kernel, out_type=jax.ShapeDtypeStruct((rows, d), x.dtype), mesh=_sc_mesh(),
                       scratch_types=[])
    def scatter_kernel(x_hbm, *rest):
        idx_hbms, o_hbm = rest[:n], rest[n]

        def body(x_vmem, *idx_vmems):
            for iv in idx_vmems:
                pltpu.sync_copy(x_vmem, o_hbm.at[iv.at[0]])

        pltpu.emit_pipeline(
            body, grid=(t // SC_WINDOW,),
            in_specs=[pl.BlockSpec((SC_WINDOW, d), lambda i: (i, 0))]
            + [pl.BlockSpec((1, SC_WINDOW), lambda i: (0, i))] * n,
            out_specs=[], core_axis_name=("c", "s"), dimension_semantics=(pltpu.PARALLEL,),
        )(x_hbm, *idx_hbms)

    return scatter_kernel(x, *idx_list)


def _sc_gather_rows(y, idx):
    m = idx.shape[1]
    d = y.shape[1]

    @functools.partial(pl.kernel, out_type=jax.ShapeDtypeStruct((m, d), y.dtype), mesh=_sc_mesh(),
                       scratch_types=[])
    def gather_kernel(y_hbm, idx_hbm, o_hbm):
        def body(idx_vmem, o_vmem):
            pltpu.sync_copy(y_hbm.at[idx_vmem.at[0]], o_vmem)

        pltpu.emit_pipeline(
            body, grid=(m // SC_WINDOW,),
            in_specs=[pl.BlockSpec((1, SC_WINDOW), lambda i: (0, i))],
            out_specs=[pl.BlockSpec((SC_WINDOW, d), lambda i: (i, 0))],
            core_axis_name=("c", "s"), dimension_semantics=(pltpu.PARALLEL,),
        )(idx_hbm, o_hbm)

    return gather_kernel(y, idx)


def kernel(x, attn_norm, w_in, w_out, diff_lambda, diff_subln, mla_q_norm, mla_w_uq, mla_kv_norm, mla_w_ukv,
           swa_sinks, ffn_norm, ffn_w1, ffn_w3, ffn_w2, moe_router, moe_w1, moe_w3, moe_w2, final_norm):
    bsz, seq, _ = x.shape
    t = bsz * seq
    depth = w_in.shape[0]
    slopes_a, slopes_b, slopes_d = _alibi_slopes()
    tq_dense = _pick(seq, 512)
    tq_band = _pick(seq, 256)

    tabs = _rope_tables(seq)
    colscale = _col_scale()
    di = jnp.arange(tq_dense)
    causal_mask_t = jnp.where(di[:, None] <= di[None, :], 0.0, NEG_INF).astype(F32)
    tq_diff = tq_dense
    alibi_kb = _alibi_key_columns(slopes_a * LOG2E, tq_diff)
    tq_dil = _pick(seq, 512)
    n_off_dil = min((max(w for w, _ in DIL_PATTERNS) + tq_dil - 1) // tq_dil + 1, seq // tq_dil)
    n_off_swa = min((SWA_WINDOW - 1 + tq_band - 1) // tq_band + 1, seq // tq_band)
    dil_bias = _band_bias(slopes_b, tq_dil, n_off_dil, _dil_mult).reshape(2, 2, n_off_dil, tq_dil, tq_dil)
    swa_bias = _band_bias(slopes_d[np.array([0, 2, 1, 3])], tq_band, n_off_swa, _swa_mult
                          ).reshape(2, 2, n_off_swa, tq_band, tq_band)
    slopes_a_l2 = jnp.asarray(slopes_a * LOG2E, F32)
    dil_m0 = jnp.full((4,), NEG_INF, F32)

    x2d = x.reshape(t, D_MODEL)
    for l in range(depth):
        w_main, wq_big, wkv, w4 = _prep_attn_weights(w_in[l], w_out[l], mla_w_uq[l], mla_w_ukv[l])
        a, avt, b, bvt, d, dvt, cq, ck, cvt = _proj(x2d, attn_norm[l][None], w_main, colscale, mla_q_norm[l][None],
                                                    wq_big, mla_kv_norm[l][None], wkv, tabs, seq)
        a3, b3, d3 = (v.reshape(bsz, seq, -1) for v in (a, b, d))
        sub128 = jnp.concatenate([diff_subln[l], diff_subln[l]])[None]
        o_a = _diff_attention(a3, avt, slopes_a_l2, causal_mask_t[:tq_diff, :tq_diff], alibi_kb, diff_lambda[l],
                              sub128, l, tq_diff)
        o_b = _band_attention(b3, bvt, lambda p: 2 + p, lambda p: p, dil_bias, dil_m0, 0.0, tq_dil, 1, "dil_attn")
        o_c = _mla_attention(cq.reshape(bsz, seq, -1), ck.reshape(bsz, seq, -1), cvt, causal_mask_t, tq_dense)
        swa_m0 = (swa_sinks[l].astype(F32) * LOG2E)[np.array([0, 2, 1, 3])]
        o_d = _band_attention(d3, dvt, lambda p: 2, lambda p: 0, swa_bias, swa_m0, 1.0, tq_band, 2, "swa_attn")
        outs = [v.reshape(t, 256) for v in (o_a, o_b, o_c, o_d)]
        j = l // 2
        if l % 2 == 0:
            x1, h2 = _outproj(*outs, w4, x2d, ffn_norm[l][None])
            x2d = _ffn(h2, ffn_w1[j].astype(BF16), ffn_w3[j].astype(BF16), ffn_w2[j].astype(BF16), x1)
            moe = None
        else:
            r = jnp.pad(moe_router[j], ((0, 0), (0, LANES - N_EXPERTS)))
            r_hi, r_lo = _split_bf16(r, 2)
            x1, h2, logits = _outproj(*outs, w4, x2d, ffn_norm[l][None], (r_hi, r_lo))
            bm = _pick(t, 512)
            moe = _moe_layer(h2, logits, moe_w1[j].astype(BF16), moe_w3[j].astype(BF16),
                             moe_w2[j].astype(BF16), bm)
            x2d = x1
        if l < depth - 1 and moe is not None:
            raise NotImplementedError("MoE combine is fused into the final norm; MoE layer must be last")
    return _final(x2d, final_norm[None], moe).reshape(bsz, seq, D_MODEL)
```

```python
import functools
import math

import numpy as np
import jax
import jax.numpy as jnp
from jax import lax
from jax.experimental import pallas as pl
from jax.experimental.pallas import tpu as pltpu
from jax.experimental.pallas import tpu_sc as plsc

D_MODEL = 1024
DIFF_HEADS, DIFF_HD = 4, 32
DIL_HEADS, DIL_HD = 4, 64
DIL_PATTERNS = ((128, 1), (512, 4), (2048, 16))
MLA_HEADS, MLA_Q_LORA, MLA_KV_LORA, MLA_NOPE, MLA_ROPE, MLA_V = 4, 384, 128, 64, 32, 64
ROPE_THETA = 10000.0
SWA_HEADS, SWA_KV_HEADS, SWA_HD, SWA_WINDOW = 4, 2, 64, 128
D_FF = 2816
N_EXPERTS, TOP_K, D_FF_EXPERT = 8, 2, 3584
RMS_EPS = 1e-6
NEG_INF = -1e30
N_ALIBI = DIFF_HEADS + DIL_HEADS + SWA_HEADS

LOG2E = 1.4426950408889634
LANES = 128
HEAD_LANES = 64
VMEM_LIMIT = 56 * 1024 * 1024

F32 = jnp.float32
BF16 = jnp.bfloat16

_NT = (((1,), (1,)), ((), ()))


def _params(sem, vmem=VMEM_LIMIT):
    return pltpu.CompilerParams(dimension_semantics=sem, vmem_limit_bytes=vmem)


def _alibi_slopes():
    s = 2.0 ** (-8.0 * (np.arange(N_ALIBI) + 1) / N_ALIBI)
    return s[0::3], s[1::3], s[2::3]


def _pick(n, pref):
    t = min(pref, n)
    while n % t:
        t //= 2
    return t


N_SCALED = 2048
C_CQ, C_CKV, C_KPE, C_KPE_SW, N_MAIN = 2048, 2432, 2560, 2688, 2816


ONES_ROWS = 16
VT_ROWS = HEAD_LANES + ONES_ROWS


def _vt_with_ones(v):
    vt = v.T
    ones = jnp.ones((ONES_ROWS, v.shape[0]), F32)
    parts = []
    for hd in range(v.shape[1] // HEAD_LANES):
        parts += [vt[hd * HEAD_LANES:(hd + 1) * HEAD_LANES], ones]
    return jnp.concatenate(parts, axis=0).astype(BF16)


def _proj_body(x_ref, g_ref, w_ref, cs_ref, qn_ref, wq_ref, kn_ref, wkv_ref,
               cosq_ref, sinq_ref, cosk_ref, sink_ref,
               a_ref, avt_ref, b_ref, bvt_ref, d_ref, dvt_ref, cq_ref, ck_ref, cvt_ref):
    x = x_ref[...]
    ms = jnp.mean(x * x, axis=-1, keepdims=True)
    h = (x * lax.rsqrt(ms + RMS_EPS) * g_ref[...]).astype(BF16)
    acc = jnp.dot(h, w_ref[...], preferred_element_type=F32)
    sc = acc[:, :N_SCALED] * cs_ref[...]
    a_ref[...] = sc[:, 0:512].astype(BF16)
    avt_ref[...] = _vt_with_ones(sc[:, 512:768])
    b_ref[...] = sc[:, 768:1280].astype(BF16)
    bvt_ref[...] = _vt_with_ones(sc[:, 1280:1536])
    d_ref[...] = sc[:, 1536:1920].astype(BF16)
    dvt_ref[...] = _vt_with_ones(sc[:, 1920:2048])

    cq = acc[:, C_CQ:C_CKV]
    hq = (cq * lax.rsqrt(jnp.mean(cq * cq, axis=-1, keepdims=True) + RMS_EPS) * qn_ref[...]).astype(BF16)
    yq = jnp.dot(hq, wq_ref[...], preferred_element_type=F32)
    cosq, sinq = cosq_ref[...], sinq_ref[...]
    c_mla = (MLA_NOPE + MLA_ROPE) ** -0.5 * LOG2E
    for hd in range(MLA_HEADS):
        main = yq[:, hd * 256:(hd + 1) * 256]
        sw = yq[:, 1024 + hd * LANES:1024 + (hd + 1) * LANES]
        cq_ref[:, hd * 256:hd * 256 + LANES] = (main[:, :LANES] * cosq + sw * sinq).astype(BF16)
        cq_ref[:, hd * 256 + LANES:(hd + 1) * 256] = (main[:, LANES:] * c_mla).astype(BF16)

    ckv = acc[:, C_CKV:C_KPE]
    hk = (ckv * lax.rsqrt(jnp.mean(ckv * ckv, axis=-1, keepdims=True) + RMS_EPS) * kn_ref[...]).astype(BF16)
    kv = jnp.dot(hk, wkv_ref[...], preferred_element_type=F32)
    kr = (acc[:, C_KPE:C_KPE_SW] * cosk_ref[...] + acc[:, C_KPE_SW:N_MAIN] * sink_ref[...]).astype(BF16)
    for grp in range(2):
        ck_ref[:, grp * 256:grp * 256 + LANES] = kr
        ck_ref[:, grp * 256 + LANES:(grp + 1) * 256] = kv[:, grp * LANES:(grp + 1) * LANES].astype(BF16)
    cvt_ref[...] = _vt_with_ones(kv[:, 256:512])


def _proj(x2d, g, w_main, colscale, qn, wq_big, kn, wkv, tabs, seq):
    t = x2d.shape[0]
    tm = _pick(seq, 512)
    nsb = seq // tm
    full = lambda shape: pl.BlockSpec(shape, lambda i: (0,) * len(shape))
    tab = pl.BlockSpec((tm, LANES), lambda i: (i % nsb, 0))
    row = lambda w: pl.BlockSpec((tm, w), lambda i: (i, 0))
    vt = lambda w: pl.BlockSpec((None, w, tm), lambda i: (i // nsb, 0, i % nsb))
    row_sds = lambda w: jax.ShapeDtypeStruct((t, w), BF16)
    vt_sds = lambda w: jax.ShapeDtypeStruct((t // seq, w, seq), BF16)
    return pl.pallas_call(
        _proj_body,
        grid=(t // tm,),
        in_specs=[row(D_MODEL), full((1, D_MODEL)), full((D_MODEL, N_MAIN)), full((1, N_SCALED)),
                  full((1, MLA_Q_LORA)), full((MLA_Q_LORA, 1536)), full((1, MLA_KV_LORA)),
                  full((MLA_KV_LORA, 512)), tab, tab, tab, tab],
        out_specs=[row(512), vt(4 * VT_ROWS), row(512), vt(4 * VT_ROWS), row(384), vt(2 * VT_ROWS),
                   row(1024), row(512), vt(4 * VT_ROWS)],
        out_shape=[row_sds(512), vt_sds(4 * VT_ROWS), row_sds(512), vt_sds(4 * VT_ROWS), row_sds(384),
                   vt_sds(2 * VT_ROWS), row_sds(1024), row_sds(512), vt_sds(4 * VT_ROWS)],
        compiler_params=_params(("parallel",)),
        name="proj",
    )(x2d, g, w_main, colscale, qn, wq_big, kn, wkv, *tabs)


def _attn_pipeline(n_soft, n_steps, scores, shift, load_vt, m0, s_ref, acc_ref):
    def stage_a(t, ss, dst_ref, m_cur):
        m_next, alpha = [], []
        for i in range(n_soft):
            m_new = jnp.maximum(m_cur[i], jnp.max(ss[i], axis=0, keepdims=True) + shift(i, t))
            dst_ref[i] = ss[i]
            m_next.append(m_new)
            alpha.append(jnp.exp2(m_cur[i] - m_new))
        return tuple(m_next), tuple(alpha)

    def stage_b(t, src_ref, carry):
        m_cur, alpha = carry
        for i in range(n_soft):
            p = jnp.exp2(src_ref[i] - (m_cur[i] - shift(i, t)))
            acc_ref[i] = alpha[i] * acc_ref[i] + jnp.dot(load_vt(t, i), p.astype(BF16),
                                                         preferred_element_type=F32)

    def half(t, src_ref, dst_ref, carry):
        ss = [scores(i, t + 1, False) for i in range(n_soft)]
        stage_b(t, src_ref, carry)
        return stage_a(t + 1, ss, dst_ref, carry[0])

    def body(u, carry):
        carry = half(2 * u, s_ref[0], s_ref[1], carry)
        return half(2 * u + 1, s_ref[1], s_ref[0], carry)

    carry = stage_a(0, [scores(i, 0, True) for i in range(n_soft)], s_ref[0], m0)
    carry = lax.fori_loop(0, n_steps // 2, body, carry)

    @pl.when(n_steps % 2 == 1)
    def _():
        stage_b(n_steps, s_ref[1], half(n_steps - 1, s_ref[0], s_ref[1], carry))

    @pl.when(n_steps % 2 == 0)
    def _():
        stage_b(n_steps, s_ref[0], carry)


def _row_mask(shape, lo, hi):
    row = lax.broadcasted_iota(jnp.int32, shape, 0)
    return jnp.logical_and(row >= lo, row < hi)


def _lane_mask(shape, lo, hi):
    lane = lax.broadcasted_iota(jnp.int32, shape, len(shape) - 1)
    return jnp.logical_and(lane >= lo, lane < hi)


def _normalized(acc_ref, i):
    return acc_ref[i, 0:HEAD_LANES, :] / acc_ref[i, HEAD_LANES:HEAD_LANES + 1, :]


def _diff_body(sl_ref, q_ref, k_ref, vt_ref, mask_ref, kb_ref, lam_ref, sub_ref, o_ref,
               qs_ref, s0_ref, s1_ref, acc_ref, *, tq, lam_init):
    pair, qi = pl.program_id(1), pl.program_id(2)
    tk = tq
    q = q_ref[...].astype(F32)
    for hd in range(2):
        sel = jnp.where(_lane_mask(q.shape, N_BIAS_SPLIT * hd, N_BIAS_SPLIT * (hd + 1)), 1.0, 0.0).astype(BF16)
        for mp in range(2):
            lo = hd * HEAD_LANES + mp * DIFF_HD
            qs_ref[2 * hd + mp, :, 0:LANES] = jnp.where(_lane_mask(q.shape, lo, lo + DIFF_HD), q, 0.0).astype(BF16)
            qs_ref[2 * hd + mp, :, LANES:2 * LANES] = sel
    acc_ref[...] = jnp.zeros(acc_ref.shape, F32)

    def scores(i, t, first):
        off = pl.multiple_of((qi - t) * tk, tk)
        k = jnp.concatenate([k_ref[pl.ds(off, tk), :], kb_ref[...]], axis=1)
        s = lax.dot_general(k, qs_ref[i], _NT, preferred_element_type=F32)
        return s + mask_ref[...] if first else s

    def shift(i, t):
        return sl_ref[2 * pair + i // 2] * (-(t * tk)).astype(F32) if not isinstance(t, int) else 0.0

    def load_vt(t, i):
        return vt_ref[pl.ds((i // 2) * VT_ROWS, VT_ROWS), pl.ds(pl.multiple_of((qi - t) * tk, tk), tk)]

    m0 = tuple(jnp.full((1, tq), NEG_INF, F32) for _ in range(4))
    _attn_pipeline(4, qi, scores, shift, load_vt, m0, (s0_ref, s1_ref), acc_ref)

    lp = lam_ref[...]
    lam = (jnp.exp(jnp.sum(lp[0:1, :] * lp[1:2, :], axis=-1, keepdims=True))
           - jnp.exp(jnp.sum(lp[2:3, :] * lp[3:4, :], axis=-1, keepdims=True)) + lam_init)
    o_h = [_normalized(acc_ref, 2 * hd) - lam * _normalized(acc_ref, 2 * hd + 1) for hd in range(2)]
    o = jnp.concatenate(o_h, axis=0).T
    in_h0 = _lane_mask((tq, LANES), 0, HEAD_LANES)
    sq = o * o
    ms0 = jnp.sum(jnp.where(in_h0, sq, 0.0), axis=-1, keepdims=True)
    ms1 = jnp.sum(jnp.where(in_h0, 0.0, sq), axis=-1, keepdims=True)
    ms = jnp.where(in_h0, ms0, ms1) * (1.0 / HEAD_LANES)
    y = o * lax.rsqrt(ms + RMS_EPS) * sub_ref[...]
    o_ref[...] = (y * (1.0 - lam_init)).astype(o_ref.dtype)


N_BIAS_SPLIT = 3


def _alibi_key_columns(slopes_l2, tk):
    dj = jnp.arange(tk, dtype=F32)[None, :, None]
    val = jnp.asarray(slopes_l2, F32).reshape(-1, 1, 2) * dj
    cols = jnp.stack(_split_bf16(val, N_BIAS_SPLIT), axis=-1).reshape(val.shape[0], tk, 2 * N_BIAS_SPLIT)
    return jnp.pad(cols, ((0, 0), (0, 0), (0, LANES - 2 * N_BIAS_SPLIT)))


def _split_bf16(x, n):
    pieces = []
    for _ in range(n):
        bits = lax.bitcast_convert_type(x, jnp.uint32) & jnp.uint32(0xFFFF0000)
        head = lax.bitcast_convert_type(bits, F32)
        pieces.append(head.astype(BF16))
        x = x - head
    return pieces


def _diff_attention(a3, avt, slopes_l2, mask_t, kb, lam_p, sub128, layer_idx, tq):
    b, s, _ = a3.shape
    nq = s // tq
    lam_init = 0.8 - 0.6 * math.exp(-0.3 * layer_idx)
    body = functools.partial(_diff_body, tq=tq, lam_init=lam_init)
    grid_spec = pltpu.PrefetchScalarGridSpec(
        num_scalar_prefetch=1,
        grid=(b, 2, nq),
        in_specs=[
            pl.BlockSpec((None, tq, LANES), lambda bi, p, qi, sl: (bi, qi, p)),
            pl.BlockSpec((None, s, LANES), lambda bi, p, qi, sl: (bi, 0, 2 + p)),
            pl.BlockSpec((None, 2 * VT_ROWS, s), lambda bi, p, qi, sl: (bi, p, 0)),
            pl.BlockSpec((tq, tq), lambda bi, p, qi, sl: (0, 0)),
            pl.BlockSpec((None, tq, LANES), lambda bi, p, qi, sl: (p, 0, 0)),
            pl.BlockSpec((4, DIFF_HD), lambda bi, p, qi, sl: (0, 0)),
            pl.BlockSpec((1, LANES), lambda bi, p, qi, sl: (0, 0)),
        ],
        out_specs=pl.BlockSpec((None, tq, LANES), lambda bi, p, qi, sl: (bi, qi, p)),
        scratch_shapes=[pltpu.VMEM((4, tq, 2 * LANES), BF16), pltpu.VMEM((4, tq, tq), F32),
                        pltpu.VMEM((4, tq, tq), F32), pltpu.VMEM((4, VT_ROWS, tq), F32)],
    )
    return pl.pallas_call(
        body, grid_spec=grid_spec,
        out_shape=jax.ShapeDtypeStruct((b, s, 256), BF16),
        compiler_params=_params(("parallel", "parallel", "arbitrary")),
        name="diff_attn",
    )(slopes_l2, a3, a3, avt, mask_t, kb, lam_p, sub128)


def _mla_body(q_ref, k_ref, vt_ref, mask_ref, o_ref, s0_ref, s1_ref, acc_ref, *, tq):
    qi = pl.program_id(1)
    tk = tq
    acc_ref[...] = jnp.zeros(acc_ref.shape, F32)

    def scores(i, t, first):
        k = k_ref[pl.ds(pl.multiple_of((qi - t) * tk, tk), tk), (i // 2) * 256:(i // 2 + 1) * 256]
        s = lax.dot_general(k, q_ref[:, i * 256:(i + 1) * 256], _NT, preferred_element_type=F32)
        return s + mask_ref[...] if first else s

    def load_vt(t, i):
        return vt_ref[pl.ds(i * VT_ROWS, VT_ROWS), pl.ds(pl.multiple_of((qi - t) * tk, tk), tk)]

    m0 = tuple(jnp.full((1, tq), NEG_INF, F32) for _ in range(MLA_HEADS))
    _attn_pipeline(MLA_HEADS, qi, scores, lambda i, t: 0.0, load_vt, m0, (s0_ref, s1_ref), acc_ref)
    o_t = jnp.concatenate([_normalized(acc_ref, i) for i in range(MLA_HEADS)], axis=0)
    o_ref[...] = o_t.T.astype(o_ref.dtype)


def _mla_attention(cq3, ck3, cvt, mask_t, tq):
    b, s, _ = cq3.shape
    nq = s // tq
    return pl.pallas_call(
        functools.partial(_mla_body, tq=tq),
        grid=(b, nq),
        in_specs=[
            pl.BlockSpec((None, tq, MLA_HEADS * 256), lambda bi, qi: (bi, qi, 0)),
            pl.BlockSpec((None, s, 512), lambda bi, qi: (bi, 0, 0)),
            pl.BlockSpec((None, MLA_HEADS * VT_ROWS, s), lambda bi, qi: (bi, 0, 0)),
            pl.BlockSpec((tq, tq), lambda bi, qi: (0, 0)),
        ],
        out_specs=pl.BlockSpec((None, tq, 256), lambda bi, qi: (bi, qi, 0)),
        out_shape=jax.ShapeDtypeStruct((b, s, 256), BF16),
        scratch_shapes=[pltpu.VMEM((MLA_HEADS, tq, tq), F32), pltpu.VMEM((MLA_HEADS, tq, tq), F32),
                        pltpu.VMEM((MLA_HEADS, VT_ROWS, tq), F32)],
        compiler_params=_params(("parallel", "arbitrary")),
        name="mla_attn",
    )(cq3, ck3, cvt, mask_t)


def _band_body(m0_ref, q_ref, k_ref, vt_ref, bias_ref, o_ref, qs_ref, s0_ref, s1_ref, acc_ref,
               *, tq, n_off, l_init, n_slabs):
    slab0, qi = pl.program_id(1) * n_slabs, pl.program_id(2)
    tk = tq
    n_soft = 2 * n_slabs
    for sl in range(n_slabs):
        q = q_ref[:, sl * LANES:(sl + 1) * LANES].astype(F32)
        for g in range(2):
            qs_ref[2 * sl + g] = jnp.where(_lane_mask(q.shape, g * HEAD_LANES, (g + 1) * HEAD_LANES), q, 0.0
                                           ).astype(BF16)
    for i in range(n_soft):
        acc_ref[i] = jnp.where(_row_mask((VT_ROWS, tq), 0, HEAD_LANES), 0.0, l_init).astype(F32)

    def scores(i, t, first):
        k = k_ref[pl.ds(pl.multiple_of((qi - t) * tk, tk), tk), :]
        return lax.dot_general(k, qs_ref[i], _NT, preferred_element_type=F32) + bias_ref[i // 2, i % 2, t]

    def load_vt(t, i):
        return vt_ref[pl.ds((i % 2) * VT_ROWS, VT_ROWS), pl.ds(pl.multiple_of((qi - t) * tk, tk), tk)]

    m0 = tuple(jnp.full((1, tq), m0_ref[2 * slab0 + i], F32) for i in range(n_soft))
    _attn_pipeline(n_soft, jnp.minimum(qi, n_off - 1), scores, lambda i, t: 0.0, load_vt, m0,
                   (s0_ref, s1_ref), acc_ref)
    o_t = jnp.concatenate([_normalized(acc_ref, i) for i in range(n_soft)], axis=0)
    o_ref[...] = o_t.T.astype(o_ref.dtype)


def _band_attention(x3, xvt, k_blk_of, vt_blk_of, bias, m0, l_init, tq, n_slabs, name):
    b, s, _ = x3.shape
    nq = s // tq
    n_off = bias.shape[2]
    n_soft = 2 * n_slabs
    grid_spec = pltpu.PrefetchScalarGridSpec(
        num_scalar_prefetch=1,
        grid=(b, 2 // n_slabs, nq),
        in_specs=[
            pl.BlockSpec((None, tq, n_slabs * LANES), lambda bi, p, qi, m0r: (bi, qi, p)),
            pl.BlockSpec((None, s, LANES), lambda bi, p, qi, m0r: (bi, 0, k_blk_of(p))),
            pl.BlockSpec((None, 2 * VT_ROWS, s), lambda bi, p, qi, m0r: (bi, vt_blk_of(p), 0)),
            pl.BlockSpec((n_slabs, 2, n_off, tq, tq), lambda bi, p, qi, m0r: (p, 0, 0, 0, 0)),
        ],
        out_specs=pl.BlockSpec((None, tq, n_slabs * LANES), lambda bi, p, qi, m0r: (bi, qi, p)),
        scratch_shapes=[pltpu.VMEM((n_soft, tq, LANES), BF16), pltpu.VMEM((n_soft, tq, tq), F32),
                        pltpu.VMEM((n_soft, tq, tq), F32), pltpu.VMEM((n_soft, VT_ROWS, tq), F32)],
    )
    return pl.pallas_call(
        functools.partial(_band_body, tq=tq, n_off=n_off, l_init=l_init, n_slabs=n_slabs),
        grid_spec=grid_spec,
        out_shape=jax.ShapeDtypeStruct((b, s, 256), BF16),
        compiler_params=_params(("parallel", "parallel", "arbitrary")),
        name=name,
    )(m0, x3, x3, xvt, bias)


def _band_bias(slopes, tq, n_off, mult_fn):
    di = jnp.arange(tq, dtype=jnp.int32)[None, :]
    dj = jnp.arange(tq, dtype=jnp.int32)[:, None]
    off = jnp.arange(n_off, dtype=jnp.int32)[:, None, None]
    delta = off * tq + di - dj
    mult = mult_fn(delta)
    sl = jnp.asarray(slopes, F32)[:, None, None, None] * LOG2E
    val = -sl * delta.astype(F32)[None] + jnp.log2(jnp.maximum(mult, 1).astype(F32))[None]
    return jnp.where((mult > 0)[None], val, NEG_INF)


def _dil_mult(delta):
    m = jnp.zeros(delta.shape, jnp.int32)
    for window, d in DIL_PATTERNS:
        m = m + ((delta >= 0) & (delta <= window) & (delta % d == 0)).astype(jnp.int32)
    return m


def _swa_mult(delta):
    return ((delta >= 0) & (delta <= SWA_WINDOW - 1)).astype(jnp.int32)


def _outproj_body(*refs, with_router):
    if with_router:
        (oa, ob, oc, od, w_ref, x_ref, g_ref, rh_ref, rl_ref, x1_ref, h2_ref, lg_ref) = refs
    else:
        (oa, ob, oc, od, w_ref, x_ref, g_ref, x1_ref, h2_ref) = refs
    acc = x_ref[...]
    for i, o in enumerate((oa, ob, oc, od)):
        acc = acc + jnp.dot(o[...], w_ref[i], preferred_element_type=F32)
    x1_ref[...] = acc
    h2 = acc * lax.rsqrt(jnp.mean(acc * acc, axis=-1, keepdims=True) + RMS_EPS) * g_ref[...]
    hi = h2.astype(BF16)
    if with_router:
        for c in range(SC_CHUNKS):
            h2_ref[c] = h2[:, c * SC_COLS:(c + 1) * SC_COLS]
        lo = (h2 - hi.astype(F32)).astype(BF16)
        lg_ref[...] = (jnp.dot(hi, rh_ref[...], preferred_element_type=F32)
                       + (jnp.dot(hi, rl_ref[...], preferred_element_type=F32)
                          + jnp.dot(lo, rh_ref[...], preferred_element_type=F32)))
    else:
        h2_ref[...] = hi


def _outproj(oa, ob, oc, od, w4, x2d, g, router_hl=None):
    t = x2d.shape[0]
    tm = _pick(t, 512)
    with_router = router_hl is not None
    full = lambda shape: pl.BlockSpec(shape, lambda i: (0,) * len(shape))
    row = lambda w: pl.BlockSpec((tm, w), lambda i: (i, 0))
    in_specs = [row(256)] * 4 + [full((4, 256, D_MODEL)), row(D_MODEL), full((1, D_MODEL))]
    args = [oa, ob, oc, od, w4, x2d, g]
    out_specs = [row(D_MODEL), row(D_MODEL)]
    out_shape = [jax.ShapeDtypeStruct((t, D_MODEL), F32), jax.ShapeDtypeStruct((t, D_MODEL), BF16)]
    if with_router:
        out_specs[1] = pl.BlockSpec((SC_CHUNKS, tm, SC_COLS), lambda i: (0, i, 0))
        out_shape[1] = jax.ShapeDtypeStruct((SC_CHUNKS, t, SC_COLS), F32)
        in_specs += [full((D_MODEL, LANES))] * 2
        args += list(router_hl)
        out_specs.append(row(LANES))
        out_shape.append(jax.ShapeDtypeStruct((t, LANES), F32))
    return pl.pallas_call(
        functools.partial(_outproj_body, with_router=with_router),
        grid=(t // tm,), in_specs=in_specs, out_specs=out_specs, out_shape=out_shape,
        compiler_params=_params(("parallel",)),
        name="outproj",
    )(*args)


FF_CHUNK = 512


def _swiglu(x, w1_ref, w3_ref, w2_ref):
    tf = w1_ref.shape[1]
    total = None
    for lo in range(0, tf, FF_CHUNK):
        hi = min(lo + FF_CHUNK, tf)
        a = jnp.dot(x, w1_ref[:, lo:hi], preferred_element_type=F32)
        b = jnp.dot(x, w3_ref[:, lo:hi], preferred_element_type=F32)
        mid = (a * jax.nn.sigmoid(a) * b).astype(BF16)
        part = jnp.dot(mid, w2_ref[lo:hi, :], preferred_element_type=F32)
        total = part if total is None else total + part
    return total


def _ffn_body(h_ref, w1_ref, w3_ref, w2_ref, x_ref, o_ref):
    o_ref[...] = x_ref[...] + _swiglu(h_ref[...], w1_ref, w3_ref, w2_ref)


def _ffn(h2, w1, w3, w2, x1):
    t = h2.shape[0]
    tm = _pick(t, 512)
    resident = lambda shape: pl.BlockSpec(shape, lambda i: (0, 0), pipeline_mode=pl.Buffered(1))
    return pl.pallas_call(
        _ffn_body,
        grid=(t // tm,),
        in_specs=[pl.BlockSpec((tm, D_MODEL), lambda i: (i, 0)),
                  resident((D_MODEL, D_FF)), resident((D_MODEL, D_FF)), resident((D_FF, D_MODEL)),
                  pl.BlockSpec((tm, D_MODEL), lambda i: (i, 0))],
        out_specs=pl.BlockSpec((tm, D_MODEL), lambda i: (i, 0)),
        out_shape=jax.ShapeDtypeStruct((t, D_MODEL), F32),
        compiler_params=_params(("parallel",)),
        name="ffn",
    )(h2, w1, w3, w2, x1)


def _route_body(lg_ref, tri_ref, info_ref, cnt_ref, carry_ref):
    i = pl.program_id(0)

    @pl.when(i == 0)
    def _():
        carry_ref[...] = jnp.zeros(carry_ref.shape, F32)

    lg = lg_ref[...]
    lane = lax.broadcasted_iota(jnp.int32, lg.shape, 1)
    lg = jnp.where(lane < N_EXPERTS, lg, -jnp.inf)
    m1 = jnp.max(lg, axis=-1, keepdims=True)
    i1 = jnp.min(jnp.where(lg == m1, lane, LANES), axis=-1, keepdims=True)
    oh1 = lane == i1
    lg2 = jnp.where(oh1, -jnp.inf, lg)
    m2 = jnp.max(lg2, axis=-1, keepdims=True)
    i2 = jnp.min(jnp.where(lg2 == m2, lane, LANES), axis=-1, keepdims=True)
    oh2 = lane == i2
    e2 = jnp.exp(m2 - m1)
    g1 = 1.0 / (1.0 + e2)
    g2 = e2 / (1.0 + e2)
    oh = jnp.where(jnp.logical_or(oh1, oh2), 1.0, 0.0)
    rank = jnp.dot(tri_ref[...], oh.astype(BF16), preferred_element_type=F32) + carry_ref[...]
    r1 = jnp.sum(jnp.where(oh1, rank, 0.0), axis=-1, keepdims=True)
    r2 = jnp.sum(jnp.where(oh2, rank, 0.0), axis=-1, keepdims=True)
    carry_ref[...] += jnp.sum(oh, axis=0, keepdims=True)
    cnt_ref[...] = carry_ref[...]
    info = jnp.zeros(lg.shape, F32)
    for c, val in enumerate((i1.astype(F32), i2.astype(F32), r1, r2, g1, g2)):
        info = jnp.where(lane == c, val, info)
    info_ref[...] = info


def _route(logits):
    t = logits.shape[0]
    tm = _pick(t, 512)
    tri = (jnp.arange(tm)[:, None] > jnp.arange(tm)[None, :]).astype(BF16)
    return pl.pallas_call(
        _route_body,
        grid=(t // tm,),
        in_specs=[pl.BlockSpec((tm, LANES), lambda i: (i, 0)), pl.BlockSpec((tm, tm), lambda i: (0, 0))],
        out_specs=[pl.BlockSpec((tm, LANES), lambda i: (i, 0)), pl.BlockSpec((1, LANES), lambda i: (0, 0))],
        out_shape=[jax.ShapeDtypeStruct((t, LANES), F32), jax.ShapeDtypeStruct((1, LANES), F32)],
        scratch_shapes=[pltpu.VMEM((1, LANES), F32)],
        compiler_params=_params(("arbitrary",)),
        name="route",
    )(logits, tri)


def _moe_ffn_body(be_ref, bv_ref, x_ref, w1_ref, w3_ref, w2_ref, o_ref, acc_ref):
    i, f = pl.program_id(0), pl.program_id(1)
    last = pl.num_programs(1) - 1
    valid = bv_ref[i]

    @pl.when(valid > 0)
    def _():
        x = jnp.concatenate([x_ref[c] for c in range(SC_CHUNKS)], axis=1)
        x = jnp.where(lax.broadcasted_iota(jnp.int32, x.shape, 0) < valid, x, 0.0).astype(BF16)
        contrib = _swiglu(x, w1_ref, w3_ref, w2_ref)

        @pl.when(f == 0)
        def _():
            acc_ref[...] = contrib

        @pl.when(f > 0)
        def _():
            acc_ref[...] += contrib

        @pl.when(f == last)
        def _():
            for c in range(SC_CHUNKS):
                o_ref[c] = acc_ref[:, c * SC_COLS:(c + 1) * SC_COLS]

    @pl.when(jnp.logical_and(valid == 0, f == last))
    def _():
        o_ref[...] = jnp.zeros(o_ref.shape, o_ref.dtype)


def _moe_ffn(xs, blk_expert, blk_valid, w1, w3, w2, bm):
    rows = xs.shape[1]
    tf = D_FF_EXPERT // 2
    chunked = pl.BlockSpec((SC_CHUNKS, bm, SC_COLS), lambda i, f, be, bv: (0, i, 0))
    grid_spec = pltpu.PrefetchScalarGridSpec(
        num_scalar_prefetch=2,
        grid=(rows // bm, D_FF_EXPERT // tf),
        in_specs=[chunked,
                  pl.BlockSpec((None, D_MODEL, tf), lambda i, f, be, bv: (be[i], 0, f)),
                  pl.BlockSpec((None, D_MODEL, tf), lambda i, f, be, bv: (be[i], 0, f)),
                  pl.BlockSpec((None, tf, D_MODEL), lambda i, f, be, bv: (be[i], f, 0))],
        out_specs=chunked,
        scratch_shapes=[pltpu.VMEM((bm, D_MODEL), F32)],
    )
    return pl.pallas_call(
        _moe_ffn_body, grid_spec=grid_spec,
        out_shape=jax.ShapeDtypeStruct(xs.shape, F32),
        compiler_params=_params(("parallel", "arbitrary")),
        name="moe_ffn",
    )(blk_expert, blk_valid, xs, w1, w3, w2)


def _final_body(*refs, with_moe):
    if with_moe:
        x_ref, y0_ref, y1_ref, info_ref, g_ref, o_ref = refs
        info = info_ref[...]
        y0 = jnp.concatenate([y0_ref[c] for c in range(SC_CHUNKS)], axis=1)
        y1 = jnp.concatenate([y1_ref[c] for c in range(SC_CHUNKS)], axis=1)
        x = x_ref[...] + (info[:, 4:5] * y0 + info[:, 5:6] * y1)
    else:
        x_ref, g_ref, o_ref = refs
        x = x_ref[...]
    o_ref[...] = x * lax.rsqrt(jnp.mean(x * x, axis=-1, keepdims=True) + RMS_EPS) * g_ref[...]


def _final(x2d, g, moe=None):
    t = x2d.shape[0]
    tm = _pick(t, 512)
    row = lambda w: pl.BlockSpec((tm, w), lambda i: (i, 0))
    in_specs, args = [row(D_MODEL)], [x2d]
    if moe is not None:
        yg, info = moe
        slot = lambda k: pl.BlockSpec((None, SC_CHUNKS, tm, SC_COLS), lambda i: (k, 0, i, 0))
        in_specs += [slot(0), slot(1), row(LANES)]
        args += [yg, yg, info]
    in_specs.append(pl.BlockSpec((1, D_MODEL), lambda i: (0, 0)))
    args.append(g)
    return pl.pallas_call(
        functools.partial(_final_body, with_moe=moe is not None),
        grid=(t // tm,), in_specs=in_specs, out_specs=row(D_MODEL),
        out_shape=jax.ShapeDtypeStruct((t, D_MODEL), F32),
        compiler_params=_params(("parallel",)),
        name="final",
    )(*args)


def _prep_attn_weights(w_in, w_out, w_uq, w_ukv):
    widths = (256, 256, 256, 256, 256, 256, MLA_Q_LORA, MLA_KV_LORA, MLA_ROPE, 256, 128, 128)
    offs = np.concatenate([[0], np.cumsum(widths)])
    qa, ka, va, qb, kb, vb, cq, ckv, kpe, qd, kd, vd = [w_in[:, offs[i]:offs[i + 1]] for i in range(12)]
    hperm = np.array([0, 2, 1, 3])
    qd = qd.reshape(D_MODEL, 4, SWA_HD)[:, hperm].reshape(D_MODEL, 256)
    pad = jnp.zeros((D_MODEL, LANES - MLA_ROPE), w_in.dtype)
    half = MLA_ROPE // 2
    kpe_sw = jnp.concatenate([kpe[:, half:], kpe[:, :half]], axis=1)
    w_main = jnp.concatenate([qa, ka, va, qb, kb, vb, qd, kd, vd, cq, ckv, kpe, pad, kpe_sw, pad], axis=1)

    uq = w_uq.reshape(MLA_Q_LORA, MLA_HEADS, MLA_NOPE + MLA_ROPE)
    z = lambda n: jnp.zeros((MLA_Q_LORA, n), w_uq.dtype)
    mains, sws = [], []
    for hd in range(MLA_HEADS):
        nope, rope = uq[:, hd, :MLA_NOPE], uq[:, hd, MLA_NOPE:]
        nope128 = jnp.concatenate([nope, z(64)] if hd % 2 == 0 else [z(64), nope], axis=1)
        mains.append(jnp.concatenate([rope, z(LANES - MLA_ROPE), nope128], axis=1))
        sws.append(jnp.concatenate([rope[:, half:], rope[:, :half], z(LANES - MLA_ROPE)], axis=1))
    wq_big = jnp.concatenate(mains + sws, axis=1)

    ukv = w_ukv.reshape(MLA_KV_LORA, MLA_HEADS, MLA_NOPE + MLA_V)
    wkv = jnp.concatenate([ukv[:, :, :MLA_NOPE].reshape(MLA_KV_LORA, 256),
                           ukv[:, :, MLA_NOPE:].reshape(MLA_KV_LORA, 256)], axis=1)

    wo = w_out.reshape(4, 256, D_MODEL)
    wo_d = wo[3].reshape(4, SWA_HD, D_MODEL)[hperm].reshape(256, D_MODEL)
    w4 = jnp.stack([wo[0], wo[1], wo[2], wo_d])
    return w_main.astype(BF16), wq_big.astype(BF16), wkv.astype(BF16), w4.astype(BF16)


def _rope_tables(s):
    inv = ROPE_THETA ** (-jnp.arange(0, MLA_ROPE, 2, dtype=F32) / MLA_ROPE)
    ang = jnp.arange(s, dtype=F32)[:, None] * inv[None, :]
    cos, sin = jnp.cos(ang), jnp.sin(ang)
    z = jnp.zeros((s, LANES - MLA_ROPE), F32)
    cos128 = jnp.concatenate([cos, cos, z], axis=1)
    sin128 = jnp.concatenate([-sin, sin, z], axis=1)
    c_mla = (MLA_NOPE + MLA_ROPE) ** -0.5 * LOG2E
    return cos128 * c_mla, sin128 * c_mla, cos128, sin128


def _col_scale():
    ca = DIFF_HD ** -0.5 * LOG2E
    cb = DIL_HD ** -0.5 * LOG2E
    cd = SWA_HD ** -0.5 * LOG2E
    v = np.ones((1, N_SCALED), np.float32)
    v[0, 0:256] = ca
    v[0, 768:1024] = cb
    v[0, 1536:1792] = cd
    return jnp.asarray(v)


def _moe_layer(h2, logits, w1, w3, w2, bm):
    t = logits.shape[0]
    info, counts = _route(logits)
    cnt = counts[0, :N_EXPERTS].astype(jnp.int32)
    padded = (cnt + bm - 1) // bm * bm
    pends = jnp.cumsum(padded)
    pstarts = pends - padded
    e12 = info[:, 0:2].astype(jnp.int32)
    pos = pstarts[e12] + info[:, 2:4].astype(jnp.int32)
    nb = (t * TOP_K) // bm + N_EXPERTS
    rows = nb * bm
    blk_start = jnp.arange(nb, dtype=jnp.int32) * bm
    blk_expert = jnp.minimum(jnp.sum((pends[None, :] <= blk_start[:, None]).astype(jnp.int32), axis=1),
                             N_EXPERTS - 1)
    blk_valid = jnp.clip(pstarts[blk_expert] + cnt[blk_expert] - blk_start, 0, bm).astype(jnp.int32)
    sub = (jnp.arange(SC_CHUNKS, dtype=jnp.int32) * rows)[None, :, None] + pos.T[:, None, :]
    xs = _sc_scatter_rows(h2.reshape(SC_CHUNKS * t, SC_COLS),
                          [sub[k].reshape(1, SC_CHUNKS * t) for k in range(TOP_K)], SC_CHUNKS * rows)
    ys = _moe_ffn(xs.reshape(SC_CHUNKS, rows, SC_COLS), blk_expert.astype(jnp.int32), blk_valid, w1, w3, w2, bm)
    yg = _sc_gather_rows(ys.reshape(SC_CHUNKS * rows, SC_COLS), sub.reshape(1, TOP_K * SC_CHUNKS * t))
    return yg.reshape(TOP_K, SC_CHUNKS, t, SC_COLS), info


SC_WINDOW = 128
SC_COLS = 256
SC_CHUNKS = D_MODEL // SC_COLS


def _sc_mesh():
    return plsc.VectorSubcoreMesh(core_axis_name="c", subcore_axis_name="s")


def _sc_scatter_rows(x, idx_list, rows):
    t, d = x.shape
    n = len(idx_list)

    @functools.partial(pl.kernel, out_type=jax.ShapeDtypeStruct((rows, d), x.dtype), mesh=_sc_mesh(),
                       scratch_types=[])
    def scatter_kernel(x_hbm, *rest):
        idx_hbms, o_hbm = rest[:n], rest[n]

        def body(x_vmem, *idx_vmems):
            for iv in idx_vmems:
                pltpu.sync_copy(x_vmem, o_hbm.at[iv.at[0]])

        pltpu.emit_pipeline(
            body, grid=(t // SC_WINDOW,),
            in_specs=[pl.BlockSpec((SC_WINDOW, d), lambda i: (i, 0))]
            + [pl.BlockSpec((1, SC_WINDOW), lambda i: (0, i))] * n,
            out_specs=[], core_axis_name=("c", "s"), dimension_semantics=(pltpu.PARALLEL,),
        )(x_hbm, *idx_hbms)

    return scatter_kernel(x, *idx_list)


def _sc_gather_rows(y, idx):
    m = idx.shape[1]
    d = y.shape[1]

    @functools.partial(pl.kernel, out_type=jax.ShapeDtypeStruct((m, d), y.dtype), mesh=_sc_mesh(),
                       scratch_types=[])
    def gather_kernel(y_hbm, idx_hbm, o_hbm):
        def body(idx_vmem, o_vmem):
            pltpu.sync_copy(y_hbm.at[idx_vmem.at[0]], o_vmem)

        pltpu.emit_pipeline(
            body, grid=(m // SC_WINDOW,),
            in_specs=[pl.BlockSpec((1, SC_WINDOW), lambda i: (0, i))],
            out_specs=[pl.BlockSpec((SC_WINDOW, d), lambda i: (i, 0))],
            core_axis_name=("c", "s"), dimension_semantics=(pltpu.PARALLEL,),
        )(idx_hbm, o_hbm)

    return gather_kernel(y, idx)


def kernel(x, attn_norm, w_in, w_out, diff_lambda, diff_subln, mla_q_norm, mla_w_uq, mla_kv_norm, mla_w_ukv,
           swa_sinks, ffn_norm, ffn_w1, ffn_w3, ffn_w2, moe_router, moe_w1, moe_w3, moe_w2, final_norm):
    bsz, seq, _ = x.shape
    t = bsz * seq
    depth = w_in.shape[0]
    slopes_a, slopes_b, slopes_d = _alibi_slopes()
    tq_dense = _pick(seq, 512)
    tq_band = _pick(seq, 256)

    tabs = _rope_tables(seq)
    colscale = _col_scale()
    di = jnp.arange(tq_dense)
    causal_mask_t = jnp.where(di[:, None] <= di[None, :], 0.0, NEG_INF).astype(F32)
    tq_diff = tq_dense
    alibi_kb = _alibi_key_columns(slopes_a * LOG2E, tq_diff)
    tq_dil = _pick(seq, 512)
    n_off_dil = min((max(w for w, _ in DIL_PATTERNS) + tq_dil - 1) // tq_dil + 1, seq // tq_dil)
    n_off_swa = min((SWA_WINDOW - 1 + tq_band - 1) // tq_band + 1, seq // tq_band)
    dil_bias = _band_bias(slopes_b, tq_dil, n_off_dil, _dil_mult).reshape(2, 2, n_off_dil, tq_dil, tq_dil)
    swa_bias = _band_bias(slopes_d[np.array([0, 2, 1, 3])], tq_band, n_off_swa, _swa_mult
                          ).reshape(2, 2, n_off_swa, tq_band, tq_band)
    slopes_a_l2 = jnp.asarray(slopes_a * LOG2E, F32)
    dil_m0 = jnp.full((4,), NEG_INF, F32)

    x2d = x.reshape(t, D_MODEL)
    for l in range(depth):
        w_main, wq_big, wkv, w4 = _prep_attn_weights(w_in[l], w_out[l], mla_w_uq[l], mla_w_ukv[l])
        a, avt, b, bvt, d, dvt, cq, ck, cvt = _proj(x2d, attn_norm[l][None], w_main, colscale, mla_q_norm[l][None],
                                                    wq_big, mla_kv_norm[l][None], wkv, tabs, seq)
        a3, b3, d3 = (v.reshape(bsz, seq, -1) for v in (a, b, d))
        sub128 = jnp.concatenate([diff_subln[l], diff_subln[l]])[None]
        o_a = _diff_attention(a3, avt, slopes_a_l2, causal_mask_t[:tq_diff, :tq_diff], alibi_kb, diff_lambda[l],
                              sub128, l, tq_diff)
        o_b = _band_attention(b3, bvt, lambda p: 2 + p, lambda p: p, dil_bias, dil_m0, 0.0, tq_dil, 1, "dil_attn")
        o_c = _mla_attention(cq.reshape(bsz, seq, -1), ck.reshape(bsz, seq, -1), cvt, causal_mask_t, tq_dense)
        swa_m0 = (swa_sinks[l].astype(F32) * LOG2E)[np.array([0, 2, 1, 3])]
        o_d = _band_attention(d3, dvt, lambda p: 2, lambda p: 0, swa_bias, swa_m0, 1.0, tq_band, 2, "swa_attn")
        outs = [v.reshape(t, 256) for v in (o_a, o_b, o_c, o_d)]
        j = l // 2
        if l % 2 == 0:
            x1, h2 = _outproj(*outs, w4, x2d, ffn_norm[l][None])
            x2d = _ffn(h2, ffn_w1[j].astype(BF16), ffn_w3[j].astype(BF16), ffn_w2[j].astype(BF16), x1)
            moe = None
        else:
            r = jnp.pad(moe_router[j], ((0, 0), (0, LANES - N_EXPERTS)))
            r_hi, r_lo = _split_bf16(r, 2)
            x1, h2, logits = _outproj(*outs, w4, x2d, ffn_norm[l][None], (r_hi, r_lo))
            bm = _pick(t, 512)
            moe = _moe_layer(h2, logits, moe_w1[j].astype(BF16), moe_w3[j].astype(BF16),
                             moe_w2[j].astype(BF16), bm)
            x2d = x1
        if l < depth - 1 and moe is not None:
            raise NotImplementedError("MoE combine is fused into the final norm; MoE layer must be last")
    return _final(x2d, final_norm[None], moe).reshape(bsz, seq, D_MODEL)
```

```python
import functools
import math

import numpy as np
import jax
import jax.numpy as jnp
from jax import lax
from jax.experimental import pallas as pl
from jax.experimental.pallas import tpu as pltpu
from jax.experimental.pallas import tpu_sc as plsc

D_MODEL = 1024
DIFF_HEADS, DIFF_HD = 4, 32
DIL_HEADS, DIL_HD = 4, 64
DIL_PATTERNS = ((128, 1), (512, 4), (2048, 16))
MLA_HEADS, MLA_Q_LORA, MLA_KV_LORA, MLA_NOPE, MLA_ROPE, MLA_V = 4, 384, 128, 64, 32, 64
ROPE_THETA = 10000.0
SWA_HEADS, SWA_KV_HEADS, SWA_HD, SWA_WINDOW = 4, 2, 64, 128
D_FF = 2816
N_EXPERTS, TOP_K, D_FF_EXPERT = 8, 2, 3584
RMS_EPS = 1e-6
NEG_INF = -1e30
N_ALIBI = DIFF_HEADS + DIL_HEADS + SWA_HEADS

LOG2E = 1.4426950408889634
LANES = 128
HEAD_LANES = 64
VMEM_LIMIT = 56 * 1024 * 1024

F32 = jnp.float32
BF16 = jnp.bfloat16

_NT = (((1,), (1,)), ((), ()))


def _params(sem, vmem=VMEM_LIMIT):
    return pltpu.CompilerParams(dimension_semantics=sem, vmem_limit_bytes=vmem)


def _alibi_slopes():
    s = 2.0 ** (-8.0 * (np.arange(N_ALIBI) + 1) / N_ALIBI)
    return s[0::3], s[1::3], s[2::3]


def _pick(n, pref):
    t = min(pref, n)
    while n % t:
        t //= 2
    return t


N_SCALED = 2048
C_CQ, C_CKV, C_KPE, C_KPE_SW, N_MAIN = 2048, 2432, 2560, 2688, 2816


ONES_ROWS = 16
VT_ROWS = HEAD_LANES + ONES_ROWS


def _vt_with_ones(v):
    vt = v.T
    ones = jnp.ones((ONES_ROWS, v.shape[0]), F32)
    parts = []
    for hd in range(v.shape[1] // HEAD_LANES):
        parts += [vt[hd * HEAD_LANES:(hd + 1) * HEAD_LANES], ones]
    return jnp.concatenate(parts, axis=0).astype(BF16)


def _proj_body(x_ref, g_ref, w_ref, cs_ref, qn_ref, wq_ref, kn_ref, wkv_ref,
               cosq_ref, sinq_ref, cosk_ref, sink_ref,
               a_ref, avt_ref, b_ref, bvt_ref, d_ref, dvt_ref, cq_ref, ck_ref, cvt_ref):
    x = x_ref[...]
    ms = jnp.mean(x * x, axis=-1, keepdims=True)
    h = (x * lax.rsqrt(ms + RMS_EPS) * g_ref[...]).astype(BF16)
    acc = jnp.dot(h, w_ref[...], preferred_element_type=F32)
    sc = acc[:, :N_SCALED] * cs_ref[...]
    a_ref[...] = sc[:, 0:512].astype(BF16)
    avt_ref[...] = _vt_with_ones(sc[:, 512:768])
    b_ref[...] = sc[:, 768:1280].astype(BF16)
    bvt_ref[...] = _vt_with_ones(sc[:, 1280:1536])
    d_ref[...] = sc[:, 1536:1920].astype(BF16)
    dvt_ref[...] = _vt_with_ones(sc[:, 1920:2048])

    cq = acc[:, C_CQ:C_CKV]
    hq = (cq * lax.rsqrt(jnp.mean(cq * cq, axis=-1, keepdims=True) + RMS_EPS) * qn_ref[...]).astype(BF16)
    yq = jnp.dot(hq, wq_ref[...], preferred_element_type=F32)
    cosq, sinq = cosq_ref[...], sinq_ref[...]
    c_mla = (MLA_NOPE + MLA_ROPE) ** -0.5 * LOG2E
    for hd in range(MLA_HEADS):
        main = yq[:, hd * 256:(hd + 1) * 256]
        sw = yq[:, 1024 + hd * LANES:1024 + (hd + 1) * LANES]
        cq_ref[:, hd * 256:hd * 256 + LANES] = (main[:, :LANES] * cosq + sw * sinq).astype(BF16)
        cq_ref[:, hd * 256 + LANES:(hd + 1) * 256] = (main[:, LANES:] * c_mla).astype(BF16)

    ckv = acc[:, C_CKV:C_KPE]
    hk = (ckv * lax.rsqrt(jnp.mean(ckv * ckv, axis=-1, keepdims=True) + RMS_EPS) * kn_ref[...]).astype(BF16)
    kv = jnp.dot(hk, wkv_ref[...], preferred_element_type=F32)
    kr = (acc[:, C_KPE:C_KPE_SW] * cosk_ref[...] + acc[:, C_KPE_SW:N_MAIN] * sink_ref[...]).astype(BF16)
    for grp in range(2):
        ck_ref[:, grp * 256:grp * 256 + LANES] = kr
        ck_ref[:, grp * 256 + LANES:(grp + 1) * 256] = kv[:, grp * LANES:(grp + 1) * LANES].astype(BF16)
    cvt_ref[...] = _vt_with_ones(kv[:, 256:512])


def _proj(x2d, g, w_main, colscale, qn, wq_big, kn, wkv, tabs, seq):
    t = x2d.shape[0]
    tm = _pick(seq, 512)
    nsb = seq // tm
    full = lambda shape: pl.BlockSpec(shape, lambda i: (0,) * len(shape))
    tab = pl.BlockSpec((tm, LANES), lambda i: (i % nsb, 0))
    row = lambda w: pl.BlockSpec((tm, w), lambda i: (i, 0))
    vt = lambda w: pl.BlockSpec((None, w, tm), lambda i: (i // nsb, 0, i % nsb))
    row_sds = lambda w: jax.ShapeDtypeStruct((t, w), BF16)
    vt_sds = lambda w: jax.ShapeDtypeStruct((t // seq, w, seq), BF16)
    return pl.pallas_call(
        _proj_body,
        grid=(t // tm,),
        in_specs=[row(D_MODEL), full((1, D_MODEL)), full((D_MODEL, N_MAIN)), full((1, N_SCALED)),
                  full((1, MLA_Q_LORA)), full((MLA_Q_LORA, 1536)), full((1, MLA_KV_LORA)),
                  full((MLA_KV_LORA, 512)), tab, tab, tab, tab],
        out_specs=[row(512), vt(4 * VT_ROWS), row(512), vt(4 * VT_ROWS), row(384), vt(2 * VT_ROWS),
                   row(1024), row(512), vt(4 * VT_ROWS)],
        out_shape=[row_sds(512), vt_sds(4 * VT_ROWS), row_sds(512), vt_sds(4 * VT_ROWS), row_sds(384),
                   vt_sds(2 * VT_ROWS), row_sds(1024), row_sds(512), vt_sds(4 * VT_ROWS)],
        compiler_params=_params(("parallel",)),
        name="proj",
    )(x2d, g, w_main, colscale, qn, wq_big, kn, wkv, *tabs)


def _attn_pipeline(n_soft, n_steps, scores, shift, load_vt, m0, s_ref, acc_ref):
    def stage_a(t, ss, dst_ref, m_cur):
        m_next, alpha = [], []
        for i in range(n_soft):
            m_new = jnp.maximum(m_cur[i], jnp.max(ss[i], axis=0, keepdims=True) + shift(i, t))
            dst_ref[i] = ss[i]
            m_next.append(m_new)
            alpha.append(jnp.exp2(m_cur[i] - m_new))
        return tuple(m_next), tuple(alpha)

    def stage_b(t, src_ref, carry):
        m_cur, alpha = carry
        for i in range(n_soft):
            p = jnp.exp2(src_ref[i] - (m_cur[i] - shift(i, t)))
            acc_ref[i] = alpha[i] * acc_ref[i] + jnp.dot(load_vt(t, i), p.astype(BF16),
                                                         preferred_element_type=F32)

    def half(t, src_ref, dst_ref, carry):
        ss = [scores(i, t + 1, False) for i in range(n_soft)]
        stage_b(t, src_ref, carry)
        return stage_a(t + 1, ss, dst_ref, carry[0])

    def body(u, carry):
        carry = half(2 * u, s_ref[0], s_ref[1], carry)
        return half(2 * u + 1, s_ref[1], s_ref[0], carry)

    carry = stage_a(0, [scores(i, 0, True) for i in range(n_soft)], s_ref[0], m0)
    carry = lax.fori_loop(0, n_steps // 2, body, carry)

    @pl.when(n_steps % 2 == 1)
    def _():
        stage_b(n_steps, s_ref[1], half(n_steps - 1, s_ref[0], s_ref[1], carry))

    @pl.when(n_steps % 2 == 0)
    def _():
        stage_b(n_steps, s_ref[0], carry)


def _row_mask(shape, lo, hi):
    row = lax.broadcasted_iota(jnp.int32, shape, 0)
    return jnp.logical_and(row >= lo, row < hi)


def _lane_mask(shape, lo, hi):
    lane = lax.broadcasted_iota(jnp.int32, shape, len(shape) - 1)
    return jnp.logical_and(lane >= lo, lane < hi)


def _normalized(acc_ref, i):
    return acc_ref[i, 0:HEAD_LANES, :] / acc_ref[i, HEAD_LANES:HEAD_LANES + 1, :]


def _diff_body(sl_ref, q_ref, k_ref, vt_ref, mask_ref, kb_ref, lam_ref, sub_ref, o_ref,
               qs_ref, s0_ref, s1_ref, acc_ref, *, tq, lam_init):
    pair, qi = pl.program_id(1), pl.program_id(2)
    tk = tq
    q = q_ref[...].astype(F32)
    for hd in range(2):
        sel = jnp.where(_lane_mask(q.shape, N_BIAS_SPLIT * hd, N_BIAS_SPLIT * (hd + 1)), 1.0, 0.0).astype(BF16)
        for mp in range(2):
            lo = hd * HEAD_LANES + mp * DIFF_HD
            qs_ref[2 * hd + mp, :, 0:LANES] = jnp.where(_lane_mask(q.shape, lo, lo + DIFF_HD), q, 0.0).astype(BF16)
            qs_ref[2 * hd + mp, :, LANES:2 * LANES] = sel
    acc_ref[...] = jnp.zeros(acc_ref.shape, F32)

    def scores(i, t, first):
        off = pl.multiple_of((qi - t) * tk, tk)
        k = jnp.concatenate([k_ref[pl.ds(off, tk), :], kb_ref[...]], axis=1)
        s = lax.dot_general(k, qs_ref[i], _NT, preferred_element_type=F32)
        return s + mask_ref[...] if first else s

    def shift(i, t):
        return sl_ref[2 * pair + i // 2] * (-(t * tk)).astype(F32) if not isinstance(t, int) else 0.0

    def load_vt(t, i):
        return vt_ref[pl.ds((i // 2) * VT_ROWS, VT_ROWS), pl.ds(pl.multiple_of((qi - t) * tk, tk), tk)]

    m0 = tuple(jnp.full((1, tq), NEG_INF, F32) for _ in range(4))
    _attn_pipeline(4, qi, scores, shift, load_vt, m0, (s0_ref, s1_ref), acc_ref)

    lp = lam_ref[...]
    lam = (jnp.exp(jnp.sum(lp[0:1, :] * lp[1:2, :], axis=-1, keepdims=True))
           - jnp.exp(jnp.sum(lp[2:3, :] * lp[3:4, :], axis=-1, keepdims=True)) + lam_init)
    o_h = [_normalized(acc_ref, 2 * hd) - lam * _normalized(acc_ref, 2 * hd + 1) for hd in range(2)]
    o = jnp.concatenate(o_h, axis=0).T
    in_h0 = _lane_mask((tq, LANES), 0, HEAD_LANES)
    sq = o * o
    ms0 = jnp.sum(jnp.where(in_h0, sq, 0.0), axis=-1, keepdims=True)
    ms1 = jnp.sum(jnp.where(in_h0, 0.0, sq), axis=-1, keepdims=True)
    ms = jnp.where(in_h0, ms0, ms1) * (1.0 / HEAD_LANES)
    y = o * lax.rsqrt(ms + RMS_EPS) * sub_ref[...]
    o_ref[...] = (y * (1.0 - lam_init)).astype(o_ref.dtype)


N_BIAS_SPLIT = 3


def _alibi_key_columns(slopes_l2, tk):
    dj = jnp.arange(tk, dtype=F32)[None, :, None]
    val = jnp.asarray(slopes_l2, F32).reshape(-1, 1, 2) * dj
    cols = jnp.stack(_split_bf16(val, N_BIAS_SPLIT), axis=-1).reshape(val.shape[0], tk, 2 * N_BIAS_SPLIT)
    return jnp.pad(cols, ((0, 0), (0, 0), (0, LANES - 2 * N_BIAS_SPLIT)))


def _split_bf16(x, n):
    pieces = []
    for _ in range(n):
        bits = lax.bitcast_convert_type(x, jnp.uint32) & jnp.uint32(0xFFFF0000)
        head = lax.bitcast_convert_type(bits, F32)
        pieces.append(head.astype(BF16))
        x = x - head
    return pieces


def _diff_attention(a3, avt, slopes_l2, mask_t, kb, lam_p, sub128, layer_idx, tq):
    b, s, _ = a3.shape
    nq = s // tq
    lam_init = 0.8 - 0.6 * math.exp(-0.3 * layer_idx)
    body = functools.partial(_diff_body, tq=tq, lam_init=lam_init)
    grid_spec = pltpu.PrefetchScalarGridSpec(
        num_scalar_prefetch=1,
        grid=(b, 2, nq),
        in_specs=[
            pl.BlockSpec((None, tq, LANES), lambda bi, p, qi, sl: (bi, qi, p)),
            pl.BlockSpec((None, s, LANES), lambda bi, p, qi, sl: (bi, 0, 2 + p)),
            pl.BlockSpec((None, 2 * VT_ROWS, s), lambda bi, p, qi, sl: (bi, p, 0)),
            pl.BlockSpec((tq, tq), lambda bi, p, qi, sl: (0, 0)),
            pl.BlockSpec((None, tq, LANES), lambda bi, p, qi, sl: (p, 0, 0)),
            pl.BlockSpec((4, DIFF_HD), lambda bi, p, qi, sl: (0, 0)),
            pl.BlockSpec((1, LANES), lambda bi, p, qi, sl: (0, 0)),
        ],
        out_specs=pl.BlockSpec((None, tq, LANES), lambda bi, p, qi, sl: (bi, qi, p)),
        scratch_shapes=[pltpu.VMEM((4, tq, 2 * LANES), BF16), pltpu.VMEM((4, tq, tq), F32),
                        pltpu.VMEM((4, tq, tq), F32), pltpu.VMEM((4, VT_ROWS, tq), F32)],
    )
    return pl.pallas_call(
        body, grid_spec=grid_spec,
        out_shape=jax.ShapeDtypeStruct((b, s, 256), BF16),
        compiler_params=_params(("parallel", "parallel", "arbitrary")),
        name="diff_attn",
    )(slopes_l2, a3, a3, avt, mask_t, kb, lam_p, sub128)


def _mla_body(q_ref, k_ref, vt_ref, mask_ref, o_ref, s0_ref, s1_ref, acc_ref, *, tq):
    qi = pl.program_id(1)
    tk = tq
    acc_ref[...] = jnp.zeros(acc_ref.shape, F32)

    def scores(i, t, first):
        k = k_ref[pl.ds(pl.multiple_of((qi - t) * tk, tk), tk), (i // 2) * 256:(i // 2 + 1) * 256]
        s = lax.dot_general(k, q_ref[:, i * 256:(i + 1) * 256], _NT, preferred_element_type=F32)
        return s + mask_ref[...] if first else s

    def load_vt(t, i):
        return vt_ref[pl.ds(i * VT_ROWS, VT_ROWS), pl.ds(pl.multiple_of((qi - t) * tk, tk), tk)]

    m0 = tuple(jnp.full((1, tq), NEG_INF, F32) for _ in range(MLA_HEADS))
    _attn_pipeline(MLA_HEADS, qi, scores, lambda i, t: 0.0, load_vt, m0, (s0_ref, s1_ref), acc_ref)
    o_t = jnp.concatenate([_normalized(acc_ref, i) for i in range(MLA_HEADS)], axis=0)
    o_ref[...] = o_t.T.astype(o_ref.dtype)


def _mla_attention(cq3, ck3, cvt, mask_t, tq):
    b, s, _ = cq3.shape
    nq = s // tq
    return pl.pallas_call(
        functools.partial(_mla_body, tq=tq),
        grid=(b, nq),
        in_specs=[
            pl.BlockSpec((None, tq, MLA_HEADS * 256), lambda bi, qi: (bi, qi, 0)),
            pl.BlockSpec((None, s, 512), lambda bi, qi: (bi, 0, 0)),
            pl.BlockSpec((None, MLA_HEADS * VT_ROWS, s), lambda bi, qi: (bi, 0, 0)),
            pl.BlockSpec((tq, tq), lambda bi, qi: (0, 0)),
        ],
        out_specs=pl.BlockSpec((None, tq, 256), lambda bi, qi: (bi, qi, 0)),
        out_shape=jax.ShapeDtypeStruct((b, s, 256), BF16),
        scratch_shapes=[pltpu.VMEM((MLA_HEADS, tq, tq), F32), pltpu.VMEM((MLA_HEADS, tq, tq), F32),
                        pltpu.VMEM((MLA_HEADS, VT_ROWS, tq), F32)],
        compiler_params=_params(("parallel", "arbitrary")),
        name="mla_attn",
    )(cq3, ck3, cvt, mask_t)


def _band_body(m0_ref, q_ref, k_ref, vt_ref, bias_ref, o_ref, qs_ref, s0_ref, s1_ref, acc_ref,
               *, tq, n_off, l_init, kv_slabs):
    qi = pl.program_id(1)
    tk = tq
    n_soft = 4
    for sl in range(2):
        q = q_ref[:, sl * LANES:(sl + 1) * LANES].astype(F32)
        for g in range(2):
            qs_ref[2 * sl + g] = jnp.where(_lane_mask(q.shape, g * HEAD_LANES, (g + 1) * HEAD_LANES), q, 0.0
                                           ).astype(BF16)
    for i in range(n_soft):
        acc_ref[i] = jnp.where(_row_mask((VT_ROWS, tq), 0, HEAD_LANES), 0.0, l_init).astype(F32)

    kv_of = lambda i: (i // 2) * (kv_slabs - 1)

    def scores(i, t, first):
        k = k_ref[pl.ds(pl.multiple_of((qi - t) * tk, tk), tk), kv_of(i) * LANES:(kv_of(i) + 1) * LANES]
        return lax.dot_general(k, qs_ref[i], _NT, preferred_element_type=F32) + bias_ref[i // 2, i % 2, t]

    def load_vt(t, i):
        return vt_ref[pl.ds((2 * kv_of(i) + i % 2) * VT_ROWS, VT_ROWS),
                      pl.ds(pl.multiple_of((qi - t) * tk, tk), tk)]

    m0 = tuple(jnp.full((1, tq), m0_ref[i], F32) for i in range(n_soft))
    _attn_pipeline(n_soft, jnp.minimum(qi, n_off - 1), scores, lambda i, t: 0.0, load_vt, m0,
                   (s0_ref, s1_ref), acc_ref)
    o_t = jnp.concatenate([_normalized(acc_ref, i) for i in range(n_soft)], axis=0)
    o_ref[...] = o_t.T.astype(o_ref.dtype)


def _band_attention(x3, xvt, k_blk, kv_slabs, bias, m0, l_init, tq, name):
    b, s, _ = x3.shape
    nq = s // tq
    n_off = bias.shape[2]
    n_soft = 4
    grid_spec = pltpu.PrefetchScalarGridSpec(
        num_scalar_prefetch=1,
        grid=(b, nq),
        in_specs=[
            pl.BlockSpec((None, tq, 2 * LANES), lambda bi, qi, m0r: (bi, qi, 0)),
            pl.BlockSpec((None, s, kv_slabs * LANES), lambda bi, qi, m0r: (bi, 0, k_blk)),
            pl.BlockSpec((None, kv_slabs * 2 * VT_ROWS, s), lambda bi, qi, m0r: (bi, 0, 0)),
            pl.BlockSpec((2, 2, n_off, tq, tq), lambda bi, qi, m0r: (0, 0, 0, 0, 0),
                         pipeline_mode=pl.Buffered(1)),
        ],
        out_specs=pl.BlockSpec((None, tq, 2 * LANES), lambda bi, qi, m0r: (bi, qi, 0)),
        scratch_shapes=[pltpu.VMEM((n_soft, tq, LANES), BF16), pltpu.VMEM((n_soft, tq, tq), F32),
                        pltpu.VMEM((n_soft, tq, tq), F32), pltpu.VMEM((n_soft, VT_ROWS, tq), F32)],
    )
    return pl.pallas_call(
        functools.partial(_band_body, tq=tq, n_off=n_off, l_init=l_init, kv_slabs=kv_slabs),
        grid_spec=grid_spec,
        out_shape=jax.ShapeDtypeStruct((b, s, 256), BF16),
        compiler_params=_params(("parallel", "arbitrary")),
        name=name,
    )(m0, x3, x3, xvt, bias)


def _band_bias(slopes, tq, n_off, mult_fn):
    di = jnp.arange(tq, dtype=jnp.int32)[None, :]
    dj = jnp.arange(tq, dtype=jnp.int32)[:, None]
    off = jnp.arange(n_off, dtype=jnp.int32)[:, None, None]
    delta = off * tq + di - dj
    mult = mult_fn(delta)
    sl = jnp.asarray(slopes, F32)[:, None, None, None] * LOG2E
    val = -sl * delta.astype(F32)[None] + jnp.log2(jnp.maximum(mult, 1).astype(F32))[None]
    return jnp.where((mult > 0)[None], val, NEG_INF)


def _dil_mult(delta):
    m = jnp.zeros(delta.shape, jnp.int32)
    for window, d in DIL_PATTERNS:
        m = m + ((delta >= 0) & (delta <= window) & (delta % d == 0)).astype(jnp.int32)
    return m


def _swa_mult(delta):
    return ((delta >= 0) & (delta <= SWA_WINDOW - 1)).astype(jnp.int32)


def _outproj_body(*refs, with_router):
    if with_router:
        (oa, ob, oc, od, w_ref, x_ref, g_ref, rh_ref, rl_ref, x1_ref, h2_ref, lg_ref) = refs
    else:
        (oa, ob, oc, od, w_ref, x_ref, g_ref, x1_ref, h2_ref) = refs
    acc = x_ref[...]
    for i, o in enumerate((oa, ob, oc, od)):
        acc = acc + jnp.dot(o[...], w_ref[i], preferred_element_type=F32)
    x1_ref[...] = acc
    h2 = acc * lax.rsqrt(jnp.mean(acc * acc, axis=-1, keepdims=True) + RMS_EPS) * g_ref[...]
    hi = h2.astype(BF16)
    if with_router:
        for c in range(SC_CHUNKS):
            h2_ref[c] = h2[:, c * SC_COLS:(c + 1) * SC_COLS]
        lo = (h2 - hi.astype(F32)).astype(BF16)
        lg_ref[...] = (jnp.dot(hi, rh_ref[...], preferred_element_type=F32)
                       + (jnp.dot(hi, rl_ref[...], preferred_element_type=F32)
                          + jnp.dot(lo, rh_ref[...], preferred_element_type=F32)))
    else:
        h2_ref[...] = hi


def _outproj(oa, ob, oc, od, w4, x2d, g, router_hl=None):
    t = x2d.shape[0]
    tm = _pick(t, 512)
    with_router = router_hl is not None
    full = lambda shape: pl.BlockSpec(shape, lambda i: (0,) * len(shape))
    row = lambda w: pl.BlockSpec((tm, w), lambda i: (i, 0))
    in_specs = [row(256)] * 4 + [full((4, 256, D_MODEL)), row(D_MODEL), full((1, D_MODEL))]
    args = [oa, ob, oc, od, w4, x2d, g]
    out_specs = [row(D_MODEL), row(D_MODEL)]
    out_shape = [jax.ShapeDtypeStruct((t, D_MODEL), F32), jax.ShapeDtypeStruct((t, D_MODEL), BF16)]
    if with_router:
        out_specs[1] = pl.BlockSpec((SC_CHUNKS, tm, SC_COLS), lambda i: (0, i, 0))
        out_shape[1] = jax.ShapeDtypeStruct((SC_CHUNKS, t, SC_COLS), F32)
        in_specs += [full((D_MODEL, LANES))] * 2
        args += list(router_hl)
        out_specs.append(row(LANES))
        out_shape.append(jax.ShapeDtypeStruct((t, LANES), F32))
    return pl.pallas_call(
        functools.partial(_outproj_body, with_router=with_router),
        grid=(t // tm,), in_specs=in_specs, out_specs=out_specs, out_shape=out_shape,
        compiler_params=_params(("parallel",)),
        name="outproj",
    )(*args)


FF_CHUNK = 512


def _swiglu(x, w1_ref, w3_ref, w2_ref):
    tf = w1_ref.shape[1]
    total = None
    for lo in range(0, tf, FF_CHUNK):
        hi = min(lo + FF_CHUNK, tf)
        a = jnp.dot(x, w1_ref[:, lo:hi], preferred_element_type=F32)
        b = jnp.dot(x, w3_ref[:, lo:hi], preferred_element_type=F32)
        mid = (a * jax.nn.sigmoid(a) * b).astype(BF16)
        part = jnp.dot(mid, w2_ref[lo:hi, :], preferred_element_type=F32)
        total = part if total is None else total + part
    return total


def _ffn_body(h_ref, w1_ref, w3_ref, w2_ref, x_ref, o_ref):
    o_ref[...] = x_ref[...] + _swiglu(h_ref[...], w1_ref, w3_ref, w2_ref)


def _ffn(h2, w1, w3, w2, x1):
    t = h2.shape[0]
    tm = _pick(t, 512)
    resident = lambda shape: pl.BlockSpec(shape, lambda i: (0, 0), pipeline_mode=pl.Buffered(1))
    return pl.pallas_call(
        _ffn_body,
        grid=(t // tm,),
        in_specs=[pl.BlockSpec((tm, D_MODEL), lambda i: (i, 0)),
                  resident((D_MODEL, D_FF)), resident((D_MODEL, D_FF)), resident((D_FF, D_MODEL)),
                  pl.BlockSpec((tm, D_MODEL), lambda i: (i, 0))],
        out_specs=pl.BlockSpec((tm, D_MODEL), lambda i: (i, 0)),
        out_shape=jax.ShapeDtypeStruct((t, D_MODEL), F32),
        compiler_params=_params(("parallel",)),
        name="ffn",
    )(h2, w1, w3, w2, x1)


def _route_body(lg_ref, tri_ref, info_ref, cnt_ref, carry_ref):
    i = pl.program_id(0)

    @pl.when(i == 0)
    def _():
        carry_ref[...] = jnp.zeros(carry_ref.shape, F32)

    lg = lg_ref[...]
    lane = lax.broadcasted_iota(jnp.int32, lg.shape, 1)
    lg = jnp.where(lane < N_EXPERTS, lg, -jnp.inf)
    m1 = jnp.max(lg, axis=-1, keepdims=True)
    i1 = jnp.min(jnp.where(lg == m1, lane, LANES), axis=-1, keepdims=True)
    oh1 = lane == i1
    lg2 = jnp.where(oh1, -jnp.inf, lg)
    m2 = jnp.max(lg2, axis=-1, keepdims=True)
    i2 = jnp.min(jnp.where(lg2 == m2, lane, LANES), axis=-1, keepdims=True)
    oh2 = lane == i2
    e2 = jnp.exp(m2 - m1)
    g1 = 1.0 / (1.0 + e2)
    g2 = e2 / (1.0 + e2)
    oh = jnp.where(jnp.logical_or(oh1, oh2), 1.0, 0.0)
    rank = jnp.dot(tri_ref[...], oh.astype(BF16), preferred_element_type=F32) + carry_ref[...]
    r1 = jnp.sum(jnp.where(oh1, rank, 0.0), axis=-1, keepdims=True)
    r2 = jnp.sum(jnp.where(oh2, rank, 0.0), axis=-1, keepdims=True)
    carry_ref[...] += jnp.sum(oh, axis=0, keepdims=True)
    cnt_ref[...] = carry_ref[...]
    info = jnp.zeros(lg.shape, F32)
    for c, val in enumerate((i1.astype(F32), i2.astype(F32), r1, r2, g1, g2)):
        info = jnp.where(lane == c, val, info)
    info_ref[...] = info


def _route(logits):
    t = logits.shape[0]
    tm = _pick(t, 512)
    tri = (jnp.arange(tm)[:, None] > jnp.arange(tm)[None, :]).astype(BF16)
    return pl.pallas_call(
        _route_body,
        grid=(t // tm,),
        in_specs=[pl.BlockSpec((tm, LANES), lambda i: (i, 0)), pl.BlockSpec((tm, tm), lambda i: (0, 0))],
        out_specs=[pl.BlockSpec((tm, LANES), lambda i: (i, 0)), pl.BlockSpec((1, LANES), lambda i: (0, 0))],
        out_shape=[jax.ShapeDtypeStruct((t, LANES), F32), jax.ShapeDtypeStruct((1, LANES), F32)],
        scratch_shapes=[pltpu.VMEM((1, LANES), F32)],
        compiler_params=_params(("arbitrary",)),
        name="route",
    )(logits, tri)


def _moe_ffn_body(be_ref, bv_ref, x_ref, w1_ref, w3_ref, w2_ref, o_ref, acc_ref):
    i, f = pl.program_id(0), pl.program_id(1)
    last = pl.num_programs(1) - 1
    valid = bv_ref[i]

    @pl.when(valid > 0)
    def _():
        x = jnp.concatenate([x_ref[c] for c in range(SC_CHUNKS)], axis=1)
        x = jnp.where(lax.broadcasted_iota(jnp.int32, x.shape, 0) < valid, x, 0.0).astype(BF16)
        contrib = _swiglu(x, w1_ref, w3_ref, w2_ref)

        @pl.when(f == 0)
        def _():
            acc_ref[...] = contrib

        @pl.when(f > 0)
        def _():
            acc_ref[...] += contrib

        @pl.when(f == last)
        def _():
            for c in range(SC_CHUNKS):
                o_ref[c] = acc_ref[:, c * SC_COLS:(c + 1) * SC_COLS]

    @pl.when(jnp.logical_and(valid == 0, f == last))
    def _():
        o_ref[...] = jnp.zeros(o_ref.shape, o_ref.dtype)


def _moe_ffn(xs, blk_expert, blk_valid, w1, w3, w2, bm):
    rows = xs.shape[1]
    tf = D_FF_EXPERT // 2
    chunked = pl.BlockSpec((SC_CHUNKS, bm, SC_COLS), lambda i, f, be, bv: (0, i, 0))
    grid_spec = pltpu.PrefetchScalarGridSpec(
        num_scalar_prefetch=2,
        grid=(rows // bm, D_FF_EXPERT // tf),
        in_specs=[chunked,
                  pl.BlockSpec((None, D_MODEL, tf), lambda i, f, be, bv: (be[i], 0, f)),
                  pl.BlockSpec((None, D_MODEL, tf), lambda i, f, be, bv: (be[i], 0, f)),
                  pl.BlockSpec((None, tf, D_MODEL), lambda i, f, be, bv: (be[i], f, 0))],
        out_specs=chunked,
        scratch_shapes=[pltpu.VMEM((bm, D_MODEL), F32)],
    )
    return pl.pallas_call(
        _moe_ffn_body, grid_spec=grid_spec,
        out_shape=jax.ShapeDtypeStruct(xs.shape, F32),
        compiler_params=_params(("parallel", "arbitrary")),
        name="moe_ffn",
    )(blk_expert, blk_valid, xs, w1, w3, w2)


def _final_body(*refs, with_moe):
    if with_moe:
        x_ref, y0_ref, y1_ref, info_ref, g_ref, o_ref = refs
        info = info_ref[...]
        y0 = jnp.concatenate([y0_ref[c] for c in range(SC_CHUNKS)], axis=1)
        y1 = jnp.concatenate([y1_ref[c] for c in range(SC_CHUNKS)], axis=1)
        x = x_ref[...] + (info[:, 4:5] * y0 + info[:, 5:6] * y1)
    else:
        x_ref, g_ref, o_ref = refs
        x = x_ref[...]
    o_ref[...] = x * lax.rsqrt(jnp.mean(x * x, axis=-1, keepdims=True) + RMS_EPS) * g_ref[...]


def _final(x2d, g, moe=None):
    t = x2d.shape[0]
    tm = _pick(t, 512)
    row = lambda w: pl.BlockSpec((tm, w), lambda i: (i, 0))
    in_specs, args = [row(D_MODEL)], [x2d]
    if moe is not None:
        yg, info = moe
        slot = lambda k: pl.BlockSpec((None, SC_CHUNKS, tm, SC_COLS), lambda i: (k, 0, i, 0))
        in_specs += [slot(0), slot(1), row(LANES)]
        args += [yg, yg, info]
    in_specs.append(pl.BlockSpec((1, D_MODEL), lambda i: (0, 0)))
    args.append(g)
    return pl.pallas_call(
        functools.partial(_final_body, with_moe=moe is not None),
        grid=(t // tm,), in_specs=in_specs, out_specs=row(D_MODEL),
        out_shape=jax.ShapeDtypeStruct((t, D_MODEL), F32),
        compiler_params=_params(("parallel",)),
        name="final",
    )(*args)


def _prep_attn_weights(w_in, w_out, w_uq, w_ukv):
    widths = (256, 256, 256, 256, 256, 256, MLA_Q_LORA, MLA_KV_LORA, MLA_ROPE, 256, 128, 128)
    offs = np.concatenate([[0], np.cumsum(widths)])
    qa, ka, va, qb, kb, vb, cq, ckv, kpe, qd, kd, vd = [w_in[:, offs[i]:offs[i + 1]] for i in range(12)]
    hperm = np.array([0, 2, 1, 3])
    qd = qd.reshape(D_MODEL, 4, SWA_HD)[:, hperm].reshape(D_MODEL, 256)
    pad = jnp.zeros((D_MODEL, LANES - MLA_ROPE), w_in.dtype)
    half = MLA_ROPE // 2
    kpe_sw = jnp.concatenate([kpe[:, half:], kpe[:, :half]], axis=1)
    w_main = jnp.concatenate([qa, ka, va, qb, kb, vb, qd, kd, vd, cq, ckv, kpe, pad, kpe_sw, pad], axis=1)

    uq = w_uq.reshape(MLA_Q_LORA, MLA_HEADS, MLA_NOPE + MLA_ROPE)
    z = lambda n: jnp.zeros((MLA_Q_LORA, n), w_uq.dtype)
    mains, sws = [], []
    for hd in range(MLA_HEADS):
        nope, rope = uq[:, hd, :MLA_NOPE], uq[:, hd, MLA_NOPE:]
        nope128 = jnp.concatenate([nope, z(64)] if hd % 2 == 0 else [z(64), nope], axis=1)
        mains.append(jnp.concatenate([rope, z(LANES - MLA_ROPE), nope128], axis=1))
        sws.append(jnp.concatenate([rope[:, half:], rope[:, :half], z(LANES - MLA_ROPE)], axis=1))
    wq_big = jnp.concatenate(mains + sws, axis=1)

    ukv = w_ukv.reshape(MLA_KV_LORA, MLA_HEADS, MLA_NOPE + MLA_V)
    wkv = jnp.concatenate([ukv[:, :, :MLA_NOPE].reshape(MLA_KV_LORA, 256),
                           ukv[:, :, MLA_NOPE:].reshape(MLA_KV_LORA, 256)], axis=1)

    wo = w_out.reshape(4, 256, D_MODEL)
    wo_d = wo[3].reshape(4, SWA_HD, D_MODEL)[hperm].reshape(256, D_MODEL)
    w4 = jnp.stack([wo[0], wo[1], wo[2], wo_d])
    return w_main.astype(BF16), wq_big.astype(BF16), wkv.astype(BF16), w4.astype(BF16)


def _rope_tables(s):
    inv = ROPE_THETA ** (-jnp.arange(0, MLA_ROPE, 2, dtype=F32) / MLA_ROPE)
    ang = jnp.arange(s, dtype=F32)[:, None] * inv[None, :]
    cos, sin = jnp.cos(ang), jnp.sin(ang)
    z = jnp.zeros((s, LANES - MLA_ROPE), F32)
    cos128 = jnp.concatenate([cos, cos, z], axis=1)
    sin128 = jnp.concatenate([-sin, sin, z], axis=1)
    c_mla = (MLA_NOPE + MLA_ROPE) ** -0.5 * LOG2E
    return cos128 * c_mla, sin128 * c_mla, cos128, sin128


def _col_scale():
    ca = DIFF_HD ** -0.5 * LOG2E
    cb = DIL_HD ** -0.5 * LOG2E
    cd = SWA_HD ** -0.5 * LOG2E
    v = np.ones((1, N_SCALED), np.float32)
    v[0, 0:256] = ca
    v[0, 768:1024] = cb
    v[0, 1536:1792] = cd
    return jnp.asarray(v)


def _moe_layer(h2, logits, w1, w3, w2, bm):
    t = logits.shape[0]
    info, counts = _route(logits)
    cnt = counts[0, :N_EXPERTS].astype(jnp.int32)
    padded = (cnt + bm - 1) // bm * bm
    pends = jnp.cumsum(padded)
    pstarts = pends - padded
    e12 = info[:, 0:2].astype(jnp.int32)
    pos = pstarts[e12] + info[:, 2:4].astype(jnp.int32)
    nb = (t * TOP_K) // bm + N_EXPERTS
    rows = nb * bm
    blk_start = jnp.arange(nb, dtype=jnp.int32) * bm
    blk_expert = jnp.minimum(jnp.sum((pends[None, :] <= blk_start[:, None]).astype(jnp.int32), axis=1),
                             N_EXPERTS - 1)
    blk_valid = jnp.clip(pstarts[blk_expert] + cnt[blk_expert] - blk_start, 0, bm).astype(jnp.int32)
    sub = (jnp.arange(SC_CHUNKS, dtype=jnp.int32) * rows)[None, :, None] + pos.T[:, None, :]
    xs = _sc_scatter_rows(h2.reshape(SC_CHUNKS * t, SC_COLS),
                          [sub[k].reshape(1, SC_CHUNKS * t) for k in range(TOP_K)], SC_CHUNKS * rows)
    ys = _moe_ffn(xs.reshape(SC_CHUNKS, rows, SC_COLS), blk_expert.astype(jnp.int32), blk_valid, w1, w3, w2, bm)
    yg = _sc_gather_rows(ys.reshape(SC_CHUNKS * rows, SC_COLS), sub.reshape(1, TOP_K * SC_CHUNKS * t))
    return yg.reshape(TOP_K, SC_CHUNKS, t, SC_COLS), info


SC_WINDOW = 128
SC_COLS = 256
SC_CHUNKS = D_MODEL // SC_COLS


def _sc_mesh():
    return plsc.VectorSubcoreMesh(core_axis_name="c", subcore_axis_name="s")


def _sc_scatter_rows(x, idx_list, rows):
    t, d = x.shape
    n = len(idx_list)

    @functools.partial(pl.kernel, out_type=jax.ShapeDtypeStruct((rows, d), x.dtype), mesh=_sc_mesh(),
                       scratch_types=[])
    def scatter_kernel(x_hbm, *rest):
        idx_hbms, o_hbm = rest[:n], rest[n]

        def body(x_vmem, *idx_vmems):
            for iv in idx_vmems:
                pltpu.sync_copy(x_vmem, o_hbm.at[iv.at[0]])

        pltpu.emit_pipeline(
            body, grid=(t // SC_WINDOW,),
            in_specs=[pl.BlockSpec((SC_WINDOW, d), lambda i: (i, 0))]
            + [pl.BlockSpec((1, SC_WINDOW), lambda i: (0, i))] * n,
            out_specs=[], core_axis_name=("c", "s"), dimension_semantics=(pltpu.PARALLEL,),
        )(x_hbm, *idx_hbms)

    return scatter_kernel(x, *idx_list)


def _sc_gather_rows(y, idx):
    m = idx.shape[1]
    d = y.shape[1]

    @functools.partial(pl.kernel, out_type=jax.ShapeDtypeStruct((m, d), y.dtype), mesh=_sc_mesh(),
                       scratch_types=[])
    def gather_kernel(y_hbm, idx_hbm, o_hbm):
        def body(idx_vmem, o_vmem):
            pltpu.sync_copy(y_hbm.at[idx_vmem.at[0]], o_vmem)

        pltpu.emit_pipeline(
            body, grid=(m // SC_WINDOW,),
            in_specs=[pl.BlockSpec((1, SC_WINDOW), lambda i: (0, i))],
            out_specs=[pl.BlockSpec((SC_WINDOW, d), lambda i: (i, 0))],
            core_axis_name=("c", "s"), dimension_semantics=(pltpu.PARALLEL,),
        )(idx_hbm, o_hbm)

    return gather_kernel(y, idx)


def kernel(x, attn_norm, w_in, w_out, diff_lambda, diff_subln, mla_q_norm, mla_w_uq, mla_kv_norm, mla_w_ukv,
           swa_sinks, ffn_norm, ffn_w1, ffn_w3, ffn_w2, moe_router, moe_w1, moe_w3, moe_w2, final_norm):
    bsz, seq, _ = x.shape
    t = bsz * seq
    depth = w_in.shape[0]
    slopes_a, slopes_b, slopes_d = _alibi_slopes()
    tq_dense = _pick(seq, 512)
    tq_band = _pick(seq, 256)

    tabs = _rope_tables(seq)
    colscale = _col_scale()
    di = jnp.arange(tq_dense)
    causal_mask_t = jnp.where(di[:, None] <= di[None, :], 0.0, NEG_INF).astype(F32)
    tq_diff = tq_dense
    alibi_kb = _alibi_key_columns(slopes_a * LOG2E, tq_diff)
    tq_dil = _pick(seq, 512)
    n_off_dil = min((max(w for w, _ in DIL_PATTERNS) + tq_dil - 1) // tq_dil + 1, seq // tq_dil)
    n_off_swa = min((SWA_WINDOW - 1 + tq_band - 1) // tq_band + 1, seq // tq_band)
    dil_bias = _band_bias(slopes_b, tq_dil, n_off_dil, _dil_mult).reshape(2, 2, n_off_dil, tq_dil, tq_dil)
    swa_bias = _band_bias(slopes_d[np.array([0, 2, 1, 3])], tq_band, n_off_swa, _swa_mult
                          ).reshape(2, 2, n_off_swa, tq_band, tq_band)
    slopes_a_l2 = jnp.asarray(slopes_a * LOG2E, F32)
    dil_m0 = jnp.full((4,), NEG_INF, F32)

    x2d = x.reshape(t, D_MODEL)
    for l in range(depth):
        w_main, wq_big, wkv, w4 = _prep_attn_weights(w_in[l], w_out[l], mla_w_uq[l], mla_w_ukv[l])
        a, avt, b, bvt, d, dvt, cq, ck, cvt = _proj(x2d, attn_norm[l][None], w_main, colscale, mla_q_norm[l][None],
                                                    wq_big, mla_kv_norm[l][None], wkv, tabs, seq)
        a3, b3, d3 = (v.reshape(bsz, seq, -1) for v in (a, b, d))
        sub128 = jnp.concatenate([diff_subln[l], diff_subln[l]])[None]
        o_a = _diff_attention(a3, avt, slopes_a_l2, causal_mask_t[:tq_diff, :tq_diff], alibi_kb, diff_lambda[l],
                              sub128, l, tq_diff)
        o_b = _band_attention(b3, bvt, 1, 2, dil_bias, dil_m0, 0.0, tq_dil, "dil_attn")
        o_c = _mla_attention(cq.reshape(bsz, seq, -1), ck.reshape(bsz, seq, -1), cvt, causal_mask_t, tq_dense)
        swa_m0 = (swa_sinks[l].astype(F32) * LOG2E)[np.array([0, 2, 1, 3])]
        o_d = _band_attention(d3, dvt, 2, 1, swa_bias, swa_m0, 1.0, tq_band, "swa_attn")
        outs = [v.reshape(t, 256) for v in (o_a, o_b, o_c, o_d)]
        j = l // 2
        if l % 2 == 0:
            x1, h2 = _outproj(*outs, w4, x2d, ffn_norm[l][None])
            x2d = _ffn(h2, ffn_w1[j].astype(BF16), ffn_w3[j].astype(BF16), ffn_w2[j].astype(BF16), x1)
            moe = None
        else:
            r = jnp.pad(moe_router[j], ((0, 0), (0, LANES - N_EXPERTS)))
            r_hi, r_lo = _split_bf16(r, 2)
            x1, h2, logits = _outproj(*outs, w4, x2d, ffn_norm[l][None], (r_hi, r_lo))
            bm = _pick(t, 512)
            moe = _moe_layer(h2, logits, moe_w1[j].astype(BF16), moe_w3[j].astype(BF16),
                             moe_w2[j].astype(BF16), bm)
            x2d = x1
        if l < depth - 1 and moe is not None:
            raise NotImplementedError("MoE combine is fused into the final norm; MoE layer must be last")
    return _final(x2d, final_norm[None], moe).reshape(bsz, seq, D_MODEL)
```

```python
import functools
import math

import numpy as np
import jax
import jax.numpy as jnp
from jax import lax
from jax.experimental import pallas as pl
from jax.experimental.pallas import tpu as pltpu
from jax.experimental.pallas import tpu_sc as plsc

D_MODEL = 1024
DIFF_HEADS, DIFF_HD = 4, 32
DIL_HEADS, DIL_HD = 4, 64
DIL_PATTERNS = ((128, 1), (512, 4), (2048, 16))
MLA_HEADS, MLA_Q_LORA, MLA_KV_LORA, MLA_NOPE, MLA_ROPE, MLA_V = 4, 384, 128, 64, 32, 64
ROPE_THETA = 10000.0
SWA_HEADS, SWA_KV_HEADS, SWA_HD, SWA_WINDOW = 4, 2, 64, 128
D_FF = 2816
N_EXPERTS, TOP_K, D_FF_EXPERT = 8, 2, 3584
RMS_EPS = 1e-6
NEG_INF = -1e30
N_ALIBI = DIFF_HEADS + DIL_HEADS + SWA_HEADS

LOG2E = 1.4426950408889634
LANES = 128
HEAD_LANES = 64
VMEM_LIMIT = 56 * 1024 * 1024

F32 = jnp.float32
BF16 = jnp.bfloat16

_NT = (((1,), (1,)), ((), ()))


def _params(sem, vmem=VMEM_LIMIT):
    return pltpu.CompilerParams(dimension_semantics=sem, vmem_limit_bytes=vmem)


def _alibi_slopes():
    s = 2.0 ** (-8.0 * (np.arange(N_ALIBI) + 1) / N_ALIBI)
    return s[0::3], s[1::3], s[2::3]


def _pick(n, pref):
    t = min(pref, n)
    while n % t:
        t //= 2
    return t


N_SCALED = 2048
C_CQ, C_CKV, C_KPE, C_KPE_SW, N_MAIN = 2048, 2432, 2560, 2688, 2816


ONES_ROWS = 16
VT_ROWS = HEAD_LANES + ONES_ROWS


def _vt_with_ones(v):
    vt = v.T
    ones = jnp.ones((ONES_ROWS, v.shape[0]), F32)
    parts = []
    for hd in range(v.shape[1] // HEAD_LANES):
        parts += [vt[hd * HEAD_LANES:(hd + 1) * HEAD_LANES], ones]
    return jnp.concatenate(parts, axis=0).astype(BF16)


def _proj_body(x_ref, g_ref, w_ref, cs_ref, qn_ref, wq_ref, kn_ref, wkv_ref,
               cosq_ref, sinq_ref, cosk_ref, sink_ref,
               a_ref, avt_ref, b_ref, bvt_ref, d_ref, dvt_ref, cq_ref, ck_ref, cvt_ref):
    x = x_ref[...]
    ms = jnp.mean(x * x, axis=-1, keepdims=True)
    h = (x * lax.rsqrt(ms + RMS_EPS) * g_ref[...]).astype(BF16)
    acc = jnp.dot(h, w_ref[...], preferred_element_type=F32)
    sc = acc[:, :N_SCALED] * cs_ref[...]
    a_ref[...] = sc[:, 0:512].astype(BF16)
    avt_ref[...] = _vt_with_ones(sc[:, 512:768])
    b_ref[...] = sc[:, 768:1280].astype(BF16)
    bvt_ref[...] = _vt_with_ones(sc[:, 1280:1536])
    d_ref[...] = sc[:, 1536:1920].astype(BF16)
    dvt_ref[...] = _vt_with_ones(sc[:, 1920:2048])

    cq = acc[:, C_CQ:C_CKV]
    hq = (cq * lax.rsqrt(jnp.mean(cq * cq, axis=-1, keepdims=True) + RMS_EPS) * qn_ref[...]).astype(BF16)
    yq = jnp.dot(hq, wq_ref[...], preferred_element_type=F32)
    cosq, sinq = cosq_ref[...], sinq_ref[...]
    c_mla = (MLA_NOPE + MLA_ROPE) ** -0.5 * LOG2E
    for hd in range(MLA_HEADS):
        main = yq[:, hd * 256:(hd + 1) * 256]
        sw = yq[:, 1024 + hd * LANES:1024 + (hd + 1) * LANES]
        cq_ref[:, hd * 256:hd * 256 + LANES] = (main[:, :LANES] * cosq + sw * sinq).astype(BF16)
        cq_ref[:, hd * 256 + LANES:(hd + 1) * 256] = (main[:, LANES:] * c_mla).astype(BF16)

    ckv = acc[:, C_CKV:C_KPE]
    hk = (ckv * lax.rsqrt(jnp.mean(ckv * ckv, axis=-1, keepdims=True) + RMS_EPS) * kn_ref[...]).astype(BF16)
    kv = jnp.dot(hk, wkv_ref[...], preferred_element_type=F32)
    kr = (acc[:, C_KPE:C_KPE_SW] * cosk_ref[...] + acc[:, C_KPE_SW:N_MAIN] * sink_ref[...]).astype(BF16)
    for grp in range(2):
        ck_ref[:, grp * 256:grp * 256 + LANES] = kr
        ck_ref[:, grp * 256 + LANES:(grp + 1) * 256] = kv[:, grp * LANES:(grp + 1) * LANES].astype(BF16)
    cvt_ref[...] = _vt_with_ones(kv[:, 256:512])


def _proj(x2d, g, w_main, colscale, qn, wq_big, kn, wkv, tabs, seq):
    t = x2d.shape[0]
    tm = _pick(seq, 512)
    nsb = seq // tm
    full = lambda shape: pl.BlockSpec(shape, lambda i: (0,) * len(shape))
    tab = pl.BlockSpec((tm, LANES), lambda i: (i % nsb, 0))
    row = lambda w: pl.BlockSpec((tm, w), lambda i: (i, 0))
    vt = lambda w: pl.BlockSpec((None, w, tm), lambda i: (i // nsb, 0, i % nsb))
    row_sds = lambda w: jax.ShapeDtypeStruct((t, w), BF16)
    vt_sds = lambda w: jax.ShapeDtypeStruct((t // seq, w, seq), BF16)
    return pl.pallas_call(
        _proj_body,
        grid=(t // tm,),
        in_specs=[row(D_MODEL), full((1, D_MODEL)), full((D_MODEL, N_MAIN)), full((1, N_SCALED)),
                  full((1, MLA_Q_LORA)), full((MLA_Q_LORA, 1536)), full((1, MLA_KV_LORA)),
                  full((MLA_KV_LORA, 512)), tab, tab, tab, tab],
        out_specs=[row(512), vt(4 * VT_ROWS), row(512), vt(4 * VT_ROWS), row(384), vt(2 * VT_ROWS),
                   row(1024), row(512), vt(4 * VT_ROWS)],
        out_shape=[row_sds(512), vt_sds(4 * VT_ROWS), row_sds(512), vt_sds(4 * VT_ROWS), row_sds(384),
                   vt_sds(2 * VT_ROWS), row_sds(1024), row_sds(512), vt_sds(4 * VT_ROWS)],
        compiler_params=_params(("parallel",)),
        name="proj",
    )(x2d, g, w_main, colscale, qn, wq_big, kn, wkv, *tabs)


def _attn_pipeline(n_soft, n_steps, scores, shift, load_vt, m0, s_ref, acc_ref):
    def stage_a(t, ss, dst_ref, m_cur):
        m_next, alpha = [], []
        for i in range(n_soft):
            m_new = jnp.maximum(m_cur[i], jnp.max(ss[i], axis=0, keepdims=True) + shift(i, t))
            dst_ref[i] = ss[i]
            m_next.append(m_new)
            alpha.append(jnp.exp2(m_cur[i] - m_new))
        return tuple(m_next), tuple(alpha)

    def stage_b(t, src_ref, carry):
        m_cur, alpha = carry
        for i in range(n_soft):
            p = jnp.exp2(src_ref[i] - (m_cur[i] - shift(i, t)))
            acc_ref[i] = alpha[i] * acc_ref[i] + jnp.dot(load_vt(t, i), p.astype(BF16),
                                                         preferred_element_type=F32)

    def half(t, src_ref, dst_ref, carry):
        ss = [scores(i, t + 1, False) for i in range(n_soft)]
        stage_b(t, src_ref, carry)
        return stage_a(t + 1, ss, dst_ref, carry[0])

    def body(u, carry):
        carry = half(2 * u, s_ref[0], s_ref[1], carry)
        return half(2 * u + 1, s_ref[1], s_ref[0], carry)

    carry = stage_a(0, [scores(i, 0, True) for i in range(n_soft)], s_ref[0], m0)
    carry = lax.fori_loop(0, n_steps // 2, body, carry)

    @pl.when(n_steps % 2 == 1)
    def _():
        stage_b(n_steps, s_ref[1], half(n_steps - 1, s_ref[0], s_ref[1], carry))

    @pl.when(n_steps % 2 == 0)
    def _():
        stage_b(n_steps, s_ref[0], carry)


def _row_mask(shape, lo, hi):
    row = lax.broadcasted_iota(jnp.int32, shape, 0)
    return jnp.logical_and(row >= lo, row < hi)


def _lane_mask(shape, lo, hi):
    lane = lax.broadcasted_iota(jnp.int32, shape, len(shape) - 1)
    return jnp.logical_and(lane >= lo, lane < hi)


def _normalized(acc_ref, i):
    return acc_ref[i, 0:HEAD_LANES, :] / acc_ref[i, HEAD_LANES:HEAD_LANES + 1, :]


def _diff_body(sl_ref, q_ref, k_ref, vt_ref, mask_ref, kb_ref, lam_ref, sub_ref, o_ref,
               qs_ref, s0_ref, s1_ref, acc_ref, *, tq, lam_init):
    qi = pl.program_id(1)
    tk = tq
    n_soft = 2 * DIFF_HEADS
    head_of = lambda i: i // 2
    for pair in range(2):
        q = q_ref[:, pair * LANES:(pair + 1) * LANES].astype(F32)
        for hd in range(2):
            sel = jnp.where(_lane_mask(q.shape, N_BIAS_SPLIT * hd, N_BIAS_SPLIT * (hd + 1)), 1.0, 0.0).astype(BF16)
            for mp in range(2):
                i = 4 * pair + 2 * hd + mp
                lo = hd * HEAD_LANES + mp * DIFF_HD
                qs_ref[i, :, 0:LANES] = jnp.where(_lane_mask(q.shape, lo, lo + DIFF_HD), q, 0.0).astype(BF16)
                qs_ref[i, :, LANES:2 * LANES] = sel
    acc_ref[...] = jnp.zeros(acc_ref.shape, F32)

    def scores(i, t, first):
        off = pl.multiple_of((qi - t) * tk, tk)
        pair = i // 4
        k = jnp.concatenate([k_ref[pl.ds(off, tk), pair * LANES:(pair + 1) * LANES], kb_ref[pair]], axis=1)
        s = lax.dot_general(k, qs_ref[i], _NT, preferred_element_type=F32)
        return s + mask_ref[...] if first else s

    def shift(i, t):
        return sl_ref[head_of(i)] * (-(t * tk)).astype(F32) if not isinstance(t, int) else 0.0

    def load_vt(t, i):
        return vt_ref[pl.ds(head_of(i) * VT_ROWS, VT_ROWS), pl.ds(pl.multiple_of((qi - t) * tk, tk), tk)]

    m0 = tuple(jnp.full((1, tq), NEG_INF, F32) for _ in range(n_soft))
    _attn_pipeline(n_soft, qi, scores, shift, load_vt, m0, (s0_ref, s1_ref), acc_ref)

    lp = lam_ref[...]
    lam = (jnp.exp(jnp.sum(lp[0:1, :] * lp[1:2, :], axis=-1, keepdims=True))
           - jnp.exp(jnp.sum(lp[2:3, :] * lp[3:4, :], axis=-1, keepdims=True)) + lam_init)
    in_h0 = _lane_mask((tq, LANES), 0, HEAD_LANES)
    for pair in range(2):
        o_h = [_normalized(acc_ref, 4 * pair + 2 * hd) - lam * _normalized(acc_ref, 4 * pair + 2 * hd + 1)
               for hd in range(2)]
        o = jnp.concatenate(o_h, axis=0).T
        sq = o * o
        ms0 = jnp.sum(jnp.where(in_h0, sq, 0.0), axis=-1, keepdims=True)
        ms1 = jnp.sum(jnp.where(in_h0, 0.0, sq), axis=-1, keepdims=True)
        ms = jnp.where(in_h0, ms0, ms1) * (1.0 / HEAD_LANES)
        y = o * lax.rsqrt(ms + RMS_EPS) * sub_ref[...]
        o_ref[:, pair * LANES:(pair + 1) * LANES] = (y * (1.0 - lam_init)).astype(o_ref.dtype)


N_BIAS_SPLIT = 3


def _alibi_key_columns(slopes_l2, tk):
    dj = jnp.arange(tk, dtype=F32)[None, :, None]
    val = jnp.asarray(slopes_l2, F32).reshape(-1, 1, 2) * dj
    cols = jnp.stack(_split_bf16(val, N_BIAS_SPLIT), axis=-1).reshape(val.shape[0], tk, 2 * N_BIAS_SPLIT)
    return jnp.pad(cols, ((0, 0), (0, 0), (0, LANES - 2 * N_BIAS_SPLIT)))


def _split_bf16(x, n):
    pieces = []
    for _ in range(n):
        bits = lax.bitcast_convert_type(x, jnp.uint32) & jnp.uint32(0xFFFF0000)
        head = lax.bitcast_convert_type(bits, F32)
        pieces.append(head.astype(BF16))
        x = x - head
    return pieces


def _diff_attention(a3, avt, slopes_l2, mask_t, kb, lam_p, sub128, layer_idx, tq):
    b, s, _ = a3.shape
    nq = s // tq
    lam_init = 0.8 - 0.6 * math.exp(-0.3 * layer_idx)
    body = functools.partial(_diff_body, tq=tq, lam_init=lam_init)
    grid_spec = pltpu.PrefetchScalarGridSpec(
        num_scalar_prefetch=1,
        grid=(b, nq),
        in_specs=[
            pl.BlockSpec((None, tq, 2 * LANES), lambda bi, qi, sl: (bi, qi, 0)),
            pl.BlockSpec((None, s, 2 * LANES), lambda bi, qi, sl: (bi, 0, 1)),
            pl.BlockSpec((None, DIFF_HEADS * VT_ROWS, s), lambda bi, qi, sl: (bi, 0, 0)),
            pl.BlockSpec((tq, tq), lambda bi, qi, sl: (0, 0), pipeline_mode=pl.Buffered(1)),
            pl.BlockSpec((2, tq, LANES), lambda bi, qi, sl: (0, 0, 0), pipeline_mode=pl.Buffered(1)),
            pl.BlockSpec((4, DIFF_HD), lambda bi, qi, sl: (0, 0)),
            pl.BlockSpec((1, LANES), lambda bi, qi, sl: (0, 0)),
        ],
        out_specs=pl.BlockSpec((None, tq, 2 * LANES), lambda bi, qi, sl: (bi, qi, 0)),
        scratch_shapes=[pltpu.VMEM((2 * DIFF_HEADS, tq, 2 * LANES), BF16),
                        pltpu.VMEM((2 * DIFF_HEADS, tq, tq), F32), pltpu.VMEM((2 * DIFF_HEADS, tq, tq), F32),
                        pltpu.VMEM((2 * DIFF_HEADS, VT_ROWS, tq), F32)],
    )
    return pl.pallas_call(
        body, grid_spec=grid_spec,
        out_shape=jax.ShapeDtypeStruct((b, s, 256), BF16),
        compiler_params=_params(("parallel", "arbitrary")),
        name="diff_attn",
    )(slopes_l2, a3, a3, avt, mask_t, kb, lam_p, sub128)


def _mla_body(q_ref, k_ref, vt_ref, mask_ref, o_ref, s0_ref, s1_ref, acc_ref, *, tq):
    qi = pl.program_id(1)
    tk = tq
    acc_ref[...] = jnp.zeros(acc_ref.shape, F32)

    def scores(i, t, first):
        k = k_ref[pl.ds(pl.multiple_of((qi - t) * tk, tk), tk), (i // 2) * 256:(i // 2 + 1) * 256]
        s = lax.dot_general(k, q_ref[:, i * 256:(i + 1) * 256], _NT, preferred_element_type=F32)
        return s + mask_ref[...] if first else s

    def load_vt(t, i):
        return vt_ref[pl.ds(i * VT_ROWS, VT_ROWS), pl.ds(pl.multiple_of((qi - t) * tk, tk), tk)]

    m0 = tuple(jnp.full((1, tq), NEG_INF, F32) for _ in range(MLA_HEADS))
    _attn_pipeline(MLA_HEADS, qi, scores, lambda i, t: 0.0, load_vt, m0, (s0_ref, s1_ref), acc_ref)
    o_t = jnp.concatenate([_normalized(acc_ref, i) for i in range(MLA_HEADS)], axis=0)
    o_ref[...] = o_t.T.astype(o_ref.dtype)


def _mla_attention(cq3, ck3, cvt, mask_t, tq):
    b, s, _ = cq3.shape
    nq = s // tq
    return pl.pallas_call(
        functools.partial(_mla_body, tq=tq),
        grid=(b, nq),
        in_specs=[
            pl.BlockSpec((None, tq, MLA_HEADS * 256), lambda bi, qi: (bi, qi, 0)),
            pl.BlockSpec((None, s, 512), lambda bi, qi: (bi, 0, 0)),
            pl.BlockSpec((None, MLA_HEADS * VT_ROWS, s), lambda bi, qi: (bi, 0, 0)),
            pl.BlockSpec((tq, tq), lambda bi, qi: (0, 0)),
        ],
        out_specs=pl.BlockSpec((None, tq, 256), lambda bi, qi: (bi, qi, 0)),
        out_shape=jax.ShapeDtypeStruct((b, s, 256), BF16),
        scratch_shapes=[pltpu.VMEM((MLA_HEADS, tq, tq), F32), pltpu.VMEM((MLA_HEADS, tq, tq), F32),
                        pltpu.VMEM((MLA_HEADS, VT_ROWS, tq), F32)],
        compiler_params=_params(("parallel", "arbitrary")),
        name="mla_attn",
    )(cq3, ck3, cvt, mask_t)


def _band_body(m0_ref, q_ref, k_ref, vt_ref, bias_ref, o_ref, qs_ref, s0_ref, s1_ref, acc_ref,
               *, tq, n_off, l_init, kv_slabs):
    qi = pl.program_id(1)
    tk = tq
    n_soft = 4
    for sl in range(2):
        q = q_ref[:, sl * LANES:(sl + 1) * LANES].astype(F32)
        for g in range(2):
            qs_ref[2 * sl + g] = jnp.where(_lane_mask(q.shape, g * HEAD_LANES, (g + 1) * HEAD_LANES), q, 0.0
                                           ).astype(BF16)
    for i in range(n_soft):
        acc_ref[i] = jnp.where(_row_mask((VT_ROWS, tq), 0, HEAD_LANES), 0.0, l_init).astype(F32)

    kv_of = lambda i: (i // 2) * (kv_slabs - 1)

    def scores(i, t, first):
        k = k_ref[pl.ds(pl.multiple_of((qi - t) * tk, tk), tk), kv_of(i) * LANES:(kv_of(i) + 1) * LANES]
        return lax.dot_general(k, qs_ref[i], _NT, preferred_element_type=F32) + bias_ref[i // 2, i % 2, t]

    def load_vt(t, i):
        return vt_ref[pl.ds((2 * kv_of(i) + i % 2) * VT_ROWS, VT_ROWS),
                      pl.ds(pl.multiple_of((qi - t) * tk, tk), tk)]

    m0 = tuple(jnp.full((1, tq), m0_ref[i], F32) for i in range(n_soft))
    _attn_pipeline(n_soft, jnp.minimum(qi, n_off - 1), scores, lambda i, t: 0.0, load_vt, m0,
                   (s0_ref, s1_ref), acc_ref)
    o_t = jnp.concatenate([_normalized(acc_ref, i) for i in range(n_soft)], axis=0)
    o_ref[...] = o_t.T.astype(o_ref.dtype)


def _band_attention(x3, xvt, k_blk, kv_slabs, bias, m0, l_init, tq, name):
    b, s, _ = x3.shape
    nq = s // tq
    n_off = bias.shape[2]
    n_soft = 4
    grid_spec = pltpu.PrefetchScalarGridSpec(
        num_scalar_prefetch=1,
        grid=(b, nq),
        in_specs=[
            pl.BlockSpec((None, tq, 2 * LANES), lambda bi, qi, m0r: (bi, qi, 0)),
            pl.BlockSpec((None, s, kv_slabs * LANES), lambda bi, qi, m0r: (bi, 0, k_blk)),
            pl.BlockSpec((None, kv_slabs * 2 * VT_ROWS, s), lambda bi, qi, m0r: (bi, 0, 0)),
            pl.BlockSpec((2, 2, n_off, tq, tq), lambda bi, qi, m0r: (0, 0, 0, 0, 0),
                         pipeline_mode=pl.Buffered(1)),
        ],
        out_specs=pl.BlockSpec((None, tq, 2 * LANES), lambda bi, qi, m0r: (bi, qi, 0)),
        scratch_shapes=[pltpu.VMEM((n_soft, tq, LANES), BF16), pltpu.VMEM((n_soft, tq, tq), F32),
                        pltpu.VMEM((n_soft, tq, tq), F32), pltpu.VMEM((n_soft, VT_ROWS, tq), F32)],
    )
    return pl.pallas_call(
        functools.partial(_band_body, tq=tq, n_off=n_off, l_init=l_init, kv_slabs=kv_slabs),
        grid_spec=grid_spec,
        out_shape=jax.ShapeDtypeStruct((b, s, 256), BF16),
        compiler_params=_params(("parallel", "arbitrary")),
        name=name,
    )(m0, x3, x3, xvt, bias)


def _band_bias(slopes, tq, n_off, mult_fn):
    di = jnp.arange(tq, dtype=jnp.int32)[None, :]
    dj = jnp.arange(tq, dtype=jnp.int32)[:, None]
    off = jnp.arange(n_off, dtype=jnp.int32)[:, None, None]
    delta = off * tq + di - dj
    mult = mult_fn(delta)
    sl = jnp.asarray(slopes, F32)[:, None, None, None] * LOG2E
    val = -sl * delta.astype(F32)[None] + jnp.log2(jnp.maximum(mult, 1).astype(F32))[None]
    return jnp.where((mult > 0)[None], val, NEG_INF)


def _dil_mult(delta):
    m = jnp.zeros(delta.shape, jnp.int32)
    for window, d in DIL_PATTERNS:
        m = m + ((delta >= 0) & (delta <= window) & (delta % d == 0)).astype(jnp.int32)
    return m


def _swa_mult(delta):
    return ((delta >= 0) & (delta <= SWA_WINDOW - 1)).astype(jnp.int32)


def _outproj_body(*refs, with_router):
    if with_router:
        (oa, ob, oc, od, w_ref, x_ref, g_ref, rh_ref, rl_ref, x1_ref, h2_ref, lg_ref) = refs
    else:
        (oa, ob, oc, od, w_ref, x_ref, g_ref, x1_ref, h2_ref) = refs
    acc = x_ref[...]
    for i, o in enumerate((oa, ob, oc, od)):
        acc = acc + jnp.dot(o[...], w_ref[i], preferred_element_type=F32)
    x1_ref[...] = acc
    h2 = acc * lax.rsqrt(jnp.mean(acc * acc, axis=-1, keepdims=True) + RMS_EPS) * g_ref[...]
    hi = h2.astype(BF16)
    if with_router:
        for c in range(SC_CHUNKS):
            h2_ref[c] = h2[:, c * SC_COLS:(c + 1) * SC_COLS]
        lo = (h2 - hi.astype(F32)).astype(BF16)
        lg_ref[...] = (jnp.dot(hi, rh_ref[...], preferred_element_type=F32)
                       + (jnp.dot(hi, rl_ref[...], preferred_element_type=F32)
                          + jnp.dot(lo, rh_ref[...], preferred_element_type=F32)))
    else:
        h2_ref[...] = hi


def _outproj(oa, ob, oc, od, w4, x2d, g, router_hl=None):
    t = x2d.shape[0]
    tm = _pick(t, 512)
    with_router = router_hl is not None
    full = lambda shape: pl.BlockSpec(shape, lambda i: (0,) * len(shape))
    row = lambda w: pl.BlockSpec((tm, w), lambda i: (i, 0))
    in_specs = [row(256)] * 4 + [full((4, 256, D_MODEL)), row(D_MODEL), full((1, D_MODEL))]
    args = [oa, ob, oc, od, w4, x2d, g]
    out_specs = [row(D_MODEL), row(D_MODEL)]
    out_shape = [jax.ShapeDtypeStruct((t, D_MODEL), F32), jax.ShapeDtypeStruct((t, D_MODEL), BF16)]
    if with_router:
        out_specs[1] = pl.BlockSpec((SC_CHUNKS, tm, SC_COLS), lambda i: (0, i, 0))
        out_shape[1] = jax.ShapeDtypeStruct((SC_CHUNKS, t, SC_COLS), F32)
        in_specs += [full((D_MODEL, LANES))] * 2
        args += list(router_hl)
        out_specs.append(row(LANES))
        out_shape.append(jax.ShapeDtypeStruct((t, LANES), F32))
    return pl.pallas_call(
        functools.partial(_outproj_body, with_router=with_router),
        grid=(t // tm,), in_specs=in_specs, out_specs=out_specs, out_shape=out_shape,
        compiler_params=_params(("parallel",)),
        name="outproj",
    )(*args)


FF_CHUNK = 512


def _swiglu(x, w1_ref, w3_ref, w2_ref):
    tf = w1_ref.shape[1]
    total = None
    for lo in range(0, tf, FF_CHUNK):
        hi = min(lo + FF_CHUNK, tf)
        a = jnp.dot(x, w1_ref[:, lo:hi], preferred_element_type=F32)
        b = jnp.dot(x, w3_ref[:, lo:hi], preferred_element_type=F32)
        mid = (a * jax.nn.sigmoid(a) * b).astype(BF16)
        part = jnp.dot(mid, w2_ref[lo:hi, :], preferred_element_type=F32)
        total = part if total is None else total + part
    return total


def _ffn_body(h_ref, w1_ref, w3_ref, w2_ref, x_ref, o_ref):
    o_ref[...] = x_ref[...] + _swiglu(h_ref[...], w1_ref, w3_ref, w2_ref)


def _ffn(h2, w1, w3, w2, x1):
    t = h2.shape[0]
    tm = _pick(t, 512)
    resident = lambda shape: pl.BlockSpec(shape, lambda i: (0, 0), pipeline_mode=pl.Buffered(1))
    return pl.pallas_call(
        _ffn_body,
        grid=(t // tm,),
        in_specs=[pl.BlockSpec((tm, D_MODEL), lambda i: (i, 0)),
                  resident((D_MODEL, D_FF)), resident((D_MODEL, D_FF)), resident((D_FF, D_MODEL)),
                  pl.BlockSpec((tm, D_MODEL), lambda i: (i, 0))],
        out_specs=pl.BlockSpec((tm, D_MODEL), lambda i: (i, 0)),
        out_shape=jax.ShapeDtypeStruct((t, D_MODEL), F32),
        compiler_params=_params(("parallel",)),
        name="ffn",
    )(h2, w1, w3, w2, x1)


def _route_body(lg_ref, tri_ref, info_ref, cnt_ref, carry_ref):
    i = pl.program_id(0)

    @pl.when(i == 0)
    def _():
        carry_ref[...] = jnp.zeros(carry_ref.shape, F32)

    lg = lg_ref[...]
    lane = lax.broadcasted_iota(jnp.int32, lg.shape, 1)
    lg = jnp.where(lane < N_EXPERTS, lg, -jnp.inf)
    m1 = jnp.max(lg, axis=-1, keepdims=True)
    i1 = jnp.min(jnp.where(lg == m1, lane, LANES), axis=-1, keepdims=True)
    oh1 = lane == i1
    lg2 = jnp.where(oh1, -jnp.inf, lg)
    m2 = jnp.max(lg2, axis=-1, keepdims=True)
    i2 = jnp.min(jnp.where(lg2 == m2, lane, LANES), axis=-1, keepdims=True)
    oh2 = lane == i2
    e2 = jnp.exp(m2 - m1)
    g1 = 1.0 / (1.0 + e2)
    g2 = e2 / (1.0 + e2)
    oh = jnp.where(jnp.logical_or(oh1, oh2), 1.0, 0.0)
    rank = jnp.dot(tri_ref[...], oh.astype(BF16), preferred_element_type=F32) + carry_ref[...]
    r1 = jnp.sum(jnp.where(oh1, rank, 0.0), axis=-1, keepdims=True)
    r2 = jnp.sum(jnp.where(oh2, rank, 0.0), axis=-1, keepdims=True)
    carry_ref[...] += jnp.sum(oh, axis=0, keepdims=True)
    cnt_ref[...] = carry_ref[...]
    info = jnp.zeros(lg.shape, F32)
    for c, val in enumerate((i1.astype(F32), i2.astype(F32), r1, r2, g1, g2)):
        info = jnp.where(lane == c, val, info)
    info_ref[...] = info


def _route(logits):
    t = logits.shape[0]
    tm = _pick(t, 512)
    tri = (jnp.arange(tm)[:, None] > jnp.arange(tm)[None, :]).astype(BF16)
    return pl.pallas_call(
        _route_body,
        grid=(t // tm,),
        in_specs=[pl.BlockSpec((tm, LANES), lambda i: (i, 0)), pl.BlockSpec((tm, tm), lambda i: (0, 0))],
        out_specs=[pl.BlockSpec((tm, LANES), lambda i: (i, 0)), pl.BlockSpec((1, LANES), lambda i: (0, 0))],
        out_shape=[jax.ShapeDtypeStruct((t, LANES), F32), jax.ShapeDtypeStruct((1, LANES), F32)],
        scratch_shapes=[pltpu.VMEM((1, LANES), F32)],
        compiler_params=_params(("arbitrary",)),
        name="route",
    )(logits, tri)


def _moe_ffn_body(be_ref, bv_ref, x_ref, w1_ref, w3_ref, w2_ref, o_ref, acc_ref):
    i, f = pl.program_id(0), pl.program_id(1)
    last = pl.num_programs(1) - 1
    valid = bv_ref[i]

    @pl.when(valid > 0)
    def _():
        x = jnp.concatenate([x_ref[c] for c in range(SC_CHUNKS)], axis=1)
        x = jnp.where(lax.broadcasted_iota(jnp.int32, x.shape, 0) < valid, x, 0.0).astype(BF16)
        contrib = _swiglu(x, w1_ref, w3_ref, w2_ref)

        @pl.when(f == 0)
        def _():
            acc_ref[...] = contrib

        @pl.when(f > 0)
        def _():
            acc_ref[...] += contrib

        @pl.when(f == last)
        def _():
            for c in range(SC_CHUNKS):
                o_ref[c] = acc_ref[:, c * SC_COLS:(c + 1) * SC_COLS]

    @pl.when(jnp.logical_and(valid == 0, f == last))
    def _():
        o_ref[...] = jnp.zeros(o_ref.shape, o_ref.dtype)


def _moe_ffn(xs, blk_expert, blk_valid, w1, w3, w2, bm):
    rows = xs.shape[1]
    tf = D_FF_EXPERT // 2
    chunked = pl.BlockSpec((SC_CHUNKS, bm, SC_COLS), lambda i, f, be, bv: (0, i, 0))
    grid_spec = pltpu.PrefetchScalarGridSpec(
        num_scalar_prefetch=2,
        grid=(rows // bm, D_FF_EXPERT // tf),
        in_specs=[chunked,
                  pl.BlockSpec((None, D_MODEL, tf), lambda i, f, be, bv: (be[i], 0, f)),
                  pl.BlockSpec((None, D_MODEL, tf), lambda i, f, be, bv: (be[i], 0, f)),
                  pl.BlockSpec((None, tf, D_MODEL), lambda i, f, be, bv: (be[i], f, 0))],
        out_specs=chunked,
        scratch_shapes=[pltpu.VMEM((bm, D_MODEL), F32)],
    )
    return pl.pallas_call(
        _moe_ffn_body, grid_spec=grid_spec,
        out_shape=jax.ShapeDtypeStruct(xs.shape, F32),
        compiler_params=_params(("parallel", "arbitrary")),
        name="moe_ffn",
    )(blk_expert, blk_valid, xs, w1, w3, w2)


def _final_body(*refs, with_moe):
    if with_moe:
        x_ref, y0_ref, y1_ref, info_ref, g_ref, o_ref = refs
        info = info_ref[...]
        y0 = jnp.concatenate([y0_ref[c] for c in range(SC_CHUNKS)], axis=1)
        y1 = jnp.concatenate([y1_ref[c] for c in range(SC_CHUNKS)], axis=1)
        x = x_ref[...] + (info[:, 4:5] * y0 + info[:, 5:6] * y1)
    else:
        x_ref, g_ref, o_ref = refs
        x = x_ref[...]
    o_ref[...] = x * lax.rsqrt(jnp.mean(x * x, axis=-1, keepdims=True) + RMS_EPS) * g_ref[...]


def _final(x2d, g, moe=None):
    t = x2d.shape[0]
    tm = _pick(t, 512)
    row = lambda w: pl.BlockSpec((tm, w), lambda i: (i, 0))
    in_specs, args = [row(D_MODEL)], [x2d]
    if moe is not None:
        yg, info = moe
        slot = lambda k: pl.BlockSpec((None, SC_CHUNKS, tm, SC_COLS), lambda i: (k, 0, i, 0))
        in_specs += [slot(0), slot(1), row(LANES)]
        args += [yg, yg, info]
    in_specs.append(pl.BlockSpec((1, D_MODEL), lambda i: (0, 0)))
    args.append(g)
    return pl.pallas_call(
        functools.partial(_final_body, with_moe=moe is not None),
        grid=(t // tm,), in_specs=in_specs, out_specs=row(D_MODEL),
        out_shape=jax.ShapeDtypeStruct((t, D_MODEL), F32),
        compiler_params=_params(("parallel",)),
        name="final",
    )(*args)


def _prep_attn_weights(w_in, w_out, w_uq, w_ukv):
    widths = (256, 256, 256, 256, 256, 256, MLA_Q_LORA, MLA_KV_LORA, MLA_ROPE, 256, 128, 128)
    offs = np.concatenate([[0], np.cumsum(widths)])
    qa, ka, va, qb, kb, vb, cq, ckv, kpe, qd, kd, vd = [w_in[:, offs[i]:offs[i + 1]] for i in range(12)]
    hperm = np.array([0, 2, 1, 3])
    qd = qd.reshape(D_MODEL, 4, SWA_HD)[:, hperm].reshape(D_MODEL, 256)
    pad = jnp.zeros((D_MODEL, LANES - MLA_ROPE), w_in.dtype)
    half = MLA_ROPE // 2
    kpe_sw = jnp.concatenate([kpe[:, half:], kpe[:, :half]], axis=1)
    w_main = jnp.concatenate([qa, ka, va, qb, kb, vb, qd, kd, vd, cq, ckv, kpe, pad, kpe_sw, pad], axis=1)

    uq = w_uq.reshape(MLA_Q_LORA, MLA_HEADS, MLA_NOPE + MLA_ROPE)
    z = lambda n: jnp.zeros((MLA_Q_LORA, n), w_uq.dtype)
    mains, sws = [], []
    for hd in range(MLA_HEADS):
        nope, rope = uq[:, hd, :MLA_NOPE], uq[:, hd, MLA_NOPE:]
        nope128 = jnp.concatenate([nope, z(64)] if hd % 2 == 0 else [z(64), nope], axis=1)
        mains.append(jnp.concatenate([rope, z(LANES - MLA_ROPE), nope128], axis=1))
        sws.append(jnp.concatenate([rope[:, half:], rope[:, :half], z(LANES - MLA_ROPE)], axis=1))
    wq_big = jnp.concatenate(mains + sws, axis=1)

    ukv = w_ukv.reshape(MLA_KV_LORA, MLA_HEADS, MLA_NOPE + MLA_V)
    wkv = jnp.concatenate([ukv[:, :, :MLA_NOPE].reshape(MLA_KV_LORA, 256),
                           ukv[:, :, MLA_NOPE:].reshape(MLA_KV_LORA, 256)], axis=1)

    wo = w_out.reshape(4, 256, D_MODEL)
    wo_d = wo[3].reshape(4, SWA_HD, D_MODEL)[hperm].reshape(256, D_MODEL)
    w4 = jnp.stack([wo[0], wo[1], wo[2], wo_d])
    return w_main.astype(BF16), wq_big.astype(BF16), wkv.astype(BF16), w4.astype(BF16)


def _rope_tables(s):
    inv = ROPE_THETA ** (-jnp.arange(0, MLA_ROPE, 2, dtype=F32) / MLA_ROPE)
    ang = jnp.arange(s, dtype=F32)[:, None] * inv[None, :]
    cos, sin = jnp.cos(ang), jnp.sin(ang)
    z = jnp.zeros((s, LANES - MLA_ROPE), F32)
    cos128 = jnp.concatenate([cos, cos, z], axis=1)
    sin128 = jnp.concatenate([-sin, sin, z], axis=1)
    c_mla = (MLA_NOPE + MLA_ROPE) ** -0.5 * LOG2E
    return cos128 * c_mla, sin128 * c_mla, cos128, sin128


def _col_scale():
    ca = DIFF_HD ** -0.5 * LOG2E
    cb = DIL_HD ** -0.5 * LOG2E
    cd = SWA_HD ** -0.5 * LOG2E
    v = np.ones((1, N_SCALED), np.float32)
    v[0, 0:256] = ca
    v[0, 768:1024] = cb
    v[0, 1536:1792] = cd
    return jnp.asarray(v)


def _moe_layer(h2, logits, w1, w3, w2, bm):
    t = logits.shape[0]
    info, counts = _route(logits)
    cnt = counts[0, :N_EXPERTS].astype(jnp.int32)
    padded = (cnt + bm - 1) // bm * bm
    pends = jnp.cumsum(padded)
    pstarts = pends - padded
    e12 = info[:, 0:2].astype(jnp.int32)
    pos = pstarts[e12] + info[:, 2:4].astype(jnp.int32)
    nb = (t * TOP_K) // bm + N_EXPERTS
    rows = nb * bm
    blk_start = jnp.arange(nb, dtype=jnp.int32) * bm
    blk_expert = jnp.minimum(jnp.sum((pends[None, :] <= blk_start[:, None]).astype(jnp.int32), axis=1),
                             N_EXPERTS - 1)
    blk_valid = jnp.clip(pstarts[blk_expert] + cnt[blk_expert] - blk_start, 0, bm).astype(jnp.int32)
    sub = (jnp.arange(SC_CHUNKS, dtype=jnp.int32) * rows)[None, :, None] + pos.T[:, None, :]
    xs = _sc_scatter_rows(h2.reshape(SC_CHUNKS * t, SC_COLS),
                          [sub[k].reshape(1, SC_CHUNKS * t) for k in range(TOP_K)], SC_CHUNKS * rows)
    ys = _moe_ffn(xs.reshape(SC_CHUNKS, rows, SC_COLS), blk_expert.astype(jnp.int32), blk_valid, w1, w3, w2, bm)
    yg = _sc_gather_rows(ys.reshape(SC_CHUNKS * rows, SC_COLS), sub.reshape(1, TOP_K * SC_CHUNKS * t))
    return yg.reshape(TOP_K, SC_CHUNKS, t, SC_COLS), info


SC_WINDOW = 128
SC_COLS = 256
SC_CHUNKS = D_MODEL // SC_COLS


def _sc_mesh():
    return plsc.VectorSubcoreMesh(core_axis_name="c", subcore_axis_name="s")


def _sc_scatter_rows(x, idx_list, rows):
    t, d = x.shape
    n = len(idx_list)

    @functools.partial(pl.kernel, out_type=jax.ShapeDtypeStruct((rows, d), x.dtype), mesh=_sc_mesh(),
                       scratch_types=[])
    def scatter_kernel(x_hbm, *rest):
        idx_hbms, o_hbm = rest[:n], rest[n]

        def body(x_vmem, *idx_vmems):
            for iv in idx_vmems:
                pltpu.sync_copy(x_vmem, o_hbm.at[iv.at[0]])

        pltpu.emit_pipeline(
            body, grid=(t // SC_WINDOW,),
            in_specs=[pl.BlockSpec((SC_WINDOW, d), lambda i: (i, 0))]
            + [pl.BlockSpec((1, SC_WINDOW), lambda i: (0, i))] * n,
            out_specs=[], core_axis_name=("c", "s"), dimension_semantics=(pltpu.PARALLEL,),
        )(x_hbm, *idx_hbms)

    return scatter_kernel(x, *idx_list)


def _sc_gather_rows(y, idx):
    m = idx.shape[1]
    d = y.shape[1]

    @functools.partial(pl.kernel, out_type=jax.ShapeDtypeStruct((m, d), y.dtype), mesh=_sc_mesh(),
                       scratch_types=[])
    def gather_kernel(y_hbm, idx_hbm, o_hbm):
        def body(idx_vmem, o_vmem):
            pltpu.sync_copy(y_hbm.at[idx_vmem.at[0]], o_vmem)

        pltpu.emit_pipeline(
            body, grid=(m // SC_WINDOW,),
            in_specs=[pl.BlockSpec((1, SC_WINDOW), lambda i: (0, i))],
            out_specs=[pl.BlockSpec((SC_WINDOW, d), lambda i: (i, 0))],
            core_axis_name=("c", "s"), dimension_semantics=(pltpu.PARALLEL,),
        )(idx_hbm, o_hbm)

    return gather_kernel(y, idx)


def kernel(x, attn_norm, w_in, w_out, diff_lambda, diff_subln, mla_q_norm, mla_w_uq, mla_kv_norm, mla_w_ukv,
           swa_sinks, ffn_norm, ffn_w1, ffn_w3, ffn_w2, moe_router, moe_w1, moe_w3, moe_w2, final_norm):
    bsz, seq, _ = x.shape
    t = bsz * seq
    depth = w_in.shape[0]
    slopes_a, slopes_b, slopes_d = _alibi_slopes()
    tq_dense = _pick(seq, 512)
    tq_band = _pick(seq, 256)

    tabs = _rope_tables(seq)
    colscale = _col_scale()
    di = jnp.arange(tq_dense)
    causal_mask_t = jnp.where(di[:, None] <= di[None, :], 0.0, NEG_INF).astype(F32)
    tq_diff = tq_dense
    alibi_kb = _alibi_key_columns(slopes_a * LOG2E, tq_diff)
    tq_dil = _pick(seq, 512)
    n_off_dil = min((max(w for w, _ in DIL_PATTERNS) + tq_dil - 1) // tq_dil + 1, seq // tq_dil)
    n_off_swa = min((SWA_WINDOW - 1 + tq_band - 1) // tq_band + 1, seq // tq_band)
    dil_bias = _band_bias(slopes_b, tq_dil, n_off_dil, _dil_mult).reshape(2, 2, n_off_dil, tq_dil, tq_dil)
    swa_bias = _band_bias(slopes_d[np.array([0, 2, 1, 3])], tq_band, n_off_swa, _swa_mult
                          ).reshape(2, 2, n_off_swa, tq_band, tq_band)
    slopes_a_l2 = jnp.asarray(slopes_a * LOG2E, F32)
    dil_m0 = jnp.full((4,), NEG_INF, F32)

    x2d = x.reshape(t, D_MODEL)
    for l in range(depth):
        w_main, wq_big, wkv, w4 = _prep_attn_weights(w_in[l], w_out[l], mla_w_uq[l], mla_w_ukv[l])
        a, avt, b, bvt, d, dvt, cq, ck, cvt = _proj(x2d, attn_norm[l][None], w_main, colscale, mla_q_norm[l][None],
                                                    wq_big, mla_kv_norm[l][None], wkv, tabs, seq)
        a3, b3, d3 = (v.reshape(bsz, seq, -1) for v in (a, b, d))
        sub128 = jnp.concatenate([diff_subln[l], diff_subln[l]])[None]
        o_a = _diff_attention(a3, avt, slopes_a_l2, causal_mask_t[:tq_diff, :tq_diff], alibi_kb, diff_lambda[l],
                              sub128, l, tq_diff)
        o_b = _band_attention(b3, bvt, 1, 2, dil_bias, dil_m0, 0.0, tq_dil, "dil_attn")
        o_c = _mla_attention(cq.reshape(bsz, seq, -1), ck.reshape(bsz, seq, -1), cvt, causal_mask_t, tq_dense)
        swa_m0 = (swa_sinks[l].astype(F32) * LOG2E)[np.array([0, 2, 1, 3])]
        o_d = _band_attention(d3, dvt, 2, 1, swa_bias, swa_m0, 1.0, tq_band, "swa_attn")
        outs = [v.reshape(t, 256) for v in (o_a, o_b, o_c, o_d)]
        j = l // 2
        if l % 2 == 0:
            x1, h2 = _outproj(*outs, w4, x2d, ffn_norm[l][None])
            x2d = _ffn(h2, ffn_w1[j].astype(BF16), ffn_w3[j].astype(BF16), ffn_w2[j].astype(BF16), x1)
            moe = None
        else:
            r = jnp.pad(moe_router[j], ((0, 0), (0, LANES - N_EXPERTS)))
            r_hi, r_lo = _split_bf16(r, 2)
            x1, h2, logits = _outproj(*outs, w4, x2d, ffn_norm[l][None], (r_hi, r_lo))
            bm = _pick(t, 512)
            moe = _moe_layer(h2, logits, moe_w1[j].astype(BF16), moe_w3[j].astype(BF16),
                             moe_w2[j].astype(BF16), bm)
            x2d = x1
        if l < depth - 1 and moe is not None:
            raise NotImplementedError("MoE combine is fused into the final norm; MoE layer must be last")
    return _final(x2d, final_norm[None], moe).reshape(bsz, seq, D_MODEL)
```

```python
import functools
import math

import numpy as np
import jax
import jax.numpy as jnp
from jax import lax
from jax.experimental import pallas as pl
from jax.experimental.pallas import tpu as pltpu
from jax.experimental.pallas import tpu_sc as plsc

D_MODEL = 1024
DIFF_HEADS, DIFF_HD = 4, 32
DIL_HEADS, DIL_HD = 4, 64
DIL_PATTERNS = ((128, 1), (512, 4), (2048, 16))
MLA_HEADS, MLA_Q_LORA, MLA_KV_LORA, MLA_NOPE, MLA_ROPE, MLA_V = 4, 384, 128, 64, 32, 64
ROPE_THETA = 10000.0
SWA_HEADS, SWA_KV_HEADS, SWA_HD, SWA_WINDOW = 4, 2, 64, 128
D_FF = 2816
N_EXPERTS, TOP_K, D_FF_EXPERT = 8, 2, 3584
RMS_EPS = 1e-6
NEG_INF = -1e30
N_ALIBI = DIFF_HEADS + DIL_HEADS + SWA_HEADS

LOG2E = 1.4426950408889634
LANES = 128
HEAD_LANES = 64
VMEM_LIMIT = 56 * 1024 * 1024

F32 = jnp.float32
BF16 = jnp.bfloat16

_NT = (((1,), (1,)), ((), ()))


def _params(sem, vmem=VMEM_LIMIT):
    return pltpu.CompilerParams(dimension_semantics=sem, vmem_limit_bytes=vmem)


def _alibi_slopes():
    s = 2.0 ** (-8.0 * (np.arange(N_ALIBI) + 1) / N_ALIBI)
    return s[0::3], s[1::3], s[2::3]


def _pick(n, pref):
    t = min(pref, n)
    while n % t:
        t //= 2
    return t


N_SCALED = 2048
C_CQ, C_CKV, C_KPE, C_KPE_SW, N_MAIN = 2048, 2432, 2560, 2688, 2816


ONES_ROWS = 16
VT_ROWS = HEAD_LANES + ONES_ROWS


def _vt_with_ones(v):
    vt = v.T
    ones = jnp.ones((ONES_ROWS, v.shape[0]), F32)
    parts = []
    for hd in range(v.shape[1] // HEAD_LANES):
        parts += [vt[hd * HEAD_LANES:(hd + 1) * HEAD_LANES], ones]
    return jnp.concatenate(parts, axis=0).astype(BF16)


def _proj_body(x_ref, g_ref, w_ref, cs_ref, qn_ref, wq_ref, kn_ref, wkv_ref,
               cosq_ref, sinq_ref, cosk_ref, sink_ref,
               a_ref, avt_ref, b_ref, bvt_ref, d_ref, dvt_ref, cq_ref, ck_ref, cvt_ref):
    x = x_ref[...]
    ms = jnp.mean(x * x, axis=-1, keepdims=True)
    h = (x * lax.rsqrt(ms + RMS_EPS) * g_ref[...]).astype(BF16)
    acc = jnp.dot(h, w_ref[...], preferred_element_type=F32)
    sc = acc[:, :N_SCALED] * cs_ref[...]
    a_ref[...] = sc[:, 0:512].astype(BF16)
    avt_ref[...] = _vt_with_ones(sc[:, 512:768])
    b_ref[...] = sc[:, 768:1280].astype(BF16)
    bvt_ref[...] = _vt_with_ones(sc[:, 1280:1536])
    d_ref[...] = sc[:, 1536:1920].astype(BF16)
    dvt_ref[...] = _vt_with_ones(sc[:, 1920:2048])

    cq = acc[:, C_CQ:C_CKV]
    hq = (cq * lax.rsqrt(jnp.mean(cq * cq, axis=-1, keepdims=True) + RMS_EPS) * qn_ref[...]).astype(BF16)
    yq = jnp.dot(hq, wq_ref[...], preferred_element_type=F32)
    cosq, sinq = cosq_ref[...], sinq_ref[...]
    c_mla = (MLA_NOPE + MLA_ROPE) ** -0.5 * LOG2E
    for hd in range(MLA_HEADS):
        main = yq[:, hd * 256:(hd + 1) * 256]
        sw = yq[:, 1024 + hd * LANES:1024 + (hd + 1) * LANES]
        cq_ref[:, hd * 256:hd * 256 + LANES] = (main[:, :LANES] * cosq + sw * sinq).astype(BF16)
        cq_ref[:, hd * 256 + LANES:(hd + 1) * 256] = (main[:, LANES:] * c_mla).astype(BF16)

    ckv = acc[:, C_CKV:C_KPE]
    hk = (ckv * lax.rsqrt(jnp.mean(ckv * ckv, axis=-1, keepdims=True) + RMS_EPS) * kn_ref[...]).astype(BF16)
    kv = jnp.dot(hk, wkv_ref[...], preferred_element_type=F32)
    kr = (acc[:, C_KPE:C_KPE_SW] * cosk_ref[...] + acc[:, C_KPE_SW:N_MAIN] * sink_ref[...]).astype(BF16)
    for grp in range(2):
        ck_ref[:, grp * 256:grp * 256 + LANES] = kr
        ck_ref[:, grp * 256 + LANES:(grp + 1) * 256] = kv[:, grp * LANES:(grp + 1) * LANES].astype(BF16)
    cvt_ref[...] = _vt_with_ones(kv[:, 256:512])


def _proj(x2d, g, w_main, colscale, qn, wq_big, kn, wkv, tabs, seq):
    t = x2d.shape[0]
    tm = _pick(seq, 512)
    nsb = seq // tm
    full = lambda shape: pl.BlockSpec(shape, lambda i: (0,) * len(shape))
    tab = pl.BlockSpec((tm, LANES), lambda i: (i % nsb, 0))
    row = lambda w: pl.BlockSpec((tm, w), lambda i: (i, 0))
    vt = lambda w: pl.BlockSpec((None, w, tm), lambda i: (i // nsb, 0, i % nsb))
    row_sds = lambda w: jax.ShapeDtypeStruct((t, w), BF16)
    vt_sds = lambda w: jax.ShapeDtypeStruct((t // seq, w, seq), BF16)
    return pl.pallas_call(
        _proj_body,
        grid=(t // tm,),
        in_specs=[row(D_MODEL), full((1, D_MODEL)), full((D_MODEL, N_MAIN)), full((1, N_SCALED)),
                  full((1, MLA_Q_LORA)), full((MLA_Q_LORA, 1536)), full((1, MLA_KV_LORA)),
                  full((MLA_KV_LORA, 512)), tab, tab, tab, tab],
        out_specs=[row(512), vt(4 * VT_ROWS), row(512), vt(4 * VT_ROWS), row(384), vt(2 * VT_ROWS),
                   row(1024), row(512), vt(4 * VT_ROWS)],
        out_shape=[row_sds(512), vt_sds(4 * VT_ROWS), row_sds(512), vt_sds(4 * VT_ROWS), row_sds(384),
                   vt_sds(2 * VT_ROWS), row_sds(1024), row_sds(512), vt_sds(4 * VT_ROWS)],
        compiler_params=_params(("parallel",)),
        name="proj",
    )(x2d, g, w_main, colscale, qn, wq_big, kn, wkv, *tabs)


def _attn_pipeline(n_soft, n_steps, scores, shift, load_vt, m0, s_ref, acc_ref):
    def stage_a(t, ss, dst_ref, m_cur):
        m_next, alpha = [], []
        for i in range(n_soft):
            m_new = jnp.maximum(m_cur[i], jnp.max(ss[i], axis=0, keepdims=True) + shift(i, t))
            dst_ref[i] = ss[i]
            m_next.append(m_new)
            alpha.append(jnp.exp2(m_cur[i] - m_new))
        return tuple(m_next), tuple(alpha)

    def stage_b(t, src_ref, carry):
        m_cur, alpha = carry
        for i in range(n_soft):
            p = jnp.exp2(src_ref[i] - (m_cur[i] - shift(i, t)))
            acc_ref[i] = alpha[i] * acc_ref[i] + jnp.dot(load_vt(t, i), p.astype(BF16),
                                                         preferred_element_type=F32)

    def half(t, src_ref, dst_ref, carry):
        ss = [scores(i, t + 1, False) for i in range(n_soft)]
        stage_b(t, src_ref, carry)
        return stage_a(t + 1, ss, dst_ref, carry[0])

    def body(u, carry):
        carry = half(2 * u, s_ref[0], s_ref[1], carry)
        return half(2 * u + 1, s_ref[1], s_ref[0], carry)

    carry = stage_a(0, [scores(i, 0, True) for i in range(n_soft)], s_ref[0], m0)
    carry = lax.fori_loop(0, n_steps // 2, body, carry)

    @pl.when(n_steps % 2 == 1)
    def _():
        stage_b(n_steps, s_ref[1], half(n_steps - 1, s_ref[0], s_ref[1], carry))

    @pl.when(n_steps % 2 == 0)
    def _():
        stage_b(n_steps, s_ref[0], carry)


def _row_mask(shape, lo, hi):
    row = lax.broadcasted_iota(jnp.int32, shape, 0)
    return jnp.logical_and(row >= lo, row < hi)


def _lane_mask(shape, lo, hi):
    lane = lax.broadcasted_iota(jnp.int32, shape, len(shape) - 1)
    return jnp.logical_and(lane >= lo, lane < hi)


def _normalized(acc_ref, i):
    return acc_ref[i, 0:HEAD_LANES, :] / acc_ref[i, HEAD_LANES:HEAD_LANES + 1, :]


def _diff_body(sl_ref, q_ref, k_ref, vt_ref, mask_ref, kb_ref, lam_ref, sub_ref, o_ref,
               qs_ref, s0_ref, s1_ref, acc_ref, *, tq, lam_init):
    qi = pl.program_id(1)
    tk = tq
    n_soft = 2 * DIFF_HEADS
    head_of = lambda i: i // 2
    for pair in range(2):
        q = q_ref[:, pair * LANES:(pair + 1) * LANES].astype(F32)
        for hd in range(2):
            sel = jnp.where(_lane_mask(q.shape, N_BIAS_SPLIT * hd, N_BIAS_SPLIT * (hd + 1)), 1.0, 0.0).astype(BF16)
            for mp in range(2):
                i = 4 * pair + 2 * hd + mp
                lo = hd * HEAD_LANES + mp * DIFF_HD
                qs_ref[i, :, 0:LANES] = jnp.where(_lane_mask(q.shape, lo, lo + DIFF_HD), q, 0.0).astype(BF16)
                qs_ref[i, :, LANES:2 * LANES] = sel
    acc_ref[...] = jnp.zeros(acc_ref.shape, F32)

    def scores(i, t, first):
        off = pl.multiple_of((qi - t) * tk, tk)
        pair = i // 4
        k = jnp.concatenate([k_ref[pl.ds(off, tk), pair * LANES:(pair + 1) * LANES], kb_ref[pair]], axis=1)
        s = lax.dot_general(k, qs_ref[i], _NT, preferred_element_type=F32)
        return s + mask_ref[...] if first else s

    def shift(i, t):
        return sl_ref[head_of(i)] * (-(t * tk)).astype(F32) if not isinstance(t, int) else 0.0

    def load_vt(t, i):
        return vt_ref[pl.ds(head_of(i) * VT_ROWS, VT_ROWS), pl.ds(pl.multiple_of((qi - t) * tk, tk), tk)]

    m0 = tuple(jnp.full((1, tq), NEG_INF, F32) for _ in range(n_soft))
    _attn_pipeline(n_soft, qi, scores, shift, load_vt, m0, (s0_ref, s1_ref), acc_ref)

    lp = lam_ref[...]
    lam = (jnp.exp(jnp.sum(lp[0:1, :] * lp[1:2, :], axis=-1, keepdims=True))
           - jnp.exp(jnp.sum(lp[2:3, :] * lp[3:4, :], axis=-1, keepdims=True)) + lam_init)
    in_h0 = _lane_mask((tq, LANES), 0, HEAD_LANES)
    for pair in range(2):
        o_h = [_normalized(acc_ref, 4 * pair + 2 * hd) - lam * _normalized(acc_ref, 4 * pair + 2 * hd + 1)
               for hd in range(2)]
        o = jnp.concatenate(o_h, axis=0).T
        sq = o * o
        ms0 = jnp.sum(jnp.where(in_h0, sq, 0.0), axis=-1, keepdims=True)
        ms1 = jnp.sum(jnp.where(in_h0, 0.0, sq), axis=-1, keepdims=True)
        ms = jnp.where(in_h0, ms0, ms1) * (1.0 / HEAD_LANES)
        y = o * lax.rsqrt(ms + RMS_EPS) * sub_ref[...]
        o_ref[:, pair * LANES:(pair + 1) * LANES] = (y * (1.0 - lam_init)).astype(o_ref.dtype)


N_BIAS_SPLIT = 3


def _alibi_key_columns(slopes_l2, tk):
    dj = jnp.arange(tk, dtype=F32)[None, :, None]
    val = jnp.asarray(slopes_l2, F32).reshape(-1, 1, 2) * dj
    cols = jnp.stack(_split_bf16(val, N_BIAS_SPLIT), axis=-1).reshape(val.shape[0], tk, 2 * N_BIAS_SPLIT)
    return jnp.pad(cols, ((0, 0), (0, 0), (0, LANES - 2 * N_BIAS_SPLIT)))


def _split_bf16(x, n):
    pieces = []
    for _ in range(n):
        bits = lax.bitcast_convert_type(x, jnp.uint32) & jnp.uint32(0xFFFF0000)
        head = lax.bitcast_convert_type(bits, F32)
        pieces.append(head.astype(BF16))
        x = x - head
    return pieces


def _diff_attention(a3, avt, slopes_l2, mask_t, kb, lam_p, sub128, layer_idx, tq):
    b, s, _ = a3.shape
    nq = s // tq
    lam_init = 0.8 - 0.6 * math.exp(-0.3 * layer_idx)
    body = functools.partial(_diff_body, tq=tq, lam_init=lam_init)
    grid_spec = pltpu.PrefetchScalarGridSpec(
        num_scalar_prefetch=1,
        grid=(b, nq),
        in_specs=[
            pl.BlockSpec((None, tq, 2 * LANES), lambda bi, qi, sl: (bi, qi, 0)),
            pl.BlockSpec((None, s, 2 * LANES), lambda bi, qi, sl: (bi, 0, 1)),
            pl.BlockSpec((None, DIFF_HEADS * VT_ROWS, s), lambda bi, qi, sl: (bi, 0, 0)),
            pl.BlockSpec((tq, tq), lambda bi, qi, sl: (0, 0), pipeline_mode=pl.Buffered(1)),
            pl.BlockSpec((2, tq, LANES), lambda bi, qi, sl: (0, 0, 0), pipeline_mode=pl.Buffered(1)),
            pl.BlockSpec((4, DIFF_HD), lambda bi, qi, sl: (0, 0)),
            pl.BlockSpec((1, LANES), lambda bi, qi, sl: (0, 0)),
        ],
        out_specs=pl.BlockSpec((None, tq, 2 * LANES), lambda bi, qi, sl: (bi, qi, 0)),
        scratch_shapes=[pltpu.VMEM((2 * DIFF_HEADS, tq, 2 * LANES), BF16),
                        pltpu.VMEM((2 * DIFF_HEADS, tq, tq), F32), pltpu.VMEM((2 * DIFF_HEADS, tq, tq), F32),
                        pltpu.VMEM((2 * DIFF_HEADS, VT_ROWS, tq), F32)],
    )
    return pl.pallas_call(
        body, grid_spec=grid_spec,
        out_shape=jax.ShapeDtypeStruct((b, s, 256), BF16),
        compiler_params=_params(("parallel", "arbitrary")),
        name="diff_attn",
    )(slopes_l2, a3, a3, avt, mask_t, kb, lam_p, sub128)


def _mla_body(q_ref, k_ref, vt_ref, mask_ref, o_ref, s0_ref, s1_ref, acc_ref, *, tq):
    qi = pl.program_id(1)
    tk = tq
    acc_ref[...] = jnp.zeros(acc_ref.shape, F32)

    def scores(i, t, first):
        k = k_ref[pl.ds(pl.multiple_of((qi - t) * tk, tk), tk), (i // 2) * 256:(i // 2 + 1) * 256]
        s = lax.dot_general(k, q_ref[:, i * 256:(i + 1) * 256], _NT, preferred_element_type=F32)
        return s + mask_ref[...] if first else s

    def load_vt(t, i):
        return vt_ref[pl.ds(i * VT_ROWS, VT_ROWS), pl.ds(pl.multiple_of((qi - t) * tk, tk), tk)]

    m0 = tuple(jnp.full((1, tq), NEG_INF, F32) for _ in range(MLA_HEADS))
    _attn_pipeline(MLA_HEADS, qi, scores, lambda i, t: 0.0, load_vt, m0, (s0_ref, s1_ref), acc_ref)
    o_t = jnp.concatenate([_normalized(acc_ref, i) for i in range(MLA_HEADS)], axis=0)
    o_ref[...] = o_t.T.astype(o_ref.dtype)


def _mla_attention(cq3, ck3, cvt, mask_t, tq):
    b, s, _ = cq3.shape
    nq = s // tq
    return pl.pallas_call(
        functools.partial(_mla_body, tq=tq),
        grid=(b, nq),
        in_specs=[
            pl.BlockSpec((None, tq, MLA_HEADS * 256), lambda bi, qi: (bi, qi, 0)),
            pl.BlockSpec((None, s, 512), lambda bi, qi: (bi, 0, 0)),
            pl.BlockSpec((None, MLA_HEADS * VT_ROWS, s), lambda bi, qi: (bi, 0, 0)),
            pl.BlockSpec((tq, tq), lambda bi, qi: (0, 0)),
        ],
        out_specs=pl.BlockSpec((None, tq, 256), lambda bi, qi: (bi, qi, 0)),
        out_shape=jax.ShapeDtypeStruct((b, s, 256), BF16),
        scratch_shapes=[pltpu.VMEM((MLA_HEADS, tq, tq), F32), pltpu.VMEM((MLA_HEADS, tq, tq), F32),
                        pltpu.VMEM((MLA_HEADS, VT_ROWS, tq), F32)],
        compiler_params=_params(("parallel", "arbitrary")),
        name="mla_attn",
    )(cq3, ck3, cvt, mask_t)


def _band_body(m0_ref, q_ref, k_ref, vt_ref, bias_ref, o_ref, qs_ref, s0_ref, s1_ref, acc_ref,
               *, tq, n_off, l_init, kv_slabs):
    qi = pl.program_id(1)
    tk = tq
    n_soft = 4
    for sl in range(2):
        q = q_ref[:, sl * LANES:(sl + 1) * LANES].astype(F32)
        for g in range(2):
            qs_ref[2 * sl + g] = jnp.where(_lane_mask(q.shape, g * HEAD_LANES, (g + 1) * HEAD_LANES), q, 0.0
                                           ).astype(BF16)
    for i in range(n_soft):
        acc_ref[i] = jnp.where(_row_mask((VT_ROWS, tq), 0, HEAD_LANES), 0.0, l_init).astype(F32)

    kv_of = lambda i: (i // 2) * (kv_slabs - 1)

    def scores(i, t, first):
        k = k_ref[pl.ds(pl.multiple_of((qi - t) * tk, tk), tk), kv_of(i) * LANES:(kv_of(i) + 1) * LANES]
        return lax.dot_general(k, qs_ref[i], _NT, preferred_element_type=F32) + bias_ref[i // 2, i % 2, t]

    def load_vt(t, i):
        return vt_ref[pl.ds((2 * kv_of(i) + i % 2) * VT_ROWS, VT_ROWS),
                      pl.ds(pl.multiple_of((qi - t) * tk, tk), tk)]

    m0 = tuple(jnp.full((1, tq), m0_ref[i], F32) for i in range(n_soft))
    _attn_pipeline(n_soft, jnp.minimum(qi, n_off - 1), scores, lambda i, t: 0.0, load_vt, m0,
                   (s0_ref, s1_ref), acc_ref)
    o_t = jnp.concatenate([_normalized(acc_ref, i) for i in range(n_soft)], axis=0)
    o_ref[...] = o_t.T.astype(o_ref.dtype)


def _band_attention(x3, xvt, k_blk, kv_slabs, bias, m0, l_init, tq, name):
    b, s, _ = x3.shape
    nq = s // tq
    n_off = bias.shape[2]
    n_soft = 4
    grid_spec = pltpu.PrefetchScalarGridSpec(
        num_scalar_prefetch=1,
        grid=(b, nq),
        in_specs=[
            pl.BlockSpec((None, tq, 2 * LANES), lambda bi, qi, m0r: (bi, qi, 0)),
            pl.BlockSpec((None, s, kv_slabs * LANES), lambda bi, qi, m0r: (bi, 0, k_blk)),
            pl.BlockSpec((None, kv_slabs * 2 * VT_ROWS, s), lambda bi, qi, m0r: (bi, 0, 0)),
            pl.BlockSpec((2, 2, n_off, tq, tq), lambda bi, qi, m0r: (0, 0, 0, 0, 0),
                         pipeline_mode=pl.Buffered(1)),
        ],
        out_specs=pl.BlockSpec((None, tq, 2 * LANES), lambda bi, qi, m0r: (bi, qi, 0)),
        scratch_shapes=[pltpu.VMEM((n_soft, tq, LANES), BF16), pltpu.VMEM((n_soft, tq, tq), F32),
                        pltpu.VMEM((n_soft, tq, tq), F32), pltpu.VMEM((n_soft, VT_ROWS, tq), F32)],
    )
    return pl.pallas_call(
        functools.partial(_band_body, tq=tq, n_off=n_off, l_init=l_init, kv_slabs=kv_slabs),
        grid_spec=grid_spec,
        out_shape=jax.ShapeDtypeStruct((b, s, 256), BF16),
        compiler_params=_params(("parallel", "arbitrary")),
        name=name,
    )(m0, x3, x3, xvt, bias)


def _band_bias(slopes, tq, n_off, mult_fn):
    di = jnp.arange(tq, dtype=jnp.int32)[None, :]
    dj = jnp.arange(tq, dtype=jnp.int32)[:, None]
    off = jnp.arange(n_off, dtype=jnp.int32)[:, None, None]
    delta = off * tq + di - dj
    mult = mult_fn(delta)
    sl = jnp.asarray(slopes, F32)[:, None, None, None] * LOG2E
    val = -sl * delta.astype(F32)[None] + jnp.log2(jnp.maximum(mult, 1).astype(F32))[None]
    return jnp.where((mult > 0)[None], val, NEG_INF)


def _dil_mult(delta):
    m = jnp.zeros(delta.shape, jnp.int32)
    for window, d in DIL_PATTERNS:
        m = m + ((delta >= 0) & (delta <= window) & (delta % d == 0)).astype(jnp.int32)
    return m


def _swa_mult(delta):
    return ((delta >= 0) & (delta <= SWA_WINDOW - 1)).astype(jnp.int32)


def _outproj_body(*refs, with_router):
    if with_router:
        (oa, ob, oc, od, w_ref, x_ref, g_ref, rh_ref, rl_ref, x1_ref, h2_ref, lg_ref) = refs
    else:
        (oa, ob, oc, od, w_ref, x_ref, g_ref, x1_ref, h2_ref) = refs
    acc = x_ref[...]
    for i, o in enumerate((oa, ob, oc, od)):
        acc = acc + jnp.dot(o[...], w_ref[i], preferred_element_type=F32)
    x1_ref[...] = acc
    h2 = acc * lax.rsqrt(jnp.mean(acc * acc, axis=-1, keepdims=True) + RMS_EPS) * g_ref[...]
    hi = h2.astype(BF16)
    if with_router:
        for c in range(SC_CHUNKS):
            h2_ref[c] = h2[:, c * SC_COLS:(c + 1) * SC_COLS]
        lo = (h2 - hi.astype(F32)).astype(BF16)
        lg_ref[...] = (jnp.dot(hi, rh_ref[...], preferred_element_type=F32)
                       + (jnp.dot(hi, rl_ref[...], preferred_element_type=F32)
                          + jnp.dot(lo, rh_ref[...], preferred_element_type=F32)))
    else:
        h2_ref[...] = hi


def _outproj(oa, ob, oc, od, w4, x2d, g, router_hl=None):
    t = x2d.shape[0]
    tm = _pick(t, 512)
    with_router = router_hl is not None
    full = lambda shape: pl.BlockSpec(shape, lambda i: (0,) * len(shape))
    row = lambda w: pl.BlockSpec((tm, w), lambda i: (i, 0))
    in_specs = [row(256)] * 4 + [full((4, 256, D_MODEL)), row(D_MODEL), full((1, D_MODEL))]
    args = [oa, ob, oc, od, w4, x2d, g]
    out_specs = [row(D_MODEL), row(D_MODEL)]
    out_shape = [jax.ShapeDtypeStruct((t, D_MODEL), F32), jax.ShapeDtypeStruct((t, D_MODEL), BF16)]
    if with_router:
        out_specs[1] = pl.BlockSpec((SC_CHUNKS, tm, SC_COLS), lambda i: (0, i, 0))
        out_shape[1] = jax.ShapeDtypeStruct((SC_CHUNKS, t, SC_COLS), F32)
        in_specs += [full((D_MODEL, LANES))] * 2
        args += list(router_hl)
        out_specs.append(row(LANES))
        out_shape.append(jax.ShapeDtypeStruct((t, LANES), F32))
    return pl.pallas_call(
        functools.partial(_outproj_body, with_router=with_router),
        grid=(t // tm,), in_specs=in_specs, out_specs=out_specs, out_shape=out_shape,
        compiler_params=_params(("parallel",)),
        name="outproj",
    )(*args)


FF_CHUNK = 512


def _swiglu(x, w1_ref, w3_ref, w2_ref):
    tf = w1_ref.shape[1]
    total = None
    for lo in range(0, tf, FF_CHUNK):
        hi = min(lo + FF_CHUNK, tf)
        a = jnp.dot(x, w1_ref[:, lo:hi], preferred_element_type=F32)
        b = jnp.dot(x, w3_ref[:, lo:hi], preferred_element_type=F32)
        mid = (a * jax.nn.sigmoid(a) * b).astype(BF16)
        part = jnp.dot(mid, w2_ref[lo:hi, :], preferred_element_type=F32)
        total = part if total is None else total + part
    return total


def _ffn_body(h_ref, w1_ref, w3_ref, w2_ref, x_ref, o_ref):
    o_ref[...] = x_ref[...] + _swiglu(h_ref[...], w1_ref, w3_ref, w2_ref)


def _ffn(h2, w1, w3, w2, x1):
    t = h2.shape[0]
    tm = _pick(t, 512)
    resident = lambda shape: pl.BlockSpec(shape, lambda i: (0, 0), pipeline_mode=pl.Buffered(1))
    return pl.pallas_call(
        _ffn_body,
        grid=(t // tm,),
        in_specs=[pl.BlockSpec((tm, D_MODEL), lambda i: (i, 0)),
                  resident((D_MODEL, D_FF)), resident((D_MODEL, D_FF)), resident((D_FF, D_MODEL)),
                  pl.BlockSpec((tm, D_MODEL), lambda i: (i, 0))],
        out_specs=pl.BlockSpec((tm, D_MODEL), lambda i: (i, 0)),
        out_shape=jax.ShapeDtypeStruct((t, D_MODEL), F32),
        compiler_params=_params(("parallel",)),
        name="ffn",
    )(h2, w1, w3, w2, x1)


def _route_body(lg_ref, tri_ref, info_ref, cnt_ref, carry_ref):
    i = pl.program_id(0)

    @pl.when(i == 0)
    def _():
        carry_ref[...] = jnp.zeros(carry_ref.shape, F32)

    lg = lg_ref[...]
    lane = lax.broadcasted_iota(jnp.int32, lg.shape, 1)
    lg = jnp.where(lane < N_EXPERTS, lg, -jnp.inf)
    m1 = jnp.max(lg, axis=-1, keepdims=True)
    i1 = jnp.min(jnp.where(lg == m1, lane, LANES), axis=-1, keepdims=True)
    oh1 = lane == i1
    lg2 = jnp.where(oh1, -jnp.inf, lg)
    m2 = jnp.max(lg2, axis=-1, keepdims=True)
    i2 = jnp.min(jnp.where(lg2 == m2, lane, LANES), axis=-1, keepdims=True)
    oh2 = lane == i2
    e2 = jnp.exp(m2 - m1)
    g1 = 1.0 / (1.0 + e2)
    g2 = e2 / (1.0 + e2)
    oh = jnp.where(jnp.logical_or(oh1, oh2), 1.0, 0.0)
    rank = jnp.dot(tri_ref[...], oh.astype(BF16), preferred_element_type=F32) + carry_ref[...]
    r1 = jnp.sum(jnp.where(oh1, rank, 0.0), axis=-1, keepdims=True)
    r2 = jnp.sum(jnp.where(oh2, rank, 0.0), axis=-1, keepdims=True)
    carry_ref[...] += jnp.sum(oh, axis=0, keepdims=True)
    cnt_ref[...] = carry_ref[...]
    info = jnp.zeros(lg.shape, F32)
    for c, val in enumerate((i1.astype(F32), i2.astype(F32), r1, r2, g1, g2)):
        info = jnp.where(lane == c, val, info)
    info_ref[...] = info


def _route(logits):
    t = logits.shape[0]
    tm = _pick(t, 512)
    tri = (jnp.arange(tm)[:, None] > jnp.arange(tm)[None, :]).astype(BF16)
    return pl.pallas_call(
        _route_body,
        grid=(t // tm,),
        in_specs=[pl.BlockSpec((tm, LANES), lambda i: (i, 0)), pl.BlockSpec((tm, tm), lambda i: (0, 0))],
        out_specs=[pl.BlockSpec((tm, LANES), lambda i: (i, 0)), pl.BlockSpec((1, LANES), lambda i: (0, 0))],
        out_shape=[jax.ShapeDtypeStruct((t, LANES), F32), jax.ShapeDtypeStruct((1, LANES), F32)],
        scratch_shapes=[pltpu.VMEM((1, LANES), F32)],
        compiler_params=_params(("arbitrary",)),
        name="route",
    )(logits, tri)


def _moe_ffn_body(be_ref, bv_ref, x_ref, w1_ref, w3_ref, w2_ref, o_ref):
    valid = bv_ref[pl.program_id(0)]

    @pl.when(valid > 0)
    def _():
        x = jnp.concatenate([x_ref[c] for c in range(SC_CHUNKS)], axis=1)
        x = jnp.where(lax.broadcasted_iota(jnp.int32, x.shape, 0) < valid, x, 0.0).astype(BF16)
        y = _swiglu(x, w1_ref, w3_ref, w2_ref)
        for c in range(SC_CHUNKS):
            o_ref[c] = y[:, c * SC_COLS:(c + 1) * SC_COLS]

    @pl.when(valid == 0)
    def _():
        o_ref[...] = jnp.zeros(o_ref.shape, o_ref.dtype)


def _moe_ffn(xs, blk_expert, blk_valid, w1, w3, w2, bm):
    rows = xs.shape[1]
    chunked = pl.BlockSpec((SC_CHUNKS, bm, SC_COLS), lambda i, be, bv: (0, i, 0))
    expert = lambda shape: pl.BlockSpec((None,) + shape, lambda i, be, bv: (be[i], 0, 0),
                                        pipeline_mode=pl.Buffered(1))
    grid_spec = pltpu.PrefetchScalarGridSpec(
        num_scalar_prefetch=2,
        grid=(rows // bm,),
        in_specs=[chunked, expert((D_MODEL, D_FF_EXPERT)), expert((D_MODEL, D_FF_EXPERT)),
                  expert((D_FF_EXPERT, D_MODEL))],
        out_specs=chunked,
    )
    return pl.pallas_call(
        _moe_ffn_body, grid_spec=grid_spec,
        out_shape=jax.ShapeDtypeStruct(xs.shape, F32),
        compiler_params=_params(("arbitrary",)),
        name="moe_ffn",
    )(blk_expert, blk_valid, xs, w1, w3, w2)


def _final_body(*refs, with_moe):
    if with_moe:
        x_ref, y0_ref, y1_ref, info_ref, g_ref, o_ref = refs
        info = info_ref[...]
        y0 = jnp.concatenate([y0_ref[c] for c in range(SC_CHUNKS)], axis=1)
        y1 = jnp.concatenate([y1_ref[c] for c in range(SC_CHUNKS)], axis=1)
        x = x_ref[...] + (info[:, 4:5] * y0 + info[:, 5:6] * y1)
    else:
        x_ref, g_ref, o_ref = refs
        x = x_ref[...]
    o_ref[...] = x * lax.rsqrt(jnp.mean(x * x, axis=-1, keepdims=True) + RMS_EPS) * g_ref[...]


def _final(x2d, g, moe=None):
    t = x2d.shape[0]
    tm = _pick(t, 512)
    row = lambda w: pl.BlockSpec((tm, w), lambda i: (i, 0))
    in_specs, args = [row(D_MODEL)], [x2d]
    if moe is not None:
        yg, info = moe
        slot = lambda k: pl.BlockSpec((None, SC_CHUNKS, tm, SC_COLS), lambda i: (k, 0, i, 0))
        in_specs += [slot(0), slot(1), row(LANES)]
        args += [yg, yg, info]
    in_specs.append(pl.BlockSpec((1, D_MODEL), lambda i: (0, 0)))
    args.append(g)
    return pl.pallas_call(
        functools.partial(_final_body, with_moe=moe is not None),
        grid=(t // tm,), in_specs=in_specs, out_specs=row(D_MODEL),
        out_shape=jax.ShapeDtypeStruct((t, D_MODEL), F32),
        compiler_params=_params(("parallel",)),
        name="final",
    )(*args)


def _prep_attn_weights(w_in, w_out, w_uq, w_ukv):
    widths = (256, 256, 256, 256, 256, 256, MLA_Q_LORA, MLA_KV_LORA, MLA_ROPE, 256, 128, 128)
    offs = np.concatenate([[0], np.cumsum(widths)])
    qa, ka, va, qb, kb, vb, cq, ckv, kpe, qd, kd, vd = [w_in[:, offs[i]:offs[i + 1]] for i in range(12)]
    hperm = np.array([0, 2, 1, 3])
    qd = qd.reshape(D_MODEL, 4, SWA_HD)[:, hperm].reshape(D_MODEL, 256)
    pad = jnp.zeros((D_MODEL, LANES - MLA_ROPE), w_in.dtype)
    half = MLA_ROPE // 2
    kpe_sw = jnp.concatenate([kpe[:, half:], kpe[:, :half]], axis=1)
    w_main = jnp.concatenate([qa, ka, va, qb, kb, vb, qd, kd, vd, cq, ckv, kpe, pad, kpe_sw, pad], axis=1)

    uq = w_uq.reshape(MLA_Q_LORA, MLA_HEADS, MLA_NOPE + MLA_ROPE)
    z = lambda n: jnp.zeros((MLA_Q_LORA, n), w_uq.dtype)
    mains, sws = [], []
    for hd in range(MLA_HEADS):
        nope, rope = uq[:, hd, :MLA_NOPE], uq[:, hd, MLA_NOPE:]
        nope128 = jnp.concatenate([nope, z(64)] if hd % 2 == 0 else [z(64), nope], axis=1)
        mains.append(jnp.concatenate([rope, z(LANES - MLA_ROPE), nope128], axis=1))
        sws.append(jnp.concatenate([rope[:, half:], rope[:, :half], z(LANES - MLA_ROPE)], axis=1))
    wq_big = jnp.concatenate(mains + sws, axis=1)

    ukv = w_ukv.reshape(MLA_KV_LORA, MLA_HEADS, MLA_NOPE + MLA_V)
    wkv = jnp.concatenate([ukv[:, :, :MLA_NOPE].reshape(MLA_KV_LORA, 256),
                           ukv[:, :, MLA_NOPE:].reshape(MLA_KV_LORA, 256)], axis=1)

    wo = w_out.reshape(4, 256, D_MODEL)
    wo_d = wo[3].reshape(4, SWA_HD, D_MODEL)[hperm].reshape(256, D_MODEL)
    w4 = jnp.stack([wo[0], wo[1], wo[2], wo_d])
    return w_main.astype(BF16), wq_big.astype(BF16), wkv.astype(BF16), w4.astype(BF16)


def _rope_tables(s):
    inv = ROPE_THETA ** (-jnp.arange(0, MLA_ROPE, 2, dtype=F32) / MLA_ROPE)
    ang = jnp.arange(s, dtype=F32)[:, None] * inv[None, :]
    cos, sin = jnp.cos(ang), jnp.sin(ang)
    z = jnp.zeros((s, LANES - MLA_ROPE), F32)
    cos128 = jnp.concatenate([cos, cos, z], axis=1)
    sin128 = jnp.concatenate([-sin, sin, z], axis=1)
    c_mla = (MLA_NOPE + MLA_ROPE) ** -0.5 * LOG2E
    return cos128 * c_mla, sin128 * c_mla, cos128, sin128


def _col_scale():
    ca = DIFF_HD ** -0.5 * LOG2E
    cb = DIL_HD ** -0.5 * LOG2E
    cd = SWA_HD ** -0.5 * LOG2E
    v = np.ones((1, N_SCALED), np.float32)
    v[0, 0:256] = ca
    v[0, 768:1024] = cb
    v[0, 1536:1792] = cd
    return jnp.asarray(v)


def _moe_layer(h2, logits, w1, w3, w2, bm):
    t = logits.shape[0]
    info, counts = _route(logits)
    cnt = counts[0, :N_EXPERTS].astype(jnp.int32)
    padded = (cnt + bm - 1) // bm * bm
    pends = jnp.cumsum(padded)
    pstarts = pends - padded
    e12 = info[:, 0:2].astype(jnp.int32)
    pos = pstarts[e12] + info[:, 2:4].astype(jnp.int32)
    nb = (t * TOP_K) // bm + N_EXPERTS
    rows = nb * bm
    blk_start = jnp.arange(nb, dtype=jnp.int32) * bm
    blk_expert = jnp.minimum(jnp.sum((pends[None, :] <= blk_start[:, None]).astype(jnp.int32), axis=1),
                             N_EXPERTS - 1)
    blk_valid = jnp.clip(pstarts[blk_expert] + cnt[blk_expert] - blk_start, 0, bm).astype(jnp.int32)
    sub = (jnp.arange(SC_CHUNKS, dtype=jnp.int32) * rows)[None, :, None] + pos.T[:, None, :]
    xs = _sc_scatter_rows(h2.reshape(SC_CHUNKS * t, SC_COLS),
                          [sub[k].reshape(1, SC_CHUNKS * t) for k in range(TOP_K)], SC_CHUNKS * rows)
    ys = _moe_ffn(xs.reshape(SC_CHUNKS, rows, SC_COLS), blk_expert.astype(jnp.int32), blk_valid, w1, w3, w2, bm)
    yg = _sc_gather_rows(ys.reshape(SC_CHUNKS * rows, SC_COLS), sub.reshape(1, TOP_K * SC_CHUNKS * t))
    return yg.reshape(TOP_K, SC_CHUNKS, t, SC_COLS), info


SC_WINDOW = 128
SC_COLS = 256
SC_CHUNKS = D_MODEL // SC_COLS


def _sc_mesh():
    return plsc.VectorSubcoreMesh(core_axis_name="c", subcore_axis_name="s")


def _sc_scatter_rows(x, idx_list, rows):
    t, d = x.shape
    n = len(idx_list)

    @functools.partial(pl.kernel, out_type=jax.ShapeDtypeStruct((rows, d), x.dtype), mesh=_sc_mesh(),
                       scratch_types=[])
    def scatter_kernel(x_hbm, *rest):
        idx_hbms, o_hbm = rest[:n], rest[n]

        def body(x_vmem, *idx_vmems):
            for iv in idx_vmems:
                pltpu.sync_copy(x_vmem, o_hbm.at[iv.at[0]])

        pltpu.emit_pipeline(
            body, grid=(t // SC_WINDOW,),
            in_specs=[pl.BlockSpec((SC_WINDOW, d), lambda i: (i, 0))]
            + [pl.BlockSpec((1, SC_WINDOW), lambda i: (0, i))] * n,
            out_specs=[], core_axis_name=("c", "s"), dimension_semantics=(pltpu.PARALLEL,),
        )(x_hbm, *idx_hbms)

    return scatter_kernel(x, *idx_list)


def _sc_gather_rows(y, idx):
    m = idx.shape[1]
    d = y.shape[1]

    @functools.partial(pl.kernel, out_type=jax.ShapeDtypeStruct((m, d), y.dtype), mesh=_sc_mesh(),
                       scratch_types=[])
    def gather_kernel(y_hbm, idx_hbm, o_hbm):
        def body(idx_vmem, o_vmem):
            pltpu.sync_copy(y_hbm.at[idx_vmem.at[0]], o_vmem)

        pltpu.emit_pipeline(
            body, grid=(m // SC_WINDOW,),
            in_specs=[pl.BlockSpec((1, SC_WINDOW), lambda i: (0, i))],
            out_specs=[pl.BlockSpec((SC_WINDOW, d), lambda i: (i, 0))],
            core_axis_name=("c", "s"), dimension_semantics=(pltpu.PARALLEL,),
        )(idx_hbm, o_hbm)

    return gather_kernel(y, idx)


def kernel(x, attn_norm, w_in, w_out, diff_lambda, diff_subln, mla_q_norm, mla_w_uq, mla_kv_norm, mla_w_ukv,
           swa_sinks, ffn_norm, ffn_w1, ffn_w3, ffn_w2, moe_router, moe_w1, moe_w3, moe_w2, final_norm):
    bsz, seq, _ = x.shape
    t = bsz * seq
    depth = w_in.shape[0]
    slopes_a, slopes_b, slopes_d = _alibi_slopes()
    tq_dense = _pick(seq, 512)
    tq_band = _pick(seq, 256)

    tabs = _rope_tables(seq)
    colscale = _col_scale()
    di = jnp.arange(tq_dense)
    causal_mask_t = jnp.where(di[:, None] <= di[None, :], 0.0, NEG_INF).astype(F32)
    tq_diff = tq_dense
    alibi_kb = _alibi_key_columns(slopes_a * LOG2E, tq_diff)
    tq_dil = _pick(seq, 512)
    n_off_dil = min((max(w for w, _ in DIL_PATTERNS) + tq_dil - 1) // tq_dil + 1, seq // tq_dil)
    n_off_swa = min((SWA_WINDOW - 1 + tq_band - 1) // tq_band + 1, seq // tq_band)
    dil_bias = _band_bias(slopes_b, tq_dil, n_off_dil, _dil_mult).reshape(2, 2, n_off_dil, tq_dil, tq_dil)
    swa_bias = _band_bias(slopes_d[np.array([0, 2, 1, 3])], tq_band, n_off_swa, _swa_mult
                          ).reshape(2, 2, n_off_swa, tq_band, tq_band)
    slopes_a_l2 = jnp.asarray(slopes_a * LOG2E, F32)
    dil_m0 = jnp.full((4,), NEG_INF, F32)

    x2d = x.reshape(t, D_MODEL)
    for l in range(depth):
        w_main, wq_big, wkv, w4 = _prep_attn_weights(w_in[l], w_out[l], mla_w_uq[l], mla_w_ukv[l])
        a, avt, b, bvt, d, dvt, cq, ck, cvt = _proj(x2d, attn_norm[l][None], w_main, colscale, mla_q_norm[l][None],
                                                    wq_big, mla_kv_norm[l][None], wkv, tabs, seq)
        a3, b3, d3 = (v.reshape(bsz, seq, -1) for v in (a, b, d))
        sub128 = jnp.concatenate([diff_subln[l], diff_subln[l]])[None]
        o_a = _diff_attention(a3, avt, slopes_a_l2, causal_mask_t[:tq_diff, :tq_diff], alibi_kb, diff_lambda[l],
                              sub128, l, tq_diff)
        o_b = _band_attention(b3, bvt, 1, 2, dil_bias, dil_m0, 0.0, tq_dil, "dil_attn")
        o_c = _mla_attention(cq.reshape(bsz, seq, -1), ck.reshape(bsz, seq, -1), cvt, causal_mask_t, tq_dense)
        swa_m0 = (swa_sinks[l].astype(F32) * LOG2E)[np.array([0, 2, 1, 3])]
        o_d = _band_attention(d3, dvt, 2, 1, swa_bias, swa_m0, 1.0, tq_band, "swa_attn")
        outs = [v.reshape(t, 256) for v in (o_a, o_b, o_c, o_d)]
        j = l // 2
        if l % 2 == 0:
            x1, h2 = _outproj(*outs, w4, x2d, ffn_norm[l][None])
            x2d = _ffn(h2, ffn_w1[j].astype(BF16), ffn_w3[j].astype(BF16), ffn_w2[j].astype(BF16), x1)
            moe = None
        else:
            r = jnp.pad(moe_router[j], ((0, 0), (0, LANES - N_EXPERTS)))
            r_hi, r_lo = _split_bf16(r, 2)
            x1, h2, logits = _outproj(*outs, w4, x2d, ffn_norm[l][None], (r_hi, r_lo))
            bm = _pick(t, 512)
            moe = _moe_layer(h2, logits, moe_w1[j].astype(BF16), moe_w3[j].astype(BF16),
                             moe_w2[j].astype(BF16), bm)
            x2d = x1
        if l < depth - 1 and moe is not None:
            raise NotImplementedError("MoE combine is fused into the final norm; MoE layer must be last")
    return _final(x2d, final_norm[None], moe).reshape(bsz, seq, D_MODEL)
```

```python
import functools
import math

import numpy as np
import jax
import jax.numpy as jnp
from jax import lax
from jax.experimental import pallas as pl
from jax.experimental.pallas import tpu as pltpu
from jax.experimental.pallas import tpu_sc as plsc

D_MODEL = 1024
DIFF_HEADS, DIFF_HD = 4, 32
DIL_HEADS, DIL_HD = 4, 64
DIL_PATTERNS = ((128, 1), (512, 4), (2048, 16))
MLA_HEADS, MLA_Q_LORA, MLA_KV_LORA, MLA_NOPE, MLA_ROPE, MLA_V = 4, 384, 128, 64, 32, 64
ROPE_THETA = 10000.0
SWA_HEADS, SWA_KV_HEADS, SWA_HD, SWA_WINDOW = 4, 2, 64, 128
D_FF = 2816
N_EXPERTS, TOP_K, D_FF_EXPERT = 8, 2, 3584
RMS_EPS = 1e-6
NEG_INF = -1e30
N_ALIBI = DIFF_HEADS + DIL_HEADS + SWA_HEADS

LOG2E = 1.4426950408889634
LANES = 128
HEAD_LANES = 64
VMEM_LIMIT = 56 * 1024 * 1024

F32 = jnp.float32
BF16 = jnp.bfloat16

_NT = (((1,), (1,)), ((), ()))


def _params(sem, vmem=VMEM_LIMIT):
    return pltpu.CompilerParams(dimension_semantics=sem, vmem_limit_bytes=vmem)


def _alibi_slopes():
    s = 2.0 ** (-8.0 * (np.arange(N_ALIBI) + 1) / N_ALIBI)
    return s[0::3], s[1::3], s[2::3]


def _pick(n, pref):
    t = min(pref, n)
    while n % t:
        t //= 2
    return t


N_SCALED = 2048
C_CQ, C_CKV, C_KPE, C_KPE_SW, N_MAIN = 2048, 2432, 2560, 2688, 2816


ONES_ROWS = 16
VT_ROWS = HEAD_LANES + ONES_ROWS


def _vt_with_ones(v):
    vt = v.T
    ones = jnp.ones((ONES_ROWS, v.shape[0]), F32)
    parts = []
    for hd in range(v.shape[1] // HEAD_LANES):
        parts += [vt[hd * HEAD_LANES:(hd + 1) * HEAD_LANES], ones]
    return jnp.concatenate(parts, axis=0).astype(BF16)


def _proj_body(x_ref, g_ref, w_ref, cs_ref, qn_ref, wq_ref, kn_ref, wkv_ref,
               cosq_ref, sinq_ref, cosk_ref, sink_ref,
               a_ref, avt_ref, b_ref, bvt_ref, d_ref, dvt_ref, cq_ref, ck_ref, cvt_ref):
    x = x_ref[...]
    ms = jnp.mean(x * x, axis=-1, keepdims=True)
    h = (x * lax.rsqrt(ms + RMS_EPS) * g_ref[...]).astype(BF16)
    acc = jnp.dot(h, w_ref[...], preferred_element_type=F32)
    sc = acc[:, :N_SCALED] * cs_ref[...]
    a_ref[...] = sc[:, 0:512].astype(BF16)
    avt_ref[...] = _vt_with_ones(sc[:, 512:768])
    b_ref[...] = sc[:, 768:1280].astype(BF16)
    bvt_ref[...] = _vt_with_ones(sc[:, 1280:1536])
    d_ref[...] = sc[:, 1536:1920].astype(BF16)
    dvt_ref[...] = _vt_with_ones(sc[:, 1920:2048])

    cq = acc[:, C_CQ:C_CKV]
    hq = (cq * lax.rsqrt(jnp.mean(cq * cq, axis=-1, keepdims=True) + RMS_EPS) * qn_ref[...]).astype(BF16)
    yq = jnp.dot(hq, wq_ref[...], preferred_element_type=F32)
    cosq, sinq = cosq_ref[...], sinq_ref[...]
    c_mla = (MLA_NOPE + MLA_ROPE) ** -0.5 * LOG2E
    for hd in range(MLA_HEADS):
        main = yq[:, hd * 256:(hd + 1) * 256]
        sw = yq[:, 1024 + hd * LANES:1024 + (hd + 1) * LANES]
        cq_ref[:, hd * 256:hd * 256 + LANES] = (main[:, :LANES] * cosq + sw * sinq).astype(BF16)
        cq_ref[:, hd * 256 + LANES:(hd + 1) * 256] = (main[:, LANES:] * c_mla).astype(BF16)

    ckv = acc[:, C_CKV:C_KPE]
    hk = (ckv * lax.rsqrt(jnp.mean(ckv * ckv, axis=-1, keepdims=True) + RMS_EPS) * kn_ref[...]).astype(BF16)
    kv = jnp.dot(hk, wkv_ref[...], preferred_element_type=F32)
    kr = (acc[:, C_KPE:C_KPE_SW] * cosk_ref[...] + acc[:, C_KPE_SW:N_MAIN] * sink_ref[...]).astype(BF16)
    for grp in range(2):
        ck_ref[:, grp * 256:grp * 256 + LANES] = kr
        ck_ref[:, grp * 256 + LANES:(grp + 1) * 256] = kv[:, grp * LANES:(grp + 1) * LANES].astype(BF16)
    cvt_ref[...] = _vt_with_ones(kv[:, 256:512])


def _proj(x2d, g, w_main, colscale, qn, wq_big, kn, wkv, tabs, seq):
    t = x2d.shape[0]
    tm = _pick(seq, 512)
    nsb = seq // tm
    full = lambda shape: pl.BlockSpec(shape, lambda i: (0,) * len(shape))
    tab = pl.BlockSpec((tm, LANES), lambda i: (i % nsb, 0))
    row = lambda w: pl.BlockSpec((tm, w), lambda i: (i, 0))
    vt = lambda w: pl.BlockSpec((None, w, tm), lambda i: (i // nsb, 0, i % nsb))
    row_sds = lambda w: jax.ShapeDtypeStruct((t, w), BF16)
    vt_sds = lambda w: jax.ShapeDtypeStruct((t // seq, w, seq), BF16)
    return pl.pallas_call(
        _proj_body,
        grid=(t // tm,),
        in_specs=[row(D_MODEL), full((1, D_MODEL)), full((D_MODEL, N_MAIN)), full((1, N_SCALED)),
                  full((1, MLA_Q_LORA)), full((MLA_Q_LORA, 1536)), full((1, MLA_KV_LORA)),
                  full((MLA_KV_LORA, 512)), tab, tab, tab, tab],
        out_specs=[row(512), vt(4 * VT_ROWS), row(512), vt(4 * VT_ROWS), row(384), vt(2 * VT_ROWS),
                   row(1024), row(512), vt(4 * VT_ROWS)],
        out_shape=[row_sds(512), vt_sds(4 * VT_ROWS), row_sds(512), vt_sds(4 * VT_ROWS), row_sds(384),
                   vt_sds(2 * VT_ROWS), row_sds(1024), row_sds(512), vt_sds(4 * VT_ROWS)],
        compiler_params=_params(("parallel",)),
        name="proj",
    )(x2d, g, w_main, colscale, qn, wq_big, kn, wkv, *tabs)


def _attn_pipeline(n_soft, n_steps, scores, shift, load_vt, m0, s_ref, acc_ref):
    def stage_a(t, ss, dst_ref, m_cur):
        m_next, alpha = [], []
        for i in range(n_soft):
            m_new = jnp.maximum(m_cur[i], jnp.max(ss[i], axis=0, keepdims=True) + shift(i, t))
            dst_ref[i] = ss[i]
            m_next.append(m_new)
            alpha.append(jnp.exp2(m_cur[i] - m_new))
        return tuple(m_next), tuple(alpha)

    def stage_b(t, src_ref, carry):
        m_cur, alpha = carry
        for i in range(n_soft):
            p = jnp.exp2(src_ref[i] - (m_cur[i] - shift(i, t)))
            acc_ref[i] = alpha[i] * acc_ref[i] + jnp.dot(load_vt(t, i), p.astype(BF16),
                                                         preferred_element_type=F32)

    def half(t, src_ref, dst_ref, carry):
        ss = [scores(i, t + 1, False) for i in range(n_soft)]
        stage_b(t, src_ref, carry)
        return stage_a(t + 1, ss, dst_ref, carry[0])

    def body(u, carry):
        carry = half(2 * u, s_ref[0], s_ref[1], carry)
        return half(2 * u + 1, s_ref[1], s_ref[0], carry)

    carry = stage_a(0, [scores(i, 0, True) for i in range(n_soft)], s_ref[0], m0)
    carry = lax.fori_loop(0, n_steps // 2, body, carry)

    @pl.when(n_steps % 2 == 1)
    def _():
        stage_b(n_steps, s_ref[1], half(n_steps - 1, s_ref[0], s_ref[1], carry))

    @pl.when(n_steps % 2 == 0)
    def _():
        stage_b(n_steps, s_ref[0], carry)


def _row_mask(shape, lo, hi):
    row = lax.broadcasted_iota(jnp.int32, shape, 0)
    return jnp.logical_and(row >= lo, row < hi)


def _lane_mask(shape, lo, hi):
    lane = lax.broadcasted_iota(jnp.int32, shape, len(shape) - 1)
    return jnp.logical_and(lane >= lo, lane < hi)


def _diag_scores(k, q_ref, mask_ref):
    tq = q_ref.shape[0]
    h = tq // 2
    left = lax.dot_general(k[:h], q_ref[0:h, :], _NT, preferred_element_type=F32)
    right = lax.dot_general(k, q_ref[h:tq, :], _NT, preferred_element_type=F32)
    left = jnp.concatenate([left, jnp.full((k.shape[0] - h, h), NEG_INF, F32)], axis=0)
    return jnp.concatenate([left, right], axis=1) + mask_ref[...]


def _normalized(acc_ref, i):
    return acc_ref[i, 0:HEAD_LANES, :] / acc_ref[i, HEAD_LANES:HEAD_LANES + 1, :]


def _diff_body(sl_ref, q_ref, k_ref, vt_ref, mask_ref, kb_ref, lam_ref, sub_ref, o_ref,
               qs_ref, s0_ref, s1_ref, acc_ref, *, tq, lam_init):
    qi = pl.program_id(1)
    tk = tq
    n_soft = 2 * DIFF_HEADS
    head_of = lambda i: i // 2
    for pair in range(2):
        q = q_ref[:, pair * LANES:(pair + 1) * LANES].astype(F32)
        for hd in range(2):
            sel = jnp.where(_lane_mask(q.shape, N_BIAS_SPLIT * hd, N_BIAS_SPLIT * (hd + 1)), 1.0, 0.0).astype(BF16)
            for mp in range(2):
                i = 4 * pair + 2 * hd + mp
                lo = hd * HEAD_LANES + mp * DIFF_HD
                qs_ref[i, :, 0:LANES] = jnp.where(_lane_mask(q.shape, lo, lo + DIFF_HD), q, 0.0).astype(BF16)
                qs_ref[i, :, LANES:2 * LANES] = sel
    acc_ref[...] = jnp.zeros(acc_ref.shape, F32)

    def scores(i, t, first):
        off = pl.multiple_of((qi - t) * tk, tk)
        pair = i // 4
        k = jnp.concatenate([k_ref[pl.ds(off, tk), pair * LANES:(pair + 1) * LANES], kb_ref[pair]], axis=1)
        if first:
            return _diag_scores(k, qs_ref.at[i], mask_ref)
        return lax.dot_general(k, qs_ref[i], _NT, preferred_element_type=F32)

    def shift(i, t):
        return sl_ref[head_of(i)] * (-(t * tk)).astype(F32) if not isinstance(t, int) else 0.0

    def load_vt(t, i):
        return vt_ref[pl.ds(head_of(i) * VT_ROWS, VT_ROWS), pl.ds(pl.multiple_of((qi - t) * tk, tk), tk)]

    m0 = tuple(jnp.full((1, tq), NEG_INF, F32) for _ in range(n_soft))
    _attn_pipeline(n_soft, qi, scores, shift, load_vt, m0, (s0_ref, s1_ref), acc_ref)

    lp = lam_ref[...]
    lam = (jnp.exp(jnp.sum(lp[0:1, :] * lp[1:2, :], axis=-1, keepdims=True))
           - jnp.exp(jnp.sum(lp[2:3, :] * lp[3:4, :], axis=-1, keepdims=True)) + lam_init)
    in_h0 = _lane_mask((tq, LANES), 0, HEAD_LANES)
    for pair in range(2):
        o_h = [_normalized(acc_ref, 4 * pair + 2 * hd) - lam * _normalized(acc_ref, 4 * pair + 2 * hd + 1)
               for hd in range(2)]
        o = jnp.concatenate(o_h, axis=0).T
        sq = o * o
        ms0 = jnp.sum(jnp.where(in_h0, sq, 0.0), axis=-1, keepdims=True)
        ms1 = jnp.sum(jnp.where(in_h0, 0.0, sq), axis=-1, keepdims=True)
        ms = jnp.where(in_h0, ms0, ms1) * (1.0 / HEAD_LANES)
        y = o * lax.rsqrt(ms + RMS_EPS) * sub_ref[...]
        o_ref[:, pair * LANES:(pair + 1) * LANES] = (y * (1.0 - lam_init)).astype(o_ref.dtype)


N_BIAS_SPLIT = 3


def _alibi_key_columns(slopes_l2, tk):
    dj = jnp.arange(tk, dtype=F32)[None, :, None]
    val = jnp.asarray(slopes_l2, F32).reshape(-1, 1, 2) * dj
    cols = jnp.stack(_split_bf16(val, N_BIAS_SPLIT), axis=-1).reshape(val.shape[0], tk, 2 * N_BIAS_SPLIT)
    return jnp.pad(cols, ((0, 0), (0, 0), (0, LANES - 2 * N_BIAS_SPLIT)))


def _split_bf16(x, n):
    pieces = []
    for _ in range(n):
        bits = lax.bitcast_convert_type(x, jnp.uint32) & jnp.uint32(0xFFFF0000)
        head = lax.bitcast_convert_type(bits, F32)
        pieces.append(head.astype(BF16))
        x = x - head
    return pieces


def _diff_attention(a3, avt, slopes_l2, mask_t, kb, lam_p, sub128, layer_idx, tq):
    b, s, _ = a3.shape
    nq = s // tq
    lam_init = 0.8 - 0.6 * math.exp(-0.3 * layer_idx)
    body = functools.partial(_diff_body, tq=tq, lam_init=lam_init)
    grid_spec = pltpu.PrefetchScalarGridSpec(
        num_scalar_prefetch=1,
        grid=(b, nq),
        in_specs=[
            pl.BlockSpec((None, tq, 2 * LANES), lambda bi, qi, sl: (bi, qi, 0)),
            pl.BlockSpec((None, s, 2 * LANES), lambda bi, qi, sl: (bi, 0, 1)),
            pl.BlockSpec((None, DIFF_HEADS * VT_ROWS, s), lambda bi, qi, sl: (bi, 0, 0)),
            pl.BlockSpec((tq, tq), lambda bi, qi, sl: (0, 0), pipeline_mode=pl.Buffered(1)),
            pl.BlockSpec((2, tq, LANES), lambda bi, qi, sl: (0, 0, 0), pipeline_mode=pl.Buffered(1)),
            pl.BlockSpec((4, DIFF_HD), lambda bi, qi, sl: (0, 0)),
            pl.BlockSpec((1, LANES), lambda bi, qi, sl: (0, 0)),
        ],
        out_specs=pl.BlockSpec((None, tq, 2 * LANES), lambda bi, qi, sl: (bi, qi, 0)),
        scratch_shapes=[pltpu.VMEM((2 * DIFF_HEADS, tq, 2 * LANES), BF16),
                        pltpu.VMEM((2 * DIFF_HEADS, tq, tq), F32), pltpu.VMEM((2 * DIFF_HEADS, tq, tq), F32),
                        pltpu.VMEM((2 * DIFF_HEADS, VT_ROWS, tq), F32)],
    )
    return pl.pallas_call(
        body, grid_spec=grid_spec,
        out_shape=jax.ShapeDtypeStruct((b, s, 256), BF16),
        compiler_params=_params(("parallel", "arbitrary")),
        name="diff_attn",
    )(slopes_l2, a3, a3, avt, mask_t, kb, lam_p, sub128)


def _mla_body(q_ref, k_ref, vt_ref, mask_ref, o_ref, s0_ref, s1_ref, acc_ref, *, tq):
    qi = pl.program_id(1)
    tk = tq
    acc_ref[...] = jnp.zeros(acc_ref.shape, F32)

    def scores(i, t, first):
        k = k_ref[pl.ds(pl.multiple_of((qi - t) * tk, tk), tk), (i // 2) * 256:(i // 2 + 1) * 256]
        if first:
            return _diag_scores(k, q_ref.at[:, i * 256:(i + 1) * 256], mask_ref)
        return lax.dot_general(k, q_ref[:, i * 256:(i + 1) * 256], _NT, preferred_element_type=F32)

    def load_vt(t, i):
        return vt_ref[pl.ds(i * VT_ROWS, VT_ROWS), pl.ds(pl.multiple_of((qi - t) * tk, tk), tk)]

    m0 = tuple(jnp.full((1, tq), NEG_INF, F32) for _ in range(MLA_HEADS))
    _attn_pipeline(MLA_HEADS, qi, scores, lambda i, t: 0.0, load_vt, m0, (s0_ref, s1_ref), acc_ref)
    o_t = jnp.concatenate([_normalized(acc_ref, i) for i in range(MLA_HEADS)], axis=0)
    o_ref[...] = o_t.T.astype(o_ref.dtype)


def _mla_attention(cq3, ck3, cvt, mask_t, tq):
    b, s, _ = cq3.shape
    nq = s // tq
    return pl.pallas_call(
        functools.partial(_mla_body, tq=tq),
        grid=(b, nq),
        in_specs=[
            pl.BlockSpec((None, tq, MLA_HEADS * 256), lambda bi, qi: (bi, qi, 0)),
            pl.BlockSpec((None, s, 512), lambda bi, qi: (bi, 0, 0)),
            pl.BlockSpec((None, MLA_HEADS * VT_ROWS, s), lambda bi, qi: (bi, 0, 0)),
            pl.BlockSpec((tq, tq), lambda bi, qi: (0, 0)),
        ],
        out_specs=pl.BlockSpec((None, tq, 256), lambda bi, qi: (bi, qi, 0)),
        out_shape=jax.ShapeDtypeStruct((b, s, 256), BF16),
        scratch_shapes=[pltpu.VMEM((MLA_HEADS, tq, tq), F32), pltpu.VMEM((MLA_HEADS, tq, tq), F32),
                        pltpu.VMEM((MLA_HEADS, VT_ROWS, tq), F32)],
        compiler_params=_params(("parallel", "arbitrary")),
        name="mla_attn",
    )(cq3, ck3, cvt, mask_t)


def _band_body(m0_ref, q_ref, k_ref, vt_ref, bias_ref, o_ref, qs_ref, s0_ref, s1_ref, acc_ref,
               *, tq, n_off, l_init, kv_slabs):
    qi = pl.program_id(1)
    tk = tq
    n_soft = 4
    for sl in range(2):
        q = q_ref[:, sl * LANES:(sl + 1) * LANES].astype(F32)
        for g in range(2):
            qs_ref[2 * sl + g] = jnp.where(_lane_mask(q.shape, g * HEAD_LANES, (g + 1) * HEAD_LANES), q, 0.0
                                           ).astype(BF16)
    for i in range(n_soft):
        acc_ref[i] = jnp.where(_row_mask((VT_ROWS, tq), 0, HEAD_LANES), 0.0, l_init).astype(F32)

    kv_of = lambda i: (i // 2) * (kv_slabs - 1)

    def scores(i, t, first):
        k = k_ref[pl.ds(pl.multiple_of((qi - t) * tk, tk), tk), kv_of(i) * LANES:(kv_of(i) + 1) * LANES]
        return lax.dot_general(k, qs_ref[i], _NT, preferred_element_type=F32) + bias_ref[i // 2, i % 2, t]

    def load_vt(t, i):
        return vt_ref[pl.ds((2 * kv_of(i) + i % 2) * VT_ROWS, VT_ROWS),
                      pl.ds(pl.multiple_of((qi - t) * tk, tk), tk)]

    m0 = tuple(jnp.full((1, tq), m0_ref[i], F32) for i in range(n_soft))
    _attn_pipeline(n_soft, jnp.minimum(qi, n_off - 1), scores, lambda i, t: 0.0, load_vt, m0,
                   (s0_ref, s1_ref), acc_ref)
    o_t = jnp.concatenate([_normalized(acc_ref, i) for i in range(n_soft)], axis=0)
    o_ref[...] = o_t.T.astype(o_ref.dtype)


def _band_attention(x3, xvt, k_blk, kv_slabs, bias, m0, l_init, tq, name):
    b, s, _ = x3.shape
    nq = s // tq
    n_off = bias.shape[2]
    n_soft = 4
    grid_spec = pltpu.PrefetchScalarGridSpec(
        num_scalar_prefetch=1,
        grid=(b, nq),
        in_specs=[
            pl.BlockSpec((None, tq, 2 * LANES), lambda bi, qi, m0r: (bi, qi, 0)),
            pl.BlockSpec((None, s, kv_slabs * LANES), lambda bi, qi, m0r: (bi, 0, k_blk)),
            pl.BlockSpec((None, kv_slabs * 2 * VT_ROWS, s), lambda bi, qi, m0r: (bi, 0, 0)),
            pl.BlockSpec((2, 2, n_off, tq, tq), lambda bi, qi, m0r: (0, 0, 0, 0, 0),
                         pipeline_mode=pl.Buffered(1)),
        ],
        out_specs=pl.BlockSpec((None, tq, 2 * LANES), lambda bi, qi, m0r: (bi, qi, 0)),
        scratch_shapes=[pltpu.VMEM((n_soft, tq, LANES), BF16), pltpu.VMEM((n_soft, tq, tq), F32),
                        pltpu.VMEM((n_soft, tq, tq), F32), pltpu.VMEM((n_soft, VT_ROWS, tq), F32)],
    )
    return pl.pallas_call(
        functools.partial(_band_body, tq=tq, n_off=n_off, l_init=l_init, kv_slabs=kv_slabs),
        grid_spec=grid_spec,
        out_shape=jax.ShapeDtypeStruct((b, s, 256), BF16),
        compiler_params=_params(("parallel", "arbitrary")),
        name=name,
    )(m0, x3, x3, xvt, bias)


def _band_bias(slopes, tq, n_off, mult_fn):
    di = jnp.arange(tq, dtype=jnp.int32)[None, :]
    dj = jnp.arange(tq, dtype=jnp.int32)[:, None]
    off = jnp.arange(n_off, dtype=jnp.int32)[:, None, None]
    delta = off * tq + di - dj
    mult = mult_fn(delta)
    sl = jnp.asarray(slopes, F32)[:, None, None, None] * LOG2E
    val = -sl * delta.astype(F32)[None] + jnp.log2(jnp.maximum(mult, 1).astype(F32))[None]
    return jnp.where((mult > 0)[None], val, NEG_INF)


def _dil_mult(delta):
    m = jnp.zeros(delta.shape, jnp.int32)
    for window, d in DIL_PATTERNS:
        m = m + ((delta >= 0) & (delta <= window) & (delta % d == 0)).astype(jnp.int32)
    return m


def _swa_mult(delta):
    return ((delta >= 0) & (delta <= SWA_WINDOW - 1)).astype(jnp.int32)


def _outproj_body(*refs, with_router):
    if with_router:
        (oa, ob, oc, od, w_ref, x_ref, g_ref, rh_ref, rl_ref, x1_ref, h2_ref, lg_ref) = refs
    else:
        (oa, ob, oc, od, w_ref, x_ref, g_ref, x1_ref, h2_ref) = refs
    acc = x_ref[...]
    for i, o in enumerate((oa, ob, oc, od)):
        acc = acc + jnp.dot(o[...], w_ref[i], preferred_element_type=F32)
    x1_ref[...] = acc
    h2 = acc * lax.rsqrt(jnp.mean(acc * acc, axis=-1, keepdims=True) + RMS_EPS) * g_ref[...]
    hi = h2.astype(BF16)
    if with_router:
        for c in range(SC_CHUNKS):
            h2_ref[c] = h2[:, c * SC_COLS:(c + 1) * SC_COLS]
        lo = (h2 - hi.astype(F32)).astype(BF16)
        lg_ref[...] = (jnp.dot(hi, rh_ref[...], preferred_element_type=F32)
                       + (jnp.dot(hi, rl_ref[...], preferred_element_type=F32)
                          + jnp.dot(lo, rh_ref[...], preferred_element_type=F32)))
    else:
        h2_ref[...] = hi


def _outproj(oa, ob, oc, od, w4, x2d, g, router_hl=None):
    t = x2d.shape[0]
    tm = _pick(t, 512)
    with_router = router_hl is not None
    full = lambda shape: pl.BlockSpec(shape, lambda i: (0,) * len(shape))
    row = lambda w: pl.BlockSpec((tm, w), lambda i: (i, 0))
    in_specs = [row(256)] * 4 + [full((4, 256, D_MODEL)), row(D_MODEL), full((1, D_MODEL))]
    args = [oa, ob, oc, od, w4, x2d, g]
    out_specs = [row(D_MODEL), row(D_MODEL)]
    out_shape = [jax.ShapeDtypeStruct((t, D_MODEL), F32), jax.ShapeDtypeStruct((t, D_MODEL), BF16)]
    if with_router:
        out_specs[1] = pl.BlockSpec((SC_CHUNKS, tm, SC_COLS), lambda i: (0, i, 0))
        out_shape[1] = jax.ShapeDtypeStruct((SC_CHUNKS, t, SC_COLS), F32)
        in_specs += [full((D_MODEL, LANES))] * 2
        args += list(router_hl)
        out_specs.append(row(LANES))
        out_shape.append(jax.ShapeDtypeStruct((t, LANES), F32))
    return pl.pallas_call(
        functools.partial(_outproj_body, with_router=with_router),
        grid=(t // tm,), in_specs=in_specs, out_specs=out_specs, out_shape=out_shape,
        compiler_params=_params(("parallel",)),
        name="outproj",
    )(*args)


FF_CHUNK = 512


def _swiglu(x, w1_ref, w3_ref, w2_ref):
    tf = w1_ref.shape[1]
    total = None
    for lo in range(0, tf, FF_CHUNK):
        hi = min(lo + FF_CHUNK, tf)
        a = jnp.dot(x, w1_ref[:, lo:hi], preferred_element_type=F32)
        b = jnp.dot(x, w3_ref[:, lo:hi], preferred_element_type=F32)
        mid = (a * jax.nn.sigmoid(a) * b).astype(BF16)
        part = jnp.dot(mid, w2_ref[lo:hi, :], preferred_element_type=F32)
        total = part if total is None else total + part
    return total


def _ffn_body(h_ref, w1_ref, w3_ref, w2_ref, x_ref, o_ref):
    o_ref[...] = x_ref[...] + _swiglu(h_ref[...], w1_ref, w3_ref, w2_ref)


def _ffn(h2, w1, w3, w2, x1):
    t = h2.shape[0]
    tm = _pick(t, 512)
    resident = lambda shape: pl.BlockSpec(shape, lambda i: (0, 0), pipeline_mode=pl.Buffered(1))
    return pl.pallas_call(
        _ffn_body,
        grid=(t // tm,),
        in_specs=[pl.BlockSpec((tm, D_MODEL), lambda i: (i, 0)),
                  resident((D_MODEL, D_FF)), resident((D_MODEL, D_FF)), resident((D_FF, D_MODEL)),
                  pl.BlockSpec((tm, D_MODEL), lambda i: (i, 0))],
        out_specs=pl.BlockSpec((tm, D_MODEL), lambda i: (i, 0)),
        out_shape=jax.ShapeDtypeStruct((t, D_MODEL), F32),
        compiler_params=_params(("parallel",)),
        name="ffn",
    )(h2, w1, w3, w2, x1)


def _route_body(lg_ref, tri_ref, info_ref, cnt_ref, carry_ref):
    i = pl.program_id(0)

    @pl.when(i == 0)
    def _():
        carry_ref[...] = jnp.zeros(carry_ref.shape, F32)

    lg = lg_ref[...]
    lane = lax.broadcasted_iota(jnp.int32, lg.shape, 1)
    lg = jnp.where(lane < N_EXPERTS, lg, -jnp.inf)
    m1 = jnp.max(lg, axis=-1, keepdims=True)
    i1 = jnp.min(jnp.where(lg == m1, lane, LANES), axis=-1, keepdims=True)
    oh1 = lane == i1
    lg2 = jnp.where(oh1, -jnp.inf, lg)
    m2 = jnp.max(lg2, axis=-1, keepdims=True)
    i2 = jnp.min(jnp.where(lg2 == m2, lane, LANES), axis=-1, keepdims=True)
    oh2 = lane == i2
    e2 = jnp.exp(m2 - m1)
    g1 = 1.0 / (1.0 + e2)
    g2 = e2 / (1.0 + e2)
    oh = jnp.where(jnp.logical_or(oh1, oh2), 1.0, 0.0)
    rank = jnp.dot(tri_ref[...], oh.astype(BF16), preferred_element_type=F32) + carry_ref[...]
    r1 = jnp.sum(jnp.where(oh1, rank, 0.0), axis=-1, keepdims=True)
    r2 = jnp.sum(jnp.where(oh2, rank, 0.0), axis=-1, keepdims=True)
    carry_ref[...] += jnp.sum(oh, axis=0, keepdims=True)
    cnt_ref[...] = carry_ref[...]
    info = jnp.zeros(lg.shape, F32)
    for c, val in enumerate((i1.astype(F32), i2.astype(F32), r1, r2, g1, g2)):
        info = jnp.where(lane == c, val, info)
    info_ref[...] = info


def _route(logits):
    t = logits.shape[0]
    tm = _pick(t, 512)
    tri = (jnp.arange(tm)[:, None] > jnp.arange(tm)[None, :]).astype(BF16)
    return pl.pallas_call(
        _route_body,
        grid=(t // tm,),
        in_specs=[pl.BlockSpec((tm, LANES), lambda i: (i, 0)), pl.BlockSpec((tm, tm), lambda i: (0, 0))],
        out_specs=[pl.BlockSpec((tm, LANES), lambda i: (i, 0)), pl.BlockSpec((1, LANES), lambda i: (0, 0))],
        out_shape=[jax.ShapeDtypeStruct((t, LANES), F32), jax.ShapeDtypeStruct((1, LANES), F32)],
        scratch_shapes=[pltpu.VMEM((1, LANES), F32)],
        compiler_params=_params(("arbitrary",)),
        name="route",
    )(logits, tri)


def _moe_ffn_body(be_ref, bv_ref, x_ref, w1_ref, w3_ref, w2_ref, o_ref):
    valid = bv_ref[pl.program_id(0)]

    @pl.when(valid > 0)
    def _():
        x = jnp.concatenate([x_ref[c] for c in range(SC_CHUNKS)], axis=1)
        x = jnp.where(lax.broadcasted_iota(jnp.int32, x.shape, 0) < valid, x, 0.0).astype(BF16)
        y = _swiglu(x, w1_ref, w3_ref, w2_ref)
        for c in range(SC_CHUNKS):
            o_ref[c] = y[:, c * SC_COLS:(c + 1) * SC_COLS]

    @pl.when(valid == 0)
    def _():
        o_ref[...] = jnp.zeros(o_ref.shape, o_ref.dtype)


def _moe_ffn(xs, blk_expert, blk_valid, w1, w3, w2, bm):
    rows = xs.shape[1]
    chunked = pl.BlockSpec((SC_CHUNKS, bm, SC_COLS), lambda i, be, bv: (0, i, 0))
    expert = lambda shape: pl.BlockSpec((None,) + shape, lambda i, be, bv: (be[i], 0, 0),
                                        pipeline_mode=pl.Buffered(1))
    grid_spec = pltpu.PrefetchScalarGridSpec(
        num_scalar_prefetch=2,
        grid=(rows // bm,),
        in_specs=[chunked, expert((D_MODEL, D_FF_EXPERT)), expert((D_MODEL, D_FF_EXPERT)),
                  expert((D_FF_EXPERT, D_MODEL))],
        out_specs=chunked,
    )
    return pl.pallas_call(
        _moe_ffn_body, grid_spec=grid_spec,
        out_shape=jax.ShapeDtypeStruct(xs.shape, F32),
        compiler_params=_params(("arbitrary",)),
        name="moe_ffn",
    )(blk_expert, blk_valid, xs, w1, w3, w2)


def _final_body(*refs, with_moe):
    if with_moe:
        x_ref, y0_ref, y1_ref, info_ref, g_ref, o_ref = refs
        info = info_ref[...]
        y0 = jnp.concatenate([y0_ref[c] for c in range(SC_CHUNKS)], axis=1)
        y1 = jnp.concatenate([y1_ref[c] for c in range(SC_CHUNKS)], axis=1)
        x = x_ref[...] + (info[:, 4:5] * y0 + info[:, 5:6] * y1)
    else:
        x_ref, g_ref, o_ref = refs
        x = x_ref[...]
    o_ref[...] = x * lax.rsqrt(jnp.mean(x * x, axis=-1, keepdims=True) + RMS_EPS) * g_ref[...]


def _final(x2d, g, moe=None):
    t = x2d.shape[0]
    tm = _pick(t, 512)
    row = lambda w: pl.BlockSpec((tm, w), lambda i: (i, 0))
    in_specs, args = [row(D_MODEL)], [x2d]
    if moe is not None:
        yg, info = moe
        slot = lambda k: pl.BlockSpec((None, SC_CHUNKS, tm, SC_COLS), lambda i: (k, 0, i, 0))
        in_specs += [slot(0), slot(1), row(LANES)]
        args += [yg, yg, info]
    in_specs.append(pl.BlockSpec((1, D_MODEL), lambda i: (0, 0)))
    args.append(g)
    return pl.pallas_call(
        functools.partial(_final_body, with_moe=moe is not None),
        grid=(t // tm,), in_specs=in_specs, out_specs=row(D_MODEL),
        out_shape=jax.ShapeDtypeStruct((t, D_MODEL), F32),
        compiler_params=_params(("parallel",)),
        name="final",
    )(*args)


def _prep_attn_weights(w_in, w_out, w_uq, w_ukv):
    widths = (256, 256, 256, 256, 256, 256, MLA_Q_LORA, MLA_KV_LORA, MLA_ROPE, 256, 128, 128)
    offs = np.concatenate([[0], np.cumsum(widths)])
    qa, ka, va, qb, kb, vb, cq, ckv, kpe, qd, kd, vd = [w_in[:, offs[i]:offs[i + 1]] for i in range(12)]
    hperm = np.array([0, 2, 1, 3])
    qd = qd.reshape(D_MODEL, 4, SWA_HD)[:, hperm].reshape(D_MODEL, 256)
    pad = jnp.zeros((D_MODEL, LANES - MLA_ROPE), w_in.dtype)
    half = MLA_ROPE // 2
    kpe_sw = jnp.concatenate([kpe[:, half:], kpe[:, :half]], axis=1)
    w_main = jnp.concatenate([qa, ka, va, qb, kb, vb, qd, kd, vd, cq, ckv, kpe, pad, kpe_sw, pad], axis=1)

    uq = w_uq.reshape(MLA_Q_LORA, MLA_HEADS, MLA_NOPE + MLA_ROPE)
    z = lambda n: jnp.zeros((MLA_Q_LORA, n), w_uq.dtype)
    mains, sws = [], []
    for hd in range(MLA_HEADS):
        nope, rope = uq[:, hd, :MLA_NOPE], uq[:, hd, MLA_NOPE:]
        nope128 = jnp.concatenate([nope, z(64)] if hd % 2 == 0 else [z(64), nope], axis=1)
        mains.append(jnp.concatenate([rope, z(LANES - MLA_ROPE), nope128], axis=1))
        sws.append(jnp.concatenate([rope[:, half:], rope[:, :half], z(LANES - MLA_ROPE)], axis=1))
    wq_big = jnp.concatenate(mains + sws, axis=1)

    ukv = w_ukv.reshape(MLA_KV_LORA, MLA_HEADS, MLA_NOPE + MLA_V)
    wkv = jnp.concatenate([ukv[:, :, :MLA_NOPE].reshape(MLA_KV_LORA, 256),
                           ukv[:, :, MLA_NOPE:].reshape(MLA_KV_LORA, 256)], axis=1)

    wo = w_out.reshape(4, 256, D_MODEL)
    wo_d = wo[3].reshape(4, SWA_HD, D_MODEL)[hperm].reshape(256, D_MODEL)
    w4 = jnp.stack([wo[0], wo[1], wo[2], wo_d])
    return w_main.astype(BF16), wq_big.astype(BF16), wkv.astype(BF16), w4.astype(BF16)


def _rope_tables(s):
    inv = ROPE_THETA ** (-jnp.arange(0, MLA_ROPE, 2, dtype=F32) / MLA_ROPE)
    ang = jnp.arange(s, dtype=F32)[:, None] * inv[None, :]
    cos, sin = jnp.cos(ang), jnp.sin(ang)
    z = jnp.zeros((s, LANES - MLA_ROPE), F32)
    cos128 = jnp.concatenate([cos, cos, z], axis=1)
    sin128 = jnp.concatenate([-sin, sin, z], axis=1)
    c_mla = (MLA_NOPE + MLA_ROPE) ** -0.5 * LOG2E
    return cos128 * c_mla, sin128 * c_mla, cos128, sin128


def _col_scale():
    ca = DIFF_HD ** -0.5 * LOG2E
    cb = DIL_HD ** -0.5 * LOG2E
    cd = SWA_HD ** -0.5 * LOG2E
    v = np.ones((1, N_SCALED), np.float32)
    v[0, 0:256] = ca
    v[0, 768:1024] = cb
    v[0, 1536:1792] = cd
    return jnp.asarray(v)


def _moe_layer(h2, logits, w1, w3, w2, bm):
    t = logits.shape[0]
    info, counts = _route(logits)
    cnt = counts[0, :N_EXPERTS].astype(jnp.int32)
    padded = (cnt + bm - 1) // bm * bm
    pends = jnp.cumsum(padded)
    pstarts = pends - padded
    e12 = info[:, 0:2].astype(jnp.int32)
    pos = pstarts[e12] + info[:, 2:4].astype(jnp.int32)
    nb = (t * TOP_K) // bm + N_EXPERTS
    rows = nb * bm
    blk_start = jnp.arange(nb, dtype=jnp.int32) * bm
    blk_expert = jnp.minimum(jnp.sum((pends[None, :] <= blk_start[:, None]).astype(jnp.int32), axis=1),
                             N_EXPERTS - 1)
    blk_valid = jnp.clip(pstarts[blk_expert] + cnt[blk_expert] - blk_start, 0, bm).astype(jnp.int32)
    sub = (jnp.arange(SC_CHUNKS, dtype=jnp.int32) * rows)[None, :, None] + pos.T[:, None, :]
    xs = _sc_scatter_rows(h2.reshape(SC_CHUNKS * t, SC_COLS),
                          [sub[k].reshape(1, SC_CHUNKS * t) for k in range(TOP_K)], SC_CHUNKS * rows)
    ys = _moe_ffn(xs.reshape(SC_CHUNKS, rows, SC_COLS), blk_expert.astype(jnp.int32), blk_valid, w1, w3, w2, bm)
    yg = _sc_gather_rows(ys.reshape(SC_CHUNKS * rows, SC_COLS), sub.reshape(1, TOP_K * SC_CHUNKS * t))
    return yg.reshape(TOP_K, SC_CHUNKS, t, SC_COLS), info


SC_WINDOW = 128
SC_COLS = 256
SC_CHUNKS = D_MODEL // SC_COLS


def _sc_mesh():
    return plsc.VectorSubcoreMesh(core_axis_name="c", subcore_axis_name="s")


def _sc_scatter_rows(x, idx_list, rows):
    t, d = x.shape
    n = len(idx_list)

    @functools.partial(pl.kernel, out_type=jax.ShapeDtypeStruct((rows, d), x.dtype), mesh=_sc_mesh(),
                       scratch_types=[])
    def scatter_kernel(x_hbm, *rest):
        idx_hbms, o_hbm = rest[:n], rest[n]

        def body(x_vmem, *idx_vmems):
            for iv in idx_vmems:
                pltpu.sync_copy(x_vmem, o_hbm.at[iv.at[0]])

        pltpu.emit_pipeline(
            body, grid=(t // SC_WINDOW,),
            in_specs=[pl.BlockSpec((SC_WINDOW, d), lambda i: (i, 0))]
            + [pl.BlockSpec((1, SC_WINDOW), lambda i: (0, i))] * n,
            out_specs=[], core_axis_name=("c", "s"), dimension_semantics=(pltpu.PARALLEL,),
        )(x_hbm, *idx_hbms)

    return scatter_kernel(x, *idx_list)


def _sc_gather_rows(y, idx):
    m = idx.shape[1]
    d = y.shape[1]

    @functools.partial(pl.kernel, out_type=jax.ShapeDtypeStruct((m, d), y.dtype), mesh=_sc_mesh(),
                       scratch_types=[])
    def gather_kernel(y_hbm, idx_hbm, o_hbm):
        def body(idx_vmem, o_vmem):
            pltpu.sync_copy(y_hbm.at[idx_vmem.at[0]], o_vmem)

        pltpu.emit_pipeline(
            body, grid=(m // SC_WINDOW,),
            in_specs=[pl.BlockSpec((1, SC_WINDOW), lambda i: (0, i))],
            out_specs=[pl.BlockSpec((SC_WINDOW, d), lambda i: (i, 0))],
            core_axis_name=("c", "s"), dimension_semantics=(pltpu.PARALLEL,),
        )(idx_hbm, o_hbm)

    return gather_kernel(y, idx)


def kernel(x, attn_norm, w_in, w_out, diff_lambda, diff_subln, mla_q_norm, mla_w_uq, mla_kv_norm, mla_w_ukv,
           swa_sinks, ffn_norm, ffn_w1, ffn_w3, ffn_w2, moe_router, moe_w1, moe_w3, moe_w2, final_norm):
    bsz, seq, _ = x.shape
    t = bsz * seq
    depth = w_in.shape[0]
    slopes_a, slopes_b, slopes_d = _alibi_slopes()
    tq_dense = _pick(seq, 512)
    tq_band = _pick(seq, 256)

    tabs = _rope_tables(seq)
    colscale = _col_scale()
    di = jnp.arange(tq_dense)
    causal_mask_t = jnp.where(di[:, None] <= di[None, :], 0.0, NEG_INF).astype(F32)
    tq_diff = tq_dense
    alibi_kb = _alibi_key_columns(slopes_a * LOG2E, tq_diff)
    tq_dil = _pick(seq, 512)
    n_off_dil = min((max(w for w, _ in DIL_PATTERNS) + tq_dil - 1) // tq_dil + 1, seq // tq_dil)
    n_off_swa = min((SWA_WINDOW - 1 + tq_band - 1) // tq_band + 1, seq // tq_band)
    dil_bias = _band_bias(slopes_b, tq_dil, n_off_dil, _dil_mult).reshape(2, 2, n_off_dil, tq_dil, tq_dil)
    swa_bias = _band_bias(slopes_d[np.array([0, 2, 1, 3])], tq_band, n_off_swa, _swa_mult
                          ).reshape(2, 2, n_off_swa, tq_band, tq_band)
    slopes_a_l2 = jnp.asarray(slopes_a * LOG2E, F32)
    dil_m0 = jnp.full((4,), NEG_INF, F32)

    x2d = x.reshape(t, D_MODEL)
    for l in range(depth):
        w_main, wq_big, wkv, w4 = _prep_attn_weights(w_in[l], w_out[l], mla_w_uq[l], mla_w_ukv[l])
        a, avt, b, bvt, d, dvt, cq, ck, cvt = _proj(x2d, attn_norm[l][None], w_main, colscale, mla_q_norm[l][None],
                                                    wq_big, mla_kv_norm[l][None], wkv, tabs, seq)
        a3, b3, d3 = (v.reshape(bsz, seq, -1) for v in (a, b, d))
        sub128 = jnp.concatenate([diff_subln[l], diff_subln[l]])[None]
        o_a = _diff_attention(a3, avt, slopes_a_l2, causal_mask_t[:tq_diff, :tq_diff], alibi_kb, diff_lambda[l],
                              sub128, l, tq_diff)
        o_b = _band_attention(b3, bvt, 1, 2, dil_bias, dil_m0, 0.0, tq_dil, "dil_attn")
        o_c = _mla_attention(cq.reshape(bsz, seq, -1), ck.reshape(bsz, seq, -1), cvt, causal_mask_t, tq_dense)
        swa_m0 = (swa_sinks[l].astype(F32) * LOG2E)[np.array([0, 2, 1, 3])]
        o_d = _band_attention(d3, dvt, 2, 1, swa_bias, swa_m0, 1.0, tq_band, "swa_attn")
        outs = [v.reshape(t, 256) for v in (o_a, o_b, o_c, o_d)]
        j = l // 2
        if l % 2 == 0:
            x1, h2 = _outproj(*outs, w4, x2d, ffn_norm[l][None])
            x2d = _ffn(h2, ffn_w1[j].astype(BF16), ffn_w3[j].astype(BF16), ffn_w2[j].astype(BF16), x1)
            moe = None
        else:
            r = jnp.pad(moe_router[j], ((0, 0), (0, LANES - N_EXPERTS)))
            r_hi, r_lo = _split_bf16(r, 2)
            x1, h2, logits = _outproj(*outs, w4, x2d, ffn_norm[l][None], (r_hi, r_lo))
            bm = _pick(t, 512)
            moe = _moe_layer(h2, logits, moe_w1[j].astype(BF16), moe_w3[j].astype(BF16),
                             moe_w2[j].astype(BF16), bm)
            x2d = x1
        if l < depth - 1 and moe is not None:
            raise NotImplementedError("MoE combine is fused into the final norm; MoE layer must be last")
    return _final(x2d, final_norm[None], moe).reshape(bsz, seq, D_MODEL)
```

```python
import functools
import math

import numpy as np
import jax
import jax.numpy as jnp
from jax import lax
from jax.experimental import pallas as pl
from jax.experimental.pallas import tpu as pltpu
from jax.experimental.pallas import tpu_sc as plsc

D_MODEL = 1024
DIFF_HEADS, DIFF_HD = 4, 32
DIL_HEADS, DIL_HD = 4, 64
DIL_PATTERNS = ((128, 1), (512, 4), (2048, 16))
MLA_HEADS, MLA_Q_LORA, MLA_KV_LORA, MLA_NOPE, MLA_ROPE, MLA_V = 4, 384, 128, 64, 32, 64
ROPE_THETA = 10000.0
SWA_HEADS, SWA_KV_HEADS, SWA_HD, SWA_WINDOW = 4, 2, 64, 128
D_FF = 2816
N_EXPERTS, TOP_K, D_FF_EXPERT = 8, 2, 3584
RMS_EPS = 1e-6
NEG_INF = -1e30
N_ALIBI = DIFF_HEADS + DIL_HEADS + SWA_HEADS

LOG2E = 1.4426950408889634
LANES = 128
HEAD_LANES = 64
VMEM_LIMIT = 56 * 1024 * 1024

F32 = jnp.float32
BF16 = jnp.bfloat16

_NT = (((1,), (1,)), ((), ()))


def _params(sem, vmem=VMEM_LIMIT):
    return pltpu.CompilerParams(dimension_semantics=sem, vmem_limit_bytes=vmem)


def _alibi_slopes():
    s = 2.0 ** (-8.0 * (np.arange(N_ALIBI) + 1) / N_ALIBI)
    return s[0::3], s[1::3], s[2::3]


def _pick(n, pref):
    t = min(pref, n)
    while n % t:
        t //= 2
    return t


N_SCALED = 2048
C_CQ, C_CKV, C_KPE, C_KPE_SW, N_MAIN = 2048, 2432, 2560, 2688, 2816


ONES_ROWS = 16
VT_ROWS = HEAD_LANES + ONES_ROWS


def _vt_with_ones(v):
    vt = v.T
    ones = jnp.ones((ONES_ROWS, v.shape[0]), F32)
    parts = []
    for hd in range(v.shape[1] // HEAD_LANES):
        parts += [vt[hd * HEAD_LANES:(hd + 1) * HEAD_LANES], ones]
    return jnp.concatenate(parts, axis=0).astype(BF16)


def _proj_body(x_ref, g_ref, w_ref, cs_ref, qn_ref, wq_ref, kn_ref, wkv_ref,
               cosq_ref, sinq_ref, cosk_ref, sink_ref,
               a_ref, avt_ref, b_ref, bvt_ref, d_ref, dvt_ref, cq_ref, ck_ref, cvt_ref):
    x = x_ref[...]
    ms = jnp.mean(x * x, axis=-1, keepdims=True)
    h = (x * lax.rsqrt(ms + RMS_EPS) * g_ref[...]).astype(BF16)
    acc = jnp.dot(h, w_ref[...], preferred_element_type=F32)
    sc = acc[:, :N_SCALED] * cs_ref[...]
    a_ref[...] = sc[:, 0:512].astype(BF16)
    avt_ref[...] = _vt_with_ones(sc[:, 512:768])
    b_ref[...] = sc[:, 768:1280].astype(BF16)
    bvt_ref[...] = _vt_with_ones(sc[:, 1280:1536])
    d_ref[...] = sc[:, 1536:1920].astype(BF16)
    dvt_ref[...] = _vt_with_ones(sc[:, 1920:2048])

    cq = acc[:, C_CQ:C_CKV]
    hq = (cq * lax.rsqrt(jnp.mean(cq * cq, axis=-1, keepdims=True) + RMS_EPS) * qn_ref[...]).astype(BF16)
    yq = jnp.dot(hq, wq_ref[...], preferred_element_type=F32)
    cosq, sinq = cosq_ref[...], sinq_ref[...]
    c_mla = (MLA_NOPE + MLA_ROPE) ** -0.5 * LOG2E
    for hd in range(MLA_HEADS):
        main = yq[:, hd * 256:(hd + 1) * 256]
        sw = yq[:, 1024 + hd * LANES:1024 + (hd + 1) * LANES]
        cq_ref[:, hd * 256:hd * 256 + LANES] = (main[:, :LANES] * cosq + sw * sinq).astype(BF16)
        cq_ref[:, hd * 256 + LANES:(hd + 1) * 256] = (main[:, LANES:] * c_mla).astype(BF16)

    ckv = acc[:, C_CKV:C_KPE]
    hk = (ckv * lax.rsqrt(jnp.mean(ckv * ckv, axis=-1, keepdims=True) + RMS_EPS) * kn_ref[...]).astype(BF16)
    kv = jnp.dot(hk, wkv_ref[...], preferred_element_type=F32)
    kr = (acc[:, C_KPE:C_KPE_SW] * cosk_ref[...] + acc[:, C_KPE_SW:N_MAIN] * sink_ref[...]).astype(BF16)
    for grp in range(2):
        ck_ref[:, grp * 256:grp * 256 + LANES] = kr
        ck_ref[:, grp * 256 + LANES:(grp + 1) * 256] = kv[:, grp * LANES:(grp + 1) * LANES].astype(BF16)
    cvt_ref[...] = _vt_with_ones(kv[:, 256:512])


def _proj(x2d, g, w_main, colscale, qn, wq_big, kn, wkv, tabs, seq):
    t = x2d.shape[0]
    tm = _pick(seq, 512)
    nsb = seq // tm
    full = lambda shape: pl.BlockSpec(shape, lambda i: (0,) * len(shape))
    tab = pl.BlockSpec((tm, LANES), lambda i: (i % nsb, 0))
    row = lambda w: pl.BlockSpec((tm, w), lambda i: (i, 0))
    vt = lambda w: pl.BlockSpec((None, w, tm), lambda i: (i // nsb, 0, i % nsb))
    row_sds = lambda w: jax.ShapeDtypeStruct((t, w), BF16)
    vt_sds = lambda w: jax.ShapeDtypeStruct((t // seq, w, seq), BF16)
    return pl.pallas_call(
        _proj_body,
        grid=(t // tm,),
        in_specs=[row(D_MODEL), full((1, D_MODEL)), full((D_MODEL, N_MAIN)), full((1, N_SCALED)),
                  full((1, MLA_Q_LORA)), full((MLA_Q_LORA, 1536)), full((1, MLA_KV_LORA)),
                  full((MLA_KV_LORA, 512)), tab, tab, tab, tab],
        out_specs=[row(512), vt(4 * VT_ROWS), row(512), vt(4 * VT_ROWS), row(384), vt(2 * VT_ROWS),
                   row(1024), row(512), vt(4 * VT_ROWS)],
        out_shape=[row_sds(512), vt_sds(4 * VT_ROWS), row_sds(512), vt_sds(4 * VT_ROWS), row_sds(384),
                   vt_sds(2 * VT_ROWS), row_sds(1024), row_sds(512), vt_sds(4 * VT_ROWS)],
        compiler_params=_params(("parallel",)),
        name="proj",
    )(x2d, g, w_main, colscale, qn, wq_big, kn, wkv, *tabs)


def _attn_pipeline(n_soft, n_steps, scores, shift, load_vt, m0, s_ref, acc_ref):
    def stage_a(t, ss, dst_ref, m_cur):
        m_next, alpha = [], []
        for i in range(n_soft):
            m_new = jnp.maximum(m_cur[i], jnp.max(ss[i], axis=0, keepdims=True) + shift(i, t))
            dst_ref[i] = ss[i]
            m_next.append(m_new)
            alpha.append(jnp.exp2(m_cur[i] - m_new))
        return tuple(m_next), tuple(alpha)

    def stage_b(t, src_ref, carry):
        m_cur, alpha = carry
        for i in range(n_soft):
            p = jnp.exp2(src_ref[i] - (m_cur[i] - shift(i, t)))
            acc_ref[i] = alpha[i] * acc_ref[i] + jnp.dot(load_vt(t, i), p.astype(BF16),
                                                         preferred_element_type=F32)

    def half(t, src_ref, dst_ref, carry):
        ss = [scores(i, t + 1, False) for i in range(n_soft)]
        stage_b(t, src_ref, carry)
        return stage_a(t + 1, ss, dst_ref, carry[0])

    def body(u, carry):
        carry = half(2 * u, s_ref[0], s_ref[1], carry)
        return half(2 * u + 1, s_ref[1], s_ref[0], carry)

    carry = stage_a(0, [scores(i, 0, True) for i in range(n_soft)], s_ref[0], m0)
    carry = lax.fori_loop(0, n_steps // 2, body, carry)

    @pl.when(n_steps % 2 == 1)
    def _():
        stage_b(n_steps, s_ref[1], half(n_steps - 1, s_ref[0], s_ref[1], carry))

    @pl.when(n_steps % 2 == 0)
    def _():
        stage_b(n_steps, s_ref[0], carry)


def _row_mask(shape, lo, hi):
    row = lax.broadcasted_iota(jnp.int32, shape, 0)
    return jnp.logical_and(row >= lo, row < hi)


def _lane_mask(shape, lo, hi):
    lane = lax.broadcasted_iota(jnp.int32, shape, len(shape) - 1)
    return jnp.logical_and(lane >= lo, lane < hi)


def _normalized(acc_ref, i):
    return acc_ref[i, 0:HEAD_LANES, :] / acc_ref[i, HEAD_LANES:HEAD_LANES + 1, :]


def _diff_body(sl_ref, q_ref, k_ref, vt_ref, mask_ref, kb_ref, lam_ref, sub_ref, o_ref,
               qs_ref, s0_ref, s1_ref, acc_ref, *, tq, lam_init):
    qi = pl.program_id(1)
    tk = tq
    n_soft = 2 * DIFF_HEADS
    head_of = lambda i: i // 2
    for pair in range(2):
        q = q_ref[:, pair * LANES:(pair + 1) * LANES].astype(F32)
        for hd in range(2):
            sel = jnp.where(_lane_mask(q.shape, N_BIAS_SPLIT * hd, N_BIAS_SPLIT * (hd + 1)), 1.0, 0.0).astype(BF16)
            for mp in range(2):
                i = 4 * pair + 2 * hd + mp
                lo = hd * HEAD_LANES + mp * DIFF_HD
                qs_ref[i, :, 0:LANES] = jnp.where(_lane_mask(q.shape, lo, lo + DIFF_HD), q, 0.0).astype(BF16)
                qs_ref[i, :, LANES:2 * LANES] = sel
    acc_ref[...] = jnp.zeros(acc_ref.shape, F32)

    def scores(i, t, first):
        off = pl.multiple_of((qi - t) * tk, tk)
        pair = i // 4
        k = jnp.concatenate([k_ref[pl.ds(off, tk), pair * LANES:(pair + 1) * LANES], kb_ref[pair]], axis=1)
        s = lax.dot_general(k, qs_ref[i], _NT, preferred_element_type=F32)
        return s + mask_ref[...] if first else s

    def shift(i, t):
        return sl_ref[head_of(i)] * (-(t * tk)).astype(F32) if not isinstance(t, int) else 0.0

    def load_vt(t, i):
        return vt_ref[pl.ds(head_of(i) * VT_ROWS, VT_ROWS), pl.ds(pl.multiple_of((qi - t) * tk, tk), tk)]

    m0 = tuple(jnp.full((1, tq), NEG_INF, F32) for _ in range(n_soft))
    _attn_pipeline(n_soft, qi, scores, shift, load_vt, m0, (s0_ref, s1_ref), acc_ref)

    lp = lam_ref[...]
    lam = (jnp.exp(jnp.sum(lp[0:1, :] * lp[1:2, :], axis=-1, keepdims=True))
           - jnp.exp(jnp.sum(lp[2:3, :] * lp[3:4, :], axis=-1, keepdims=True)) + lam_init)
    in_h0 = _lane_mask((tq, LANES), 0, HEAD_LANES)
    for pair in range(2):
        o_h = [_normalized(acc_ref, 4 * pair + 2 * hd) - lam * _normalized(acc_ref, 4 * pair + 2 * hd + 1)
               for hd in range(2)]
        o = jnp.concatenate(o_h, axis=0).T
        sq = o * o
        ms0 = jnp.sum(jnp.where(in_h0, sq, 0.0), axis=-1, keepdims=True)
        ms1 = jnp.sum(jnp.where(in_h0, 0.0, sq), axis=-1, keepdims=True)
        ms = jnp.where(in_h0, ms0, ms1) * (1.0 / HEAD_LANES)
        y = o * lax.rsqrt(ms + RMS_EPS) * sub_ref[...]
        o_ref[:, pair * LANES:(pair + 1) * LANES] = (y * (1.0 - lam_init)).astype(o_ref.dtype)


N_BIAS_SPLIT = 3


def _alibi_key_columns(slopes_l2, tk):
    dj = jnp.arange(tk, dtype=F32)[None, :, None]
    val = jnp.asarray(slopes_l2, F32).reshape(-1, 1, 2) * dj
    cols = jnp.stack(_split_bf16(val, N_BIAS_SPLIT), axis=-1).reshape(val.shape[0], tk, 2 * N_BIAS_SPLIT)
    return jnp.pad(cols, ((0, 0), (0, 0), (0, LANES - 2 * N_BIAS_SPLIT)))


def _split_bf16(x, n):
    pieces = []
    for _ in range(n):
        bits = lax.bitcast_convert_type(x, jnp.uint32) & jnp.uint32(0xFFFF0000)
        head = lax.bitcast_convert_type(bits, F32)
        pieces.append(head.astype(BF16))
        x = x - head
    return pieces


def _diff_attention(a3, avt, slopes_l2, mask_t, kb, lam_p, sub128, layer_idx, tq):
    b, s, _ = a3.shape
    nq = s // tq
    lam_init = 0.8 - 0.6 * math.exp(-0.3 * layer_idx)
    body = functools.partial(_diff_body, tq=tq, lam_init=lam_init)
    grid_spec = pltpu.PrefetchScalarGridSpec(
        num_scalar_prefetch=1,
        grid=(b, nq),
        in_specs=[
            pl.BlockSpec((None, tq, 2 * LANES), lambda bi, qi, sl: (bi, qi, 0)),
            pl.BlockSpec((None, s, 2 * LANES), lambda bi, qi, sl: (bi, 0, 1)),
            pl.BlockSpec((None, DIFF_HEADS * VT_ROWS, s), lambda bi, qi, sl: (bi, 0, 0)),
            pl.BlockSpec((tq, tq), lambda bi, qi, sl: (0, 0), pipeline_mode=pl.Buffered(1)),
            pl.BlockSpec((2, tq, LANES), lambda bi, qi, sl: (0, 0, 0), pipeline_mode=pl.Buffered(1)),
            pl.BlockSpec((4, DIFF_HD), lambda bi, qi, sl: (0, 0)),
            pl.BlockSpec((1, LANES), lambda bi, qi, sl: (0, 0)),
        ],
        out_specs=pl.BlockSpec((None, tq, 2 * LANES), lambda bi, qi, sl: (bi, qi, 0)),
        scratch_shapes=[pltpu.VMEM((2 * DIFF_HEADS, tq, 2 * LANES), BF16),
                        pltpu.VMEM((2 * DIFF_HEADS, tq, tq), F32), pltpu.VMEM((2 * DIFF_HEADS, tq, tq), F32),
                        pltpu.VMEM((2 * DIFF_HEADS, VT_ROWS, tq), F32)],
    )
    return pl.pallas_call(
        body, grid_spec=grid_spec,
        out_shape=jax.ShapeDtypeStruct((b, s, 256), BF16),
        compiler_params=_params(("parallel", "arbitrary")),
        name="diff_attn",
    )(slopes_l2, a3, a3, avt, mask_t, kb, lam_p, sub128)


def _mla_body(q_ref, k_ref, vt_ref, mask_ref, o_ref, s0_ref, s1_ref, acc_ref, *, tq):
    qi = pl.program_id(1)
    tk = tq
    acc_ref[...] = jnp.zeros(acc_ref.shape, F32)

    def scores(i, t, first):
        k = k_ref[pl.ds(pl.multiple_of((qi - t) * tk, tk), tk), (i // 2) * 256:(i // 2 + 1) * 256]
        s = lax.dot_general(k, q_ref[:, i * 256:(i + 1) * 256], _NT, preferred_element_type=F32)
        return s + mask_ref[...] if first else s

    def load_vt(t, i):
        return vt_ref[pl.ds(i * VT_ROWS, VT_ROWS), pl.ds(pl.multiple_of((qi - t) * tk, tk), tk)]

    m0 = tuple(jnp.full((1, tq), NEG_INF, F32) for _ in range(MLA_HEADS))
    _attn_pipeline(MLA_HEADS, qi, scores, lambda i, t: 0.0, load_vt, m0, (s0_ref, s1_ref), acc_ref)
    o_t = jnp.concatenate([_normalized(acc_ref, i) for i in range(MLA_HEADS)], axis=0)
    o_ref[...] = o_t.T.astype(o_ref.dtype)


def _mla_attention(cq3, ck3, cvt, mask_t, tq):
    b, s, _ = cq3.shape
    nq = s // tq
    return pl.pallas_call(
        functools.partial(_mla_body, tq=tq),
        grid=(b, nq),
        in_specs=[
            pl.BlockSpec((None, tq, MLA_HEADS * 256), lambda bi, qi: (bi, qi, 0)),
            pl.BlockSpec((None, s, 512), lambda bi, qi: (bi, 0, 0)),
            pl.BlockSpec((None, MLA_HEADS * VT_ROWS, s), lambda bi, qi: (bi, 0, 0)),
            pl.BlockSpec((tq, tq), lambda bi, qi: (0, 0)),
        ],
        out_specs=pl.BlockSpec((None, tq, 256), lambda bi, qi: (bi, qi, 0)),
        out_shape=jax.ShapeDtypeStruct((b, s, 256), BF16),
        scratch_shapes=[pltpu.VMEM((MLA_HEADS, tq, tq), F32), pltpu.VMEM((MLA_HEADS, tq, tq), F32),
                        pltpu.VMEM((MLA_HEADS, VT_ROWS, tq), F32)],
        compiler_params=_params(("parallel", "arbitrary")),
        name="mla_attn",
    )(cq3, ck3, cvt, mask_t)


def _band_body(m0_ref, q_ref, k_ref, vt_ref, bias_ref, o_ref, qs_ref, s0_ref, s1_ref, acc_ref,
               *, tq, n_off, l_init, kv_slabs):
    qi = pl.program_id(1)
    tk = tq
    n_soft = 4
    for sl in range(2):
        q = q_ref[:, sl * LANES:(sl + 1) * LANES].astype(F32)
        for g in range(2):
            qs_ref[2 * sl + g] = jnp.where(_lane_mask(q.shape, g * HEAD_LANES, (g + 1) * HEAD_LANES), q, 0.0
                                           ).astype(BF16)
    for i in range(n_soft):
        acc_ref[i] = jnp.where(_row_mask((VT_ROWS, tq), 0, HEAD_LANES), 0.0, l_init).astype(F32)

    kv_of = lambda i: (i // 2) * (kv_slabs - 1)

    def scores(i, t, first):
        k = k_ref[pl.ds(pl.multiple_of((qi - t) * tk, tk), tk), kv_of(i) * LANES:(kv_of(i) + 1) * LANES]
        return lax.dot_general(k, qs_ref[i], _NT, preferred_element_type=F32) + bias_ref[i // 2, i % 2, t]

    def load_vt(t, i):
        return vt_ref[pl.ds((2 * kv_of(i) + i % 2) * VT_ROWS, VT_ROWS),
                      pl.ds(pl.multiple_of((qi - t) * tk, tk), tk)]

    m0 = tuple(jnp.full((1, tq), m0_ref[i], F32) for i in range(n_soft))
    _attn_pipeline(n_soft, jnp.minimum(qi, n_off - 1), scores, lambda i, t: 0.0, load_vt, m0,
                   (s0_ref, s1_ref), acc_ref)
    o_t = jnp.concatenate([_normalized(acc_ref, i) for i in range(n_soft)], axis=0)
    o_ref[...] = o_t.T.astype(o_ref.dtype)


def _band_attention(x3, xvt, k_blk, kv_slabs, bias, m0, l_init, tq, name):
    b, s, _ = x3.shape
    nq = s // tq
    n_off = bias.shape[2]
    n_soft = 4
    grid_spec = pltpu.PrefetchScalarGridSpec(
        num_scalar_prefetch=1,
        grid=(b, nq),
        in_specs=[
            pl.BlockSpec((None, tq, 2 * LANES), lambda bi, qi, m0r: (bi, qi, 0)),
            pl.BlockSpec((None, s, kv_slabs * LANES), lambda bi, qi, m0r: (bi, 0, k_blk)),
            pl.BlockSpec((None, kv_slabs * 2 * VT_ROWS, s), lambda bi, qi, m0r: (bi, 0, 0)),
            pl.BlockSpec((2, 2, n_off, tq, tq), lambda bi, qi, m0r: (0, 0, 0, 0, 0),
                         pipeline_mode=pl.Buffered(1)),
        ],
        out_specs=pl.BlockSpec((None, tq, 2 * LANES), lambda bi, qi, m0r: (bi, qi, 0)),
        scratch_shapes=[pltpu.VMEM((n_soft, tq, LANES), BF16), pltpu.VMEM((n_soft, tq, tq), F32),
                        pltpu.VMEM((n_soft, tq, tq), F32), pltpu.VMEM((n_soft, VT_ROWS, tq), F32)],
    )
    return pl.pallas_call(
        functools.partial(_band_body, tq=tq, n_off=n_off, l_init=l_init, kv_slabs=kv_slabs),
        grid_spec=grid_spec,
        out_shape=jax.ShapeDtypeStruct((b, s, 256), BF16),
        compiler_params=_params(("parallel", "arbitrary")),
        name=name,
    )(m0, x3, x3, xvt, bias)


def _band_bias(slopes, tq, n_off, mult_fn):
    di = jnp.arange(tq, dtype=jnp.int32)[None, :]
    dj = jnp.arange(tq, dtype=jnp.int32)[:, None]
    off = jnp.arange(n_off, dtype=jnp.int32)[:, None, None]
    delta = off * tq + di - dj
    mult = mult_fn(delta)
    sl = jnp.asarray(slopes, F32)[:, None, None, None] * LOG2E
    val = -sl * delta.astype(F32)[None] + jnp.log2(jnp.maximum(mult, 1).astype(F32))[None]
    return jnp.where((mult > 0)[None], val, NEG_INF)


def _dil_mult(delta):
    m = jnp.zeros(delta.shape, jnp.int32)
    for window, d in DIL_PATTERNS:
        m = m + ((delta >= 0) & (delta <= window) & (delta % d == 0)).astype(jnp.int32)
    return m


def _swa_mult(delta):
    return ((delta >= 0) & (delta <= SWA_WINDOW - 1)).astype(jnp.int32)


def _outproj_body(*refs, with_router):
    if with_router:
        (oa, ob, oc, od, w_ref, x_ref, g_ref, rh_ref, rl_ref, x1_ref, h2_ref, lg_ref) = refs
    else:
        (oa, ob, oc, od, w_ref, x_ref, g_ref, x1_ref, h2_ref) = refs
    acc = x_ref[...]
    for i, o in enumerate((oa, ob, oc, od)):
        acc = acc + jnp.dot(o[...], w_ref[i], preferred_element_type=F32)
    x1_ref[...] = acc
    h2 = acc * lax.rsqrt(jnp.mean(acc * acc, axis=-1, keepdims=True) + RMS_EPS) * g_ref[...]
    hi = h2.astype(BF16)
    if with_router:
        for c in range(SC_CHUNKS):
            h2_ref[c] = h2[:, c * SC_COLS:(c + 1) * SC_COLS]
        lo = (h2 - hi.astype(F32)).astype(BF16)
        lg_ref[...] = (jnp.dot(hi, rh_ref[...], preferred_element_type=F32)
                       + (jnp.dot(hi, rl_ref[...], preferred_element_type=F32)
                          + jnp.dot(lo, rh_ref[...], preferred_element_type=F32)))
    else:
        h2_ref[...] = hi


def _outproj(oa, ob, oc, od, w4, x2d, g, router_hl=None):
    t = x2d.shape[0]
    tm = _pick(t, 512)
    with_router = router_hl is not None
    full = lambda shape: pl.BlockSpec(shape, lambda i: (0,) * len(shape))
    row = lambda w: pl.BlockSpec((tm, w), lambda i: (i, 0))
    in_specs = [row(256)] * 4 + [full((4, 256, D_MODEL)), row(D_MODEL), full((1, D_MODEL))]
    args = [oa, ob, oc, od, w4, x2d, g]
    out_specs = [row(D_MODEL), row(D_MODEL)]
    out_shape = [jax.ShapeDtypeStruct((t, D_MODEL), F32), jax.ShapeDtypeStruct((t, D_MODEL), BF16)]
    if with_router:
        out_specs[1] = pl.BlockSpec((SC_CHUNKS, tm, SC_COLS), lambda i: (0, i, 0))
        out_shape[1] = jax.ShapeDtypeStruct((SC_CHUNKS, t, SC_COLS), F32)
        in_specs += [full((D_MODEL, LANES))] * 2
        args += list(router_hl)
        out_specs.append(row(LANES))
        out_shape.append(jax.ShapeDtypeStruct((t, LANES), F32))
    return pl.pallas_call(
        functools.partial(_outproj_body, with_router=with_router),
        grid=(t // tm,), in_specs=in_specs, out_specs=out_specs, out_shape=out_shape,
        compiler_params=_params(("parallel",)),
        name="outproj",
    )(*args)


FF_CHUNK = 512


def _swiglu(x, w1_ref, w3_ref, w2_ref):
    tf = w1_ref.shape[1]
    total = None
    for lo in range(0, tf, FF_CHUNK):
        hi = min(lo + FF_CHUNK, tf)
        a = jnp.dot(x, w1_ref[:, lo:hi], preferred_element_type=F32)
        b = jnp.dot(x, w3_ref[:, lo:hi], preferred_element_type=F32)
        mid = (a * jax.nn.sigmoid(a) * b).astype(BF16)
        part = jnp.dot(mid, w2_ref[lo:hi, :], preferred_element_type=F32)
        total = part if total is None else total + part
    return total


def _ffn_body(*refs, n_cast):
    h_ref, w1_ref, w3_ref, w2_ref, x_ref = refs[:5]
    cast_in, o_ref, cast_out = refs[5:5 + n_cast], refs[5 + n_cast], refs[6 + n_cast:]
    o_ref[...] = x_ref[...] + _swiglu(h_ref[...], w1_ref, w3_ref, w2_ref)
    for src, dst in zip(cast_in, cast_out):
        dst[...] = src[...].astype(BF16)


def _ffn(h2, w1, w3, w2, x1, to_cast=()):
    t = h2.shape[0]
    tm = _pick(t, 512)
    n_steps = t // tm
    resident = lambda shape: pl.BlockSpec(shape, lambda i: (0, 0), pipeline_mode=pl.Buffered(1))
    sliced = [pl.BlockSpec((a.shape[0] // n_steps, a.shape[1]), lambda i: (i, 0)) for a in to_cast]
    outs = pl.pallas_call(
        functools.partial(_ffn_body, n_cast=len(to_cast)),
        grid=(n_steps,),
        in_specs=[pl.BlockSpec((tm, D_MODEL), lambda i: (i, 0)),
                  resident((D_MODEL, D_FF)), resident((D_MODEL, D_FF)), resident((D_FF, D_MODEL)),
                  pl.BlockSpec((tm, D_MODEL), lambda i: (i, 0))] + sliced,
        out_specs=[pl.BlockSpec((tm, D_MODEL), lambda i: (i, 0))] + sliced,
        out_shape=[jax.ShapeDtypeStruct((t, D_MODEL), F32)]
        + [jax.ShapeDtypeStruct(a.shape, BF16) for a in to_cast],
        compiler_params=_params(("parallel",)),
        name="ffn",
    )(h2, w1, w3, w2, x1, *to_cast)
    return outs[0], outs[1:]


def _route_body(lg_ref, tri_ref, info_ref, cnt_ref, carry_ref):
    i = pl.program_id(0)

    @pl.when(i == 0)
    def _():
        carry_ref[...] = jnp.zeros(carry_ref.shape, F32)

    lg = lg_ref[...]
    lane = lax.broadcasted_iota(jnp.int32, lg.shape, 1)
    lg = jnp.where(lane < N_EXPERTS, lg, -jnp.inf)
    m1 = jnp.max(lg, axis=-1, keepdims=True)
    i1 = jnp.min(jnp.where(lg == m1, lane, LANES), axis=-1, keepdims=True)
    oh1 = lane == i1
    lg2 = jnp.where(oh1, -jnp.inf, lg)
    m2 = jnp.max(lg2, axis=-1, keepdims=True)
    i2 = jnp.min(jnp.where(lg2 == m2, lane, LANES), axis=-1, keepdims=True)
    oh2 = lane == i2
    e2 = jnp.exp(m2 - m1)
    g1 = 1.0 / (1.0 + e2)
    g2 = e2 / (1.0 + e2)
    oh = jnp.where(jnp.logical_or(oh1, oh2), 1.0, 0.0)
    rank = jnp.dot(tri_ref[...], oh.astype(BF16), preferred_element_type=F32) + carry_ref[...]
    r1 = jnp.sum(jnp.where(oh1, rank, 0.0), axis=-1, keepdims=True)
    r2 = jnp.sum(jnp.where(oh2, rank, 0.0), axis=-1, keepdims=True)
    carry_ref[...] += jnp.sum(oh, axis=0, keepdims=True)
    cnt_ref[...] = carry_ref[...]
    info = jnp.zeros(lg.shape, F32)
    for c, val in enumerate((i1.astype(F32), i2.astype(F32), r1, r2, g1, g2)):
        info = jnp.where(lane == c, val, info)
    info_ref[...] = info


def _route(logits):
    t = logits.shape[0]
    tm = _pick(t, 512)
    tri = (jnp.arange(tm)[:, None] > jnp.arange(tm)[None, :]).astype(BF16)
    return pl.pallas_call(
        _route_body,
        grid=(t // tm,),
        in_specs=[pl.BlockSpec((tm, LANES), lambda i: (i, 0)), pl.BlockSpec((tm, tm), lambda i: (0, 0))],
        out_specs=[pl.BlockSpec((tm, LANES), lambda i: (i, 0)), pl.BlockSpec((1, LANES), lambda i: (0, 0))],
        out_shape=[jax.ShapeDtypeStruct((t, LANES), F32), jax.ShapeDtypeStruct((1, LANES), F32)],
        scratch_shapes=[pltpu.VMEM((1, LANES), F32)],
        compiler_params=_params(("arbitrary",)),
        name="route",
    )(logits, tri)


def _moe_ffn_body(be_ref, bv_ref, x_ref, w1_ref, w3_ref, w2_ref, o_ref):
    valid = bv_ref[pl.program_id(0)]

    @pl.when(valid > 0)
    def _():
        x = jnp.concatenate([x_ref[c] for c in range(SC_CHUNKS)], axis=1)
        x = jnp.where(lax.broadcasted_iota(jnp.int32, x.shape, 0) < valid, x, 0.0).astype(BF16)
        y = _swiglu(x, w1_ref, w3_ref, w2_ref)
        for c in range(SC_CHUNKS):
            o_ref[c] = y[:, c * SC_COLS:(c + 1) * SC_COLS]

    @pl.when(valid == 0)
    def _():
        o_ref[...] = jnp.zeros(o_ref.shape, o_ref.dtype)


def _moe_ffn(xs, blk_expert, blk_valid, w1, w3, w2, bm):
    rows = xs.shape[1]
    chunked = pl.BlockSpec((SC_CHUNKS, bm, SC_COLS), lambda i, be, bv: (0, i, 0))
    expert = lambda shape: pl.BlockSpec((None,) + shape, lambda i, be, bv: (be[i], 0, 0),
                                        pipeline_mode=pl.Buffered(1))
    grid_spec = pltpu.PrefetchScalarGridSpec(
        num_scalar_prefetch=2,
        grid=(rows // bm,),
        in_specs=[chunked, expert((D_MODEL, D_FF_EXPERT)), expert((D_MODEL, D_FF_EXPERT)),
                  expert((D_FF_EXPERT, D_MODEL))],
        out_specs=chunked,
    )
    return pl.pallas_call(
        _moe_ffn_body, grid_spec=grid_spec,
        out_shape=jax.ShapeDtypeStruct(xs.shape, F32),
        compiler_params=_params(("arbitrary",)),
        name="moe_ffn",
    )(blk_expert, blk_valid, xs, w1, w3, w2)


def _final_body(*refs, with_moe):
    if with_moe:
        x_ref, y0_ref, y1_ref, info_ref, g_ref, o_ref = refs
        info = info_ref[...]
        y0 = jnp.concatenate([y0_ref[c] for c in range(SC_CHUNKS)], axis=1)
        y1 = jnp.concatenate([y1_ref[c] for c in range(SC_CHUNKS)], axis=1)
        x = x_ref[...] + (info[:, 4:5] * y0 + info[:, 5:6] * y1)
    else:
        x_ref, g_ref, o_ref = refs
        x = x_ref[...]
    o_ref[...] = x * lax.rsqrt(jnp.mean(x * x, axis=-1, keepdims=True) + RMS_EPS) * g_ref[...]


def _final(x2d, g, moe=None):
    t = x2d.shape[0]
    tm = _pick(t, 512)
    row = lambda w: pl.BlockSpec((tm, w), lambda i: (i, 0))
    in_specs, args = [row(D_MODEL)], [x2d]
    if moe is not None:
        yg, info = moe
        slot = lambda k: pl.BlockSpec((None, SC_CHUNKS, tm, SC_COLS), lambda i: (k, 0, i, 0))
        in_specs += [slot(0), slot(1), row(LANES)]
        args += [yg, yg, info]
    in_specs.append(pl.BlockSpec((1, D_MODEL), lambda i: (0, 0)))
    args.append(g)
    return pl.pallas_call(
        functools.partial(_final_body, with_moe=moe is not None),
        grid=(t // tm,), in_specs=in_specs, out_specs=row(D_MODEL),
        out_shape=jax.ShapeDtypeStruct((t, D_MODEL), F32),
        compiler_params=_params(("parallel",)),
        name="final",
    )(*args)


def _prep_attn_weights(w_in, w_out, w_uq, w_ukv):
    widths = (256, 256, 256, 256, 256, 256, MLA_Q_LORA, MLA_KV_LORA, MLA_ROPE, 256, 128, 128)
    offs = np.concatenate([[0], np.cumsum(widths)])
    qa, ka, va, qb, kb, vb, cq, ckv, kpe, qd, kd, vd = [w_in[:, offs[i]:offs[i + 1]] for i in range(12)]
    hperm = np.array([0, 2, 1, 3])
    qd = qd.reshape(D_MODEL, 4, SWA_HD)[:, hperm].reshape(D_MODEL, 256)
    pad = jnp.zeros((D_MODEL, LANES - MLA_ROPE), w_in.dtype)
    half = MLA_ROPE // 2
    kpe_sw = jnp.concatenate([kpe[:, half:], kpe[:, :half]], axis=1)
    w_main = jnp.concatenate([qa, ka, va, qb, kb, vb, qd, kd, vd, cq, ckv, kpe, pad, kpe_sw, pad], axis=1)

    uq = w_uq.reshape(MLA_Q_LORA, MLA_HEADS, MLA_NOPE + MLA_ROPE)
    z = lambda n: jnp.zeros((MLA_Q_LORA, n), w_uq.dtype)
    mains, sws = [], []
    for hd in range(MLA_HEADS):
        nope, rope = uq[:, hd, :MLA_NOPE], uq[:, hd, MLA_NOPE:]
        nope128 = jnp.concatenate([nope, z(64)] if hd % 2 == 0 else [z(64), nope], axis=1)
        mains.append(jnp.concatenate([rope, z(LANES - MLA_ROPE), nope128], axis=1))
        sws.append(jnp.concatenate([rope[:, half:], rope[:, :half], z(LANES - MLA_ROPE)], axis=1))
    wq_big = jnp.concatenate(mains + sws, axis=1)

    ukv = w_ukv.reshape(MLA_KV_LORA, MLA_HEADS, MLA_NOPE + MLA_V)
    wkv = jnp.concatenate([ukv[:, :, :MLA_NOPE].reshape(MLA_KV_LORA, 256),
                           ukv[:, :, MLA_NOPE:].reshape(MLA_KV_LORA, 256)], axis=1)

    wo = w_out.reshape(4, 256, D_MODEL)
    wo_d = wo[3].reshape(4, SWA_HD, D_MODEL)[hperm].reshape(256, D_MODEL)
    w4 = jnp.stack([wo[0], wo[1], wo[2], wo_d])
    return w_main.astype(BF16), wq_big.astype(BF16), wkv.astype(BF16), w4.astype(BF16)


def _rope_tables(s):
    inv = ROPE_THETA ** (-jnp.arange(0, MLA_ROPE, 2, dtype=F32) / MLA_ROPE)
    ang = jnp.arange(s, dtype=F32)[:, None] * inv[None, :]
    cos, sin = jnp.cos(ang), jnp.sin(ang)
    z = jnp.zeros((s, LANES - MLA_ROPE), F32)
    cos128 = jnp.concatenate([cos, cos, z], axis=1)
    sin128 = jnp.concatenate([-sin, sin, z], axis=1)
    c_mla = (MLA_NOPE + MLA_ROPE) ** -0.5 * LOG2E
    return cos128 * c_mla, sin128 * c_mla, cos128, sin128


def _col_scale():
    ca = DIFF_HD ** -0.5 * LOG2E
    cb = DIL_HD ** -0.5 * LOG2E
    cd = SWA_HD ** -0.5 * LOG2E
    v = np.ones((1, N_SCALED), np.float32)
    v[0, 0:256] = ca
    v[0, 768:1024] = cb
    v[0, 1536:1792] = cd
    return jnp.asarray(v)


def _moe_layer(h2, logits, w1, w3, w2, bm):
    t = logits.shape[0]
    info, counts = _route(logits)
    cnt = counts[0, :N_EXPERTS].astype(jnp.int32)
    padded = (cnt + bm - 1) // bm * bm
    pends = jnp.cumsum(padded)
    pstarts = pends - padded
    e12 = info[:, 0:2].astype(jnp.int32)
    pos = pstarts[e12] + info[:, 2:4].astype(jnp.int32)
    nb = (t * TOP_K) // bm + N_EXPERTS
    rows = nb * bm
    blk_start = jnp.arange(nb, dtype=jnp.int32) * bm
    blk_expert = jnp.minimum(jnp.sum((pends[None, :] <= blk_start[:, None]).astype(jnp.int32), axis=1),
                             N_EXPERTS - 1)
    blk_valid = jnp.clip(pstarts[blk_expert] + cnt[blk_expert] - blk_start, 0, bm).astype(jnp.int32)
    sub = (jnp.arange(SC_CHUNKS, dtype=jnp.int32) * rows)[None, :, None] + pos.T[:, None, :]
    xs = _sc_scatter_rows(h2.reshape(SC_CHUNKS * t, SC_COLS),
                          [sub[k].reshape(1, SC_CHUNKS * t) for k in range(TOP_K)], SC_CHUNKS * rows)
    ys = _moe_ffn(xs.reshape(SC_CHUNKS, rows, SC_COLS), blk_expert.astype(jnp.int32), blk_valid, w1, w3, w2, bm)
    yg = _sc_gather_rows(ys.reshape(SC_CHUNKS * rows, SC_COLS), sub.reshape(1, TOP_K * SC_CHUNKS * t))
    return yg.reshape(TOP_K, SC_CHUNKS, t, SC_COLS), info


SC_WINDOW = 128
SC_COLS = 256
SC_CHUNKS = D_MODEL // SC_COLS


def _sc_mesh():
    return plsc.VectorSubcoreMesh(core_axis_name="c", subcore_axis_name="s")


def _sc_scatter_rows(x, idx_list, rows):
    t, d = x.shape
    n = len(idx_list)

    @functools.partial(pl.kernel, out_type=jax.ShapeDtypeStruct((rows, d), x.dtype), mesh=_sc_mesh(),
                       scratch_types=[])
    def scatter_kernel(x_hbm, *rest):
        idx_hbms, o_hbm = rest[:n], rest[n]

        def body(x_vmem, *idx_vmems):
            for iv in idx_vmems:
                pltpu.sync_copy(x_vmem, o_hbm.at[iv.at[0]])

        pltpu.emit_pipeline(
            body, grid=(t // SC_WINDOW,),
            in_specs=[pl.BlockSpec((SC_WINDOW, d), lambda i: (i, 0))]
            + [pl.BlockSpec((1, SC_WINDOW), lambda i: (0, i))] * n,
            out_specs=[], core_axis_name=("c", "s"), dimension_semantics=(pltpu.PARALLEL,),
        )(x_hbm, *idx_hbms)

    return scatter_kernel(x, *idx_list)


def _sc_gather_rows(y, idx):
    m = idx.shape[1]
    d = y.shape[1]

    @functools.partial(pl.kernel, out_type=jax.ShapeDtypeStruct((m, d), y.dtype), mesh=_sc_mesh(),
                       scratch_types=[])
    def gather_kernel(y_hbm, idx_hbm, o_hbm):
        def body(idx_vmem, o_vmem):
            pltpu.sync_copy(y_hbm.at[idx_vmem.at[0]], o_vmem)

        pltpu.emit_pipeline(
            body, grid=(m // SC_WINDOW,),
            in_specs=[pl.BlockSpec((1, SC_WINDOW), lambda i: (0, i))],
            out_specs=[pl.BlockSpec((SC_WINDOW, d), lambda i: (i, 0))],
            core_axis_name=("c", "s"), dimension_semantics=(pltpu.PARALLEL,),
        )(idx_hbm, o_hbm)

    return gather_kernel(y, idx)


def kernel(x, attn_norm, w_in, w_out, diff_lambda, diff_subln, mla_q_norm, mla_w_uq, mla_kv_norm, mla_w_ukv,
           swa_sinks, ffn_norm, ffn_w1, ffn_w3, ffn_w2, moe_router, moe_w1, moe_w3, moe_w2, final_norm):
    bsz, seq, _ = x.shape
    t = bsz * seq
    depth = w_in.shape[0]
    slopes_a, slopes_b, slopes_d = _alibi_slopes()
    tq_dense = _pick(seq, 512)
    tq_band = _pick(seq, 256)

    tabs = _rope_tables(seq)
    colscale = _col_scale()
    di = jnp.arange(tq_dense)
    causal_mask_t = jnp.where(di[:, None] <= di[None, :], 0.0, NEG_INF).astype(F32)
    tq_diff = tq_dense
    alibi_kb = _alibi_key_columns(slopes_a * LOG2E, tq_diff)
    tq_dil = _pick(seq, 512)
    n_off_dil = min((max(w for w, _ in DIL_PATTERNS) + tq_dil - 1) // tq_dil + 1, seq // tq_dil)
    n_off_swa = min((SWA_WINDOW - 1 + tq_band - 1) // tq_band + 1, seq // tq_band)
    dil_bias = _band_bias(slopes_b, tq_dil, n_off_dil, _dil_mult).reshape(2, 2, n_off_dil, tq_dil, tq_dil)
    swa_bias = _band_bias(slopes_d[np.array([0, 2, 1, 3])], tq_band, n_off_swa, _swa_mult
                          ).reshape(2, 2, n_off_swa, tq_band, tq_band)
    slopes_a_l2 = jnp.asarray(slopes_a * LOG2E, F32)
    dil_m0 = jnp.full((4,), NEG_INF, F32)

    x2d = x.reshape(t, D_MODEL)
    moe_bf16 = []
    for l in range(depth):
        w_main, wq_big, wkv, w4 = _prep_attn_weights(w_in[l], w_out[l], mla_w_uq[l], mla_w_ukv[l])
        a, avt, b, bvt, d, dvt, cq, ck, cvt = _proj(x2d, attn_norm[l][None], w_main, colscale, mla_q_norm[l][None],
                                                    wq_big, mla_kv_norm[l][None], wkv, tabs, seq)
        a3, b3, d3 = (v.reshape(bsz, seq, -1) for v in (a, b, d))
        sub128 = jnp.concatenate([diff_subln[l], diff_subln[l]])[None]
        o_a = _diff_attention(a3, avt, slopes_a_l2, causal_mask_t[:tq_diff, :tq_diff], alibi_kb, diff_lambda[l],
                              sub128, l, tq_diff)
        o_b = _band_attention(b3, bvt, 1, 2, dil_bias, dil_m0, 0.0, tq_dil, "dil_attn")
        o_c = _mla_attention(cq.reshape(bsz, seq, -1), ck.reshape(bsz, seq, -1), cvt, causal_mask_t, tq_dense)
        swa_m0 = (swa_sinks[l].astype(F32) * LOG2E)[np.array([0, 2, 1, 3])]
        o_d = _band_attention(d3, dvt, 2, 1, swa_bias, swa_m0, 1.0, tq_band, "swa_attn")
        outs = [v.reshape(t, 256) for v in (o_a, o_b, o_c, o_d)]
        j = l // 2
        if l % 2 == 0:
            x1, h2 = _outproj(*outs, w4, x2d, ffn_norm[l][None])
            nxt = [w[j].reshape(-1, w.shape[-1]) for w in (moe_w1, moe_w3, moe_w2)] if l + 1 < depth else []
            if any(a.shape[0] % (16 * (t // _pick(t, 512))) for a in nxt):
                nxt = []
            x2d, cast = _ffn(h2, ffn_w1[j].astype(BF16), ffn_w3[j].astype(BF16), ffn_w2[j].astype(BF16), x1, nxt)
            moe_bf16 = [c.reshape(moe_w.shape[1:]) for c, moe_w in zip(cast, (moe_w1, moe_w3, moe_w2))]
            moe = None
        else:
            r = jnp.pad(moe_router[j], ((0, 0), (0, LANES - N_EXPERTS)))
            r_hi, r_lo = _split_bf16(r, 2)
            x1, h2, logits = _outproj(*outs, w4, x2d, ffn_norm[l][None], (r_hi, r_lo))
            bm = _pick(t, 512)
            if not moe_bf16:
                moe_bf16 = [w[j].astype(BF16) for w in (moe_w1, moe_w3, moe_w2)]
            moe = _moe_layer(h2, logits, *moe_bf16, bm)
            x2d = x1
        if l < depth - 1 and moe is not None:
            raise NotImplementedError("MoE combine is fused into the final norm; MoE layer must be last")
    return _final(x2d, final_norm[None], moe).reshape(bsz, seq, D_MODEL)
```

```python
import functools
import math

import numpy as np
import jax
import jax.numpy as jnp
from jax import lax
from jax.experimental import pallas as pl
from jax.experimental.pallas import tpu as pltpu
from jax.experimental.pallas import tpu_sc as plsc

D_MODEL = 1024
DIFF_HEADS, DIFF_HD = 4, 32
DIL_HEADS, DIL_HD = 4, 64
DIL_PATTERNS = ((128, 1), (512, 4), (2048, 16))
MLA_HEADS, MLA_Q_LORA, MLA_KV_LORA, MLA_NOPE, MLA_ROPE, MLA_V = 4, 384, 128, 64, 32, 64
ROPE_THETA = 10000.0
SWA_HEADS, SWA_KV_HEADS, SWA_HD, SWA_WINDOW = 4, 2, 64, 128
D_FF = 2816
N_EXPERTS, TOP_K, D_FF_EXPERT = 8, 2, 3584
RMS_EPS = 1e-6
NEG_INF = -1e30
N_ALIBI = DIFF_HEADS + DIL_HEADS + SWA_HEADS

LOG2E = 1.4426950408889634
LANES = 128
HEAD_LANES = 64
VMEM_LIMIT = 56 * 1024 * 1024

F32 = jnp.float32
BF16 = jnp.bfloat16

_NT = (((1,), (1,)), ((), ()))


def _params(sem, vmem=VMEM_LIMIT):
    return pltpu.CompilerParams(dimension_semantics=sem, vmem_limit_bytes=vmem)


def _alibi_slopes():
    s = 2.0 ** (-8.0 * (np.arange(N_ALIBI) + 1) / N_ALIBI)
    return s[0::3], s[1::3], s[2::3]


def _pick(n, pref):
    t = min(pref, n)
    while n % t:
        t //= 2
    return t


N_SCALED = 2048
C_CQ, C_CKV, C_KPE, C_KPE_SW, N_MAIN = 2048, 2432, 2560, 2688, 2816


ONES_ROWS = 16
VT_ROWS = HEAD_LANES + ONES_ROWS


def _vt_with_ones(v):
    vt = v.T
    ones = jnp.ones((ONES_ROWS, v.shape[0]), F32)
    parts = []
    for hd in range(v.shape[1] // HEAD_LANES):
        parts += [vt[hd * HEAD_LANES:(hd + 1) * HEAD_LANES], ones]
    return jnp.concatenate(parts, axis=0).astype(BF16)


def _proj_body(x_ref, g_ref, w_ref, cs_ref, qn_ref, wq_ref, kn_ref, wkv_ref,
               cosq_ref, sinq_ref, cosk_ref, sink_ref,
               a_ref, avt_ref, b_ref, bvt_ref, d_ref, dvt_ref, cq_ref, ck_ref, cvt_ref):
    x = x_ref[...]
    ms = jnp.mean(x * x, axis=-1, keepdims=True)
    h = (x * lax.rsqrt(ms + RMS_EPS) * g_ref[...]).astype(BF16)
    acc = jnp.dot(h, w_ref[...], preferred_element_type=F32)
    sc = acc[:, :N_SCALED] * cs_ref[...]
    a_ref[...] = sc[:, 0:512].astype(BF16)
    avt_ref[...] = _vt_with_ones(sc[:, 512:768])
    b_ref[...] = sc[:, 768:1280].astype(BF16)
    bvt_ref[...] = _vt_with_ones(sc[:, 1280:1536])
    d_ref[...] = sc[:, 1536:1920].astype(BF16)
    dvt_ref[...] = _vt_with_ones(sc[:, 1920:2048])

    cq = acc[:, C_CQ:C_CKV]
    hq = (cq * lax.rsqrt(jnp.mean(cq * cq, axis=-1, keepdims=True) + RMS_EPS) * qn_ref[...]).astype(BF16)
    yq = jnp.dot(hq, wq_ref[...], preferred_element_type=F32)
    cosq, sinq = cosq_ref[...], sinq_ref[...]
    c_mla = (MLA_NOPE + MLA_ROPE) ** -0.5 * LOG2E
    for hd in range(MLA_HEADS):
        main = yq[:, hd * 256:(hd + 1) * 256]
        sw = yq[:, 1024 + hd * LANES:1024 + (hd + 1) * LANES]
        cq_ref[:, hd * 256:hd * 256 + LANES] = (main[:, :LANES] * cosq + sw * sinq).astype(BF16)
        cq_ref[:, hd * 256 + LANES:(hd + 1) * 256] = (main[:, LANES:] * c_mla).astype(BF16)

    ckv = acc[:, C_CKV:C_KPE]
    hk = (ckv * lax.rsqrt(jnp.mean(ckv * ckv, axis=-1, keepdims=True) + RMS_EPS) * kn_ref[...]).astype(BF16)
    kv = jnp.dot(hk, wkv_ref[...], preferred_element_type=F32)
    kr = (acc[:, C_KPE:C_KPE_SW] * cosk_ref[...] + acc[:, C_KPE_SW:N_MAIN] * sink_ref[...]).astype(BF16)
    for grp in range(2):
        ck_ref[:, grp * 256:grp * 256 + LANES] = kr
        ck_ref[:, grp * 256 + LANES:(grp + 1) * 256] = kv[:, grp * LANES:(grp + 1) * LANES].astype(BF16)
    cvt_ref[...] = _vt_with_ones(kv[:, 256:512])


def _proj(x2d, g, w_main, colscale, qn, wq_big, kn, wkv, tabs, seq):
    t = x2d.shape[0]
    tm = _pick(seq, 512)
    nsb = seq // tm
    full = lambda shape: pl.BlockSpec(shape, lambda i: (0,) * len(shape))
    tab = pl.BlockSpec((tm, LANES), lambda i: (i % nsb, 0))
    row = lambda w: pl.BlockSpec((tm, w), lambda i: (i, 0))
    vt = lambda w: pl.BlockSpec((None, w, tm), lambda i: (i // nsb, 0, i % nsb))
    row_sds = lambda w: jax.ShapeDtypeStruct((t, w), BF16)
    vt_sds = lambda w: jax.ShapeDtypeStruct((t // seq, w, seq), BF16)
    return pl.pallas_call(
        _proj_body,
        grid=(t // tm,),
        in_specs=[row(D_MODEL), full((1, D_MODEL)), full((D_MODEL, N_MAIN)), full((1, N_SCALED)),
                  full((1, MLA_Q_LORA)), full((MLA_Q_LORA, 1536)), full((1, MLA_KV_LORA)),
                  full((MLA_KV_LORA, 512)), tab, tab, tab, tab],
        out_specs=[row(512), vt(4 * VT_ROWS), row(512), vt(4 * VT_ROWS), row(384), vt(2 * VT_ROWS),
                   row(1024), row(512), vt(4 * VT_ROWS)],
        out_shape=[row_sds(512), vt_sds(4 * VT_ROWS), row_sds(512), vt_sds(4 * VT_ROWS), row_sds(384),
                   vt_sds(2 * VT_ROWS), row_sds(1024), row_sds(512), vt_sds(4 * VT_ROWS)],
        compiler_params=_params(("parallel",)),
        name="proj",
    )(x2d, g, w_main, colscale, qn, wq_big, kn, wkv, *tabs)


def _attn_pipeline(n_soft, n_steps, scores, shift, load_vt, m0, s_ref, acc_ref):
    def stage_a(t, ss, dst_ref, m_cur):
        m_next, alpha = [], []
        for i in range(n_soft):
            m_new = jnp.maximum(m_cur[i], jnp.max(ss[i], axis=0, keepdims=True) + shift(i, t))
            dst_ref[i] = ss[i]
            m_next.append(m_new)
            alpha.append(jnp.exp2(m_cur[i] - m_new))
        return tuple(m_next), tuple(alpha)

    def stage_b(t, src_ref, carry):
        m_cur, alpha = carry
        for i in range(n_soft):
            p = jnp.exp2(src_ref[i] - (m_cur[i] - shift(i, t)))
            acc_ref[i] = alpha[i] * acc_ref[i] + jnp.dot(load_vt(t, i), p.astype(BF16),
                                                         preferred_element_type=F32)

    def half(t, src_ref, dst_ref, carry):
        ss = [scores(i, t + 1, False) for i in range(n_soft)]
        stage_b(t, src_ref, carry)
        return stage_a(t + 1, ss, dst_ref, carry[0])

    def body(u, carry):
        carry = half(2 * u, s_ref[0], s_ref[1], carry)
        return half(2 * u + 1, s_ref[1], s_ref[0], carry)

    carry = stage_a(0, [scores(i, 0, True) for i in range(n_soft)], s_ref[0], m0)
    carry = lax.fori_loop(0, n_steps // 2, body, carry)

    @pl.when(n_steps % 2 == 1)
    def _():
        stage_b(n_steps, s_ref[1], half(n_steps - 1, s_ref[0], s_ref[1], carry))

    @pl.when(n_steps % 2 == 0)
    def _():
        stage_b(n_steps, s_ref[0], carry)


def _row_mask(shape, lo, hi):
    row = lax.broadcasted_iota(jnp.int32, shape, 0)
    return jnp.logical_and(row >= lo, row < hi)


def _lane_mask(shape, lo, hi):
    lane = lax.broadcasted_iota(jnp.int32, shape, len(shape) - 1)
    return jnp.logical_and(lane >= lo, lane < hi)


def _normalized(acc_ref, i):
    return acc_ref[i, 0:HEAD_LANES, :] / acc_ref[i, HEAD_LANES:HEAD_LANES + 1, :]


def _diff_body(sl_ref, q_ref, k_ref, vt_ref, mask_ref, kb_ref, lam_ref, sub_ref, o_ref,
               qs_ref, s0_ref, s1_ref, acc_ref, *, tq, lam_init):
    qi = pl.program_id(1)
    tk = tq
    n_soft = 2 * DIFF_HEADS
    head_of = lambda i: i // 2
    for pair in range(2):
        q = q_ref[:, pair * LANES:(pair + 1) * LANES].astype(F32)
        for hd in range(2):
            sel = jnp.where(_lane_mask(q.shape, N_BIAS_SPLIT * hd, N_BIAS_SPLIT * (hd + 1)), 1.0, 0.0).astype(BF16)
            for mp in range(2):
                i = 4 * pair + 2 * hd + mp
                lo = hd * HEAD_LANES + mp * DIFF_HD
                qs_ref[i, :, 0:LANES] = jnp.where(_lane_mask(q.shape, lo, lo + DIFF_HD), q, 0.0).astype(BF16)
                qs_ref[i, :, LANES:2 * LANES] = sel
    acc_ref[...] = jnp.zeros(acc_ref.shape, F32)

    def scores(i, t, first):
        off = pl.multiple_of((qi - t) * tk, tk)
        pair = i // 4
        k = jnp.concatenate([k_ref[pl.ds(off, tk), pair * LANES:(pair + 1) * LANES], kb_ref[pair]], axis=1)
        s = lax.dot_general(k, qs_ref[i], _NT, preferred_element_type=F32)
        return s + mask_ref[...] if first else s

    def shift(i, t):
        return sl_ref[head_of(i)] * (-(t * tk)).astype(F32) if not isinstance(t, int) else 0.0

    def load_vt(t, i):
        return vt_ref[pl.ds(head_of(i) * VT_ROWS, VT_ROWS), pl.ds(pl.multiple_of((qi - t) * tk, tk), tk)]

    m0 = tuple(jnp.full((1, tq), NEG_INF, F32) for _ in range(n_soft))
    _attn_pipeline(n_soft, qi, scores, shift, load_vt, m0, (s0_ref, s1_ref), acc_ref)

    lp = lam_ref[...]
    lam = (jnp.exp(jnp.sum(lp[0:1, :] * lp[1:2, :], axis=-1, keepdims=True))
           - jnp.exp(jnp.sum(lp[2:3, :] * lp[3:4, :], axis=-1, keepdims=True)) + lam_init)
    in_h0 = _lane_mask((tq, LANES), 0, HEAD_LANES)
    for pair in range(2):
        o_h = [_normalized(acc_ref, 4 * pair + 2 * hd) - lam * _normalized(acc_ref, 4 * pair + 2 * hd + 1)
               for hd in range(2)]
        o = jnp.concatenate(o_h, axis=0).T
        sq = o * o
        ms0 = jnp.sum(jnp.where(in_h0, sq, 0.0), axis=-1, keepdims=True)
        ms1 = jnp.sum(jnp.where(in_h0, 0.0, sq), axis=-1, keepdims=True)
        ms = jnp.where(in_h0, ms0, ms1) * (1.0 / HEAD_LANES)
        y = o * lax.rsqrt(ms + RMS_EPS) * sub_ref[...]
        o_ref[:, pair * LANES:(pair + 1) * LANES] = (y * (1.0 - lam_init)).astype(o_ref.dtype)


N_BIAS_SPLIT = 3


def _alibi_key_columns(slopes_l2, tk):
    dj = jnp.arange(tk, dtype=F32)[None, :, None]
    val = jnp.asarray(slopes_l2, F32).reshape(-1, 1, 2) * dj
    cols = jnp.stack(_split_bf16(val, N_BIAS_SPLIT), axis=-1).reshape(val.shape[0], tk, 2 * N_BIAS_SPLIT)
    return jnp.pad(cols, ((0, 0), (0, 0), (0, LANES - 2 * N_BIAS_SPLIT)))


def _split_bf16(x, n):
    pieces = []
    for _ in range(n):
        bits = lax.bitcast_convert_type(x, jnp.uint32) & jnp.uint32(0xFFFF0000)
        head = lax.bitcast_convert_type(bits, F32)
        pieces.append(head.astype(BF16))
        x = x - head
    return pieces


def _diff_attention(a3, avt, slopes_l2, mask_t, kb, lam_p, sub128, layer_idx, tq):
    b, s, _ = a3.shape
    nq = s // tq
    lam_init = 0.8 - 0.6 * math.exp(-0.3 * layer_idx)
    body = functools.partial(_diff_body, tq=tq, lam_init=lam_init)
    grid_spec = pltpu.PrefetchScalarGridSpec(
        num_scalar_prefetch=1,
        grid=(b, nq),
        in_specs=[
            pl.BlockSpec((None, tq, 2 * LANES), lambda bi, qi, sl: (bi, qi, 0)),
            pl.BlockSpec((None, s, 2 * LANES), lambda bi, qi, sl: (bi, 0, 1)),
            pl.BlockSpec((None, DIFF_HEADS * VT_ROWS, s), lambda bi, qi, sl: (bi, 0, 0)),
            pl.BlockSpec((tq, tq), lambda bi, qi, sl: (0, 0), pipeline_mode=pl.Buffered(1)),
            pl.BlockSpec((2, tq, LANES), lambda bi, qi, sl: (0, 0, 0), pipeline_mode=pl.Buffered(1)),
            pl.BlockSpec((4, DIFF_HD), lambda bi, qi, sl: (0, 0)),
            pl.BlockSpec((1, LANES), lambda bi, qi, sl: (0, 0)),
        ],
        out_specs=pl.BlockSpec((None, tq, 2 * LANES), lambda bi, qi, sl: (bi, qi, 0)),
        scratch_shapes=[pltpu.VMEM((2 * DIFF_HEADS, tq, 2 * LANES), BF16),
                        pltpu.VMEM((2 * DIFF_HEADS, tq, tq), F32), pltpu.VMEM((2 * DIFF_HEADS, tq, tq), F32),
                        pltpu.VMEM((2 * DIFF_HEADS, VT_ROWS, tq), F32)],
    )
    return pl.pallas_call(
        body, grid_spec=grid_spec,
        out_shape=jax.ShapeDtypeStruct((b, s, 256), BF16),
        compiler_params=_params(("parallel", "arbitrary")),
        name="diff_attn",
    )(slopes_l2, a3, a3, avt, mask_t, kb, lam_p, sub128)


def _mla_body(q_ref, k_ref, vt_ref, mask_ref, o_ref, s0_ref, s1_ref, acc_ref, *, tq):
    qi = pl.program_id(1)
    tk = tq
    acc_ref[...] = jnp.zeros(acc_ref.shape, F32)

    def scores(i, t, first):
        k = k_ref[pl.ds(pl.multiple_of((qi - t) * tk, tk), tk), (i // 2) * 256:(i // 2 + 1) * 256]
        s = lax.dot_general(k, q_ref[:, i * 256:(i + 1) * 256], _NT, preferred_element_type=F32)
        return s + mask_ref[...] if first else s

    def load_vt(t, i):
        return vt_ref[pl.ds(i * VT_ROWS, VT_ROWS), pl.ds(pl.multiple_of((qi - t) * tk, tk), tk)]

    m0 = tuple(jnp.full((1, tq), NEG_INF, F32) for _ in range(MLA_HEADS))
    _attn_pipeline(MLA_HEADS, qi, scores, lambda i, t: 0.0, load_vt, m0, (s0_ref, s1_ref), acc_ref)
    o_t = jnp.concatenate([_normalized(acc_ref, i) for i in range(MLA_HEADS)], axis=0)
    o_ref[...] = o_t.T.astype(o_ref.dtype)


def _mla_attention(cq3, ck3, cvt, mask_t, tq):
    b, s, _ = cq3.shape
    nq = s // tq
    return pl.pallas_call(
        functools.partial(_mla_body, tq=tq),
        grid=(b, nq),
        in_specs=[
            pl.BlockSpec((None, tq, MLA_HEADS * 256), lambda bi, qi: (bi, qi, 0)),
            pl.BlockSpec((None, s, 512), lambda bi, qi: (bi, 0, 0)),
            pl.BlockSpec((None, MLA_HEADS * VT_ROWS, s), lambda bi, qi: (bi, 0, 0)),
            pl.BlockSpec((tq, tq), lambda bi, qi: (0, 0)),
        ],
        out_specs=pl.BlockSpec((None, tq, 256), lambda bi, qi: (bi, qi, 0)),
        out_shape=jax.ShapeDtypeStruct((b, s, 256), BF16),
        scratch_shapes=[pltpu.VMEM((MLA_HEADS, tq, tq), F32), pltpu.VMEM((MLA_HEADS, tq, tq), F32),
                        pltpu.VMEM((MLA_HEADS, VT_ROWS, tq), F32)],
        compiler_params=_params(("parallel", "arbitrary")),
        name="mla_attn",
    )(cq3, ck3, cvt, mask_t)


def _band_body(m0_ref, q_ref, k_ref, vt_ref, bias_ref, o_ref, qs_ref, s0_ref, s1_ref, acc_ref,
               *, tq, n_off, l_init, kv_slabs):
    qi = pl.program_id(1)
    tk = tq
    n_soft = 4
    for sl in range(2):
        q = q_ref[:, sl * LANES:(sl + 1) * LANES].astype(F32)
        for g in range(2):
            qs_ref[2 * sl + g] = jnp.where(_lane_mask(q.shape, g * HEAD_LANES, (g + 1) * HEAD_LANES), q, 0.0
                                           ).astype(BF16)
    for i in range(n_soft):
        acc_ref[i] = jnp.where(_row_mask((VT_ROWS, tq), 0, HEAD_LANES), 0.0, l_init).astype(F32)

    kv_of = lambda i: (i // 2) * (kv_slabs - 1)

    def scores(i, t, first):
        k = k_ref[pl.ds(pl.multiple_of((qi - t) * tk, tk), tk), kv_of(i) * LANES:(kv_of(i) + 1) * LANES]
        return lax.dot_general(k, qs_ref[i], _NT, preferred_element_type=F32) + bias_ref[i // 2, i % 2, t]

    def load_vt(t, i):
        return vt_ref[pl.ds((2 * kv_of(i) + i % 2) * VT_ROWS, VT_ROWS),
                      pl.ds(pl.multiple_of((qi - t) * tk, tk), tk)]

    m0 = tuple(jnp.full((1, tq), m0_ref[i], F32) for i in range(n_soft))
    _attn_pipeline(n_soft, jnp.minimum(qi, n_off - 1), scores, lambda i, t: 0.0, load_vt, m0,
                   (s0_ref, s1_ref), acc_ref)
    o_t = jnp.concatenate([_normalized(acc_ref, i) for i in range(n_soft)], axis=0)
    o_ref[...] = o_t.T.astype(o_ref.dtype)


def _band_attention(x3, xvt, k_blk, kv_slabs, bias, m0, l_init, tq, name):
    b, s, _ = x3.shape
    nq = s // tq
    n_off = bias.shape[2]
    n_soft = 4
    grid_spec = pltpu.PrefetchScalarGridSpec(
        num_scalar_prefetch=1,
        grid=(b, nq),
        in_specs=[
            pl.BlockSpec((None, tq, 2 * LANES), lambda bi, qi, m0r: (bi, qi, 0)),
            pl.BlockSpec((None, s, kv_slabs * LANES), lambda bi, qi, m0r: (bi, 0, k_blk)),
            pl.BlockSpec((None, kv_slabs * 2 * VT_ROWS, s), lambda bi, qi, m0r: (bi, 0, 0)),
            pl.BlockSpec((2, 2, n_off, tq, tq), lambda bi, qi, m0r: (0, 0, 0, 0, 0),
                         pipeline_mode=pl.Buffered(1)),
        ],
        out_specs=pl.BlockSpec((None, tq, 2 * LANES), lambda bi, qi, m0r: (bi, qi, 0)),
        scratch_shapes=[pltpu.VMEM((n_soft, tq, LANES), BF16), pltpu.VMEM((n_soft, tq, tq), F32),
                        pltpu.VMEM((n_soft, tq, tq), F32), pltpu.VMEM((n_soft, VT_ROWS, tq), F32)],
    )
    return pl.pallas_call(
        functools.partial(_band_body, tq=tq, n_off=n_off, l_init=l_init, kv_slabs=kv_slabs),
        grid_spec=grid_spec,
        out_shape=jax.ShapeDtypeStruct((b, s, 256), BF16),
        compiler_params=_params(("parallel", "arbitrary")),
        name=name,
    )(m0, x3, x3, xvt, bias)


def _band_bias(slopes, tq, n_off, mult_fn):
    di = jnp.arange(tq, dtype=jnp.int32)[None, :]
    dj = jnp.arange(tq, dtype=jnp.int32)[:, None]
    off = jnp.arange(n_off, dtype=jnp.int32)[:, None, None]
    delta = off * tq + di - dj
    mult = mult_fn(delta)
    sl = jnp.asarray(slopes, F32)[:, None, None, None] * LOG2E
    val = -sl * delta.astype(F32)[None] + jnp.log2(jnp.maximum(mult, 1).astype(F32))[None]
    return jnp.where((mult > 0)[None], val, NEG_INF)


def _dil_mult(delta):
    m = jnp.zeros(delta.shape, jnp.int32)
    for window, d in DIL_PATTERNS:
        m = m + ((delta >= 0) & (delta <= window) & (delta % d == 0)).astype(jnp.int32)
    return m


def _swa_mult(delta):
    return ((delta >= 0) & (delta <= SWA_WINDOW - 1)).astype(jnp.int32)


def _outproj_body(*refs, with_router):
    if with_router:
        (oa, ob, oc, od, w_ref, x_ref, g_ref, rh_ref, rl_ref, x1_ref, h2_ref, lg_ref) = refs
    else:
        (oa, ob, oc, od, w_ref, x_ref, g_ref, x1_ref, h2_ref) = refs
    acc = x_ref[...]
    for i, o in enumerate((oa, ob, oc, od)):
        acc = acc + jnp.dot(o[...], w_ref[i], preferred_element_type=F32)
    x1_ref[...] = acc
    h2 = acc * lax.rsqrt(jnp.mean(acc * acc, axis=-1, keepdims=True) + RMS_EPS) * g_ref[...]
    hi = h2.astype(BF16)
    if with_router:
        for c in range(SC_CHUNKS):
            h2_ref[c] = h2[:, c * SC_COLS:(c + 1) * SC_COLS]
        lo = (h2 - hi.astype(F32)).astype(BF16)
        lg_ref[...] = (jnp.dot(hi, rh_ref[...], preferred_element_type=F32)
                       + (jnp.dot(hi, rl_ref[...], preferred_element_type=F32)
                          + jnp.dot(lo, rh_ref[...], preferred_element_type=F32)))
    else:
        h2_ref[...] = hi


def _outproj(oa, ob, oc, od, w4, x2d, g, router_hl=None):
    t = x2d.shape[0]
    tm = _pick(t, 512)
    with_router = router_hl is not None
    full = lambda shape: pl.BlockSpec(shape, lambda i: (0,) * len(shape))
    row = lambda w: pl.BlockSpec((tm, w), lambda i: (i, 0))
    in_specs = [row(256)] * 4 + [full((4, 256, D_MODEL)), row(D_MODEL), full((1, D_MODEL))]
    args = [oa, ob, oc, od, w4, x2d, g]
    out_specs = [row(D_MODEL), row(D_MODEL)]
    out_shape = [jax.ShapeDtypeStruct((t, D_MODEL), F32), jax.ShapeDtypeStruct((t, D_MODEL), BF16)]
    if with_router:
        out_specs[1] = pl.BlockSpec((SC_CHUNKS, tm, SC_COLS), lambda i: (0, i, 0))
        out_shape[1] = jax.ShapeDtypeStruct((SC_CHUNKS, t, SC_COLS), F32)
        in_specs += [full((D_MODEL, LANES))] * 2
        args += list(router_hl)
        out_specs.append(row(LANES))
        out_shape.append(jax.ShapeDtypeStruct((t, LANES), F32))
    return pl.pallas_call(
        functools.partial(_outproj_body, with_router=with_router),
        grid=(t // tm,), in_specs=in_specs, out_specs=out_specs, out_shape=out_shape,
        compiler_params=_params(("parallel",)),
        name="outproj",
    )(*args)


FF_CHUNK = 512


def _swiglu(x, w1_ref, w3_ref, w2_ref):
    tf = w1_ref.shape[1]
    total = None
    for lo in range(0, tf, FF_CHUNK):
        hi = min(lo + FF_CHUNK, tf)
        a = jnp.dot(x, w1_ref[:, lo:hi], preferred_element_type=F32)
        b = jnp.dot(x, w3_ref[:, lo:hi], preferred_element_type=F32)
        mid = (a * jax.nn.sigmoid(a) * b).astype(BF16)
        part = jnp.dot(mid, w2_ref[lo:hi, :], preferred_element_type=F32)
        total = part if total is None else total + part
    return total


def _ffn_body(*refs, n_cast):
    h_ref, w1_ref, w3_ref, w2_ref, x_ref = refs[:5]
    cast_in, o_ref, cast_out = refs[5:5 + n_cast], refs[5 + n_cast], refs[6 + n_cast:]
    o_ref[...] = x_ref[...] + _swiglu(h_ref[...], w1_ref, w3_ref, w2_ref)
    for src, dst in zip(cast_in, cast_out):
        dst[...] = src[...].astype(BF16)


def _ffn(h2, w1, w3, w2, x1, to_cast=()):
    t = h2.shape[0]
    tm = _pick(t, 512)
    n_steps = t // tm
    resident = lambda shape: pl.BlockSpec(shape, lambda i: (0, 0), pipeline_mode=pl.Buffered(1))
    sliced = [pl.BlockSpec((a.shape[0] // n_steps, a.shape[1]), lambda i: (i, 0)) for a in to_cast]
    outs = pl.pallas_call(
        functools.partial(_ffn_body, n_cast=len(to_cast)),
        grid=(n_steps,),
        in_specs=[pl.BlockSpec((tm, D_MODEL), lambda i: (i, 0)),
                  resident((D_MODEL, D_FF)), resident((D_MODEL, D_FF)), resident((D_FF, D_MODEL)),
                  pl.BlockSpec((tm, D_MODEL), lambda i: (i, 0))] + sliced,
        out_specs=[pl.BlockSpec((tm, D_MODEL), lambda i: (i, 0))] + sliced,
        out_shape=[jax.ShapeDtypeStruct((t, D_MODEL), F32)]
        + [jax.ShapeDtypeStruct(a.shape, BF16) for a in to_cast],
        compiler_params=_params(("parallel",)),
        name="ffn",
    )(h2, w1, w3, w2, x1, *to_cast)
    return outs[0], outs[1:]


def _route_body(lg_ref, tri_ref, info_ref, cnt_ref, carry_ref):
    i = pl.program_id(0)

    @pl.when(i == 0)
    def _():
        carry_ref[...] = jnp.zeros(carry_ref.shape, F32)

    lg = lg_ref[...]
    lane = lax.broadcasted_iota(jnp.int32, lg.shape, 1)
    lg = jnp.where(lane < N_EXPERTS, lg, -jnp.inf)
    m1 = jnp.max(lg, axis=-1, keepdims=True)
    i1 = jnp.min(jnp.where(lg == m1, lane, LANES), axis=-1, keepdims=True)
    oh1 = lane == i1
    lg2 = jnp.where(oh1, -jnp.inf, lg)
    m2 = jnp.max(lg2, axis=-1, keepdims=True)
    i2 = jnp.min(jnp.where(lg2 == m2, lane, LANES), axis=-1, keepdims=True)
    oh2 = lane == i2
    e2 = jnp.exp(m2 - m1)
    g1 = 1.0 / (1.0 + e2)
    g2 = e2 / (1.0 + e2)
    oh = jnp.where(jnp.logical_or(oh1, oh2), 1.0, 0.0)
    rank = jnp.dot(tri_ref[...], oh.astype(BF16), preferred_element_type=F32) + carry_ref[...]
    r1 = jnp.sum(jnp.where(oh1, rank, 0.0), axis=-1, keepdims=True)
    r2 = jnp.sum(jnp.where(oh2, rank, 0.0), axis=-1, keepdims=True)
    carry_ref[...] += jnp.sum(oh, axis=0, keepdims=True)
    cnt_ref[...] = carry_ref[...]
    info = jnp.zeros(lg.shape, F32)
    for c, val in enumerate((i1.astype(F32), i2.astype(F32), r1, r2, g1, g2)):
        info = jnp.where(lane == c, val, info)
    info_ref[...] = info


def _route(logits):
    t = logits.shape[0]
    tm = _pick(t, 512)
    tri = (jnp.arange(tm)[:, None] > jnp.arange(tm)[None, :]).astype(BF16)
    return pl.pallas_call(
        _route_body,
        grid=(t // tm,),
        in_specs=[pl.BlockSpec((tm, LANES), lambda i: (i, 0)), pl.BlockSpec((tm, tm), lambda i: (0, 0))],
        out_specs=[pl.BlockSpec((tm, LANES), lambda i: (i, 0)), pl.BlockSpec((1, LANES), lambda i: (0, 0))],
        out_shape=[jax.ShapeDtypeStruct((t, LANES), F32), jax.ShapeDtypeStruct((1, LANES), F32)],
        scratch_shapes=[pltpu.VMEM((1, LANES), F32)],
        compiler_params=_params(("arbitrary",)),
        name="route",
    )(logits, tri)


def _moe_ffn_body(be_ref, bv_ref, x_ref, w1_ref, w3_ref, w2_ref, o_ref):
    valid = bv_ref[pl.program_id(0)]

    @pl.when(valid > 0)
    def _():
        x = jnp.concatenate([x_ref[c] for c in range(SC_CHUNKS)], axis=1)
        x = jnp.where(lax.broadcasted_iota(jnp.int32, x.shape, 0) < valid, x, 0.0).astype(BF16)
        y = _swiglu(x, w1_ref, w3_ref, w2_ref)
        for c in range(SC_CHUNKS):
            o_ref[c] = y[:, c * SC_COLS:(c + 1) * SC_COLS]

    @pl.when(valid == 0)
    def _():
        o_ref[...] = jnp.zeros(o_ref.shape, o_ref.dtype)


def _moe_ffn(xs, blk_expert, blk_valid, w1, w3, w2, bm):
    rows = xs.shape[1]
    chunked = pl.BlockSpec((SC_CHUNKS, bm, SC_COLS), lambda i, be, bv: (0, i, 0))
    expert = lambda shape: pl.BlockSpec((None,) + shape, lambda i, be, bv: (be[i], 0, 0),
                                        pipeline_mode=pl.Buffered(1))
    grid_spec = pltpu.PrefetchScalarGridSpec(
        num_scalar_prefetch=2,
        grid=(rows // bm,),
        in_specs=[chunked, expert((D_MODEL, D_FF_EXPERT)), expert((D_MODEL, D_FF_EXPERT)),
                  expert((D_FF_EXPERT, D_MODEL))],
        out_specs=chunked,
    )
    return pl.pallas_call(
        _moe_ffn_body, grid_spec=grid_spec,
        out_shape=jax.ShapeDtypeStruct(xs.shape, F32),
        compiler_params=_params(("arbitrary",)),
        name="moe_ffn",
    )(blk_expert, blk_valid, xs, w1, w3, w2)


def _final_body(*refs, with_moe):
    if with_moe:
        x_ref, y0_ref, y1_ref, info_ref, g_ref, o_ref = refs
        info = info_ref[...]
        y0 = jnp.concatenate([y0_ref[c] for c in range(SC_CHUNKS)], axis=1)
        y1 = jnp.concatenate([y1_ref[c] for c in range(SC_CHUNKS)], axis=1)
        x = x_ref[...] + (info[:, 4:5] * y0 + info[:, 5:6] * y1)
    else:
        x_ref, g_ref, o_ref = refs
        x = x_ref[...]
    o_ref[...] = x * lax.rsqrt(jnp.mean(x * x, axis=-1, keepdims=True) + RMS_EPS) * g_ref[...]


def _final(x2d, g, moe=None):
    t = x2d.shape[0]
    tm = _pick(t, 512)
    row = lambda w: pl.BlockSpec((tm, w), lambda i: (i, 0))
    in_specs, args = [row(D_MODEL)], [x2d]
    if moe is not None:
        yg, info = moe
        slot = lambda k: pl.BlockSpec((None, SC_CHUNKS, tm, SC_COLS), lambda i: (k, 0, i, 0))
        in_specs += [slot(0), slot(1), row(LANES)]
        args += [yg, yg, info]
    in_specs.append(pl.BlockSpec((1, D_MODEL), lambda i: (0, 0)))
    args.append(g)
    return pl.pallas_call(
        functools.partial(_final_body, with_moe=moe is not None),
        grid=(t // tm,), in_specs=in_specs, out_specs=row(D_MODEL),
        out_shape=jax.ShapeDtypeStruct((t, D_MODEL), F32),
        compiler_params=_params(("parallel",)),
        name="final",
    )(*args)


def _prep_attn_weights(w_in, w_out, w_uq, w_ukv):
    widths = (256, 256, 256, 256, 256, 256, MLA_Q_LORA, MLA_KV_LORA, MLA_ROPE, 256, 128, 128)
    offs = np.concatenate([[0], np.cumsum(widths)])
    qa, ka, va, qb, kb, vb, cq, ckv, kpe, qd, kd, vd = [w_in[:, offs[i]:offs[i + 1]] for i in range(12)]
    hperm = np.array([0, 2, 1, 3])
    qd = qd.reshape(D_MODEL, 4, SWA_HD)[:, hperm].reshape(D_MODEL, 256)
    pad = jnp.zeros((D_MODEL, LANES - MLA_ROPE), w_in.dtype)
    half = MLA_ROPE // 2
    kpe_sw = jnp.concatenate([kpe[:, half:], kpe[:, :half]], axis=1)
    w_main = jnp.concatenate([qa, ka, va, qb, kb, vb, qd, kd, vd, cq, ckv, kpe, pad, kpe_sw, pad], axis=1)

    uq = w_uq.reshape(MLA_Q_LORA, MLA_HEADS, MLA_NOPE + MLA_ROPE)
    z = lambda n: jnp.zeros((MLA_Q_LORA, n), w_uq.dtype)
    mains, sws = [], []
    for hd in range(MLA_HEADS):
        nope, rope = uq[:, hd, :MLA_NOPE], uq[:, hd, MLA_NOPE:]
        nope128 = jnp.concatenate([nope, z(64)] if hd % 2 == 0 else [z(64), nope], axis=1)
        mains.append(jnp.concatenate([rope, z(LANES - MLA_ROPE), nope128], axis=1))
        sws.append(jnp.concatenate([rope[:, half:], rope[:, :half], z(LANES - MLA_ROPE)], axis=1))
    wq_big = jnp.concatenate(mains + sws, axis=1)

    ukv = w_ukv.reshape(MLA_KV_LORA, MLA_HEADS, MLA_NOPE + MLA_V)
    wkv = jnp.concatenate([ukv[:, :, :MLA_NOPE].reshape(MLA_KV_LORA, 256),
                           ukv[:, :, MLA_NOPE:].reshape(MLA_KV_LORA, 256)], axis=1)

    wo = w_out.reshape(4, 256, D_MODEL)
    wo_d = wo[3].reshape(4, SWA_HD, D_MODEL)[hperm].reshape(256, D_MODEL)
    w4 = jnp.stack([wo[0], wo[1], wo[2], wo_d])
    return w_main.astype(BF16), wq_big.astype(BF16), wkv.astype(BF16), w4.astype(BF16)


def _rope_tables(s):
    inv = ROPE_THETA ** (-jnp.arange(0, MLA_ROPE, 2, dtype=F32) / MLA_ROPE)
    ang = jnp.arange(s, dtype=F32)[:, None] * inv[None, :]
    cos, sin = jnp.cos(ang), jnp.sin(ang)
    z = jnp.zeros((s, LANES - MLA_ROPE), F32)
    cos128 = jnp.concatenate([cos, cos, z], axis=1)
    sin128 = jnp.concatenate([-sin, sin, z], axis=1)
    c_mla = (MLA_NOPE + MLA_ROPE) ** -0.5 * LOG2E
    return cos128 * c_mla, sin128 * c_mla, cos128, sin128


def _col_scale():
    ca = DIFF_HD ** -0.5 * LOG2E
    cb = DIL_HD ** -0.5 * LOG2E
    cd = SWA_HD ** -0.5 * LOG2E
    v = np.ones((1, N_SCALED), np.float32)
    v[0, 0:256] = ca
    v[0, 768:1024] = cb
    v[0, 1536:1792] = cd
    return jnp.asarray(v)


def _moe_layer(h2, logits, w1, w3, w2, bm):
    t = logits.shape[0]
    info, counts = _route(logits)
    cnt = counts[0, :N_EXPERTS].astype(jnp.int32)
    padded = (cnt + bm - 1) // bm * bm
    pends = jnp.cumsum(padded)
    pstarts = pends - padded
    e12 = info[:, 0:2].astype(jnp.int32)
    pos = pstarts[e12] + info[:, 2:4].astype(jnp.int32)
    nb = (t * TOP_K) // bm + N_EXPERTS
    rows = nb * bm
    blk_start = jnp.arange(nb, dtype=jnp.int32) * bm
    blk_expert = jnp.minimum(jnp.sum((pends[None, :] <= blk_start[:, None]).astype(jnp.int32), axis=1),
                             N_EXPERTS - 1)
    blk_valid = jnp.clip(pstarts[blk_expert] + cnt[blk_expert] - blk_start, 0, bm).astype(jnp.int32)
    sub = (jnp.arange(SC_CHUNKS, dtype=jnp.int32) * rows)[None, :, None] + pos.T[:, None, :]
    xs = _sc_scatter_rows(h2.reshape(SC_CHUNKS * t, SC_COLS),
                          [sub[k].reshape(1, SC_CHUNKS * t) for k in range(TOP_K)], SC_CHUNKS * rows)
    ys = _moe_ffn(xs.reshape(SC_CHUNKS, rows, SC_COLS), blk_expert.astype(jnp.int32), blk_valid, w1, w3, w2, bm)
    yg = _sc_gather_rows(ys.reshape(SC_CHUNKS * rows, SC_COLS), sub.reshape(1, TOP_K * SC_CHUNKS * t))
    return yg.reshape(TOP_K, SC_CHUNKS, t, SC_COLS), info


SC_WINDOW = 128
SC_COLS = 256
SC_CHUNKS = D_MODEL // SC_COLS


def _sc_mesh():
    return plsc.VectorSubcoreMesh(core_axis_name="c", subcore_axis_name="s")


def _sc_scatter_rows(x, idx_list, rows):
    t, d = x.shape
    n = len(idx_list)

    @functools.partial(pl.kernel, out_type=jax.ShapeDtypeStruct((rows, d), x.dtype), mesh=_sc_mesh(),
                       scratch_types=[pltpu.SemaphoreType.DMA] * n)
    def scatter_kernel(x_hbm, *rest):
        idx_hbms, o_hbm, sems = rest[:n], rest[n], rest[n + 1:]

        def body(x_vmem, *idx_vmems):
            copies = [pltpu.async_copy(x_vmem, o_hbm.at[iv.at[0]], sem) for iv, sem in zip(idx_vmems, sems)]
            for cp in copies:
                cp.wait()

        pltpu.emit_pipeline(
            body, grid=(t // SC_WINDOW,),
            in_specs=[pl.BlockSpec((SC_WINDOW, d), lambda i: (i, 0))]
            + [pl.BlockSpec((1, SC_WINDOW), lambda i: (0, i))] * n,
            out_specs=[], core_axis_name=("c", "s"), dimension_semantics=(pltpu.PARALLEL,),
        )(x_hbm, *idx_hbms)

    return scatter_kernel(x, *idx_list)


def _sc_gather_rows(y, idx):
    m = idx.shape[1]
    d = y.shape[1]

    @functools.partial(pl.kernel, out_type=jax.ShapeDtypeStruct((m, d), y.dtype), mesh=_sc_mesh(),
                       scratch_types=[])
    def gather_kernel(y_hbm, idx_hbm, o_hbm):
        def body(idx_vmem, o_vmem):
            pltpu.sync_copy(y_hbm.at[idx_vmem.at[0]], o_vmem)

        pltpu.emit_pipeline(
            body, grid=(m // SC_WINDOW,),
            in_specs=[pl.BlockSpec((1, SC_WINDOW), lambda i: (0, i))],
            out_specs=[pl.BlockSpec((SC_WINDOW, d), lambda i: (i, 0))],
            core_axis_name=("c", "s"), dimension_semantics=(pltpu.PARALLEL,),
        )(idx_hbm, o_hbm)

    return gather_kernel(y, idx)


def kernel(x, attn_norm, w_in, w_out, diff_lambda, diff_subln, mla_q_norm, mla_w_uq, mla_kv_norm, mla_w_ukv,
           swa_sinks, ffn_norm, ffn_w1, ffn_w3, ffn_w2, moe_router, moe_w1, moe_w3, moe_w2, final_norm):
    bsz, seq, _ = x.shape
    t = bsz * seq
    depth = w_in.shape[0]
    slopes_a, slopes_b, slopes_d = _alibi_slopes()
    tq_dense = _pick(seq, 512)
    tq_band = _pick(seq, 256)

    tabs = _rope_tables(seq)
    colscale = _col_scale()
    di = jnp.arange(tq_dense)
    causal_mask_t = jnp.where(di[:, None] <= di[None, :], 0.0, NEG_INF).astype(F32)
    tq_diff = tq_dense
    alibi_kb = _alibi_key_columns(slopes_a * LOG2E, tq_diff)
    tq_dil = _pick(seq, 512)
    n_off_dil = min((max(w for w, _ in DIL_PATTERNS) + tq_dil - 1) // tq_dil + 1, seq // tq_dil)
    n_off_swa = min((SWA_WINDOW - 1 + tq_band - 1) // tq_band + 1, seq // tq_band)
    dil_bias = _band_bias(slopes_b, tq_dil, n_off_dil, _dil_mult).reshape(2, 2, n_off_dil, tq_dil, tq_dil)
    swa_bias = _band_bias(slopes_d[np.array([0, 2, 1, 3])], tq_band, n_off_swa, _swa_mult
                          ).reshape(2, 2, n_off_swa, tq_band, tq_band)
    slopes_a_l2 = jnp.asarray(slopes_a * LOG2E, F32)
    dil_m0 = jnp.full((4,), NEG_INF, F32)

    x2d = x.reshape(t, D_MODEL)
    moe_bf16 = []
    for l in range(depth):
        w_main, wq_big, wkv, w4 = _prep_attn_weights(w_in[l], w_out[l], mla_w_uq[l], mla_w_ukv[l])
        a, avt, b, bvt, d, dvt, cq, ck, cvt = _proj(x2d, attn_norm[l][None], w_main, colscale, mla_q_norm[l][None],
                                                    wq_big, mla_kv_norm[l][None], wkv, tabs, seq)
        a3, b3, d3 = (v.reshape(bsz, seq, -1) for v in (a, b, d))
        sub128 = jnp.concatenate([diff_subln[l], diff_subln[l]])[None]
        o_a = _diff_attention(a3, avt, slopes_a_l2, causal_mask_t[:tq_diff, :tq_diff], alibi_kb, diff_lambda[l],
                              sub128, l, tq_diff)
        o_b = _band_attention(b3, bvt, 1, 2, dil_bias, dil_m0, 0.0, tq_dil, "dil_attn")
        o_c = _mla_attention(cq.reshape(bsz, seq, -1), ck.reshape(bsz, seq, -1), cvt, causal_mask_t, tq_dense)
        swa_m0 = (swa_sinks[l].astype(F32) * LOG2E)[np.array([0, 2, 1, 3])]
        o_d = _band_attention(d3, dvt, 2, 1, swa_bias, swa_m0, 1.0, tq_band, "swa_attn")
        outs = [v.reshape(t, 256) for v in (o_a, o_b, o_c, o_d)]
        j = l // 2
        if l % 2 == 0:
            x1, h2 = _outproj(*outs, w4, x2d, ffn_norm[l][None])
            nxt = [w[j].reshape(-1, w.shape[-1]) for w in (moe_w1, moe_w3, moe_w2)] if l + 1 < depth else []
            if any(a.shape[0] % (16 * (t // _pick(t, 512))) for a in nxt):
                nxt = []
            x2d, cast = _ffn(h2, ffn_w1[j].astype(BF16), ffn_w3[j].astype(BF16), ffn_w2[j].astype(BF16), x1, nxt)
            moe_bf16 = [c.reshape(moe_w.shape[1:]) for c, moe_w in zip(cast, (moe_w1, moe_w3, moe_w2))]
            moe = None
        else:
            r = jnp.pad(moe_router[j], ((0, 0), (0, LANES - N_EXPERTS)))
            r_hi, r_lo = _split_bf16(r, 2)
            x1, h2, logits = _outproj(*outs, w4, x2d, ffn_norm[l][None], (r_hi, r_lo))
            bm = _pick(t, 512)
            if not moe_bf16:
                moe_bf16 = [w[j].astype(BF16) for w in (moe_w1, moe_w3, moe_w2)]
            moe = _moe_layer(h2, logits, *moe_bf16, bm)
            x2d = x1
        if l < depth - 1 and moe is not None:
            raise NotImplementedError("MoE combine is fused into the final norm; MoE layer must be last")
    return _final(x2d, final_norm[None], moe).reshape(bsz, seq, D_MODEL)
```

```python
import functools
import math

import numpy as np
import jax
import jax.numpy as jnp
from jax import lax
from jax.experimental import pallas as pl
from jax.experimental.pallas import tpu as pltpu
from jax.experimental.pallas import tpu_sc as plsc

D_MODEL = 1024
DIFF_HEADS, DIFF_HD = 4, 32
DIL_HEADS, DIL_HD = 4, 64
DIL_PATTERNS = ((128, 1), (512, 4), (2048, 16))
MLA_HEADS, MLA_Q_LORA, MLA_KV_LORA, MLA_NOPE, MLA_ROPE, MLA_V = 4, 384, 128, 64, 32, 64
ROPE_THETA = 10000.0
SWA_HEADS, SWA_KV_HEADS, SWA_HD, SWA_WINDOW = 4, 2, 64, 128
D_FF = 2816
N_EXPERTS, TOP_K, D_FF_EXPERT = 8, 2, 3584
RMS_EPS = 1e-6
NEG_INF = -1e30
N_ALIBI = DIFF_HEADS + DIL_HEADS + SWA_HEADS

LOG2E = 1.4426950408889634
LANES = 128
HEAD_LANES = 64
VMEM_LIMIT = 56 * 1024 * 1024

F32 = jnp.float32
BF16 = jnp.bfloat16

_NT = (((1,), (1,)), ((), ()))


def _params(sem, vmem=VMEM_LIMIT):
    return pltpu.CompilerParams(dimension_semantics=sem, vmem_limit_bytes=vmem)


def _alibi_slopes():
    s = 2.0 ** (-8.0 * (np.arange(N_ALIBI) + 1) / N_ALIBI)
    return s[0::3], s[1::3], s[2::3]


def _pick(n, pref):
    t = min(pref, n)
    while n % t:
        t //= 2
    return t


N_SCALED = 2048
C_CQ, C_CKV, C_KPE, C_KPE_SW, N_MAIN = 2048, 2432, 2560, 2688, 2816


ONES_ROWS = 16
VT_ROWS = HEAD_LANES + ONES_ROWS


def _vt_with_ones(v):
    vt = v.T
    ones = jnp.ones((ONES_ROWS, v.shape[0]), F32)
    parts = []
    for hd in range(v.shape[1] // HEAD_LANES):
        parts += [vt[hd * HEAD_LANES:(hd + 1) * HEAD_LANES], ones]
    return jnp.concatenate(parts, axis=0).astype(BF16)


def _proj_body(x_ref, g_ref, w_ref, cs_ref, qn_ref, wq_ref, kn_ref, wkv_ref,
               cosq_ref, sinq_ref, cosk_ref, sink_ref,
               a_ref, avt_ref, b_ref, bvt_ref, d_ref, dvt_ref, cq_ref, ck_ref, cvt_ref):
    x = x_ref[...]
    ms = jnp.mean(x * x, axis=-1, keepdims=True)
    h = (x * lax.rsqrt(ms + RMS_EPS) * g_ref[...]).astype(BF16)
    acc = jnp.dot(h, w_ref[...], preferred_element_type=F32)
    sc = acc[:, :N_SCALED] * cs_ref[...]
    a_ref[...] = sc[:, 0:512].astype(BF16)
    avt_ref[...] = _vt_with_ones(sc[:, 512:768])
    b_ref[...] = sc[:, 768:1280].astype(BF16)
    bvt_ref[...] = _vt_with_ones(sc[:, 1280:1536])
    d_ref[...] = sc[:, 1536:1920].astype(BF16)
    dvt_ref[...] = _vt_with_ones(sc[:, 1920:2048])

    cq = acc[:, C_CQ:C_CKV]
    hq = (cq * lax.rsqrt(jnp.mean(cq * cq, axis=-1, keepdims=True) + RMS_EPS) * qn_ref[...]).astype(BF16)
    yq = jnp.dot(hq, wq_ref[...], preferred_element_type=F32)
    cosq, sinq = cosq_ref[...], sinq_ref[...]
    c_mla = (MLA_NOPE + MLA_ROPE) ** -0.5 * LOG2E
    for hd in range(MLA_HEADS):
        main = yq[:, hd * 256:(hd + 1) * 256]
        sw = yq[:, 1024 + hd * LANES:1024 + (hd + 1) * LANES]
        cq_ref[:, hd * 256:hd * 256 + LANES] = (main[:, :LANES] * cosq + sw * sinq).astype(BF16)
        cq_ref[:, hd * 256 + LANES:(hd + 1) * 256] = (main[:, LANES:] * c_mla).astype(BF16)

    ckv = acc[:, C_CKV:C_KPE]
    hk = (ckv * lax.rsqrt(jnp.mean(ckv * ckv, axis=-1, keepdims=True) + RMS_EPS) * kn_ref[...]).astype(BF16)
    kv = jnp.dot(hk, wkv_ref[...], preferred_element_type=F32)
    kr = (acc[:, C_KPE:C_KPE_SW] * cosk_ref[...] + acc[:, C_KPE_SW:N_MAIN] * sink_ref[...]).astype(BF16)
    for grp in range(2):
        ck_ref[:, grp * 256:grp * 256 + LANES] = kr
        ck_ref[:, grp * 256 + LANES:(grp + 1) * 256] = kv[:, grp * LANES:(grp + 1) * LANES].astype(BF16)
    cvt_ref[...] = _vt_with_ones(kv[:, 256:512])


def _proj(x2d, g, w_main, colscale, qn, wq_big, kn, wkv, tabs, seq):
    t = x2d.shape[0]
    tm = _pick(seq, 512)
    nsb = seq // tm
    full = lambda shape: pl.BlockSpec(shape, lambda i: (0,) * len(shape))
    tab = pl.BlockSpec((tm, LANES), lambda i: (i % nsb, 0))
    row = lambda w: pl.BlockSpec((tm, w), lambda i: (i, 0))
    vt = lambda w: pl.BlockSpec((None, w, tm), lambda i: (i // nsb, 0, i % nsb))
    row_sds = lambda w: jax.ShapeDtypeStruct((t, w), BF16)
    vt_sds = lambda w: jax.ShapeDtypeStruct((t // seq, w, seq), BF16)
    return pl.pallas_call(
        _proj_body,
        grid=(t // tm,),
        in_specs=[row(D_MODEL), full((1, D_MODEL)), full((D_MODEL, N_MAIN)), full((1, N_SCALED)),
                  full((1, MLA_Q_LORA)), full((MLA_Q_LORA, 1536)), full((1, MLA_KV_LORA)),
                  full((MLA_KV_LORA, 512)), tab, tab, tab, tab],
        out_specs=[row(512), vt(4 * VT_ROWS), row(512), vt(4 * VT_ROWS), row(384), vt(2 * VT_ROWS),
                   row(1024), row(512), vt(4 * VT_ROWS)],
        out_shape=[row_sds(512), vt_sds(4 * VT_ROWS), row_sds(512), vt_sds(4 * VT_ROWS), row_sds(384),
                   vt_sds(2 * VT_ROWS), row_sds(1024), row_sds(512), vt_sds(4 * VT_ROWS)],
        compiler_params=_params(("parallel",)),
        name="proj",
    )(x2d, g, w_main, colscale, qn, wq_big, kn, wkv, *tabs)


def _attn_pipeline(n_soft, n_steps, scores, shift, load_vt, m0, s_ref, acc_ref):
    def stage_a(t, ss, dst_ref, m_cur):
        m_next, alpha = [], []
        for i in range(n_soft):
            m_new = jnp.maximum(m_cur[i], jnp.max(ss[i], axis=0, keepdims=True) + shift(i, t))
            dst_ref[i] = ss[i]
            m_next.append(m_new)
            alpha.append(jnp.exp2(m_cur[i] - m_new))
        return tuple(m_next), tuple(alpha)

    def stage_b(t, src_ref, carry):
        m_cur, alpha = carry
        for i in range(n_soft):
            p = jnp.exp2(src_ref[i] - (m_cur[i] - shift(i, t)))
            acc_ref[i] = alpha[i] * acc_ref[i] + jnp.dot(load_vt(t, i), p.astype(BF16),
                                                         preferred_element_type=F32)

    def half(t, src_ref, dst_ref, carry):
        ss = [scores(i, t + 1, False) for i in range(n_soft)]
        stage_b(t, src_ref, carry)
        return stage_a(t + 1, ss, dst_ref, carry[0])

    def body(u, carry):
        carry = half(2 * u, s_ref[0], s_ref[1], carry)
        return half(2 * u + 1, s_ref[1], s_ref[0], carry)

    carry = stage_a(0, [scores(i, 0, True) for i in range(n_soft)], s_ref[0], m0)
    carry = lax.fori_loop(0, n_steps // 2, body, carry)

    @pl.when(n_steps % 2 == 1)
    def _():
        stage_b(n_steps, s_ref[1], half(n_steps - 1, s_ref[0], s_ref[1], carry))

    @pl.when(n_steps % 2 == 0)
    def _():
        stage_b(n_steps, s_ref[0], carry)


def _row_mask(shape, lo, hi):
    row = lax.broadcasted_iota(jnp.int32, shape, 0)
    return jnp.logical_and(row >= lo, row < hi)


def _lane_mask(shape, lo, hi):
    lane = lax.broadcasted_iota(jnp.int32, shape, len(shape) - 1)
    return jnp.logical_and(lane >= lo, lane < hi)


def _normalized(acc_ref, i):
    return acc_ref[i, 0:HEAD_LANES, :] / acc_ref[i, HEAD_LANES:HEAD_LANES + 1, :]


def _diff_body(sl_ref, q_ref, k_ref, vt_ref, mask_ref, kb_ref, lam_ref, sub_ref, o_ref,
               qs_ref, s0_ref, s1_ref, acc_ref, *, tq, lam_init):
    qi = pl.program_id(1)
    tk = tq
    n_soft = 2 * DIFF_HEADS
    head_of = lambda i: i // 2
    for pair in range(2):
        q = q_ref[:, pair * LANES:(pair + 1) * LANES].astype(F32)
        for hd in range(2):
            sel = jnp.where(_lane_mask(q.shape, N_BIAS_SPLIT * hd, N_BIAS_SPLIT * (hd + 1)), 1.0, 0.0).astype(BF16)
            for mp in range(2):
                i = 4 * pair + 2 * hd + mp
                lo = hd * HEAD_LANES + mp * DIFF_HD
                qs_ref[i, :, 0:LANES] = jnp.where(_lane_mask(q.shape, lo, lo + DIFF_HD), q, 0.0).astype(BF16)
                qs_ref[i, :, LANES:2 * LANES] = sel
    acc_ref[...] = jnp.zeros(acc_ref.shape, F32)

    def scores(i, t, first):
        off = pl.multiple_of((qi - t) * tk, tk)
        pair = i // 4
        k = jnp.concatenate([k_ref[pl.ds(off, tk), pair * LANES:(pair + 1) * LANES], kb_ref[pair]], axis=1)
        s = lax.dot_general(k, qs_ref[i], _NT, preferred_element_type=F32)
        return s + mask_ref[...] if first else s

    def shift(i, t):
        return sl_ref[head_of(i)] * (-(t * tk)).astype(F32) if not isinstance(t, int) else 0.0

    def load_vt(t, i):
        return vt_ref[pl.ds(head_of(i) * VT_ROWS, VT_ROWS), pl.ds(pl.multiple_of((qi - t) * tk, tk), tk)]

    m0 = tuple(jnp.full((1, tq), NEG_INF, F32) for _ in range(n_soft))
    _attn_pipeline(n_soft, qi, scores, shift, load_vt, m0, (s0_ref, s1_ref), acc_ref)

    lp = lam_ref[...]
    lam = (jnp.exp(jnp.sum(lp[0:1, :] * lp[1:2, :], axis=-1, keepdims=True))
           - jnp.exp(jnp.sum(lp[2:3, :] * lp[3:4, :], axis=-1, keepdims=True)) + lam_init)
    in_h0 = _lane_mask((tq, LANES), 0, HEAD_LANES)
    for pair in range(2):
        o_h = [_normalized(acc_ref, 4 * pair + 2 * hd) - lam * _normalized(acc_ref, 4 * pair + 2 * hd + 1)
               for hd in range(2)]
        o = jnp.concatenate(o_h, axis=0).T
        sq = o * o
        ms0 = jnp.sum(jnp.where(in_h0, sq, 0.0), axis=-1, keepdims=True)
        ms1 = jnp.sum(jnp.where(in_h0, 0.0, sq), axis=-1, keepdims=True)
        ms = jnp.where(in_h0, ms0, ms1) * (1.0 / HEAD_LANES)
        y = o * lax.rsqrt(ms + RMS_EPS) * sub_ref[...]
        o_ref[:, pair * LANES:(pair + 1) * LANES] = (y * (1.0 - lam_init)).astype(o_ref.dtype)


N_BIAS_SPLIT = 3


def _alibi_key_columns(slopes_l2, tk):
    dj = jnp.arange(tk, dtype=F32)[None, :, None]
    val = jnp.asarray(slopes_l2, F32).reshape(-1, 1, 2) * dj
    cols = jnp.stack(_split_bf16(val, N_BIAS_SPLIT), axis=-1).reshape(val.shape[0], tk, 2 * N_BIAS_SPLIT)
    return jnp.pad(cols, ((0, 0), (0, 0), (0, LANES - 2 * N_BIAS_SPLIT)))


def _split_bf16(x, n):
    pieces = []
    for _ in range(n):
        bits = lax.bitcast_convert_type(x, jnp.uint32) & jnp.uint32(0xFFFF0000)
        head = lax.bitcast_convert_type(bits, F32)
        pieces.append(head.astype(BF16))
        x = x - head
    return pieces


def _diff_attention(a3, avt, slopes_l2, mask_t, kb, lam_p, sub128, layer_idx, tq):
    b, s, _ = a3.shape
    nq = s // tq
    lam_init = 0.8 - 0.6 * math.exp(-0.3 * layer_idx)
    body = functools.partial(_diff_body, tq=tq, lam_init=lam_init)
    grid_spec = pltpu.PrefetchScalarGridSpec(
        num_scalar_prefetch=1,
        grid=(b, nq),
        in_specs=[
            pl.BlockSpec((None, tq, 2 * LANES), lambda bi, qi, sl: (bi, qi, 0)),
            pl.BlockSpec((None, s, 2 * LANES), lambda bi, qi, sl: (bi, 0, 1)),
            pl.BlockSpec((None, DIFF_HEADS * VT_ROWS, s), lambda bi, qi, sl: (bi, 0, 0)),
            pl.BlockSpec((tq, tq), lambda bi, qi, sl: (0, 0), pipeline_mode=pl.Buffered(1)),
            pl.BlockSpec((2, tq, LANES), lambda bi, qi, sl: (0, 0, 0), pipeline_mode=pl.Buffered(1)),
            pl.BlockSpec((4, DIFF_HD), lambda bi, qi, sl: (0, 0)),
            pl.BlockSpec((1, LANES), lambda bi, qi, sl: (0, 0)),
        ],
        out_specs=pl.BlockSpec((None, tq, 2 * LANES), lambda bi, qi, sl: (bi, qi, 0)),
        scratch_shapes=[pltpu.VMEM((2 * DIFF_HEADS, tq, 2 * LANES), BF16),
                        pltpu.VMEM((2 * DIFF_HEADS, tq, tq), F32), pltpu.VMEM((2 * DIFF_HEADS, tq, tq), F32),
                        pltpu.VMEM((2 * DIFF_HEADS, VT_ROWS, tq), F32)],
    )
    return pl.pallas_call(
        body, grid_spec=grid_spec,
        out_shape=jax.ShapeDtypeStruct((b, s, 256), BF16),
        compiler_params=_params(("parallel", "arbitrary")),
        name="diff_attn",
    )(slopes_l2, a3, a3, avt, mask_t, kb, lam_p, sub128)


def _mla_body(q_ref, k_ref, vt_ref, mask_ref, o_ref, s0_ref, s1_ref, acc_ref, *, tq):
    qi = pl.program_id(1)
    tk = tq
    acc_ref[...] = jnp.zeros(acc_ref.shape, F32)

    def scores(i, t, first):
        k = k_ref[pl.ds(pl.multiple_of((qi - t) * tk, tk), tk), (i // 2) * 256:(i // 2 + 1) * 256]
        s = lax.dot_general(k, q_ref[:, i * 256:(i + 1) * 256], _NT, preferred_element_type=F32)
        return s + mask_ref[...] if first else s

    def load_vt(t, i):
        return vt_ref[pl.ds(i * VT_ROWS, VT_ROWS), pl.ds(pl.multiple_of((qi - t) * tk, tk), tk)]

    m0 = tuple(jnp.full((1, tq), NEG_INF, F32) for _ in range(MLA_HEADS))
    _attn_pipeline(MLA_HEADS, qi, scores, lambda i, t: 0.0, load_vt, m0, (s0_ref, s1_ref), acc_ref)
    o_t = jnp.concatenate([_normalized(acc_ref, i) for i in range(MLA_HEADS)], axis=0)
    o_ref[...] = o_t.T.astype(o_ref.dtype)


def _mla_attention(cq3, ck3, cvt, mask_t, tq):
    b, s, _ = cq3.shape
    nq = s // tq
    return pl.pallas_call(
        functools.partial(_mla_body, tq=tq),
        grid=(b, nq),
        in_specs=[
            pl.BlockSpec((None, tq, MLA_HEADS * 256), lambda bi, qi: (bi, qi, 0)),
            pl.BlockSpec((None, s, 512), lambda bi, qi: (bi, 0, 0)),
            pl.BlockSpec((None, MLA_HEADS * VT_ROWS, s), lambda bi, qi: (bi, 0, 0)),
            pl.BlockSpec((tq, tq), lambda bi, qi: (0, 0)),
        ],
        out_specs=pl.BlockSpec((None, tq, 256), lambda bi, qi: (bi, qi, 0)),
        out_shape=jax.ShapeDtypeStruct((b, s, 256), BF16),
        scratch_shapes=[pltpu.VMEM((MLA_HEADS, tq, tq), F32), pltpu.VMEM((MLA_HEADS, tq, tq), F32),
                        pltpu.VMEM((MLA_HEADS, VT_ROWS, tq), F32)],
        compiler_params=_params(("parallel", "arbitrary")),
        name="mla_attn",
    )(cq3, ck3, cvt, mask_t)


def _band_body(m0_ref, q_ref, k_ref, vt_ref, bias_ref, o_ref, qs_ref, s0_ref, s1_ref, acc_ref,
               *, tq, n_off, l_init, kv_slabs):
    qi = pl.program_id(1)
    tk = tq
    n_soft = 4
    for sl in range(2):
        q = q_ref[:, sl * LANES:(sl + 1) * LANES].astype(F32)
        for g in range(2):
            qs_ref[2 * sl + g] = jnp.where(_lane_mask(q.shape, g * HEAD_LANES, (g + 1) * HEAD_LANES), q, 0.0
                                           ).astype(BF16)
    for i in range(n_soft):
        acc_ref[i] = jnp.where(_row_mask((VT_ROWS, tq), 0, HEAD_LANES), 0.0, l_init).astype(F32)

    kv_of = lambda i: (i // 2) * (kv_slabs - 1)

    def scores(i, t, first):
        k = k_ref[pl.ds(pl.multiple_of((qi - t) * tk, tk), tk), kv_of(i) * LANES:(kv_of(i) + 1) * LANES]
        return lax.dot_general(k, qs_ref[i], _NT, preferred_element_type=F32) + bias_ref[i // 2, i % 2, t]

    def load_vt(t, i):
        return vt_ref[pl.ds((2 * kv_of(i) + i % 2) * VT_ROWS, VT_ROWS),
                      pl.ds(pl.multiple_of((qi - t) * tk, tk), tk)]

    m0 = tuple(jnp.full((1, tq), m0_ref[i], F32) for i in range(n_soft))
    _attn_pipeline(n_soft, jnp.minimum(qi, n_off - 1), scores, lambda i, t: 0.0, load_vt, m0,
                   (s0_ref, s1_ref), acc_ref)
    o_t = jnp.concatenate([_normalized(acc_ref, i) for i in range(n_soft)], axis=0)
    o_ref[...] = o_t.T.astype(o_ref.dtype)


def _band_attention(x3, xvt, k_blk, kv_slabs, bias, m0, l_init, tq, name):
    b, s, _ = x3.shape
    nq = s // tq
    n_off = bias.shape[2]
    n_soft = 4
    grid_spec = pltpu.PrefetchScalarGridSpec(
        num_scalar_prefetch=1,
        grid=(b, nq),
        in_specs=[
            pl.BlockSpec((None, tq, 2 * LANES), lambda bi, qi, m0r: (bi, qi, 0)),
            pl.BlockSpec((None, s, kv_slabs * LANES), lambda bi, qi, m0r: (bi, 0, k_blk)),
            pl.BlockSpec((None, kv_slabs * 2 * VT_ROWS, s), lambda bi, qi, m0r: (bi, 0, 0)),
            pl.BlockSpec((2, 2, n_off, tq, tq), lambda bi, qi, m0r: (0, 0, 0, 0, 0),
                         pipeline_mode=pl.Buffered(1)),
        ],
        out_specs=pl.BlockSpec((None, tq, 2 * LANES), lambda bi, qi, m0r: (bi, qi, 0)),
        scratch_shapes=[pltpu.VMEM((n_soft, tq, LANES), BF16), pltpu.VMEM((n_soft, tq, tq), F32),
                        pltpu.VMEM((n_soft, tq, tq), F32), pltpu.VMEM((n_soft, VT_ROWS, tq), F32)],
    )
    return pl.pallas_call(
        functools.partial(_band_body, tq=tq, n_off=n_off, l_init=l_init, kv_slabs=kv_slabs),
        grid_spec=grid_spec,
        out_shape=jax.ShapeDtypeStruct((b, s, 256), BF16),
        compiler_params=_params(("parallel", "arbitrary")),
        name=name,
    )(m0, x3, x3, xvt, bias)


def _band_bias(slopes, tq, n_off, mult_fn):
    di = jnp.arange(tq, dtype=jnp.int32)[None, :]
    dj = jnp.arange(tq, dtype=jnp.int32)[:, None]
    off = jnp.arange(n_off, dtype=jnp.int32)[:, None, None]
    delta = off * tq + di - dj
    mult = mult_fn(delta)
    sl = jnp.asarray(slopes, F32)[:, None, None, None] * LOG2E
    val = -sl * delta.astype(F32)[None] + jnp.log2(jnp.maximum(mult, 1).astype(F32))[None]
    return jnp.where((mult > 0)[None], val, NEG_INF)


def _dil_mult(delta):
    m = jnp.zeros(delta.shape, jnp.int32)
    for window, d in DIL_PATTERNS:
        m = m + ((delta >= 0) & (delta <= window) & (delta % d == 0)).astype(jnp.int32)
    return m


def _swa_mult(delta):
    return ((delta >= 0) & (delta <= SWA_WINDOW - 1)).astype(jnp.int32)


def _outproj_body(*refs, with_router):
    if with_router:
        (oa, ob, oc, od, w_ref, x_ref, g_ref, rh_ref, rl_ref, x1_ref, h2_ref, lg_ref) = refs
    else:
        (oa, ob, oc, od, w_ref, x_ref, g_ref, x1_ref, h2_ref) = refs
    acc = x_ref[...]
    for i, o in enumerate((oa, ob, oc, od)):
        acc = acc + jnp.dot(o[...], w_ref[i], preferred_element_type=F32)
    x1_ref[...] = acc
    h2 = acc * lax.rsqrt(jnp.mean(acc * acc, axis=-1, keepdims=True) + RMS_EPS) * g_ref[...]
    hi = h2.astype(BF16)
    if with_router:
        for c, words in enumerate(_pack_rows(h2)):
            h2_ref[c] = words
        lo = (h2 - hi.astype(F32)).astype(BF16)
        lg_ref[...] = (jnp.dot(hi, rh_ref[...], preferred_element_type=F32)
                       + (jnp.dot(hi, rl_ref[...], preferred_element_type=F32)
                          + jnp.dot(lo, rh_ref[...], preferred_element_type=F32)))
    else:
        h2_ref[...] = hi


def _outproj(oa, ob, oc, od, w4, x2d, g, router_hl=None):
    t = x2d.shape[0]
    tm = _pick(t, 512)
    with_router = router_hl is not None
    full = lambda shape: pl.BlockSpec(shape, lambda i: (0,) * len(shape))
    row = lambda w: pl.BlockSpec((tm, w), lambda i: (i, 0))
    in_specs = [row(256)] * 4 + [full((4, 256, D_MODEL)), row(D_MODEL), full((1, D_MODEL))]
    args = [oa, ob, oc, od, w4, x2d, g]
    out_specs = [row(D_MODEL), row(D_MODEL)]
    out_shape = [jax.ShapeDtypeStruct((t, D_MODEL), F32), jax.ShapeDtypeStruct((t, D_MODEL), BF16)]
    if with_router:
        out_specs[1] = pl.BlockSpec((SC_CHUNKS, tm, SC_COLS), lambda i: (0, i, 0))
        out_shape[1] = jax.ShapeDtypeStruct((SC_CHUNKS, t, SC_COLS), U32)
        in_specs += [full((D_MODEL, LANES))] * 2
        args += list(router_hl)
        out_specs.append(row(LANES))
        out_shape.append(jax.ShapeDtypeStruct((t, LANES), F32))
    return pl.pallas_call(
        functools.partial(_outproj_body, with_router=with_router),
        grid=(t // tm,), in_specs=in_specs, out_specs=out_specs, out_shape=out_shape,
        compiler_params=_params(("parallel",)),
        name="outproj",
    )(*args)


FF_CHUNK = 512


def _swiglu(x, w1_ref, w3_ref, w2_ref):
    tf = w1_ref.shape[1]
    total = None
    for lo in range(0, tf, FF_CHUNK):
        hi = min(lo + FF_CHUNK, tf)
        a = jnp.dot(x, w1_ref[:, lo:hi], preferred_element_type=F32)
        b = jnp.dot(x, w3_ref[:, lo:hi], preferred_element_type=F32)
        mid = (a * jax.nn.sigmoid(a) * b).astype(BF16)
        part = jnp.dot(mid, w2_ref[lo:hi, :], preferred_element_type=F32)
        total = part if total is None else total + part
    return total


def _ffn_body(*refs, n_cast):
    h_ref, w1_ref, w3_ref, w2_ref, x_ref = refs[:5]
    cast_in, o_ref, cast_out = refs[5:5 + n_cast], refs[5 + n_cast], refs[6 + n_cast:]
    o_ref[...] = x_ref[...] + _swiglu(h_ref[...], w1_ref, w3_ref, w2_ref)
    for src, dst in zip(cast_in, cast_out):
        dst[...] = src[...].astype(BF16)


def _ffn(h2, w1, w3, w2, x1, to_cast=()):
    t = h2.shape[0]
    tm = _pick(t, 512)
    n_steps = t // tm
    resident = lambda shape: pl.BlockSpec(shape, lambda i: (0, 0), pipeline_mode=pl.Buffered(1))
    sliced = [pl.BlockSpec((a.shape[0] // n_steps, a.shape[1]), lambda i: (i, 0)) for a in to_cast]
    outs = pl.pallas_call(
        functools.partial(_ffn_body, n_cast=len(to_cast)),
        grid=(n_steps,),
        in_specs=[pl.BlockSpec((tm, D_MODEL), lambda i: (i, 0)),
                  resident((D_MODEL, D_FF)), resident((D_MODEL, D_FF)), resident((D_FF, D_MODEL)),
                  pl.BlockSpec((tm, D_MODEL), lambda i: (i, 0))] + sliced,
        out_specs=[pl.BlockSpec((tm, D_MODEL), lambda i: (i, 0))] + sliced,
        out_shape=[jax.ShapeDtypeStruct((t, D_MODEL), F32)]
        + [jax.ShapeDtypeStruct(a.shape, BF16) for a in to_cast],
        compiler_params=_params(("parallel",)),
        name="ffn",
    )(h2, w1, w3, w2, x1, *to_cast)
    return outs[0], outs[1:]


def _route_body(lg_ref, tri_ref, info_ref, cnt_ref, carry_ref):
    i = pl.program_id(0)

    @pl.when(i == 0)
    def _():
        carry_ref[...] = jnp.zeros(carry_ref.shape, F32)

    lg = lg_ref[...]
    lane = lax.broadcasted_iota(jnp.int32, lg.shape, 1)
    lg = jnp.where(lane < N_EXPERTS, lg, -jnp.inf)
    m1 = jnp.max(lg, axis=-1, keepdims=True)
    i1 = jnp.min(jnp.where(lg == m1, lane, LANES), axis=-1, keepdims=True)
    oh1 = lane == i1
    lg2 = jnp.where(oh1, -jnp.inf, lg)
    m2 = jnp.max(lg2, axis=-1, keepdims=True)
    i2 = jnp.min(jnp.where(lg2 == m2, lane, LANES), axis=-1, keepdims=True)
    oh2 = lane == i2
    e2 = jnp.exp(m2 - m1)
    g1 = 1.0 / (1.0 + e2)
    g2 = e2 / (1.0 + e2)
    oh = jnp.where(jnp.logical_or(oh1, oh2), 1.0, 0.0)
    rank = jnp.dot(tri_ref[...], oh.astype(BF16), preferred_element_type=F32) + carry_ref[...]
    r1 = jnp.sum(jnp.where(oh1, rank, 0.0), axis=-1, keepdims=True)
    r2 = jnp.sum(jnp.where(oh2, rank, 0.0), axis=-1, keepdims=True)
    carry_ref[...] += jnp.sum(oh, axis=0, keepdims=True)
    cnt_ref[...] = carry_ref[...]
    info = jnp.zeros(lg.shape, F32)
    for c, val in enumerate((i1.astype(F32), i2.astype(F32), r1, r2, g1, g2)):
        info = jnp.where(lane == c, val, info)
    info_ref[...] = info


def _route(logits):
    t = logits.shape[0]
    tm = _pick(t, 512)
    tri = (jnp.arange(tm)[:, None] > jnp.arange(tm)[None, :]).astype(BF16)
    return pl.pallas_call(
        _route_body,
        grid=(t // tm,),
        in_specs=[pl.BlockSpec((tm, LANES), lambda i: (i, 0)), pl.BlockSpec((tm, tm), lambda i: (0, 0))],
        out_specs=[pl.BlockSpec((tm, LANES), lambda i: (i, 0)), pl.BlockSpec((1, LANES), lambda i: (0, 0))],
        out_shape=[jax.ShapeDtypeStruct((t, LANES), F32), jax.ShapeDtypeStruct((1, LANES), F32)],
        scratch_shapes=[pltpu.VMEM((1, LANES), F32)],
        compiler_params=_params(("arbitrary",)),
        name="route",
    )(logits, tri)


def _moe_ffn_body(be_ref, bv_ref, x_ref, w1_ref, w3_ref, w2_ref, o_ref):
    valid = bv_ref[pl.program_id(0)]

    @pl.when(valid > 0)
    def _():
        x = _unpack_rows([x_ref[c] for c in range(SC_CHUNKS)])
        x = jnp.where(lax.broadcasted_iota(jnp.int32, x.shape, 0) < valid, x, 0.0).astype(BF16)
        for c, words in enumerate(_pack_rows(_swiglu(x, w1_ref, w3_ref, w2_ref))):
            o_ref[c] = words

    @pl.when(valid == 0)
    def _():
        o_ref[...] = jnp.zeros(o_ref.shape, o_ref.dtype)


def _moe_ffn(xs, blk_expert, blk_valid, w1, w3, w2, bm):
    rows = xs.shape[1]
    chunked = pl.BlockSpec((SC_CHUNKS, bm, SC_COLS), lambda i, be, bv: (0, i, 0))
    expert = lambda shape: pl.BlockSpec((None,) + shape, lambda i, be, bv: (be[i], 0, 0),
                                        pipeline_mode=pl.Buffered(1))
    grid_spec = pltpu.PrefetchScalarGridSpec(
        num_scalar_prefetch=2,
        grid=(rows // bm,),
        in_specs=[chunked, expert((D_MODEL, D_FF_EXPERT)), expert((D_MODEL, D_FF_EXPERT)),
                  expert((D_FF_EXPERT, D_MODEL))],
        out_specs=chunked,
    )
    return pl.pallas_call(
        _moe_ffn_body, grid_spec=grid_spec,
        out_shape=jax.ShapeDtypeStruct(xs.shape, U32),
        compiler_params=_params(("arbitrary",)),
        name="moe_ffn",
    )(blk_expert, blk_valid, xs, w1, w3, w2)


def _final_body(*refs, with_moe):
    if with_moe:
        x_ref, y0_ref, y1_ref, info_ref, g_ref, o_ref = refs
        info = info_ref[...]
        y0 = _unpack_rows([y0_ref[c] for c in range(SC_CHUNKS)])
        y1 = _unpack_rows([y1_ref[c] for c in range(SC_CHUNKS)])
        x = x_ref[...] + (info[:, 4:5] * y0 + info[:, 5:6] * y1)
    else:
        x_ref, g_ref, o_ref = refs
        x = x_ref[...]
    o_ref[...] = x * lax.rsqrt(jnp.mean(x * x, axis=-1, keepdims=True) + RMS_EPS) * g_ref[...]


def _final(x2d, g, moe=None):
    t = x2d.shape[0]
    tm = _pick(t, 512)
    row = lambda w: pl.BlockSpec((tm, w), lambda i: (i, 0))
    in_specs, args = [row(D_MODEL)], [x2d]
    if moe is not None:
        yg, info = moe
        slot = lambda k: pl.BlockSpec((None, SC_CHUNKS, tm, SC_COLS), lambda i: (k, 0, i, 0))
        in_specs += [slot(0), slot(1), row(LANES)]
        args += [yg, yg, info]
    in_specs.append(pl.BlockSpec((1, D_MODEL), lambda i: (0, 0)))
    args.append(g)
    return pl.pallas_call(
        functools.partial(_final_body, with_moe=moe is not None),
        grid=(t // tm,), in_specs=in_specs, out_specs=row(D_MODEL),
        out_shape=jax.ShapeDtypeStruct((t, D_MODEL), F32),
        compiler_params=_params(("parallel",)),
        name="final",
    )(*args)


def _prep_attn_weights(w_in, w_out, w_uq, w_ukv):
    widths = (256, 256, 256, 256, 256, 256, MLA_Q_LORA, MLA_KV_LORA, MLA_ROPE, 256, 128, 128)
    offs = np.concatenate([[0], np.cumsum(widths)])
    qa, ka, va, qb, kb, vb, cq, ckv, kpe, qd, kd, vd = [w_in[:, offs[i]:offs[i + 1]] for i in range(12)]
    hperm = np.array([0, 2, 1, 3])
    qd = qd.reshape(D_MODEL, 4, SWA_HD)[:, hperm].reshape(D_MODEL, 256)
    pad = jnp.zeros((D_MODEL, LANES - MLA_ROPE), w_in.dtype)
    half = MLA_ROPE // 2
    kpe_sw = jnp.concatenate([kpe[:, half:], kpe[:, :half]], axis=1)
    w_main = jnp.concatenate([qa, ka, va, qb, kb, vb, qd, kd, vd, cq, ckv, kpe, pad, kpe_sw, pad], axis=1)

    uq = w_uq.reshape(MLA_Q_LORA, MLA_HEADS, MLA_NOPE + MLA_ROPE)
    z = lambda n: jnp.zeros((MLA_Q_LORA, n), w_uq.dtype)
    mains, sws = [], []
    for hd in range(MLA_HEADS):
        nope, rope = uq[:, hd, :MLA_NOPE], uq[:, hd, MLA_NOPE:]
        nope128 = jnp.concatenate([nope, z(64)] if hd % 2 == 0 else [z(64), nope], axis=1)
        mains.append(jnp.concatenate([rope, z(LANES - MLA_ROPE), nope128], axis=1))
        sws.append(jnp.concatenate([rope[:, half:], rope[:, :half], z(LANES - MLA_ROPE)], axis=1))
    wq_big = jnp.concatenate(mains + sws, axis=1)

    ukv = w_ukv.reshape(MLA_KV_LORA, MLA_HEADS, MLA_NOPE + MLA_V)
    wkv = jnp.concatenate([ukv[:, :, :MLA_NOPE].reshape(MLA_KV_LORA, 256),
                           ukv[:, :, MLA_NOPE:].reshape(MLA_KV_LORA, 256)], axis=1)

    wo = w_out.reshape(4, 256, D_MODEL)
    wo_d = wo[3].reshape(4, SWA_HD, D_MODEL)[hperm].reshape(256, D_MODEL)
    w4 = jnp.stack([wo[0], wo[1], wo[2], wo_d])
    return w_main.astype(BF16), wq_big.astype(BF16), wkv.astype(BF16), w4.astype(BF16)


def _rope_tables(s):
    inv = ROPE_THETA ** (-jnp.arange(0, MLA_ROPE, 2, dtype=F32) / MLA_ROPE)
    ang = jnp.arange(s, dtype=F32)[:, None] * inv[None, :]
    cos, sin = jnp.cos(ang), jnp.sin(ang)
    z = jnp.zeros((s, LANES - MLA_ROPE), F32)
    cos128 = jnp.concatenate([cos, cos, z], axis=1)
    sin128 = jnp.concatenate([-sin, sin, z], axis=1)
    c_mla = (MLA_NOPE + MLA_ROPE) ** -0.5 * LOG2E
    return cos128 * c_mla, sin128 * c_mla, cos128, sin128


def _col_scale():
    ca = DIFF_HD ** -0.5 * LOG2E
    cb = DIL_HD ** -0.5 * LOG2E
    cd = SWA_HD ** -0.5 * LOG2E
    v = np.ones((1, N_SCALED), np.float32)
    v[0, 0:256] = ca
    v[0, 768:1024] = cb
    v[0, 1536:1792] = cd
    return jnp.asarray(v)


def _moe_layer(h2, logits, w1, w3, w2, bm):
    t = logits.shape[0]
    info, counts = _route(logits)
    cnt = counts[0, :N_EXPERTS].astype(jnp.int32)
    padded = (cnt + bm - 1) // bm * bm
    pends = jnp.cumsum(padded)
    pstarts = pends - padded
    e12 = info[:, 0:2].astype(jnp.int32)
    pos = pstarts[e12] + info[:, 2:4].astype(jnp.int32)
    nb = (t * TOP_K) // bm + N_EXPERTS
    rows = nb * bm
    blk_start = jnp.arange(nb, dtype=jnp.int32) * bm
    blk_expert = jnp.minimum(jnp.sum((pends[None, :] <= blk_start[:, None]).astype(jnp.int32), axis=1),
                             N_EXPERTS - 1)
    blk_valid = jnp.clip(pstarts[blk_expert] + cnt[blk_expert] - blk_start, 0, bm).astype(jnp.int32)
    sub = (jnp.arange(SC_CHUNKS, dtype=jnp.int32) * rows)[None, :, None] + pos.T[:, None, :]
    xs = _sc_scatter_rows(h2.reshape(SC_CHUNKS * t, SC_COLS),
                          [sub[k].reshape(1, SC_CHUNKS * t) for k in range(TOP_K)], SC_CHUNKS * rows)
    ys = _moe_ffn(xs.reshape(SC_CHUNKS, rows, SC_COLS), blk_expert.astype(jnp.int32), blk_valid, w1, w3, w2, bm)
    yg = _sc_gather_rows(ys.reshape(SC_CHUNKS * rows, SC_COLS), sub.reshape(1, TOP_K * SC_CHUNKS * t))
    return yg.reshape(TOP_K, SC_CHUNKS, t, SC_COLS), info


SC_WINDOW = 128
SC_COLS = 256
SC_CHUNKS = D_MODEL // (2 * SC_COLS)
U32 = jnp.uint32


def _bf16_bits(v):
    bits = lax.bitcast_convert_type(v, U32)
    return bits + jnp.uint32(0x7FFF) + ((bits >> 16) & jnp.uint32(1))


def _pack_rows(v):
    out = []
    for c in range(SC_CHUNKS):
        lo = _bf16_bits(v[:, 2 * c * SC_COLS:(2 * c + 1) * SC_COLS]) >> 16
        hi = _bf16_bits(v[:, (2 * c + 1) * SC_COLS:(2 * c + 2) * SC_COLS]) & jnp.uint32(0xFFFF0000)
        out.append(lo | hi)
    return out


def _unpack_rows(chunks):
    cols = []
    for u in chunks:
        cols.append(lax.bitcast_convert_type(u << 16, F32))
        cols.append(lax.bitcast_convert_type(u & jnp.uint32(0xFFFF0000), F32))
    return jnp.concatenate(cols, axis=1)


def _sc_mesh():
    return plsc.VectorSubcoreMesh(core_axis_name="c", subcore_axis_name="s")


def _sc_scatter_rows(x, idx_list, rows):
    t, d = x.shape
    n = len(idx_list)

    @functools.partial(pl.kernel, out_type=jax.ShapeDtypeStruct((rows, d), x.dtype), mesh=_sc_mesh(),
                       scratch_types=[])
    def scatter_kernel(x_hbm, *rest):
        idx_hbms, o_hbm = rest[:n], rest[n]

        def body(x_vmem, *idx_vmems):
            for iv in idx_vmems:
                pltpu.sync_copy(x_vmem, o_hbm.at[iv.at[0]])

        pltpu.emit_pipeline(
            body, grid=(t // SC_WINDOW,),
            in_specs=[pl.BlockSpec((SC_WINDOW, d), lambda i: (i, 0))]
            + [pl.BlockSpec((1, SC_WINDOW), lambda i: (0, i))] * n,
            out_specs=[], core_axis_name=("c", "s"), dimension_semantics=(pltpu.PARALLEL,),
        )(x_hbm, *idx_hbms)

    return scatter_kernel(x, *idx_list)


def _sc_gather_rows(y, idx):
    m = idx.shape[1]
    d = y.shape[1]

    @functools.partial(pl.kernel, out_type=jax.ShapeDtypeStruct((m, d), y.dtype), mesh=_sc_mesh(),
                       scratch_types=[])
    def gather_kernel(y_hbm, idx_hbm, o_hbm):
        def body(idx_vmem, o_vmem):
            pltpu.sync_copy(y_hbm.at[idx_vmem.at[0]], o_vmem)

        pltpu.emit_pipeline(
            body, grid=(m // SC_WINDOW,),
            in_specs=[pl.BlockSpec((1, SC_WINDOW), lambda i: (0, i))],
            out_specs=[pl.BlockSpec((SC_WINDOW, d), lambda i: (i, 0))],
            core_axis_name=("c", "s"), dimension_semantics=(pltpu.PARALLEL,),
        )(idx_hbm, o_hbm)

    return gather_kernel(y, idx)


def kernel(x, attn_norm, w_in, w_out, diff_lambda, diff_subln, mla_q_norm, mla_w_uq, mla_kv_norm, mla_w_ukv,
           swa_sinks, ffn_norm, ffn_w1, ffn_w3, ffn_w2, moe_router, moe_w1, moe_w3, moe_w2, final_norm):
    bsz, seq, _ = x.shape
    t = bsz * seq
    depth = w_in.shape[0]
    slopes_a, slopes_b, slopes_d = _alibi_slopes()
    tq_dense = _pick(seq, 512)
    tq_band = _pick(seq, 256)

    tabs = _rope_tables(seq)
    colscale = _col_scale()
    di = jnp.arange(tq_dense)
    causal_mask_t = jnp.where(di[:, None] <= di[None, :], 0.0, NEG_INF).astype(F32)
    tq_diff = tq_dense
    alibi_kb = _alibi_key_columns(slopes_a * LOG2E, tq_diff)
    tq_dil = _pick(seq, 512)
    n_off_dil = min((max(w for w, _ in DIL_PATTERNS) + tq_dil - 1) // tq_dil + 1, seq // tq_dil)
    n_off_swa = min((SWA_WINDOW - 1 + tq_band - 1) // tq_band + 1, seq // tq_band)
    dil_bias = _band_bias(slopes_b, tq_dil, n_off_dil, _dil_mult).reshape(2, 2, n_off_dil, tq_dil, tq_dil)
    swa_bias = _band_bias(slopes_d[np.array([0, 2, 1, 3])], tq_band, n_off_swa, _swa_mult
                          ).reshape(2, 2, n_off_swa, tq_band, tq_band)
    slopes_a_l2 = jnp.asarray(slopes_a * LOG2E, F32)
    dil_m0 = jnp.full((4,), NEG_INF, F32)

    x2d = x.reshape(t, D_MODEL)
    moe_bf16 = []
    for l in range(depth):
        w_main, wq_big, wkv, w4 = _prep_attn_weights(w_in[l], w_out[l], mla_w_uq[l], mla_w_ukv[l])
        a, avt, b, bvt, d, dvt, cq, ck, cvt = _proj(x2d, attn_norm[l][None], w_main, colscale, mla_q_norm[l][None],
                                                    wq_big, mla_kv_norm[l][None], wkv, tabs, seq)
        a3, b3, d3 = (v.reshape(bsz, seq, -1) for v in (a, b, d))
        sub128 = jnp.concatenate([diff_subln[l], diff_subln[l]])[None]
        o_a = _diff_attention(a3, avt, slopes_a_l2, causal_mask_t[:tq_diff, :tq_diff], alibi_kb, diff_lambda[l],
                              sub128, l, tq_diff)
        o_b = _band_attention(b3, bvt, 1, 2, dil_bias, dil_m0, 0.0, tq_dil, "dil_attn")
        o_c = _mla_attention(cq.reshape(bsz, seq, -1), ck.reshape(bsz, seq, -1), cvt, causal_mask_t, tq_dense)
        swa_m0 = (swa_sinks[l].astype(F32) * LOG2E)[np.array([0, 2, 1, 3])]
        o_d = _band_attention(d3, dvt, 2, 1, swa_bias, swa_m0, 1.0, tq_band, "swa_attn")
        outs = [v.reshape(t, 256) for v in (o_a, o_b, o_c, o_d)]
        j = l // 2
        if l % 2 == 0:
            x1, h2 = _outproj(*outs, w4, x2d, ffn_norm[l][None])
            nxt = [w[j].reshape(-1, w.shape[-1]) for w in (moe_w1, moe_w3, moe_w2)] if l + 1 < depth else []
            if any(a.shape[0] % (16 * (t // _pick(t, 512))) for a in nxt):
                nxt = []
            x2d, cast = _ffn(h2, ffn_w1[j].astype(BF16), ffn_w3[j].astype(BF16), ffn_w2[j].astype(BF16), x1, nxt)
            moe_bf16 = [c.reshape(moe_w.shape[1:]) for c, moe_w in zip(cast, (moe_w1, moe_w3, moe_w2))]
            moe = None
        else:
            r = jnp.pad(moe_router[j], ((0, 0), (0, LANES - N_EXPERTS)))
            r_hi, r_lo = _split_bf16(r, 2)
            x1, h2, logits = _outproj(*outs, w4, x2d, ffn_norm[l][None], (r_hi, r_lo))
            bm = _pick(t, 512)
            if not moe_bf16:
                moe_bf16 = [w[j].astype(BF16) for w in (moe_w1, moe_w3, moe_w2)]
            moe = _moe_layer(h2, logits, *moe_bf16, bm)
            x2d = x1
        if l < depth - 1 and moe is not None:
            raise NotImplementedError("MoE combine is fused into the final norm; MoE layer must be last")
    return _final(x2d, final_norm[None], moe).reshape(bsz, seq, D_MODEL)
```

```python
import functools
import math

import numpy as np
import jax
import jax.numpy as jnp
from jax import lax
from jax.experimental import pallas as pl
from jax.experimental.pallas import tpu as pltpu
from jax.experimental.pallas import tpu_sc as plsc

D_MODEL = 1024
DIFF_HEADS, DIFF_HD = 4, 32
DIL_HEADS, DIL_HD = 4, 64
DIL_PATTERNS = ((128, 1), (512, 4), (2048, 16))
MLA_HEADS, MLA_Q_LORA, MLA_KV_LORA, MLA_NOPE, MLA_ROPE, MLA_V = 4, 384, 128, 64, 32, 64
ROPE_THETA = 10000.0
SWA_HEADS, SWA_KV_HEADS, SWA_HD, SWA_WINDOW = 4, 2, 64, 128
D_FF = 2816
N_EXPERTS, TOP_K, D_FF_EXPERT = 8, 2, 3584
RMS_EPS = 1e-6
NEG_INF = -1e30
N_ALIBI = DIFF_HEADS + DIL_HEADS + SWA_HEADS

LOG2E = 1.4426950408889634
LANES = 128
HEAD_LANES = 64
VMEM_LIMIT = 56 * 1024 * 1024

F32 = jnp.float32
BF16 = jnp.bfloat16

_NT = (((1,), (1,)), ((), ()))


def _params(sem, vmem=VMEM_LIMIT):
    return pltpu.CompilerParams(dimension_semantics=sem, vmem_limit_bytes=vmem)


def _alibi_slopes():
    s = 2.0 ** (-8.0 * (np.arange(N_ALIBI) + 1) / N_ALIBI)
    return s[0::3], s[1::3], s[2::3]


def _pick(n, pref):
    t = min(pref, n)
    while n % t:
        t //= 2
    return t


N_SCALED = 2048
C_CQ, C_CKV, C_KPE, C_KPE_SW, N_MAIN = 2048, 2432, 2560, 2688, 2816


ONES_ROWS = 16
VT_ROWS = HEAD_LANES + ONES_ROWS


def _vt_with_ones(v):
    vt = v.T
    ones = jnp.ones((ONES_ROWS, v.shape[0]), F32)
    parts = []
    for hd in range(v.shape[1] // HEAD_LANES):
        parts += [vt[hd * HEAD_LANES:(hd + 1) * HEAD_LANES], ones]
    return jnp.concatenate(parts, axis=0).astype(BF16)


def _proj_body(x_ref, g_ref, w_ref, cs_ref, qn_ref, wq_ref, kn_ref, wkv_ref,
               cosq_ref, sinq_ref, cosk_ref, sink_ref,
               a_ref, avt_ref, b_ref, bvt_ref, d_ref, dvt_ref, cq_ref, ck_ref, cvt_ref):
    x = x_ref[...]
    ms = jnp.mean(x * x, axis=-1, keepdims=True)
    h = (x * lax.rsqrt(ms + RMS_EPS) * g_ref[...]).astype(BF16)
    acc = jnp.dot(h, w_ref[...], preferred_element_type=F32)
    sc = acc[:, :N_SCALED] * cs_ref[...]
    a_ref[...] = sc[:, 0:512].astype(BF16)
    avt_ref[...] = _vt_with_ones(sc[:, 512:768])
    b_ref[...] = sc[:, 768:1280].astype(BF16)
    bvt_ref[...] = _vt_with_ones(sc[:, 1280:1536])
    d_ref[...] = sc[:, 1536:1920].astype(BF16)
    dvt_ref[...] = _vt_with_ones(sc[:, 1920:2048])

    cq = acc[:, C_CQ:C_CKV]
    hq = (cq * lax.rsqrt(jnp.mean(cq * cq, axis=-1, keepdims=True) + RMS_EPS) * qn_ref[...]).astype(BF16)
    yq = jnp.dot(hq, wq_ref[...], preferred_element_type=F32)
    cosq, sinq = cosq_ref[...], sinq_ref[...]
    c_mla = (MLA_NOPE + MLA_ROPE) ** -0.5 * LOG2E
    for hd in range(MLA_HEADS):
        main = yq[:, hd * 256:(hd + 1) * 256]
        sw = yq[:, 1024 + hd * LANES:1024 + (hd + 1) * LANES]
        cq_ref[:, hd * 256:hd * 256 + LANES] = (main[:, :LANES] * cosq + sw * sinq).astype(BF16)
        cq_ref[:, hd * 256 + LANES:(hd + 1) * 256] = (main[:, LANES:] * c_mla).astype(BF16)

    ckv = acc[:, C_CKV:C_KPE]
    hk = (ckv * lax.rsqrt(jnp.mean(ckv * ckv, axis=-1, keepdims=True) + RMS_EPS) * kn_ref[...]).astype(BF16)
    kv = jnp.dot(hk, wkv_ref[...], preferred_element_type=F32)
    kr = (acc[:, C_KPE:C_KPE_SW] * cosk_ref[...] + acc[:, C_KPE_SW:N_MAIN] * sink_ref[...]).astype(BF16)
    for grp in range(2):
        ck_ref[:, grp * 256:grp * 256 + LANES] = kr
        ck_ref[:, grp * 256 + LANES:(grp + 1) * 256] = kv[:, grp * LANES:(grp + 1) * LANES].astype(BF16)
    cvt_ref[...] = _vt_with_ones(kv[:, 256:512])


def _proj(x2d, g, w_main, colscale, qn, wq_big, kn, wkv, tabs, seq):
    t = x2d.shape[0]
    tm = _pick(seq, 512)
    nsb = seq // tm
    full = lambda shape: pl.BlockSpec(shape, lambda i: (0,) * len(shape))
    tab = pl.BlockSpec((tm, LANES), lambda i: (i % nsb, 0))
    row = lambda w: pl.BlockSpec((tm, w), lambda i: (i, 0))
    vt = lambda w: pl.BlockSpec((None, w, tm), lambda i: (i // nsb, 0, i % nsb))
    row_sds = lambda w: jax.ShapeDtypeStruct((t, w), BF16)
    vt_sds = lambda w: jax.ShapeDtypeStruct((t // seq, w, seq), BF16)
    return pl.pallas_call(
        _proj_body,
        grid=(t // tm,),
        in_specs=[row(D_MODEL), full((1, D_MODEL)), full((D_MODEL, N_MAIN)), full((1, N_SCALED)),
                  full((1, MLA_Q_LORA)), full((MLA_Q_LORA, 1536)), full((1, MLA_KV_LORA)),
                  full((MLA_KV_LORA, 512)), tab, tab, tab, tab],
        out_specs=[row(512), vt(4 * VT_ROWS), row(512), vt(4 * VT_ROWS), row(384), vt(2 * VT_ROWS),
                   row(1024), row(512), vt(4 * VT_ROWS)],
        out_shape=[row_sds(512), vt_sds(4 * VT_ROWS), row_sds(512), vt_sds(4 * VT_ROWS), row_sds(384),
                   vt_sds(2 * VT_ROWS), row_sds(1024), row_sds(512), vt_sds(4 * VT_ROWS)],
        compiler_params=_params(("parallel",)),
        name="proj",
    )(x2d, g, w_main, colscale, qn, wq_big, kn, wkv, *tabs)


def _attn_pipeline(n_soft, n_steps, scores, shift, load_vt, m0, s_ref, acc_ref):
    def stage_a(t, ss, dst_ref, m_cur):
        m_next, alpha = [], []
        for i in range(n_soft):
            m_new = jnp.maximum(m_cur[i], jnp.max(ss[i], axis=0, keepdims=True) + shift(i, t))
            dst_ref[i] = ss[i]
            m_next.append(m_new)
            alpha.append(jnp.exp2(m_cur[i] - m_new))
        return tuple(m_next), tuple(alpha)

    def stage_b(t, src_ref, carry):
        m_cur, alpha = carry
        for i in range(n_soft):
            p = jnp.exp2(src_ref[i] - (m_cur[i] - shift(i, t)))
            acc_ref[i] = alpha[i] * acc_ref[i] + jnp.dot(load_vt(t, i), p.astype(BF16),
                                                         preferred_element_type=F32)

    def half(t, src_ref, dst_ref, carry):
        ss = [scores(i, t + 1, False) for i in range(n_soft)]
        stage_b(t, src_ref, carry)
        return stage_a(t + 1, ss, dst_ref, carry[0])

    def body(u, carry):
        carry = half(2 * u, s_ref[0], s_ref[1], carry)
        return half(2 * u + 1, s_ref[1], s_ref[0], carry)

    carry = stage_a(0, [scores(i, 0, True) for i in range(n_soft)], s_ref[0], m0)
    carry = lax.fori_loop(0, n_steps // 2, body, carry)

    @pl.when(n_steps % 2 == 1)
    def _():
        stage_b(n_steps, s_ref[1], half(n_steps - 1, s_ref[0], s_ref[1], carry))

    @pl.when(n_steps % 2 == 0)
    def _():
        stage_b(n_steps, s_ref[0], carry)


def _row_mask(shape, lo, hi):
    row = lax.broadcasted_iota(jnp.int32, shape, 0)
    return jnp.logical_and(row >= lo, row < hi)


def _lane_mask(shape, lo, hi):
    lane = lax.broadcasted_iota(jnp.int32, shape, len(shape) - 1)
    return jnp.logical_and(lane >= lo, lane < hi)


def _normalized(acc_ref, i):
    return acc_ref[i, 0:HEAD_LANES, :] / acc_ref[i, HEAD_LANES:HEAD_LANES + 1, :]


def _diff_body(sl_ref, q_ref, k_ref, vt_ref, mask_ref, kb_ref, lam_ref, sub_ref, o_ref,
               qs_ref, s0_ref, s1_ref, acc_ref, *, tq, lam_init):
    qi = pl.program_id(1)
    tk = tq
    n_soft = 2 * DIFF_HEADS
    head_of = lambda i: i // 2
    for pair in range(2):
        q = q_ref[:, pair * LANES:(pair + 1) * LANES].astype(F32)
        for hd in range(2):
            sel = jnp.where(_lane_mask(q.shape, N_BIAS_SPLIT * hd, N_BIAS_SPLIT * (hd + 1)), 1.0, 0.0).astype(BF16)
            for mp in range(2):
                i = 4 * pair + 2 * hd + mp
                lo = hd * HEAD_LANES + mp * DIFF_HD
                qs_ref[i, :, 0:LANES] = jnp.where(_lane_mask(q.shape, lo, lo + DIFF_HD), q, 0.0).astype(BF16)
                qs_ref[i, :, LANES:2 * LANES] = sel
    acc_ref[...] = jnp.zeros(acc_ref.shape, F32)

    def scores(i, t, first):
        off = pl.multiple_of((qi - t) * tk, tk)
        pair = i // 4
        k = jnp.concatenate([k_ref[pl.ds(off, tk), pair * LANES:(pair + 1) * LANES], kb_ref[pair]], axis=1)
        s = lax.dot_general(k, qs_ref[i], _NT, preferred_element_type=F32)
        return s + mask_ref[...] if first else s

    def shift(i, t):
        return sl_ref[head_of(i)] * (-(t * tk)).astype(F32) if not isinstance(t, int) else 0.0

    def load_vt(t, i):
        return vt_ref[pl.ds(head_of(i) * VT_ROWS, VT_ROWS), pl.ds(pl.multiple_of((qi - t) * tk, tk), tk)]

    m0 = tuple(jnp.full((1, tq), NEG_INF, F32) for _ in range(n_soft))
    _attn_pipeline(n_soft, qi, scores, shift, load_vt, m0, (s0_ref, s1_ref), acc_ref)

    lp = lam_ref[...]
    lam = (jnp.exp(jnp.sum(lp[0:1, :] * lp[1:2, :], axis=-1, keepdims=True))
           - jnp.exp(jnp.sum(lp[2:3, :] * lp[3:4, :], axis=-1, keepdims=True)) + lam_init)
    in_h0 = _lane_mask((tq, LANES), 0, HEAD_LANES)
    for pair in range(2):
        o_h = [_normalized(acc_ref, 4 * pair + 2 * hd) - lam * _normalized(acc_ref, 4 * pair + 2 * hd + 1)
               for hd in range(2)]
        o = jnp.concatenate(o_h, axis=0).T
        sq = o * o
        ms0 = jnp.sum(jnp.where(in_h0, sq, 0.0), axis=-1, keepdims=True)
        ms1 = jnp.sum(jnp.where(in_h0, 0.0, sq), axis=-1, keepdims=True)
        ms = jnp.where(in_h0, ms0, ms1) * (1.0 / HEAD_LANES)
        y = o * lax.rsqrt(ms + RMS_EPS) * sub_ref[...]
        o_ref[:, pair * LANES:(pair + 1) * LANES] = (y * (1.0 - lam_init)).astype(o_ref.dtype)


N_BIAS_SPLIT = 3


def _alibi_key_columns(slopes_l2, tk):
    dj = jnp.arange(tk, dtype=F32)[None, :, None]
    val = jnp.asarray(slopes_l2, F32).reshape(-1, 1, 2) * dj
    cols = jnp.stack(_split_bf16(val, N_BIAS_SPLIT), axis=-1).reshape(val.shape[0], tk, 2 * N_BIAS_SPLIT)
    return jnp.pad(cols, ((0, 0), (0, 0), (0, LANES - 2 * N_BIAS_SPLIT)))


def _split_bf16(x, n):
    pieces = []
    for _ in range(n):
        bits = lax.bitcast_convert_type(x, jnp.uint32) & jnp.uint32(0xFFFF0000)
        head = lax.bitcast_convert_type(bits, F32)
        pieces.append(head.astype(BF16))
        x = x - head
    return pieces


def _diff_attention(a3, avt, slopes_l2, mask_t, kb, lam_p, sub128, layer_idx, tq):
    b, s, _ = a3.shape
    nq = s // tq
    lam_init = 0.8 - 0.6 * math.exp(-0.3 * layer_idx)
    body = functools.partial(_diff_body, tq=tq, lam_init=lam_init)
    grid_spec = pltpu.PrefetchScalarGridSpec(
        num_scalar_prefetch=1,
        grid=(b, nq),
        in_specs=[
            pl.BlockSpec((None, tq, 2 * LANES), lambda bi, qi, sl: (bi, qi, 0)),
            pl.BlockSpec((None, s, 2 * LANES), lambda bi, qi, sl: (bi, 0, 1)),
            pl.BlockSpec((None, DIFF_HEADS * VT_ROWS, s), lambda bi, qi, sl: (bi, 0, 0)),
            pl.BlockSpec((tq, tq), lambda bi, qi, sl: (0, 0), pipeline_mode=pl.Buffered(1)),
            pl.BlockSpec((2, tq, LANES), lambda bi, qi, sl: (0, 0, 0), pipeline_mode=pl.Buffered(1)),
            pl.BlockSpec((4, DIFF_HD), lambda bi, qi, sl: (0, 0)),
            pl.BlockSpec((1, LANES), lambda bi, qi, sl: (0, 0)),
        ],
        out_specs=pl.BlockSpec((None, tq, 2 * LANES), lambda bi, qi, sl: (bi, qi, 0)),
        scratch_shapes=[pltpu.VMEM((2 * DIFF_HEADS, tq, 2 * LANES), BF16),
                        pltpu.VMEM((2 * DIFF_HEADS, tq, tq), F32), pltpu.VMEM((2 * DIFF_HEADS, tq, tq), F32),
                        pltpu.VMEM((2 * DIFF_HEADS, VT_ROWS, tq), F32)],
    )
    return pl.pallas_call(
        body, grid_spec=grid_spec,
        out_shape=jax.ShapeDtypeStruct((b, s, 256), BF16),
        compiler_params=_params(("parallel", "arbitrary")),
        name="diff_attn",
    )(slopes_l2, a3, a3, avt, mask_t, kb, lam_p, sub128)


def _mla_body(q_ref, k_ref, vt_ref, mask_ref, o_ref, s0_ref, s1_ref, acc_ref, *, tq):
    qi = pl.program_id(1)
    tk = tq
    acc_ref[...] = jnp.zeros(acc_ref.shape, F32)

    def scores(i, t, first):
        k = k_ref[pl.ds(pl.multiple_of((qi - t) * tk, tk), tk), (i // 2) * 256:(i // 2 + 1) * 256]
        s = lax.dot_general(k, q_ref[:, i * 256:(i + 1) * 256], _NT, preferred_element_type=F32)
        return s + mask_ref[...] if first else s

    def load_vt(t, i):
        return vt_ref[pl.ds(i * VT_ROWS, VT_ROWS), pl.ds(pl.multiple_of((qi - t) * tk, tk), tk)]

    m0 = tuple(jnp.full((1, tq), NEG_INF, F32) for _ in range(MLA_HEADS))
    _attn_pipeline(MLA_HEADS, qi, scores, lambda i, t: 0.0, load_vt, m0, (s0_ref, s1_ref), acc_ref)
    o_t = jnp.concatenate([_normalized(acc_ref, i) for i in range(MLA_HEADS)], axis=0)
    o_ref[...] = o_t.T.astype(o_ref.dtype)


def _mla_attention(cq3, ck3, cvt, mask_t, tq):
    b, s, _ = cq3.shape
    nq = s // tq
    return pl.pallas_call(
        functools.partial(_mla_body, tq=tq),
        grid=(b, nq),
        in_specs=[
            pl.BlockSpec((None, tq, MLA_HEADS * 256), lambda bi, qi: (bi, qi, 0)),
            pl.BlockSpec((None, s, 512), lambda bi, qi: (bi, 0, 0)),
            pl.BlockSpec((None, MLA_HEADS * VT_ROWS, s), lambda bi, qi: (bi, 0, 0)),
            pl.BlockSpec((tq, tq), lambda bi, qi: (0, 0)),
        ],
        out_specs=pl.BlockSpec((None, tq, 256), lambda bi, qi: (bi, qi, 0)),
        out_shape=jax.ShapeDtypeStruct((b, s, 256), BF16),
        scratch_shapes=[pltpu.VMEM((MLA_HEADS, tq, tq), F32), pltpu.VMEM((MLA_HEADS, tq, tq), F32),
                        pltpu.VMEM((MLA_HEADS, VT_ROWS, tq), F32)],
        compiler_params=_params(("parallel", "arbitrary")),
        name="mla_attn",
    )(cq3, ck3, cvt, mask_t)


def _band_body(m0_ref, q_ref, k_ref, vt_ref, bias_ref, o_ref, qs_ref, s0_ref, s1_ref, acc_ref,
               *, tq, n_off, l_init, kv_slabs):
    qi = pl.program_id(1)
    tk = tq
    n_soft = 4
    for sl in range(2):
        q = q_ref[:, sl * LANES:(sl + 1) * LANES].astype(F32)
        for g in range(2):
            qs_ref[2 * sl + g] = jnp.where(_lane_mask(q.shape, g * HEAD_LANES, (g + 1) * HEAD_LANES), q, 0.0
                                           ).astype(BF16)
    for i in range(n_soft):
        acc_ref[i] = jnp.where(_row_mask((VT_ROWS, tq), 0, HEAD_LANES), 0.0, l_init).astype(F32)

    kv_of = lambda i: (i // 2) * (kv_slabs - 1)

    def scores(i, t, first):
        k = k_ref[pl.ds(pl.multiple_of((qi - t) * tk, tk), tk), kv_of(i) * LANES:(kv_of(i) + 1) * LANES]
        return lax.dot_general(k, qs_ref[i], _NT, preferred_element_type=F32) + bias_ref[i // 2, i % 2, t]

    def load_vt(t, i):
        return vt_ref[pl.ds((2 * kv_of(i) + i % 2) * VT_ROWS, VT_ROWS),
                      pl.ds(pl.multiple_of((qi - t) * tk, tk), tk)]

    m0 = tuple(jnp.full((1, tq), m0_ref[i], F32) for i in range(n_soft))
    _attn_pipeline(n_soft, jnp.minimum(qi, n_off - 1), scores, lambda i, t: 0.0, load_vt, m0,
                   (s0_ref, s1_ref), acc_ref)
    o_t = jnp.concatenate([_normalized(acc_ref, i) for i in range(n_soft)], axis=0)
    o_ref[...] = o_t.T.astype(o_ref.dtype)


def _band_attention(x3, xvt, k_blk, kv_slabs, bias, m0, l_init, tq, name):
    b, s, _ = x3.shape
    nq = s // tq
    n_off = bias.shape[2]
    n_soft = 4
    grid_spec = pltpu.PrefetchScalarGridSpec(
        num_scalar_prefetch=1,
        grid=(b, nq),
        in_specs=[
            pl.BlockSpec((None, tq, 2 * LANES), lambda bi, qi, m0r: (bi, qi, 0)),
            pl.BlockSpec((None, s, kv_slabs * LANES), lambda bi, qi, m0r: (bi, 0, k_blk)),
            pl.BlockSpec((None, kv_slabs * 2 * VT_ROWS, s), lambda bi, qi, m0r: (bi, 0, 0)),
            pl.BlockSpec((2, 2, n_off, tq, tq), lambda bi, qi, m0r: (0, 0, 0, 0, 0),
                         pipeline_mode=pl.Buffered(1)),
        ],
        out_specs=pl.BlockSpec((None, tq, 2 * LANES), lambda bi, qi, m0r: (bi, qi, 0)),
        scratch_shapes=[pltpu.VMEM((n_soft, tq, LANES), BF16), pltpu.VMEM((n_soft, tq, tq), F32),
                        pltpu.VMEM((n_soft, tq, tq), F32), pltpu.VMEM((n_soft, VT_ROWS, tq), F32)],
    )
    return pl.pallas_call(
        functools.partial(_band_body, tq=tq, n_off=n_off, l_init=l_init, kv_slabs=kv_slabs),
        grid_spec=grid_spec,
        out_shape=jax.ShapeDtypeStruct((b, s, 256), BF16),
        compiler_params=_params(("parallel", "arbitrary")),
        name=name,
    )(m0, x3, x3, xvt, bias)


def _band_bias(slopes, tq, n_off, mult_fn):
    di = jnp.arange(tq, dtype=jnp.int32)[None, :]
    dj = jnp.arange(tq, dtype=jnp.int32)[:, None]
    off = jnp.arange(n_off, dtype=jnp.int32)[:, None, None]
    delta = off * tq + di - dj
    mult = mult_fn(delta)
    sl = jnp.asarray(slopes, F32)[:, None, None, None] * LOG2E
    val = -sl * delta.astype(F32)[None] + jnp.log2(jnp.maximum(mult, 1).astype(F32))[None]
    return jnp.where((mult > 0)[None], val, NEG_INF)


def _dil_mult(delta):
    m = jnp.zeros(delta.shape, jnp.int32)
    for window, d in DIL_PATTERNS:
        m = m + ((delta >= 0) & (delta <= window) & (delta % d == 0)).astype(jnp.int32)
    return m


def _swa_mult(delta):
    return ((delta >= 0) & (delta <= SWA_WINDOW - 1)).astype(jnp.int32)


def _outproj_body(*refs, with_router):
    if with_router:
        (oa, ob, oc, od, w_ref, x_ref, g_ref, rh_ref, rl_ref, x1_ref, h2_ref, lg_ref) = refs
    else:
        (oa, ob, oc, od, w_ref, x_ref, g_ref, x1_ref, h2_ref) = refs
    acc = x_ref[...]
    for i, o in enumerate((oa, ob, oc, od)):
        acc = acc + jnp.dot(o[...], w_ref[i], preferred_element_type=F32)
    x1_ref[...] = acc
    h2 = acc * lax.rsqrt(jnp.mean(acc * acc, axis=-1, keepdims=True) + RMS_EPS) * g_ref[...]
    hi = h2.astype(BF16)
    if with_router:
        for c, words in enumerate(_pack_rows(h2)):
            h2_ref[c] = words
        lo = (h2 - hi.astype(F32)).astype(BF16)
        hh = jnp.dot(hi, jnp.concatenate([rh_ref[...], rl_ref[...]], axis=1), preferred_element_type=F32)
        lg_ref[...] = hh[:, :LANES] + (hh[:, LANES:] + jnp.dot(lo, rh_ref[...], preferred_element_type=F32))
    else:
        h2_ref[...] = hi


def _outproj(oa, ob, oc, od, w4, x2d, g, router_hl=None):
    t = x2d.shape[0]
    tm = _pick(t, 512)
    with_router = router_hl is not None
    full = lambda shape: pl.BlockSpec(shape, lambda i: (0,) * len(shape))
    row = lambda w: pl.BlockSpec((tm, w), lambda i: (i, 0))
    in_specs = [row(256)] * 4 + [full((4, 256, D_MODEL)), row(D_MODEL), full((1, D_MODEL))]
    args = [oa, ob, oc, od, w4, x2d, g]
    out_specs = [row(D_MODEL), row(D_MODEL)]
    out_shape = [jax.ShapeDtypeStruct((t, D_MODEL), F32), jax.ShapeDtypeStruct((t, D_MODEL), BF16)]
    if with_router:
        out_specs[1] = pl.BlockSpec((SC_CHUNKS, tm, SC_COLS), lambda i: (0, i, 0))
        out_shape[1] = jax.ShapeDtypeStruct((SC_CHUNKS, t, SC_COLS), U32)
        in_specs += [full((D_MODEL, LANES))] * 2
        args += list(router_hl)
        out_specs.append(row(LANES))
        out_shape.append(jax.ShapeDtypeStruct((t, LANES), F32))
    return pl.pallas_call(
        functools.partial(_outproj_body, with_router=with_router),
        grid=(t // tm,), in_specs=in_specs, out_specs=out_specs, out_shape=out_shape,
        compiler_params=_params(("parallel",)),
        name="outproj",
    )(*args)


FF_CHUNK = 512


def _swiglu(x, w1_ref, w3_ref, w2_ref):
    tf = w1_ref.shape[1]
    total = None
    for lo in range(0, tf, FF_CHUNK):
        hi = min(lo + FF_CHUNK, tf)
        a = jnp.dot(x, w1_ref[:, lo:hi], preferred_element_type=F32)
        b = jnp.dot(x, w3_ref[:, lo:hi], preferred_element_type=F32)
        mid = (a * jax.nn.sigmoid(a) * b).astype(BF16)
        part = jnp.dot(mid, w2_ref[lo:hi, :], preferred_element_type=F32)
        total = part if total is None else total + part
    return total


def _ffn_body(*refs, n_cast):
    h_ref, w1_ref, w3_ref, w2_ref, x_ref = refs[:5]
    cast_in, o_ref, cast_out = refs[5:5 + n_cast], refs[5 + n_cast], refs[6 + n_cast:]
    o_ref[...] = x_ref[...] + _swiglu(h_ref[...], w1_ref, w3_ref, w2_ref)
    for src, dst in zip(cast_in, cast_out):
        dst[...] = src[...].astype(BF16)


def _ffn(h2, w1, w3, w2, x1, to_cast=()):
    t = h2.shape[0]
    tm = _pick(t, 512)
    n_steps = t // tm
    resident = lambda shape: pl.BlockSpec(shape, lambda i: (0, 0), pipeline_mode=pl.Buffered(1))
    sliced = [pl.BlockSpec((a.shape[0] // n_steps, a.shape[1]), lambda i: (i, 0)) for a in to_cast]
    outs = pl.pallas_call(
        functools.partial(_ffn_body, n_cast=len(to_cast)),
        grid=(n_steps,),
        in_specs=[pl.BlockSpec((tm, D_MODEL), lambda i: (i, 0)),
                  resident((D_MODEL, D_FF)), resident((D_MODEL, D_FF)), resident((D_FF, D_MODEL)),
                  pl.BlockSpec((tm, D_MODEL), lambda i: (i, 0))] + sliced,
        out_specs=[pl.BlockSpec((tm, D_MODEL), lambda i: (i, 0))] + sliced,
        out_shape=[jax.ShapeDtypeStruct((t, D_MODEL), F32)]
        + [jax.ShapeDtypeStruct(a.shape, BF16) for a in to_cast],
        compiler_params=_params(("parallel",)),
        name="ffn",
    )(h2, w1, w3, w2, x1, *to_cast)
    return outs[0], outs[1:]


def _route_body(lg_ref, tri_ref, info_ref, cnt_ref, carry_ref):
    i = pl.program_id(0)

    @pl.when(i == 0)
    def _():
        carry_ref[...] = jnp.zeros(carry_ref.shape, F32)

    lg = lg_ref[...]
    lane = lax.broadcasted_iota(jnp.int32, lg.shape, 1)
    lg = jnp.where(lane < N_EXPERTS, lg, -jnp.inf)
    m1 = jnp.max(lg, axis=-1, keepdims=True)
    i1 = jnp.min(jnp.where(lg == m1, lane, LANES), axis=-1, keepdims=True)
    oh1 = lane == i1
    lg2 = jnp.where(oh1, -jnp.inf, lg)
    m2 = jnp.max(lg2, axis=-1, keepdims=True)
    i2 = jnp.min(jnp.where(lg2 == m2, lane, LANES), axis=-1, keepdims=True)
    oh2 = lane == i2
    e2 = jnp.exp(m2 - m1)
    g1 = 1.0 / (1.0 + e2)
    g2 = e2 / (1.0 + e2)
    oh = jnp.where(jnp.logical_or(oh1, oh2), 1.0, 0.0)
    rank = jnp.dot(tri_ref[...], oh.astype(BF16), preferred_element_type=F32) + carry_ref[...]
    r1 = jnp.sum(jnp.where(oh1, rank, 0.0), axis=-1, keepdims=True)
    r2 = jnp.sum(jnp.where(oh2, rank, 0.0), axis=-1, keepdims=True)
    carry_ref[...] += jnp.sum(oh, axis=0, keepdims=True)
    cnt_ref[...] = carry_ref[...]
    info = jnp.zeros(lg.shape, F32)
    for c, val in enumerate((i1.astype(F32), i2.astype(F32), r1, r2, g1, g2)):
        info = jnp.where(lane == c, val, info)
    info_ref[...] = info


def _route(logits):
    t = logits.shape[0]
    tm = _pick(t, 512)
    tri = (jnp.arange(tm)[:, None] > jnp.arange(tm)[None, :]).astype(BF16)
    return pl.pallas_call(
        _route_body,
        grid=(t // tm,),
        in_specs=[pl.BlockSpec((tm, LANES), lambda i: (i, 0)), pl.BlockSpec((tm, tm), lambda i: (0, 0))],
        out_specs=[pl.BlockSpec((tm, LANES), lambda i: (i, 0)), pl.BlockSpec((1, LANES), lambda i: (0, 0))],
        out_shape=[jax.ShapeDtypeStruct((t, LANES), F32), jax.ShapeDtypeStruct((1, LANES), F32)],
        scratch_shapes=[pltpu.VMEM((1, LANES), F32)],
        compiler_params=_params(("arbitrary",)),
        name="route",
    )(logits, tri)


def _moe_ffn_body(be_ref, bv_ref, x_ref, w1_ref, w3_ref, w2_ref, o_ref):
    valid = bv_ref[pl.program_id(0)]

    @pl.when(valid > 0)
    def _():
        x = _unpack_rows([x_ref[c] for c in range(SC_CHUNKS)])
        x = jnp.where(lax.broadcasted_iota(jnp.int32, x.shape, 0) < valid, x, 0.0).astype(BF16)
        for c, words in enumerate(_pack_rows(_swiglu(x, w1_ref, w3_ref, w2_ref))):
            o_ref[c] = words

    @pl.when(valid == 0)
    def _():
        o_ref[...] = jnp.zeros(o_ref.shape, o_ref.dtype)


def _moe_ffn(xs, blk_expert, blk_valid, w1, w3, w2, bm):
    rows = xs.shape[1]
    chunked = pl.BlockSpec((SC_CHUNKS, bm, SC_COLS), lambda i, be, bv: (0, i, 0))
    expert = lambda shape: pl.BlockSpec((None,) + shape, lambda i, be, bv: (be[i], 0, 0),
                                        pipeline_mode=pl.Buffered(1))
    grid_spec = pltpu.PrefetchScalarGridSpec(
        num_scalar_prefetch=2,
        grid=(rows // bm,),
        in_specs=[chunked, expert((D_MODEL, D_FF_EXPERT)), expert((D_MODEL, D_FF_EXPERT)),
                  expert((D_FF_EXPERT, D_MODEL))],
        out_specs=chunked,
    )
    return pl.pallas_call(
        _moe_ffn_body, grid_spec=grid_spec,
        out_shape=jax.ShapeDtypeStruct(xs.shape, U32),
        compiler_params=_params(("arbitrary",)),
        name="moe_ffn",
    )(blk_expert, blk_valid, xs, w1, w3, w2)


def _final_body(*refs, with_moe):
    if with_moe:
        x_ref, y0_ref, y1_ref, info_ref, g_ref, o_ref = refs
        info = info_ref[...]
        y0 = _unpack_rows([y0_ref[c] for c in range(SC_CHUNKS)])
        y1 = _unpack_rows([y1_ref[c] for c in range(SC_CHUNKS)])
        x = x_ref[...] + (info[:, 4:5] * y0 + info[:, 5:6] * y1)
    else:
        x_ref, g_ref, o_ref = refs
        x = x_ref[...]
    o_ref[...] = x * lax.rsqrt(jnp.mean(x * x, axis=-1, keepdims=True) + RMS_EPS) * g_ref[...]


def _final(x2d, g, moe=None):
    t = x2d.shape[0]
    tm = _pick(t, 512)
    row = lambda w: pl.BlockSpec((tm, w), lambda i: (i, 0))
    in_specs, args = [row(D_MODEL)], [x2d]
    if moe is not None:
        yg, info = moe
        slot = lambda k: pl.BlockSpec((None, SC_CHUNKS, tm, SC_COLS), lambda i: (k, 0, i, 0))
        in_specs += [slot(0), slot(1), row(LANES)]
        args += [yg, yg, info]
    in_specs.append(pl.BlockSpec((1, D_MODEL), lambda i: (0, 0)))
    args.append(g)
    return pl.pallas_call(
        functools.partial(_final_body, with_moe=moe is not None),
        grid=(t // tm,), in_specs=in_specs, out_specs=row(D_MODEL),
        out_shape=jax.ShapeDtypeStruct((t, D_MODEL), F32),
        compiler_params=_params(("parallel",)),
        name="final",
    )(*args)


def _prep_attn_weights(w_in, w_out, w_uq, w_ukv):
    widths = (256, 256, 256, 256, 256, 256, MLA_Q_LORA, MLA_KV_LORA, MLA_ROPE, 256, 128, 128)
    offs = np.concatenate([[0], np.cumsum(widths)])
    qa, ka, va, qb, kb, vb, cq, ckv, kpe, qd, kd, vd = [w_in[:, offs[i]:offs[i + 1]] for i in range(12)]
    hperm = np.array([0, 2, 1, 3])
    qd = qd.reshape(D_MODEL, 4, SWA_HD)[:, hperm].reshape(D_MODEL, 256)
    pad = jnp.zeros((D_MODEL, LANES - MLA_ROPE), w_in.dtype)
    half = MLA_ROPE // 2
    kpe_sw = jnp.concatenate([kpe[:, half:], kpe[:, :half]], axis=1)
    w_main = jnp.concatenate([qa, ka, va, qb, kb, vb, qd, kd, vd, cq, ckv, kpe, pad, kpe_sw, pad], axis=1)

    uq = w_uq.reshape(MLA_Q_LORA, MLA_HEADS, MLA_NOPE + MLA_ROPE)
    z = lambda n: jnp.zeros((MLA_Q_LORA, n), w_uq.dtype)
    mains, sws = [], []
    for hd in range(MLA_HEADS):
        nope, rope = uq[:, hd, :MLA_NOPE], uq[:, hd, MLA_NOPE:]
        nope128 = jnp.concatenate([nope, z(64)] if hd % 2 == 0 else [z(64), nope], axis=1)
        mains.append(jnp.concatenate([rope, z(LANES - MLA_ROPE), nope128], axis=1))
        sws.append(jnp.concatenate([rope[:, half:], rope[:, :half], z(LANES - MLA_ROPE)], axis=1))
    wq_big = jnp.concatenate(mains + sws, axis=1)

    ukv = w_ukv.reshape(MLA_KV_LORA, MLA_HEADS, MLA_NOPE + MLA_V)
    wkv = jnp.concatenate([ukv[:, :, :MLA_NOPE].reshape(MLA_KV_LORA, 256),
                           ukv[:, :, MLA_NOPE:].reshape(MLA_KV_LORA, 256)], axis=1)

    wo = w_out.reshape(4, 256, D_MODEL)
    wo_d = wo[3].reshape(4, SWA_HD, D_MODEL)[hperm].reshape(256, D_MODEL)
    w4 = jnp.stack([wo[0], wo[1], wo[2], wo_d])
    return w_main.astype(BF16), wq_big.astype(BF16), wkv.astype(BF16), w4.astype(BF16)


def _rope_tables(s):
    inv = ROPE_THETA ** (-jnp.arange(0, MLA_ROPE, 2, dtype=F32) / MLA_ROPE)
    ang = jnp.arange(s, dtype=F32)[:, None] * inv[None, :]
    cos, sin = jnp.cos(ang), jnp.sin(ang)
    z = jnp.zeros((s, LANES - MLA_ROPE), F32)
    cos128 = jnp.concatenate([cos, cos, z], axis=1)
    sin128 = jnp.concatenate([-sin, sin, z], axis=1)
    c_mla = (MLA_NOPE + MLA_ROPE) ** -0.5 * LOG2E
    return cos128 * c_mla, sin128 * c_mla, cos128, sin128


def _col_scale():
    ca = DIFF_HD ** -0.5 * LOG2E
    cb = DIL_HD ** -0.5 * LOG2E
    cd = SWA_HD ** -0.5 * LOG2E
    v = np.ones((1, N_SCALED), np.float32)
    v[0, 0:256] = ca
    v[0, 768:1024] = cb
    v[0, 1536:1792] = cd
    return jnp.asarray(v)


def _moe_layer(h2, logits, w1, w3, w2, bm):
    t = logits.shape[0]
    info, counts = _route(logits)
    cnt = counts[0, :N_EXPERTS].astype(jnp.int32)
    padded = (cnt + bm - 1) // bm * bm
    pends = jnp.cumsum(padded)
    pstarts = pends - padded
    e12 = info[:, 0:2].astype(jnp.int32)
    pos = pstarts[e12] + info[:, 2:4].astype(jnp.int32)
    nb = (t * TOP_K) // bm + N_EXPERTS
    rows = nb * bm
    blk_start = jnp.arange(nb, dtype=jnp.int32) * bm
    blk_expert = jnp.minimum(jnp.sum((pends[None, :] <= blk_start[:, None]).astype(jnp.int32), axis=1),
                             N_EXPERTS - 1)
    blk_valid = jnp.clip(pstarts[blk_expert] + cnt[blk_expert] - blk_start, 0, bm).astype(jnp.int32)
    sub = (jnp.arange(SC_CHUNKS, dtype=jnp.int32) * rows)[None, :, None] + pos.T[:, None, :]
    xs = _sc_scatter_rows(h2.reshape(SC_CHUNKS * t, SC_COLS),
                          [sub[k].reshape(1, SC_CHUNKS * t) for k in range(TOP_K)], SC_CHUNKS * rows)
    ys = _moe_ffn(xs.reshape(SC_CHUNKS, rows, SC_COLS), blk_expert.astype(jnp.int32), blk_valid, w1, w3, w2, bm)
    yg = _sc_gather_rows(ys.reshape(SC_CHUNKS * rows, SC_COLS), sub.reshape(1, TOP_K * SC_CHUNKS * t))
    return yg.reshape(TOP_K, SC_CHUNKS, t, SC_COLS), info


SC_WINDOW = 128
SC_COLS = 256
SC_CHUNKS = D_MODEL // (2 * SC_COLS)
U32 = jnp.uint32


def _bf16_bits(v):
    bits = lax.bitcast_convert_type(v, U32)
    return bits + jnp.uint32(0x7FFF) + ((bits >> 16) & jnp.uint32(1))


def _pack_rows(v):
    out = []
    for c in range(SC_CHUNKS):
        lo = _bf16_bits(v[:, 2 * c * SC_COLS:(2 * c + 1) * SC_COLS]) >> 16
        hi = _bf16_bits(v[:, (2 * c + 1) * SC_COLS:(2 * c + 2) * SC_COLS]) & jnp.uint32(0xFFFF0000)
        out.append(lo | hi)
    return out


def _unpack_rows(chunks):
    cols = []
    for u in chunks:
        cols.append(lax.bitcast_convert_type(u << 16, F32))
        cols.append(lax.bitcast_convert_type(u & jnp.uint32(0xFFFF0000), F32))
    return jnp.concatenate(cols, axis=1)


def _sc_mesh():
    return plsc.VectorSubcoreMesh(core_axis_name="c", subcore_axis_name="s")


def _sc_scatter_rows(x, idx_list, rows):
    t, d = x.shape
    n = len(idx_list)

    @functools.partial(pl.kernel, out_type=jax.ShapeDtypeStruct((rows, d), x.dtype), mesh=_sc_mesh(),
                       scratch_types=[])
    def scatter_kernel(x_hbm, *rest):
        idx_hbms, o_hbm = rest[:n], rest[n]

        def body(x_vmem, *idx_vmems):
            for iv in idx_vmems:
                pltpu.sync_copy(x_vmem, o_hbm.at[iv.at[0]])

        pltpu.emit_pipeline(
            body, grid=(t // SC_WINDOW,),
            in_specs=[pl.BlockSpec((SC_WINDOW, d), lambda i: (i, 0))]
            + [pl.BlockSpec((1, SC_WINDOW), lambda i: (0, i))] * n,
            out_specs=[], core_axis_name=("c", "s"), dimension_semantics=(pltpu.PARALLEL,),
        )(x_hbm, *idx_hbms)

    return scatter_kernel(x, *idx_list)


def _sc_gather_rows(y, idx):
    m = idx.shape[1]
    d = y.shape[1]

    @functools.partial(pl.kernel, out_type=jax.ShapeDtypeStruct((m, d), y.dtype), mesh=_sc_mesh(),
                       scratch_types=[])
    def gather_kernel(y_hbm, idx_hbm, o_hbm):
        def body(idx_vmem, o_vmem):
            pltpu.sync_copy(y_hbm.at[idx_vmem.at[0]], o_vmem)

        pltpu.emit_pipeline(
            body, grid=(m // SC_WINDOW,),
            in_specs=[pl.BlockSpec((1, SC_WINDOW), lambda i: (0, i))],
            out_specs=[pl.BlockSpec((SC_WINDOW, d), lambda i: (i, 0))],
            core_axis_name=("c", "s"), dimension_semantics=(pltpu.PARALLEL,),
        )(idx_hbm, o_hbm)

    return gather_kernel(y, idx)


def kernel(x, attn_norm, w_in, w_out, diff_lambda, diff_subln, mla_q_norm, mla_w_uq, mla_kv_norm, mla_w_ukv,
           swa_sinks, ffn_norm, ffn_w1, ffn_w3, ffn_w2, moe_router, moe_w1, moe_w3, moe_w2, final_norm):
    bsz, seq, _ = x.shape
    t = bsz * seq
    depth = w_in.shape[0]
    slopes_a, slopes_b, slopes_d = _alibi_slopes()
    tq_dense = _pick(seq, 512)
    tq_band = _pick(seq, 256)

    tabs = _rope_tables(seq)
    colscale = _col_scale()
    di = jnp.arange(tq_dense)
    causal_mask_t = jnp.where(di[:, None] <= di[None, :], 0.0, NEG_INF).astype(F32)
    tq_diff = tq_dense
    alibi_kb = _alibi_key_columns(slopes_a * LOG2E, tq_diff)
    tq_dil = _pick(seq, 512)
    n_off_dil = min((max(w for w, _ in DIL_PATTERNS) + tq_dil - 1) // tq_dil + 1, seq // tq_dil)
    n_off_swa = min((SWA_WINDOW - 1 + tq_band - 1) // tq_band + 1, seq // tq_band)
    dil_bias = _band_bias(slopes_b, tq_dil, n_off_dil, _dil_mult).reshape(2, 2, n_off_dil, tq_dil, tq_dil)
    swa_bias = _band_bias(slopes_d[np.array([0, 2, 1, 3])], tq_band, n_off_swa, _swa_mult
                          ).reshape(2, 2, n_off_swa, tq_band, tq_band)
    slopes_a_l2 = jnp.asarray(slopes_a * LOG2E, F32)
    dil_m0 = jnp.full((4,), NEG_INF, F32)

    x2d = x.reshape(t, D_MODEL)
    moe_bf16 = []
    for l in range(depth):
        w_main, wq_big, wkv, w4 = _prep_attn_weights(w_in[l], w_out[l], mla_w_uq[l], mla_w_ukv[l])
        a, avt, b, bvt, d, dvt, cq, ck, cvt = _proj(x2d, attn_norm[l][None], w_main, colscale, mla_q_norm[l][None],
                                                    wq_big, mla_kv_norm[l][None], wkv, tabs, seq)
        a3, b3, d3 = (v.reshape(bsz, seq, -1) for v in (a, b, d))
        sub128 = jnp.concatenate([diff_subln[l], diff_subln[l]])[None]
        o_a = _diff_attention(a3, avt, slopes_a_l2, causal_mask_t[:tq_diff, :tq_diff], alibi_kb, diff_lambda[l],
                              sub128, l, tq_diff)
        o_b = _band_attention(b3, bvt, 1, 2, dil_bias, dil_m0, 0.0, tq_dil, "dil_attn")
        o_c = _mla_attention(cq.reshape(bsz, seq, -1), ck.reshape(bsz, seq, -1), cvt, causal_mask_t, tq_dense)
        swa_m0 = (swa_sinks[l].astype(F32) * LOG2E)[np.array([0, 2, 1, 3])]
        o_d = _band_attention(d3, dvt, 2, 1, swa_bias, swa_m0, 1.0, tq_band, "swa_attn")
        outs = [v.reshape(t, 256) for v in (o_a, o_b, o_c, o_d)]
        j = l // 2
        if l % 2 == 0:
            x1, h2 = _outproj(*outs, w4, x2d, ffn_norm[l][None])
            nxt = [w[j].reshape(-1, w.shape[-1]) for w in (moe_w1, moe_w3, moe_w2)] if l + 1 < depth else []
            if any(a.shape[0] % (16 * (t // _pick(t, 512))) for a in nxt):
                nxt = []
            x2d, cast = _ffn(h2, ffn_w1[j].astype(BF16), ffn_w3[j].astype(BF16), ffn_w2[j].astype(BF16), x1, nxt)
            moe_bf16 = [c.reshape(moe_w.shape[1:]) for c, moe_w in zip(cast, (moe_w1, moe_w3, moe_w2))]
            moe = None
        else:
            r = jnp.pad(moe_router[j], ((0, 0), (0, LANES - N_EXPERTS)))
            r_hi, r_lo = _split_bf16(r, 2)
            x1, h2, logits = _outproj(*outs, w4, x2d, ffn_norm[l][None], (r_hi, r_lo))
            bm = _pick(t, 512)
            if not moe_bf16:
                moe_bf16 = [w[j].astype(BF16) for w in (moe_w1, moe_w3, moe_w2)]
            moe = _moe_layer(h2, logits, *moe_bf16, bm)
            x2d = x1
        if l < depth - 1 and moe is not None:
            raise NotImplementedError("MoE combine is fused into the final norm; MoE layer must be last")
    return _final(x2d, final_norm[None], moe).reshape(bsz, seq, D_MODEL)
```

```python
import functools
import math

import numpy as np
import jax
import jax.numpy as jnp
from jax import lax
from jax.experimental import pallas as pl
from jax.experimental.pallas import tpu as pltpu
from jax.experimental.pallas import tpu_sc as plsc

D_MODEL = 1024
DIFF_HEADS, DIFF_HD = 4, 32
DIL_HEADS, DIL_HD = 4, 64
DIL_PATTERNS = ((128, 1), (512, 4), (2048, 16))
MLA_HEADS, MLA_Q_LORA, MLA_KV_LORA, MLA_NOPE, MLA_ROPE, MLA_V = 4, 384, 128, 64, 32, 64
ROPE_THETA = 10000.0
SWA_HEADS, SWA_KV_HEADS, SWA_HD, SWA_WINDOW = 4, 2, 64, 128
D_FF = 2816
N_EXPERTS, TOP_K, D_FF_EXPERT = 8, 2, 3584
RMS_EPS = 1e-6
NEG_INF = -1e30
N_ALIBI = DIFF_HEADS + DIL_HEADS + SWA_HEADS

LOG2E = 1.4426950408889634
LANES = 128
HEAD_LANES = 64
VMEM_LIMIT = 56 * 1024 * 1024

F32 = jnp.float32
BF16 = jnp.bfloat16

_NT = (((1,), (1,)), ((), ()))


def _params(sem, vmem=VMEM_LIMIT):
    return pltpu.CompilerParams(dimension_semantics=sem, vmem_limit_bytes=vmem)


def _alibi_slopes():
    s = 2.0 ** (-8.0 * (np.arange(N_ALIBI) + 1) / N_ALIBI)
    return s[0::3], s[1::3], s[2::3]


def _pick(n, pref):
    t = min(pref, n)
    while n % t:
        t //= 2
    return t


N_SCALED = 2048
C_CQ, C_CKV, C_KPE, C_KPE_SW, N_MAIN = 2048, 2432, 2560, 2688, 2816


ONES_ROWS = 16
VT_ROWS = HEAD_LANES + ONES_ROWS


def _vt_with_ones(v):
    vt = v.T
    ones = jnp.ones((ONES_ROWS, v.shape[0]), F32)
    parts = []
    for hd in range(v.shape[1] // HEAD_LANES):
        parts += [vt[hd * HEAD_LANES:(hd + 1) * HEAD_LANES], ones]
    return jnp.concatenate(parts, axis=0).astype(BF16)


def _proj_body(x_ref, g_ref, w_ref, cs_ref, qn_ref, wq_ref, kn_ref, wkv_ref,
               cosq_ref, sinq_ref, cosk_ref, sink_ref,
               a_ref, avt_ref, b_ref, bvt_ref, d_ref, dvt_ref, cq_ref, ck_ref, cvt_ref):
    x = x_ref[...]
    ms = jnp.mean(x * x, axis=-1, keepdims=True)
    h = (x * lax.rsqrt(ms + RMS_EPS) * g_ref[...]).astype(BF16)
    acc = jnp.dot(h, w_ref[...], preferred_element_type=F32)
    sc = acc[:, :N_SCALED] * cs_ref[...]
    a_ref[...] = sc[:, 0:512].astype(BF16)
    avt_ref[...] = _vt_with_ones(sc[:, 512:768])
    b_ref[...] = sc[:, 768:1280].astype(BF16)
    bvt_ref[...] = _vt_with_ones(sc[:, 1280:1536])
    d_ref[...] = sc[:, 1536:1920].astype(BF16)
    dvt_ref[...] = _vt_with_ones(sc[:, 1920:2048])

    cq = acc[:, C_CQ:C_CKV]
    hq = (cq * lax.rsqrt(jnp.mean(cq * cq, axis=-1, keepdims=True) + RMS_EPS) * qn_ref[...]).astype(BF16)
    yq = jnp.dot(hq, wq_ref[...], preferred_element_type=F32)
    c_mla = (MLA_NOPE + MLA_ROPE) ** -0.5 * LOG2E
    rot = yq[:, 256:384] * cosq_ref[...] + yq[:, 384:512] * sinq_ref[...]
    for hd in range(MLA_HEADS):
        pair_nope = yq[:, (hd // 2) * LANES:(hd // 2 + 1) * LANES] * c_mla
        cq_ref[:, hd * 256:hd * 256 + LANES] = jnp.where(
            _lane_mask(rot.shape, hd * MLA_ROPE, (hd + 1) * MLA_ROPE), rot, 0.0).astype(BF16)
        cq_ref[:, hd * 256 + LANES:(hd + 1) * 256] = jnp.where(
            _lane_mask(rot.shape, (hd % 2) * HEAD_LANES, (hd % 2 + 1) * HEAD_LANES), pair_nope, 0.0).astype(BF16)

    ckv = acc[:, C_CKV:C_KPE]
    hk = (ckv * lax.rsqrt(jnp.mean(ckv * ckv, axis=-1, keepdims=True) + RMS_EPS) * kn_ref[...]).astype(BF16)
    kv = jnp.dot(hk, wkv_ref[...], preferred_element_type=F32)
    kr = (acc[:, C_KPE:C_KPE_SW] * cosk_ref[...] + acc[:, C_KPE_SW:N_MAIN] * sink_ref[...]).astype(BF16)
    for grp in range(2):
        ck_ref[:, grp * 256:grp * 256 + LANES] = kr
        ck_ref[:, grp * 256 + LANES:(grp + 1) * 256] = kv[:, grp * LANES:(grp + 1) * LANES].astype(BF16)
    cvt_ref[...] = _vt_with_ones(kv[:, 256:512])


def _proj(x2d, g, w_main, colscale, qn, wq_big, kn, wkv, tabs, seq):
    t = x2d.shape[0]
    tm = _pick(seq, 512)
    nsb = seq // tm
    full = lambda shape: pl.BlockSpec(shape, lambda i: (0,) * len(shape))
    tab = pl.BlockSpec((tm, LANES), lambda i: (i % nsb, 0))
    row = lambda w: pl.BlockSpec((tm, w), lambda i: (i, 0))
    vt = lambda w: pl.BlockSpec((None, w, tm), lambda i: (i // nsb, 0, i % nsb))
    row_sds = lambda w: jax.ShapeDtypeStruct((t, w), BF16)
    vt_sds = lambda w: jax.ShapeDtypeStruct((t // seq, w, seq), BF16)
    return pl.pallas_call(
        _proj_body,
        grid=(t // tm,),
        in_specs=[row(D_MODEL), full((1, D_MODEL)), full((D_MODEL, N_MAIN)), full((1, N_SCALED)),
                  full((1, MLA_Q_LORA)), full((MLA_Q_LORA, 512)), full((1, MLA_KV_LORA)),
                  full((MLA_KV_LORA, 512)), tab, tab, tab, tab],
        out_specs=[row(512), vt(4 * VT_ROWS), row(512), vt(4 * VT_ROWS), row(384), vt(2 * VT_ROWS),
                   row(1024), row(512), vt(4 * VT_ROWS)],
        out_shape=[row_sds(512), vt_sds(4 * VT_ROWS), row_sds(512), vt_sds(4 * VT_ROWS), row_sds(384),
                   vt_sds(2 * VT_ROWS), row_sds(1024), row_sds(512), vt_sds(4 * VT_ROWS)],
        compiler_params=_params(("parallel",)),
        name="proj",
    )(x2d, g, w_main, colscale, qn, wq_big, kn, wkv, *tabs)


def _attn_pipeline(n_soft, n_steps, scores, shift, load_vt, m0, s_ref, acc_ref):
    def stage_a(t, ss, dst_ref, m_cur):
        m_next, alpha = [], []
        for i in range(n_soft):
            m_new = jnp.maximum(m_cur[i], jnp.max(ss[i], axis=0, keepdims=True) + shift(i, t))
            dst_ref[i] = ss[i]
            m_next.append(m_new)
            alpha.append(jnp.exp2(m_cur[i] - m_new))
        return tuple(m_next), tuple(alpha)

    def stage_b(t, src_ref, carry):
        m_cur, alpha = carry
        for i in range(n_soft):
            p = jnp.exp2(src_ref[i] - (m_cur[i] - shift(i, t)))
            acc_ref[i] = alpha[i] * acc_ref[i] + jnp.dot(load_vt(t, i), p.astype(BF16),
                                                         preferred_element_type=F32)

    def half(t, src_ref, dst_ref, carry):
        ss = [scores(i, t + 1, False) for i in range(n_soft)]
        stage_b(t, src_ref, carry)
        return stage_a(t + 1, ss, dst_ref, carry[0])

    def body(u, carry):
        carry = half(2 * u, s_ref[0], s_ref[1], carry)
        return half(2 * u + 1, s_ref[1], s_ref[0], carry)

    carry = stage_a(0, [scores(i, 0, True) for i in range(n_soft)], s_ref[0], m0)
    carry = lax.fori_loop(0, n_steps // 2, body, carry)

    @pl.when(n_steps % 2 == 1)
    def _():
        stage_b(n_steps, s_ref[1], half(n_steps - 1, s_ref[0], s_ref[1], carry))

    @pl.when(n_steps % 2 == 0)
    def _():
        stage_b(n_steps, s_ref[0], carry)


def _row_mask(shape, lo, hi):
    row = lax.broadcasted_iota(jnp.int32, shape, 0)
    return jnp.logical_and(row >= lo, row < hi)


def _lane_mask(shape, lo, hi):
    lane = lax.broadcasted_iota(jnp.int32, shape, len(shape) - 1)
    return jnp.logical_and(lane >= lo, lane < hi)


def _normalized(acc_ref, i):
    return acc_ref[i, 0:HEAD_LANES, :] / acc_ref[i, HEAD_LANES:HEAD_LANES + 1, :]


def _diff_body(sl_ref, q_ref, k_ref, vt_ref, mask_ref, kb_ref, lam_ref, sub_ref, o_ref,
               qs_ref, s0_ref, s1_ref, acc_ref, *, tq, lam_init):
    qi = pl.program_id(1)
    tk = tq
    n_soft = 2 * DIFF_HEADS
    head_of = lambda i: i // 2
    for pair in range(2):
        q = q_ref[:, pair * LANES:(pair + 1) * LANES].astype(F32)
        for hd in range(2):
            sel = jnp.where(_lane_mask(q.shape, N_BIAS_SPLIT * hd, N_BIAS_SPLIT * (hd + 1)), 1.0, 0.0).astype(BF16)
            for mp in range(2):
                i = 4 * pair + 2 * hd + mp
                lo = hd * HEAD_LANES + mp * DIFF_HD
                qs_ref[i, :, 0:LANES] = jnp.where(_lane_mask(q.shape, lo, lo + DIFF_HD), q, 0.0).astype(BF16)
                qs_ref[i, :, LANES:2 * LANES] = sel
    acc_ref[...] = jnp.zeros(acc_ref.shape, F32)

    def scores(i, t, first):
        off = pl.multiple_of((qi - t) * tk, tk)
        pair = i // 4
        k = jnp.concatenate([k_ref[pl.ds(off, tk), pair * LANES:(pair + 1) * LANES], kb_ref[pair]], axis=1)
        s = lax.dot_general(k, qs_ref[i], _NT, preferred_element_type=F32)
        return s + mask_ref[...] if first else s

    def shift(i, t):
        return sl_ref[head_of(i)] * (-(t * tk)).astype(F32) if not isinstance(t, int) else 0.0

    def load_vt(t, i):
        return vt_ref[pl.ds(head_of(i) * VT_ROWS, VT_ROWS), pl.ds(pl.multiple_of((qi - t) * tk, tk), tk)]

    m0 = tuple(jnp.full((1, tq), NEG_INF, F32) for _ in range(n_soft))
    _attn_pipeline(n_soft, qi, scores, shift, load_vt, m0, (s0_ref, s1_ref), acc_ref)

    lp = lam_ref[...]
    lam = (jnp.exp(jnp.sum(lp[0:1, :] * lp[1:2, :], axis=-1, keepdims=True))
           - jnp.exp(jnp.sum(lp[2:3, :] * lp[3:4, :], axis=-1, keepdims=True)) + lam_init)
    in_h0 = _lane_mask((tq, LANES), 0, HEAD_LANES)
    for pair in range(2):
        o_h = [_normalized(acc_ref, 4 * pair + 2 * hd) - lam * _normalized(acc_ref, 4 * pair + 2 * hd + 1)
               for hd in range(2)]
        o = jnp.concatenate(o_h, axis=0).T
        sq = o * o
        ms0 = jnp.sum(jnp.where(in_h0, sq, 0.0), axis=-1, keepdims=True)
        ms1 = jnp.sum(jnp.where(in_h0, 0.0, sq), axis=-1, keepdims=True)
        ms = jnp.where(in_h0, ms0, ms1) * (1.0 / HEAD_LANES)
        y = o * lax.rsqrt(ms + RMS_EPS) * sub_ref[...]
        o_ref[:, pair * LANES:(pair + 1) * LANES] = (y * (1.0 - lam_init)).astype(o_ref.dtype)


N_BIAS_SPLIT = 3


def _alibi_key_columns(slopes_l2, tk):
    dj = jnp.arange(tk, dtype=F32)[None, :, None]
    val = jnp.asarray(slopes_l2, F32).reshape(-1, 1, 2) * dj
    cols = jnp.stack(_split_bf16(val, N_BIAS_SPLIT), axis=-1).reshape(val.shape[0], tk, 2 * N_BIAS_SPLIT)
    return jnp.pad(cols, ((0, 0), (0, 0), (0, LANES - 2 * N_BIAS_SPLIT)))


def _split_bf16(x, n):
    pieces = []
    for _ in range(n):
        bits = lax.bitcast_convert_type(x, jnp.uint32) & jnp.uint32(0xFFFF0000)
        head = lax.bitcast_convert_type(bits, F32)
        pieces.append(head.astype(BF16))
        x = x - head
    return pieces


def _diff_attention(a3, avt, slopes_l2, mask_t, kb, lam_p, sub128, layer_idx, tq):
    b, s, _ = a3.shape
    nq = s // tq
    lam_init = 0.8 - 0.6 * math.exp(-0.3 * layer_idx)
    body = functools.partial(_diff_body, tq=tq, lam_init=lam_init)
    grid_spec = pltpu.PrefetchScalarGridSpec(
        num_scalar_prefetch=1,
        grid=(b, nq),
        in_specs=[
            pl.BlockSpec((None, tq, 2 * LANES), lambda bi, qi, sl: (bi, qi, 0)),
            pl.BlockSpec((None, s, 2 * LANES), lambda bi, qi, sl: (bi, 0, 1)),
            pl.BlockSpec((None, DIFF_HEADS * VT_ROWS, s), lambda bi, qi, sl: (bi, 0, 0)),
            pl.BlockSpec((tq, tq), lambda bi, qi, sl: (0, 0), pipeline_mode=pl.Buffered(1)),
            pl.BlockSpec((2, tq, LANES), lambda bi, qi, sl: (0, 0, 0), pipeline_mode=pl.Buffered(1)),
            pl.BlockSpec((4, DIFF_HD), lambda bi, qi, sl: (0, 0)),
            pl.BlockSpec((1, LANES), lambda bi, qi, sl: (0, 0)),
        ],
        out_specs=pl.BlockSpec((None, tq, 2 * LANES), lambda bi, qi, sl: (bi, qi, 0)),
        scratch_shapes=[pltpu.VMEM((2 * DIFF_HEADS, tq, 2 * LANES), BF16),
                        pltpu.VMEM((2 * DIFF_HEADS, tq, tq), F32), pltpu.VMEM((2 * DIFF_HEADS, tq, tq), F32),
                        pltpu.VMEM((2 * DIFF_HEADS, VT_ROWS, tq), F32)],
    )
    return pl.pallas_call(
        body, grid_spec=grid_spec,
        out_shape=jax.ShapeDtypeStruct((b, s, 256), BF16),
        compiler_params=_params(("parallel", "arbitrary")),
        name="diff_attn",
    )(slopes_l2, a3, a3, avt, mask_t, kb, lam_p, sub128)


def _mla_body(q_ref, k_ref, vt_ref, mask_ref, o_ref, s0_ref, s1_ref, acc_ref, *, tq):
    qi = pl.program_id(1)
    tk = tq
    acc_ref[...] = jnp.zeros(acc_ref.shape, F32)

    def scores(i, t, first):
        k = k_ref[pl.ds(pl.multiple_of((qi - t) * tk, tk), tk), (i // 2) * 256:(i // 2 + 1) * 256]
        s = lax.dot_general(k, q_ref[:, i * 256:(i + 1) * 256], _NT, preferred_element_type=F32)
        return s + mask_ref[...] if first else s

    def load_vt(t, i):
        return vt_ref[pl.ds(i * VT_ROWS, VT_ROWS), pl.ds(pl.multiple_of((qi - t) * tk, tk), tk)]

    m0 = tuple(jnp.full((1, tq), NEG_INF, F32) for _ in range(MLA_HEADS))
    _attn_pipeline(MLA_HEADS, qi, scores, lambda i, t: 0.0, load_vt, m0, (s0_ref, s1_ref), acc_ref)
    o_t = jnp.concatenate([_normalized(acc_ref, i) for i in range(MLA_HEADS)], axis=0)
    o_ref[...] = o_t.T.astype(o_ref.dtype)


def _mla_attention(cq3, ck3, cvt, mask_t, tq):
    b, s, _ = cq3.shape
    nq = s // tq
    return pl.pallas_call(
        functools.partial(_mla_body, tq=tq),
        grid=(b, nq),
        in_specs=[
            pl.BlockSpec((None, tq, MLA_HEADS * 256), lambda bi, qi: (bi, qi, 0)),
            pl.BlockSpec((None, s, 512), lambda bi, qi: (bi, 0, 0)),
            pl.BlockSpec((None, MLA_HEADS * VT_ROWS, s), lambda bi, qi: (bi, 0, 0)),
            pl.BlockSpec((tq, tq), lambda bi, qi: (0, 0)),
        ],
        out_specs=pl.BlockSpec((None, tq, 256), lambda bi, qi: (bi, qi, 0)),
        out_shape=jax.ShapeDtypeStruct((b, s, 256), BF16),
        scratch_shapes=[pltpu.VMEM((MLA_HEADS, tq, tq), F32), pltpu.VMEM((MLA_HEADS, tq, tq), F32),
                        pltpu.VMEM((MLA_HEADS, VT_ROWS, tq), F32)],
        compiler_params=_params(("parallel", "arbitrary")),
        name="mla_attn",
    )(cq3, ck3, cvt, mask_t)


def _band_body(m0_ref, q_ref, k_ref, vt_ref, bias_ref, o_ref, qs_ref, s0_ref, s1_ref, acc_ref,
               *, tq, n_off, l_init, kv_slabs):
    qi = pl.program_id(1)
    tk = tq
    n_soft = 4
    for sl in range(2):
        q = q_ref[:, sl * LANES:(sl + 1) * LANES].astype(F32)
        for g in range(2):
            qs_ref[2 * sl + g] = jnp.where(_lane_mask(q.shape, g * HEAD_LANES, (g + 1) * HEAD_LANES), q, 0.0
                                           ).astype(BF16)
    for i in range(n_soft):
        acc_ref[i] = jnp.where(_row_mask((VT_ROWS, tq), 0, HEAD_LANES), 0.0, l_init).astype(F32)

    kv_of = lambda i: (i // 2) * (kv_slabs - 1)

    def scores(i, t, first):
        k = k_ref[pl.ds(pl.multiple_of((qi - t) * tk, tk), tk), kv_of(i) * LANES:(kv_of(i) + 1) * LANES]
        return lax.dot_general(k, qs_ref[i], _NT, preferred_element_type=F32) + bias_ref[i // 2, i % 2, t]

    def load_vt(t, i):
        return vt_ref[pl.ds((2 * kv_of(i) + i % 2) * VT_ROWS, VT_ROWS),
                      pl.ds(pl.multiple_of((qi - t) * tk, tk), tk)]

    m0 = tuple(jnp.full((1, tq), m0_ref[i], F32) for i in range(n_soft))
    _attn_pipeline(n_soft, jnp.minimum(qi, n_off - 1), scores, lambda i, t: 0.0, load_vt, m0,
                   (s0_ref, s1_ref), acc_ref)
    o_t = jnp.concatenate([_normalized(acc_ref, i) for i in range(n_soft)], axis=0)
    o_ref[...] = o_t.T.astype(o_ref.dtype)


def _band_attention(x3, xvt, k_blk, kv_slabs, bias, m0, l_init, tq, name):
    b, s, _ = x3.shape
    nq = s // tq
    n_off = bias.shape[2]
    n_soft = 4
    grid_spec = pltpu.PrefetchScalarGridSpec(
        num_scalar_prefetch=1,
        grid=(b, nq),
        in_specs=[
            pl.BlockSpec((None, tq, 2 * LANES), lambda bi, qi, m0r: (bi, qi, 0)),
            pl.BlockSpec((None, s, kv_slabs * LANES), lambda bi, qi, m0r: (bi, 0, k_blk)),
            pl.BlockSpec((None, kv_slabs * 2 * VT_ROWS, s), lambda bi, qi, m0r: (bi, 0, 0)),
            pl.BlockSpec((2, 2, n_off, tq, tq), lambda bi, qi, m0r: (0, 0, 0, 0, 0),
                         pipeline_mode=pl.Buffered(1)),
        ],
        out_specs=pl.BlockSpec((None, tq, 2 * LANES), lambda bi, qi, m0r: (bi, qi, 0)),
        scratch_shapes=[pltpu.VMEM((n_soft, tq, LANES), BF16), pltpu.VMEM((n_soft, tq, tq), F32),
                        pltpu.VMEM((n_soft, tq, tq), F32), pltpu.VMEM((n_soft, VT_ROWS, tq), F32)],
    )
    return pl.pallas_call(
        functools.partial(_band_body, tq=tq, n_off=n_off, l_init=l_init, kv_slabs=kv_slabs),
        grid_spec=grid_spec,
        out_shape=jax.ShapeDtypeStruct((b, s, 256), BF16),
        compiler_params=_params(("parallel", "arbitrary")),
        name=name,
    )(m0, x3, x3, xvt, bias)


def _band_bias(slopes, tq, n_off, mult_fn):
    di = jnp.arange(tq, dtype=jnp.int32)[None, :]
    dj = jnp.arange(tq, dtype=jnp.int32)[:, None]
    off = jnp.arange(n_off, dtype=jnp.int32)[:, None, None]
    delta = off * tq + di - dj
    mult = mult_fn(delta)
    sl = jnp.asarray(slopes, F32)[:, None, None, None] * LOG2E
    val = -sl * delta.astype(F32)[None] + jnp.log2(jnp.maximum(mult, 1).astype(F32))[None]
    return jnp.where((mult > 0)[None], val, NEG_INF)


def _dil_mult(delta):
    m = jnp.zeros(delta.shape, jnp.int32)
    for window, d in DIL_PATTERNS:
        m = m + ((delta >= 0) & (delta <= window) & (delta % d == 0)).astype(jnp.int32)
    return m


def _swa_mult(delta):
    return ((delta >= 0) & (delta <= SWA_WINDOW - 1)).astype(jnp.int32)


def _outproj_body(*refs, with_router):
    if with_router:
        (oa, ob, oc, od, w_ref, x_ref, g_ref, rh_ref, rl_ref, x1_ref, h2_ref, lg_ref) = refs
    else:
        (oa, ob, oc, od, w_ref, x_ref, g_ref, x1_ref, h2_ref) = refs
    acc = x_ref[...]
    for i, o in enumerate((oa, ob, oc, od)):
        acc = acc + jnp.dot(o[...], w_ref[i], preferred_element_type=F32)
    x1_ref[...] = acc
    h2 = acc * lax.rsqrt(jnp.mean(acc * acc, axis=-1, keepdims=True) + RMS_EPS) * g_ref[...]
    hi = h2.astype(BF16)
    if with_router:
        for c, words in enumerate(_pack_rows(h2)):
            h2_ref[c] = words
        lo = (h2 - hi.astype(F32)).astype(BF16)
        hh = jnp.dot(hi, jnp.concatenate([rh_ref[...], rl_ref[...]], axis=1), preferred_element_type=F32)
        lg_ref[...] = hh[:, :LANES] + (hh[:, LANES:] + jnp.dot(lo, rh_ref[...], preferred_element_type=F32))
    else:
        h2_ref[...] = hi


def _outproj(oa, ob, oc, od, w4, x2d, g, router_hl=None):
    t = x2d.shape[0]
    tm = _pick(t, 512)
    with_router = router_hl is not None
    full = lambda shape: pl.BlockSpec(shape, lambda i: (0,) * len(shape))
    row = lambda w: pl.BlockSpec((tm, w), lambda i: (i, 0))
    in_specs = [row(256)] * 4 + [full((4, 256, D_MODEL)), row(D_MODEL), full((1, D_MODEL))]
    args = [oa, ob, oc, od, w4, x2d, g]
    out_specs = [row(D_MODEL), row(D_MODEL)]
    out_shape = [jax.ShapeDtypeStruct((t, D_MODEL), F32), jax.ShapeDtypeStruct((t, D_MODEL), BF16)]
    if with_router:
        out_specs[1] = pl.BlockSpec((SC_CHUNKS, tm, SC_COLS), lambda i: (0, i, 0))
        out_shape[1] = jax.ShapeDtypeStruct((SC_CHUNKS, t, SC_COLS), U32)
        in_specs += [full((D_MODEL, LANES))] * 2
        args += list(router_hl)
        out_specs.append(row(LANES))
        out_shape.append(jax.ShapeDtypeStruct((t, LANES), F32))
    return pl.pallas_call(
        functools.partial(_outproj_body, with_router=with_router),
        grid=(t // tm,), in_specs=in_specs, out_specs=out_specs, out_shape=out_shape,
        compiler_params=_params(("parallel",)),
        name="outproj",
    )(*args)


FF_CHUNK = 512


def _swiglu(x, w1_ref, w3_ref, w2_ref):
    tf = w1_ref.shape[1]
    total = None
    for lo in range(0, tf, FF_CHUNK):
        hi = min(lo + FF_CHUNK, tf)
        a = jnp.dot(x, w1_ref[:, lo:hi], preferred_element_type=F32)
        b = jnp.dot(x, w3_ref[:, lo:hi], preferred_element_type=F32)
        mid = (a * jax.nn.sigmoid(a) * b).astype(BF16)
        part = jnp.dot(mid, w2_ref[lo:hi, :], preferred_element_type=F32)
        total = part if total is None else total + part
    return total


def _ffn_body(*refs, n_cast):
    h_ref, w1_ref, w3_ref, w2_ref, x_ref = refs[:5]
    cast_in, o_ref, cast_out = refs[5:5 + n_cast], refs[5 + n_cast], refs[6 + n_cast:]
    o_ref[...] = x_ref[...] + _swiglu(h_ref[...], w1_ref, w3_ref, w2_ref)
    for src, dst in zip(cast_in, cast_out):
        dst[...] = src[...].astype(BF16)


def _ffn(h2, w1, w3, w2, x1, to_cast=()):
    t = h2.shape[0]
    tm = _pick(t, 512)
    n_steps = t // tm
    resident = lambda shape: pl.BlockSpec(shape, lambda i: (0, 0), pipeline_mode=pl.Buffered(1))
    sliced = [pl.BlockSpec((a.shape[0] // n_steps, a.shape[1]), lambda i: (i, 0)) for a in to_cast]
    outs = pl.pallas_call(
        functools.partial(_ffn_body, n_cast=len(to_cast)),
        grid=(n_steps,),
        in_specs=[pl.BlockSpec((tm, D_MODEL), lambda i: (i, 0)),
                  resident((D_MODEL, D_FF)), resident((D_MODEL, D_FF)), resident((D_FF, D_MODEL)),
                  pl.BlockSpec((tm, D_MODEL), lambda i: (i, 0))] + sliced,
        out_specs=[pl.BlockSpec((tm, D_MODEL), lambda i: (i, 0))] + sliced,
        out_shape=[jax.ShapeDtypeStruct((t, D_MODEL), F32)]
        + [jax.ShapeDtypeStruct(a.shape, BF16) for a in to_cast],
        compiler_params=_params(("parallel",)),
        name="ffn",
    )(h2, w1, w3, w2, x1, *to_cast)
    return outs[0], outs[1:]


def _route_body(lg_ref, tri_ref, info_ref, cnt_ref, carry_ref):
    i = pl.program_id(0)

    @pl.when(i == 0)
    def _():
        carry_ref[...] = jnp.zeros(carry_ref.shape, F32)

    lg = lg_ref[...]
    lane = lax.broadcasted_iota(jnp.int32, lg.shape, 1)
    lg = jnp.where(lane < N_EXPERTS, lg, -jnp.inf)
    m1 = jnp.max(lg, axis=-1, keepdims=True)
    i1 = jnp.min(jnp.where(lg == m1, lane, LANES), axis=-1, keepdims=True)
    oh1 = lane == i1
    lg2 = jnp.where(oh1, -jnp.inf, lg)
    m2 = jnp.max(lg2, axis=-1, keepdims=True)
    i2 = jnp.min(jnp.where(lg2 == m2, lane, LANES), axis=-1, keepdims=True)
    oh2 = lane == i2
    e2 = jnp.exp(m2 - m1)
    g1 = 1.0 / (1.0 + e2)
    g2 = e2 / (1.0 + e2)
    oh = jnp.where(jnp.logical_or(oh1, oh2), 1.0, 0.0)
    rank = jnp.dot(tri_ref[...], oh.astype(BF16), preferred_element_type=F32) + carry_ref[...]
    r1 = jnp.sum(jnp.where(oh1, rank, 0.0), axis=-1, keepdims=True)
    r2 = jnp.sum(jnp.where(oh2, rank, 0.0), axis=-1, keepdims=True)
    carry_ref[...] += jnp.sum(oh, axis=0, keepdims=True)
    cnt_ref[...] = carry_ref[...]
    info = jnp.zeros(lg.shape, F32)
    for c, val in enumerate((i1.astype(F32), i2.astype(F32), r1, r2, g1, g2)):
        info = jnp.where(lane == c, val, info)
    info_ref[...] = info


def _route(logits):
    t = logits.shape[0]
    tm = _pick(t, 512)
    tri = (jnp.arange(tm)[:, None] > jnp.arange(tm)[None, :]).astype(BF16)
    return pl.pallas_call(
        _route_body,
        grid=(t // tm,),
        in_specs=[pl.BlockSpec((tm, LANES), lambda i: (i, 0)), pl.BlockSpec((tm, tm), lambda i: (0, 0))],
        out_specs=[pl.BlockSpec((tm, LANES), lambda i: (i, 0)), pl.BlockSpec((1, LANES), lambda i: (0, 0))],
        out_shape=[jax.ShapeDtypeStruct((t, LANES), F32), jax.ShapeDtypeStruct((1, LANES), F32)],
        scratch_shapes=[pltpu.VMEM((1, LANES), F32)],
        compiler_params=_params(("arbitrary",)),
        name="route",
    )(logits, tri)


def _moe_ffn_body(be_ref, bv_ref, x_ref, w1_ref, w3_ref, w2_ref, o_ref):
    valid = bv_ref[pl.program_id(0)]

    @pl.when(valid > 0)
    def _():
        x = _unpack_rows([x_ref[c] for c in range(SC_CHUNKS)])
        x = jnp.where(lax.broadcasted_iota(jnp.int32, x.shape, 0) < valid, x, 0.0).astype(BF16)
        for c, words in enumerate(_pack_rows(_swiglu(x, w1_ref, w3_ref, w2_ref))):
            o_ref[c] = words

    @pl.when(valid == 0)
    def _():
        o_ref[...] = jnp.zeros(o_ref.shape, o_ref.dtype)


def _moe_ffn(xs, blk_expert, blk_valid, w1, w3, w2, bm):
    rows = xs.shape[1]
    chunked = pl.BlockSpec((SC_CHUNKS, bm, SC_COLS), lambda i, be, bv: (0, i, 0))
    expert = lambda shape: pl.BlockSpec((None,) + shape, lambda i, be, bv: (be[i], 0, 0),
                                        pipeline_mode=pl.Buffered(1))
    grid_spec = pltpu.PrefetchScalarGridSpec(
        num_scalar_prefetch=2,
        grid=(rows // bm,),
        in_specs=[chunked, expert((D_MODEL, D_FF_EXPERT)), expert((D_MODEL, D_FF_EXPERT)),
                  expert((D_FF_EXPERT, D_MODEL))],
        out_specs=chunked,
    )
    return pl.pallas_call(
        _moe_ffn_body, grid_spec=grid_spec,
        out_shape=jax.ShapeDtypeStruct(xs.shape, U32),
        compiler_params=_params(("arbitrary",)),
        name="moe_ffn",
    )(blk_expert, blk_valid, xs, w1, w3, w2)


def _final_body(*refs, with_moe):
    if with_moe:
        x_ref, y0_ref, y1_ref, info_ref, g_ref, o_ref = refs
        info = info_ref[...]
        y0 = _unpack_rows([y0_ref[c] for c in range(SC_CHUNKS)])
        y1 = _unpack_rows([y1_ref[c] for c in range(SC_CHUNKS)])
        x = x_ref[...] + (info[:, 4:5] * y0 + info[:, 5:6] * y1)
    else:
        x_ref, g_ref, o_ref = refs
        x = x_ref[...]
    o_ref[...] = x * lax.rsqrt(jnp.mean(x * x, axis=-1, keepdims=True) + RMS_EPS) * g_ref[...]


def _final(x2d, g, moe=None):
    t = x2d.shape[0]
    tm = _pick(t, 512)
    row = lambda w: pl.BlockSpec((tm, w), lambda i: (i, 0))
    in_specs, args = [row(D_MODEL)], [x2d]
    if moe is not None:
        yg, info = moe
        slot = lambda k: pl.BlockSpec((None, SC_CHUNKS, tm, SC_COLS), lambda i: (k, 0, i, 0))
        in_specs += [slot(0), slot(1), row(LANES)]
        args += [yg, yg, info]
    in_specs.append(pl.BlockSpec((1, D_MODEL), lambda i: (0, 0)))
    args.append(g)
    return pl.pallas_call(
        functools.partial(_final_body, with_moe=moe is not None),
        grid=(t // tm,), in_specs=in_specs, out_specs=row(D_MODEL),
        out_shape=jax.ShapeDtypeStruct((t, D_MODEL), F32),
        compiler_params=_params(("parallel",)),
        name="final",
    )(*args)


def _prep_attn_weights(w_in, w_out, w_uq, w_ukv):
    widths = (256, 256, 256, 256, 256, 256, MLA_Q_LORA, MLA_KV_LORA, MLA_ROPE, 256, 128, 128)
    offs = np.concatenate([[0], np.cumsum(widths)])
    qa, ka, va, qb, kb, vb, cq, ckv, kpe, qd, kd, vd = [w_in[:, offs[i]:offs[i + 1]] for i in range(12)]
    hperm = np.array([0, 2, 1, 3])
    qd = qd.reshape(D_MODEL, 4, SWA_HD)[:, hperm].reshape(D_MODEL, 256)
    half = MLA_ROPE // 2
    kpe_sw = jnp.concatenate([kpe[:, half:], kpe[:, :half]], axis=1)
    w_main = jnp.concatenate([qa, ka, va, qb, kb, vb, qd, kd, vd, cq, ckv] + [kpe] * MLA_HEADS
                             + [kpe_sw] * MLA_HEADS, axis=1)

    uq = w_uq.reshape(MLA_Q_LORA, MLA_HEADS, MLA_NOPE + MLA_ROPE)
    nope = uq[:, :, :MLA_NOPE].reshape(MLA_Q_LORA, MLA_HEADS * MLA_NOPE)
    rope = uq[:, :, MLA_NOPE:]
    rope_sw = jnp.concatenate([rope[:, :, half:], rope[:, :, :half]], axis=2)
    wq_big = jnp.concatenate([nope, rope.reshape(MLA_Q_LORA, LANES), rope_sw.reshape(MLA_Q_LORA, LANES)], axis=1)

    ukv = w_ukv.reshape(MLA_KV_LORA, MLA_HEADS, MLA_NOPE + MLA_V)
    wkv = jnp.concatenate([ukv[:, :, :MLA_NOPE].reshape(MLA_KV_LORA, 256),
                           ukv[:, :, MLA_NOPE:].reshape(MLA_KV_LORA, 256)], axis=1)

    wo = w_out.reshape(4, 256, D_MODEL)
    wo_d = wo[3].reshape(4, SWA_HD, D_MODEL)[hperm].reshape(256, D_MODEL)
    w4 = jnp.stack([wo[0], wo[1], wo[2], wo_d])
    return w_main.astype(BF16), wq_big.astype(BF16), wkv.astype(BF16), w4.astype(BF16)


def _rope_tables(s):
    inv = ROPE_THETA ** (-jnp.arange(0, MLA_ROPE, 2, dtype=F32) / MLA_ROPE)
    ang = jnp.arange(s, dtype=F32)[:, None] * inv[None, :]
    cos, sin = jnp.cos(ang), jnp.sin(ang)
    cos128 = jnp.concatenate([cos, cos] * MLA_HEADS, axis=1)
    sin128 = jnp.concatenate([-sin, sin] * MLA_HEADS, axis=1)
    c_mla = (MLA_NOPE + MLA_ROPE) ** -0.5 * LOG2E
    return cos128 * c_mla, sin128 * c_mla, cos128, sin128


def _col_scale():
    ca = DIFF_HD ** -0.5 * LOG2E
    cb = DIL_HD ** -0.5 * LOG2E
    cd = SWA_HD ** -0.5 * LOG2E
    v = np.ones((1, N_SCALED), np.float32)
    v[0, 0:256] = ca
    v[0, 768:1024] = cb
    v[0, 1536:1792] = cd
    return jnp.asarray(v)


def _moe_layer(h2, logits, w1, w3, w2, bm):
    t = logits.shape[0]
    info, counts = _route(logits)
    cnt = counts[0, :N_EXPERTS].astype(jnp.int32)
    padded = (cnt + bm - 1) // bm * bm
    pends = jnp.cumsum(padded)
    pstarts = pends - padded
    e12 = info[:, 0:2].astype(jnp.int32)
    pos = pstarts[e12] + info[:, 2:4].astype(jnp.int32)
    nb = (t * TOP_K) // bm + N_EXPERTS
    rows = nb * bm
    blk_start = jnp.arange(nb, dtype=jnp.int32) * bm
    blk_expert = jnp.minimum(jnp.sum((pends[None, :] <= blk_start[:, None]).astype(jnp.int32), axis=1),
                             N_EXPERTS - 1)
    blk_valid = jnp.clip(pstarts[blk_expert] + cnt[blk_expert] - blk_start, 0, bm).astype(jnp.int32)
    sub = (jnp.arange(SC_CHUNKS, dtype=jnp.int32) * rows)[None, :, None] + pos.T[:, None, :]
    xs = _sc_scatter_rows(h2.reshape(SC_CHUNKS * t, SC_COLS),
                          [sub[k].reshape(1, SC_CHUNKS * t) for k in range(TOP_K)], SC_CHUNKS * rows)
    ys = _moe_ffn(xs.reshape(SC_CHUNKS, rows, SC_COLS), blk_expert.astype(jnp.int32), blk_valid, w1, w3, w2, bm)
    yg = _sc_gather_rows(ys.reshape(SC_CHUNKS * rows, SC_COLS), sub.reshape(1, TOP_K * SC_CHUNKS * t))
    return yg.reshape(TOP_K, SC_CHUNKS, t, SC_COLS), info


SC_WINDOW = 128
SC_COLS = 256
SC_CHUNKS = D_MODEL // (2 * SC_COLS)
U32 = jnp.uint32


def _bf16_bits(v):
    bits = lax.bitcast_convert_type(v, U32)
    return bits + jnp.uint32(0x7FFF) + ((bits >> 16) & jnp.uint32(1))


def _pack_rows(v):
    out = []
    for c in range(SC_CHUNKS):
        lo = _bf16_bits(v[:, 2 * c * SC_COLS:(2 * c + 1) * SC_COLS]) >> 16
        hi = _bf16_bits(v[:, (2 * c + 1) * SC_COLS:(2 * c + 2) * SC_COLS]) & jnp.uint32(0xFFFF0000)
        out.append(lo | hi)
    return out


def _unpack_rows(chunks):
    cols = []
    for u in chunks:
        cols.append(lax.bitcast_convert_type(u << 16, F32))
        cols.append(lax.bitcast_convert_type(u & jnp.uint32(0xFFFF0000), F32))
    return jnp.concatenate(cols, axis=1)


def _sc_mesh():
    return plsc.VectorSubcoreMesh(core_axis_name="c", subcore_axis_name="s")


def _sc_scatter_rows(x, idx_list, rows):
    t, d = x.shape
    n = len(idx_list)

    @functools.partial(pl.kernel, out_type=jax.ShapeDtypeStruct((rows, d), x.dtype), mesh=_sc_mesh(),
                       scratch_types=[])
    def scatter_kernel(x_hbm, *rest):
        idx_hbms, o_hbm = rest[:n], rest[n]

        def body(x_vmem, *idx_vmems):
            for iv in idx_vmems:
                pltpu.sync_copy(x_vmem, o_hbm.at[iv.at[0]])

        pltpu.emit_pipeline(
            body, grid=(t // SC_WINDOW,),
            in_specs=[pl.BlockSpec((SC_WINDOW, d), lambda i: (i, 0))]
            + [pl.BlockSpec((1, SC_WINDOW), lambda i: (0, i))] * n,
            out_specs=[], core_axis_name=("c", "s"), dimension_semantics=(pltpu.PARALLEL,),
        )(x_hbm, *idx_hbms)

    return scatter_kernel(x, *idx_list)


def _sc_gather_rows(y, idx):
    m = idx.shape[1]
    d = y.shape[1]

    @functools.partial(pl.kernel, out_type=jax.ShapeDtypeStruct((m, d), y.dtype), mesh=_sc_mesh(),
                       scratch_types=[])
    def gather_kernel(y_hbm, idx_hbm, o_hbm):
        def body(idx_vmem, o_vmem):
            pltpu.sync_copy(y_hbm.at[idx_vmem.at[0]], o_vmem)

        pltpu.emit_pipeline(
            body, grid=(m // SC_WINDOW,),
            in_specs=[pl.BlockSpec((1, SC_WINDOW), lambda i: (0, i))],
            out_specs=[pl.BlockSpec((SC_WINDOW, d), lambda i: (i, 0))],
            core_axis_name=("c", "s"), dimension_semantics=(pltpu.PARALLEL,),
        )(idx_hbm, o_hbm)

    return gather_kernel(y, idx)


def kernel(x, attn_norm, w_in, w_out, diff_lambda, diff_subln, mla_q_norm, mla_w_uq, mla_kv_norm, mla_w_ukv,
           swa_sinks, ffn_norm, ffn_w1, ffn_w3, ffn_w2, moe_router, moe_w1, moe_w3, moe_w2, final_norm):
    bsz, seq, _ = x.shape
    t = bsz * seq
    depth = w_in.shape[0]
    slopes_a, slopes_b, slopes_d = _alibi_slopes()
    tq_dense = _pick(seq, 512)
    tq_band = _pick(seq, 256)

    tabs = _rope_tables(seq)
    colscale = _col_scale()
    di = jnp.arange(tq_dense)
    causal_mask_t = jnp.where(di[:, None] <= di[None, :], 0.0, NEG_INF).astype(F32)
    tq_diff = tq_dense
    alibi_kb = _alibi_key_columns(slopes_a * LOG2E, tq_diff)
    tq_dil = _pick(seq, 512)
    n_off_dil = min((max(w for w, _ in DIL_PATTERNS) + tq_dil - 1) // tq_dil + 1, seq // tq_dil)
    n_off_swa = min((SWA_WINDOW - 1 + tq_band - 1) // tq_band + 1, seq // tq_band)
    dil_bias = _band_bias(slopes_b, tq_dil, n_off_dil, _dil_mult).reshape(2, 2, n_off_dil, tq_dil, tq_dil)
    swa_bias = _band_bias(slopes_d[np.array([0, 2, 1, 3])], tq_band, n_off_swa, _swa_mult
                          ).reshape(2, 2, n_off_swa, tq_band, tq_band)
    slopes_a_l2 = jnp.asarray(slopes_a * LOG2E, F32)
    dil_m0 = jnp.full((4,), NEG_INF, F32)

    x2d = x.reshape(t, D_MODEL)
    moe_bf16 = []
    for l in range(depth):
        w_main, wq_big, wkv, w4 = _prep_attn_weights(w_in[l], w_out[l], mla_w_uq[l], mla_w_ukv[l])
        a, avt, b, bvt, d, dvt, cq, ck, cvt = _proj(x2d, attn_norm[l][None], w_main, colscale, mla_q_norm[l][None],
                                                    wq_big, mla_kv_norm[l][None], wkv, tabs, seq)
        a3, b3, d3 = (v.reshape(bsz, seq, -1) for v in (a, b, d))
        sub128 = jnp.concatenate([diff_subln[l], diff_subln[l]])[None]
        o_a = _diff_attention(a3, avt, slopes_a_l2, causal_mask_t[:tq_diff, :tq_diff], alibi_kb, diff_lambda[l],
                              sub128, l, tq_diff)
        o_b = _band_attention(b3, bvt, 1, 2, dil_bias, dil_m0, 0.0, tq_dil, "dil_attn")
        o_c = _mla_attention(cq.reshape(bsz, seq, -1), ck.reshape(bsz, seq, -1), cvt, causal_mask_t, tq_dense)
        swa_m0 = (swa_sinks[l].astype(F32) * LOG2E)[np.array([0, 2, 1, 3])]
        o_d = _band_attention(d3, dvt, 2, 1, swa_bias, swa_m0, 1.0, tq_band, "swa_attn")
        outs = [v.reshape(t, 256) for v in (o_a, o_b, o_c, o_d)]
        j = l // 2
        if l % 2 == 0:
            x1, h2 = _outproj(*outs, w4, x2d, ffn_norm[l][None])
            nxt = [w[j].reshape(-1, w.shape[-1]) for w in (moe_w1, moe_w3, moe_w2)] if l + 1 < depth else []
            if any(a.shape[0] % (16 * (t // _pick(t, 512))) for a in nxt):
                nxt = []
            x2d, cast = _ffn(h2, ffn_w1[j].astype(BF16), ffn_w3[j].astype(BF16), ffn_w2[j].astype(BF16), x1, nxt)
            moe_bf16 = [c.reshape(moe_w.shape[1:]) for c, moe_w in zip(cast, (moe_w1, moe_w3, moe_w2))]
            moe = None
        else:
            r = jnp.pad(moe_router[j], ((0, 0), (0, LANES - N_EXPERTS)))
            r_hi, r_lo = _split_bf16(r, 2)
            x1, h2, logits = _outproj(*outs, w4, x2d, ffn_norm[l][None], (r_hi, r_lo))
            bm = _pick(t, 512)
            if not moe_bf16:
                moe_bf16 = [w[j].astype(BF16) for w in (moe_w1, moe_w3, moe_w2)]
            moe = _moe_layer(h2, logits, *moe_bf16, bm)
            x2d = x1
        if l < depth - 1 and moe is not None:
            raise NotImplementedError("MoE combine is fused into the final norm; MoE layer must be last")
    return _final(x2d, final_norm[None], moe).reshape(bsz, seq, D_MODEL)
```
